```python
import jax
import jax.numpy as jnp
from jax import lax
import numpy as np

D_MODEL = 1024
BATCH = 4
SEQ = 4096
DEPTH = 2
DEC_BATCH = 128
DEC_SEQ = 1
PAST_LEN = 2048
PAGE_SIZE = 128

HEAD_DIM = 64
D_MIX = D_MODEL
D_FOX = D_MIX // 4
D_NSA = D_MIX // 4
D_GLA = D_MIX // 2
H_FOX = D_FOX // HEAD_DIM
H_NSA = D_NSA // HEAD_DIM
GLA_DV = 128
H_GLA = D_GLA // GLA_DV
GLA_DK = HEAD_DIM
GLA_RANK = 16
GLA_TAU = 16.0
GLA_CHUNK = 64
Q_BLOCK = 128
CMP_LEN = 32
SEL_LEN = 64
TOP_N = 16
WINDOW = 512
ROPE_THETA = 500000.0
ROPE_DIM = HEAD_DIM // 4
D_FF = 3584
N_EXPERTS = 8
TOP_K = 2
MOE_BLOCK = 128
EPS = 1e-6
SEL_FORCE = 1e9
IN_SPLITS = (D_FOX, D_FOX, D_FOX, H_FOX,
             D_NSA, 6 * HEAD_DIM, 3 * H_NSA,
             H_GLA * GLA_DK, H_GLA * GLA_DK, D_GLA, GLA_RANK, D_GLA)

kernel_name = 'hybrid_fox_nsa_gla_decoder'

F32 = jnp.float32


def rmsnorm(x, g):
    xf = x.astype(F32)
    y = xf * lax.rsqrt(jnp.mean(xf * xf, axis=-1, keepdims=True) + EPS)
    return (y * g.astype(F32)).astype(x.dtype)


def partial_rope(x, pos):
    half = ROPE_DIM // 2
    inv = ROPE_THETA ** (-jnp.arange(half, dtype=F32) / half)
    ang = pos.astype(F32)[:, None] * inv[None, :]
    shape = (pos.shape[0],) + (1,) * (x.ndim - 3) + (half,)
    cos = jnp.cos(ang).reshape(shape).astype(x.dtype)
    sin = jnp.sin(ang).reshape(shape).astype(x.dtype)
    x1 = x[..., :half]
    x2 = x[..., half:ROPE_DIM]
    return jnp.concatenate([x1 * cos - x2 * sin, x2 * cos + x1 * sin, x[..., ROPE_DIM:]], axis=-1)


def masked_softmax(s, mask):
    return jax.nn.softmax(jnp.where(mask, s, -jnp.inf), axis=-1)


def swiglu(x, wg, wu, wd):
    return (jax.nn.silu(x @ wg) * (x @ wu)) @ wd


def in_projection(h, w_in, b_fox_f, w_gla_gk, b_gla_gk):
    B, T, _ = h.shape
    offs = np.cumsum(IN_SPLITS)[:-1].tolist()
    fq, fk, fv, ff, nq, nkv, ng, gq, gk, gv, glr, gog = jnp.split(h @ w_in, offs, axis=-1)
    fox = (fq.reshape(B, T, H_FOX, HEAD_DIM), fk.reshape(B, T, H_FOX, HEAD_DIM),
           fv.reshape(B, T, H_FOX, HEAD_DIM), jax.nn.log_sigmoid((ff + b_fox_f).astype(F32)))
    nsa = (nq.reshape(B, T, H_NSA, HEAD_DIM), nkv.reshape(B, T, 6, HEAD_DIM), ng.reshape(B, T, H_NSA, 3))
    glog = jax.nn.log_sigmoid((glr @ w_gla_gk + b_gla_gk).astype(F32)) / GLA_TAU
    gla = (gq.reshape(B, T, H_GLA, GLA_DK), gk.reshape(B, T, H_GLA, GLA_DK),
           gv.reshape(B, T, H_GLA, GLA_DV), glog.reshape(B, T, H_GLA, GLA_DK))
    return fox, nsa, gla, gog


def out_projection(o_fox, o_nsa, o_gla, gog, g_gla_norm, w_out):
    B, T = o_fox.shape[:2]
    og = rmsnorm(o_gla, g_gla_norm).reshape(B, T, D_GLA) * jax.nn.silu(gog)
    return jnp.concatenate([o_fox, o_nsa, og.astype(o_fox.dtype)], axis=-1) @ w_out


def fox_attend(q, c_q, q_pos, k, v, c_k, k_pos):
    s = jnp.einsum('bqhd,bkhd->bhqk', q, k).astype(F32) * HEAD_DIM ** -0.5
    bias = jnp.swapaxes(c_q, 1, 2)[..., :, None] - jnp.swapaxes(c_k, 1, 2)[..., None, :]
    mask = k_pos[None, :] <= q_pos[:, None]
    p = masked_softmax(s + bias, mask)
    return jnp.einsum('bhqk,bkhd->bqhd', p.astype(v.dtype), v)


def fox_prompt(q, k, v, logf):
    B, S = q.shape[:2]
    c = jnp.cumsum(logf, axis=1)
    pos = jnp.arange(S)

    def block(i):
        qs = i * Q_BLOCK
        qb = lax.dynamic_slice_in_dim(q, qs, Q_BLOCK, axis=1)
        cb = lax.dynamic_slice_in_dim(c, qs, Q_BLOCK, axis=1)
        return fox_attend(qb, cb, qs + jnp.arange(Q_BLOCK), k, v, c, pos)

    o = lax.map(block, jnp.arange(S // Q_BLOCK))
    return jnp.moveaxis(o, 0, 1).reshape(B, S, D_FOX)


def nsa_compress(k, v, w_cmp, pe_cmp):
    B, T, _ = k.shape
    nb = T // CMP_LEN
    kb = (k.reshape(B, nb, CMP_LEN, HEAD_DIM) + pe_cmp[0]).reshape(B, nb, CMP_LEN * HEAD_DIM) @ w_cmp[0]
    vb = (v.reshape(B, nb, CMP_LEN, HEAD_DIM) + pe_cmp[1]).reshape(B, nb, CMP_LEN * HEAD_DIM) @ w_cmp[1]
    return kb, vb


def nsa_cmp_attend(q, q_pos, kc, vc):
    nb = kc.shape[1]
    s = jnp.einsum('bqhd,bnd->bhqn', q, kc).astype(F32) * HEAD_DIM ** -0.5
    block_end = (jnp.arange(nb) + 1) * CMP_LEN - 1
    mask = block_end[None, :] <= q_pos[:, None]
    p = jax.nn.softmax(jnp.where(mask, s, -1e30), axis=-1) * mask
    o = jnp.einsum('bhqn,bnd->bqhd', p.astype(vc.dtype), vc)
    return o, p


def nsa_select(p_cmp, q_pos, n_sel):
    B, H, Q, nb = p_cmp.shape
    imp = p_cmp.sum(axis=1).reshape(B, Q, n_sel, SEL_LEN // CMP_LEN).sum(-1)
    j = jnp.arange(n_sel)[None, :]
    jt = (q_pos // SEL_LEN)[:, None]
    score = jnp.where(j == jt, 2.0 * SEL_FORCE,
                      jnp.where((j == 0) | (j == jt - 1), SEL_FORCE,
                                jnp.where(j <= jt, imp, -1.0)))
    _, idx = lax.top_k(score, min(TOP_N, n_sel))
    return idx


def nsa_sel_attend(q, q_pos, idx, kb, vb):
    B, Q, N = idx.shape
    kg = jax.vmap(lambda blocks, ib: blocks[ib])(kb, idx).reshape(B, Q, N * SEL_LEN, HEAD_DIM)
    vg = jax.vmap(lambda blocks, ib: blocks[ib])(vb, idx).reshape(B, Q, N * SEL_LEN, HEAD_DIM)
    k_pos = (idx[..., None] * SEL_LEN + jnp.arange(SEL_LEN)).reshape(B, Q, N * SEL_LEN)
    s = jnp.einsum('bqhd,bqkd->bhqk', q, kg).astype(F32) * HEAD_DIM ** -0.5
    mask = (k_pos <= q_pos[None, :, None])[:, None]
    p = masked_softmax(s, mask)
    return jnp.einsum('bhqk,bqkd->bqhd', p.astype(vg.dtype), vg)


def nsa_win_attend(q, q_pos, k, v, k_pos):
    s = jnp.einsum('bqhd,bkd->bhqk', q, k).astype(F32) * HEAD_DIM ** -0.5
    diff = q_pos[:, None] - k_pos[None, :]
    mask = (diff >= 0) & (diff < WINDOW) & (k_pos[None, :] >= 0)
    p = masked_softmax(s, mask)
    return jnp.einsum('bhqk,bkd->bqhd', p.astype(v.dtype), v)


def nsa_combine(gates, o_cmp, o_sel, o_win):
    B, T = o_cmp.shape[:2]
    g = jax.nn.sigmoid(gates.astype(F32)).astype(o_cmp.dtype)
    o = g[..., 0:1] * o_cmp + g[..., 1:2] * o_sel + g[..., 2:3] * o_win
    return o.reshape(B, T, D_NSA)


def nsa_prompt(q, kv, gates, w_cmp, pe_cmp):
    B, S = q.shape[:2]
    pos = jnp.arange(S)
    kc, vc, ks, vs, kw, vw = [kv[:, :, i] for i in range(6)]
    q_r = partial_rope(q, pos)
    ks = partial_rope(ks, pos)
    kw = partial_rope(kw, pos)
    kcb, vcb = nsa_compress(kc, vc, w_cmp, pe_cmp)
    o_cmp, p_cmp = nsa_cmp_attend(q, pos, kcb, vcb)
    n_sel = S // SEL_LEN
    idx = nsa_select(p_cmp, pos, n_sel)
    ksb = ks.reshape(B, n_sel, SEL_LEN, HEAD_DIM)
    vsb = vs.reshape(B, n_sel, SEL_LEN, HEAD_DIM)
    kw_pad = jnp.pad(kw, ((0, 0), (WINDOW, 0), (0, 0)))
    vw_pad = jnp.pad(vw, ((0, 0), (WINDOW, 0), (0, 0)))

    def block(i):
        qs = i * Q_BLOCK
        qpos = qs + jnp.arange(Q_BLOCK)
        qb = lax.dynamic_slice_in_dim(q_r, qs, Q_BLOCK, axis=1)
        ib = lax.dynamic_slice_in_dim(idx, qs, Q_BLOCK, axis=1)
        o_sel = nsa_sel_attend(qb, qpos, ib, ksb, vsb)
        kwb = lax.dynamic_slice_in_dim(kw_pad, qs, WINDOW + Q_BLOCK, axis=1)
        vwb = lax.dynamic_slice_in_dim(vw_pad, qs, WINDOW + Q_BLOCK, axis=1)
        o_win = nsa_win_attend(qb, qpos, kwb, vwb, qs - WINDOW + jnp.arange(WINDOW + Q_BLOCK))
        return o_sel, o_win

    o_sel, o_win = lax.map(block, jnp.arange(S // Q_BLOCK))
    o_sel = jnp.moveaxis(o_sel, 0, 1).reshape(B, S, H_NSA, HEAD_DIM)
    o_win = jnp.moveaxis(o_win, 0, 1).reshape(B, S, H_NSA, HEAD_DIM)
    o = nsa_combine(gates, o_cmp, o_sel, o_win)
    wb = min(WINDOW, S)
    rows = jnp.stack([kc, vc, ks, vs], axis=2)
    win = jnp.stack([kw[:, S - wb:], vw[:, S - wb:]], axis=2)
    return o, rows, win


def nsa_sample(q, kv, gates, past_rows, win_buf, past_len, w_cmp, pe_cmp):
    DB, TN = q.shape[:2]
    q_pos = past_len + jnp.arange(TN)
    kc, vc, ks, vs, kw, vw = [kv[:, :, i] for i in range(6)]
    q_r = partial_rope(q, q_pos)
    ks = partial_rope(ks, q_pos)
    kw = partial_rope(kw, q_pos)
    new_rows = jnp.stack([kc, vc, ks, vs], axis=2)
    T = past_len + TN
    t_pad = -(-T // SEL_LEN) * SEL_LEN
    all_rows = jnp.pad(jnp.concatenate([past_rows, new_rows], axis=1),
                       ((0, 0), (0, t_pad - T), (0, 0), (0, 0)))
    kcb, vcb = nsa_compress(all_rows[:, :, 0], all_rows[:, :, 1], w_cmp, pe_cmp)
    o_cmp, p_cmp = nsa_cmp_attend(q, q_pos, kcb, vcb)
    n_sel = t_pad // SEL_LEN
    idx = nsa_select(p_cmp, q_pos, n_sel)
    o_sel = nsa_sel_attend(q_r, q_pos, idx,
                           all_rows[:, :, 2].reshape(DB, n_sel, SEL_LEN, HEAD_DIM),
                           all_rows[:, :, 3].reshape(DB, n_sel, SEL_LEN, HEAD_DIM))
    wb = win_buf.shape[1]
    win_all = jnp.concatenate([win_buf, jnp.stack([kw, vw], axis=2)], axis=1)
    win_pos = past_len - wb + jnp.arange(wb + TN)
    o_win = nsa_win_attend(q_r, q_pos, win_all[:, :, 0], win_all[:, :, 1], win_pos)
    o = nsa_combine(gates, o_cmp, o_sel, o_win)
    return o, new_rows, win_all[:, TN:]


def gla_prompt(q, k, v, g):
    B, S, H, DK = q.shape
    DV = v.shape[-1]
    n = S // GLA_CHUNK

    def chunks(t):
        return t.astype(F32).reshape(B, n, GLA_CHUNK, H, t.shape[-1]).transpose(0, 3, 1, 2, 4)

    qc = chunks(q) * DK ** -0.5
    kc = chunks(k)
    vc = chunks(v)
    G = jnp.cumsum(chunks(g), axis=3)
    G_last = G[:, :, :, -1:]
    q_e = qc * jnp.exp(G)
    k_e = kc * jnp.exp(-G)
    causal = jnp.tril(jnp.ones((GLA_CHUNK, GLA_CHUNK), bool))
    A = jnp.where(causal, jnp.einsum('bhncd,bhnsd->bhncs', q_e, k_e), 0.0)
    o_intra = jnp.einsum('bhncs,bhnsv->bhncv', A, vc)
    dS = jnp.einsum('bhncd,bhncv->bhndv', kc * jnp.exp(G_last - G), vc)
    decay = jnp.exp(G_last[:, :, :, 0])

    def step(state, inp):
        d, ds = inp
        return d[..., None] * state + ds, state

    s_final, s_before = lax.scan(step, jnp.zeros((B, H, DK, DV), F32),
                                 (jnp.moveaxis(decay, 2, 0), jnp.moveaxis(dS, 2, 0)))
    o_inter = jnp.einsum('bhncd,nbhdv->bhncv', q_e, s_before)
    o = (o_intra + o_inter).transpose(0, 2, 3, 1, 4).reshape(B, S, H, DV)
    return o.astype(v.dtype), s_final.astype(v.dtype)


def gla_sample(q, k, v, g, state):
    scale = q.shape[-1] ** -0.5

    def step(s, inp):
        qt, kt, vt, gt = inp
        s = jnp.exp(gt)[..., None] * s + kt[..., :, None] * vt[..., None, :]
        return s, jnp.einsum('bhd,bhdv->bhv', qt, s)

    xs = tuple(jnp.moveaxis(t.astype(F32), 1, 0) for t in (q * scale, k, v, g))
    s_final, o = lax.scan(step, state.astype(F32), xs)
    return jnp.moveaxis(o, 0, 1).astype(v.dtype), s_final.astype(v.dtype)


def moe_swiglu(x, w_router, w_gate, w_up, w_down):
    n, d = x.shape
    logits = (x @ w_router).astype(F32)
    top_logit, top_e = lax.top_k(logits, TOP_K)
    gate = jax.nn.softmax(top_logit, axis=-1).astype(x.dtype)
    flat_e = top_e.reshape(-1)
    flat_tok = jnp.repeat(jnp.arange(n, dtype=jnp.int32), TOP_K)
    order = jnp.argsort(flat_e)
    e_sorted = flat_e[order]
    counts = jnp.zeros((N_EXPERTS,), jnp.int32).at[flat_e].add(1)
    padded = (counts + MOE_BLOCK - 1) // MOE_BLOCK * MOE_BLOCK
    start = jnp.cumsum(counts) - counts
    ends_padded = jnp.cumsum(padded)
    pstart = ends_padded - padded
    dest = pstart[e_sorted] + jnp.arange(n * TOP_K, dtype=jnp.int32) - start[e_sorted]
    n_blocks = -(-(n * TOP_K) // MOE_BLOCK) + N_EXPERTS
    cap = n_blocks * MOE_BLOCK
    slot_tok = jnp.full((cap,), n, jnp.int32).at[dest].set(flat_tok[order])
    slot_gate = jnp.zeros((cap,), x.dtype).at[dest].set(gate.reshape(-1)[order])
    block_e = jnp.minimum(jnp.searchsorted(ends_padded, jnp.arange(n_blocks, dtype=jnp.int32) * MOE_BLOCK,
                                           side='right'), N_EXPERTS - 1)
    xb = jnp.concatenate([x, jnp.zeros((1, d), x.dtype)], axis=0)[slot_tok].reshape(n_blocks, MOE_BLOCK, d)

    def expert_block(args):
        xblk, e = args
        return swiglu(xblk, w_gate[e], w_up[e], w_down[e])

    yb = lax.map(expert_block, (xb, block_e)).reshape(cap, d)
    return jax.ops.segment_sum(yb * slot_gate[:, None], slot_tok, num_segments=n + 1)[:n]


def setup_inputs(seed: int = 0) -> dict:
    key = jax.random.key(seed)
    ks = jax.random.split(key, 26)
    n_pages = PAST_LEN // PAGE_SIZE
    n_pool = (5 * DEC_BATCH * n_pages + 3) // 4
    wb = min(WINDOW, PAST_LEN)
    n_dense = (DEPTH + 1) // 2
    n_moe = DEPTH // 2
    d_in = sum(IN_SPLITS)

    def nrm(k, shape, scale):
        return scale * jax.random.normal(k, shape, F32)

    return {
        'x_prompt': nrm(ks[0], (BATCH, SEQ, D_MODEL), 1.0),
        'x_sample': nrm(ks[1], (DEC_BATCH, DEC_SEQ, D_MODEL), 1.0),
        'cache_fox_kv': nrm(ks[2], (DEPTH, n_pool, PAGE_SIZE, 2, H_FOX, HEAD_DIM), 1.0),
        'cache_fox_logf': jax.nn.log_sigmoid(3.0 + nrm(ks[3], (DEPTH, n_pool, PAGE_SIZE, H_FOX), 1.0)),
        'cache_nsa_kv': nrm(ks[4], (DEPTH, n_pool, PAGE_SIZE, 4, HEAD_DIM), 1.0),
        'state_nsa_win': nrm(ks[5], (DEPTH, DEC_BATCH, wb, 2, HEAD_DIM), 1.0),
        'state_gla': nrm(ks[6], (DEPTH, DEC_BATCH, H_GLA, GLA_DK, GLA_DV), 2.0),
        'page_table': jax.random.permutation(ks[7], n_pool)[:DEC_BATCH * n_pages]
                         .reshape(DEC_BATCH, n_pages).astype(jnp.int32),
        'norm_mix_g': 1.0 + nrm(ks[8], (DEPTH, D_MODEL), 0.02),
        'w_in': nrm(ks[9], (DEPTH, D_MODEL, d_in), D_MODEL ** -0.5),
        'b_fox_f': 3.0 + nrm(ks[10], (DEPTH, H_FOX), 1.0),
        'w_cmp': nrm(ks[11], (DEPTH, 2, CMP_LEN * HEAD_DIM, HEAD_DIM), (CMP_LEN * HEAD_DIM) ** -0.5),
        'pe_cmp': nrm(ks[12], (DEPTH, 2, CMP_LEN, HEAD_DIM), 0.02),
        'w_gla_gk': nrm(ks[13], (DEPTH, GLA_RANK, H_GLA * GLA_DK), GLA_RANK ** -0.5),
        'b_gla_gk': nrm(ks[14], (DEPTH, H_GLA * GLA_DK), 0.5),
        'g_gla_norm': 1.0 + nrm(ks[15], (DEPTH, GLA_DV), 0.02),
        'w_out': nrm(ks[16], (DEPTH, D_MIX, D_MODEL), D_MIX ** -0.5),
        'norm_ffn_g': 1.0 + nrm(ks[17], (DEPTH, D_MODEL), 0.02),
        'dense_w_gate': nrm(ks[18], (n_dense, D_MODEL, D_FF), D_MODEL ** -0.5),
        'dense_w_up': nrm(ks[19], (n_dense, D_MODEL, D_FF), D_MODEL ** -0.5),
        'dense_w_down': nrm(ks[20], (n_dense, D_FF, D_MODEL), D_FF ** -0.5),
        'moe_w_router': nrm(ks[21], (n_moe, D_MODEL, N_EXPERTS), D_MODEL ** -0.5),
        'moe_w_gate': nrm(ks[22], (n_moe, N_EXPERTS, D_MODEL, D_FF), D_MODEL ** -0.5),
        'moe_w_up': nrm(ks[23], (n_moe, N_EXPERTS, D_MODEL, D_FF), D_MODEL ** -0.5),
        'moe_w_down': nrm(ks[24], (n_moe, N_EXPERTS, D_FF, D_MODEL), D_FF ** -0.5),
        'final_norm_g': 1.0 + nrm(ks[25], (D_MODEL,), 0.02),
    }


def reference(x_prompt, x_sample, cache_fox_kv, cache_fox_logf, cache_nsa_kv, state_nsa_win, state_gla,
              page_table, norm_mix_g, w_in, b_fox_f, w_cmp, pe_cmp, w_gla_gk, b_gla_gk, g_gla_norm, w_out,
              norm_ffn_g, dense_w_gate, dense_w_up, dense_w_down, moe_w_router, moe_w_gate, moe_w_up,
              moe_w_down, final_norm_g):
    B, S, _ = x_prompt.shape
    DB, TN, _ = x_sample.shape
    past_len = page_table.shape[1] * cache_fox_kv.shape[2]
    q_pos_s = past_len + jnp.arange(TN)
    k_pos_s = jnp.arange(past_len + TN)
    xp, xs = x_prompt, x_sample
    fox_kv_p, fox_lf_p, nsa_kv_p, nsa_win_p, gla_p = [], [], [], [], []
    fox_kv_s, fox_lf_s, nsa_kv_s, nsa_win_s, gla_s = [], [], [], [], []

    for l in range(DEPTH):
        h = rmsnorm(xp, norm_mix_g[l])
        (fq, fk, fv, flf), (nq, nkv, ng), (gq, gk, gv, glg), gog = in_projection(
            h, w_in[l], b_fox_f[l], w_gla_gk[l], b_gla_gk[l])
        o_fox = fox_prompt(fq, fk, fv, flf)
        o_nsa, rows, win = nsa_prompt(nq, nkv, ng, w_cmp[l], pe_cmp[l])
        o_gla, g_state = gla_prompt(gq, gk, gv, glg)
        xp = xp + out_projection(o_fox, o_nsa, o_gla, gog, g_gla_norm[l], w_out[l])
        fox_kv_p.append(jnp.stack([fk, fv], axis=2))
        fox_lf_p.append(flf)
        nsa_kv_p.append(rows)
        nsa_win_p.append(win)
        gla_p.append(g_state)

        h = rmsnorm(xs, norm_mix_g[l])
        (fq, fk, fv, flf), (nq, nkv, ng), (gq, gk, gv, glg), gog = in_projection(
            h, w_in[l], b_fox_f[l], w_gla_gk[l], b_gla_gk[l])
        past = cache_fox_kv[l][page_table].reshape(DB, past_len, 2, H_FOX, HEAD_DIM)
        past_lf = cache_fox_logf[l][page_table].reshape(DB, past_len, H_FOX)
        k_all = jnp.concatenate([past[:, :, 0], fk], axis=1)
        v_all = jnp.concatenate([past[:, :, 1], fv], axis=1)
        c_all = jnp.cumsum(jnp.concatenate([past_lf.astype(F32), flf], axis=1), axis=1)
        o_fox = fox_attend(fq, c_all[:, past_len:], q_pos_s, k_all, v_all, c_all, k_pos_s).reshape(DB, TN, D_FOX)
        past_nsa = cache_nsa_kv[l][page_table].reshape(DB, past_len, 4, HEAD_DIM)
        o_nsa, rows, win = nsa_sample(nq, nkv, ng, past_nsa, state_nsa_win[l], past_len, w_cmp[l], pe_cmp[l])
        o_gla, g_state = gla_sample(gq, gk, gv, glg, state_gla[l])
        xs = xs + out_projection(o_fox, o_nsa, o_gla, gog, g_gla_norm[l], w_out[l])
        fox_kv_s.append(jnp.stack([fk, fv], axis=2))
        fox_lf_s.append(flf)
        nsa_kv_s.append(rows)
        nsa_win_s.append(win)
        gla_s.append(g_state)

        hp = rmsnorm(xp, norm_ffn_g[l])
        hs = rmsnorm(xs, norm_ffn_g[l])
        i = l // 2
        if l % 2 == 0:
            xp = xp + swiglu(hp, dense_w_gate[i], dense_w_up[i], dense_w_down[i])
            xs = xs + swiglu(hs, dense_w_gate[i], dense_w_up[i], dense_w_down[i])
        else:
            xp = xp + moe_swiglu(hp.reshape(B * S, D_MODEL), moe_w_router[i], moe_w_gate[i],
                                 moe_w_up[i], moe_w_down[i]).reshape(B, S, D_MODEL)
            xs = xs + moe_swiglu(hs.reshape(DB * TN, D_MODEL), moe_w_router[i], moe_w_gate[i],
                                 moe_w_up[i], moe_w_down[i]).reshape(DB, TN, D_MODEL)

    y_prompt = rmsnorm(xp, final_norm_g)
    y_sample = rmsnorm(xs, final_norm_g)
    return (y_prompt, y_sample,
            jnp.stack(fox_kv_p), jnp.stack(fox_lf_p), jnp.stack(nsa_kv_p), jnp.stack(nsa_win_p), jnp.stack(gla_p),
            jnp.stack(fox_kv_s), jnp.stack(fox_lf_s), jnp.stack(nsa_kv_s), jnp.stack(nsa_win_s), jnp.stack(gla_s))
```

```python
import functools

import jax
import jax.numpy as jnp
from jax import lax
from jax.experimental import pallas as pl
from jax.experimental.pallas import tpu as pltpu

F32 = jnp.float32
BF16 = jnp.bfloat16
I32 = jnp.int32
HI = lax.Precision.HIGHEST

D_MODEL = 1024
HEAD_DIM = 64
N_HEADS = 4
D_FOX = N_HEADS * HEAD_DIM
D_NSA = N_HEADS * HEAD_DIM
GLA_DV = 128
D_GLA = N_HEADS * GLA_DV
GLA_RANK = 16
GLA_TAU = 16.0
GLA_CHUNK = 64
CMP_LEN = 32
SEL_LEN = 64
TOP_N = 16
WINDOW = 512
ROPE_THETA = 500000.0
ROPE_DIM = HEAD_DIM // 4
ROPE_HALF = ROPE_DIM // 2
D_FF = 3584
N_EXPERTS = 8
EPS = 1e-6
SEL_FORCE = 1e9
NEG = -1e30
SCALE = HEAD_DIM ** -0.5

LANES = 128
SUBLANES = 8
VMEM_LIMIT = 56 * 1024 * 1024

C_FQ = 0
C_FKV = 256
C_NQ = 768
C_NKV = 1024
C_GQK = 1408
C_GV = 1920
C_GOG = 2432
C_SMALL = 2944
C_END = 3072
SM_FF = 0
SM_NG = 4
SM_GLR = 16

NT = (((1,), (1,)), ((), ()))


def _params(sem):
    return pltpu.CompilerParams(dimension_semantics=sem, vmem_limit_bytes=VMEM_LIMIT)


def _rms(x, g):
    ms = jnp.mean(x * x, axis=-1, keepdims=True)
    return x * lax.rsqrt(ms + EPS) * g


def _sigmoid(x):
    return 1.0 / (1.0 + jnp.exp(-x))


def _log_sigmoid(x):
    return -(jnp.maximum(-x, 0.0) + jnp.log1p(jnp.exp(-jnp.abs(x))))


def _silu(x):
    return x * _sigmoid(x)


def _bdot(a, b):
    return jnp.dot(a.astype(BF16), b.astype(BF16), preferred_element_type=F32)


def _bdot_nt(a, b):
    return lax.dot_general(a.astype(BF16), b.astype(BF16), NT, preferred_element_type=F32)


def _rope128(x, a, bp, bm):
    return x * a + pltpu.roll(x, ROPE_HALF, 1) * bp + pltpu.roll(x, LANES - ROPE_HALF, 1) * bm


def _inproj_kernel(x_ref, g_ref, w_ref, sb_ref, wgk_ref, bgk_ref, tab_ref,
                   fq_ref, fkv_ref, nq_ref, nqr_ref, rows_ref, win_ref,
                   gqk_ref, gv_ref, gog_ref, glog_ref, small_ref):
    h = _rms(x_ref[...], g_ref[...]).astype(BF16)

    def mm(a, b):
        return jnp.dot(h, w_ref[:, a:b], preferred_element_type=F32)

    fq_ref[...] = mm(C_FQ, C_FKV)
    fkv_ref[...] = mm(C_FKV, C_NQ)
    tab = tab_ref[...]
    ab, pb, mb = tab[:, 0:128], tab[:, 128:256], tab[:, 256:384]
    af, pf, mf = tab[:, 384:512], tab[:, 512:640], tab[:, 640:768]
    nq = mm(C_NQ, C_NKV)
    nq_ref[...] = nq
    nqr_ref[:, 0:128] = _rope128(nq[:, 0:128], ab, pb, mb)
    nqr_ref[:, 128:256] = _rope128(nq[:, 128:256], ab, pb, mb)
    nkv = mm(C_NKV, C_GQK)
    rows_ref[:, 0:128] = nkv[:, 0:128]
    rows_ref[:, 128:256] = _rope128(nkv[:, 128:256], af, pf, mf)
    win_ref[...] = _rope128(nkv[:, 256:384], af, pf, mf)
    gqk_ref[...] = mm(C_GQK, C_GV)
    gv_ref[...] = mm(C_GV, C_GOG)
    gog_ref[...] = mm(C_GOG, C_SMALL)
    sm = mm(C_SMALL, C_END)
    glog_ref[...] = _log_sigmoid(_bdot(sm, wgk_ref[...]) + bgk_ref[...]) * (1.0 / GLA_TAU)
    smb = sm + sb_ref[...]
    lane = lax.broadcasted_iota(I32, smb.shape, 1)
    small_ref[...] = jnp.where(lane < SM_NG, _log_sigmoid(smb), _sigmoid(smb))


def _in_projection(x_all, row0, n_rows, tm, g, w_r, sb, wgk, bgk, tab, tab_period):
    assert n_rows % tm == 0 and row0 % tm == 0 and tab_period % tm == 0
    nt = n_rows // tm
    b0 = row0 // tm
    npd = tab_period // tm
    widths = (256, 512, 256, 256, 256, 128, 512, 512, 512, 256, 128)
    full = lambda shape: pl.BlockSpec(shape, lambda i: (0, 0))
    return pl.pallas_call(
        _inproj_kernel,
        grid=(nt,),
        in_specs=[
            pl.BlockSpec((tm, D_MODEL), lambda i: (b0 + i, 0)),
            full((1, D_MODEL)),
            full((D_MODEL, C_END)),
            full((1, LANES)),
            full((LANES, 256)),
            full((1, 256)),
            pl.BlockSpec((tm, 768), lambda i: (i % npd, 0)),
        ],
        out_specs=[pl.BlockSpec((tm, w), lambda i: (i, 0)) for w in widths],
        out_shape=[jax.ShapeDtypeStruct((n_rows, w), F32) for w in widths],
        compiler_params=_params(("parallel",)),
        name="in_projection",
    )(x_all, g, w_r, sb, wgk, bgk, tab)


def _cumsum_kernel(sm_ref, cc_ref, cr_ref, carry):
    t = pl.program_id(1)
    ts = sm_ref.shape[0]

    @pl.when(t == 0)
    def _():
        carry[...] = jnp.zeros_like(carry)

    r = lax.broadcasted_iota(I32, (ts, ts), 0)
    c = lax.broadcasted_iota(I32, (ts, ts), 1)
    tri = (c <= r).astype(F32)
    cs = jnp.dot(tri, sm_ref[...], precision=HI, preferred_element_type=F32) + carry[...]
    cc_ref[...] = cs
    carry[...] = cs[ts - 1:ts, :]
    cr_ref[...] = cs.T[0:SUBLANES, :]


def _fox_cumsum(small, B, S, ts):
    ns = S // ts
    return pl.pallas_call(
        _cumsum_kernel,
        grid=(B, ns),
        in_specs=[pl.BlockSpec((ts, LANES), lambda b, t: (b * ns + t, 0))],
        out_specs=[pl.BlockSpec((ts, LANES), lambda b, t: (b * ns + t, 0)),
                   pl.BlockSpec((None, SUBLANES, ts), lambda b, t: (b, 0, t))],
        out_shape=[jax.ShapeDtypeStruct((B * S, LANES), F32),
                   jax.ShapeDtypeStruct((B, SUBLANES, S), F32)],
        scratch_shapes=[pltpu.VMEM((1, LANES), F32)],
        compiler_params=_params(("parallel", "arbitrary")),
        name="fox_cumsum",
    )(small)


def _fox_prompt_kernel(q_ref, kv_ref, cc_ref, cr_ref, o_ref, m_sc, l_sc, acc_sc):
    i = pl.program_id(1)
    j = pl.program_id(2)
    nk = pl.num_programs(2)
    tq = q_ref.shape[0]
    tk = kv_ref.shape[0]

    @pl.when(j == 0)
    def _():
        m_sc[...] = jnp.full_like(m_sc, NEG)
        l_sc[...] = jnp.zeros_like(l_sc)
        acc_sc[...] = jnp.zeros_like(acc_sc)

    @pl.when(j <= i)
    def _():
        qpos = i * tq + lax.broadcasted_iota(I32, (tq, 1), 0)
        kpos = j * tk + lax.broadcasted_iota(I32, (1, tk), 1)
        mask = kpos <= qpos
        for h in range(N_HEADS):
            lo = h * HEAD_DIM
            hi = lo + HEAD_DIM
            s = _bdot_nt(q_ref[:, lo:hi], kv_ref[:, lo:hi]) * SCALE
            s = s + (cc_ref[:, h:h + 1] - cr_ref[h:h + 1, :])
            s = jnp.where(mask, s, NEG)
            m_old = m_sc[h]
            m_new = jnp.maximum(m_old, jnp.max(s, axis=-1, keepdims=True))
            alpha = jnp.exp(m_old - m_new)
            p = jnp.exp(s - m_new)
            l_sc[h] = alpha * l_sc[h] + jnp.sum(p, axis=-1, keepdims=True)
            acc_sc[:, lo:hi] = alpha * acc_sc[:, lo:hi] + _bdot(p, kv_ref[:, D_FOX + lo:D_FOX + hi])
            m_sc[h] = m_new

    @pl.when(j == nk - 1)
    def _():
        for h in range(N_HEADS):
            lo = h * HEAD_DIM
            hi = lo + HEAD_DIM
            o_ref[:, lo:hi] = acc_sc[:, lo:hi] / l_sc[h]


def _fox_prompt(fq, fkv, cc, cr, B, S, t):
    n = S // t
    return pl.pallas_call(
        _fox_prompt_kernel,
        grid=(B, n, n),
        in_specs=[
            pl.BlockSpec((t, D_FOX), lambda b, i, j: (b * n + i, 0)),
            pl.BlockSpec((t, 2 * D_FOX), lambda b, i, j: (b * n + jnp.minimum(i, j), 0)),
            pl.BlockSpec((t, LANES), lambda b, i, j: (b * n + i, 0)),
            pl.BlockSpec((None, SUBLANES, t), lambda b, i, j: (b, 0, jnp.minimum(i, j))),
        ],
        out_specs=pl.BlockSpec((t, D_FOX), lambda b, i, j: (b * n + i, 0)),
        out_shape=jax.ShapeDtypeStruct((B * S, D_FOX), F32),
        scratch_shapes=[pltpu.VMEM((N_HEADS, t, 1), F32), pltpu.VMEM((N_HEADS, t, 1), F32),
                        pltpu.VMEM((t, D_FOX), F32)],
        compiler_params=_params(("parallel", "parallel", "arbitrary")),
        name="fox_prompt",
    )(fq, fkv, cc, cr)


def _compress_kernel(x_ref, pe_ref, w_ref, o_ref, *, exact):
    x = x_ref[...] + pe_ref[...]
    if exact:
        o_ref[...] = jnp.dot(x, w_ref[...], precision=HI, preferred_element_type=F32)
    else:
        o_ref[...] = _bdot(x, w_ref[...])


def _compress(x2d, pe_big, w_big, tr, exact, row0=0, n_rows=None):
    K = x2d.shape[1]
    R = x2d.shape[0] if n_rows is None else n_rows
    assert R % tr == 0 and row0 % tr == 0
    b0 = row0 // tr
    return pl.pallas_call(
        functools.partial(_compress_kernel, exact=exact),
        grid=(R // tr,),
        in_specs=[pl.BlockSpec((tr, K), lambda i: (b0 + i, 0)),
                  pl.BlockSpec((1, K), lambda i: (0, 0)),
                  pl.BlockSpec((K, LANES), lambda i: (0, 0))],
        out_specs=pl.BlockSpec((tr, LANES), lambda i: (i, 0)),
        out_shape=jax.ShapeDtypeStruct((R, LANES), F32),
        compiler_params=_params(("parallel",)),
        name="nsa_compress",
    )(x2d, pe_big, w_big)


def _order_key(x):
    b = lax.bitcast_convert_type(x, I32)
    return jnp.where(b < 0, b ^ jnp.int32(0x7FFFFFFF), b)


def _nsa_prompt_kernel(nq_ref, nqr_ref, sm_ref, cmp_ref, rows_ref, win_ref, o_ref, *, tk):
    QB = nq_ref.shape[0]
    S = rows_ref.shape[0]
    nb = cmp_ref.shape[0]
    nsel = S // SEL_LEN
    i = pl.program_id(1)
    qs = i * QB
    qpos = qs + lax.broadcasted_iota(I32, (QB, 1), 0)

    cmp = cmp_ref[...]
    kc = cmp[:, 0:HEAD_DIM]
    vc = cmp[:, HEAD_DIM:2 * HEAD_DIM]
    n_l = lax.broadcasted_iota(I32, (1, nb), 1)
    complete = ((n_l + 1) * CMP_LEN - 1) <= qpos
    complete_f = complete.astype(F32)
    psum = jnp.zeros((QB, nb), F32)
    o_cmp = []
    for h in range(N_HEADS):
        qh = nq_ref[:, h * HEAD_DIM:(h + 1) * HEAD_DIM]
        s = lax.dot_general(qh, kc, NT, precision=HI, preferred_element_type=F32) * SCALE
        s = jnp.where(complete, s, NEG)
        e = jnp.exp(s - jnp.max(s, axis=-1, keepdims=True))
        p = e / jnp.sum(e, axis=-1, keepdims=True) * complete_f
        o_cmp.append(_bdot(p, vc))
        psum = psum + p

    pj = lax.broadcasted_iota(I32, (nsel, nb), 0)
    pn = lax.broadcasted_iota(I32, (nsel, nb), 1)
    pair_t = (pn // (SEL_LEN // CMP_LEN) == pj).astype(F32)
    imp_t = lax.dot_general(pair_t, psum, NT, precision=HI, preferred_element_type=F32)
    jt = (qs + lax.broadcasted_iota(I32, (1, QB), 1)) // SEL_LEN
    jj = lax.broadcasted_iota(I32, (nsel, 1), 0)
    score = jnp.where(jj == jt, 2.0 * SEL_FORCE,
                      jnp.where((jj == 0) | (jj == jt - 1), SEL_FORCE,
                                jnp.where(jj <= jt, imp_t + 0.0, -1.0)))
    key = _order_key(score)
    key_m1 = key - 1
    ngrp = nsel // SUBLANES
    sub = lax.broadcasted_iota(I32, (SUBLANES, QB), 0)
    kg = [key[r * SUBLANES:(r + 1) * SUBLANES, :] for r in range(ngrp)]
    kg1 = [key_m1[r * SUBLANES:(r + 1) * SUBLANES, :] for r in range(ngrp)]
    cnt = [jnp.zeros((SUBLANES, QB), I32) for _ in range(ngrp)]
    for jp in range(nsel):
        g = jp // SUBLANES
        row = key[jp:jp + 1, :]
        mixed = jnp.where(sub > (jp % SUBLANES), kg1[g], kg[g])
        for r in range(ngrp):
            thr = kg[r] if r < g else (kg1[r] if r > g else mixed)
            cnt[r] = cnt[r] + (row > thr).astype(I32)
    sel_t = jnp.concatenate([(c < TOP_N).astype(F32) for c in cnt], axis=0)
    if nsel < QB:
        sel_t = jnp.concatenate([sel_t, jnp.zeros((QB - nsel, QB), F32)], axis=0)
    sel = sel_t.T.astype(BF16)

    qr = [nqr_ref[:, h * HEAD_DIM:(h + 1) * HEAD_DIM].astype(BF16) for h in range(N_HEADS)]
    jrow = lax.broadcasted_iota(I32, (QB, 1), 0)

    def sel_step(t, carry):
        ms, ls, accs = carry
        k0 = pl.multiple_of(t * tk, tk)
        kpos = k0 + lax.broadcasted_iota(I32, (1, tk), 1)
        expand = (jrow == kpos // SEL_LEN).astype(BF16)
        picked = jnp.dot(sel, expand, preferred_element_type=F32)
        mbias = jnp.where((picked > 0.5) & (kpos <= qpos), 0.0, NEG)
        ks = rows_ref[pl.ds(k0, tk), 2 * HEAD_DIM:3 * HEAD_DIM].astype(BF16)
        vs = rows_ref[pl.ds(k0, tk), 3 * HEAD_DIM:4 * HEAD_DIM].astype(BF16)
        ms2, ls2, accs2 = [], [], []
        for h in range(N_HEADS):
            s = lax.dot_general(qr[h], ks, NT, preferred_element_type=F32) * SCALE + mbias
            m_new = jnp.maximum(ms[h], jnp.max(s, axis=-1, keepdims=True))
            alpha = jnp.exp(ms[h] - m_new)
            p = jnp.exp(s - m_new)
            ls2.append(alpha * ls[h] + jnp.sum(p, axis=-1, keepdims=True))
            accs2.append(alpha * accs[h] + jnp.dot(p.astype(BF16), vs, preferred_element_type=F32))
            ms2.append(m_new)
        return tuple(ms2), tuple(ls2), tuple(accs2)

    init = (tuple(jnp.full((QB, 1), NEG, F32) for _ in range(N_HEADS)),
            tuple(jnp.zeros((QB, 1), F32) for _ in range(N_HEADS)),
            tuple(jnp.zeros((QB, HEAD_DIM), F32) for _ in range(N_HEADS)))
    n_tiles = (qs + QB + tk - 1) // tk
    _, ls, accs = lax.fori_loop(0, n_tiles, sel_step, init)
    o_sel = [accs[h] / ls[h] for h in range(N_HEADS)]

    wlen = WINDOW + QB
    w0 = pl.multiple_of(jnp.maximum(qs - WINDOW, 0), QB)
    wpos = w0 + lax.broadcasted_iota(I32, (1, wlen), 1)
    d = qpos - wpos
    wmask = (d >= 0) & (d < WINDOW)
    kw = win_ref[pl.ds(w0, wlen), 0:HEAD_DIM].astype(BF16)
    vw = win_ref[pl.ds(w0, wlen), HEAD_DIM:2 * HEAD_DIM].astype(BF16)
    sm = sm_ref[...]
    for h in range(N_HEADS):
        s = lax.dot_general(qr[h], kw, NT, preferred_element_type=F32) * SCALE
        s = jnp.where(wmask, s, NEG)
        e = jnp.exp(s - jnp.max(s, axis=-1, keepdims=True))
        p = e / jnp.sum(e, axis=-1, keepdims=True)
        o_win = jnp.dot(p.astype(BF16), vw, preferred_element_type=F32)
        c = SM_NG + 3 * h
        o_ref[:, h * HEAD_DIM:(h + 1) * HEAD_DIM] = (
            sm[:, c:c + 1] * o_cmp[h] + sm[:, c + 1:c + 2] * o_sel[h] + sm[:, c + 2:c + 3] * o_win)


def _nsa_prompt(nq, nqr, small, cmp, rows, win, B, S, tk):
    QB = 128
    nq_t = S // QB
    nb = S // CMP_LEN
    assert S % tk == 0 and S >= WINDOW + QB
    return pl.pallas_call(
        functools.partial(_nsa_prompt_kernel, tk=tk),
        grid=(B, nq_t),
        in_specs=[
            pl.BlockSpec((QB, D_NSA), lambda b, i: (b * nq_t + i, 0)),
            pl.BlockSpec((QB, D_NSA), lambda b, i: (b * nq_t + i, 0)),
            pl.BlockSpec((QB, LANES), lambda b, i: (b * nq_t + i, 0)),
            pl.BlockSpec((nb, LANES), lambda b, i: (b, 0)),
            pl.BlockSpec((S, 4 * HEAD_DIM), lambda b, i: (b, 0)),
            pl.BlockSpec((S, 2 * HEAD_DIM), lambda b, i: (b, 0)),
        ],
        out_specs=pl.BlockSpec((QB, D_NSA), lambda b, i: (b * nq_t + i, 0)),
        out_shape=jax.ShapeDtypeStruct((B * S, D_NSA), F32),
        compiler_params=_params(("parallel", "parallel")),
        name="nsa_prompt",
    )(nq, nqr, small, cmp, rows, win)


def _gla_prompt_kernel(qk_ref, v_ref, g_ref, o_ref, st_ref, s_sc):
    t = pl.program_id(1)
    nt = pl.num_programs(1)
    tc = qk_ref.shape[0]
    C = GLA_CHUNK

    @pl.when(t == 0)
    def _():
        s_sc[...] = jnp.zeros_like(s_sc)

    r = lax.broadcasted_iota(I32, (C, C), 0)
    c = lax.broadcasted_iota(I32, (C, C), 1)
    causal = c <= r
    low = causal.astype(F32)
    up = (r <= c).astype(F32)
    k_t = qk_ref[:, D_FOX:2 * D_FOX].T
    g_t = g_ref[...].T
    for ci in range(tc // C):
        rs = slice(ci * C, (ci + 1) * C)
        for h in range(N_HEADS):
            hs = slice(h * HEAD_DIM, (h + 1) * HEAD_DIM)
            q = qk_ref[rs, hs] * SCALE
            k = qk_ref[rs, D_FOX + h * HEAD_DIM:D_FOX + (h + 1) * HEAD_DIM]
            v = v_ref[rs, h * GLA_DV:(h + 1) * GLA_DV].astype(BF16)
            gcum = jnp.dot(low, g_ref[rs, hs], precision=HI, preferred_element_type=F32)
            qe = (q * jnp.exp(gcum)).astype(BF16)
            ke = (k * jnp.exp(-gcum)).astype(BF16)
            a = jnp.where(causal, lax.dot_general(qe, ke, NT, preferred_element_type=F32), 0.0)
            gcum_t = jnp.dot(g_t[hs, rs], up, precision=HI, preferred_element_type=F32)
            glast_t = gcum_t[:, C - 1:C]
            kd_t = (k_t[hs, rs] * jnp.exp(glast_t - gcum_t)).astype(BF16)
            state = s_sc[h]
            o_ref[rs, h * GLA_DV:(h + 1) * GLA_DV] = (
                jnp.dot(a.astype(BF16), v, preferred_element_type=F32)
                + jnp.dot(qe, state.astype(BF16), preferred_element_type=F32))
            s_sc[h] = jnp.exp(glast_t) * state + jnp.dot(kd_t, v, preferred_element_type=F32)

    @pl.when(t == nt - 1)
    def _():
        st_ref[...] = s_sc[...]


def _gla_prompt(gqk, gv, glog, B, S, tc):
    nt = S // tc
    return pl.pallas_call(
        _gla_prompt_kernel,
        grid=(B, nt),
        in_specs=[pl.BlockSpec((tc, 2 * D_FOX), lambda b, t: (b * nt + t, 0)),
                  pl.BlockSpec((tc, D_GLA), lambda b, t: (b * nt + t, 0)),
                  pl.BlockSpec((tc, D_FOX), lambda b, t: (b * nt + t, 0))],
        out_specs=[pl.BlockSpec((tc, D_GLA), lambda b, t: (b * nt + t, 0)),
                   pl.BlockSpec((None, N_HEADS, HEAD_DIM, GLA_DV), lambda b, t: (b, 0, 0, 0))],
        out_shape=[jax.ShapeDtypeStruct((B * S, D_GLA), F32),
                   jax.ShapeDtypeStruct((B, N_HEADS, HEAD_DIM, GLA_DV), F32)],
        scratch_shapes=[pltpu.VMEM((N_HEADS, HEAD_DIM, GLA_DV), F32)],
        compiler_params=_params(("parallel", "arbitrary")),
        name="gla_prompt",
    )(gqk, gv, glog)


def _outproj_kernel(x_ref, of_ref, on_ref, og_ref, gog_ref, gn_ref, w_ref, o_ref):
    acc = _bdot(of_ref[...], w_ref[0:D_FOX, :])
    acc = acc + _bdot(on_ref[...], w_ref[D_FOX:D_FOX + D_NSA, :])
    for h in range(N_HEADS):
        hs = slice(h * GLA_DV, (h + 1) * GLA_DV)
        z = _rms(og_ref[:, hs], gn_ref[...]) * _silu(gog_ref[:, hs])
        w0 = D_FOX + D_NSA + h * GLA_DV
        acc = acc + _bdot(z, w_ref[w0:w0 + GLA_DV, :])
    o_ref[...] = x_ref[...] + acc


def _out_projection(x, o_fox, o_nsa, o_gla, gog, gn, w_out, tm):
    T = x.shape[0]
    assert T % tm == 0
    row = lambda w: pl.BlockSpec((tm, w), lambda i: (i, 0))
    return pl.pallas_call(
        _outproj_kernel,
        grid=(T // tm,),
        in_specs=[row(D_MODEL), row(D_FOX), row(D_NSA), row(D_GLA), row(D_GLA),
                  pl.BlockSpec((1, GLA_DV), lambda i: (0, 0)),
                  pl.BlockSpec((D_MODEL, D_MODEL), lambda i: (0, 0))],
        out_specs=row(D_MODEL),
        out_shape=jax.ShapeDtypeStruct((T, D_MODEL), F32),
        compiler_params=_params(("parallel",)),
        name="out_projection",
    )(x, o_fox, o_nsa, o_gla, gog, gn, w_out)


def _dense_ffn_kernel(x_ref, g_ref, wg_ref, wu_ref, wd_ref, o_ref, h_sc, acc_sc):
    f = pl.program_id(1)
    nf = pl.num_programs(1)

    @pl.when(f == 0)
    def _():
        h_sc[...] = _rms(x_ref[...], g_ref[...]).astype(BF16)
        acc_sc[...] = jnp.zeros_like(acc_sc)

    h = h_sc[...]
    a = jnp.dot(h, wg_ref[...], preferred_element_type=F32)
    u = jnp.dot(h, wu_ref[...], preferred_element_type=F32)
    acc_sc[...] += _bdot(_silu(a) * u, wd_ref[...])

    @pl.when(f == nf - 1)
    def _():
        o_ref[...] = x_ref[...] + acc_sc[...]


def _dense_ffn(x, g, wg, wu, wd, tm, tf):
    T = x.shape[0]
    assert T % tm == 0 and D_FF % tf == 0
    return pl.pallas_call(
        _dense_ffn_kernel,
        grid=(T // tm, D_FF // tf),
        in_specs=[pl.BlockSpec((tm, D_MODEL), lambda i, f: (i, 0)),
                  pl.BlockSpec((1, D_MODEL), lambda i, f: (0, 0)),
                  pl.BlockSpec((D_MODEL, tf), lambda i, f: (0, f)),
                  pl.BlockSpec((D_MODEL, tf), lambda i, f: (0, f)),
                  pl.BlockSpec((tf, D_MODEL), lambda i, f: (f, 0))],
        out_specs=pl.BlockSpec((tm, D_MODEL), lambda i, f: (i, 0)),
        out_shape=jax.ShapeDtypeStruct((T, D_MODEL), F32),
        scratch_shapes=[pltpu.VMEM((tm, D_MODEL), BF16), pltpu.VMEM((tm, D_MODEL), F32)],
        compiler_params=_params(("parallel", "arbitrary")),
        name="dense_ffn",
    )(x, g, wg, wu, wd)


def _router_kernel(x_ref, g_ref, wr_ref, h_ref, r_ref):
    h = _rms(x_ref[...], g_ref[...])
    h_ref[...] = h
    logits = jnp.dot(h, wr_ref[...], precision=HI, preferred_element_type=F32)
    lane = lax.broadcasted_iota(I32, logits.shape, 1)
    lg = jnp.where(lane < N_EXPERTS, logits, -jnp.inf)
    m1 = jnp.max(lg, axis=-1, keepdims=True)
    i1 = jnp.min(jnp.where(lg == m1, lane, LANES), axis=-1, keepdims=True)
    lg2 = jnp.where(lane == i1, -jnp.inf, lg)
    m2 = jnp.max(lg2, axis=-1, keepdims=True)
    i2 = jnp.min(jnp.where(lg2 == m2, lane, LANES), axis=-1, keepdims=True)
    e = jnp.exp(m2 - m1)
    den = 1.0 + e
    r_ref[...] = jnp.where(lane == 0, i1.astype(F32),
                           jnp.where(lane == 1, i2.astype(F32),
                                     jnp.where(lane == 2, 1.0 / den,
                                               jnp.where(lane == 3, e / den, 0.0))))


def _router(x, g, wr_pad, tm):
    T = x.shape[0]
    assert T % tm == 0
    return pl.pallas_call(
        _router_kernel,
        grid=(T // tm,),
        in_specs=[pl.BlockSpec((tm, D_MODEL), lambda i: (i, 0)),
                  pl.BlockSpec((1, D_MODEL), lambda i: (0, 0)),
                  pl.BlockSpec((D_MODEL, LANES), lambda i: (0, 0))],
        out_specs=[pl.BlockSpec((tm, D_MODEL), lambda i: (i, 0)),
                   pl.BlockSpec((tm, LANES), lambda i: (i, 0))],
        out_shape=[jax.ShapeDtypeStruct((T, D_MODEL), F32),
                   jax.ShapeDtypeStruct((T, LANES), F32)],
        compiler_params=_params(("parallel",)),
        name="moe_router",
    )(x, g, wr_pad)


GATHER_ROWS = 128


def _row_copy(src_ref, dst_ref, sem, src_row, dst_row):
    return pltpu.make_async_copy(src_ref.at[pl.ds(src_row, 1)], dst_ref.at[pl.ds(dst_row, 1)], sem)


def _gather_kernel(idx_ref, src_ref, dst_ref, sem):
    base = pl.program_id(0) * GATHER_ROWS

    def issue(r, c):
        _row_copy(src_ref, dst_ref, sem, idx_ref[base + r], base + r).start()
        return c

    lax.fori_loop(0, GATHER_ROWS, issue, 0)

    def drain(r, c):
        _row_copy(src_ref, dst_ref, sem, 0, base + r).wait()
        return c

    lax.fori_loop(0, GATHER_ROWS, drain, 0)


def _row_gather(src, idx):
    n = idx.shape[0]
    assert n % GATHER_ROWS == 0
    return pl.pallas_call(
        _gather_kernel,
        grid_spec=pltpu.PrefetchScalarGridSpec(
            num_scalar_prefetch=1,
            grid=(n // GATHER_ROWS,),
            in_specs=[pl.BlockSpec(memory_space=pl.ANY)],
            out_specs=pl.BlockSpec(memory_space=pl.ANY),
            scratch_shapes=[pltpu.SemaphoreType.DMA(())],
        ),
        out_shape=jax.ShapeDtypeStruct((n, src.shape[1]), src.dtype),
        compiler_params=_params(("arbitrary",)),
        name="row_gather",
    )(idx, src)


def _moe_ffn_kernel(be_ref, nu_ref, x_ref, wg_ref, wu_ref, wd_ref, o_ref, acc_sc):
    b = pl.program_id(0)
    f = pl.program_id(1)
    nf = pl.num_programs(1)
    used = b < nu_ref[0]

    @pl.when(used)
    def _():
        @pl.when(f == 0)
        def _():
            acc_sc[...] = jnp.zeros_like(acc_sc)

        x = x_ref[...].astype(BF16)
        a = jnp.dot(x, wg_ref[...].astype(BF16), preferred_element_type=F32)
        u = jnp.dot(x, wu_ref[...].astype(BF16), preferred_element_type=F32)
        acc_sc[...] += _bdot(_silu(a) * u, wd_ref[...])

        @pl.when(f == nf - 1)
        def _():
            o_ref[...] = acc_sc[...]

    @pl.when(jnp.logical_not(used) & (f == nf - 1))
    def _():
        o_ref[...] = jnp.zeros_like(o_ref)


def _moe_ffn(xb, block_e, n_used, wg, wu, wd, blk, tf):
    cap = xb.shape[0]
    nb = cap // blk
    nf = D_FF // tf

    def bsel(b, nu):
        return jnp.minimum(b, nu[0] - 1)

    def fsel(b, f, nu):
        return jnp.where(b < nu[0], f, nf - 1)

    return pl.pallas_call(
        _moe_ffn_kernel,
        grid_spec=pltpu.PrefetchScalarGridSpec(
            num_scalar_prefetch=2,
            grid=(nb, nf),
            in_specs=[
                pl.BlockSpec((blk, D_MODEL), lambda b, f, be, nu: (bsel(b, nu), 0)),
                pl.BlockSpec((None, D_MODEL, tf), lambda b, f, be, nu: (be[bsel(b, nu)], 0, fsel(b, f, nu))),
                pl.BlockSpec((None, D_MODEL, tf), lambda b, f, be, nu: (be[bsel(b, nu)], 0, fsel(b, f, nu))),
                pl.BlockSpec((None, tf, D_MODEL), lambda b, f, be, nu: (be[bsel(b, nu)], fsel(b, f, nu), 0)),
            ],
            out_specs=pl.BlockSpec((blk, D_MODEL), lambda b, f, be, nu: (b, 0)),
            scratch_shapes=[pltpu.VMEM((blk, D_MODEL), F32)],
        ),
        out_shape=jax.ShapeDtypeStruct((cap, D_MODEL), F32),
        compiler_params=_params(("arbitrary", "arbitrary")),
        name="moe_ffn",
    )(block_e, n_used, xb, wg, wu, wd)


def _moe_combine_kernel(x_ref, y1_ref, y2_ref, r_ref, o_ref):
    r = r_ref[...]
    o_ref[...] = x_ref[...] + (r[:, 2:3] * y1_ref[...] + r[:, 3:4] * y2_ref[...])


def _moe_combine(x, yg, route, tm):
    T = x.shape[0]
    nt = T // tm
    return pl.pallas_call(
        _moe_combine_kernel,
        grid=(nt,),
        in_specs=[pl.BlockSpec((tm, D_MODEL), lambda i: (i, 0)),
                  pl.BlockSpec((tm, D_MODEL), lambda i: (i, 0)),
                  pl.BlockSpec((tm, D_MODEL), lambda i: (nt + i, 0)),
                  pl.BlockSpec((tm, LANES), lambda i: (i, 0))],
        out_specs=pl.BlockSpec((tm, D_MODEL), lambda i: (i, 0)),
        out_shape=jax.ShapeDtypeStruct((T, D_MODEL), F32),
        compiler_params=_params(("parallel",)),
        name="moe_combine",
    )(x, yg, yg, route)


def _moe_plan(e_top, blk):
    T = e_top.shape[0]
    n = 2 * T
    assert n % blk == 0
    flat_e = e_top.reshape(-1)
    onehot = (flat_e[:, None] == jnp.arange(N_EXPERTS, dtype=I32)[None, :]).astype(I32)
    csum = jnp.cumsum(onehot, axis=0)
    rank = jnp.sum((csum - onehot) * onehot, axis=1)
    counts = csum[-1]
    padded = (counts + blk - 1) // blk * blk
    ends = jnp.cumsum(padded)
    pstart = ends - padded
    dest = (pstart[flat_e] + rank).astype(I32)
    n_blocks = n // blk + N_EXPERTS
    cap = n_blocks * blk
    slot_tok = jnp.zeros((cap,), I32).at[dest].set(jnp.arange(n, dtype=I32) // 2)
    block_e = jnp.minimum(jnp.searchsorted(ends, jnp.arange(n_blocks, dtype=I32) * blk, side='right'),
                          N_EXPERTS - 1).astype(I32)
    n_used = (ends[-1] // blk).astype(I32).reshape(1)
    return dest, slot_tok, block_e, n_used


def _moe_layer(x, g, wr_pad, wg, wu, wd, tm, blk, tf):
    T = x.shape[0]
    h, route = _router(x, g, wr_pad, tm)
    e_top = route[:, 0:2].astype(I32)
    dest, slot_tok, block_e, n_used = _moe_plan(e_top, blk)
    xb = _row_gather(h, slot_tok)
    yb = _moe_ffn(xb, block_e, n_used, wg, wu, wd, blk, tf)
    d2 = dest.reshape(T, 2)
    yg = _row_gather(yb, jnp.concatenate([d2[:, 0], d2[:, 1]]))
    return _moe_combine(x, yg, route, tm)


def _norm_kernel(x_ref, g_ref, o_ref):
    o_ref[...] = _rms(x_ref[...], g_ref[...])


def _final_norm(x, g, tm):
    T = x.shape[0]
    return pl.pallas_call(
        _norm_kernel,
        grid=(T // tm,),
        in_specs=[pl.BlockSpec((tm, D_MODEL), lambda i: (i, 0)),
                  pl.BlockSpec((1, D_MODEL), lambda i: (0, 0))],
        out_specs=pl.BlockSpec((tm, D_MODEL), lambda i: (i, 0)),
        out_shape=jax.ShapeDtypeStruct((T, D_MODEL), F32),
        compiler_params=_params(("parallel",)),
        name="final_norm",
    )(x, g)


_IN_SPLITS = (D_FOX, D_FOX, D_FOX, N_HEADS, D_NSA, 6 * HEAD_DIM, 3 * N_HEADS,
              D_FOX, D_FOX, D_GLA, GLA_RANK, D_GLA)


def _reorder_w_in(w):
    offs = [0]
    for s in _IN_SPLITS:
        offs.append(offs[-1] + s)
    seg = lambda k: w[:, offs[k]:offs[k + 1]]
    fq, fk, fv, ff, nq, nkv, ng, gq, gk, gv, glr, gog = [seg(k) for k in range(12)]
    pad = jnp.zeros((w.shape[0], LANES - SM_GLR - GLA_RANK), w.dtype)
    return jnp.concatenate([fq, fk, fv, nq, nkv, gq, gk, gv, gog, ff, ng, glr, pad], axis=1).astype(BF16)


def _rope_table(pos):
    inv = ROPE_THETA ** (-jnp.arange(ROPE_HALF, dtype=F32) / ROPE_HALF)
    ang = pos.astype(F32)[:, None] * inv[None, :]
    cos, sin = jnp.cos(ang), jnp.sin(ang)
    P = pos.shape[0]
    one = jnp.ones((P, HEAD_DIM - ROPE_DIM), F32)
    zero = jnp.zeros((P, HEAD_DIM - ROPE_DIM), F32)
    z8 = jnp.zeros((P, ROPE_HALF), F32)
    a64 = jnp.concatenate([cos, cos, one], axis=1)
    p64 = jnp.concatenate([z8, sin, zero], axis=1)
    m64 = jnp.concatenate([-sin, z8, zero], axis=1)
    i64 = jnp.ones((P, HEAD_DIM), F32)
    o64 = jnp.zeros((P, HEAD_DIM), F32)
    return jnp.concatenate([a64, a64, p64, p64, m64, m64, a64, i64, p64, o64, m64, o64], axis=1)


def _compress_weights(w_cmp, pe_cmp):
    wk = w_cmp[0].reshape(CMP_LEN, HEAD_DIM, HEAD_DIM)
    wv = w_cmp[1].reshape(CMP_LEN, HEAD_DIM, HEAD_DIM)
    z = jnp.zeros_like(wk)
    top = jnp.concatenate([wk, z], axis=2)
    mid = jnp.concatenate([z, wv], axis=2)
    rest = jnp.zeros((CMP_LEN, 2 * HEAD_DIM, 2 * HEAD_DIM), w_cmp.dtype)
    w_big = jnp.concatenate([top, mid, rest], axis=1).reshape(CMP_LEN * 4 * HEAD_DIM, 2 * HEAD_DIM)
    pe = jnp.concatenate([pe_cmp[0], pe_cmp[1], jnp.zeros((CMP_LEN, 2 * HEAD_DIM), pe_cmp.dtype)], axis=1)
    return w_big, pe.reshape(1, CMP_LEN * 4 * HEAD_DIM)


def _layer_mix_params(l, norm_mix_g, w_in, b_fox_f, w_cmp, pe_cmp, w_gla_gk, b_gla_gk, g_gla_norm, w_out):
    sb = jnp.zeros((1, LANES), F32).at[0, SM_FF:SM_FF + N_HEADS].set(b_fox_f[l])
    wgk = jnp.zeros((LANES, D_FOX), F32).at[SM_GLR:SM_GLR + GLA_RANK].set(w_gla_gk[l]).astype(BF16)
    w_big, pe_big = _compress_weights(w_cmp[l], pe_cmp[l])
    return dict(g=norm_mix_g[l].reshape(1, D_MODEL), w_r=_reorder_w_in(w_in[l]), sb=sb, wgk=wgk,
                bgk=b_gla_gk[l].reshape(1, D_FOX), w_big=w_big, pe_big=pe_big,
                gn=g_gla_norm[l].reshape(1, GLA_DV), w_out=w_out[l].astype(BF16))


def _mix_prompt(x, B, S, p, tab, tm, t_fox, tk_sel, tc_gla):
    (fq, fkv, nq, nqr, rows, win, gqk, gv, gog, glog, small) = _in_projection(
        x, 0, B * S, tm, p['g'], p['w_r'], p['sb'], p['wgk'], p['bgk'], tab, S)
    cc, cr = _fox_cumsum(small, B, S, min(S, 512))
    o_fox = _fox_prompt(fq, fkv, cc, cr, B, S, t_fox)
    blocks = rows.reshape(B * S // CMP_LEN, CMP_LEN * 4 * HEAD_DIM)
    cmp = _compress(blocks, p['pe_big'], p['w_big'], min(256, blocks.shape[0]), True)
    o_nsa = _nsa_prompt(nq, nqr, small, cmp, rows, win, B, S, tk_sel)
    o_gla, g_state = _gla_prompt(gqk, gv, glog, B, S, tc_gla)
    x_new = _out_projection(x, o_fox, o_nsa, o_gla, gog, p['gn'], p['w_out'], tm)
    return x_new, dict(fkv=fkv, small=small, rows=rows, win=win, g_state=g_state)


def _head_rows(row, width):
    r = lax.broadcasted_iota(I32, (SUBLANES, width), 0)
    c = lax.broadcasted_iota(I32, (SUBLANES, width), 1)
    return jnp.where(c // HEAD_DIM == r, jnp.broadcast_to(row, (SUBLANES, width)), 0.0)


def _fox_decode_kernel(pt_ref, q_ref, kvn_ref, smn_ref, *refs, n_pages):
    del pt_ref
    kv_refs = refs[0:n_pages]
    lf_refs = refs[n_pages:2 * n_pages]
    o_ref = refs[2 * n_pages]
    lf_sc = refs[2 * n_pages + 1]
    R = n_pages * SUBLANES
    PG = kv_refs[0].shape[0]
    qrows = _head_rows(q_ref[...], D_FOX)
    qrows_b = qrows.astype(BF16)

    lf_sc[...] = jnp.zeros_like(lf_sc)
    for p in range(n_pages):
        lf_sc[p * SUBLANES:p * SUBLANES + N_HEADS, :] = lf_refs[p][...]
    lft = lf_sc[...]
    k0 = lax.broadcasted_iota(I32, (PG, PG), 0)
    k1 = lax.broadcasted_iota(I32, (PG, PG), 1)
    within = jnp.dot(lft, (k0 > k1).astype(F32), precision=HI, preferred_element_type=F32)
    tot = jnp.broadcast_to(jnp.sum(lft, axis=1, keepdims=True), (R, PG))
    r0 = lax.broadcasted_iota(I32, (R, R), 0)
    r1 = lax.broadcasted_iota(I32, (R, R), 1)
    later = ((r1 % SUBLANES == r0 % SUBLANES) & (r1 // SUBLANES > r0 // SUBLANES)).astype(F32)
    cross = jnp.dot(later, tot, precision=HI, preferred_element_type=F32)
    rr = lax.broadcasted_iota(I32, (R, 1), 0) % SUBLANES
    smn = smn_ref[...]
    newcol = jnp.zeros((R, 1), F32)
    for h in range(N_HEADS):
        newcol = newcol + jnp.where(rr == h, smn[:, SM_FF + h:SM_FF + h + 1], 0.0)
    bias = (within + cross + newcol).reshape(n_pages, SUBLANES, PG)

    s3 = jnp.stack([lax.dot_general(qrows_b, kv_refs[p][:, 0:D_FOX].astype(BF16), NT,
                                    preferred_element_type=F32) for p in range(n_pages)])
    s3 = s3 * SCALE + bias
    kvn = kvn_ref[...]
    s_new = jnp.sum(qrows * kvn[:, 0:D_FOX], axis=1, keepdims=True) * SCALE
    m = jnp.max(jnp.max(s3, axis=2, keepdims=True), axis=0)
    m = jnp.maximum(m, s_new)
    p3 = jnp.exp(s3 - m[None])
    pn = jnp.exp(s_new - m)
    den = jnp.sum(jnp.sum(p3, axis=2, keepdims=True), axis=0) + pn
    acc = pn * kvn[:, D_FOX:2 * D_FOX]
    for p in range(n_pages):
        acc = acc + _bdot(p3[p], kv_refs[p][:, D_FOX:2 * D_FOX])
    o = acc / den
    o_ref[...] = jnp.sum(_head_rows(jnp.ones((1, D_FOX), F32), D_FOX) * o, axis=0, keepdims=True)


def _fox_decode(l, pt_flat, n_pages, fq, fkv, small, kv_cache, lft_cache):
    DB = fq.shape[0]
    PG = kv_cache.shape[2]
    page = lambda p: (lambda b, pt: (l, pt[b * n_pages + p], 0, 0))
    row = lambda w: pl.BlockSpec((None, 1, w), lambda b, pt: (b, 0, 0))
    return pl.pallas_call(
        functools.partial(_fox_decode_kernel, n_pages=n_pages),
        grid_spec=pltpu.PrefetchScalarGridSpec(
            num_scalar_prefetch=1,
            grid=(DB,),
            in_specs=[row(D_FOX), row(2 * D_FOX), row(LANES)]
            + [pl.BlockSpec((None, None, PG, 2 * D_FOX), page(p)) for p in range(n_pages)]
            + [pl.BlockSpec((None, None, N_HEADS, PG), page(p)) for p in range(n_pages)],
            out_specs=row(D_FOX),
            scratch_shapes=[pltpu.VMEM((n_pages * SUBLANES, PG), F32)],
        ),
        out_shape=jax.ShapeDtypeStruct((DB, 1, D_FOX), F32),
        compiler_params=_params(("arbitrary",)),
        name="fox_decode",
    )(pt_flat, fq.reshape(DB, 1, D_FOX), fkv.reshape(DB, 1, 2 * D_FOX), small.reshape(DB, 1, LANES),
      *([kv_cache] * n_pages), *([lft_cache] * n_pages)).reshape(DB, D_FOX)


def _nsa_decode_kernel(pt_ref, q_ref, qr_ref, rown_ref, winn_ref, gate_ref, win_ref, *refs,
                       n_pages, past_len):
    del pt_ref
    pg_refs = refs[0:n_pages]
    cmp_refs = refs[n_pages:2 * n_pages]
    o_ref, nw_ref, cmp_sc, qc_sc, qs_sc, qw_sc = refs[2 * n_pages:]
    R = n_pages * SUBLANES
    PG = pg_refs[0].shape[0]
    WB = win_ref.shape[0]
    per_page = PG // CMP_LEN
    assert per_page <= SUBLANES and PG == 2 * SEL_LEN and R == LANES
    jt = past_len // SEL_LEN

    qc_sc[...] = jnp.zeros_like(qc_sc)
    qc_sc[0:N_HEADS, 0:HEAD_DIM] = q_ref[...]
    qs_sc[...] = jnp.zeros_like(qs_sc)
    qs_sc[0:N_HEADS, 2 * HEAD_DIM:3 * HEAD_DIM] = qr_ref[...]
    qw_sc[...] = jnp.zeros_like(qw_sc)
    qw_sc[0:N_HEADS, 0:HEAD_DIM] = qr_ref[...]
    head_row = lax.broadcasted_iota(I32, (SUBLANES, 1), 0) < N_HEADS

    cmp_sc[...] = jnp.zeros_like(cmp_sc)
    for p in range(n_pages):
        cmp_sc[p * SUBLANES:p * SUBLANES + per_page, :] = cmp_refs[p][...]
    cmpa = cmp_sc[...]
    lane = lax.broadcasted_iota(I32, (1, R), 1)
    blk = per_page * (lane // SUBLANES) + lane % SUBLANES
    complete = (lane % SUBLANES < per_page) & ((blk + 1) * CMP_LEN - 1 <= past_len)
    s = lax.dot_general(qc_sc[...], cmpa, NT, precision=HI, preferred_element_type=F32) * SCALE
    s = jnp.where(complete, s, NEG)
    e = jnp.exp(s - jnp.max(s, axis=-1, keepdims=True))
    pc = e / jnp.sum(e, axis=-1, keepdims=True) * complete.astype(F32)
    o_cmp = _bdot(pc, cmpa)

    imp_c = jnp.sum(jnp.where(head_row, pc, 0.0), axis=0, keepdims=True)
    imp_s = imp_c + pltpu.roll(imp_c, R - 1, 1)
    cand = (lane % SUBLANES == 0) | (lane % SUBLANES == 2)
    jsel = 2 * (lane // SUBLANES) + (lane % SUBLANES) // 2
    score = jnp.where(jsel == jt, 2.0 * SEL_FORCE,
                      jnp.where((jsel == 0) | (jsel == jt - 1), SEL_FORCE,
                                jnp.where(jsel <= jt, imp_s + 0.0, -1.0)))
    score_b = jnp.broadcast_to(score, (R, R))
    key_row = _order_key(score_b)
    key_col = _order_key(score_b.T)
    l0 = lax.broadcasted_iota(I32, (R, R), 0)
    l1 = lax.broadcasted_iota(I32, (R, R), 1)
    cand_col = (l0 % SUBLANES == 0) | (l0 % SUBLANES == 2)
    beats = cand_col & (key_col > jnp.where(l0 < l1, key_row - 1, key_row))
    cnt = jnp.sum(beats.astype(I32), axis=0, keepdims=True)
    sel_row = (cand & (cnt < TOP_N - 1)).astype(F32)
    sel_col = jnp.broadcast_to(sel_row, (R, R)).T
    half = ((l0 % SUBLANES == 0) & (l1 < SEL_LEN)) | ((l0 % SUBLANES == 2) & (l1 >= SEL_LEN))
    z = jnp.where(half, sel_col, 0.0)
    same_page = (l1 // SUBLANES == l0 // SUBLANES).astype(BF16)
    picked = jnp.dot(same_page, z.astype(BF16), preferred_element_type=F32)
    picked = picked.reshape(n_pages, SUBLANES, PG) > 0.5

    rown = rown_ref[...]
    qs = qs_sc[...]
    qs_b = qs.astype(BF16)
    s3 = jnp.stack([lax.dot_general(qs_b, pg_refs[p][...].astype(BF16), NT, preferred_element_type=F32)
                    for p in range(n_pages)]) * SCALE
    s3 = jnp.where(picked, s3, NEG)
    s_new = jnp.sum(qs * rown, axis=1, keepdims=True) * SCALE
    m = jnp.maximum(jnp.max(jnp.max(s3, axis=2, keepdims=True), axis=0), s_new)
    p3 = jnp.exp(s3 - m[None])
    pn = jnp.exp(s_new - m)
    den = jnp.sum(jnp.sum(p3, axis=2, keepdims=True), axis=0) + pn
    acc = pn * rown
    for p in range(n_pages):
        acc = acc + _bdot(p3[p], pg_refs[p][...])
    o_sel = acc / den

    wb = win_ref[...]
    winn = winn_ref[...]
    qw = qw_sc[...]
    wpos = past_len - WB + lax.broadcasted_iota(I32, (1, WB), 1)
    wd = past_len - wpos
    wok = (wd >= 0) & (wd < WINDOW) & (wpos >= 0)
    sw = lax.dot_general(qw.astype(BF16), wb.astype(BF16), NT, preferred_element_type=F32) * SCALE
    sw = jnp.where(wok, sw, NEG)
    sw_new = jnp.sum(qw * winn, axis=1, keepdims=True) * SCALE
    mw = jnp.maximum(jnp.max(sw, axis=-1, keepdims=True), sw_new)
    ew = jnp.exp(sw - mw)
    en = jnp.exp(sw_new - mw)
    o_win = (_bdot(ew, wb) + en * winn) / (jnp.sum(ew, axis=-1, keepdims=True) + en)

    g = gate_ref[...]
    o_ref[...] = (g[:, 0:1] * o_cmp[0:N_HEADS, HEAD_DIM:2 * HEAD_DIM]
                  + g[:, 1:2] * o_sel[0:N_HEADS, 3 * HEAD_DIM:4 * HEAD_DIM]
                  + g[:, 2:3] * o_win[0:N_HEADS, HEAD_DIM:2 * HEAD_DIM])
    nw_ref[0:WB - 1, :] = win_ref[1:WB, :]
    nw_ref[WB - 1:WB, :] = winn


def _nsa_decode(l, pt_flat, n_pages, past_len, nq, nqr, rows, win, small, nsa_cache, cmp_pool, win_state):
    DB = nq.shape[0]
    PG = nsa_cache.shape[2]
    WB = win_state.shape[2]
    page = lambda p: (lambda b, pt: (l, pt[b * n_pages + p], 0, 0))
    cpage = lambda p: (lambda b, pt: (pt[b * n_pages + p], 0, 0))
    per_b = lambda *s: pl.BlockSpec((None,) + s, lambda b, pt: (b,) + (0,) * len(s))
    gates = small[:, SM_NG:SM_NG + 3 * N_HEADS].reshape(DB, N_HEADS, 3)
    o, nw = pl.pallas_call(
        functools.partial(_nsa_decode_kernel, n_pages=n_pages, past_len=past_len),
        grid_spec=pltpu.PrefetchScalarGridSpec(
            num_scalar_prefetch=1,
            grid=(DB,),
            in_specs=[per_b(N_HEADS, HEAD_DIM), per_b(N_HEADS, HEAD_DIM), per_b(1, 4 * HEAD_DIM),
                      per_b(1, 2 * HEAD_DIM), per_b(N_HEADS, 3),
                      pl.BlockSpec((None, None, WB, 2 * HEAD_DIM), lambda b, pt: (l, b, 0, 0))]
            + [pl.BlockSpec((None, None, PG, 4 * HEAD_DIM), page(p)) for p in range(n_pages)]
            + [pl.BlockSpec((None, PG // CMP_LEN, LANES), cpage(p)) for p in range(n_pages)],
            out_specs=[per_b(N_HEADS, HEAD_DIM), per_b(WB, 2 * HEAD_DIM)],
            scratch_shapes=[pltpu.VMEM((n_pages * SUBLANES, LANES), F32),
                            pltpu.VMEM((SUBLANES, LANES), F32),
                            pltpu.VMEM((SUBLANES, 4 * HEAD_DIM), F32),
                            pltpu.VMEM((SUBLANES, LANES), F32)],
        ),
        out_shape=[jax.ShapeDtypeStruct((DB, N_HEADS, HEAD_DIM), F32),
                   jax.ShapeDtypeStruct((DB, WB, 2 * HEAD_DIM), F32)],
        compiler_params=_params(("arbitrary",)),
        name="nsa_decode",
    )(pt_flat, nq.reshape(DB, N_HEADS, HEAD_DIM), nqr.reshape(DB, N_HEADS, HEAD_DIM),
      rows.reshape(DB, 1, 4 * HEAD_DIM), win.reshape(DB, 1, 2 * HEAD_DIM), gates, win_state,
      *([nsa_cache] * n_pages), *([cmp_pool] * n_pages))
    return o.reshape(DB, D_NSA), nw


def _gla_decode_kernel(q_ref, k_ref, g_ref, v_ref, s_ref, o_ref, so_ref):
    s_new = jnp.exp(g_ref[...]) * s_ref[...] + k_ref[...] * v_ref[...]
    so_ref[...] = s_new
    o_ref[...] = jnp.sum((q_ref[...] * SCALE) * s_new, axis=2, keepdims=True)


def _gla_decode(l, gqk, gv, glog, state, nb):
    DB = gqk.shape[0]
    col = lambda a: a.reshape(DB, N_HEADS, HEAD_DIM, 1)
    cspec = pl.BlockSpec((nb, N_HEADS, HEAD_DIM, 1), lambda i: (i, 0, 0, 0))
    vspec = pl.BlockSpec((nb, N_HEADS, 1, GLA_DV), lambda i: (i, 0, 0, 0))
    sspec = pl.BlockSpec((nb, N_HEADS, HEAD_DIM, GLA_DV), lambda i: (i, 0, 0, 0))
    o, s_new = pl.pallas_call(
        _gla_decode_kernel,
        grid=(DB // nb,),
        in_specs=[cspec, cspec, cspec, vspec,
                  pl.BlockSpec((None, nb, N_HEADS, HEAD_DIM, GLA_DV), lambda i: (l, i, 0, 0, 0))],
        out_specs=[vspec, sspec],
        out_shape=[jax.ShapeDtypeStruct((DB, N_HEADS, 1, GLA_DV), F32),
                   jax.ShapeDtypeStruct((DB, N_HEADS, HEAD_DIM, GLA_DV), F32)],
        compiler_params=_params(("parallel",)),
        name="gla_decode",
    )(col(gqk[:, 0:D_FOX]), col(gqk[:, D_FOX:2 * D_FOX]), col(glog),
      gv.reshape(DB, N_HEADS, 1, GLA_DV), state)
    return o.reshape(DB, D_GLA), s_new


def _mix_sample(x, l, p, tab, pt_flat, n_pages, past_len, fox_kv_c, fox_lft_c, nsa_c, win_state, gla_state):
    DB = x.shape[0]
    (fq, fkv, nq, nqr, rows, win, gqk, gv, gog, glog, small) = _in_projection(
        x, 0, DB, DB, p['g'], p['w_r'], p['sb'], p['wgk'], p['bgk'], tab, DB)
    o_fox = _fox_decode(l, pt_flat, n_pages, fq, fkv, small, fox_kv_c, fox_lft_c)
    n_pool, PG = nsa_c.shape[1], nsa_c.shape[2]
    blocks = nsa_c.reshape(-1, CMP_LEN * 4 * HEAD_DIM)
    per_layer = n_pool * (PG // CMP_LEN)
    cmp_pool = _compress(blocks, p['pe_big'], p['w_big'], 256, False, row0=l * per_layer, n_rows=per_layer)
    cmp_pool = cmp_pool.reshape(n_pool, PG // CMP_LEN, LANES)
    o_nsa, new_win = _nsa_decode(l, pt_flat, n_pages, past_len, nq, nqr, rows, win, small,
                                 nsa_c, cmp_pool, win_state)
    o_gla, g_state = _gla_decode(l, gqk, gv, glog, gla_state, 8)
    x_new = _out_projection(x, o_fox, o_nsa, o_gla, gog, p['gn'], p['w_out'], DB)
    return x_new, dict(fkv=fkv, small=small, rows=rows, win=new_win, g_state=g_state)


def kernel(x_prompt, x_sample, cache_fox_kv, cache_fox_logf, cache_nsa_kv, state_nsa_win, state_gla,
           page_table, norm_mix_g, w_in, b_fox_f, w_cmp, pe_cmp, w_gla_gk, b_gla_gk, g_gla_norm, w_out,
           norm_ffn_g, dense_w_gate, dense_w_up, dense_w_down, moe_w_router, moe_w_gate, moe_w_up,
           moe_w_down, final_norm_g):
    B, S, _ = x_prompt.shape
    DB, TN, _ = x_sample.shape
    assert TN == 1
    depth, n_pool, PG = cache_fox_kv.shape[0:3]
    n_pages = page_table.shape[1]
    past_len = n_pages * PG
    WB = state_nsa_win.shape[2]
    xp = x_prompt.reshape(B * S, D_MODEL)
    xs = x_sample.reshape(DB, D_MODEL)
    tab_p = _rope_table(jnp.arange(S))
    tab_s = _rope_table(jnp.full((DB,), past_len, I32))
    pt_flat = page_table.reshape(-1).astype(I32)
    fox_kv_c = cache_fox_kv.reshape(depth, n_pool, PG, 2 * D_FOX)
    fox_lft_c = jnp.swapaxes(cache_fox_logf, 2, 3)
    nsa_c = cache_nsa_kv.reshape(depth, n_pool, PG, 4 * HEAD_DIM)
    win_state = state_nsa_win.reshape(depth, DB, WB, 2 * HEAD_DIM)
    cp, cs = [], []
    for l in range(depth):
        p = _layer_mix_params(l, norm_mix_g, w_in, b_fox_f, w_cmp, pe_cmp, w_gla_gk, b_gla_gk,
                              g_gla_norm, w_out)
        xp, c = _mix_prompt(xp, B, S, p, tab_p, 512, 512, 512, 256)
        cp.append(c)
        xs, c = _mix_sample(xs, l, p, tab_s, pt_flat, n_pages, past_len, fox_kv_c, fox_lft_c, nsa_c,
                            win_state, state_gla)
        cs.append(c)
        gf = norm_ffn_g[l].reshape(1, D_MODEL)
        i = l // 2
        if l % 2 == 0:
            wg, wu, wd = (dense_w_gate[i].astype(BF16), dense_w_up[i].astype(BF16),
                          dense_w_down[i].astype(BF16))
            xp = _dense_ffn(xp, gf, wg, wu, wd, 1024, 512)
            xs = _dense_ffn(xs, gf, wg, wu, wd, DB, 512)
        else:
            wr = jnp.zeros((D_MODEL, LANES), F32).at[:, 0:N_EXPERTS].set(moe_w_router[i])
            xp = _moe_layer(xp, gf, wr, moe_w_gate[i], moe_w_up[i], moe_w_down[i], 512, 1024, 512)
            xs = _moe_layer(xs, gf, wr, moe_w_gate[i], moe_w_up[i], moe_w_down[i], DB, 128, 512)
    gfin = final_norm_g.reshape(1, D_MODEL)
    y_p = _final_norm(xp, gfin, 512).reshape(B, S, D_MODEL)
    y_s = _final_norm(xs, gfin, DB).reshape(DB, 1, D_MODEL)
    wp = min(WINDOW, S)
    st = lambda key, group: jnp.stack([c[key] for c in group])
    return (y_p, y_s,
            st('fkv', cp).reshape(depth, B, S, 2, N_HEADS, HEAD_DIM),
            st('small', cp)[:, :, SM_FF:SM_FF + N_HEADS].reshape(depth, B, S, N_HEADS),
            st('rows', cp).reshape(depth, B, S, 4, HEAD_DIM),
            st('win', cp).reshape(depth, B, S, 2, HEAD_DIM)[:, :, S - wp:],
            st('g_state', cp),
            st('fkv', cs).reshape(depth, DB, 1, 2, N_HEADS, HEAD_DIM),
            st('small', cs)[:, :, SM_FF:SM_FF + N_HEADS].reshape(depth, DB, 1, N_HEADS),
            st('rows', cs).reshape(depth, DB, 1, 4, HEAD_DIM),
            st('win', cs).reshape(depth, DB, WB, 2, HEAD_DIM),
            st('g_state', cs))
```

```python
import functools

import jax
import jax.numpy as jnp
from jax import lax
from jax.experimental import pallas as pl
from jax.experimental.pallas import tpu as pltpu
from jax.experimental.pallas import tpu_sc as plsc

F32 = jnp.float32
BF16 = jnp.bfloat16
I32 = jnp.int32
HI = lax.Precision.HIGHEST

D_MODEL = 1024
HEAD_DIM = 64
N_HEADS = 4
D_FOX = N_HEADS * HEAD_DIM
D_NSA = N_HEADS * HEAD_DIM
GLA_DV = 128
D_GLA = N_HEADS * GLA_DV
GLA_RANK = 16
GLA_TAU = 16.0
GLA_CHUNK = 64
CMP_LEN = 32
SEL_LEN = 64
TOP_N = 16
WINDOW = 512
ROPE_THETA = 500000.0
ROPE_DIM = HEAD_DIM // 4
ROPE_HALF = ROPE_DIM // 2
D_FF = 3584
N_EXPERTS = 8
EPS = 1e-6
SEL_FORCE = 1e9
NEG = -1e30
SCALE = HEAD_DIM ** -0.5

LANES = 128
SUBLANES = 8
VMEM_LIMIT = 56 * 1024 * 1024

C_FQ = 0
C_FKV = 256
C_NQ = 768
C_NKV = 1024
C_GQK = 1408
C_GV = 1920
C_GOG = 2432
C_SMALL = 2944
C_END = 3072
SM_FF = 0
SM_NG = 4
SM_GLR = 16

NT = (((1,), (1,)), ((), ()))


def _params(sem):
    return pltpu.CompilerParams(dimension_semantics=sem, vmem_limit_bytes=VMEM_LIMIT)


def _rms(x, g):
    ms = jnp.mean(x * x, axis=-1, keepdims=True)
    return x * lax.rsqrt(ms + EPS) * g


def _sigmoid(x):
    return 1.0 / (1.0 + jnp.exp(-x))


def _log_sigmoid(x):
    return -(jnp.maximum(-x, 0.0) + jnp.log1p(jnp.exp(-jnp.abs(x))))


def _silu(x):
    return x * _sigmoid(x)


def _bdot(a, b):
    return jnp.dot(a.astype(BF16), b.astype(BF16), preferred_element_type=F32)


def _bdot_nt(a, b):
    return lax.dot_general(a.astype(BF16), b.astype(BF16), NT, preferred_element_type=F32)


def _rope128(x, a, bp, bm):
    return x * a + pltpu.roll(x, ROPE_HALF, 1) * bp + pltpu.roll(x, LANES - ROPE_HALF, 1) * bm


def _inproj_kernel(x_ref, g_ref, w_ref, sb_ref, wgk_ref, bgk_ref, tab_ref,
                   fq_ref, fkv_ref, nq_ref, nqr_ref, rows_ref, win_ref,
                   gqk_ref, gv_ref, gog_ref, glog_ref, small_ref):
    h = _rms(x_ref[...], g_ref[...]).astype(BF16)

    def mm(a, b):
        return jnp.dot(h, w_ref[:, a:b], preferred_element_type=F32)

    fq_ref[...] = mm(C_FQ, C_FKV)
    fkv_ref[...] = mm(C_FKV, C_NQ)
    tab = tab_ref[...]
    ab, pb, mb = tab[:, 0:128], tab[:, 128:256], tab[:, 256:384]
    af, pf, mf = tab[:, 384:512], tab[:, 512:640], tab[:, 640:768]
    nq = mm(C_NQ, C_NKV)
    nq_ref[...] = nq
    nqr_ref[:, 0:128] = _rope128(nq[:, 0:128], ab, pb, mb)
    nqr_ref[:, 128:256] = _rope128(nq[:, 128:256], ab, pb, mb)
    nkv = mm(C_NKV, C_GQK)
    rows_ref[:, 0:128] = nkv[:, 0:128]
    rows_ref[:, 128:256] = _rope128(nkv[:, 128:256], af, pf, mf)
    win_ref[...] = _rope128(nkv[:, 256:384], af, pf, mf)
    gqk_ref[...] = mm(C_GQK, C_GV)
    gv_ref[...] = mm(C_GV, C_GOG)
    gog_ref[...] = mm(C_GOG, C_SMALL)
    sm = mm(C_SMALL, C_END)
    glog_ref[...] = _log_sigmoid(_bdot(sm, wgk_ref[...]) + bgk_ref[...]) * (1.0 / GLA_TAU)
    smb = sm + sb_ref[...]
    lane = lax.broadcasted_iota(I32, smb.shape, 1)
    small_ref[...] = jnp.where(lane < SM_NG, _log_sigmoid(smb), _sigmoid(smb))


def _in_projection(x_all, row0, n_rows, tm, g, w_r, sb, wgk, bgk, tab, tab_period):
    assert n_rows % tm == 0 and row0 % tm == 0 and tab_period % tm == 0
    nt = n_rows // tm
    b0 = row0 // tm
    npd = tab_period // tm
    widths = (256, 512, 256, 256, 256, 128, 512, 512, 512, 256, 128)
    full = lambda shape: pl.BlockSpec(shape, lambda i: (0, 0))
    return pl.pallas_call(
        _inproj_kernel,
        grid=(nt,),
        in_specs=[
            pl.BlockSpec((tm, D_MODEL), lambda i: (b0 + i, 0)),
            full((1, D_MODEL)),
            full((D_MODEL, C_END)),
            full((1, LANES)),
            full((LANES, 256)),
            full((1, 256)),
            pl.BlockSpec((tm, 768), lambda i: (i % npd, 0)),
        ],
        out_specs=[pl.BlockSpec((tm, w), lambda i: (i, 0)) for w in widths],
        out_shape=[jax.ShapeDtypeStruct((n_rows, w), F32) for w in widths],
        compiler_params=_params(("parallel",)),
        name="in_projection",
    )(x_all, g, w_r, sb, wgk, bgk, tab)


def _cumsum_kernel(sm_ref, cc_ref, cr_ref, carry):
    t = pl.program_id(1)
    ts = sm_ref.shape[0]

    @pl.when(t == 0)
    def _():
        carry[...] = jnp.zeros_like(carry)

    r = lax.broadcasted_iota(I32, (ts, ts), 0)
    c = lax.broadcasted_iota(I32, (ts, ts), 1)
    tri = (c <= r).astype(F32)
    cs = jnp.dot(tri, sm_ref[...], precision=HI, preferred_element_type=F32) + carry[...]
    cc_ref[...] = cs
    carry[...] = cs[ts - 1:ts, :]
    cr_ref[...] = cs.T[0:SUBLANES, :]


def _fox_cumsum(small, B, S, ts):
    ns = S // ts
    return pl.pallas_call(
        _cumsum_kernel,
        grid=(B, ns),
        in_specs=[pl.BlockSpec((ts, LANES), lambda b, t: (b * ns + t, 0))],
        out_specs=[pl.BlockSpec((ts, LANES), lambda b, t: (b * ns + t, 0)),
                   pl.BlockSpec((None, SUBLANES, ts), lambda b, t: (b, 0, t))],
        out_shape=[jax.ShapeDtypeStruct((B * S, LANES), F32),
                   jax.ShapeDtypeStruct((B, SUBLANES, S), F32)],
        scratch_shapes=[pltpu.VMEM((1, LANES), F32)],
        compiler_params=_params(("parallel", "arbitrary")),
        name="fox_cumsum",
    )(small)


def _fox_prompt_kernel(q_ref, kv_ref, cc_ref, cr_ref, o_ref, m_sc, l_sc, acc_sc):
    i = pl.program_id(1)
    j = pl.program_id(2)
    nk = pl.num_programs(2)
    tq = q_ref.shape[0]
    tk = kv_ref.shape[0]

    @pl.when(j == 0)
    def _():
        m_sc[...] = jnp.full_like(m_sc, NEG)
        l_sc[...] = jnp.zeros_like(l_sc)
        acc_sc[...] = jnp.zeros_like(acc_sc)

    @pl.when(j <= i)
    def _():
        qpos = i * tq + lax.broadcasted_iota(I32, (tq, 1), 0)
        kpos = j * tk + lax.broadcasted_iota(I32, (1, tk), 1)
        mask = kpos <= qpos
        for h in range(N_HEADS):
            lo = h * HEAD_DIM
            hi = lo + HEAD_DIM
            s = _bdot_nt(q_ref[:, lo:hi], kv_ref[:, lo:hi]) * SCALE
            s = s + (cc_ref[:, h:h + 1] - cr_ref[h:h + 1, :])
            s = jnp.where(mask, s, NEG)
            m_old = m_sc[h]
            m_new = jnp.maximum(m_old, jnp.max(s, axis=-1, keepdims=True))
            alpha = jnp.exp(m_old - m_new)
            p = jnp.exp(s - m_new)
            l_sc[h] = alpha * l_sc[h] + jnp.sum(p, axis=-1, keepdims=True)
            acc_sc[:, lo:hi] = alpha * acc_sc[:, lo:hi] + _bdot(p, kv_ref[:, D_FOX + lo:D_FOX + hi])
            m_sc[h] = m_new

    @pl.when(j == nk - 1)
    def _():
        for h in range(N_HEADS):
            lo = h * HEAD_DIM
            hi = lo + HEAD_DIM
            o_ref[:, lo:hi] = acc_sc[:, lo:hi] / l_sc[h]


def _fox_prompt(fq, fkv, cc, cr, B, S, t):
    n = S // t
    return pl.pallas_call(
        _fox_prompt_kernel,
        grid=(B, n, n),
        in_specs=[
            pl.BlockSpec((t, D_FOX), lambda b, i, j: (b * n + i, 0)),
            pl.BlockSpec((t, 2 * D_FOX), lambda b, i, j: (b * n + jnp.minimum(i, j), 0)),
            pl.BlockSpec((t, LANES), lambda b, i, j: (b * n + i, 0)),
            pl.BlockSpec((None, SUBLANES, t), lambda b, i, j: (b, 0, jnp.minimum(i, j))),
        ],
        out_specs=pl.BlockSpec((t, D_FOX), lambda b, i, j: (b * n + i, 0)),
        out_shape=jax.ShapeDtypeStruct((B * S, D_FOX), F32),
        scratch_shapes=[pltpu.VMEM((N_HEADS, t, 1), F32), pltpu.VMEM((N_HEADS, t, 1), F32),
                        pltpu.VMEM((t, D_FOX), F32)],
        compiler_params=_params(("parallel", "parallel", "arbitrary")),
        name="fox_prompt",
    )(fq, fkv, cc, cr)


def _compress_kernel(x_ref, pe_ref, w_ref, o_ref, *, exact):
    for s in range(2):
        x = x_ref[s] + pe_ref[s]
        if exact:
            y = jnp.dot(x, w_ref[s], precision=HI, preferred_element_type=F32)
        else:
            y = _bdot(x, w_ref[s])
        o_ref[:, s * HEAD_DIM:(s + 1) * HEAD_DIM] = y


def _compress(x3, pe, w, tr, exact, row0=0, n_rows=None):
    K = x3.shape[2]
    R = x3.shape[1] if n_rows is None else n_rows
    assert R % tr == 0 and row0 % tr == 0
    b0 = row0 // tr
    return pl.pallas_call(
        functools.partial(_compress_kernel, exact=exact),
        grid=(R // tr,),
        in_specs=[pl.BlockSpec((2, tr, K), lambda i: (0, b0 + i, 0)),
                  pl.BlockSpec((2, 1, K), lambda i: (0, 0, 0)),
                  pl.BlockSpec((2, K, HEAD_DIM), lambda i: (0, 0, 0))],
        out_specs=pl.BlockSpec((tr, LANES), lambda i: (i, 0)),
        out_shape=jax.ShapeDtypeStruct((R, LANES), F32),
        compiler_params=_params(("parallel",)),
        name="nsa_compress",
    )(x3, pe, w)


def _block_major(kv, n_blocks):
    return kv.reshape(n_blocks, CMP_LEN, 2, HEAD_DIM).transpose(2, 0, 1, 3).reshape(
        2, n_blocks, CMP_LEN * HEAD_DIM)


def _order_key(x):
    b = lax.bitcast_convert_type(x, I32)
    return jnp.where(b < 0, b ^ jnp.int32(0x7FFFFFFF), b)


def _nsa_prompt_kernel(nq_ref, nqr_ref, sm_ref, cmp_ref, rows_ref, win_ref, o_ref, *, tk):
    QB = nq_ref.shape[0]
    S = rows_ref.shape[0]
    nb = cmp_ref.shape[0]
    nsel = S // SEL_LEN
    i = pl.program_id(1)
    qs = i * QB
    qpos = qs + lax.broadcasted_iota(I32, (QB, 1), 0)

    cmp = cmp_ref[...]
    kc = cmp[:, 0:HEAD_DIM]
    vc = cmp[:, HEAD_DIM:2 * HEAD_DIM]
    n_l = lax.broadcasted_iota(I32, (1, nb), 1)
    complete = ((n_l + 1) * CMP_LEN - 1) <= qpos
    complete_f = complete.astype(F32)
    psum = jnp.zeros((QB, nb), F32)
    o_cmp = []
    for h in range(N_HEADS):
        qh = nq_ref[:, h * HEAD_DIM:(h + 1) * HEAD_DIM]
        s = lax.dot_general(qh, kc, NT, precision=HI, preferred_element_type=F32) * SCALE
        s = jnp.where(complete, s, NEG)
        e = jnp.exp(s - jnp.max(s, axis=-1, keepdims=True))
        p = e / jnp.sum(e, axis=-1, keepdims=True) * complete_f
        o_cmp.append(_bdot(p, vc))
        psum = psum + p

    pj = lax.broadcasted_iota(I32, (nsel, nb), 0)
    pn = lax.broadcasted_iota(I32, (nsel, nb), 1)
    pair_t = (pn // (SEL_LEN // CMP_LEN) == pj).astype(F32)
    imp_t = lax.dot_general(pair_t, psum, NT, precision=HI, preferred_element_type=F32)
    jt = (qs + lax.broadcasted_iota(I32, (1, QB), 1)) // SEL_LEN
    jj = lax.broadcasted_iota(I32, (nsel, 1), 0)
    score = jnp.where(jj == jt, 2.0 * SEL_FORCE,
                      jnp.where((jj == 0) | (jj == jt - 1), SEL_FORCE,
                                jnp.where(jj <= jt, imp_t + 0.0, -1.0)))
    key = _order_key(score)
    key_m1 = key - 1
    ngrp = nsel // SUBLANES
    sub = lax.broadcasted_iota(I32, (SUBLANES, QB), 0)
    kg = [key[r * SUBLANES:(r + 1) * SUBLANES, :] for r in range(ngrp)]
    kg1 = [key_m1[r * SUBLANES:(r + 1) * SUBLANES, :] for r in range(ngrp)]
    cnt = [jnp.zeros((SUBLANES, QB), I32) for _ in range(ngrp)]
    for jp in range(nsel):
        g = jp // SUBLANES
        row = key[jp:jp + 1, :]
        mixed = jnp.where(sub > (jp % SUBLANES), kg1[g], kg[g])
        for r in range(ngrp):
            thr = kg[r] if r < g else (kg1[r] if r > g else mixed)
            cnt[r] = cnt[r] + (row > thr).astype(I32)
    sel_t = jnp.concatenate([(c < TOP_N).astype(F32) for c in cnt], axis=0)
    if nsel < QB:
        sel_t = jnp.concatenate([sel_t, jnp.zeros((QB - nsel, QB), F32)], axis=0)
    sel = sel_t.T.astype(BF16)

    qr = [nqr_ref[:, h * HEAD_DIM:(h + 1) * HEAD_DIM].astype(BF16) for h in range(N_HEADS)]
    jrow = lax.broadcasted_iota(I32, (QB, 1), 0)

    def sel_step(t, carry):
        ms, ls, accs = carry
        k0 = pl.multiple_of(t * tk, tk)
        kpos = k0 + lax.broadcasted_iota(I32, (1, tk), 1)
        expand = (jrow == kpos // SEL_LEN).astype(BF16)
        picked = jnp.dot(sel, expand, preferred_element_type=F32)
        mbias = jnp.where((picked > 0.5) & (kpos <= qpos), 0.0, NEG)
        ks = rows_ref[pl.ds(k0, tk), 2 * HEAD_DIM:3 * HEAD_DIM].astype(BF16)
        vs = rows_ref[pl.ds(k0, tk), 3 * HEAD_DIM:4 * HEAD_DIM].astype(BF16)
        ms2, ls2, accs2 = [], [], []
        for h in range(N_HEADS):
            s = lax.dot_general(qr[h], ks, NT, preferred_element_type=F32) * SCALE + mbias
            m_new = jnp.maximum(ms[h], jnp.max(s, axis=-1, keepdims=True))
            alpha = jnp.exp(ms[h] - m_new)
            p = jnp.exp(s - m_new)
            ls2.append(alpha * ls[h] + jnp.sum(p, axis=-1, keepdims=True))
            accs2.append(alpha * accs[h] + jnp.dot(p.astype(BF16), vs, preferred_element_type=F32))
            ms2.append(m_new)
        return tuple(ms2), tuple(ls2), tuple(accs2)

    init = (tuple(jnp.full((QB, 1), NEG, F32) for _ in range(N_HEADS)),
            tuple(jnp.zeros((QB, 1), F32) for _ in range(N_HEADS)),
            tuple(jnp.zeros((QB, HEAD_DIM), F32) for _ in range(N_HEADS)))
    n_tiles = (qs + QB + tk - 1) // tk
    _, ls, accs = lax.fori_loop(0, n_tiles, sel_step, init)
    o_sel = [accs[h] / ls[h] for h in range(N_HEADS)]

    wlen = WINDOW + QB
    w0 = pl.multiple_of(jnp.maximum(qs - WINDOW, 0), QB)
    wpos = w0 + lax.broadcasted_iota(I32, (1, wlen), 1)
    d = qpos - wpos
    wmask = (d >= 0) & (d < WINDOW)
    kw = win_ref[pl.ds(w0, wlen), 0:HEAD_DIM].astype(BF16)
    vw = win_ref[pl.ds(w0, wlen), HEAD_DIM:2 * HEAD_DIM].astype(BF16)
    sm = sm_ref[...]
    for h in range(N_HEADS):
        s = lax.dot_general(qr[h], kw, NT, preferred_element_type=F32) * SCALE
        s = jnp.where(wmask, s, NEG)
        e = jnp.exp(s - jnp.max(s, axis=-1, keepdims=True))
        p = e / jnp.sum(e, axis=-1, keepdims=True)
        o_win = jnp.dot(p.astype(BF16), vw, preferred_element_type=F32)
        c = SM_NG + 3 * h
        o_ref[:, h * HEAD_DIM:(h + 1) * HEAD_DIM] = (
            sm[:, c:c + 1] * o_cmp[h] + sm[:, c + 1:c + 2] * o_sel[h] + sm[:, c + 2:c + 3] * o_win)


def _nsa_prompt(nq, nqr, small, cmp, rows, win, B, S, tk):
    QB = 128
    nq_t = S // QB
    nb = S // CMP_LEN
    assert S % tk == 0 and S >= WINDOW + QB
    return pl.pallas_call(
        functools.partial(_nsa_prompt_kernel, tk=tk),
        grid=(B, nq_t),
        in_specs=[
            pl.BlockSpec((QB, D_NSA), lambda b, i: (b * nq_t + i, 0)),
            pl.BlockSpec((QB, D_NSA), lambda b, i: (b * nq_t + i, 0)),
            pl.BlockSpec((QB, LANES), lambda b, i: (b * nq_t + i, 0)),
            pl.BlockSpec((nb, LANES), lambda b, i: (b, 0)),
            pl.BlockSpec((S, 4 * HEAD_DIM), lambda b, i: (b, 0)),
            pl.BlockSpec((S, 2 * HEAD_DIM), lambda b, i: (b, 0)),
        ],
        out_specs=pl.BlockSpec((QB, D_NSA), lambda b, i: (b * nq_t + i, 0)),
        out_shape=jax.ShapeDtypeStruct((B * S, D_NSA), F32),
        compiler_params=_params(("parallel", "parallel")),
        name="nsa_prompt",
    )(nq, nqr, small, cmp, rows, win)


def _gla_prompt_kernel(qk_ref, v_ref, g_ref, o_ref, st_ref, s_sc):
    t = pl.program_id(1)
    nt = pl.num_programs(1)
    tc = qk_ref.shape[0]
    C = GLA_CHUNK

    @pl.when(t == 0)
    def _():
        s_sc[...] = jnp.zeros_like(s_sc)

    r = lax.broadcasted_iota(I32, (C, C), 0)
    c = lax.broadcasted_iota(I32, (C, C), 1)
    causal = c <= r
    low = causal.astype(F32)
    up = (r <= c).astype(F32)
    k_t = qk_ref[:, D_FOX:2 * D_FOX].T
    g_t = g_ref[...].T
    for ci in range(tc // C):
        rs = slice(ci * C, (ci + 1) * C)
        for h in range(N_HEADS):
            hs = slice(h * HEAD_DIM, (h + 1) * HEAD_DIM)
            q = qk_ref[rs, hs] * SCALE
            k = qk_ref[rs, D_FOX + h * HEAD_DIM:D_FOX + (h + 1) * HEAD_DIM]
            v = v_ref[rs, h * GLA_DV:(h + 1) * GLA_DV].astype(BF16)
            gcum = jnp.dot(low, g_ref[rs, hs], precision=HI, preferred_element_type=F32)
            qe = (q * jnp.exp(gcum)).astype(BF16)
            ke = (k * jnp.exp(-gcum)).astype(BF16)
            a = jnp.where(causal, lax.dot_general(qe, ke, NT, preferred_element_type=F32), 0.0)
            gcum_t = jnp.dot(g_t[hs, rs], up, precision=HI, preferred_element_type=F32)
            glast_t = gcum_t[:, C - 1:C]
            kd_t = (k_t[hs, rs] * jnp.exp(glast_t - gcum_t)).astype(BF16)
            state = s_sc[h]
            o_ref[rs, h * GLA_DV:(h + 1) * GLA_DV] = (
                jnp.dot(a.astype(BF16), v, preferred_element_type=F32)
                + jnp.dot(qe, state.astype(BF16), preferred_element_type=F32))
            s_sc[h] = jnp.exp(glast_t) * state + jnp.dot(kd_t, v, preferred_element_type=F32)

    @pl.when(t == nt - 1)
    def _():
        st_ref[...] = s_sc[...]


def _gla_prompt(gqk, gv, glog, B, S, tc):
    nt = S // tc
    return pl.pallas_call(
        _gla_prompt_kernel,
        grid=(B, nt),
        in_specs=[pl.BlockSpec((tc, 2 * D_FOX), lambda b, t: (b * nt + t, 0)),
                  pl.BlockSpec((tc, D_GLA), lambda b, t: (b * nt + t, 0)),
                  pl.BlockSpec((tc, D_FOX), lambda b, t: (b * nt + t, 0))],
        out_specs=[pl.BlockSpec((tc, D_GLA), lambda b, t: (b * nt + t, 0)),
                   pl.BlockSpec((None, N_HEADS, HEAD_DIM, GLA_DV), lambda b, t: (b, 0, 0, 0))],
        out_shape=[jax.ShapeDtypeStruct((B * S, D_GLA), F32),
                   jax.ShapeDtypeStruct((B, N_HEADS, HEAD_DIM, GLA_DV), F32)],
        scratch_shapes=[pltpu.VMEM((N_HEADS, HEAD_DIM, GLA_DV), F32)],
        compiler_params=_params(("parallel", "arbitrary")),
        name="gla_prompt",
    )(gqk, gv, glog)


def _outproj_kernel(x_ref, of_ref, on_ref, og_ref, gog_ref, gn_ref, w_ref, o_ref):
    acc = _bdot(of_ref[...], w_ref[0:D_FOX, :])
    acc = acc + _bdot(on_ref[...], w_ref[D_FOX:D_FOX + D_NSA, :])
    for h in range(N_HEADS):
        hs = slice(h * GLA_DV, (h + 1) * GLA_DV)
        z = _rms(og_ref[:, hs], gn_ref[...]) * _silu(gog_ref[:, hs])
        w0 = D_FOX + D_NSA + h * GLA_DV
        acc = acc + _bdot(z, w_ref[w0:w0 + GLA_DV, :])
    o_ref[...] = x_ref[...] + acc


def _out_projection(x, o_fox, o_nsa, o_gla, gog, gn, w_out, tm):
    T = x.shape[0]
    assert T % tm == 0
    row = lambda w: pl.BlockSpec((tm, w), lambda i: (i, 0))
    return pl.pallas_call(
        _outproj_kernel,
        grid=(T // tm,),
        in_specs=[row(D_MODEL), row(D_FOX), row(D_NSA), row(D_GLA), row(D_GLA),
                  pl.BlockSpec((1, GLA_DV), lambda i: (0, 0)),
                  pl.BlockSpec((D_MODEL, D_MODEL), lambda i: (0, 0))],
        out_specs=row(D_MODEL),
        out_shape=jax.ShapeDtypeStruct((T, D_MODEL), F32),
        compiler_params=_params(("parallel",)),
        name="out_projection",
    )(x, o_fox, o_nsa, o_gla, gog, gn, w_out)


def _dense_ffn_kernel(x_ref, g_ref, wg_ref, wu_ref, wd_ref, o_ref, h_sc, acc_sc):
    f = pl.program_id(1)
    nf = pl.num_programs(1)

    @pl.when(f == 0)
    def _():
        h_sc[...] = _rms(x_ref[...], g_ref[...]).astype(BF16)
        acc_sc[...] = jnp.zeros_like(acc_sc)

    h = h_sc[...]
    a = jnp.dot(h, wg_ref[...], preferred_element_type=F32)
    u = jnp.dot(h, wu_ref[...], preferred_element_type=F32)
    acc_sc[...] += _bdot(_silu(a) * u, wd_ref[...])

    @pl.when(f == nf - 1)
    def _():
        o_ref[...] = x_ref[...] + acc_sc[...]


def _dense_ffn(x, g, wg, wu, wd, tm, tf):
    T = x.shape[0]
    assert T % tm == 0 and D_FF % tf == 0
    return pl.pallas_call(
        _dense_ffn_kernel,
        grid=(T // tm, D_FF // tf),
        in_specs=[pl.BlockSpec((tm, D_MODEL), lambda i, f: (i, 0)),
                  pl.BlockSpec((1, D_MODEL), lambda i, f: (0, 0)),
                  pl.BlockSpec((D_MODEL, tf), lambda i, f: (0, f)),
                  pl.BlockSpec((D_MODEL, tf), lambda i, f: (0, f)),
                  pl.BlockSpec((tf, D_MODEL), lambda i, f: (f, 0))],
        out_specs=pl.BlockSpec((tm, D_MODEL), lambda i, f: (i, 0)),
        out_shape=jax.ShapeDtypeStruct((T, D_MODEL), F32),
        scratch_shapes=[pltpu.VMEM((tm, D_MODEL), BF16), pltpu.VMEM((tm, D_MODEL), F32)],
        compiler_params=_params(("parallel", "arbitrary")),
        name="dense_ffn",
    )(x, g, wg, wu, wd)


def _router_kernel(x_ref, g_ref, wr_ref, h_ref, r_ref):
    h = _rms(x_ref[...], g_ref[...])
    h_ref[...] = h
    logits = jnp.dot(h, wr_ref[...], precision=HI, preferred_element_type=F32)
    lane = lax.broadcasted_iota(I32, logits.shape, 1)
    lg = jnp.where(lane < N_EXPERTS, logits, -jnp.inf)
    m1 = jnp.max(lg, axis=-1, keepdims=True)
    i1 = jnp.min(jnp.where(lg == m1, lane, LANES), axis=-1, keepdims=True)
    lg2 = jnp.where(lane == i1, -jnp.inf, lg)
    m2 = jnp.max(lg2, axis=-1, keepdims=True)
    i2 = jnp.min(jnp.where(lg2 == m2, lane, LANES), axis=-1, keepdims=True)
    e = jnp.exp(m2 - m1)
    den = 1.0 + e
    r_ref[...] = jnp.where(lane == 0, i1.astype(F32),
                           jnp.where(lane == 1, i2.astype(F32),
                                     jnp.where(lane == 2, 1.0 / den,
                                               jnp.where(lane == 3, e / den, 0.0))))


def _router(x, g, wr_pad, tm):
    T = x.shape[0]
    assert T % tm == 0
    return pl.pallas_call(
        _router_kernel,
        grid=(T // tm,),
        in_specs=[pl.BlockSpec((tm, D_MODEL), lambda i: (i, 0)),
                  pl.BlockSpec((1, D_MODEL), lambda i: (0, 0)),
                  pl.BlockSpec((D_MODEL, LANES), lambda i: (0, 0))],
        out_specs=[pl.BlockSpec((tm, D_MODEL), lambda i: (i, 0)),
                   pl.BlockSpec((tm, LANES), lambda i: (i, 0))],
        out_shape=[jax.ShapeDtypeStruct((T, D_MODEL), F32),
                   jax.ShapeDtypeStruct((T, LANES), F32)],
        compiler_params=_params(("parallel",)),
        name="moe_router",
    )(x, g, wr_pad)


GATHER_WINDOW = 32
SC_WORKERS = 32


def _row_gather(src, idx):
    n = idx.shape[0]
    step = GATHER_WINDOW * SC_WORKERS
    n_pad = -(-n // step) * step
    if n_pad != n:
        idx = jnp.concatenate([idx, jnp.zeros((n_pad - n,), idx.dtype)])
    width = src.shape[1]
    per_worker = n_pad // SC_WORKERS
    mesh = plsc.VectorSubcoreMesh(core_axis_name="core", subcore_axis_name="subcore")

    @functools.partial(pl.kernel, out_type=jax.ShapeDtypeStruct((n_pad, width), src.dtype), mesh=mesh,
                       scratch_types=[pltpu.VMEM((per_worker,), I32),
                                      pltpu.VMEM((GATHER_WINDOW, width), src.dtype)],
                       name="row_gather")
    def gather(src_hbm, idx_hbm, dst_hbm, idx_v, buf):
        worker = lax.axis_index("core") * (SC_WORKERS // 2) + lax.axis_index("subcore")
        base = worker * per_worker
        pltpu.sync_copy(idx_hbm.at[pl.ds(base, per_worker)], idx_v)

        @pl.loop(0, per_worker // GATHER_WINDOW)
        def _(j):
            pltpu.sync_copy(src_hbm.at[idx_v.at[pl.ds(j * GATHER_WINDOW, GATHER_WINDOW)]], buf)
            pltpu.sync_copy(buf, dst_hbm.at[pl.ds(base + j * GATHER_WINDOW, GATHER_WINDOW)])

    return gather(src, idx)


def _moe_ffn_kernel(be_ref, nu_ref, x_ref, wg_ref, wu_ref, wd_ref, o_ref, acc_sc):
    b = pl.program_id(0)
    f = pl.program_id(1)
    nf = pl.num_programs(1)
    used = b < nu_ref[0]

    @pl.when(used)
    def _():
        @pl.when(f == 0)
        def _():
            acc_sc[...] = jnp.zeros_like(acc_sc)

        x = x_ref[...].astype(BF16)
        a = jnp.dot(x, wg_ref[...].astype(BF16), preferred_element_type=F32)
        u = jnp.dot(x, wu_ref[...].astype(BF16), preferred_element_type=F32)
        acc_sc[...] += _bdot(_silu(a) * u, wd_ref[...])

        @pl.when(f == nf - 1)
        def _():
            o_ref[...] = acc_sc[...]

    @pl.when(jnp.logical_not(used) & (f == nf - 1))
    def _():
        o_ref[...] = jnp.zeros_like(o_ref)


def _moe_ffn(xb, block_e, n_used, wg, wu, wd, blk, tf):
    cap = xb.shape[0]
    nb = cap // blk
    nf = D_FF // tf

    def bsel(b, nu):
        return jnp.minimum(b, nu[0] - 1)

    def fsel(b, f, nu):
        return jnp.where(b < nu[0], f, nf - 1)

    return pl.pallas_call(
        _moe_ffn_kernel,
        grid_spec=pltpu.PrefetchScalarGridSpec(
            num_scalar_prefetch=2,
            grid=(nb, nf),
            in_specs=[
                pl.BlockSpec((blk, D_MODEL), lambda b, f, be, nu: (bsel(b, nu), 0)),
                pl.BlockSpec((None, D_MODEL, tf), lambda b, f, be, nu: (be[bsel(b, nu)], 0, fsel(b, f, nu))),
                pl.BlockSpec((None, D_MODEL, tf), lambda b, f, be, nu: (be[bsel(b, nu)], 0, fsel(b, f, nu))),
                pl.BlockSpec((None, tf, D_MODEL), lambda b, f, be, nu: (be[bsel(b, nu)], fsel(b, f, nu), 0)),
            ],
            out_specs=pl.BlockSpec((blk, D_MODEL), lambda b, f, be, nu: (b, 0)),
            scratch_shapes=[pltpu.VMEM((blk, D_MODEL), F32)],
        ),
        out_shape=jax.ShapeDtypeStruct((cap, D_MODEL), F32),
        compiler_params=_params(("arbitrary", "arbitrary")),
        name="moe_ffn",
    )(block_e, n_used, xb, wg, wu, wd)


def _moe_combine_kernel(x_ref, y1_ref, y2_ref, r_ref, o_ref):
    r = r_ref[...]
    o_ref[...] = x_ref[...] + (r[:, 2:3] * y1_ref[...] + r[:, 3:4] * y2_ref[...])


def _moe_combine(x, y1, y2, route, tm, row0):
    T = x.shape[0]
    nt = T // tm
    assert row0 % tm == 0
    b0 = row0 // tm
    return pl.pallas_call(
        _moe_combine_kernel,
        grid=(nt,),
        in_specs=[pl.BlockSpec((tm, D_MODEL), lambda i: (i, 0)),
                  pl.BlockSpec((tm, D_MODEL), lambda i: (b0 + i, 0)),
                  pl.BlockSpec((tm, D_MODEL), lambda i: (b0 + i, 0)),
                  pl.BlockSpec((tm, LANES), lambda i: (i, 0))],
        out_specs=pl.BlockSpec((tm, D_MODEL), lambda i: (i, 0)),
        out_shape=jax.ShapeDtypeStruct((T, D_MODEL), F32),
        compiler_params=_params(("parallel",)),
        name="moe_combine",
    )(x, y1, y2, route)


def _moe_plan(e_top, blk):
    T = e_top.shape[0]
    n = 2 * T
    flat_e = e_top.reshape(-1)
    onehot = (flat_e[:, None] == jnp.arange(N_EXPERTS, dtype=I32)[None, :]).astype(I32)
    csum = jnp.cumsum(onehot, axis=0)
    rank = jnp.sum((csum - onehot) * onehot, axis=1)
    counts = csum[-1]
    padded = (counts + blk - 1) // blk * blk
    ends = jnp.cumsum(padded)
    pstart = ends - padded
    dest = (pstart[flat_e] + rank).astype(I32)
    n_blocks = -(-n // blk) + N_EXPERTS
    cap = n_blocks * blk
    slot_tok = jnp.zeros((cap,), I32).at[dest].set(jnp.arange(n, dtype=I32) // 2)
    first = jnp.arange(n_blocks, dtype=I32) * blk
    block_e = jnp.minimum(jnp.sum((ends[None, :] <= first[:, None]).astype(I32), axis=1), N_EXPERTS - 1)
    n_used = (ends[-1] // blk).astype(I32).reshape(1)
    return dest, slot_tok, block_e, n_used


def _moe_layer(xs, g, wr_pad, wg, wu, wd, tms, blk, tf):
    routed = [_router(x, g, wr_pad, tm) for x, tm in zip(xs, tms)]
    h = jnp.concatenate([r[0] for r in routed], axis=0) if len(xs) > 1 else routed[0][0]
    e_top = jnp.concatenate([r[1][:, 0:2] for r in routed], axis=0).astype(I32)
    dest, slot_tok, block_e, n_used = _moe_plan(e_top, blk)
    xb = _row_gather(h, slot_tok)
    yb = _moe_ffn(xb, block_e, n_used, wg, wu, wd, blk, tf)
    d2 = dest.reshape(-1, 2)
    y1 = _row_gather(yb, d2[:, 0])
    y2 = _row_gather(yb, d2[:, 1])
    out, row0 = [], 0
    for x, tm, r in zip(xs, tms, routed):
        out.append(_moe_combine(x, y1, y2, r[1], tm, row0))
        row0 += x.shape[0]
    return out


def _norm_kernel(x_ref, g_ref, o_ref):
    o_ref[...] = _rms(x_ref[...], g_ref[...])


def _final_norm(x, g, tm):
    T = x.shape[0]
    return pl.pallas_call(
        _norm_kernel,
        grid=(T // tm,),
        in_specs=[pl.BlockSpec((tm, D_MODEL), lambda i: (i, 0)),
                  pl.BlockSpec((1, D_MODEL), lambda i: (0, 0))],
        out_specs=pl.BlockSpec((tm, D_MODEL), lambda i: (i, 0)),
        out_shape=jax.ShapeDtypeStruct((T, D_MODEL), F32),
        compiler_params=_params(("parallel",)),
        name="final_norm",
    )(x, g)


_IN_SPLITS = (D_FOX, D_FOX, D_FOX, N_HEADS, D_NSA, 6 * HEAD_DIM, 3 * N_HEADS,
              D_FOX, D_FOX, D_GLA, GLA_RANK, D_GLA)


def _reorder_w_in(w):
    offs = [0]
    for s in _IN_SPLITS:
        offs.append(offs[-1] + s)
    seg = lambda k: w[:, offs[k]:offs[k + 1]]
    fq, fk, fv, ff, nq, nkv, ng, gq, gk, gv, glr, gog = [seg(k) for k in range(12)]
    pad = jnp.zeros((w.shape[0], LANES - SM_GLR - GLA_RANK), w.dtype)
    return jnp.concatenate([fq, fk, fv, nq, nkv, gq, gk, gv, gog, ff, ng, glr, pad], axis=1).astype(BF16)


def _rope_table(pos):
    inv = ROPE_THETA ** (-jnp.arange(ROPE_HALF, dtype=F32) / ROPE_HALF)
    ang = pos.astype(F32)[:, None] * inv[None, :]
    cos, sin = jnp.cos(ang), jnp.sin(ang)
    P = pos.shape[0]
    one = jnp.ones((P, HEAD_DIM - ROPE_DIM), F32)
    zero = jnp.zeros((P, HEAD_DIM - ROPE_DIM), F32)
    z8 = jnp.zeros((P, ROPE_HALF), F32)
    a64 = jnp.concatenate([cos, cos, one], axis=1)
    p64 = jnp.concatenate([z8, sin, zero], axis=1)
    m64 = jnp.concatenate([-sin, z8, zero], axis=1)
    i64 = jnp.ones((P, HEAD_DIM), F32)
    o64 = jnp.zeros((P, HEAD_DIM), F32)
    return jnp.concatenate([a64, a64, p64, p64, m64, m64, a64, i64, p64, o64, m64, o64], axis=1)


def _layer_mix_params(l, norm_mix_g, w_in, b_fox_f, w_cmp, pe_cmp, w_gla_gk, b_gla_gk, g_gla_norm, w_out):
    sb = jnp.zeros((1, LANES), F32).at[0, SM_FF:SM_FF + N_HEADS].set(b_fox_f[l])
    wgk = jnp.zeros((LANES, D_FOX), F32).at[SM_GLR:SM_GLR + GLA_RANK].set(w_gla_gk[l]).astype(BF16)
    return dict(g=norm_mix_g[l].reshape(1, D_MODEL), w_r=_reorder_w_in(w_in[l]), sb=sb, wgk=wgk,
                bgk=b_gla_gk[l].reshape(1, D_FOX), w_cmp=w_cmp[l],
                pe_cmp=pe_cmp[l].reshape(2, 1, CMP_LEN * HEAD_DIM),
                gn=g_gla_norm[l].reshape(1, GLA_DV), w_out=w_out[l].astype(BF16))


def _mix_prompt(x, B, S, p, tab, tm, t_fox, tk_sel, tc_gla):
    (fq, fkv, nq, nqr, rows, win, gqk, gv, gog, glog, small) = _in_projection(
        x, 0, B * S, tm, p['g'], p['w_r'], p['sb'], p['wgk'], p['bgk'], tab, S)
    cc, cr = _fox_cumsum(small, B, S, min(S, 512))
    o_fox = _fox_prompt(fq, fkv, cc, cr, B, S, t_fox)
    n_blk = B * S // CMP_LEN
    blocks = _block_major(rows[:, 0:2 * HEAD_DIM].reshape(B * S, 2, HEAD_DIM), n_blk)
    cmp = _compress(blocks, p['pe_cmp'], p['w_cmp'], min(256, n_blk), True)
    o_nsa = _nsa_prompt(nq, nqr, small, cmp, rows, win, B, S, tk_sel)
    o_gla, g_state = _gla_prompt(gqk, gv, glog, B, S, tc_gla)
    x_new = _out_projection(x, o_fox, o_nsa, o_gla, gog, p['gn'], p['w_out'], tm)
    return x_new, dict(fkv=fkv, small=small, rows=rows, win=win, g_state=g_state)


def _per_head_col(vals):
    r = lax.broadcasted_iota(I32, (SUBLANES, 1), 0)
    out = jnp.zeros((SUBLANES, 1), F32)
    for h, v in enumerate(vals):
        out = out + jnp.where(r == h, v, 0.0)
    return out


def _fox_decode_kernel(pt_ref, q_ref, kvn_ref, smn_ref, *refs, n_pages):
    del pt_ref
    kv_refs = refs[0:n_pages]
    lf_refs = refs[n_pages:2 * n_pages]
    o_ref, lf_sc, s_sc = refs[2 * n_pages:]
    R = n_pages * SUBLANES
    PG = kv_refs[0].shape[-1]

    lf_sc[...] = jnp.zeros_like(lf_sc)
    for p in range(n_pages):
        lf_sc[p * SUBLANES:p * SUBLANES + N_HEADS, :] = lf_refs[p][...]
    lft = lf_sc[...]
    k0 = lax.broadcasted_iota(I32, (PG, PG), 0)
    k1 = lax.broadcasted_iota(I32, (PG, PG), 1)
    within = jnp.dot(lft, (k0 > k1).astype(F32), precision=HI, preferred_element_type=F32)
    tot = jnp.broadcast_to(jnp.sum(lft, axis=1, keepdims=True), (R, PG))
    r0 = lax.broadcasted_iota(I32, (R, R), 0)
    r1 = lax.broadcasted_iota(I32, (R, R), 1)
    later = ((r1 % SUBLANES == r0 % SUBLANES) & (r1 // SUBLANES > r0 // SUBLANES)).astype(F32)
    cross = jnp.dot(later, tot, precision=HI, preferred_element_type=F32)
    rr = lax.broadcasted_iota(I32, (R, 1), 0) % SUBLANES
    smn = smn_ref[...]
    newcol = jnp.zeros((R, 1), F32)
    for h in range(N_HEADS):
        newcol = newcol + jnp.where(rr == h, smn[:, SM_FF + h:SM_FF + h + 1], 0.0)
    bias = (within + cross + newcol).reshape(n_pages, SUBLANES, PG)

    s_sc[...] = jnp.zeros_like(s_sc)
    for p in range(n_pages):
        for h in range(N_HEADS):
            s_sc[p, h:h + 1, :] = jnp.sum(kv_refs[p][0, h] * q_ref[h], axis=0, keepdims=True)
    s3 = s_sc[...] * SCALE + bias
    s_new = _per_head_col([jnp.sum(q_ref[h] * kvn_ref[0, h], axis=0, keepdims=True)
                           for h in range(N_HEADS)]) * SCALE
    m = jnp.max(jnp.max(s3, axis=2, keepdims=True), axis=0)
    m = jnp.maximum(m, s_new)
    p3 = jnp.exp(s3 - m[None])
    pn = jnp.exp(s_new - m)
    den = jnp.sum(jnp.sum(p3, axis=2, keepdims=True), axis=0) + pn
    for h in range(N_HEADS):
        acc = jnp.zeros((HEAD_DIM, PG), F32)
        for p in range(n_pages):
            acc = acc + kv_refs[p][1, h] * p3[p, h:h + 1, :]
        o_ref[h] = ((jnp.sum(acc, axis=1, keepdims=True) + pn[h:h + 1] * kvn_ref[1, h])
                    / den[h:h + 1])


def _fox_decode(l, pt_flat, n_pages, fq, fkv, small, kv_cache_t, lft_cache):
    DB = fq.shape[0]
    PG = kv_cache_t.shape[-1]
    page = lambda p, nz: (lambda b, pt: (l, pt[b * n_pages + p]) + (0,) * nz)
    per_b = lambda *s: pl.BlockSpec((None,) + s, lambda b, pt: (b,) + (0,) * len(s))
    return pl.pallas_call(
        functools.partial(_fox_decode_kernel, n_pages=n_pages),
        grid_spec=pltpu.PrefetchScalarGridSpec(
            num_scalar_prefetch=1,
            grid=(DB,),
            in_specs=[per_b(N_HEADS, HEAD_DIM, 1), per_b(2, N_HEADS, HEAD_DIM, 1), per_b(1, LANES)]
            + [pl.BlockSpec((None, None, 2, N_HEADS, HEAD_DIM, PG), page(p, 4)) for p in range(n_pages)]
            + [pl.BlockSpec((None, None, N_HEADS, PG), page(p, 2)) for p in range(n_pages)],
            out_specs=per_b(N_HEADS, HEAD_DIM, 1),
            scratch_shapes=[pltpu.VMEM((n_pages * SUBLANES, PG), F32),
                            pltpu.VMEM((n_pages, SUBLANES, PG), F32)],
        ),
        out_shape=jax.ShapeDtypeStruct((DB, N_HEADS, HEAD_DIM, 1), F32),
        compiler_params=_params(("arbitrary",)),
        name="fox_decode",
    )(pt_flat, fq.reshape(DB, N_HEADS, HEAD_DIM, 1), fkv.reshape(DB, 2, N_HEADS, HEAD_DIM, 1),
      small.reshape(DB, 1, LANES), *([kv_cache_t] * n_pages), *([lft_cache] * n_pages)).reshape(DB, D_FOX)


def _nsa_decode_kernel(pt_ref, q_ref, qr_ref, rown_ref, winn_ref, gate_ref, win_ref, *refs,
                       n_pages, past_len):
    del pt_ref
    pg_refs = refs[0:n_pages]
    cmp_refs = refs[n_pages:2 * n_pages]
    o_ref, nw_ref, cmp_sc, qc_sc, s_sc, sw_sc = refs[2 * n_pages:]
    R = n_pages * SUBLANES
    PG = pg_refs[0].shape[-1]
    WB = win_ref.shape[-1]
    per_page = PG // CMP_LEN
    assert per_page <= SUBLANES and PG == 2 * SEL_LEN and R == LANES
    jt = past_len // SEL_LEN

    qc_sc[...] = jnp.zeros_like(qc_sc)
    qc_sc[0:N_HEADS, 0:HEAD_DIM] = q_ref[...]
    head_row = lax.broadcasted_iota(I32, (SUBLANES, 1), 0) < N_HEADS

    cmp_sc[...] = jnp.zeros_like(cmp_sc)
    for p in range(n_pages):
        cmp_sc[p * SUBLANES:p * SUBLANES + per_page, :] = cmp_refs[p][...]
    cmpa = cmp_sc[...]
    lane = lax.broadcasted_iota(I32, (1, R), 1)
    blk = per_page * (lane // SUBLANES) + lane % SUBLANES
    complete = (lane % SUBLANES < per_page) & ((blk + 1) * CMP_LEN - 1 <= past_len)
    s = lax.dot_general(qc_sc[...], cmpa, NT, precision=HI, preferred_element_type=F32) * SCALE
    s = jnp.where(complete, s, NEG)
    e = jnp.exp(s - jnp.max(s, axis=-1, keepdims=True))
    pc = e / jnp.sum(e, axis=-1, keepdims=True) * complete.astype(F32)
    vcb_t = cmpa.T[HEAD_DIM:2 * HEAD_DIM, :]
    o_cmp = [jnp.sum(vcb_t * pc[h:h + 1, :], axis=1, keepdims=True) for h in range(N_HEADS)]

    imp_c = jnp.sum(jnp.where(head_row, pc, 0.0), axis=0, keepdims=True)
    imp_s = imp_c + pltpu.roll(imp_c, R - 1, 1)
    cand = (lane % SUBLANES == 0) | (lane % SUBLANES == 2)
    jsel = 2 * (lane // SUBLANES) + (lane % SUBLANES) // 2
    score = jnp.where(jsel == jt, 2.0 * SEL_FORCE,
                      jnp.where((jsel == 0) | (jsel == jt - 1), SEL_FORCE,
                                jnp.where(jsel <= jt, imp_s + 0.0, -1.0)))
    score_b = jnp.broadcast_to(score, (R, R))
    key_row = _order_key(score_b)
    key_col = _order_key(score_b.T)
    l0 = lax.broadcasted_iota(I32, (R, R), 0)
    l1 = lax.broadcasted_iota(I32, (R, R), 1)
    cand_col = (l0 % SUBLANES == 0) | (l0 % SUBLANES == 2)
    beats = cand_col & (key_col > jnp.where(l0 < l1, key_row - 1, key_row))
    cnt = jnp.sum(beats.astype(I32), axis=0, keepdims=True)
    sel_row = (cand & (cnt < TOP_N - 1)).astype(F32)
    sel_col = jnp.broadcast_to(sel_row, (R, R)).T
    half = ((l0 % SUBLANES == 0) & (l1 < SEL_LEN)) | ((l0 % SUBLANES == 2) & (l1 >= SEL_LEN))
    z = jnp.where(half, sel_col, 0.0)
    same_page = (l1 // SUBLANES == l0 // SUBLANES).astype(BF16)
    picked = jnp.dot(same_page, z.astype(BF16), preferred_element_type=F32)
    picked = picked.reshape(n_pages, SUBLANES, PG) > 0.5

    s_sc[...] = jnp.zeros_like(s_sc)
    for p in range(n_pages):
        ks_t = pg_refs[p][2]
        for h in range(N_HEADS):
            s_sc[p, h:h + 1, :] = jnp.sum(ks_t * qr_ref[h], axis=0, keepdims=True)
    s3 = jnp.where(picked, s_sc[...] * SCALE, NEG)
    s_new = _per_head_col([jnp.sum(qr_ref[h] * rown_ref[2], axis=0, keepdims=True)
                           for h in range(N_HEADS)]) * SCALE
    m = jnp.maximum(jnp.max(jnp.max(s3, axis=2, keepdims=True), axis=0), s_new)
    p3 = jnp.exp(s3 - m[None])
    pn = jnp.exp(s_new - m)
    den = jnp.sum(jnp.sum(p3, axis=2, keepdims=True), axis=0) + pn
    o_sel = []
    for h in range(N_HEADS):
        acc = jnp.zeros((HEAD_DIM, PG), F32)
        for p in range(n_pages):
            acc = acc + pg_refs[p][3] * p3[p, h:h + 1, :]
        o_sel.append((jnp.sum(acc, axis=1, keepdims=True) + pn[h:h + 1] * rown_ref[3]) / den[h:h + 1])

    kw_t = win_ref[0]
    vw_t = win_ref[1]
    wlane = lax.broadcasted_iota(I32, (1, WB), 1)
    wpos = past_len - WB + wlane
    wd = past_len - wpos
    wok = (wd >= 0) & (wd < WINDOW) & (wpos >= 0)
    sw_sc[...] = jnp.zeros_like(sw_sc)
    for h in range(N_HEADS):
        sw_sc[h:h + 1, :] = jnp.sum(kw_t * qr_ref[h], axis=0, keepdims=True)
    sw = jnp.where(wok, sw_sc[...] * SCALE, NEG)
    sw_new = _per_head_col([jnp.sum(qr_ref[h] * winn_ref[0], axis=0, keepdims=True)
                            for h in range(N_HEADS)]) * SCALE
    mw = jnp.maximum(jnp.max(sw, axis=-1, keepdims=True), sw_new)
    ew = jnp.exp(sw - mw)
    en = jnp.exp(sw_new - mw)
    denw = jnp.sum(ew, axis=-1, keepdims=True) + en

    g = gate_ref[...]
    for h in range(N_HEADS):
        o_win = ((jnp.sum(vw_t * ew[h:h + 1, :], axis=1, keepdims=True) + en[h:h + 1] * winn_ref[1])
                 / denw[h:h + 1])
        o_ref[h] = (g[h:h + 1, 0:1] * o_cmp[h] + g[h:h + 1, 1:2] * o_sel[h] + g[h:h + 1, 2:3] * o_win)
    last = lax.broadcasted_iota(I32, (HEAD_DIM, WB), 1) == WB - 1
    for s in range(2):
        nw_ref[s] = jnp.where(last, winn_ref[s], pltpu.roll(win_ref[s], WB - 1, 1))


def _nsa_decode(l, pt_flat, n_pages, past_len, nq, nqr, rows, win, small, nsa_cache_t, cmp_pool, win_state_t):
    DB = nq.shape[0]
    PG = nsa_cache_t.shape[-1]
    WB = win_state_t.shape[-1]
    page = lambda p: (lambda b, pt: (l, pt[b * n_pages + p], 0, 0, 0))
    cpage = lambda p: (lambda b, pt: (pt[b * n_pages + p], 0, 0))
    per_b = lambda *s: pl.BlockSpec((None,) + s, lambda b, pt: (b,) + (0,) * len(s))
    gates = small[:, SM_NG:SM_NG + 3 * N_HEADS].reshape(DB, N_HEADS, 3)
    o, nw = pl.pallas_call(
        functools.partial(_nsa_decode_kernel, n_pages=n_pages, past_len=past_len),
        grid_spec=pltpu.PrefetchScalarGridSpec(
            num_scalar_prefetch=1,
            grid=(DB,),
            in_specs=[per_b(N_HEADS, HEAD_DIM), per_b(N_HEADS, HEAD_DIM, 1), per_b(4, HEAD_DIM, 1),
                      per_b(2, HEAD_DIM, 1), per_b(N_HEADS, 3),
                      pl.BlockSpec((None, None, 2, HEAD_DIM, WB), lambda b, pt: (l, b, 0, 0, 0))]
            + [pl.BlockSpec((None, None, 4, HEAD_DIM, PG), page(p)) for p in range(n_pages)]
            + [pl.BlockSpec((None, PG // CMP_LEN, LANES), cpage(p)) for p in range(n_pages)],
            out_specs=[per_b(N_HEADS, HEAD_DIM, 1), per_b(2, HEAD_DIM, WB)],
            scratch_shapes=[pltpu.VMEM((n_pages * SUBLANES, LANES), F32),
                            pltpu.VMEM((SUBLANES, LANES), F32),
                            pltpu.VMEM((n_pages, SUBLANES, PG), F32),
                            pltpu.VMEM((SUBLANES, WB), F32)],
        ),
        out_shape=[jax.ShapeDtypeStruct((DB, N_HEADS, HEAD_DIM, 1), F32),
                   jax.ShapeDtypeStruct((DB, 2, HEAD_DIM, WB), F32)],
        compiler_params=_params(("arbitrary",)),
        name="nsa_decode",
    )(pt_flat, nq.reshape(DB, N_HEADS, HEAD_DIM), nqr.reshape(DB, N_HEADS, HEAD_DIM, 1),
      rows.reshape(DB, 4, HEAD_DIM, 1), win.reshape(DB, 2, HEAD_DIM, 1), gates, win_state_t,
      *([nsa_cache_t] * n_pages), *([cmp_pool] * n_pages))
    return o.reshape(DB, D_NSA), nw


def _gla_decode_kernel(q_ref, k_ref, g_ref, v_ref, s_ref, o_ref, so_ref):
    s_new = jnp.exp(g_ref[...]) * s_ref[...] + k_ref[...] * v_ref[...]
    so_ref[...] = s_new
    o_ref[...] = jnp.sum((q_ref[...] * SCALE) * s_new, axis=2, keepdims=True)


def _gla_decode(l, gqk, gv, glog, state, nb):
    DB = gqk.shape[0]
    col = lambda a: a.reshape(DB, N_HEADS, HEAD_DIM, 1)
    cspec = pl.BlockSpec((nb, N_HEADS, HEAD_DIM, 1), lambda i: (i, 0, 0, 0))
    vspec = pl.BlockSpec((nb, N_HEADS, 1, GLA_DV), lambda i: (i, 0, 0, 0))
    sspec = pl.BlockSpec((nb, N_HEADS, HEAD_DIM, GLA_DV), lambda i: (i, 0, 0, 0))
    o, s_new = pl.pallas_call(
        _gla_decode_kernel,
        grid=(DB // nb,),
        in_specs=[cspec, cspec, cspec, vspec,
                  pl.BlockSpec((None, nb, N_HEADS, HEAD_DIM, GLA_DV), lambda i: (l, i, 0, 0, 0))],
        out_specs=[vspec, sspec],
        out_shape=[jax.ShapeDtypeStruct((DB, N_HEADS, 1, GLA_DV), F32),
                   jax.ShapeDtypeStruct((DB, N_HEADS, HEAD_DIM, GLA_DV), F32)],
        compiler_params=_params(("parallel",)),
        name="gla_decode",
    )(col(gqk[:, 0:D_FOX]), col(gqk[:, D_FOX:2 * D_FOX]), col(glog),
      gv.reshape(DB, N_HEADS, 1, GLA_DV), state)
    return o.reshape(DB, D_GLA), s_new


def _mix_sample(x, l, p, tab, pt_flat, n_pages, past_len, fox_kv_t, fox_lft_c, nsa_t, cmp_blocks,
                win_state_t, gla_state):
    DB = x.shape[0]
    (fq, fkv, nq, nqr, rows, win, gqk, gv, gog, glog, small) = _in_projection(
        x, 0, DB, DB, p['g'], p['w_r'], p['sb'], p['wgk'], p['bgk'], tab, DB)
    o_fox = _fox_decode(l, pt_flat, n_pages, fq, fkv, small, fox_kv_t, fox_lft_c)
    n_pool, PG = nsa_t.shape[1], nsa_t.shape[-1]
    per_layer = n_pool * (PG // CMP_LEN)
    cmp_pool = _compress(cmp_blocks, p['pe_cmp'], p['w_cmp'], 256, False, row0=l * per_layer, n_rows=per_layer)
    cmp_pool = cmp_pool.reshape(n_pool, PG // CMP_LEN, LANES)
    o_nsa, new_win = _nsa_decode(l, pt_flat, n_pages, past_len, nq, nqr, rows, win, small,
                                 nsa_t, cmp_pool, win_state_t)
    o_gla, g_state = _gla_decode(l, gqk, gv, glog, gla_state, 8)
    x_new = _out_projection(x, o_fox, o_nsa, o_gla, gog, p['gn'], p['w_out'], DB)
    return x_new, dict(fkv=fkv, small=small, rows=rows, win=new_win, g_state=g_state)


def kernel(x_prompt, x_sample, cache_fox_kv, cache_fox_logf, cache_nsa_kv, state_nsa_win, state_gla,
           page_table, norm_mix_g, w_in, b_fox_f, w_cmp, pe_cmp, w_gla_gk, b_gla_gk, g_gla_norm, w_out,
           norm_ffn_g, dense_w_gate, dense_w_up, dense_w_down, moe_w_router, moe_w_gate, moe_w_up,
           moe_w_down, final_norm_g):
    B, S, _ = x_prompt.shape
    DB, TN, _ = x_sample.shape
    assert TN == 1
    depth, n_pool, PG = cache_fox_kv.shape[0:3]
    n_pages = page_table.shape[1]
    past_len = n_pages * PG
    WB = state_nsa_win.shape[2]
    xp = x_prompt.reshape(B * S, D_MODEL)
    xs = x_sample.reshape(DB, D_MODEL)
    tab_p = _rope_table(jnp.arange(S))
    tab_s = _rope_table(jnp.full((DB,), past_len, I32))
    pt_flat = page_table.reshape(-1).astype(I32)
    fox_kv_t = jnp.transpose(cache_fox_kv, (0, 1, 3, 4, 5, 2))
    fox_lft_c = jnp.swapaxes(cache_fox_logf, 2, 3)
    nsa_t = jnp.transpose(cache_nsa_kv, (0, 1, 3, 4, 2))
    win_state_t = jnp.transpose(state_nsa_win, (0, 1, 3, 4, 2))
    n_cmp = depth * n_pool * (PG // CMP_LEN)
    cmp_blocks = _block_major(cache_nsa_kv[:, :, :, 0:2, :].reshape(n_cmp * CMP_LEN, 2, HEAD_DIM), n_cmp)
    cp, cs = [], []
    for l in range(depth):
        p = _layer_mix_params(l, norm_mix_g, w_in, b_fox_f, w_cmp, pe_cmp, w_gla_gk, b_gla_gk,
                              g_gla_norm, w_out)
        xp, c = _mix_prompt(xp, B, S, p, tab_p, 512, 512, 512, 256)
        cp.append(c)
        xs, c = _mix_sample(xs, l, p, tab_s, pt_flat, n_pages, past_len, fox_kv_t, fox_lft_c, nsa_t,
                            cmp_blocks, win_state_t, state_gla)
        cs.append(c)
        gf = norm_ffn_g[l].reshape(1, D_MODEL)
        i = l // 2
        if l % 2 == 0:
            wg, wu, wd = (dense_w_gate[i].astype(BF16), dense_w_up[i].astype(BF16),
                          dense_w_down[i].astype(BF16))
            xp = _dense_ffn(xp, gf, wg, wu, wd, 1024, 512)
            xs = _dense_ffn(xs, gf, wg, wu, wd, DB, 512)
        else:
            wr = jnp.zeros((D_MODEL, LANES), F32).at[:, 0:N_EXPERTS].set(moe_w_router[i])
            xp, xs = _moe_layer([xp, xs], gf, wr, moe_w_gate[i], moe_w_up[i], moe_w_down[i],
                                [512, DB], 1024, 512)
    gfin = final_norm_g.reshape(1, D_MODEL)
    y_p = _final_norm(xp, gfin, 512).reshape(B, S, D_MODEL)
    y_s = _final_norm(xs, gfin, DB).reshape(DB, 1, D_MODEL)
    wp = min(WINDOW, S)
    st = lambda key, group: jnp.stack([c[key] for c in group])
    return (y_p, y_s,
            st('fkv', cp).reshape(depth, B, S, 2, N_HEADS, HEAD_DIM),
            st('small', cp)[:, :, SM_FF:SM_FF + N_HEADS].reshape(depth, B, S, N_HEADS),
            st('rows', cp).reshape(depth, B, S, 4, HEAD_DIM),
            st('win', cp).reshape(depth, B, S, 2, HEAD_DIM)[:, :, S - wp:],
            st('g_state', cp),
            st('fkv', cs).reshape(depth, DB, 1, 2, N_HEADS, HEAD_DIM),
            st('small', cs)[:, :, SM_FF:SM_FF + N_HEADS].reshape(depth, DB, 1, N_HEADS),
            st('rows', cs).reshape(depth, DB, 1, 4, HEAD_DIM),
            jnp.transpose(st('win', cs), (0, 1, 4, 2, 3)),
            st('g_state', cs))
```

```python
import functools

import jax
import jax.numpy as jnp
from jax import lax
from jax.experimental import pallas as pl
from jax.experimental.pallas import tpu as pltpu
from jax.experimental.pallas import tpu_sc as plsc

F32 = jnp.float32
BF16 = jnp.bfloat16
I32 = jnp.int32
HI = lax.Precision.HIGHEST

D_MODEL = 1024
HEAD_DIM = 64
N_HEADS = 4
D_FOX = N_HEADS * HEAD_DIM
D_NSA = N_HEADS * HEAD_DIM
GLA_DV = 128
D_GLA = N_HEADS * GLA_DV
GLA_RANK = 16
GLA_TAU = 16.0
GLA_CHUNK = 64
CMP_LEN = 32
SEL_LEN = 64
TOP_N = 16
WINDOW = 512
ROPE_THETA = 500000.0
ROPE_DIM = HEAD_DIM // 4
ROPE_HALF = ROPE_DIM // 2
D_FF = 3584
N_EXPERTS = 8
EPS = 1e-6
SEL_FORCE = 1e9
NEG = -1e30
SCALE = HEAD_DIM ** -0.5
LOG2E = 1.4426950408889634

LANES = 128
SUBLANES = 8
VMEM_LIMIT = 56 * 1024 * 1024

C_FQ = 0
C_FKV = 256
C_NQ = 768
C_NKV = 1024
C_GQK = 1408
C_GV = 1920
C_GOG = 2432
C_SMALL = 2944
C_END = 3072
SM_FF = 0
SM_NG = 4
SM_GLR = 16

NT = (((1,), (1,)), ((), ()))


def _params(sem):
    return pltpu.CompilerParams(dimension_semantics=sem, vmem_limit_bytes=VMEM_LIMIT)


def _rms(x, g):
    ms = jnp.mean(x * x, axis=-1, keepdims=True)
    return x * lax.rsqrt(ms + EPS) * g


def _sigmoid(x):
    return 1.0 / (1.0 + jnp.exp(-x))


def _log_sigmoid(x):
    return -(jnp.maximum(-x, 0.0) + jnp.log1p(jnp.exp(-jnp.abs(x))))


def _silu(x):
    return x * _sigmoid(x)


def _bdot(a, b):
    return jnp.dot(a.astype(BF16), b.astype(BF16), preferred_element_type=F32)


def _bdot_nt(a, b):
    return lax.dot_general(a.astype(BF16), b.astype(BF16), NT, preferred_element_type=F32)


def _rope128(x, a, bp, bm):
    return x * a + pltpu.roll(x, ROPE_HALF, 1) * bp + pltpu.roll(x, LANES - ROPE_HALF, 1) * bm


def _inproj_kernel(x_ref, g_ref, w_ref, sb_ref, wgk_ref, bgk_ref, tab_ref,
                   fq_ref, fkv_ref, nq_ref, nqr_ref, rows_ref, win_ref,
                   gqk_ref, gv_ref, gog_ref, glog_ref, small_ref):
    h = _rms(x_ref[...], g_ref[...]).astype(BF16)

    def mm(a, b):
        return jnp.dot(h, w_ref[:, a:b], preferred_element_type=F32)

    fq_ref[...] = mm(C_FQ, C_FKV)
    fkv_ref[...] = mm(C_FKV, C_NQ)
    tab = tab_ref[...]
    ab, pb, mb = tab[:, 0:128], tab[:, 128:256], tab[:, 256:384]
    af, pf, mf = tab[:, 384:512], tab[:, 512:640], tab[:, 640:768]
    nq = mm(C_NQ, C_NKV)
    nq_ref[...] = nq
    nqr_ref[:, 0:128] = _rope128(nq[:, 0:128], ab, pb, mb)
    nqr_ref[:, 128:256] = _rope128(nq[:, 128:256], ab, pb, mb)
    nkv = mm(C_NKV, C_GQK)
    rows_ref[:, 0:128] = nkv[:, 0:128]
    rows_ref[:, 128:256] = _rope128(nkv[:, 128:256], af, pf, mf)
    win_ref[...] = _rope128(nkv[:, 256:384], af, pf, mf)
    gqk_ref[...] = mm(C_GQK, C_GV)
    gv_ref[...] = mm(C_GV, C_GOG)
    gog_ref[...] = mm(C_GOG, C_SMALL)
    sm = mm(C_SMALL, C_END)
    glog_ref[...] = _log_sigmoid(_bdot(sm, wgk_ref[...]) + bgk_ref[...]) * (1.0 / GLA_TAU)
    smb = sm + sb_ref[...]
    lane = lax.broadcasted_iota(I32, smb.shape, 1)
    small_ref[...] = jnp.where(lane < SM_NG, _log_sigmoid(smb), _sigmoid(smb))


def _in_projection(x_all, row0, n_rows, tm, g, w_r, sb, wgk, bgk, tab, tab_period):
    assert n_rows % tm == 0 and row0 % tm == 0 and tab_period % tm == 0
    nt = n_rows // tm
    b0 = row0 // tm
    npd = tab_period // tm
    widths = (256, 512, 256, 256, 256, 128, 512, 512, 512, 256, 128)
    full = lambda shape: pl.BlockSpec(shape, lambda i: (0, 0))
    return pl.pallas_call(
        _inproj_kernel,
        grid=(nt,),
        in_specs=[
            pl.BlockSpec((tm, D_MODEL), lambda i: (b0 + i, 0)),
            full((1, D_MODEL)),
            full((D_MODEL, C_END)),
            full((1, LANES)),
            full((LANES, 256)),
            full((1, 256)),
            pl.BlockSpec((tm, 768), lambda i: (i % npd, 0)),
        ],
        out_specs=[pl.BlockSpec((tm, w), lambda i: (i, 0)) for w in widths],
        out_shape=[jax.ShapeDtypeStruct((n_rows, w), F32) for w in widths],
        compiler_params=_params(("parallel",)),
        name="in_projection",
    )(x_all, g, w_r, sb, wgk, bgk, tab)


def _cumsum_kernel(sm_ref, cr_ref, carry):
    t = pl.program_id(1)
    ts = sm_ref.shape[0]

    @pl.when(t == 0)
    def _():
        carry[...] = jnp.zeros_like(carry)

    r = lax.broadcasted_iota(I32, (ts, ts), 0)
    c = lax.broadcasted_iota(I32, (ts, ts), 1)
    tri = (c <= r).astype(F32)
    cs = jnp.dot(tri, sm_ref[...], precision=HI, preferred_element_type=F32) + carry[...]
    carry[...] = cs[ts - 1:ts, :]
    cr_ref[...] = cs.T[0:SUBLANES, :] * LOG2E


def _fox_cumsum(small, B, S, ts):
    ns = S // ts
    return pl.pallas_call(
        _cumsum_kernel,
        grid=(B, ns),
        in_specs=[pl.BlockSpec((ts, LANES), lambda b, t: (b * ns + t, 0))],
        out_specs=pl.BlockSpec((None, SUBLANES, ts), lambda b, t: (b, 0, t)),
        out_shape=jax.ShapeDtypeStruct((B, SUBLANES, S), F32),
        scratch_shapes=[pltpu.VMEM((1, LANES), F32)],
        compiler_params=_params(("parallel", "arbitrary")),
        name="fox_cumsum",
    )(small)


def _pair_mask(shape, h):
    return (lax.broadcasted_iota(I32, shape, 1) // HEAD_DIM) == (h % 2)


def _fox_prompt_kernel(q_ref, kv_ref, cr_ref, o_ref, q_sc, m_sc, acc_sc):
    i = pl.program_id(1)
    j = pl.program_id(2)
    nk = pl.num_programs(2)
    tq = q_ref.shape[0]
    tk = kv_ref.shape[0]

    @pl.when(j == 0)
    def _():
        m_sc[...] = jnp.full_like(m_sc, NEG)
        acc_sc[...] = jnp.zeros_like(acc_sc)
        for h in range(N_HEADS):
            slab = q_ref[:, (h // 2) * LANES:(h // 2 + 1) * LANES] * (SCALE * LOG2E)
            q_sc[h] = jnp.where(_pair_mask(slab.shape, h), slab, 0.0).astype(BF16)

    def tile(diagonal):
        if diagonal:
            mask = lax.broadcasted_iota(I32, (1, tk), 1) <= lax.broadcasted_iota(I32, (tq, 1), 0)
        for h in range(N_HEADS):
            c0 = (h // 2) * LANES
            k_slab = kv_ref[:, c0:c0 + LANES].astype(BF16)
            s = lax.dot_general(q_sc[h], k_slab, NT, preferred_element_type=F32) - cr_ref[h:h + 1, :]
            if diagonal:
                s = jnp.where(mask, s, NEG)
            m_old = m_sc[h]
            m_new = jnp.maximum(m_old, jnp.max(s, axis=-1, keepdims=True))
            p = jnp.exp2(s - m_new).astype(BF16)
            v_slab = kv_ref[:, D_FOX + c0:D_FOX + c0 + LANES]
            v_aug = jnp.where(_pair_mask(v_slab.shape, h), v_slab, 1.0).astype(BF16)
            acc_sc[h] = jnp.exp2(m_old - m_new) * acc_sc[h] + jnp.dot(p, v_aug, preferred_element_type=F32)
            m_sc[h] = m_new

    @pl.when(j < i)
    def _():
        tile(False)

    @pl.when(j == i)
    def _():
        tile(True)

    @pl.when(j == nk - 1)
    def _():
        for h in range(N_HEADS):
            a = acc_sc[h]
            lo = (h % 2) * HEAD_DIM
            den = a[:, HEAD_DIM - lo:HEAD_DIM - lo + 1]
            o_ref[:, h * HEAD_DIM:(h + 1) * HEAD_DIM] = a[:, lo:lo + HEAD_DIM] / den


def _fox_prompt(fq, fkv, cr, B, S, t):
    n = S // t
    return pl.pallas_call(
        _fox_prompt_kernel,
        grid=(B, n, n),
        in_specs=[
            pl.BlockSpec((t, D_FOX), lambda b, i, j: (b * n + i, 0)),
            pl.BlockSpec((t, 2 * D_FOX), lambda b, i, j: (b * n + jnp.minimum(i, j), 0)),
            pl.BlockSpec((None, SUBLANES, t), lambda b, i, j: (b, 0, jnp.minimum(i, j))),
        ],
        out_specs=pl.BlockSpec((t, D_FOX), lambda b, i, j: (b * n + i, 0)),
        out_shape=jax.ShapeDtypeStruct((B * S, D_FOX), F32),
        scratch_shapes=[pltpu.VMEM((N_HEADS, t, LANES), BF16), pltpu.VMEM((N_HEADS, t, 1), F32),
                        pltpu.VMEM((N_HEADS, t, LANES), F32)],
        compiler_params=_params(("parallel", "parallel", "arbitrary")),
        name="fox_prompt",
    )(fq, fkv, cr)


def _compress_kernel(x_ref, pe_ref, w_ref, o_ref, *, exact):
    for s in range(2):
        x = x_ref[s] + pe_ref[s]
        if exact:
            y = jnp.dot(x, w_ref[s], precision=HI, preferred_element_type=F32)
        else:
            y = _bdot(x, w_ref[s])
        o_ref[:, s * HEAD_DIM:(s + 1) * HEAD_DIM] = y


def _compress(x3, pe, w, tr, exact, row0=0, n_rows=None):
    K = x3.shape[2]
    R = x3.shape[1] if n_rows is None else n_rows
    assert R % tr == 0 and row0 % tr == 0
    b0 = row0 // tr
    return pl.pallas_call(
        functools.partial(_compress_kernel, exact=exact),
        grid=(R // tr,),
        in_specs=[pl.BlockSpec((2, tr, K), lambda i: (0, b0 + i, 0)),
                  pl.BlockSpec((2, 1, K), lambda i: (0, 0, 0)),
                  pl.BlockSpec((2, K, HEAD_DIM), lambda i: (0, 0, 0))],
        out_specs=pl.BlockSpec((tr, LANES), lambda i: (i, 0)),
        out_shape=jax.ShapeDtypeStruct((R, LANES), F32),
        compiler_params=_params(("parallel",)),
        name="nsa_compress",
    )(x3, pe, w)


def _block_major(kv, n_blocks):
    return kv.reshape(n_blocks, CMP_LEN, 2, HEAD_DIM).transpose(2, 0, 1, 3).reshape(
        2, n_blocks, CMP_LEN * HEAD_DIM)


def _order_key(x):
    b = lax.bitcast_convert_type(x, I32)
    return jnp.where(b < 0, b ^ jnp.int32(0x7FFFFFFF), b)


def _nsa_prompt_kernel(nq_ref, nqr_ref, sm_ref, cmp_ref, rows_ref, win_ref, o_ref, qx_sc, *, tk):
    QB = nq_ref.shape[0]
    S = rows_ref.shape[0]
    nb = cmp_ref.shape[0]
    nsel = S // SEL_LEN
    i = pl.program_id(1)
    qs = i * QB
    qpos = qs + lax.broadcasted_iota(I32, (QB, 1), 0)

    cmp = cmp_ref[...]
    kc = cmp[:, 0:HEAD_DIM]
    vc = cmp[:, HEAD_DIM:2 * HEAD_DIM]
    n_l = lax.broadcasted_iota(I32, (1, nb), 1)
    complete = ((n_l + 1) * CMP_LEN - 1) <= qpos
    complete_f = complete.astype(F32)
    psum = jnp.zeros((QB, nb), F32)
    o_cmp = []
    for h in range(N_HEADS):
        qh = nq_ref[:, h * HEAD_DIM:(h + 1) * HEAD_DIM]
        s = lax.dot_general(qh, kc, NT, precision=HI, preferred_element_type=F32) * SCALE
        s = jnp.where(complete, s, NEG)
        e = jnp.exp(s - jnp.max(s, axis=-1, keepdims=True))
        p = e / jnp.sum(e, axis=-1, keepdims=True) * complete_f
        o_cmp.append(_bdot(p, cmp))
        psum = psum + p

    pj = lax.broadcasted_iota(I32, (nsel, nb), 0)
    pn = lax.broadcasted_iota(I32, (nsel, nb), 1)
    pair_t = (pn // (SEL_LEN // CMP_LEN) == pj).astype(F32)
    imp_t = lax.dot_general(pair_t, psum, NT, precision=HI, preferred_element_type=F32)
    jt = (qs + lax.broadcasted_iota(I32, (1, QB), 1)) // SEL_LEN
    jj = lax.broadcasted_iota(I32, (nsel, 1), 0)
    score = jnp.where(jj == jt, 2.0 * SEL_FORCE,
                      jnp.where((jj == 0) | (jj == jt - 1), SEL_FORCE,
                                jnp.where(jj <= jt, imp_t + 0.0, -1.0)))
    key = _order_key(score)
    key_m1 = key - 1
    ngrp = nsel // SUBLANES
    sub = lax.broadcasted_iota(I32, (SUBLANES, QB), 0)
    kg = [key[r * SUBLANES:(r + 1) * SUBLANES, :] for r in range(ngrp)]
    kg1 = [key_m1[r * SUBLANES:(r + 1) * SUBLANES, :] for r in range(ngrp)]
    cnt = [jnp.zeros((SUBLANES, QB), I32) for _ in range(ngrp)]
    for jp in range(nsel):
        g = jp // SUBLANES
        row = key[jp:jp + 1, :]
        mixed = jnp.where(sub > (jp % SUBLANES), kg1[g], kg[g])
        for r in range(ngrp):
            thr = kg[r] if r < g else (kg1[r] if r > g else mixed)
            cnt[r] = cnt[r] + (row > thr).astype(I32)
    sel_t = jnp.concatenate([(c < TOP_N).astype(F32) for c in cnt], axis=0)
    if nsel < QB:
        sel_t = jnp.concatenate([sel_t, jnp.zeros((QB - nsel, QB), F32)], axis=0)
    sel = sel_t.T.astype(BF16)

    lo_half = lax.broadcasted_iota(I32, (QB, LANES), 1) < HEAD_DIM
    for h in range(N_HEADS):
        slab = nqr_ref[:, (h // 2) * LANES:(h // 2 + 1) * LANES] * (SCALE * LOG2E)
        if h % 2:
            slab = pltpu.roll(slab, HEAD_DIM, 1)
        qx_sc[h * QB:(h + 1) * QB, :] = jnp.where(lo_half, slab, 0.0).astype(BF16)
    qx = qx_sc[...]
    HQ = N_HEADS * QB

    def attend(s, valid, slab, m_old, acc_old):
        n = s.shape[1]
        s = jnp.where(valid[None], s.reshape(N_HEADS, QB, n), NEG).reshape(HQ, n)
        m_new = jnp.maximum(m_old, jnp.max(s, axis=-1, keepdims=True))
        p = jnp.exp2(s - m_new).astype(BF16)
        ones_k = lax.broadcasted_iota(I32, slab.shape, 1) < HEAD_DIM
        v_aug = jnp.where(ones_k, 1.0, slab).astype(BF16)
        acc = jnp.exp2(m_old - m_new) * acc_old + jnp.dot(p, v_aug, preferred_element_type=F32)
        return m_new, acc

    jrow = lax.broadcasted_iota(I32, (QB, 1), 0)

    def sel_tile(k0, m_old, acc_old, diagonal):
        kpos = k0 + lax.broadcasted_iota(I32, (1, tk), 1)
        expand = (jrow == kpos // SEL_LEN).astype(BF16)
        valid = jnp.dot(sel, expand, preferred_element_type=F32) > 0.5
        if diagonal:
            valid = valid & (kpos <= qpos)
        slab = rows_ref[pl.ds(k0, tk), 2 * HEAD_DIM:4 * HEAD_DIM]
        s = lax.dot_general(qx, slab.astype(BF16), NT, preferred_element_type=F32)
        return attend(s, valid, slab, m_old, acc_old)

    n_full = qs // tk
    init = (jnp.full((HQ, 1), NEG, F32), jnp.zeros((HQ, LANES), F32))
    m_s, acc_s = lax.fori_loop(
        0, n_full, lambda t, c: sel_tile(pl.multiple_of(t * tk, tk), c[0], c[1], False), init)
    _, acc_s = sel_tile(pl.multiple_of(n_full * tk, tk), m_s, acc_s, True)

    wlen = WINDOW + QB
    w0 = pl.multiple_of(jnp.maximum(qs - WINDOW, 0), QB)
    wpos = w0 + lax.broadcasted_iota(I32, (1, wlen), 1)
    d = qpos - wpos
    wslab = win_ref[pl.ds(w0, wlen), :]
    sw = lax.dot_general(qx, wslab.astype(BF16), NT, preferred_element_type=F32)
    _, acc_w = attend(sw, (d >= 0) & (d < WINDOW), wslab,
                      jnp.full((HQ, 1), NEG, F32), jnp.zeros((HQ, LANES), F32))

    sm = sm_ref[...]
    for h in range(N_HEADS):
        rs = slice(h * QB, (h + 1) * QB)
        o_sel = acc_s[rs] * (1.0 / acc_s[rs, 0:1])
        o_win = acc_w[rs] * (1.0 / acc_w[rs, 0:1])
        c = SM_NG + 3 * h
        mix = sm[:, c:c + 1] * o_cmp[h] + sm[:, c + 1:c + 2] * o_sel + sm[:, c + 2:c + 3] * o_win
        if h % 2 == 0:
            mix = pltpu.roll(mix, HEAD_DIM, 1)
        lo = (h % 2) * HEAD_DIM
        o_ref[:, h * HEAD_DIM:(h + 1) * HEAD_DIM] = mix[:, lo:lo + HEAD_DIM]


def _nsa_prompt(nq, nqr, small, cmp, rows, win, B, S, tk):
    QB = 128
    nq_t = S // QB
    nb = S // CMP_LEN
    assert S % tk == 0 and S >= WINDOW + QB
    return pl.pallas_call(
        functools.partial(_nsa_prompt_kernel, tk=tk),
        grid=(B, nq_t),
        in_specs=[
            pl.BlockSpec((QB, D_NSA), lambda b, i: (b * nq_t + i, 0)),
            pl.BlockSpec((QB, D_NSA), lambda b, i: (b * nq_t + i, 0)),
            pl.BlockSpec((QB, LANES), lambda b, i: (b * nq_t + i, 0)),
            pl.BlockSpec((nb, LANES), lambda b, i: (b, 0)),
            pl.BlockSpec((S, 4 * HEAD_DIM), lambda b, i: (b, 0)),
            pl.BlockSpec((S, 2 * HEAD_DIM), lambda b, i: (b, 0)),
        ],
        out_specs=pl.BlockSpec((QB, D_NSA), lambda b, i: (b * nq_t + i, 0)),
        out_shape=jax.ShapeDtypeStruct((B * S, D_NSA), F32),
        scratch_shapes=[pltpu.VMEM((N_HEADS * QB, LANES), BF16)],
        compiler_params=_params(("parallel", "parallel")),
        name="nsa_prompt",
    )(nq, nqr, small, cmp, rows, win)


def _gla_prompt_kernel(qk_ref, v_ref, g_ref, o_ref, st_ref, s_sc):
    t = pl.program_id(1)
    nt = pl.num_programs(1)
    tc = qk_ref.shape[0]
    C = GLA_CHUNK

    @pl.when(t == 0)
    def _():
        s_sc[...] = jnp.zeros_like(s_sc)

    r = lax.broadcasted_iota(I32, (C, C), 0)
    c = lax.broadcasted_iota(I32, (C, C), 1)
    causal = c <= r
    low = causal.astype(F32)
    up = (r <= c).astype(F32)
    k_t = qk_ref[:, D_FOX:2 * D_FOX].T
    g_t = g_ref[...].T
    for ci in range(tc // C):
        rs = slice(ci * C, (ci + 1) * C)
        for h in range(N_HEADS):
            hs = slice(h * HEAD_DIM, (h + 1) * HEAD_DIM)
            q = qk_ref[rs, hs] * SCALE
            k = qk_ref[rs, D_FOX + h * HEAD_DIM:D_FOX + (h + 1) * HEAD_DIM]
            v = v_ref[rs, h * GLA_DV:(h + 1) * GLA_DV].astype(BF16)
            gcum = jnp.dot(low, g_ref[rs, hs], precision=HI, preferred_element_type=F32)
            qe = (q * jnp.exp(gcum)).astype(BF16)
            ke = (k * jnp.exp(-gcum)).astype(BF16)
            a = jnp.where(causal, lax.dot_general(qe, ke, NT, preferred_element_type=F32), 0.0)
            gcum_t = jnp.dot(g_t[hs, rs], up, precision=HI, preferred_element_type=F32)
            glast_t = gcum_t[:, C - 1:C]
            kd_t = (k_t[hs, rs] * jnp.exp(glast_t - gcum_t)).astype(BF16)
            state = s_sc[h]
            o_ref[rs, h * GLA_DV:(h + 1) * GLA_DV] = (
                jnp.dot(a.astype(BF16), v, preferred_element_type=F32)
                + jnp.dot(qe, state.astype(BF16), preferred_element_type=F32))
            s_sc[h] = jnp.exp(glast_t) * state + jnp.dot(kd_t, v, preferred_element_type=F32)

    @pl.when(t == nt - 1)
    def _():
        st_ref[...] = s_sc[...]


def _gla_prompt(gqk, gv, glog, B, S, tc):
    nt = S // tc
    return pl.pallas_call(
        _gla_prompt_kernel,
        grid=(B, nt),
        in_specs=[pl.BlockSpec((tc, 2 * D_FOX), lambda b, t: (b * nt + t, 0)),
                  pl.BlockSpec((tc, D_GLA), lambda b, t: (b * nt + t, 0)),
                  pl.BlockSpec((tc, D_FOX), lambda b, t: (b * nt + t, 0))],
        out_specs=[pl.BlockSpec((tc, D_GLA), lambda b, t: (b * nt + t, 0)),
                   pl.BlockSpec((None, N_HEADS, HEAD_DIM, GLA_DV), lambda b, t: (b, 0, 0, 0))],
        out_shape=[jax.ShapeDtypeStruct((B * S, D_GLA), F32),
                   jax.ShapeDtypeStruct((B, N_HEADS, HEAD_DIM, GLA_DV), F32)],
        scratch_shapes=[pltpu.VMEM((N_HEADS, HEAD_DIM, GLA_DV), F32)],
        compiler_params=_params(("parallel", "arbitrary")),
        name="gla_prompt",
    )(gqk, gv, glog)


def _outproj_kernel(x_ref, of_ref, on_ref, og_ref, gog_ref, gn_ref, w_ref, o_ref):
    acc = _bdot(of_ref[...], w_ref[0:D_FOX, :])
    acc = acc + _bdot(on_ref[...], w_ref[D_FOX:D_FOX + D_NSA, :])
    for h in range(N_HEADS):
        hs = slice(h * GLA_DV, (h + 1) * GLA_DV)
        z = _rms(og_ref[:, hs], gn_ref[...]) * _silu(gog_ref[:, hs])
        w0 = D_FOX + D_NSA + h * GLA_DV
        acc = acc + _bdot(z, w_ref[w0:w0 + GLA_DV, :])
    o_ref[...] = x_ref[...] + acc


def _out_projection(x, o_fox, o_nsa, o_gla, gog, gn, w_out, tm):
    T = x.shape[0]
    assert T % tm == 0
    row = lambda w: pl.BlockSpec((tm, w), lambda i: (i, 0))
    return pl.pallas_call(
        _outproj_kernel,
        grid=(T // tm,),
        in_specs=[row(D_MODEL), row(D_FOX), row(D_NSA), row(D_GLA), row(D_GLA),
                  pl.BlockSpec((1, GLA_DV), lambda i: (0, 0)),
                  pl.BlockSpec((D_MODEL, D_MODEL), lambda i: (0, 0))],
        out_specs=row(D_MODEL),
        out_shape=jax.ShapeDtypeStruct((T, D_MODEL), F32),
        compiler_params=_params(("parallel",)),
        name="out_projection",
    )(x, o_fox, o_nsa, o_gla, gog, gn, w_out)


def _dense_ffn_kernel(x_ref, g_ref, wg_ref, wu_ref, wd_ref, o_ref, h_sc, acc_sc):
    f = pl.program_id(1)
    nf = pl.num_programs(1)

    @pl.when(f == 0)
    def _():
        h_sc[...] = _rms(x_ref[...], g_ref[...]).astype(BF16)
        acc_sc[...] = jnp.zeros_like(acc_sc)

    h = h_sc[...]
    a = jnp.dot(h, wg_ref[...], preferred_element_type=F32)
    u = jnp.dot(h, wu_ref[...], preferred_element_type=F32)
    acc_sc[...] += _bdot(_silu(a) * u, wd_ref[...])

    @pl.when(f == nf - 1)
    def _():
        o_ref[...] = x_ref[...] + acc_sc[...]


def _dense_ffn(x, g, wg, wu, wd, tm, tf):
    T = x.shape[0]
    assert T % tm == 0 and D_FF % tf == 0
    return pl.pallas_call(
        _dense_ffn_kernel,
        grid=(T // tm, D_FF // tf),
        in_specs=[pl.BlockSpec((tm, D_MODEL), lambda i, f: (i, 0)),
                  pl.BlockSpec((1, D_MODEL), lambda i, f: (0, 0)),
                  pl.BlockSpec((D_MODEL, tf), lambda i, f: (0, f)),
                  pl.BlockSpec((D_MODEL, tf), lambda i, f: (0, f)),
                  pl.BlockSpec((tf, D_MODEL), lambda i, f: (f, 0))],
        out_specs=pl.BlockSpec((tm, D_MODEL), lambda i, f: (i, 0)),
        out_shape=jax.ShapeDtypeStruct((T, D_MODEL), F32),
        scratch_shapes=[pltpu.VMEM((tm, D_MODEL), BF16), pltpu.VMEM((tm, D_MODEL), F32)],
        compiler_params=_params(("parallel", "arbitrary")),
        name="dense_ffn",
    )(x, g, wg, wu, wd)


def _router_kernel(x_ref, g_ref, wr_ref, h_ref, r_ref):
    h = _rms(x_ref[...], g_ref[...])
    h_ref[...] = h
    logits = jnp.dot(h, wr_ref[...], precision=HI, preferred_element_type=F32)
    lane = lax.broadcasted_iota(I32, logits.shape, 1)
    lg = jnp.where(lane < N_EXPERTS, logits, -jnp.inf)
    m1 = jnp.max(lg, axis=-1, keepdims=True)
    i1 = jnp.min(jnp.where(lg == m1, lane, LANES), axis=-1, keepdims=True)
    lg2 = jnp.where(lane == i1, -jnp.inf, lg)
    m2 = jnp.max(lg2, axis=-1, keepdims=True)
    i2 = jnp.min(jnp.where(lg2 == m2, lane, LANES), axis=-1, keepdims=True)
    e = jnp.exp(m2 - m1)
    den = 1.0 + e
    r_ref[...] = jnp.where(lane == 0, i1.astype(F32),
                           jnp.where(lane == 1, i2.astype(F32),
                                     jnp.where(lane == 2, 1.0 / den,
                                               jnp.where(lane == 3, e / den, 0.0))))


def _router(x, g, wr_pad, tm):
    T = x.shape[0]
    assert T % tm == 0
    return pl.pallas_call(
        _router_kernel,
        grid=(T // tm,),
        in_specs=[pl.BlockSpec((tm, D_MODEL), lambda i: (i, 0)),
                  pl.BlockSpec((1, D_MODEL), lambda i: (0, 0)),
                  pl.BlockSpec((D_MODEL, LANES), lambda i: (0, 0))],
        out_specs=[pl.BlockSpec((tm, D_MODEL), lambda i: (i, 0)),
                   pl.BlockSpec((tm, LANES), lambda i: (i, 0))],
        out_shape=[jax.ShapeDtypeStruct((T, D_MODEL), F32),
                   jax.ShapeDtypeStruct((T, LANES), F32)],
        compiler_params=_params(("parallel",)),
        name="moe_router",
    )(x, g, wr_pad)


GATHER_WINDOW = 32
SC_WORKERS = 32


def _row_gather(src, idx):
    n = idx.shape[0]
    step = GATHER_WINDOW * SC_WORKERS
    n_pad = -(-n // step) * step
    if n_pad != n:
        idx = jnp.concatenate([idx, jnp.zeros((n_pad - n,), idx.dtype)])
    width = src.shape[1]
    per_worker = n_pad // SC_WORKERS
    mesh = plsc.VectorSubcoreMesh(core_axis_name="core", subcore_axis_name="subcore")

    @functools.partial(pl.kernel, out_type=jax.ShapeDtypeStruct((n_pad, width), src.dtype), mesh=mesh,
                       scratch_types=[pltpu.VMEM((per_worker,), I32),
                                      pltpu.VMEM((GATHER_WINDOW, width), src.dtype)],
                       name="row_gather")
    def gather(src_hbm, idx_hbm, dst_hbm, idx_v, buf):
        worker = lax.axis_index("core") * (SC_WORKERS // 2) + lax.axis_index("subcore")
        base = worker * per_worker
        pltpu.sync_copy(idx_hbm.at[pl.ds(base, per_worker)], idx_v)

        @pl.loop(0, per_worker // GATHER_WINDOW)
        def _(j):
            pltpu.sync_copy(src_hbm.at[idx_v.at[pl.ds(j * GATHER_WINDOW, GATHER_WINDOW)]], buf)
            pltpu.sync_copy(buf, dst_hbm.at[pl.ds(base + j * GATHER_WINDOW, GATHER_WINDOW)])

    return gather(src, idx)


def _moe_ffn_kernel(be_ref, nu_ref, x_ref, wg_ref, wu_ref, wd_ref, o_ref, acc_sc):
    b = pl.program_id(0)
    f = pl.program_id(1)
    nf = pl.num_programs(1)
    used = b < nu_ref[0]

    @pl.when(used)
    def _():
        @pl.when(f == 0)
        def _():
            acc_sc[...] = jnp.zeros_like(acc_sc)

        x = x_ref[...].astype(BF16)
        a = jnp.dot(x, wg_ref[...].astype(BF16), preferred_element_type=F32)
        u = jnp.dot(x, wu_ref[...].astype(BF16), preferred_element_type=F32)
        acc_sc[...] += _bdot(_silu(a) * u, wd_ref[...])

        @pl.when(f == nf - 1)
        def _():
            o_ref[...] = acc_sc[...]

    @pl.when(jnp.logical_not(used) & (f == nf - 1))
    def _():
        o_ref[...] = jnp.zeros_like(o_ref)


def _moe_ffn(xb, block_e, n_used, wg, wu, wd, blk, tf):
    cap = xb.shape[0]
    nb = cap // blk
    nf = D_FF // tf

    def bsel(b, nu):
        return jnp.minimum(b, nu[0] - 1)

    def fsel(b, f, nu):
        return jnp.where(b < nu[0], f, nf - 1)

    return pl.pallas_call(
        _moe_ffn_kernel,
        grid_spec=pltpu.PrefetchScalarGridSpec(
            num_scalar_prefetch=2,
            grid=(nb, nf),
            in_specs=[
                pl.BlockSpec((blk, D_MODEL), lambda b, f, be, nu: (bsel(b, nu), 0)),
                pl.BlockSpec((None, D_MODEL, tf), lambda b, f, be, nu: (be[bsel(b, nu)], 0, fsel(b, f, nu))),
                pl.BlockSpec((None, D_MODEL, tf), lambda b, f, be, nu: (be[bsel(b, nu)], 0, fsel(b, f, nu))),
                pl.BlockSpec((None, tf, D_MODEL), lambda b, f, be, nu: (be[bsel(b, nu)], fsel(b, f, nu), 0)),
            ],
            out_specs=pl.BlockSpec((blk, D_MODEL), lambda b, f, be, nu: (b, 0)),
            scratch_shapes=[pltpu.VMEM((blk, D_MODEL), F32)],
        ),
        out_shape=jax.ShapeDtypeStruct((cap, D_MODEL), F32),
        compiler_params=_params(("arbitrary", "arbitrary")),
        name="moe_ffn",
    )(block_e, n_used, xb, wg, wu, wd)


def _moe_combine_kernel(x_ref, y1_ref, y2_ref, r_ref, o_ref):
    r = r_ref[...]
    o_ref[...] = x_ref[...] + (r[:, 2:3] * y1_ref[...] + r[:, 3:4] * y2_ref[...])


def _moe_combine(x, y1, y2, route, tm, row0):
    T = x.shape[0]
    nt = T // tm
    assert row0 % tm == 0
    b0 = row0 // tm
    return pl.pallas_call(
        _moe_combine_kernel,
        grid=(nt,),
        in_specs=[pl.BlockSpec((tm, D_MODEL), lambda i: (i, 0)),
                  pl.BlockSpec((tm, D_MODEL), lambda i: (b0 + i, 0)),
                  pl.BlockSpec((tm, D_MODEL), lambda i: (b0 + i, 0)),
                  pl.BlockSpec((tm, LANES), lambda i: (i, 0))],
        out_specs=pl.BlockSpec((tm, D_MODEL), lambda i: (i, 0)),
        out_shape=jax.ShapeDtypeStruct((T, D_MODEL), F32),
        compiler_params=_params(("parallel",)),
        name="moe_combine",
    )(x, y1, y2, route)


def _moe_plan(e_top, blk):
    T = e_top.shape[0]
    n = 2 * T
    flat_e = e_top.reshape(-1)
    onehot = (flat_e[:, None] == jnp.arange(N_EXPERTS, dtype=I32)[None, :]).astype(I32)
    csum = jnp.cumsum(onehot, axis=0)
    rank = jnp.sum((csum - onehot) * onehot, axis=1)
    counts = csum[-1]
    padded = (counts + blk - 1) // blk * blk
    ends = jnp.cumsum(padded)
    pstart = ends - padded
    dest = (pstart[flat_e] + rank).astype(I32)
    n_blocks = -(-n // blk) + N_EXPERTS
    cap = n_blocks * blk
    slot_tok = jnp.zeros((cap,), I32).at[dest].set(jnp.arange(n, dtype=I32) // 2)
    first = jnp.arange(n_blocks, dtype=I32) * blk
    block_e = jnp.minimum(jnp.sum((ends[None, :] <= first[:, None]).astype(I32), axis=1), N_EXPERTS - 1)
    n_used = (ends[-1] // blk).astype(I32).reshape(1)
    return dest, slot_tok, block_e, n_used


def _moe_layer(xs, g, wr_pad, wg, wu, wd, tms, blk, tf):
    routed = [_router(x, g, wr_pad, tm) for x, tm in zip(xs, tms)]
    h = jnp.concatenate([r[0] for r in routed], axis=0) if len(xs) > 1 else routed[0][0]
    e_top = jnp.concatenate([r[1][:, 0:2] for r in routed], axis=0).astype(I32)
    dest, slot_tok, block_e, n_used = _moe_plan(e_top, blk)
    xb = _row_gather(h, slot_tok)
    yb = _moe_ffn(xb, block_e, n_used, wg, wu, wd, blk, tf)
    d2 = dest.reshape(-1, 2)
    y1 = _row_gather(yb, d2[:, 0])
    y2 = _row_gather(yb, d2[:, 1])
    out, row0 = [], 0
    for x, tm, r in zip(xs, tms, routed):
        out.append(_moe_combine(x, y1, y2, r[1], tm, row0))
        row0 += x.shape[0]
    return out


def _norm_kernel(x_ref, g_ref, o_ref):
    o_ref[...] = _rms(x_ref[...], g_ref[...])


def _final_norm(x, g, tm):
    T = x.shape[0]
    return pl.pallas_call(
        _norm_kernel,
        grid=(T // tm,),
        in_specs=[pl.BlockSpec((tm, D_MODEL), lambda i: (i, 0)),
                  pl.BlockSpec((1, D_MODEL), lambda i: (0, 0))],
        out_specs=pl.BlockSpec((tm, D_MODEL), lambda i: (i, 0)),
        out_shape=jax.ShapeDtypeStruct((T, D_MODEL), F32),
        compiler_params=_params(("parallel",)),
        name="final_norm",
    )(x, g)


_IN_SPLITS = (D_FOX, D_FOX, D_FOX, N_HEADS, D_NSA, 6 * HEAD_DIM, 3 * N_HEADS,
              D_FOX, D_FOX, D_GLA, GLA_RANK, D_GLA)


def _reorder_w_in(w):
    offs = [0]
    for s in _IN_SPLITS:
        offs.append(offs[-1] + s)
    seg = lambda k: w[:, offs[k]:offs[k + 1]]
    fq, fk, fv, ff, nq, nkv, ng, gq, gk, gv, glr, gog = [seg(k) for k in range(12)]
    pad = jnp.zeros((w.shape[0], LANES - SM_GLR - GLA_RANK), w.dtype)
    return jnp.concatenate([fq, fk, fv, nq, nkv, gq, gk, gv, gog, ff, ng, glr, pad], axis=1).astype(BF16)


def _rope_table(pos):
    inv = ROPE_THETA ** (-jnp.arange(ROPE_HALF, dtype=F32) / ROPE_HALF)
    ang = pos.astype(F32)[:, None] * inv[None, :]
    cos, sin = jnp.cos(ang), jnp.sin(ang)
    P = pos.shape[0]
    one = jnp.ones((P, HEAD_DIM - ROPE_DIM), F32)
    zero = jnp.zeros((P, HEAD_DIM - ROPE_DIM), F32)
    z8 = jnp.zeros((P, ROPE_HALF), F32)
    a64 = jnp.concatenate([cos, cos, one], axis=1)
    p64 = jnp.concatenate([z8, sin, zero], axis=1)
    m64 = jnp.concatenate([-sin, z8, zero], axis=1)
    i64 = jnp.ones((P, HEAD_DIM), F32)
    o64 = jnp.zeros((P, HEAD_DIM), F32)
    return jnp.concatenate([a64, a64, p64, p64, m64, m64, a64, i64, p64, o64, m64, o64], axis=1)


def _layer_mix_params(l, norm_mix_g, w_in, b_fox_f, w_cmp, pe_cmp, w_gla_gk, b_gla_gk, g_gla_norm, w_out):
    sb = jnp.zeros((1, LANES), F32).at[0, SM_FF:SM_FF + N_HEADS].set(b_fox_f[l])
    wgk = jnp.zeros((LANES, D_FOX), F32).at[SM_GLR:SM_GLR + GLA_RANK].set(w_gla_gk[l]).astype(BF16)
    return dict(g=norm_mix_g[l].reshape(1, D_MODEL), w_r=_reorder_w_in(w_in[l]), sb=sb, wgk=wgk,
                bgk=b_gla_gk[l].reshape(1, D_FOX), w_cmp=w_cmp[l],
                pe_cmp=pe_cmp[l].reshape(2, 1, CMP_LEN * HEAD_DIM),
                gn=g_gla_norm[l].reshape(1, GLA_DV), w_out=w_out[l].astype(BF16))


def _mix_prompt(x, B, S, p, tab, tm, t_fox, tk_sel, tc_gla):
    (fq, fkv, nq, nqr, rows, win, gqk, gv, gog, glog, small) = _in_projection(
        x, 0, B * S, tm, p['g'], p['w_r'], p['sb'], p['wgk'], p['bgk'], tab, S)
    cr = _fox_cumsum(small, B, S, min(S, 512))
    o_fox = _fox_prompt(fq, fkv, cr, B, S, t_fox)
    n_blk = B * S // CMP_LEN
    blocks = _block_major(rows[:, 0:2 * HEAD_DIM].reshape(B * S, 2, HEAD_DIM), n_blk)
    cmp = _compress(blocks, p['pe_cmp'], p['w_cmp'], min(256, n_blk), True)
    o_nsa = _nsa_prompt(nq, nqr, small, cmp, rows, win, B, S, tk_sel)
    o_gla, g_state = _gla_prompt(gqk, gv, glog, B, S, tc_gla)
    x_new = _out_projection(x, o_fox, o_nsa, o_gla, gog, p['gn'], p['w_out'], tm)
    return x_new, dict(fkv=fkv, small=small, rows=rows, win=win, g_state=g_state)


def _per_head_col(vals):
    r = lax.broadcasted_iota(I32, (SUBLANES, 1), 0)
    out = jnp.zeros((SUBLANES, 1), F32)
    for h, v in enumerate(vals):
        out = out + jnp.where(r == h, v, 0.0)
    return out


def _fox_decode_kernel(pt_ref, q_ref, kvn_ref, smn_ref, *refs, n_pages):
    del pt_ref
    kv_refs = refs[0:n_pages]
    lf_refs = refs[n_pages:2 * n_pages]
    o_ref, lf_sc, s_sc = refs[2 * n_pages:]
    R = n_pages * SUBLANES
    PG = kv_refs[0].shape[-1]

    lf_sc[...] = jnp.zeros_like(lf_sc)
    for p in range(n_pages):
        lf_sc[p * SUBLANES:p * SUBLANES + N_HEADS, :] = lf_refs[p][...]
    lft = lf_sc[...]
    k0 = lax.broadcasted_iota(I32, (PG, PG), 0)
    k1 = lax.broadcasted_iota(I32, (PG, PG), 1)
    within = jnp.dot(lft, (k0 > k1).astype(F32), precision=HI, preferred_element_type=F32)
    tot = jnp.broadcast_to(jnp.sum(lft, axis=1, keepdims=True), (R, PG))
    r0 = lax.broadcasted_iota(I32, (R, R), 0)
    r1 = lax.broadcasted_iota(I32, (R, R), 1)
    later = ((r1 % SUBLANES == r0 % SUBLANES) & (r1 // SUBLANES > r0 // SUBLANES)).astype(F32)
    cross = jnp.dot(later, tot, precision=HI, preferred_element_type=F32)
    rr = lax.broadcasted_iota(I32, (R, 1), 0) % SUBLANES
    smn = smn_ref[...]
    newcol = jnp.zeros((R, 1), F32)
    for h in range(N_HEADS):
        newcol = newcol + jnp.where(rr == h, smn[:, SM_FF + h:SM_FF + h + 1], 0.0)
    bias = (within + cross + newcol).reshape(n_pages, SUBLANES, PG)

    s_sc[...] = jnp.zeros_like(s_sc)
    for p in range(n_pages):
        for h in range(N_HEADS):
            s_sc[p, h:h + 1, :] = jnp.sum(kv_refs[p][0, h] * q_ref[h], axis=0, keepdims=True)
    s3 = s_sc[...] * SCALE + bias
    s_new = _per_head_col([jnp.sum(q_ref[h] * kvn_ref[0, h], axis=0, keepdims=True)
                           for h in range(N_HEADS)]) * SCALE
    m = jnp.max(jnp.max(s3, axis=2, keepdims=True), axis=0)
    m = jnp.maximum(m, s_new)
    p3 = jnp.exp(s3 - m[None])
    pn = jnp.exp(s_new - m)
    den = jnp.sum(jnp.sum(p3, axis=2, keepdims=True), axis=0) + pn
    for h in range(N_HEADS):
        acc = jnp.zeros((HEAD_DIM, PG), F32)
        for p in range(n_pages):
            acc = acc + kv_refs[p][1, h] * p3[p, h:h + 1, :]
        o_ref[h] = ((jnp.sum(acc, axis=1, keepdims=True) + pn[h:h + 1] * kvn_ref[1, h])
                    / den[h:h + 1])


def _fox_decode(l, pt_flat, n_pages, fq, fkv, small, kv_cache_t, lft_cache):
    DB = fq.shape[0]
    PG = kv_cache_t.shape[-1]
    page = lambda p, nz: (lambda b, pt: (l, pt[b * n_pages + p]) + (0,) * nz)
    per_b = lambda *s: pl.BlockSpec((None,) + s, lambda b, pt: (b,) + (0,) * len(s))
    return pl.pallas_call(
        functools.partial(_fox_decode_kernel, n_pages=n_pages),
        grid_spec=pltpu.PrefetchScalarGridSpec(
            num_scalar_prefetch=1,
            grid=(DB,),
            in_specs=[per_b(N_HEADS, HEAD_DIM, 1), per_b(2, N_HEADS, HEAD_DIM, 1), per_b(1, LANES)]
            + [pl.BlockSpec((None, None, 2, N_HEADS, HEAD_DIM, PG), page(p, 4)) for p in range(n_pages)]
            + [pl.BlockSpec((None, None, N_HEADS, PG), page(p, 2)) for p in range(n_pages)],
            out_specs=per_b(N_HEADS, HEAD_DIM, 1),
            scratch_shapes=[pltpu.VMEM((n_pages * SUBLANES, PG), F32),
                            pltpu.VMEM((n_pages, SUBLANES, PG), F32)],
        ),
        out_shape=jax.ShapeDtypeStruct((DB, N_HEADS, HEAD_DIM, 1), F32),
        compiler_params=_params(("arbitrary",)),
        name="fox_decode",
    )(pt_flat, fq.reshape(DB, N_HEADS, HEAD_DIM, 1), fkv.reshape(DB, 2, N_HEADS, HEAD_DIM, 1),
      small.reshape(DB, 1, LANES), *([kv_cache_t] * n_pages), *([lft_cache] * n_pages)).reshape(DB, D_FOX)


def _nsa_decode_kernel(pt_ref, q_ref, qr_ref, rown_ref, winn_ref, gate_ref, win_ref, *refs,
                       n_pages, past_len):
    del pt_ref
    pg_refs = refs[0:n_pages]
    cmp_refs = refs[n_pages:2 * n_pages]
    o_ref, nw_ref, cmp_sc, qc_sc, s_sc, sw_sc = refs[2 * n_pages:]
    R = n_pages * SUBLANES
    PG = pg_refs[0].shape[-1]
    WB = win_ref.shape[-1]
    per_page = PG // CMP_LEN
    assert per_page <= SUBLANES and PG == 2 * SEL_LEN and R == LANES
    jt = past_len // SEL_LEN

    qc_sc[...] = jnp.zeros_like(qc_sc)
    qc_sc[0:N_HEADS, 0:HEAD_DIM] = q_ref[...]
    head_row = lax.broadcasted_iota(I32, (SUBLANES, 1), 0) < N_HEADS

    cmp_sc[...] = jnp.zeros_like(cmp_sc)
    for p in range(n_pages):
        cmp_sc[p * SUBLANES:p * SUBLANES + per_page, :] = cmp_refs[p][...]
    cmpa = cmp_sc[...]
    lane = lax.broadcasted_iota(I32, (1, R), 1)
    blk = per_page * (lane // SUBLANES) + lane % SUBLANES
    complete = (lane % SUBLANES < per_page) & ((blk + 1) * CMP_LEN - 1 <= past_len)
    s = lax.dot_general(qc_sc[...], cmpa, NT, precision=HI, preferred_element_type=F32) * SCALE
    s = jnp.where(complete, s, NEG)
    e = jnp.exp(s - jnp.max(s, axis=-1, keepdims=True))
    pc = e / jnp.sum(e, axis=-1, keepdims=True) * complete.astype(F32)
    vcb_t = cmpa.T[HEAD_DIM:2 * HEAD_DIM, :]
    o_cmp = [jnp.sum(vcb_t * pc[h:h + 1, :], axis=1, keepdims=True) for h in range(N_HEADS)]

    imp_c = jnp.sum(jnp.where(head_row, pc, 0.0), axis=0, keepdims=True)
    imp_s = imp_c + pltpu.roll(imp_c, R - 1, 1)
    cand = (lane % SUBLANES == 0) | (lane % SUBLANES == 2)
    jsel = 2 * (lane // SUBLANES) + (lane % SUBLANES) // 2
    score = jnp.where(jsel == jt, 2.0 * SEL_FORCE,
                      jnp.where((jsel == 0) | (jsel == jt - 1), SEL_FORCE,
                                jnp.where(jsel <= jt, imp_s + 0.0, -1.0)))
    score_b = jnp.broadcast_to(score, (R, R))
    key_row = _order_key(score_b)
    key_col = _order_key(score_b.T)
    l0 = lax.broadcasted_iota(I32, (R, R), 0)
    l1 = lax.broadcasted_iota(I32, (R, R), 1)
    cand_col = (l0 % SUBLANES == 0) | (l0 % SUBLANES == 2)
    beats = cand_col & (key_col > jnp.where(l0 < l1, key_row - 1, key_row))
    cnt = jnp.sum(beats.astype(I32), axis=0, keepdims=True)
    sel_row = (cand & (cnt < TOP_N - 1)).astype(F32)
    sel_col = jnp.broadcast_to(sel_row, (R, R)).T
    half = ((l0 % SUBLANES == 0) & (l1 < SEL_LEN)) | ((l0 % SUBLANES == 2) & (l1 >= SEL_LEN))
    z = jnp.where(half, sel_col, 0.0)
    same_page = (l1 // SUBLANES == l0 // SUBLANES).astype(BF16)
    picked = jnp.dot(same_page, z.astype(BF16), preferred_element_type=F32)
    picked = picked.reshape(n_pages, SUBLANES, PG) > 0.5

    s_sc[...] = jnp.zeros_like(s_sc)
    for p in range(n_pages):
        ks_t = pg_refs[p][2]
        for h in range(N_HEADS):
            s_sc[p, h:h + 1, :] = jnp.sum(ks_t * qr_ref[h], axis=0, keepdims=True)
    s3 = jnp.where(picked, s_sc[...] * SCALE, NEG)
    s_new = _per_head_col([jnp.sum(qr_ref[h] * rown_ref[2], axis=0, keepdims=True)
                           for h in range(N_HEADS)]) * SCALE
    m = jnp.maximum(jnp.max(jnp.max(s3, axis=2, keepdims=True), axis=0), s_new)
    p3 = jnp.exp(s3 - m[None])
    pn = jnp.exp(s_new - m)
    den = jnp.sum(jnp.sum(p3, axis=2, keepdims=True), axis=0) + pn
    o_sel = []
    for h in range(N_HEADS):
        acc = jnp.zeros((HEAD_DIM, PG), F32)
        for p in range(n_pages):
            acc = acc + pg_refs[p][3] * p3[p, h:h + 1, :]
        o_sel.append((jnp.sum(acc, axis=1, keepdims=True) + pn[h:h + 1] * rown_ref[3]) / den[h:h + 1])

    kw_t = win_ref[0]
    vw_t = win_ref[1]
    wlane = lax.broadcasted_iota(I32, (1, WB), 1)
    wpos = past_len - WB + wlane
    wd = past_len - wpos
    wok = (wd >= 0) & (wd < WINDOW) & (wpos >= 0)
    sw_sc[...] = jnp.zeros_like(sw_sc)
    for h in range(N_HEADS):
        sw_sc[h:h + 1, :] = jnp.sum(kw_t * qr_ref[h], axis=0, keepdims=True)
    sw = jnp.where(wok, sw_sc[...] * SCALE, NEG)
    sw_new = _per_head_col([jnp.sum(qr_ref[h] * winn_ref[0], axis=0, keepdims=True)
                            for h in range(N_HEADS)]) * SCALE
    mw = jnp.maximum(jnp.max(sw, axis=-1, keepdims=True), sw_new)
    ew = jnp.exp(sw - mw)
    en = jnp.exp(sw_new - mw)
    denw = jnp.sum(ew, axis=-1, keepdims=True) + en

    g = gate_ref[...]
    for h in range(N_HEADS):
        o_win = ((jnp.sum(vw_t * ew[h:h + 1, :], axis=1, keepdims=True) + en[h:h + 1] * winn_ref[1])
                 / denw[h:h + 1])
        o_ref[h] = (g[h:h + 1, 0:1] * o_cmp[h] + g[h:h + 1, 1:2] * o_sel[h] + g[h:h + 1, 2:3] * o_win)
    last = lax.broadcasted_iota(I32, (HEAD_DIM, WB), 1) == WB - 1
    for s in range(2):
        nw_ref[s] = jnp.where(last, winn_ref[s], pltpu.roll(win_ref[s], WB - 1, 1))


def _nsa_decode(l, pt_flat, n_pages, past_len, nq, nqr, rows, win, small, nsa_cache_t, cmp_pool, win_state_t):
    DB = nq.shape[0]
    PG = nsa_cache_t.shape[-1]
    WB = win_state_t.shape[-1]
    page = lambda p: (lambda b, pt: (l, pt[b * n_pages + p], 0, 0, 0))
    cpage = lambda p: (lambda b, pt: (pt[b * n_pages + p], 0, 0))
    per_b = lambda *s: pl.BlockSpec((None,) + s, lambda b, pt: (b,) + (0,) * len(s))
    gates = small[:, SM_NG:SM_NG + 3 * N_HEADS].reshape(DB, N_HEADS, 3)
    o, nw = pl.pallas_call(
        functools.partial(_nsa_decode_kernel, n_pages=n_pages, past_len=past_len),
        grid_spec=pltpu.PrefetchScalarGridSpec(
            num_scalar_prefetch=1,
            grid=(DB,),
            in_specs=[per_b(N_HEADS, HEAD_DIM), per_b(N_HEADS, HEAD_DIM, 1), per_b(4, HEAD_DIM, 1),
                      per_b(2, HEAD_DIM, 1), per_b(N_HEADS, 3),
                      pl.BlockSpec((None, None, 2, HEAD_DIM, WB), lambda b, pt: (l, b, 0, 0, 0))]
            + [pl.BlockSpec((None, None, 4, HEAD_DIM, PG), page(p)) for p in range(n_pages)]
            + [pl.BlockSpec((None, PG // CMP_LEN, LANES), cpage(p)) for p in range(n_pages)],
            out_specs=[per_b(N_HEADS, HEAD_DIM, 1), per_b(2, HEAD_DIM, WB)],
            scratch_shapes=[pltpu.VMEM((n_pages * SUBLANES, LANES), F32),
                            pltpu.VMEM((SUBLANES, LANES), F32),
                            pltpu.VMEM((n_pages, SUBLANES, PG), F32),
                            pltpu.VMEM((SUBLANES, WB), F32)],
        ),
        out_shape=[jax.ShapeDtypeStruct((DB, N_HEADS, HEAD_DIM, 1), F32),
                   jax.ShapeDtypeStruct((DB, 2, HEAD_DIM, WB), F32)],
        compiler_params=_params(("arbitrary",)),
        name="nsa_decode",
    )(pt_flat, nq.reshape(DB, N_HEADS, HEAD_DIM), nqr.reshape(DB, N_HEADS, HEAD_DIM, 1),
      rows.reshape(DB, 4, HEAD_DIM, 1), win.reshape(DB, 2, HEAD_DIM, 1), gates, win_state_t,
      *([nsa_cache_t] * n_pages), *([cmp_pool] * n_pages))
    return o.reshape(DB, D_NSA), nw


def _gla_decode_kernel(q_ref, k_ref, g_ref, v_ref, s_ref, o_ref, so_ref):
    s_new = jnp.exp(g_ref[...]) * s_ref[...] + k_ref[...] * v_ref[...]
    so_ref[...] = s_new
    o_ref[...] = jnp.sum((q_ref[...] * SCALE) * s_new, axis=2, keepdims=True)


def _gla_decode(l, gqk, gv, glog, state, nb):
    DB = gqk.shape[0]
    col = lambda a: a.reshape(DB, N_HEADS, HEAD_DIM, 1)
    cspec = pl.BlockSpec((nb, N_HEADS, HEAD_DIM, 1), lambda i: (i, 0, 0, 0))
    vspec = pl.BlockSpec((nb, N_HEADS, 1, GLA_DV), lambda i: (i, 0, 0, 0))
    sspec = pl.BlockSpec((nb, N_HEADS, HEAD_DIM, GLA_DV), lambda i: (i, 0, 0, 0))
    o, s_new = pl.pallas_call(
        _gla_decode_kernel,
        grid=(DB // nb,),
        in_specs=[cspec, cspec, cspec, vspec,
                  pl.BlockSpec((None, nb, N_HEADS, HEAD_DIM, GLA_DV), lambda i: (l, i, 0, 0, 0))],
        out_specs=[vspec, sspec],
        out_shape=[jax.ShapeDtypeStruct((DB, N_HEADS, 1, GLA_DV), F32),
                   jax.ShapeDtypeStruct((DB, N_HEADS, HEAD_DIM, GLA_DV), F32)],
        compiler_params=_params(("parallel",)),
        name="gla_decode",
    )(col(gqk[:, 0:D_FOX]), col(gqk[:, D_FOX:2 * D_FOX]), col(glog),
      gv.reshape(DB, N_HEADS, 1, GLA_DV), state)
    return o.reshape(DB, D_GLA), s_new


def _mix_sample(x, l, p, tab, pt_flat, n_pages, past_len, fox_kv_t, fox_lft_c, nsa_t, cmp_blocks,
                win_state_t, gla_state):
    DB = x.shape[0]
    (fq, fkv, nq, nqr, rows, win, gqk, gv, gog, glog, small) = _in_projection(
        x, 0, DB, DB, p['g'], p['w_r'], p['sb'], p['wgk'], p['bgk'], tab, DB)
    o_fox = _fox_decode(l, pt_flat, n_pages, fq, fkv, small, fox_kv_t, fox_lft_c)
    n_pool, PG = nsa_t.shape[1], nsa_t.shape[-1]
    per_layer = n_pool * (PG // CMP_LEN)
    cmp_pool = _compress(cmp_blocks, p['pe_cmp'], p['w_cmp'], 256, False, row0=l * per_layer, n_rows=per_layer)
    cmp_pool = cmp_pool.reshape(n_pool, PG // CMP_LEN, LANES)
    o_nsa, new_win = _nsa_decode(l, pt_flat, n_pages, past_len, nq, nqr, rows, win, small,
                                 nsa_t, cmp_pool, win_state_t)
    o_gla, g_state = _gla_decode(l, gqk, gv, glog, gla_state, 8)
    x_new = _out_projection(x, o_fox, o_nsa, o_gla, gog, p['gn'], p['w_out'], DB)
    return x_new, dict(fkv=fkv, small=small, rows=rows, win=new_win, g_state=g_state)


def kernel(x_prompt, x_sample, cache_fox_kv, cache_fox_logf, cache_nsa_kv, state_nsa_win, state_gla,
           page_table, norm_mix_g, w_in, b_fox_f, w_cmp, pe_cmp, w_gla_gk, b_gla_gk, g_gla_norm, w_out,
           norm_ffn_g, dense_w_gate, dense_w_up, dense_w_down, moe_w_router, moe_w_gate, moe_w_up,
           moe_w_down, final_norm_g):
    B, S, _ = x_prompt.shape
    DB, TN, _ = x_sample.shape
    assert TN == 1
    depth, n_pool, PG = cache_fox_kv.shape[0:3]
    n_pages = page_table.shape[1]
    past_len = n_pages * PG
    WB = state_nsa_win.shape[2]
    xp = x_prompt.reshape(B * S, D_MODEL)
    xs = x_sample.reshape(DB, D_MODEL)
    tab_p = _rope_table(jnp.arange(S))
    tab_s = _rope_table(jnp.full((DB,), past_len, I32))
    pt_flat = page_table.reshape(-1).astype(I32)
    fox_kv_t = jnp.transpose(cache_fox_kv, (0, 1, 3, 4, 5, 2))
    fox_lft_c = jnp.swapaxes(cache_fox_logf, 2, 3)
    nsa_t = jnp.transpose(cache_nsa_kv, (0, 1, 3, 4, 2))
    win_state_t = jnp.transpose(state_nsa_win, (0, 1, 3, 4, 2))
    n_cmp = depth * n_pool * (PG // CMP_LEN)
    cmp_blocks = _block_major(cache_nsa_kv[:, :, :, 0:2, :].reshape(n_cmp * CMP_LEN, 2, HEAD_DIM), n_cmp)
    cp, cs = [], []
    for l in range(depth):
        p = _layer_mix_params(l, norm_mix_g, w_in, b_fox_f, w_cmp, pe_cmp, w_gla_gk, b_gla_gk,
                              g_gla_norm, w_out)
        xp, c = _mix_prompt(xp, B, S, p, tab_p, 512, 512, 512, 256)
        cp.append(c)
        xs, c = _mix_sample(xs, l, p, tab_s, pt_flat, n_pages, past_len, fox_kv_t, fox_lft_c, nsa_t,
                            cmp_blocks, win_state_t, state_gla)
        cs.append(c)
        gf = norm_ffn_g[l].reshape(1, D_MODEL)
        i = l // 2
        if l % 2 == 0:
            wg, wu, wd = (dense_w_gate[i].astype(BF16), dense_w_up[i].astype(BF16),
                          dense_w_down[i].astype(BF16))
            xp = _dense_ffn(xp, gf, wg, wu, wd, 1024, 512)
            xs = _dense_ffn(xs, gf, wg, wu, wd, DB, 512)
        else:
            wr = jnp.zeros((D_MODEL, LANES), F32).at[:, 0:N_EXPERTS].set(moe_w_router[i])
            xp, xs = _moe_layer([xp, xs], gf, wr, moe_w_gate[i], moe_w_up[i], moe_w_down[i],
                                [512, DB], 1024, 512)
    gfin = final_norm_g.reshape(1, D_MODEL)
    y_p = _final_norm(xp, gfin, 512).reshape(B, S, D_MODEL)
    y_s = _final_norm(xs, gfin, DB).reshape(DB, 1, D_MODEL)
    wp = min(WINDOW, S)
    st = lambda key, group: jnp.stack([c[key] for c in group])
    return (y_p, y_s,
            st('fkv', cp).reshape(depth, B, S, 2, N_HEADS, HEAD_DIM),
            st('small', cp)[:, :, SM_FF:SM_FF + N_HEADS].reshape(depth, B, S, N_HEADS),
            st('rows', cp).reshape(depth, B, S, 4, HEAD_DIM),
            st('win', cp).reshape(depth, B, S, 2, HEAD_DIM)[:, :, S - wp:],
            st('g_state', cp),
            st('fkv', cs).reshape(depth, DB, 1, 2, N_HEADS, HEAD_DIM),
            st('small', cs)[:, :, SM_FF:SM_FF + N_HEADS].reshape(depth, DB, 1, N_HEADS),
            st('rows', cs).reshape(depth, DB, 1, 4, HEAD_DIM),
            jnp.transpose(st('win', cs), (0, 1, 4, 2, 3)),
            st('g_state', cs))
```

```python
import functools

import jax
import jax.numpy as jnp
from jax import lax
from jax.experimental import pallas as pl
from jax.experimental.pallas import tpu as pltpu
from jax.experimental.pallas import tpu_sc as plsc

F32 = jnp.float32
BF16 = jnp.bfloat16
I32 = jnp.int32
HI = lax.Precision.HIGHEST

D_MODEL = 1024
HEAD_DIM = 64
N_HEADS = 4
D_FOX = N_HEADS * HEAD_DIM
D_NSA = N_HEADS * HEAD_DIM
GLA_DV = 128
D_GLA = N_HEADS * GLA_DV
GLA_RANK = 16
GLA_TAU = 16.0
GLA_CHUNK = 64
CMP_LEN = 32
SEL_LEN = 64
TOP_N = 16
WINDOW = 512
ROPE_THETA = 500000.0
ROPE_DIM = HEAD_DIM // 4
ROPE_HALF = ROPE_DIM // 2
D_FF = 3584
N_EXPERTS = 8
EPS = 1e-6
SEL_FORCE = 1e9
NEG = -1e30
SCALE = HEAD_DIM ** -0.5
LOG2E = 1.4426950408889634

LANES = 128
SUBLANES = 8
VMEM_LIMIT = 56 * 1024 * 1024

C_FQ = 0
C_FKV = 256
C_NQ = 768
C_NKV = 1024
C_GQK = 1408
C_GV = 1920
C_GOG = 2432
C_SMALL = 2944
C_END = 3072
SM_FF = 0
SM_NG = 4
SM_GLR = 16

NT = (((1,), (1,)), ((), ()))


def _params(sem):
    return pltpu.CompilerParams(dimension_semantics=sem, vmem_limit_bytes=VMEM_LIMIT)


def _rms(x, g):
    ms = jnp.mean(x * x, axis=-1, keepdims=True)
    return x * lax.rsqrt(ms + EPS) * g


def _sigmoid(x):
    return 1.0 / (1.0 + jnp.exp(-x))


def _log_sigmoid(x):
    return -(jnp.maximum(-x, 0.0) + jnp.log1p(jnp.exp(-jnp.abs(x))))


def _silu(x):
    return x * _sigmoid(x)


def _bdot(a, b):
    return jnp.dot(a.astype(BF16), b.astype(BF16), preferred_element_type=F32)


def _bdot_nt(a, b):
    return lax.dot_general(a.astype(BF16), b.astype(BF16), NT, preferred_element_type=F32)


def _wdot(a, w):
    if w.dtype == F32:
        return jnp.dot(a.astype(F32), w, precision=HI, preferred_element_type=F32)
    return jnp.dot(a.astype(BF16), w, preferred_element_type=F32)


def _rope128(x, a, bp, bm):
    return x * a + pltpu.roll(x, ROPE_HALF, 1) * bp + pltpu.roll(x, LANES - ROPE_HALF, 1) * bm


def _inproj_kernel(x_ref, g_ref, w_ref, sb_ref, wgk_ref, bgk_ref, tab_ref,
                   fq_ref, fkv_ref, nq_ref, nqr_ref, rows_ref, win_ref,
                   gqk_ref, gv_ref, gog_ref, glog_ref, small_ref):
    h = _rms(x_ref[...], g_ref[...]).astype(w_ref.dtype)

    def mm(a, b):
        return _wdot(h, w_ref[:, a:b])

    fq_ref[...] = mm(C_FQ, C_FKV)
    fkv_ref[...] = mm(C_FKV, C_NQ)
    tab = tab_ref[...]
    ab, pb, mb = tab[:, 0:128], tab[:, 128:256], tab[:, 256:384]
    af, pf, mf = tab[:, 384:512], tab[:, 512:640], tab[:, 640:768]
    nq = mm(C_NQ, C_NKV)
    nq_ref[...] = nq
    nqr_ref[:, 0:128] = _rope128(nq[:, 0:128], ab, pb, mb)
    nqr_ref[:, 128:256] = _rope128(nq[:, 128:256], ab, pb, mb)
    nkv = mm(C_NKV, C_GQK)
    rows_ref[:, 0:128] = nkv[:, 0:128]
    rows_ref[:, 128:256] = _rope128(nkv[:, 128:256], af, pf, mf)
    win_ref[...] = _rope128(nkv[:, 256:384], af, pf, mf)
    gqk_ref[...] = mm(C_GQK, C_GV)
    gv_ref[...] = mm(C_GV, C_GOG)
    gog_ref[...] = mm(C_GOG, C_SMALL)
    sm = mm(C_SMALL, C_END)
    glog_ref[...] = _log_sigmoid(_wdot(sm, wgk_ref[...]) + bgk_ref[...]) * (1.0 / GLA_TAU)
    smb = sm + sb_ref[...]
    lane = lax.broadcasted_iota(I32, smb.shape, 1)
    small_ref[...] = jnp.where(lane < SM_NG, _log_sigmoid(smb), _sigmoid(smb))


def _in_projection(x_all, row0, n_rows, tm, g, w_r, sb, wgk, bgk, tab, tab_period):
    assert n_rows % tm == 0 and row0 % tm == 0 and tab_period % tm == 0
    nt = n_rows // tm
    b0 = row0 // tm
    npd = tab_period // tm
    widths = (256, 512, 256, 256, 256, 128, 512, 512, 512, 256, 128)
    full = lambda shape: pl.BlockSpec(shape, lambda i: (0, 0))
    return pl.pallas_call(
        _inproj_kernel,
        grid=(nt,),
        in_specs=[
            pl.BlockSpec((tm, D_MODEL), lambda i: (b0 + i, 0)),
            full((1, D_MODEL)),
            full((D_MODEL, C_END)),
            full((1, LANES)),
            full((LANES, 256)),
            full((1, 256)),
            pl.BlockSpec((tm, 768), lambda i: (i % npd, 0)),
        ],
        out_specs=[pl.BlockSpec((tm, w), lambda i: (i, 0)) for w in widths],
        out_shape=[jax.ShapeDtypeStruct((n_rows, w), F32) for w in widths],
        compiler_params=_params(("parallel",)),
        name="in_projection",
    )(x_all, g, w_r, sb, wgk, bgk, tab)


def _cumsum_kernel(sm_ref, cr_ref, carry):
    t = pl.program_id(1)
    ts = sm_ref.shape[0]

    @pl.when(t == 0)
    def _():
        carry[...] = jnp.zeros_like(carry)

    r = lax.broadcasted_iota(I32, (ts, ts), 0)
    c = lax.broadcasted_iota(I32, (ts, ts), 1)
    tri = (c <= r).astype(F32)
    cs = jnp.dot(tri, sm_ref[...], precision=HI, preferred_element_type=F32) + carry[...]
    carry[...] = cs[ts - 1:ts, :]
    cr_ref[...] = cs.T[0:SUBLANES, :] * LOG2E


def _fox_cumsum(small, B, S, ts):
    ns = S // ts
    return pl.pallas_call(
        _cumsum_kernel,
        grid=(B, ns),
        in_specs=[pl.BlockSpec((ts, LANES), lambda b, t: (b * ns + t, 0))],
        out_specs=pl.BlockSpec((None, SUBLANES, ts), lambda b, t: (b, 0, t)),
        out_shape=jax.ShapeDtypeStruct((B, SUBLANES, S), F32),
        scratch_shapes=[pltpu.VMEM((1, LANES), F32)],
        compiler_params=_params(("parallel", "arbitrary")),
        name="fox_cumsum",
    )(small)


def _pair_mask(shape, h):
    return (lax.broadcasted_iota(I32, shape, 1) // HEAD_DIM) == (h % 2)


def _fox_prompt_kernel(q_ref, kv_ref, cr_ref, o_ref, q_sc, m_sc, acc_sc):
    i = pl.program_id(1)
    j = pl.program_id(2)
    nk = pl.num_programs(2)
    tq = q_ref.shape[0]
    tk = kv_ref.shape[0]

    @pl.when(j == 0)
    def _():
        m_sc[...] = jnp.full_like(m_sc, NEG)
        acc_sc[...] = jnp.zeros_like(acc_sc)
        for h in range(N_HEADS):
            slab = q_ref[:, (h // 2) * LANES:(h // 2 + 1) * LANES] * (SCALE * LOG2E)
            q_sc[h] = jnp.where(_pair_mask(slab.shape, h), slab, 0.0).astype(BF16)

    def tile(diagonal):
        if diagonal:
            mask = lax.broadcasted_iota(I32, (1, tk), 1) <= lax.broadcasted_iota(I32, (tq, 1), 0)
        for h in range(N_HEADS):
            c0 = (h // 2) * LANES
            k_slab = kv_ref[:, c0:c0 + LANES].astype(BF16)
            s = lax.dot_general(q_sc[h], k_slab, NT, preferred_element_type=F32) - cr_ref[h:h + 1, :]
            if diagonal:
                s = jnp.where(mask, s, NEG)
            m_old = m_sc[h]
            m_new = jnp.maximum(m_old, jnp.max(s, axis=-1, keepdims=True))
            p = jnp.exp2(s - m_new).astype(BF16)
            v_slab = kv_ref[:, D_FOX + c0:D_FOX + c0 + LANES]
            v_aug = jnp.where(_pair_mask(v_slab.shape, h), v_slab, 1.0).astype(BF16)
            acc_sc[h] = jnp.exp2(m_old - m_new) * acc_sc[h] + jnp.dot(p, v_aug, preferred_element_type=F32)
            m_sc[h] = m_new

    @pl.when(j < i)
    def _():
        tile(False)

    @pl.when(j == i)
    def _():
        tile(True)

    @pl.when(j == nk - 1)
    def _():
        for h in range(N_HEADS):
            a = acc_sc[h]
            lo = (h % 2) * HEAD_DIM
            den = a[:, HEAD_DIM - lo:HEAD_DIM - lo + 1]
            o_ref[:, h * HEAD_DIM:(h + 1) * HEAD_DIM] = a[:, lo:lo + HEAD_DIM] / den


def _fox_prompt(fq, fkv, cr, B, S, t):
    n = S // t
    return pl.pallas_call(
        _fox_prompt_kernel,
        grid=(B, n, n),
        in_specs=[
            pl.BlockSpec((t, D_FOX), lambda b, i, j: (b * n + i, 0)),
            pl.BlockSpec((t, 2 * D_FOX), lambda b, i, j: (b * n + jnp.minimum(i, j), 0)),
            pl.BlockSpec((None, SUBLANES, t), lambda b, i, j: (b, 0, jnp.minimum(i, j))),
        ],
        out_specs=pl.BlockSpec((t, D_FOX), lambda b, i, j: (b * n + i, 0)),
        out_shape=jax.ShapeDtypeStruct((B * S, D_FOX), F32),
        scratch_shapes=[pltpu.VMEM((N_HEADS, t, LANES), BF16), pltpu.VMEM((N_HEADS, t, 1), F32),
                        pltpu.VMEM((N_HEADS, t, LANES), F32)],
        compiler_params=_params(("parallel", "parallel", "arbitrary")),
        name="fox_prompt",
    )(fq, fkv, cr)


def _compress_kernel(x_ref, pe_ref, w_ref, o_ref, *, exact):
    for s in range(2):
        x = x_ref[s] + pe_ref[s]
        w = w_ref[s]
        if exact:
            y = jnp.dot(x, w, precision=HI, preferred_element_type=F32)
        else:
            xh = x.astype(BF16)
            xl = (x - xh.astype(F32)).astype(BF16)
            wh = w.astype(BF16)
            wl = (w - wh.astype(F32)).astype(BF16)
            y = (jnp.dot(xh, wh, preferred_element_type=F32) + jnp.dot(xl, wh, preferred_element_type=F32)
                 + jnp.dot(xh, wl, preferred_element_type=F32))
        o_ref[:, s * HEAD_DIM:(s + 1) * HEAD_DIM] = y


def _compress(x3, pe, w, tr, exact, row0=0, n_rows=None):
    K = x3.shape[2]
    R = x3.shape[1] if n_rows is None else n_rows
    assert R % tr == 0 and row0 % tr == 0
    b0 = row0 // tr
    return pl.pallas_call(
        functools.partial(_compress_kernel, exact=exact),
        grid=(R // tr,),
        in_specs=[pl.BlockSpec((2, tr, K), lambda i: (0, b0 + i, 0)),
                  pl.BlockSpec((2, 1, K), lambda i: (0, 0, 0)),
                  pl.BlockSpec((2, K, HEAD_DIM), lambda i: (0, 0, 0))],
        out_specs=pl.BlockSpec((tr, LANES), lambda i: (i, 0)),
        out_shape=jax.ShapeDtypeStruct((R, LANES), F32),
        compiler_params=_params(("parallel",)),
        name="nsa_compress",
    )(x3, pe, w)


def _block_major(kv, n_blocks):
    return kv.reshape(n_blocks, CMP_LEN, 2, HEAD_DIM).transpose(2, 0, 1, 3).reshape(
        2, n_blocks, CMP_LEN * HEAD_DIM)


def _order_key(x):
    b = lax.bitcast_convert_type(x, I32)
    return jnp.where(b < 0, b ^ jnp.int32(0x7FFFFFFF), b)


def _nsa_prompt_kernel(nq_ref, nqr_ref, sm_ref, cmp_ref, rows_ref, win_ref, o_ref, qx_sc, *, tk):
    QB = nq_ref.shape[0]
    S = rows_ref.shape[0]
    nb = cmp_ref.shape[0]
    nsel = S // SEL_LEN
    i = pl.program_id(1)
    qs = i * QB
    qpos = qs + lax.broadcasted_iota(I32, (QB, 1), 0)

    cmp = cmp_ref[...]
    kc = cmp[:, 0:HEAD_DIM]
    vc = cmp[:, HEAD_DIM:2 * HEAD_DIM]
    n_l = lax.broadcasted_iota(I32, (1, nb), 1)
    complete = ((n_l + 1) * CMP_LEN - 1) <= qpos
    complete_f = complete.astype(F32)
    psum = jnp.zeros((QB, nb), F32)
    o_cmp = []
    for h in range(N_HEADS):
        qh = nq_ref[:, h * HEAD_DIM:(h + 1) * HEAD_DIM]
        s = lax.dot_general(qh, kc, NT, precision=HI, preferred_element_type=F32) * SCALE
        s = jnp.where(complete, s, NEG)
        e = jnp.exp(s - jnp.max(s, axis=-1, keepdims=True))
        p = e / jnp.sum(e, axis=-1, keepdims=True) * complete_f
        o_cmp.append(_bdot(p, cmp))
        psum = psum + p

    pj = lax.broadcasted_iota(I32, (nsel, nb), 0)
    pn = lax.broadcasted_iota(I32, (nsel, nb), 1)
    pair_t = (pn // (SEL_LEN // CMP_LEN) == pj).astype(F32)
    imp_t = lax.dot_general(pair_t, psum, NT, precision=HI, preferred_element_type=F32)
    jt = (qs + lax.broadcasted_iota(I32, (1, QB), 1)) // SEL_LEN
    jj = lax.broadcasted_iota(I32, (nsel, 1), 0)
    score = jnp.where(jj == jt, 2.0 * SEL_FORCE,
                      jnp.where((jj == 0) | (jj == jt - 1), SEL_FORCE,
                                jnp.where(jj <= jt, imp_t + 0.0, -1.0)))
    key = _order_key(score)
    key_m1 = key - 1
    ngrp = nsel // SUBLANES
    sub = lax.broadcasted_iota(I32, (SUBLANES, QB), 0)
    kg = [key[r * SUBLANES:(r + 1) * SUBLANES, :] for r in range(ngrp)]
    kg1 = [key_m1[r * SUBLANES:(r + 1) * SUBLANES, :] for r in range(ngrp)]
    cnt = [jnp.zeros((SUBLANES, QB), I32) for _ in range(ngrp)]
    for jp in range(nsel):
        g = jp // SUBLANES
        row = key[jp:jp + 1, :]
        mixed = jnp.where(sub > (jp % SUBLANES), kg1[g], kg[g])
        for r in range(ngrp):
            thr = kg[r] if r < g else (kg1[r] if r > g else mixed)
            cnt[r] = cnt[r] + (row > thr).astype(I32)
    sel_t = jnp.concatenate([(c < TOP_N).astype(F32) for c in cnt], axis=0)
    if nsel < QB:
        sel_t = jnp.concatenate([sel_t, jnp.zeros((QB - nsel, QB), F32)], axis=0)
    sel = sel_t.T.astype(BF16)

    lo_half = lax.broadcasted_iota(I32, (QB, LANES), 1) < HEAD_DIM
    for h in range(N_HEADS):
        slab = nqr_ref[:, (h // 2) * LANES:(h // 2 + 1) * LANES] * (SCALE * LOG2E)
        if h % 2:
            slab = pltpu.roll(slab, HEAD_DIM, 1)
        qx_sc[h * QB:(h + 1) * QB, :] = jnp.where(lo_half, slab, 0.0).astype(BF16)
    qx = qx_sc[...]
    HQ = N_HEADS * QB

    def attend(s, valid, slab, m_old, acc_old):
        n = s.shape[1]
        s = jnp.where(valid[None], s.reshape(N_HEADS, QB, n), NEG).reshape(HQ, n)
        m_new = jnp.maximum(m_old, jnp.max(s, axis=-1, keepdims=True))
        p = jnp.exp2(s - m_new).astype(BF16)
        ones_k = lax.broadcasted_iota(I32, slab.shape, 1) < HEAD_DIM
        v_aug = jnp.where(ones_k, 1.0, slab).astype(BF16)
        acc = jnp.exp2(m_old - m_new) * acc_old + jnp.dot(p, v_aug, preferred_element_type=F32)
        return m_new, acc

    jrow = lax.broadcasted_iota(I32, (QB, 1), 0)

    def sel_tile(k0, m_old, acc_old, diagonal):
        kpos = k0 + lax.broadcasted_iota(I32, (1, tk), 1)
        expand = (jrow == kpos // SEL_LEN).astype(BF16)
        valid = jnp.dot(sel, expand, preferred_element_type=F32) > 0.5
        if diagonal:
            valid = valid & (kpos <= qpos)
        slab = rows_ref[pl.ds(k0, tk), 2 * HEAD_DIM:4 * HEAD_DIM]
        s = lax.dot_general(qx, slab.astype(BF16), NT, preferred_element_type=F32)
        return attend(s, valid, slab, m_old, acc_old)

    n_full = qs // tk
    init = (jnp.full((HQ, 1), NEG, F32), jnp.zeros((HQ, LANES), F32))
    m_s, acc_s = lax.fori_loop(
        0, n_full, lambda t, c: sel_tile(pl.multiple_of(t * tk, tk), c[0], c[1], False), init)
    _, acc_s = sel_tile(pl.multiple_of(n_full * tk, tk), m_s, acc_s, True)

    wlen = WINDOW + QB
    w0 = pl.multiple_of(jnp.maximum(qs - WINDOW, 0), QB)
    wpos = w0 + lax.broadcasted_iota(I32, (1, wlen), 1)
    d = qpos - wpos
    wslab = win_ref[pl.ds(w0, wlen), :]
    sw = lax.dot_general(qx, wslab.astype(BF16), NT, preferred_element_type=F32)
    _, acc_w = attend(sw, (d >= 0) & (d < WINDOW), wslab,
                      jnp.full((HQ, 1), NEG, F32), jnp.zeros((HQ, LANES), F32))

    sm = sm_ref[...]
    for h in range(N_HEADS):
        rs = slice(h * QB, (h + 1) * QB)
        o_sel = acc_s[rs] * (1.0 / acc_s[rs, 0:1])
        o_win = acc_w[rs] * (1.0 / acc_w[rs, 0:1])
        c = SM_NG + 3 * h
        mix = sm[:, c:c + 1] * o_cmp[h] + sm[:, c + 1:c + 2] * o_sel + sm[:, c + 2:c + 3] * o_win
        if h % 2 == 0:
            mix = pltpu.roll(mix, HEAD_DIM, 1)
        lo = (h % 2) * HEAD_DIM
        o_ref[:, h * HEAD_DIM:(h + 1) * HEAD_DIM] = mix[:, lo:lo + HEAD_DIM]


def _nsa_prompt(nq, nqr, small, cmp, rows, win, B, S, tk):
    QB = 128
    nq_t = S // QB
    nb = S // CMP_LEN
    assert S % tk == 0 and S >= WINDOW + QB
    return pl.pallas_call(
        functools.partial(_nsa_prompt_kernel, tk=tk),
        grid=(B, nq_t),
        in_specs=[
            pl.BlockSpec((QB, D_NSA), lambda b, i: (b * nq_t + i, 0)),
            pl.BlockSpec((QB, D_NSA), lambda b, i: (b * nq_t + i, 0)),
            pl.BlockSpec((QB, LANES), lambda b, i: (b * nq_t + i, 0)),
            pl.BlockSpec((nb, LANES), lambda b, i: (b, 0)),
            pl.BlockSpec((S, 4 * HEAD_DIM), lambda b, i: (b, 0)),
            pl.BlockSpec((S, 2 * HEAD_DIM), lambda b, i: (b, 0)),
        ],
        out_specs=pl.BlockSpec((QB, D_NSA), lambda b, i: (b * nq_t + i, 0)),
        out_shape=jax.ShapeDtypeStruct((B * S, D_NSA), F32),
        scratch_shapes=[pltpu.VMEM((N_HEADS * QB, LANES), BF16)],
        compiler_params=_params(("parallel", "parallel")),
        name="nsa_prompt",
    )(nq, nqr, small, cmp, rows, win)


def _gla_prompt_kernel(qk_ref, v_ref, g_ref, o_ref, st_ref, s_sc):
    t = pl.program_id(1)
    nt = pl.num_programs(1)
    tc = qk_ref.shape[0]
    C = GLA_CHUNK

    @pl.when(t == 0)
    def _():
        s_sc[...] = jnp.zeros_like(s_sc)

    r = lax.broadcasted_iota(I32, (tc, tc), 0)
    c = lax.broadcasted_iota(I32, (tc, tc), 1)
    same = (r // C) == (c // C)
    causal = same & (c <= r)
    g = g_ref[...]
    gcum = jnp.dot(causal.astype(F32), g, precision=HI, preferred_element_type=F32)
    g_t = g.T
    gcum_t = jnp.dot(g_t, (same & (r <= c)).astype(F32), precision=HI, preferred_element_type=F32)
    gtot_t = jnp.dot(g_t, same.astype(F32), precision=HI, preferred_element_type=F32)
    q_e = (qk_ref[:, 0:D_FOX] * SCALE * jnp.exp(gcum)).astype(BF16)
    k_e = (qk_ref[:, D_FOX:2 * D_FOX] * jnp.exp(-gcum)).astype(BF16)
    kd_t = (qk_ref[:, D_FOX:2 * D_FOX].T * jnp.exp(gtot_t - gcum_t)).astype(BF16)
    decay_t = jnp.exp(gtot_t)
    for h in range(N_HEADS):
        hs = slice(h * HEAD_DIM, (h + 1) * HEAD_DIM)
        v = v_ref[:, h * GLA_DV:(h + 1) * GLA_DV].astype(BF16)
        a = jnp.where(causal, lax.dot_general(q_e[:, hs], k_e[:, hs], NT, preferred_element_type=F32), 0.0)
        o_intra = jnp.dot(a.astype(BF16), v, preferred_element_type=F32)
        state = s_sc[h]
        for ci in range(tc // C):
            rs = slice(ci * C, (ci + 1) * C)
            o_ref[rs, h * GLA_DV:(h + 1) * GLA_DV] = (
                o_intra[rs] + jnp.dot(q_e[rs, hs], state.astype(BF16), preferred_element_type=F32))
            state = (decay_t[hs, ci * C:ci * C + 1] * state
                     + jnp.dot(kd_t[hs, rs], v[rs], preferred_element_type=F32))
        s_sc[h] = state

    @pl.when(t == nt - 1)
    def _():
        st_ref[...] = s_sc[...]


def _gla_prompt(gqk, gv, glog, B, S, tc):
    nt = S // tc
    return pl.pallas_call(
        _gla_prompt_kernel,
        grid=(B, nt),
        in_specs=[pl.BlockSpec((tc, 2 * D_FOX), lambda b, t: (b * nt + t, 0)),
                  pl.BlockSpec((tc, D_GLA), lambda b, t: (b * nt + t, 0)),
                  pl.BlockSpec((tc, D_FOX), lambda b, t: (b * nt + t, 0))],
        out_specs=[pl.BlockSpec((tc, D_GLA), lambda b, t: (b * nt + t, 0)),
                   pl.BlockSpec((None, N_HEADS, HEAD_DIM, GLA_DV), lambda b, t: (b, 0, 0, 0))],
        out_shape=[jax.ShapeDtypeStruct((B * S, D_GLA), F32),
                   jax.ShapeDtypeStruct((B, N_HEADS, HEAD_DIM, GLA_DV), F32)],
        scratch_shapes=[pltpu.VMEM((N_HEADS, HEAD_DIM, GLA_DV), F32)],
        compiler_params=_params(("parallel", "arbitrary")),
        name="gla_prompt",
    )(gqk, gv, glog)


def _outproj_kernel(x_ref, of_ref, on_ref, og_ref, gog_ref, gn_ref, w_ref, o_ref):
    acc = _wdot(of_ref[...], w_ref[0:D_FOX, :])
    acc = acc + _wdot(on_ref[...], w_ref[D_FOX:D_FOX + D_NSA, :])
    for h in range(N_HEADS):
        hs = slice(h * GLA_DV, (h + 1) * GLA_DV)
        z = _rms(og_ref[:, hs], gn_ref[...]) * _silu(gog_ref[:, hs])
        w0 = D_FOX + D_NSA + h * GLA_DV
        acc = acc + _wdot(z, w_ref[w0:w0 + GLA_DV, :])
    o_ref[...] = x_ref[...] + acc


def _out_projection(x, o_fox, o_nsa, o_gla, gog, gn, w_out, tm):
    T = x.shape[0]
    assert T % tm == 0
    row = lambda w: pl.BlockSpec((tm, w), lambda i: (i, 0))
    return pl.pallas_call(
        _outproj_kernel,
        grid=(T // tm,),
        in_specs=[row(D_MODEL), row(D_FOX), row(D_NSA), row(D_GLA), row(D_GLA),
                  pl.BlockSpec((1, GLA_DV), lambda i: (0, 0)),
                  pl.BlockSpec((D_MODEL, D_MODEL), lambda i: (0, 0))],
        out_specs=row(D_MODEL),
        out_shape=jax.ShapeDtypeStruct((T, D_MODEL), F32),
        compiler_params=_params(("parallel",)),
        name="out_projection",
    )(x, o_fox, o_nsa, o_gla, gog, gn, w_out)


def _dense_ffn_kernel(x_ref, g_ref, wg_ref, wu_ref, wd_ref, o_ref, h_sc, acc_sc):
    f = pl.program_id(1)
    nf = pl.num_programs(1)

    @pl.when(f == 0)
    def _():
        h_sc[...] = _rms(x_ref[...], g_ref[...]).astype(h_sc.dtype)
        acc_sc[...] = jnp.zeros_like(acc_sc)

    h = h_sc[...]
    a = _wdot(h, wg_ref[...])
    u = _wdot(h, wu_ref[...])
    acc_sc[...] += _wdot(_silu(a) * u, wd_ref[...])

    @pl.when(f == nf - 1)
    def _():
        o_ref[...] = x_ref[...] + acc_sc[...]


def _dense_ffn(x, g, wg, wu, wd, tm, tf):
    T = x.shape[0]
    assert T % tm == 0 and D_FF % tf == 0
    return pl.pallas_call(
        _dense_ffn_kernel,
        grid=(T // tm, D_FF // tf),
        in_specs=[pl.BlockSpec((tm, D_MODEL), lambda i, f: (i, 0)),
                  pl.BlockSpec((1, D_MODEL), lambda i, f: (0, 0)),
                  pl.BlockSpec((D_MODEL, tf), lambda i, f: (0, f)),
                  pl.BlockSpec((D_MODEL, tf), lambda i, f: (0, f)),
                  pl.BlockSpec((tf, D_MODEL), lambda i, f: (f, 0))],
        out_specs=pl.BlockSpec((tm, D_MODEL), lambda i, f: (i, 0)),
        out_shape=jax.ShapeDtypeStruct((T, D_MODEL), F32),
        scratch_shapes=[pltpu.VMEM((tm, D_MODEL), wg.dtype), pltpu.VMEM((tm, D_MODEL), F32)],
        compiler_params=_params(("parallel", "arbitrary")),
        name="dense_ffn",
    )(x, g, wg, wu, wd)


def _router_kernel(x_ref, g_ref, wr_ref, h_ref, r_ref):
    h = _rms(x_ref[...], g_ref[...])
    h_ref[...] = h
    logits = jnp.dot(h, wr_ref[...], precision=HI, preferred_element_type=F32)
    lane = lax.broadcasted_iota(I32, logits.shape, 1)
    lg = jnp.where(lane < N_EXPERTS, logits, -jnp.inf)
    m1 = jnp.max(lg, axis=-1, keepdims=True)
    i1 = jnp.min(jnp.where(lg == m1, lane, LANES), axis=-1, keepdims=True)
    lg2 = jnp.where(lane == i1, -jnp.inf, lg)
    m2 = jnp.max(lg2, axis=-1, keepdims=True)
    i2 = jnp.min(jnp.where(lg2 == m2, lane, LANES), axis=-1, keepdims=True)
    e = jnp.exp(m2 - m1)
    den = 1.0 + e
    r_ref[...] = jnp.where(lane == 0, i1.astype(F32),
                           jnp.where(lane == 1, i2.astype(F32),
                                     jnp.where(lane == 2, 1.0 / den,
                                               jnp.where(lane == 3, e / den, 0.0))))


def _router(x, g, wr_pad, tm):
    T = x.shape[0]
    assert T % tm == 0
    return pl.pallas_call(
        _router_kernel,
        grid=(T // tm,),
        in_specs=[pl.BlockSpec((tm, D_MODEL), lambda i: (i, 0)),
                  pl.BlockSpec((1, D_MODEL), lambda i: (0, 0)),
                  pl.BlockSpec((D_MODEL, LANES), lambda i: (0, 0))],
        out_specs=[pl.BlockSpec((tm, D_MODEL), lambda i: (i, 0)),
                   pl.BlockSpec((tm, LANES), lambda i: (i, 0))],
        out_shape=[jax.ShapeDtypeStruct((T, D_MODEL), F32),
                   jax.ShapeDtypeStruct((T, LANES), F32)],
        compiler_params=_params(("parallel",)),
        name="moe_router",
    )(x, g, wr_pad)


GATHER_WINDOW = 32
SC_WORKERS = 32


def _row_gather(src, idx):
    n = idx.shape[0]
    step = GATHER_WINDOW * SC_WORKERS
    n_pad = -(-n // step) * step
    if n_pad != n:
        idx = jnp.concatenate([idx, jnp.zeros((n_pad - n,), idx.dtype)])
    width = src.shape[1]
    per_worker = n_pad // SC_WORKERS
    mesh = plsc.VectorSubcoreMesh(core_axis_name="core", subcore_axis_name="subcore")

    @functools.partial(pl.kernel, out_type=jax.ShapeDtypeStruct((n_pad, width), src.dtype), mesh=mesh,
                       scratch_types=[pltpu.VMEM((per_worker,), I32),
                                      pltpu.VMEM((GATHER_WINDOW, width), src.dtype)],
                       name="row_gather")
    def gather(src_hbm, idx_hbm, dst_hbm, idx_v, buf):
        worker = lax.axis_index("core") * (SC_WORKERS // 2) + lax.axis_index("subcore")
        base = worker * per_worker
        pltpu.sync_copy(idx_hbm.at[pl.ds(base, per_worker)], idx_v)

        @pl.loop(0, per_worker // GATHER_WINDOW)
        def _(j):
            pltpu.sync_copy(src_hbm.at[idx_v.at[pl.ds(j * GATHER_WINDOW, GATHER_WINDOW)]], buf)
            pltpu.sync_copy(buf, dst_hbm.at[pl.ds(base + j * GATHER_WINDOW, GATHER_WINDOW)])

    return gather(src, idx)


def _moe_ffn_kernel(be_ref, nu_ref, x_ref, wg_ref, wu_ref, wd_ref, o_ref, acc_sc):
    b = pl.program_id(0)
    f = pl.program_id(1)
    nf = pl.num_programs(1)
    used = b < nu_ref[0]

    @pl.when(used)
    def _():
        @pl.when(f == 0)
        def _():
            acc_sc[...] = jnp.zeros_like(acc_sc)

        x = x_ref[...].astype(BF16)
        a = jnp.dot(x, wg_ref[...].astype(BF16), preferred_element_type=F32)
        u = jnp.dot(x, wu_ref[...].astype(BF16), preferred_element_type=F32)
        acc_sc[...] += _bdot(_silu(a) * u, wd_ref[...])

        @pl.when(f == nf - 1)
        def _():
            o_ref[...] = acc_sc[...]

    @pl.when(jnp.logical_not(used) & (f == nf - 1))
    def _():
        o_ref[...] = jnp.zeros_like(o_ref)


def _moe_ffn(xb, block_e, n_used, wg, wu, wd, blk, tf):
    cap = xb.shape[0]
    nb = cap // blk
    nf = D_FF // tf

    def bsel(b, nu):
        return jnp.minimum(b, nu[0] - 1)

    def fsel(b, f, nu):
        return jnp.where(b < nu[0], f, nf - 1)

    return pl.pallas_call(
        _moe_ffn_kernel,
        grid_spec=pltpu.PrefetchScalarGridSpec(
            num_scalar_prefetch=2,
            grid=(nb, nf),
            in_specs=[
                pl.BlockSpec((blk, D_MODEL), lambda b, f, be, nu: (bsel(b, nu), 0)),
                pl.BlockSpec((None, D_MODEL, tf), lambda b, f, be, nu: (be[bsel(b, nu)], 0, fsel(b, f, nu))),
                pl.BlockSpec((None, D_MODEL, tf), lambda b, f, be, nu: (be[bsel(b, nu)], 0, fsel(b, f, nu))),
                pl.BlockSpec((None, tf, D_MODEL), lambda b, f, be, nu: (be[bsel(b, nu)], fsel(b, f, nu), 0)),
            ],
            out_specs=pl.BlockSpec((blk, D_MODEL), lambda b, f, be, nu: (b, 0)),
            scratch_shapes=[pltpu.VMEM((blk, D_MODEL), F32)],
        ),
        out_shape=jax.ShapeDtypeStruct((cap, D_MODEL), F32),
        compiler_params=_params(("arbitrary", "arbitrary")),
        name="moe_ffn",
    )(block_e, n_used, xb, wg, wu, wd)


def _moe_combine_kernel(x_ref, y1_ref, y2_ref, r_ref, o_ref):
    r = r_ref[...]
    o_ref[...] = x_ref[...] + (r[:, 2:3] * y1_ref[...] + r[:, 3:4] * y2_ref[...])


def _moe_combine(x, y1, y2, route, tm, row0):
    T = x.shape[0]
    nt = T // tm
    assert row0 % tm == 0
    b0 = row0 // tm
    return pl.pallas_call(
        _moe_combine_kernel,
        grid=(nt,),
        in_specs=[pl.BlockSpec((tm, D_MODEL), lambda i: (i, 0)),
                  pl.BlockSpec((tm, D_MODEL), lambda i: (b0 + i, 0)),
                  pl.BlockSpec((tm, D_MODEL), lambda i: (b0 + i, 0)),
                  pl.BlockSpec((tm, LANES), lambda i: (i, 0))],
        out_specs=pl.BlockSpec((tm, D_MODEL), lambda i: (i, 0)),
        out_shape=jax.ShapeDtypeStruct((T, D_MODEL), F32),
        compiler_params=_params(("parallel",)),
        name="moe_combine",
    )(x, y1, y2, route)


def _moe_plan(e_top, blk):
    T = e_top.shape[0]
    n = 2 * T
    flat_e = e_top.reshape(-1)
    onehot = (flat_e[:, None] == jnp.arange(N_EXPERTS, dtype=I32)[None, :]).astype(I32)
    csum = jnp.cumsum(onehot, axis=0)
    rank = jnp.sum((csum - onehot) * onehot, axis=1)
    counts = csum[-1]
    padded = (counts + blk - 1) // blk * blk
    ends = jnp.cumsum(padded)
    pstart = ends - padded
    dest = (pstart[flat_e] + rank).astype(I32)
    n_blocks = -(-n // blk) + N_EXPERTS
    cap = n_blocks * blk
    slot_tok = jnp.zeros((cap,), I32).at[dest].set(jnp.arange(n, dtype=I32) // 2)
    first = jnp.arange(n_blocks, dtype=I32) * blk
    block_e = jnp.minimum(jnp.sum((ends[None, :] <= first[:, None]).astype(I32), axis=1), N_EXPERTS - 1)
    n_used = (ends[-1] // blk).astype(I32).reshape(1)
    return dest, slot_tok, block_e, n_used


def _moe_layer(xs, g, wr_pad, wg, wu, wd, tms, blk, tf):
    routed = [_router(x, g, wr_pad, tm) for x, tm in zip(xs, tms)]
    h = jnp.concatenate([r[0] for r in routed], axis=0) if len(xs) > 1 else routed[0][0]
    e_top = jnp.concatenate([r[1][:, 0:2] for r in routed], axis=0).astype(I32)
    dest, slot_tok, block_e, n_used = _moe_plan(e_top, blk)
    xb = _row_gather(h, slot_tok)
    yb = _moe_ffn(xb, block_e, n_used, wg, wu, wd, blk, tf)
    d2 = dest.reshape(-1, 2)
    y1 = _row_gather(yb, d2[:, 0])
    y2 = _row_gather(yb, d2[:, 1])
    out, row0 = [], 0
    for x, tm, r in zip(xs, tms, routed):
        out.append(_moe_combine(x, y1, y2, r[1], tm, row0))
        row0 += x.shape[0]
    return out


def _norm_kernel(x_ref, g_ref, o_ref):
    o_ref[...] = _rms(x_ref[...], g_ref[...])


def _final_norm(x, g, tm):
    T = x.shape[0]
    return pl.pallas_call(
        _norm_kernel,
        grid=(T // tm,),
        in_specs=[pl.BlockSpec((tm, D_MODEL), lambda i: (i, 0)),
                  pl.BlockSpec((1, D_MODEL), lambda i: (0, 0))],
        out_specs=pl.BlockSpec((tm, D_MODEL), lambda i: (i, 0)),
        out_shape=jax.ShapeDtypeStruct((T, D_MODEL), F32),
        compiler_params=_params(("parallel",)),
        name="final_norm",
    )(x, g)


_IN_SPLITS = (D_FOX, D_FOX, D_FOX, N_HEADS, D_NSA, 6 * HEAD_DIM, 3 * N_HEADS,
              D_FOX, D_FOX, D_GLA, GLA_RANK, D_GLA)


def _reorder_w_in(w):
    offs = [0]
    for s in _IN_SPLITS:
        offs.append(offs[-1] + s)
    seg = lambda k: w[:, offs[k]:offs[k + 1]]
    fq, fk, fv, ff, nq, nkv, ng, gq, gk, gv, glr, gog = [seg(k) for k in range(12)]
    pad = jnp.zeros((w.shape[0], LANES - SM_GLR - GLA_RANK), w.dtype)
    return jnp.concatenate([fq, fk, fv, nq, nkv, gq, gk, gv, gog, ff, ng, glr, pad], axis=1)


def _rope_table(pos):
    inv = ROPE_THETA ** (-jnp.arange(ROPE_HALF, dtype=F32) / ROPE_HALF)
    ang = pos.astype(F32)[:, None] * inv[None, :]
    cos, sin = jnp.cos(ang), jnp.sin(ang)
    P = pos.shape[0]
    one = jnp.ones((P, HEAD_DIM - ROPE_DIM), F32)
    zero = jnp.zeros((P, HEAD_DIM - ROPE_DIM), F32)
    z8 = jnp.zeros((P, ROPE_HALF), F32)
    a64 = jnp.concatenate([cos, cos, one], axis=1)
    p64 = jnp.concatenate([z8, sin, zero], axis=1)
    m64 = jnp.concatenate([-sin, z8, zero], axis=1)
    i64 = jnp.ones((P, HEAD_DIM), F32)
    o64 = jnp.zeros((P, HEAD_DIM), F32)
    return jnp.concatenate([a64, a64, p64, p64, m64, m64, a64, i64, p64, o64, m64, o64], axis=1)


def _layer_mix_params(l, norm_mix_g, w_in, b_fox_f, w_cmp, pe_cmp, w_gla_gk, b_gla_gk, g_gla_norm, w_out):
    sb = jnp.zeros((1, LANES), F32).at[0, SM_FF:SM_FF + N_HEADS].set(b_fox_f[l])
    wgk = jnp.zeros((LANES, D_FOX), F32).at[SM_GLR:SM_GLR + GLA_RANK].set(w_gla_gk[l])
    w_r = _reorder_w_in(w_in[l])
    return dict(g=norm_mix_g[l].reshape(1, D_MODEL), w_r=w_r.astype(BF16), w_r32=w_r, sb=sb,
                wgk=wgk.astype(BF16), wgk32=wgk, w_out32=w_out[l],
                bgk=b_gla_gk[l].reshape(1, D_FOX), w_cmp=w_cmp[l],
                pe_cmp=pe_cmp[l].reshape(2, 1, CMP_LEN * HEAD_DIM),
                gn=g_gla_norm[l].reshape(1, GLA_DV), w_out=w_out[l].astype(BF16))


def _mix_prompt(x, B, S, p, tab, tm, t_fox, tk_sel, tc_gla):
    (fq, fkv, nq, nqr, rows, win, gqk, gv, gog, glog, small) = _in_projection(
        x, 0, B * S, tm, p['g'], p['w_r'], p['sb'], p['wgk'], p['bgk'], tab, S)
    cr = _fox_cumsum(small, B, S, min(S, 512))
    o_fox = _fox_prompt(fq, fkv, cr, B, S, t_fox)
    n_blk = B * S // CMP_LEN
    blocks = _block_major(rows[:, 0:2 * HEAD_DIM].reshape(B * S, 2, HEAD_DIM), n_blk)
    cmp = _compress(blocks, p['pe_cmp'], p['w_cmp'], min(256, n_blk), True)
    o_nsa = _nsa_prompt(nq, nqr, small, cmp, rows, win, B, S, tk_sel)
    o_gla, g_state = _gla_prompt(gqk, gv, glog, B, S, tc_gla)
    x_new = _out_projection(x, o_fox, o_nsa, o_gla, gog, p['gn'], p['w_out'], tm)
    return x_new, dict(fkv=fkv, small=small, rows=rows, win=win, g_state=g_state)


def _per_head_col(vals):
    r = lax.broadcasted_iota(I32, (SUBLANES, 1), 0)
    out = jnp.zeros((SUBLANES, 1), F32)
    for h, v in enumerate(vals):
        out = out + jnp.where(r == h, v, 0.0)
    return out


def _fox_decode_kernel(pt_ref, q_ref, kvn_ref, smn_ref, *refs, n_pages):
    del pt_ref
    kv_refs = refs[0:n_pages]
    lf_refs = refs[n_pages:2 * n_pages]
    o_ref, lf_sc, s_sc = refs[2 * n_pages:]
    R = n_pages * SUBLANES
    PG = kv_refs[0].shape[-1]

    lf_sc[...] = jnp.zeros_like(lf_sc)
    for p in range(n_pages):
        lf_sc[p * SUBLANES:p * SUBLANES + N_HEADS, :] = lf_refs[p][...]
    lft = lf_sc[...]
    k0 = lax.broadcasted_iota(I32, (PG, PG), 0)
    k1 = lax.broadcasted_iota(I32, (PG, PG), 1)
    within = jnp.dot(lft, (k0 > k1).astype(F32), precision=HI, preferred_element_type=F32)
    tot = jnp.broadcast_to(jnp.sum(lft, axis=1, keepdims=True), (R, PG))
    r0 = lax.broadcasted_iota(I32, (R, R), 0)
    r1 = lax.broadcasted_iota(I32, (R, R), 1)
    later = ((r1 % SUBLANES == r0 % SUBLANES) & (r1 // SUBLANES > r0 // SUBLANES)).astype(F32)
    cross = jnp.dot(later, tot, precision=HI, preferred_element_type=F32)
    rr = lax.broadcasted_iota(I32, (R, 1), 0) % SUBLANES
    smn = smn_ref[...]
    newcol = jnp.zeros((R, 1), F32)
    for h in range(N_HEADS):
        newcol = newcol + jnp.where(rr == h, smn[:, SM_FF + h:SM_FF + h + 1], 0.0)
    bias = (within + cross + newcol).reshape(n_pages, SUBLANES, PG)

    s_sc[...] = jnp.zeros_like(s_sc)
    for p in range(n_pages):
        for h in range(N_HEADS):
            s_sc[p, h:h + 1, :] = jnp.sum(kv_refs[p][0, h] * q_ref[h], axis=0, keepdims=True)
    s3 = s_sc[...] * SCALE + bias
    s_new = _per_head_col([jnp.sum(q_ref[h] * kvn_ref[0, h], axis=0, keepdims=True)
                           for h in range(N_HEADS)]) * SCALE
    m = jnp.max(jnp.max(s3, axis=2, keepdims=True), axis=0)
    m = jnp.maximum(m, s_new)
    p3 = jnp.exp(s3 - m[None])
    pn = jnp.exp(s_new - m)
    den = jnp.sum(jnp.sum(p3, axis=2, keepdims=True), axis=0) + pn
    for h in range(N_HEADS):
        acc = jnp.zeros((HEAD_DIM, PG), F32)
        for p in range(n_pages):
            acc = acc + kv_refs[p][1, h] * p3[p, h:h + 1, :]
        o_ref[h] = ((jnp.sum(acc, axis=1, keepdims=True) + pn[h:h + 1] * kvn_ref[1, h])
                    / den[h:h + 1])


def _fox_decode(l, pt_flat, n_pages, fq, fkv, small, kv_cache_t, lft_cache):
    DB = fq.shape[0]
    PG = kv_cache_t.shape[-1]
    page = lambda p, nz: (lambda b, pt: (l, pt[b * n_pages + p]) + (0,) * nz)
    per_b = lambda *s: pl.BlockSpec((None,) + s, lambda b, pt: (b,) + (0,) * len(s))
    return pl.pallas_call(
        functools.partial(_fox_decode_kernel, n_pages=n_pages),
        grid_spec=pltpu.PrefetchScalarGridSpec(
            num_scalar_prefetch=1,
            grid=(DB,),
            in_specs=[per_b(N_HEADS, HEAD_DIM, 1), per_b(2, N_HEADS, HEAD_DIM, 1), per_b(1, LANES)]
            + [pl.BlockSpec((None, None, 2, N_HEADS, HEAD_DIM, PG), page(p, 4)) for p in range(n_pages)]
            + [pl.BlockSpec((None, None, N_HEADS, PG), page(p, 2)) for p in range(n_pages)],
            out_specs=per_b(N_HEADS, HEAD_DIM, 1),
            scratch_shapes=[pltpu.VMEM((n_pages * SUBLANES, PG), F32),
                            pltpu.VMEM((n_pages, SUBLANES, PG), F32)],
        ),
        out_shape=jax.ShapeDtypeStruct((DB, N_HEADS, HEAD_DIM, 1), F32),
        compiler_params=_params(("arbitrary",)),
        name="fox_decode",
    )(pt_flat, fq.reshape(DB, N_HEADS, HEAD_DIM, 1), fkv.reshape(DB, 2, N_HEADS, HEAD_DIM, 1),
      small.reshape(DB, 1, LANES), *([kv_cache_t] * n_pages), *([lft_cache] * n_pages)).reshape(DB, D_FOX)


def _nsa_decode_kernel(pt_ref, q_ref, qr_ref, rown_ref, winn_ref, gate_ref, win_ref, *refs,
                       n_pages, past_len):
    del pt_ref
    pg_refs = refs[0:n_pages]
    cmp_refs = refs[n_pages:2 * n_pages]
    o_ref, nw_ref, cmp_sc, qc_sc, s_sc, sw_sc = refs[2 * n_pages:]
    R = n_pages * SUBLANES
    PG = pg_refs[0].shape[-1]
    WB = win_ref.shape[-1]
    per_page = PG // CMP_LEN
    assert per_page <= SUBLANES and PG == 2 * SEL_LEN and R == LANES
    jt = past_len // SEL_LEN

    qc_sc[...] = jnp.zeros_like(qc_sc)
    qc_sc[0:N_HEADS, 0:HEAD_DIM] = q_ref[...]
    head_row = lax.broadcasted_iota(I32, (SUBLANES, 1), 0) < N_HEADS

    cmp_sc[...] = jnp.zeros_like(cmp_sc)
    for p in range(n_pages):
        cmp_sc[p * SUBLANES:p * SUBLANES + per_page, :] = cmp_refs[p][...]
    cmpa = cmp_sc[...]
    lane = lax.broadcasted_iota(I32, (1, R), 1)
    blk = per_page * (lane // SUBLANES) + lane % SUBLANES
    complete = (lane % SUBLANES < per_page) & ((blk + 1) * CMP_LEN - 1 <= past_len)
    s = lax.dot_general(qc_sc[...], cmpa, NT, precision=HI, preferred_element_type=F32) * SCALE
    s = jnp.where(complete, s, NEG)
    e = jnp.exp(s - jnp.max(s, axis=-1, keepdims=True))
    pc = e / jnp.sum(e, axis=-1, keepdims=True) * complete.astype(F32)
    vcb_t = cmpa.T[HEAD_DIM:2 * HEAD_DIM, :]
    o_cmp = [jnp.sum(vcb_t * pc[h:h + 1, :], axis=1, keepdims=True) for h in range(N_HEADS)]

    imp_c = jnp.sum(jnp.where(head_row, pc, 0.0), axis=0, keepdims=True)
    imp_s = imp_c + pltpu.roll(imp_c, R - 1, 1)
    cand = (lane % SUBLANES == 0) | (lane % SUBLANES == 2)
    jsel = 2 * (lane // SUBLANES) + (lane % SUBLANES) // 2
    score = jnp.where(jsel == jt, 2.0 * SEL_FORCE,
                      jnp.where((jsel == 0) | (jsel == jt - 1), SEL_FORCE,
                                jnp.where(jsel <= jt, imp_s + 0.0, -1.0)))
    score_b = jnp.broadcast_to(score, (R, R))
    key_row = _order_key(score_b)
    key_col = _order_key(score_b.T)
    l0 = lax.broadcasted_iota(I32, (R, R), 0)
    l1 = lax.broadcasted_iota(I32, (R, R), 1)
    cand_col = (l0 % SUBLANES == 0) | (l0 % SUBLANES == 2)
    beats = cand_col & (key_col > jnp.where(l0 < l1, key_row - 1, key_row))
    cnt = jnp.sum(beats.astype(I32), axis=0, keepdims=True)
    sel_row = (cand & (cnt < TOP_N - 1)).astype(F32)
    sel_col = jnp.broadcast_to(sel_row, (R, R)).T
    half = ((l0 % SUBLANES == 0) & (l1 < SEL_LEN)) | ((l0 % SUBLANES == 2) & (l1 >= SEL_LEN))
    z = jnp.where(half, sel_col, 0.0)
    same_page = (l1 // SUBLANES == l0 // SUBLANES).astype(BF16)
    picked = jnp.dot(same_page, z.astype(BF16), preferred_element_type=F32)
    picked = picked.reshape(n_pages, SUBLANES, PG) > 0.5

    s_sc[...] = jnp.zeros_like(s_sc)
    for p in range(n_pages):
        ks_t = pg_refs[p][2]
        for h in range(N_HEADS):
            s_sc[p, h:h + 1, :] = jnp.sum(ks_t * qr_ref[h], axis=0, keepdims=True)
    s3 = jnp.where(picked, s_sc[...] * SCALE, NEG)
    s_new = _per_head_col([jnp.sum(qr_ref[h] * rown_ref[2], axis=0, keepdims=True)
                           for h in range(N_HEADS)]) * SCALE
    m = jnp.maximum(jnp.max(jnp.max(s3, axis=2, keepdims=True), axis=0), s_new)
    p3 = jnp.exp(s3 - m[None])
    pn = jnp.exp(s_new - m)
    den = jnp.sum(jnp.sum(p3, axis=2, keepdims=True), axis=0) + pn
    o_sel = []
    for h in range(N_HEADS):
        acc = jnp.zeros((HEAD_DIM, PG), F32)
        for p in range(n_pages):
            acc = acc + pg_refs[p][3] * p3[p, h:h + 1, :]
        o_sel.append((jnp.sum(acc, axis=1, keepdims=True) + pn[h:h + 1] * rown_ref[3]) / den[h:h + 1])

    kw_t = win_ref[0]
    vw_t = win_ref[1]
    wlane = lax.broadcasted_iota(I32, (1, WB), 1)
    wpos = past_len - WB + wlane
    wd = past_len - wpos
    wok = (wd >= 0) & (wd < WINDOW) & (wpos >= 0)
    sw_sc[...] = jnp.zeros_like(sw_sc)
    for h in range(N_HEADS):
        sw_sc[h:h + 1, :] = jnp.sum(kw_t * qr_ref[h], axis=0, keepdims=True)
    sw = jnp.where(wok, sw_sc[...] * SCALE, NEG)
    sw_new = _per_head_col([jnp.sum(qr_ref[h] * winn_ref[0], axis=0, keepdims=True)
                            for h in range(N_HEADS)]) * SCALE
    mw = jnp.maximum(jnp.max(sw, axis=-1, keepdims=True), sw_new)
    ew = jnp.exp(sw - mw)
    en = jnp.exp(sw_new - mw)
    denw = jnp.sum(ew, axis=-1, keepdims=True) + en

    g = gate_ref[...]
    for h in range(N_HEADS):
        o_win = ((jnp.sum(vw_t * ew[h:h + 1, :], axis=1, keepdims=True) + en[h:h + 1] * winn_ref[1])
                 / denw[h:h + 1])
        o_ref[h] = (g[h:h + 1, 0:1] * o_cmp[h] + g[h:h + 1, 1:2] * o_sel[h] + g[h:h + 1, 2:3] * o_win)
    last = lax.broadcasted_iota(I32, (HEAD_DIM, WB), 1) == WB - 1
    for s in range(2):
        nw_ref[s] = jnp.where(last, winn_ref[s], pltpu.roll(win_ref[s], WB - 1, 1))


def _nsa_decode(l, pt_flat, n_pages, past_len, nq, nqr, rows, win, small, nsa_cache_t, cmp_pool, win_state_t):
    DB = nq.shape[0]
    PG = nsa_cache_t.shape[-1]
    WB = win_state_t.shape[-1]
    page = lambda p: (lambda b, pt: (l, pt[b * n_pages + p], 0, 0, 0))
    cpage = lambda p: (lambda b, pt: (pt[b * n_pages + p], 0, 0))
    per_b = lambda *s: pl.BlockSpec((None,) + s, lambda b, pt: (b,) + (0,) * len(s))
    gates = small[:, SM_NG:SM_NG + 3 * N_HEADS].reshape(DB, N_HEADS, 3)
    o, nw = pl.pallas_call(
        functools.partial(_nsa_decode_kernel, n_pages=n_pages, past_len=past_len),
        grid_spec=pltpu.PrefetchScalarGridSpec(
            num_scalar_prefetch=1,
            grid=(DB,),
            in_specs=[per_b(N_HEADS, HEAD_DIM), per_b(N_HEADS, HEAD_DIM, 1), per_b(4, HEAD_DIM, 1),
                      per_b(2, HEAD_DIM, 1), per_b(N_HEADS, 3),
                      pl.BlockSpec((None, None, 2, HEAD_DIM, WB), lambda b, pt: (l, b, 0, 0, 0))]
            + [pl.BlockSpec((None, None, 4, HEAD_DIM, PG), page(p)) for p in range(n_pages)]
            + [pl.BlockSpec((None, PG // CMP_LEN, LANES), cpage(p)) for p in range(n_pages)],
            out_specs=[per_b(N_HEADS, HEAD_DIM, 1), per_b(2, HEAD_DIM, WB)],
            scratch_shapes=[pltpu.VMEM((n_pages * SUBLANES, LANES), F32),
                            pltpu.VMEM((SUBLANES, LANES), F32),
                            pltpu.VMEM((n_pages, SUBLANES, PG), F32),
                            pltpu.VMEM((SUBLANES, WB), F32)],
        ),
        out_shape=[jax.ShapeDtypeStruct((DB, N_HEADS, HEAD_DIM, 1), F32),
                   jax.ShapeDtypeStruct((DB, 2, HEAD_DIM, WB), F32)],
        compiler_params=_params(("arbitrary",)),
        name="nsa_decode",
    )(pt_flat, nq.reshape(DB, N_HEADS, HEAD_DIM), nqr.reshape(DB, N_HEADS, HEAD_DIM, 1),
      rows.reshape(DB, 4, HEAD_DIM, 1), win.reshape(DB, 2, HEAD_DIM, 1), gates, win_state_t,
      *([nsa_cache_t] * n_pages), *([cmp_pool] * n_pages))
    return o.reshape(DB, D_NSA), nw


def _gla_decode_kernel(q_ref, k_ref, g_ref, v_ref, s_ref, o_ref, so_ref):
    s_new = jnp.exp(g_ref[...]) * s_ref[...] + k_ref[...] * v_ref[...]
    so_ref[...] = s_new
    o_ref[...] = jnp.sum((q_ref[...] * SCALE) * s_new, axis=2, keepdims=True)


def _gla_decode(l, gqk, gv, glog, state, nb):
    DB = gqk.shape[0]
    col = lambda a: a.reshape(DB, N_HEADS, HEAD_DIM, 1)
    cspec = pl.BlockSpec((nb, N_HEADS, HEAD_DIM, 1), lambda i: (i, 0, 0, 0))
    vspec = pl.BlockSpec((nb, N_HEADS, 1, GLA_DV), lambda i: (i, 0, 0, 0))
    sspec = pl.BlockSpec((nb, N_HEADS, HEAD_DIM, GLA_DV), lambda i: (i, 0, 0, 0))
    o, s_new = pl.pallas_call(
        _gla_decode_kernel,
        grid=(DB // nb,),
        in_specs=[cspec, cspec, cspec, vspec,
                  pl.BlockSpec((None, nb, N_HEADS, HEAD_DIM, GLA_DV), lambda i: (l, i, 0, 0, 0))],
        out_specs=[vspec, sspec],
        out_shape=[jax.ShapeDtypeStruct((DB, N_HEADS, 1, GLA_DV), F32),
                   jax.ShapeDtypeStruct((DB, N_HEADS, HEAD_DIM, GLA_DV), F32)],
        compiler_params=_params(("parallel",)),
        name="gla_decode",
    )(col(gqk[:, 0:D_FOX]), col(gqk[:, D_FOX:2 * D_FOX]), col(glog),
      gv.reshape(DB, N_HEADS, 1, GLA_DV), state)
    return o.reshape(DB, D_GLA), s_new


def _mix_sample(x, l, p, tab, pt_flat, n_pages, past_len, fox_kv_t, fox_lft_c, nsa_t, cmp_blocks,
                win_state_t, gla_state):
    DB = x.shape[0]
    (fq, fkv, nq, nqr, rows, win, gqk, gv, gog, glog, small) = _in_projection(
        x, 0, DB, DB, p['g'], p['w_r32'], p['sb'], p['wgk32'], p['bgk'], tab, DB)
    o_fox = _fox_decode(l, pt_flat, n_pages, fq, fkv, small, fox_kv_t, fox_lft_c)
    n_pool, PG = nsa_t.shape[1], nsa_t.shape[-1]
    per_layer = n_pool * (PG // CMP_LEN)
    cmp_pool = _compress(cmp_blocks, p['pe_cmp'], p['w_cmp'], 256, False, row0=l * per_layer, n_rows=per_layer)
    cmp_pool = cmp_pool.reshape(n_pool, PG // CMP_LEN, LANES)
    o_nsa, new_win = _nsa_decode(l, pt_flat, n_pages, past_len, nq, nqr, rows, win, small,
                                 nsa_t, cmp_pool, win_state_t)
    o_gla, g_state = _gla_decode(l, gqk, gv, glog, gla_state, 8)
    x_new = _out_projection(x, o_fox, o_nsa, o_gla, gog, p['gn'], p['w_out32'], DB)
    return x_new, dict(fkv=fkv, small=small, rows=rows, win=new_win, g_state=g_state)


def kernel(x_prompt, x_sample, cache_fox_kv, cache_fox_logf, cache_nsa_kv, state_nsa_win, state_gla,
           page_table, norm_mix_g, w_in, b_fox_f, w_cmp, pe_cmp, w_gla_gk, b_gla_gk, g_gla_norm, w_out,
           norm_ffn_g, dense_w_gate, dense_w_up, dense_w_down, moe_w_router, moe_w_gate, moe_w_up,
           moe_w_down, final_norm_g):
    B, S, _ = x_prompt.shape
    DB, TN, _ = x_sample.shape
    assert TN == 1
    depth, n_pool, PG = cache_fox_kv.shape[0:3]
    n_pages = page_table.shape[1]
    past_len = n_pages * PG
    WB = state_nsa_win.shape[2]
    xp = x_prompt.reshape(B * S, D_MODEL)
    xs = x_sample.reshape(DB, D_MODEL)
    tab_p = _rope_table(jnp.arange(S))
    tab_s = _rope_table(jnp.full((DB,), past_len, I32))
    pt_flat = page_table.reshape(-1).astype(I32)
    fox_kv_t = jnp.transpose(cache_fox_kv, (0, 1, 3, 4, 5, 2))
    fox_lft_c = jnp.swapaxes(cache_fox_logf, 2, 3)
    nsa_t = jnp.transpose(cache_nsa_kv, (0, 1, 3, 4, 2))
    win_state_t = jnp.transpose(state_nsa_win, (0, 1, 3, 4, 2))
    n_cmp = depth * n_pool * (PG // CMP_LEN)
    cmp_blocks = _block_major(cache_nsa_kv[:, :, :, 0:2, :].reshape(n_cmp * CMP_LEN, 2, HEAD_DIM), n_cmp)
    cp, cs = [], []
    for l in range(depth):
        p = _layer_mix_params(l, norm_mix_g, w_in, b_fox_f, w_cmp, pe_cmp, w_gla_gk, b_gla_gk,
                              g_gla_norm, w_out)
        xp, c = _mix_prompt(xp, B, S, p, tab_p, 512, 512, 512, 256)
        cp.append(c)
        xs, c = _mix_sample(xs, l, p, tab_s, pt_flat, n_pages, past_len, fox_kv_t, fox_lft_c, nsa_t,
                            cmp_blocks, win_state_t, state_gla)
        cs.append(c)
        gf = norm_ffn_g[l].reshape(1, D_MODEL)
        i = l // 2
        if l % 2 == 0:
            wg, wu, wd = (dense_w_gate[i].astype(BF16), dense_w_up[i].astype(BF16),
                          dense_w_down[i].astype(BF16))
            xp = _dense_ffn(xp, gf, wg, wu, wd, 1024, 512)
            xs = _dense_ffn(xs, gf, dense_w_gate[i], dense_w_up[i], dense_w_down[i], DB, 512)
        else:
            wr = jnp.zeros((D_MODEL, LANES), F32).at[:, 0:N_EXPERTS].set(moe_w_router[i])
            xp, xs = _moe_layer([xp, xs], gf, wr, moe_w_gate[i], moe_w_up[i], moe_w_down[i],
                                [512, DB], 1024, 512)
    gfin = final_norm_g.reshape(1, D_MODEL)
    y_p = _final_norm(xp, gfin, 512).reshape(B, S, D_MODEL)
    y_s = _final_norm(xs, gfin, DB).reshape(DB, 1, D_MODEL)
    wp = min(WINDOW, S)
    st = lambda key, group: jnp.stack([c[key] for c in group])
    return (y_p, y_s,
            st('fkv', cp).reshape(depth, B, S, 2, N_HEADS, HEAD_DIM),
            st('small', cp)[:, :, SM_FF:SM_FF + N_HEADS].reshape(depth, B, S, N_HEADS),
            st('rows', cp).reshape(depth, B, S, 4, HEAD_DIM),
            st('win', cp).reshape(depth, B, S, 2, HEAD_DIM)[:, :, S - wp:],
            st('g_state', cp),
            st('fkv', cs).reshape(depth, DB, 1, 2, N_HEADS, HEAD_DIM),
            st('small', cs)[:, :, SM_FF:SM_FF + N_HEADS].reshape(depth, DB, 1, N_HEADS),
            st('rows', cs).reshape(depth, DB, 1, 4, HEAD_DIM),
            jnp.transpose(st('win', cs), (0, 1, 4, 2, 3)),
            st('g_state', cs))
```

```python
import functools

import jax
import jax.numpy as jnp
from jax import lax
from jax.experimental import pallas as pl
from jax.experimental.pallas import tpu as pltpu
from jax.experimental.pallas import tpu_sc as plsc

F32 = jnp.float32
BF16 = jnp.bfloat16
I32 = jnp.int32
HI = lax.Precision.HIGHEST

D_MODEL = 1024
HEAD_DIM = 64
N_HEADS = 4
D_FOX = N_HEADS * HEAD_DIM
D_NSA = N_HEADS * HEAD_DIM
GLA_DV = 128
D_GLA = N_HEADS * GLA_DV
GLA_RANK = 16
GLA_TAU = 16.0
GLA_CHUNK = 64
CMP_LEN = 32
SEL_LEN = 64
TOP_N = 16
WINDOW = 512
ROPE_THETA = 500000.0
ROPE_DIM = HEAD_DIM // 4
ROPE_HALF = ROPE_DIM // 2
D_FF = 3584
N_EXPERTS = 8
EPS = 1e-6
SEL_FORCE = 1e9
NEG = -1e30
SCALE = HEAD_DIM ** -0.5
LOG2E = 1.4426950408889634

LANES = 128
SUBLANES = 8
VMEM_LIMIT = 56 * 1024 * 1024

C_FQ = 0
C_FKV = 256
C_NQ = 768
C_NKV = 1024
C_GQK = 1408
C_GV = 1920
C_GOG = 2432
C_SMALL = 2944
C_END = 3072
SM_FF = 0
SM_NG = 4
SM_GLR = 16

NT = (((1,), (1,)), ((), ()))


def _params(sem):
    return pltpu.CompilerParams(dimension_semantics=sem, vmem_limit_bytes=VMEM_LIMIT)


def _rms(x, g):
    ms = jnp.mean(x * x, axis=-1, keepdims=True)
    return x * lax.rsqrt(ms + EPS) * g


def _sigmoid(x):
    return 1.0 / (1.0 + jnp.exp(-x))


def _log_sigmoid(x):
    return -(jnp.maximum(-x, 0.0) + jnp.log1p(jnp.exp(-jnp.abs(x))))


def _silu(x):
    return x * _sigmoid(x)


def _bdot(a, b):
    return jnp.dot(a.astype(BF16), b.astype(BF16), preferred_element_type=F32)


def _bdot_nt(a, b):
    return lax.dot_general(a.astype(BF16), b.astype(BF16), NT, preferred_element_type=F32)


def _wdot(a, w):
    if w.dtype == F32:
        return jnp.dot(a.astype(F32), w, precision=HI, preferred_element_type=F32)
    return jnp.dot(a.astype(BF16), w, preferred_element_type=F32)


def _rope128(x, a, bp, bm):
    return x * a + pltpu.roll(x, ROPE_HALF, 1) * bp + pltpu.roll(x, LANES - ROPE_HALF, 1) * bm


def _inproj_kernel(x_ref, g_ref, w_ref, sb_ref, wgk_ref, bgk_ref, tab_ref,
                   fq_ref, fkv_ref, nq_ref, nqr_ref, rows_ref, win_ref,
                   gqk_ref, gv_ref, gog_ref, glog_ref, small_ref):
    h = _rms(x_ref[...], g_ref[...]).astype(w_ref.dtype)

    def mm(a, b):
        return _wdot(h, w_ref[:, a:b])

    fq_ref[...] = mm(C_FQ, C_FKV)
    fkv_ref[...] = mm(C_FKV, C_NQ)
    tab = tab_ref[...]
    ab, pb, mb = tab[:, 0:128], tab[:, 128:256], tab[:, 256:384]
    af, pf, mf = tab[:, 384:512], tab[:, 512:640], tab[:, 640:768]
    nq = mm(C_NQ, C_NKV)
    nq_ref[...] = nq
    nqr_ref[:, 0:128] = _rope128(nq[:, 0:128], ab, pb, mb)
    nqr_ref[:, 128:256] = _rope128(nq[:, 128:256], ab, pb, mb)
    nkv = mm(C_NKV, C_GQK)
    rows_ref[:, 0:128] = nkv[:, 0:128]
    rows_ref[:, 128:256] = _rope128(nkv[:, 128:256], af, pf, mf)
    win_ref[...] = _rope128(nkv[:, 256:384], af, pf, mf)
    gqk_ref[...] = mm(C_GQK, C_GV)
    gv_ref[...] = mm(C_GV, C_GOG)
    gog_ref[...] = mm(C_GOG, C_SMALL)
    sm = mm(C_SMALL, C_END)
    glog_ref[...] = _log_sigmoid(_wdot(sm, wgk_ref[...]) + bgk_ref[...]) * (1.0 / GLA_TAU)
    smb = sm + sb_ref[...]
    lane = lax.broadcasted_iota(I32, smb.shape, 1)
    small_ref[...] = jnp.where(lane < SM_NG, _log_sigmoid(smb), _sigmoid(smb))


def _in_projection(x_all, row0, n_rows, tm, g, w_r, sb, wgk, bgk, tab, tab_period):
    assert n_rows % tm == 0 and row0 % tm == 0 and tab_period % tm == 0
    nt = n_rows // tm
    b0 = row0 // tm
    npd = tab_period // tm
    widths = (256, 512, 256, 256, 256, 128, 512, 512, 512, 256, 128)
    full = lambda shape: pl.BlockSpec(shape, lambda i: (0, 0))
    return pl.pallas_call(
        _inproj_kernel,
        grid=(nt,),
        in_specs=[
            pl.BlockSpec((tm, D_MODEL), lambda i: (b0 + i, 0)),
            full((1, D_MODEL)),
            full((D_MODEL, C_END)),
            full((1, LANES)),
            full((LANES, 256)),
            full((1, 256)),
            pl.BlockSpec((tm, 768), lambda i: (i % npd, 0)),
        ],
        out_specs=[pl.BlockSpec((tm, w), lambda i: (i, 0)) for w in widths],
        out_shape=[jax.ShapeDtypeStruct((n_rows, w), F32) for w in widths],
        compiler_params=_params(("parallel",)),
        name="in_projection",
    )(x_all, g, w_r, sb, wgk, bgk, tab)


def _cumsum_kernel(sm_ref, cr_ref, carry):
    t = pl.program_id(1)
    ts = sm_ref.shape[0]

    @pl.when(t == 0)
    def _():
        carry[...] = jnp.zeros_like(carry)

    r = lax.broadcasted_iota(I32, (ts, ts), 0)
    c = lax.broadcasted_iota(I32, (ts, ts), 1)
    tri = (c <= r).astype(F32)
    cs = jnp.dot(tri, sm_ref[...], precision=HI, preferred_element_type=F32) + carry[...]
    carry[...] = cs[ts - 1:ts, :]
    cr_ref[...] = cs.T[0:SUBLANES, :] * LOG2E


def _fox_cumsum(small, B, S, ts):
    ns = S // ts
    return pl.pallas_call(
        _cumsum_kernel,
        grid=(B, ns),
        in_specs=[pl.BlockSpec((ts, LANES), lambda b, t: (b * ns + t, 0))],
        out_specs=pl.BlockSpec((None, SUBLANES, ts), lambda b, t: (b, 0, t)),
        out_shape=jax.ShapeDtypeStruct((B, SUBLANES, S), F32),
        scratch_shapes=[pltpu.VMEM((1, LANES), F32)],
        compiler_params=_params(("parallel", "arbitrary")),
        name="fox_cumsum",
    )(small)


def _pair_mask(shape, h):
    return (lax.broadcasted_iota(I32, shape, 1) // HEAD_DIM) == (h % 2)


def _fox_prompt_kernel(q_ref, kv_ref, cr_ref, o_ref, q_sc, m_sc, acc_sc):
    i = pl.program_id(1)
    j = pl.program_id(2)
    nk = pl.num_programs(2)
    tq = q_ref.shape[0]
    tk = kv_ref.shape[0]

    @pl.when(j == 0)
    def _():
        m_sc[...] = jnp.full_like(m_sc, NEG)
        acc_sc[...] = jnp.zeros_like(acc_sc)
        for h in range(N_HEADS):
            slab = q_ref[:, (h // 2) * LANES:(h // 2 + 1) * LANES] * (SCALE * LOG2E)
            q_sc[h] = jnp.where(_pair_mask(slab.shape, h), slab, 0.0).astype(BF16)

    def tile(diagonal):
        if diagonal:
            mask = lax.broadcasted_iota(I32, (1, tk), 1) <= lax.broadcasted_iota(I32, (tq, 1), 0)
        for h in range(N_HEADS):
            c0 = (h // 2) * LANES
            k_slab = kv_ref[:, c0:c0 + LANES].astype(BF16)
            s = lax.dot_general(q_sc[h], k_slab, NT, preferred_element_type=F32) - cr_ref[h:h + 1, :]
            if diagonal:
                s = jnp.where(mask, s, NEG)
            m_old = m_sc[h]
            m_new = jnp.maximum(m_old, jnp.max(s, axis=-1, keepdims=True))
            p = jnp.exp2(s - m_new).astype(BF16)
            v_slab = kv_ref[:, D_FOX + c0:D_FOX + c0 + LANES]
            v_aug = jnp.where(_pair_mask(v_slab.shape, h), v_slab, 1.0).astype(BF16)
            acc_sc[h] = jnp.exp2(m_old - m_new) * acc_sc[h] + jnp.dot(p, v_aug, preferred_element_type=F32)
            m_sc[h] = m_new

    @pl.when(j < i)
    def _():
        tile(False)

    @pl.when(j == i)
    def _():
        tile(True)

    @pl.when(j == nk - 1)
    def _():
        for h in range(N_HEADS):
            a = acc_sc[h]
            lo = (h % 2) * HEAD_DIM
            den = a[:, HEAD_DIM - lo:HEAD_DIM - lo + 1]
            o_ref[:, h * HEAD_DIM:(h + 1) * HEAD_DIM] = a[:, lo:lo + HEAD_DIM] / den


def _fox_prompt(fq, fkv, cr, B, S, t):
    n = S // t
    return pl.pallas_call(
        _fox_prompt_kernel,
        grid=(B, n, n),
        in_specs=[
            pl.BlockSpec((t, D_FOX), lambda b, i, j: (b * n + i, 0)),
            pl.BlockSpec((t, 2 * D_FOX), lambda b, i, j: (b * n + jnp.minimum(i, j), 0)),
            pl.BlockSpec((None, SUBLANES, t), lambda b, i, j: (b, 0, jnp.minimum(i, j))),
        ],
        out_specs=pl.BlockSpec((t, D_FOX), lambda b, i, j: (b * n + i, 0)),
        out_shape=jax.ShapeDtypeStruct((B * S, D_FOX), F32),
        scratch_shapes=[pltpu.VMEM((N_HEADS, t, LANES), BF16), pltpu.VMEM((N_HEADS, t, 1), F32),
                        pltpu.VMEM((N_HEADS, t, LANES), F32)],
        compiler_params=_params(("parallel", "parallel", "arbitrary")),
        name="fox_prompt",
    )(fq, fkv, cr)


def _compress_kernel(x_ref, pe_ref, w_ref, o_ref, *, exact):
    for s in range(2):
        x = x_ref[s] + pe_ref[s]
        w = w_ref[s]
        if exact:
            y = jnp.dot(x, w, precision=HI, preferred_element_type=F32)
        else:
            xh = x.astype(BF16)
            xl = (x - xh.astype(F32)).astype(BF16)
            wh = w.astype(BF16)
            wl = (w - wh.astype(F32)).astype(BF16)
            y = (jnp.dot(xh, wh, preferred_element_type=F32) + jnp.dot(xl, wh, preferred_element_type=F32)
                 + jnp.dot(xh, wl, preferred_element_type=F32))
        o_ref[:, s * HEAD_DIM:(s + 1) * HEAD_DIM] = y


def _compress(x3, pe, w, tr, exact, row0=0, n_rows=None):
    K = x3.shape[2]
    R = x3.shape[1] if n_rows is None else n_rows
    assert R % tr == 0 and row0 % tr == 0
    b0 = row0 // tr
    return pl.pallas_call(
        functools.partial(_compress_kernel, exact=exact),
        grid=(R // tr,),
        in_specs=[pl.BlockSpec((2, tr, K), lambda i: (0, b0 + i, 0)),
                  pl.BlockSpec((2, 1, K), lambda i: (0, 0, 0)),
                  pl.BlockSpec((2, K, HEAD_DIM), lambda i: (0, 0, 0))],
        out_specs=pl.BlockSpec((tr, LANES), lambda i: (i, 0)),
        out_shape=jax.ShapeDtypeStruct((R, LANES), F32),
        compiler_params=_params(("parallel",)),
        name="nsa_compress",
    )(x3, pe, w)


def _block_major(kv, n_blocks):
    return kv.reshape(n_blocks, CMP_LEN, 2, HEAD_DIM).transpose(2, 0, 1, 3).reshape(
        2, n_blocks, CMP_LEN * HEAD_DIM)


def _order_key(x):
    b = lax.bitcast_convert_type(x, I32)
    return jnp.where(b < 0, b ^ jnp.int32(0x7FFFFFFF), b)


def _nsa_prompt_kernel(nq_ref, nqr_ref, sm_ref, cmp_ref, rows_ref, win_ref, o_ref, qx_sc, *, tk):
    QB = nq_ref.shape[0]
    S = rows_ref.shape[0]
    nb = cmp_ref.shape[0]
    nsel = S // SEL_LEN
    i = pl.program_id(1)
    qs = i * QB
    qpos = qs + lax.broadcasted_iota(I32, (QB, 1), 0)

    cmp = cmp_ref[...]
    kc = cmp[:, 0:HEAD_DIM]
    vc = cmp[:, HEAD_DIM:2 * HEAD_DIM]
    n_l = lax.broadcasted_iota(I32, (1, nb), 1)
    complete = ((n_l + 1) * CMP_LEN - 1) <= qpos
    complete_f = complete.astype(F32)
    psum = jnp.zeros((QB, nb), F32)
    o_cmp = []
    for h in range(N_HEADS):
        qh = nq_ref[:, h * HEAD_DIM:(h + 1) * HEAD_DIM]
        s = lax.dot_general(qh, kc, NT, precision=HI, preferred_element_type=F32) * SCALE
        s = jnp.where(complete, s, NEG)
        e = jnp.exp(s - jnp.max(s, axis=-1, keepdims=True))
        p = e / jnp.sum(e, axis=-1, keepdims=True) * complete_f
        o_cmp.append(_bdot(p, cmp))
        psum = psum + p

    pj = lax.broadcasted_iota(I32, (nsel, nb), 0)
    pn = lax.broadcasted_iota(I32, (nsel, nb), 1)
    pair_t = (pn // (SEL_LEN // CMP_LEN) == pj).astype(F32)
    imp_t = lax.dot_general(pair_t, psum, NT, precision=HI, preferred_element_type=F32)
    jt = (qs + lax.broadcasted_iota(I32, (1, QB), 1)) // SEL_LEN
    jj = lax.broadcasted_iota(I32, (nsel, 1), 0)
    score = jnp.where(jj == jt, 2.0 * SEL_FORCE,
                      jnp.where((jj == 0) | (jj == jt - 1), SEL_FORCE,
                                jnp.where(jj <= jt, imp_t + 0.0, -1.0)))
    key = _order_key(score)
    key_m1 = key - 1
    ngrp = nsel // SUBLANES
    sub = lax.broadcasted_iota(I32, (SUBLANES, QB), 0)
    kg = [key[r * SUBLANES:(r + 1) * SUBLANES, :] for r in range(ngrp)]
    kg1 = [key_m1[r * SUBLANES:(r + 1) * SUBLANES, :] for r in range(ngrp)]
    cnt = [jnp.zeros((SUBLANES, QB), I32) for _ in range(ngrp)]
    for jp in range(nsel):
        g = jp // SUBLANES
        row = key[jp:jp + 1, :]
        mixed = jnp.where(sub > (jp % SUBLANES), kg1[g], kg[g])
        for r in range(ngrp):
            thr = kg[r] if r < g else (kg1[r] if r > g else mixed)
            cnt[r] = cnt[r] + (row > thr).astype(I32)
    sel_t = jnp.concatenate([(c < TOP_N).astype(F32) for c in cnt], axis=0)
    if nsel < QB:
        sel_t = jnp.concatenate([sel_t, jnp.zeros((QB - nsel, QB), F32)], axis=0)
    sel = sel_t.T.astype(BF16)

    lo_half = lax.broadcasted_iota(I32, (QB, LANES), 1) < HEAD_DIM
    for h in range(N_HEADS):
        slab = nqr_ref[:, (h // 2) * LANES:(h // 2 + 1) * LANES] * (SCALE * LOG2E)
        if h % 2:
            slab = pltpu.roll(slab, HEAD_DIM, 1)
        qx_sc[h * QB:(h + 1) * QB, :] = jnp.where(lo_half, slab, 0.0).astype(BF16)
    qx = qx_sc[...]
    HQ = N_HEADS * QB

    def attend(s, valid, slab, m_old, acc_old):
        n = s.shape[1]
        s = jnp.where(valid[None], s.reshape(N_HEADS, QB, n), NEG).reshape(HQ, n)
        m_new = jnp.maximum(m_old, jnp.max(s, axis=-1, keepdims=True))
        p = jnp.exp2(s - m_new).astype(BF16)
        ones_k = lax.broadcasted_iota(I32, slab.shape, 1) < HEAD_DIM
        v_aug = jnp.where(ones_k, 1.0, slab).astype(BF16)
        acc = jnp.exp2(m_old - m_new) * acc_old + jnp.dot(p, v_aug, preferred_element_type=F32)
        return m_new, acc

    jrow = lax.broadcasted_iota(I32, (QB, 1), 0)

    def sel_tile(k0, m_old, acc_old, diagonal):
        kpos = k0 + lax.broadcasted_iota(I32, (1, tk), 1)
        expand = (jrow == kpos // SEL_LEN).astype(BF16)
        valid = jnp.dot(sel, expand, preferred_element_type=F32) > 0.5
        if diagonal:
            valid = valid & (kpos <= qpos)
        slab = rows_ref[pl.ds(k0, tk), 2 * HEAD_DIM:4 * HEAD_DIM]
        s = lax.dot_general(qx, slab.astype(BF16), NT, preferred_element_type=F32)
        return attend(s, valid, slab, m_old, acc_old)

    n_full = qs // tk
    init = (jnp.full((HQ, 1), NEG, F32), jnp.zeros((HQ, LANES), F32))
    m_s, acc_s = lax.fori_loop(
        0, n_full, lambda t, c: sel_tile(pl.multiple_of(t * tk, tk), c[0], c[1], False), init)
    _, acc_s = sel_tile(pl.multiple_of(n_full * tk, tk), m_s, acc_s, True)

    wlen = WINDOW + QB
    w0 = pl.multiple_of(jnp.maximum(qs - WINDOW, 0), QB)
    wpos = w0 + lax.broadcasted_iota(I32, (1, wlen), 1)
    d = qpos - wpos
    wslab = win_ref[pl.ds(w0, wlen), :]
    sw = lax.dot_general(qx, wslab.astype(BF16), NT, preferred_element_type=F32)
    _, acc_w = attend(sw, (d >= 0) & (d < WINDOW), wslab,
                      jnp.full((HQ, 1), NEG, F32), jnp.zeros((HQ, LANES), F32))

    sm = sm_ref[...]
    for h in range(N_HEADS):
        rs = slice(h * QB, (h + 1) * QB)
        o_sel = acc_s[rs] * (1.0 / acc_s[rs, 0:1])
        o_win = acc_w[rs] * (1.0 / acc_w[rs, 0:1])
        c = SM_NG + 3 * h
        mix = sm[:, c:c + 1] * o_cmp[h] + sm[:, c + 1:c + 2] * o_sel + sm[:, c + 2:c + 3] * o_win
        if h % 2 == 0:
            mix = pltpu.roll(mix, HEAD_DIM, 1)
        lo = (h % 2) * HEAD_DIM
        o_ref[:, h * HEAD_DIM:(h + 1) * HEAD_DIM] = mix[:, lo:lo + HEAD_DIM]


def _nsa_prompt(nq, nqr, small, cmp, rows, win, B, S, tk):
    QB = 128
    nq_t = S // QB
    nb = S // CMP_LEN
    assert S % tk == 0 and S >= WINDOW + QB
    return pl.pallas_call(
        functools.partial(_nsa_prompt_kernel, tk=tk),
        grid=(B, nq_t),
        in_specs=[
            pl.BlockSpec((QB, D_NSA), lambda b, i: (b * nq_t + i, 0)),
            pl.BlockSpec((QB, D_NSA), lambda b, i: (b * nq_t + i, 0)),
            pl.BlockSpec((QB, LANES), lambda b, i: (b * nq_t + i, 0)),
            pl.BlockSpec((nb, LANES), lambda b, i: (b, 0)),
            pl.BlockSpec((S, 4 * HEAD_DIM), lambda b, i: (b, 0)),
            pl.BlockSpec((S, 2 * HEAD_DIM), lambda b, i: (b, 0)),
        ],
        out_specs=pl.BlockSpec((QB, D_NSA), lambda b, i: (b * nq_t + i, 0)),
        out_shape=jax.ShapeDtypeStruct((B * S, D_NSA), F32),
        scratch_shapes=[pltpu.VMEM((N_HEADS * QB, LANES), BF16)],
        compiler_params=_params(("parallel", "parallel")),
        name="nsa_prompt",
    )(nq, nqr, small, cmp, rows, win)


def _gla_prompt_kernel(qk_ref, v_ref, g_ref, o_ref, st_ref, s_sc):
    t = pl.program_id(1)
    nt = pl.num_programs(1)
    tc = qk_ref.shape[0]
    C = GLA_CHUNK

    @pl.when(t == 0)
    def _():
        s_sc[...] = jnp.zeros_like(s_sc)

    r = lax.broadcasted_iota(I32, (tc, tc), 0)
    c = lax.broadcasted_iota(I32, (tc, tc), 1)
    same = (r // C) == (c // C)
    causal = same & (c <= r)
    g = g_ref[...]
    gcum = jnp.dot(causal.astype(F32), g, precision=HI, preferred_element_type=F32)
    g_t = g.T
    gcum_t = jnp.dot(g_t, (same & (r <= c)).astype(F32), precision=HI, preferred_element_type=F32)
    gtot_t = jnp.dot(g_t, same.astype(F32), precision=HI, preferred_element_type=F32)
    q_e = (qk_ref[:, 0:D_FOX] * SCALE * jnp.exp(gcum)).astype(BF16)
    k_e = (qk_ref[:, D_FOX:2 * D_FOX] * jnp.exp(-gcum)).astype(BF16)
    kd_t = (qk_ref[:, D_FOX:2 * D_FOX].T * jnp.exp(gtot_t - gcum_t)).astype(BF16)
    decay_t = jnp.exp(gtot_t)
    for h in range(N_HEADS):
        hs = slice(h * HEAD_DIM, (h + 1) * HEAD_DIM)
        v = v_ref[:, h * GLA_DV:(h + 1) * GLA_DV].astype(BF16)
        a = jnp.where(causal, lax.dot_general(q_e[:, hs], k_e[:, hs], NT, preferred_element_type=F32), 0.0)
        o_intra = jnp.dot(a.astype(BF16), v, preferred_element_type=F32)
        state = s_sc[h]
        for ci in range(tc // C):
            rs = slice(ci * C, (ci + 1) * C)
            o_ref[rs, h * GLA_DV:(h + 1) * GLA_DV] = (
                o_intra[rs] + jnp.dot(q_e[rs, hs], state.astype(BF16), preferred_element_type=F32))
            state = (decay_t[hs, ci * C:ci * C + 1] * state
                     + jnp.dot(kd_t[hs, rs], v[rs], preferred_element_type=F32))
        s_sc[h] = state

    @pl.when(t == nt - 1)
    def _():
        st_ref[...] = s_sc[...]


def _gla_prompt(gqk, gv, glog, B, S, tc):
    nt = S // tc
    return pl.pallas_call(
        _gla_prompt_kernel,
        grid=(B, nt),
        in_specs=[pl.BlockSpec((tc, 2 * D_FOX), lambda b, t: (b * nt + t, 0)),
                  pl.BlockSpec((tc, D_GLA), lambda b, t: (b * nt + t, 0)),
                  pl.BlockSpec((tc, D_FOX), lambda b, t: (b * nt + t, 0))],
        out_specs=[pl.BlockSpec((tc, D_GLA), lambda b, t: (b * nt + t, 0)),
                   pl.BlockSpec((None, N_HEADS, HEAD_DIM, GLA_DV), lambda b, t: (b, 0, 0, 0))],
        out_shape=[jax.ShapeDtypeStruct((B * S, D_GLA), F32),
                   jax.ShapeDtypeStruct((B, N_HEADS, HEAD_DIM, GLA_DV), F32)],
        scratch_shapes=[pltpu.VMEM((N_HEADS, HEAD_DIM, GLA_DV), F32)],
        compiler_params=_params(("parallel", "arbitrary")),
        name="gla_prompt",
    )(gqk, gv, glog)


def _outproj_kernel(x_ref, of_ref, on_ref, og_ref, gog_ref, gn_ref, w_ref, o_ref):
    acc = _wdot(of_ref[...], w_ref[0:D_FOX, :])
    acc = acc + _wdot(on_ref[...], w_ref[D_FOX:D_FOX + D_NSA, :])
    for h in range(N_HEADS):
        hs = slice(h * GLA_DV, (h + 1) * GLA_DV)
        z = _rms(og_ref[:, hs], gn_ref[...]) * _silu(gog_ref[:, hs])
        w0 = D_FOX + D_NSA + h * GLA_DV
        acc = acc + _wdot(z, w_ref[w0:w0 + GLA_DV, :])
    o_ref[...] = x_ref[...] + acc


def _out_projection(x, o_fox, o_nsa, o_gla, gog, gn, w_out, tm):
    T = x.shape[0]
    assert T % tm == 0
    row = lambda w: pl.BlockSpec((tm, w), lambda i: (i, 0))
    return pl.pallas_call(
        _outproj_kernel,
        grid=(T // tm,),
        in_specs=[row(D_MODEL), row(D_FOX), row(D_NSA), row(D_GLA), row(D_GLA),
                  pl.BlockSpec((1, GLA_DV), lambda i: (0, 0)),
                  pl.BlockSpec((D_MODEL, D_MODEL), lambda i: (0, 0))],
        out_specs=row(D_MODEL),
        out_shape=jax.ShapeDtypeStruct((T, D_MODEL), F32),
        compiler_params=_params(("parallel",)),
        name="out_projection",
    )(x, o_fox, o_nsa, o_gla, gog, gn, w_out)


def _dense_ffn_kernel(x_ref, g_ref, wg_ref, wu_ref, wd_ref, o_ref, h_sc, acc_sc):
    f = pl.program_id(1)
    nf = pl.num_programs(1)

    @pl.when(f == 0)
    def _():
        h_sc[...] = _rms(x_ref[...], g_ref[...]).astype(h_sc.dtype)
        acc_sc[...] = jnp.zeros_like(acc_sc)

    h = h_sc[...]
    a = _wdot(h, wg_ref[...])
    u = _wdot(h, wu_ref[...])
    acc_sc[...] += _wdot(_silu(a) * u, wd_ref[...])

    @pl.when(f == nf - 1)
    def _():
        o_ref[...] = x_ref[...] + acc_sc[...]


def _dense_ffn(x, g, wg, wu, wd, tm, tf):
    T = x.shape[0]
    assert T % tm == 0 and D_FF % tf == 0
    return pl.pallas_call(
        _dense_ffn_kernel,
        grid=(T // tm, D_FF // tf),
        in_specs=[pl.BlockSpec((tm, D_MODEL), lambda i, f: (i, 0)),
                  pl.BlockSpec((1, D_MODEL), lambda i, f: (0, 0)),
                  pl.BlockSpec((D_MODEL, tf), lambda i, f: (0, f)),
                  pl.BlockSpec((D_MODEL, tf), lambda i, f: (0, f)),
                  pl.BlockSpec((tf, D_MODEL), lambda i, f: (f, 0))],
        out_specs=pl.BlockSpec((tm, D_MODEL), lambda i, f: (i, 0)),
        out_shape=jax.ShapeDtypeStruct((T, D_MODEL), F32),
        scratch_shapes=[pltpu.VMEM((tm, D_MODEL), wg.dtype), pltpu.VMEM((tm, D_MODEL), F32)],
        compiler_params=_params(("parallel", "arbitrary")),
        name="dense_ffn",
    )(x, g, wg, wu, wd)


def _router_kernel(x_ref, g_ref, wr_ref, h_ref, r_ref):
    h = _rms(x_ref[...], g_ref[...])
    h_ref[...] = h
    logits = jnp.dot(h, wr_ref[...], precision=HI, preferred_element_type=F32)
    lane = lax.broadcasted_iota(I32, logits.shape, 1)
    lg = jnp.where(lane < N_EXPERTS, logits, -jnp.inf)
    m1 = jnp.max(lg, axis=-1, keepdims=True)
    i1 = jnp.min(jnp.where(lg == m1, lane, LANES), axis=-1, keepdims=True)
    lg2 = jnp.where(lane == i1, -jnp.inf, lg)
    m2 = jnp.max(lg2, axis=-1, keepdims=True)
    i2 = jnp.min(jnp.where(lg2 == m2, lane, LANES), axis=-1, keepdims=True)
    e = jnp.exp(m2 - m1)
    den = 1.0 + e
    r_ref[...] = jnp.where(lane == 0, i1.astype(F32),
                           jnp.where(lane == 1, i2.astype(F32),
                                     jnp.where(lane == 2, 1.0 / den,
                                               jnp.where(lane == 3, e / den, 0.0))))


def _router(x, g, wr_pad, tm):
    T = x.shape[0]
    assert T % tm == 0
    return pl.pallas_call(
        _router_kernel,
        grid=(T // tm,),
        in_specs=[pl.BlockSpec((tm, D_MODEL), lambda i: (i, 0)),
                  pl.BlockSpec((1, D_MODEL), lambda i: (0, 0)),
                  pl.BlockSpec((D_MODEL, LANES), lambda i: (0, 0))],
        out_specs=[pl.BlockSpec((tm, D_MODEL), lambda i: (i, 0)),
                   pl.BlockSpec((tm, LANES), lambda i: (i, 0))],
        out_shape=[jax.ShapeDtypeStruct((T, D_MODEL), F32),
                   jax.ShapeDtypeStruct((T, LANES), F32)],
        compiler_params=_params(("parallel",)),
        name="moe_router",
    )(x, g, wr_pad)


GATHER_WINDOW = 32
SC_WORKERS = 32


def _row_gather(src, idx):
    n = idx.shape[0]
    step = GATHER_WINDOW * SC_WORKERS
    n_pad = -(-n // step) * step
    if n_pad != n:
        idx = jnp.concatenate([idx, jnp.zeros((n_pad - n,), idx.dtype)])
    width = src.shape[1]
    per_worker = n_pad // SC_WORKERS
    mesh = plsc.VectorSubcoreMesh(core_axis_name="core", subcore_axis_name="subcore")

    @functools.partial(pl.kernel, out_type=jax.ShapeDtypeStruct((n_pad, width), src.dtype), mesh=mesh,
                       scratch_types=[pltpu.VMEM((per_worker,), I32),
                                      pltpu.VMEM((GATHER_WINDOW, width), src.dtype)],
                       name="row_gather")
    def gather(src_hbm, idx_hbm, dst_hbm, idx_v, buf):
        worker = lax.axis_index("core") * (SC_WORKERS // 2) + lax.axis_index("subcore")
        base = worker * per_worker
        pltpu.sync_copy(idx_hbm.at[pl.ds(base, per_worker)], idx_v)

        @pl.loop(0, per_worker // GATHER_WINDOW)
        def _(j):
            pltpu.sync_copy(src_hbm.at[idx_v.at[pl.ds(j * GATHER_WINDOW, GATHER_WINDOW)]], buf)
            pltpu.sync_copy(buf, dst_hbm.at[pl.ds(base + j * GATHER_WINDOW, GATHER_WINDOW)])

    return gather(src, idx)


def _moe_ffn_kernel(be_ref, nu_ref, x_ref, wg_ref, wu_ref, wd_ref, o_ref, acc_sc):
    b = pl.program_id(0)
    f = pl.program_id(1)
    nf = pl.num_programs(1)
    used = b < nu_ref[0]

    @pl.when(used)
    def _():
        @pl.when(f == 0)
        def _():
            acc_sc[...] = jnp.zeros_like(acc_sc)

        x = x_ref[...].astype(BF16)
        a = jnp.dot(x, wg_ref[...].astype(BF16), preferred_element_type=F32)
        u = jnp.dot(x, wu_ref[...].astype(BF16), preferred_element_type=F32)
        acc_sc[...] += _bdot(_silu(a) * u, wd_ref[...])

        @pl.when(f == nf - 1)
        def _():
            o_ref[...] = acc_sc[...]

    @pl.when(jnp.logical_not(used) & (f == nf - 1))
    def _():
        o_ref[...] = jnp.zeros_like(o_ref)


def _moe_ffn(xb, block_e, n_used, wg, wu, wd, blk, tf):
    cap = xb.shape[0]
    nb = cap // blk
    nf = D_FF // tf

    def bsel(b, nu):
        return jnp.minimum(b, nu[0] - 1)

    def fsel(b, f, nu):
        return jnp.where(b < nu[0], f, nf - 1)

    return pl.pallas_call(
        _moe_ffn_kernel,
        grid_spec=pltpu.PrefetchScalarGridSpec(
            num_scalar_prefetch=2,
            grid=(nb, nf),
            in_specs=[
                pl.BlockSpec((blk, D_MODEL), lambda b, f, be, nu: (bsel(b, nu), 0)),
                pl.BlockSpec((None, D_MODEL, tf), lambda b, f, be, nu: (be[bsel(b, nu)], 0, fsel(b, f, nu))),
                pl.BlockSpec((None, D_MODEL, tf), lambda b, f, be, nu: (be[bsel(b, nu)], 0, fsel(b, f, nu))),
                pl.BlockSpec((None, tf, D_MODEL), lambda b, f, be, nu: (be[bsel(b, nu)], fsel(b, f, nu), 0)),
            ],
            out_specs=pl.BlockSpec((blk, D_MODEL), lambda b, f, be, nu: (b, 0)),
            scratch_shapes=[pltpu.VMEM((blk, D_MODEL), F32)],
        ),
        out_shape=jax.ShapeDtypeStruct((cap, D_MODEL), F32),
        compiler_params=_params(("arbitrary", "arbitrary")),
        name="moe_ffn",
    )(block_e, n_used, xb, wg, wu, wd)


def _moe_combine_kernel(x_ref, y1_ref, y2_ref, r_ref, o_ref):
    r = r_ref[...]
    o_ref[...] = x_ref[...] + (r[:, 2:3] * y1_ref[...] + r[:, 3:4] * y2_ref[...])


def _moe_combine(x, y1, y2, route, tm, row0):
    T = x.shape[0]
    nt = T // tm
    assert row0 % tm == 0
    b0 = row0 // tm
    return pl.pallas_call(
        _moe_combine_kernel,
        grid=(nt,),
        in_specs=[pl.BlockSpec((tm, D_MODEL), lambda i: (i, 0)),
                  pl.BlockSpec((tm, D_MODEL), lambda i: (b0 + i, 0)),
                  pl.BlockSpec((tm, D_MODEL), lambda i: (b0 + i, 0)),
                  pl.BlockSpec((tm, LANES), lambda i: (i, 0))],
        out_specs=pl.BlockSpec((tm, D_MODEL), lambda i: (i, 0)),
        out_shape=jax.ShapeDtypeStruct((T, D_MODEL), F32),
        compiler_params=_params(("parallel",)),
        name="moe_combine",
    )(x, y1, y2, route)


def _moe_plan(e_top, blk):
    T = e_top.shape[0]
    n = 2 * T
    flat_e = e_top.reshape(-1)
    onehot = (flat_e[:, None] == jnp.arange(N_EXPERTS, dtype=I32)[None, :]).astype(I32)
    csum = jnp.cumsum(onehot, axis=0)
    rank = jnp.sum((csum - onehot) * onehot, axis=1)
    counts = csum[-1]
    padded = (counts + blk - 1) // blk * blk
    ends = jnp.cumsum(padded)
    pstart = ends - padded
    dest = (pstart[flat_e] + rank).astype(I32)
    n_blocks = -(-n // blk) + N_EXPERTS
    cap = n_blocks * blk
    slot_tok = jnp.zeros((cap,), I32).at[dest].set(jnp.arange(n, dtype=I32) // 2)
    first = jnp.arange(n_blocks, dtype=I32) * blk
    block_e = jnp.minimum(jnp.sum((ends[None, :] <= first[:, None]).astype(I32), axis=1), N_EXPERTS - 1)
    n_used = (ends[-1] // blk).astype(I32).reshape(1)
    return dest, slot_tok, block_e, n_used


def _moe_layer(xs, g, wr_pad, wg, wu, wd, tms, blk, tf):
    routed = [_router(x, g, wr_pad, tm) for x, tm in zip(xs, tms)]
    h = jnp.concatenate([r[0] for r in routed], axis=0) if len(xs) > 1 else routed[0][0]
    e_top = jnp.concatenate([r[1][:, 0:2] for r in routed], axis=0).astype(I32)
    dest, slot_tok, block_e, n_used = _moe_plan(e_top, blk)
    xb = _row_gather(h, slot_tok)
    yb = _moe_ffn(xb, block_e, n_used, wg, wu, wd, blk, tf)
    d2 = dest.reshape(-1, 2)
    y1 = _row_gather(yb, d2[:, 0])
    y2 = _row_gather(yb, d2[:, 1])
    out, row0 = [], 0
    for x, tm, r in zip(xs, tms, routed):
        out.append(_moe_combine(x, y1, y2, r[1], tm, row0))
        row0 += x.shape[0]
    return out


def _norm_kernel(x_ref, g_ref, o_ref):
    o_ref[...] = _rms(x_ref[...], g_ref[...])


def _final_norm(x, g, tm):
    T = x.shape[0]
    return pl.pallas_call(
        _norm_kernel,
        grid=(T // tm,),
        in_specs=[pl.BlockSpec((tm, D_MODEL), lambda i: (i, 0)),
                  pl.BlockSpec((1, D_MODEL), lambda i: (0, 0))],
        out_specs=pl.BlockSpec((tm, D_MODEL), lambda i: (i, 0)),
        out_shape=jax.ShapeDtypeStruct((T, D_MODEL), F32),
        compiler_params=_params(("parallel",)),
        name="final_norm",
    )(x, g)


_IN_SPLITS = (D_FOX, D_FOX, D_FOX, N_HEADS, D_NSA, 6 * HEAD_DIM, 3 * N_HEADS,
              D_FOX, D_FOX, D_GLA, GLA_RANK, D_GLA)


def _reorder_w_in(w):
    offs = [0]
    for s in _IN_SPLITS:
        offs.append(offs[-1] + s)
    seg = lambda k: w[:, offs[k]:offs[k + 1]]
    fq, fk, fv, ff, nq, nkv, ng, gq, gk, gv, glr, gog = [seg(k) for k in range(12)]
    pad = jnp.zeros((w.shape[0], LANES - SM_GLR - GLA_RANK), w.dtype)
    return jnp.concatenate([fq, fk, fv, nq, nkv, gq, gk, gv, gog, ff, ng, glr, pad], axis=1)


def _rope_table(pos):
    inv = ROPE_THETA ** (-jnp.arange(ROPE_HALF, dtype=F32) / ROPE_HALF)
    ang = pos.astype(F32)[:, None] * inv[None, :]
    cos, sin = jnp.cos(ang), jnp.sin(ang)
    P = pos.shape[0]
    one = jnp.ones((P, HEAD_DIM - ROPE_DIM), F32)
    zero = jnp.zeros((P, HEAD_DIM - ROPE_DIM), F32)
    z8 = jnp.zeros((P, ROPE_HALF), F32)
    a64 = jnp.concatenate([cos, cos, one], axis=1)
    p64 = jnp.concatenate([z8, sin, zero], axis=1)
    m64 = jnp.concatenate([-sin, z8, zero], axis=1)
    i64 = jnp.ones((P, HEAD_DIM), F32)
    o64 = jnp.zeros((P, HEAD_DIM), F32)
    return jnp.concatenate([a64, a64, p64, p64, m64, m64, a64, i64, p64, o64, m64, o64], axis=1)


def _layer_mix_params(l, norm_mix_g, w_in, b_fox_f, w_cmp, pe_cmp, w_gla_gk, b_gla_gk, g_gla_norm, w_out):
    sb = jnp.zeros((1, LANES), F32).at[0, SM_FF:SM_FF + N_HEADS].set(b_fox_f[l])
    wgk = jnp.zeros((LANES, D_FOX), F32).at[SM_GLR:SM_GLR + GLA_RANK].set(w_gla_gk[l])
    w_r = _reorder_w_in(w_in[l])
    return dict(g=norm_mix_g[l].reshape(1, D_MODEL), w_r=w_r.astype(BF16), w_r32=w_r, sb=sb,
                wgk=wgk.astype(BF16), wgk32=wgk, w_out32=w_out[l],
                bgk=b_gla_gk[l].reshape(1, D_FOX), w_cmp=w_cmp[l],
                pe_cmp=pe_cmp[l].reshape(2, 1, CMP_LEN * HEAD_DIM),
                gn=g_gla_norm[l].reshape(1, GLA_DV), w_out=w_out[l].astype(BF16))


def _mix_prompt(x, B, S, p, tab, tm, t_fox, tk_sel, tc_gla):
    (fq, fkv, nq, nqr, rows, win, gqk, gv, gog, glog, small) = _in_projection(
        x, 0, B * S, tm, p['g'], p['w_r'], p['sb'], p['wgk'], p['bgk'], tab, S)
    cr = _fox_cumsum(small, B, S, min(S, 512))
    o_fox = _fox_prompt(fq, fkv, cr, B, S, t_fox)
    n_blk = B * S // CMP_LEN
    blocks = _block_major(rows[:, 0:2 * HEAD_DIM].reshape(B * S, 2, HEAD_DIM), n_blk)
    cmp = _compress(blocks, p['pe_cmp'], p['w_cmp'], min(256, n_blk), True)
    o_nsa = _nsa_prompt(nq, nqr, small, cmp, rows, win, B, S, tk_sel)
    o_gla, g_state = _gla_prompt(gqk, gv, glog, B, S, tc_gla)
    x_new = _out_projection(x, o_fox, o_nsa, o_gla, gog, p['gn'], p['w_out'], tm)
    return x_new, dict(fkv=fkv, small=small, rows=rows, win=win, g_state=g_state)


def _per_head_col(vals):
    r = lax.broadcasted_iota(I32, (SUBLANES, 1), 0)
    out = jnp.zeros((SUBLANES, 1), F32)
    for h, v in enumerate(vals):
        out = out + jnp.where(r == h, v, 0.0)
    return out


def _per_head_row(vals, width):
    grp = lax.broadcasted_iota(I32, (1, width), 1) // HEAD_DIM
    out = jnp.zeros((1, width), F32)
    for h, v in enumerate(vals):
        out = out + jnp.where(grp == h, v, 0.0)
    return out


def _head_lane_sums(row):
    grp = lax.broadcasted_iota(I32, row.shape, 1) // HEAD_DIM
    return [jnp.sum(jnp.where(grp == h, row, 0.0), axis=1, keepdims=True) for h in range(N_HEADS)]


def _cols_of(row):
    return jnp.concatenate([jnp.broadcast_to(row[:, j:j + LANES], (LANES, LANES)).T
                            for j in range(0, row.shape[1], LANES)], axis=0)


def _row_of(col):
    return jnp.concatenate([jnp.broadcast_to(col[j:j + LANES], (LANES, LANES)).T[0:1, :]
                            for j in range(0, col.shape[0], LANES)], axis=1)


def _sublane_group_sum(x):
    return jnp.sum(x.reshape(x.shape[0] // SUBLANES, SUBLANES, x.shape[1]), axis=0)


def _fold_matrix(n_pages):
    r = lax.broadcasted_iota(I32, (n_pages * SUBLANES, n_pages * N_HEADS * SUBLANES), 0)
    c = lax.broadcasted_iota(I32, (n_pages * SUBLANES, n_pages * N_HEADS * SUBLANES), 1)
    blk = c // SUBLANES
    return ((blk // N_HEADS == r // SUBLANES) & (blk % N_HEADS == r % SUBLANES)).astype(F32)


def _fox_decode_kernel(pt_ref, q_ref, kvn_ref, smn_ref, *refs, n_pages):
    del pt_ref
    kv_refs = refs[0:n_pages]
    lf_refs = refs[n_pages:2 * n_pages]
    o_ref, lf_sc, part_sc = refs[2 * n_pages:]
    R = n_pages * SUBLANES
    PG = kv_refs[0].shape[-1]
    row = pl.ds(pl.program_id(0) % SUBLANES, 1)
    q_row = q_ref[row, :]
    kvn = kvn_ref[row, :]
    smn = smn_ref[row, :]
    q_cols = _cols_of(q_row)

    lf_sc[...] = jnp.zeros_like(lf_sc)
    for p in range(n_pages):
        lf_sc[p * SUBLANES:p * SUBLANES + N_HEADS, :] = lf_refs[p][...]
    lft = lf_sc[...]
    k0 = lax.broadcasted_iota(I32, (PG, PG), 0)
    k1 = lax.broadcasted_iota(I32, (PG, PG), 1)
    within = jnp.dot(lft, (k0 > k1).astype(F32), precision=HI, preferred_element_type=F32)
    tot = jnp.broadcast_to(jnp.sum(lft, axis=1, keepdims=True), (R, PG))
    r0 = lax.broadcasted_iota(I32, (R, R), 0)
    r1 = lax.broadcasted_iota(I32, (R, R), 1)
    later = ((r1 % SUBLANES == r0 % SUBLANES) & (r1 // SUBLANES > r0 // SUBLANES)).astype(F32)
    cross = jnp.dot(later, tot, precision=HI, preferred_element_type=F32)
    rr = lax.broadcasted_iota(I32, (R, 1), 0) % SUBLANES
    newcol = jnp.zeros((R, 1), F32)
    for h in range(N_HEADS):
        newcol = newcol + jnp.where(rr == h, smn[:, SM_FF + h:SM_FF + h + 1], 0.0)
    bias = (within + cross + newcol).reshape(n_pages, SUBLANES, PG)

    for p in range(n_pages):
        for h in range(N_HEADS):
            g = p * N_HEADS + h
            part_sc[g * SUBLANES:(g + 1) * SUBLANES, :] = _sublane_group_sum(
                kv_refs[p][0, h] * q_cols[h * HEAD_DIM:(h + 1) * HEAD_DIM])
    s = jnp.dot(_fold_matrix(n_pages), part_sc[...], precision=HI, preferred_element_type=F32)
    s3 = s.reshape(n_pages, SUBLANES, PG) * SCALE + bias
    s_new = _per_head_col(_head_lane_sums(q_row * kvn[:, 0:D_FOX])) * SCALE
    m = jnp.max(jnp.max(s3, axis=2, keepdims=True), axis=0)
    m = jnp.maximum(m, s_new)
    p3 = jnp.exp(s3 - m[None])
    pn = jnp.exp(s_new - m)
    inv = 1.0 / (jnp.sum(jnp.sum(p3, axis=2, keepdims=True), axis=0) + pn)
    o_cols = []
    for h in range(N_HEADS):
        acc = jnp.zeros((HEAD_DIM, PG), F32)
        for p in range(n_pages):
            acc = acc + kv_refs[p][1, h] * p3[p, h:h + 1, :]
        o_cols.append(jnp.sum(acc, axis=1, keepdims=True) * inv[h:h + 1])
    w_new = _per_head_row([pn[h:h + 1] * inv[h:h + 1] for h in range(N_HEADS)], D_FOX)
    o_ref[row, :] = _row_of(jnp.concatenate(o_cols, axis=0)) + w_new * kvn[:, D_FOX:2 * D_FOX]


def _fox_decode(l, pt_flat, n_pages, fq, fkv, small, kv_cache_t, lft_cache):
    DB = fq.shape[0]
    PG = kv_cache_t.shape[-1]
    page = lambda p, nz: (lambda b, pt: (l, pt[b * n_pages + p]) + (0,) * nz)
    rows8 = lambda w: pl.BlockSpec((SUBLANES, w), lambda b, pt: (b // SUBLANES, 0))
    return pl.pallas_call(
        functools.partial(_fox_decode_kernel, n_pages=n_pages),
        grid_spec=pltpu.PrefetchScalarGridSpec(
            num_scalar_prefetch=1,
            grid=(DB,),
            in_specs=[rows8(D_FOX), rows8(2 * D_FOX), rows8(LANES)]
            + [pl.BlockSpec((None, None, 2, N_HEADS, HEAD_DIM, PG), page(p, 4)) for p in range(n_pages)]
            + [pl.BlockSpec((None, None, N_HEADS, PG), page(p, 2)) for p in range(n_pages)],
            out_specs=rows8(D_FOX),
            scratch_shapes=[pltpu.VMEM((n_pages * SUBLANES, PG), F32),
                            pltpu.VMEM((n_pages * N_HEADS * SUBLANES, PG), F32)],
        ),
        out_shape=jax.ShapeDtypeStruct((DB, D_FOX), F32),
        compiler_params=_params(("arbitrary",)),
        name="fox_decode",
    )(pt_flat, fq, fkv, small, *([kv_cache_t] * n_pages), *([lft_cache] * n_pages))


def _nsa_decode_kernel(pt_ref, q_ref, qr_ref, rown_ref, winn_ref, sm_ref, win_ref, *refs,
                       n_pages, past_len):
    del pt_ref
    pg_refs = refs[0:n_pages]
    cmp_refs = refs[n_pages:2 * n_pages]
    o_ref, nw_ref, cmp_sc, qc_sc, part_sc, sw_sc = refs[2 * n_pages:]
    R = n_pages * SUBLANES
    PG = pg_refs[0].shape[-1]
    WB = win_ref.shape[-1]
    per_page = PG // CMP_LEN
    assert per_page <= SUBLANES and PG == 2 * SEL_LEN and R == LANES
    jt = past_len // SEL_LEN
    row = pl.ds(pl.program_id(0) % SUBLANES, 1)
    q_row = q_ref[row, :]
    qr_row = qr_ref[row, :]
    rown = rown_ref[row, :]
    winn = winn_ref[row, :]
    smn = sm_ref[row, :]
    qr_cols = _cols_of(qr_row)
    rep4 = lambda r64: jnp.concatenate([r64] * N_HEADS, axis=1)

    qc_sc[...] = jnp.zeros_like(qc_sc)
    for h in range(N_HEADS):
        qc_sc[h:h + 1, 0:HEAD_DIM] = q_row[:, h * HEAD_DIM:(h + 1) * HEAD_DIM]
    head_row = lax.broadcasted_iota(I32, (SUBLANES, 1), 0) < N_HEADS

    cmp_sc[...] = jnp.zeros_like(cmp_sc)
    for p in range(n_pages):
        cmp_sc[p * SUBLANES:p * SUBLANES + per_page, :] = cmp_refs[p][...]
    cmpa = cmp_sc[...]
    lane = lax.broadcasted_iota(I32, (1, R), 1)
    blk = per_page * (lane // SUBLANES) + lane % SUBLANES
    complete = (lane % SUBLANES < per_page) & ((blk + 1) * CMP_LEN - 1 <= past_len)
    s = lax.dot_general(qc_sc[...], cmpa, NT, precision=HI, preferred_element_type=F32) * SCALE
    s = jnp.where(complete, s, NEG)
    e = jnp.exp(s - jnp.max(s, axis=-1, keepdims=True))
    pc = e / jnp.sum(e, axis=-1, keepdims=True) * complete.astype(F32)
    vcb_t = cmpa.T[HEAD_DIM:2 * HEAD_DIM, :]
    o_cmp = [jnp.sum(vcb_t * pc[h:h + 1, :], axis=1, keepdims=True) for h in range(N_HEADS)]

    imp_c = jnp.sum(jnp.where(head_row, pc, 0.0), axis=0, keepdims=True)
    imp_s = imp_c + pltpu.roll(imp_c, R - 1, 1)
    cand = (lane % SUBLANES == 0) | (lane % SUBLANES == 2)
    jsel = 2 * (lane // SUBLANES) + (lane % SUBLANES) // 2
    score = jnp.where(jsel == jt, 2.0 * SEL_FORCE,
                      jnp.where((jsel == 0) | (jsel == jt - 1), SEL_FORCE,
                                jnp.where(jsel <= jt, imp_s + 0.0, -1.0)))
    score_b = jnp.broadcast_to(score, (R, R))
    key_row = _order_key(score_b)
    key_col = _order_key(score_b.T)
    l0 = lax.broadcasted_iota(I32, (R, R), 0)
    l1 = lax.broadcasted_iota(I32, (R, R), 1)
    cand_col = (l0 % SUBLANES == 0) | (l0 % SUBLANES == 2)
    beats = cand_col & (key_col > jnp.where(l0 < l1, key_row - 1, key_row))
    cnt = jnp.sum(beats.astype(I32), axis=0, keepdims=True)
    sel_row = (cand & (cnt < TOP_N - 1)).astype(F32)
    sel_col = jnp.broadcast_to(sel_row, (R, R)).T
    half = ((l0 % SUBLANES == 0) & (l1 < SEL_LEN)) | ((l0 % SUBLANES == 2) & (l1 >= SEL_LEN))
    z = jnp.where(half, sel_col, 0.0)
    same_page = (l1 // SUBLANES == l0 // SUBLANES).astype(BF16)
    picked = jnp.dot(same_page, z.astype(BF16), preferred_element_type=F32)
    picked = picked.reshape(n_pages, SUBLANES, PG) > 0.5

    for p in range(n_pages):
        ks_t = pg_refs[p][2]
        for h in range(N_HEADS):
            g = p * N_HEADS + h
            part_sc[g * SUBLANES:(g + 1) * SUBLANES, :] = _sublane_group_sum(
                ks_t * qr_cols[h * HEAD_DIM:(h + 1) * HEAD_DIM])
    s = jnp.dot(_fold_matrix(n_pages), part_sc[...], precision=HI, preferred_element_type=F32)
    s3 = jnp.where(picked, s.reshape(n_pages, SUBLANES, PG) * SCALE, NEG)
    s_new = _per_head_col(_head_lane_sums(qr_row * rep4(rown[:, 2 * HEAD_DIM:3 * HEAD_DIM]))) * SCALE
    m = jnp.maximum(jnp.max(jnp.max(s3, axis=2, keepdims=True), axis=0), s_new)
    p3 = jnp.exp(s3 - m[None])
    pn = jnp.exp(s_new - m)
    inv = 1.0 / (jnp.sum(jnp.sum(p3, axis=2, keepdims=True), axis=0) + pn)
    o_sel = []
    for h in range(N_HEADS):
        acc = jnp.zeros((HEAD_DIM, PG), F32)
        for p in range(n_pages):
            acc = acc + pg_refs[p][3] * p3[p, h:h + 1, :]
        o_sel.append(jnp.sum(acc, axis=1, keepdims=True) * inv[h:h + 1])

    kw_t = win_ref[0]
    vw_t = win_ref[1]
    wlane = lax.broadcasted_iota(I32, (1, WB), 1)
    wpos = past_len - WB + wlane
    wd = past_len - wpos
    wok = (wd >= 0) & (wd < WINDOW) & (wpos >= 0)
    sw_sc[...] = jnp.zeros_like(sw_sc)
    for h in range(N_HEADS):
        qh = qr_cols[h * HEAD_DIM:(h + 1) * HEAD_DIM]
        sw_sc[h:h + 1, :] = jnp.sum(kw_t * jnp.concatenate([qh] * (WB // LANES), axis=1), axis=0, keepdims=True)
    sw = jnp.where(wok, sw_sc[...] * SCALE, NEG)
    sw_new = _per_head_col(_head_lane_sums(qr_row * rep4(winn[:, 0:HEAD_DIM]))) * SCALE
    mw = jnp.maximum(jnp.max(sw, axis=-1, keepdims=True), sw_new)
    ew = jnp.exp(sw - mw)
    en = jnp.exp(sw_new - mw)
    invw = 1.0 / (jnp.sum(ew, axis=-1, keepdims=True) + en)

    gate = lambda h, c: smn[:, SM_NG + 3 * h + c:SM_NG + 3 * h + c + 1]
    o_cols = []
    for h in range(N_HEADS):
        o_win = jnp.sum(vw_t * ew[h:h + 1, :], axis=1, keepdims=True) * invw[h:h + 1]
        o_cols.append(gate(h, 0) * o_cmp[h] + gate(h, 1) * o_sel[h] + gate(h, 2) * o_win)
    w_sel = _per_head_row([gate(h, 1) * pn[h:h + 1] * inv[h:h + 1] for h in range(N_HEADS)], D_NSA)
    w_win = _per_head_row([gate(h, 2) * en[h:h + 1] * invw[h:h + 1] for h in range(N_HEADS)], D_NSA)
    o_ref[row, :] = (_row_of(jnp.concatenate(o_cols, axis=0))
                     + w_sel * rep4(rown[:, 3 * HEAD_DIM:4 * HEAD_DIM])
                     + w_win * rep4(winn[:, HEAD_DIM:2 * HEAD_DIM]))
    last = lax.broadcasted_iota(I32, (HEAD_DIM, WB), 1) == WB - 1
    winn_cols = _cols_of(winn)
    for s in range(2):
        new_col = winn_cols[s * HEAD_DIM:(s + 1) * HEAD_DIM, 0:1]
        nw_ref[s] = jnp.where(last, new_col, pltpu.roll(win_ref[s], WB - 1, 1))


def _nsa_decode(l, pt_flat, n_pages, past_len, nq, nqr, rows, win, small, nsa_cache_t, cmp_pool, win_state_t):
    DB = nq.shape[0]
    PG = nsa_cache_t.shape[-1]
    WB = win_state_t.shape[-1]
    page = lambda p: (lambda b, pt: (l, pt[b * n_pages + p], 0, 0, 0))
    cpage = lambda p: (lambda b, pt: (pt[b * n_pages + p], 0, 0))
    rows8 = lambda w: pl.BlockSpec((SUBLANES, w), lambda b, pt: (b // SUBLANES, 0))
    return pl.pallas_call(
        functools.partial(_nsa_decode_kernel, n_pages=n_pages, past_len=past_len),
        grid_spec=pltpu.PrefetchScalarGridSpec(
            num_scalar_prefetch=1,
            grid=(DB,),
            in_specs=[rows8(D_NSA), rows8(D_NSA), rows8(4 * HEAD_DIM), rows8(2 * HEAD_DIM), rows8(LANES),
                      pl.BlockSpec((None, None, 2, HEAD_DIM, WB), lambda b, pt: (l, b, 0, 0, 0))]
            + [pl.BlockSpec((None, None, 4, HEAD_DIM, PG), page(p)) for p in range(n_pages)]
            + [pl.BlockSpec((None, PG // CMP_LEN, LANES), cpage(p)) for p in range(n_pages)],
            out_specs=[rows8(D_NSA),
                       pl.BlockSpec((None, 2, HEAD_DIM, WB), lambda b, pt: (b, 0, 0, 0))],
            scratch_shapes=[pltpu.VMEM((n_pages * SUBLANES, LANES), F32),
                            pltpu.VMEM((SUBLANES, LANES), F32),
                            pltpu.VMEM((n_pages * N_HEADS * SUBLANES, PG), F32),
                            pltpu.VMEM((SUBLANES, WB), F32)],
        ),
        out_shape=[jax.ShapeDtypeStruct((DB, D_NSA), F32),
                   jax.ShapeDtypeStruct((DB, 2, HEAD_DIM, WB), F32)],
        compiler_params=_params(("arbitrary",)),
        name="nsa_decode",
    )(pt_flat, nq, nqr, rows, win, small, win_state_t,
      *([nsa_cache_t] * n_pages), *([cmp_pool] * n_pages))


def _gla_decode_kernel(q_ref, k_ref, g_ref, v_ref, s_ref, o_ref, so_ref):
    s_new = jnp.exp(g_ref[...]) * s_ref[...] + k_ref[...] * v_ref[...]
    so_ref[...] = s_new
    o_ref[...] = jnp.sum((q_ref[...] * SCALE) * s_new, axis=2, keepdims=True)


def _gla_decode(l, gqk, gv, glog, state, nb):
    DB = gqk.shape[0]
    col = lambda a: a.reshape(DB, N_HEADS, HEAD_DIM, 1)
    cspec = pl.BlockSpec((nb, N_HEADS, HEAD_DIM, 1), lambda i: (i, 0, 0, 0))
    vspec = pl.BlockSpec((nb, N_HEADS, 1, GLA_DV), lambda i: (i, 0, 0, 0))
    sspec = pl.BlockSpec((nb, N_HEADS, HEAD_DIM, GLA_DV), lambda i: (i, 0, 0, 0))
    o, s_new = pl.pallas_call(
        _gla_decode_kernel,
        grid=(DB // nb,),
        in_specs=[cspec, cspec, cspec, vspec,
                  pl.BlockSpec((None, nb, N_HEADS, HEAD_DIM, GLA_DV), lambda i: (l, i, 0, 0, 0))],
        out_specs=[vspec, sspec],
        out_shape=[jax.ShapeDtypeStruct((DB, N_HEADS, 1, GLA_DV), F32),
                   jax.ShapeDtypeStruct((DB, N_HEADS, HEAD_DIM, GLA_DV), F32)],
        compiler_params=_params(("parallel",)),
        name="gla_decode",
    )(col(gqk[:, 0:D_FOX]), col(gqk[:, D_FOX:2 * D_FOX]), col(glog),
      gv.reshape(DB, N_HEADS, 1, GLA_DV), state)
    return o.reshape(DB, D_GLA), s_new


def _mix_sample(x, l, p, tab, pt_flat, n_pages, past_len, fox_kv_t, fox_lft_c, nsa_t, cmp_blocks,
                win_state_t, gla_state):
    DB = x.shape[0]
    (fq, fkv, nq, nqr, rows, win, gqk, gv, gog, glog, small) = _in_projection(
        x, 0, DB, DB, p['g'], p['w_r32'], p['sb'], p['wgk32'], p['bgk'], tab, DB)
    o_fox = _fox_decode(l, pt_flat, n_pages, fq, fkv, small, fox_kv_t, fox_lft_c)
    n_pool, PG = nsa_t.shape[1], nsa_t.shape[-1]
    per_layer = n_pool * (PG // CMP_LEN)
    cmp_pool = _compress(cmp_blocks, p['pe_cmp'], p['w_cmp'], 256, False, row0=l * per_layer, n_rows=per_layer)
    cmp_pool = cmp_pool.reshape(n_pool, PG // CMP_LEN, LANES)
    o_nsa, new_win = _nsa_decode(l, pt_flat, n_pages, past_len, nq, nqr, rows, win, small,
                                 nsa_t, cmp_pool, win_state_t)
    o_gla, g_state = _gla_decode(l, gqk, gv, glog, gla_state, 8)
    x_new = _out_projection(x, o_fox, o_nsa, o_gla, gog, p['gn'], p['w_out32'], DB)
    return x_new, dict(fkv=fkv, small=small, rows=rows, win=new_win, g_state=g_state)


def kernel(x_prompt, x_sample, cache_fox_kv, cache_fox_logf, cache_nsa_kv, state_nsa_win, state_gla,
           page_table, norm_mix_g, w_in, b_fox_f, w_cmp, pe_cmp, w_gla_gk, b_gla_gk, g_gla_norm, w_out,
           norm_ffn_g, dense_w_gate, dense_w_up, dense_w_down, moe_w_router, moe_w_gate, moe_w_up,
           moe_w_down, final_norm_g):
    B, S, _ = x_prompt.shape
    DB, TN, _ = x_sample.shape
    assert TN == 1
    depth, n_pool, PG = cache_fox_kv.shape[0:3]
    n_pages = page_table.shape[1]
    past_len = n_pages * PG
    WB = state_nsa_win.shape[2]
    xp = x_prompt.reshape(B * S, D_MODEL)
    xs = x_sample.reshape(DB, D_MODEL)
    tab_p = _rope_table(jnp.arange(S))
    tab_s = _rope_table(jnp.full((DB,), past_len, I32))
    pt_flat = page_table.reshape(-1).astype(I32)
    fox_kv_t = jnp.transpose(cache_fox_kv, (0, 1, 3, 4, 5, 2))
    fox_lft_c = jnp.swapaxes(cache_fox_logf, 2, 3)
    nsa_t = jnp.transpose(cache_nsa_kv, (0, 1, 3, 4, 2))
    win_state_t = jnp.transpose(state_nsa_win, (0, 1, 3, 4, 2))
    n_cmp = depth * n_pool * (PG // CMP_LEN)
    cmp_blocks = _block_major(cache_nsa_kv[:, :, :, 0:2, :].reshape(n_cmp * CMP_LEN, 2, HEAD_DIM), n_cmp)
    cp, cs = [], []
    for l in range(depth):
        p = _layer_mix_params(l, norm_mix_g, w_in, b_fox_f, w_cmp, pe_cmp, w_gla_gk, b_gla_gk,
                              g_gla_norm, w_out)
        xp, c = _mix_prompt(xp, B, S, p, tab_p, 512, 512, 512, 256)
        cp.append(c)
        xs, c = _mix_sample(xs, l, p, tab_s, pt_flat, n_pages, past_len, fox_kv_t, fox_lft_c, nsa_t,
                            cmp_blocks, win_state_t, state_gla)
        cs.append(c)
        gf = norm_ffn_g[l].reshape(1, D_MODEL)
        i = l // 2
        if l % 2 == 0:
            wg, wu, wd = (dense_w_gate[i].astype(BF16), dense_w_up[i].astype(BF16),
                          dense_w_down[i].astype(BF16))
            xp = _dense_ffn(xp, gf, wg, wu, wd, 1024, 512)
            xs = _dense_ffn(xs, gf, dense_w_gate[i], dense_w_up[i], dense_w_down[i], DB, 512)
        else:
            wr = jnp.zeros((D_MODEL, LANES), F32).at[:, 0:N_EXPERTS].set(moe_w_router[i])
            xp, xs = _moe_layer([xp, xs], gf, wr, moe_w_gate[i], moe_w_up[i], moe_w_down[i],
                                [512, DB], 1024, 512)
    gfin = final_norm_g.reshape(1, D_MODEL)
    y_p = _final_norm(xp, gfin, 512).reshape(B, S, D_MODEL)
    y_s = _final_norm(xs, gfin, DB).reshape(DB, 1, D_MODEL)
    wp = min(WINDOW, S)
    st = lambda key, group: jnp.stack([c[key] for c in group])
    return (y_p, y_s,
            st('fkv', cp).reshape(depth, B, S, 2, N_HEADS, HEAD_DIM),
            st('small', cp)[:, :, SM_FF:SM_FF + N_HEADS].reshape(depth, B, S, N_HEADS),
            st('rows', cp).reshape(depth, B, S, 4, HEAD_DIM),
            st('win', cp).reshape(depth, B, S, 2, HEAD_DIM)[:, :, S - wp:],
            st('g_state', cp),
            st('fkv', cs).reshape(depth, DB, 1, 2, N_HEADS, HEAD_DIM),
            st('small', cs)[:, :, SM_FF:SM_FF + N_HEADS].reshape(depth, DB, 1, N_HEADS),
            st('rows', cs).reshape(depth, DB, 1, 4, HEAD_DIM),
            jnp.transpose(st('win', cs), (0, 1, 4, 2, 3)),
            st('g_state', cs))
```

```python
import functools

import jax
import jax.numpy as jnp
from jax import lax
from jax.experimental import pallas as pl
from jax.experimental.pallas import tpu as pltpu
from jax.experimental.pallas import tpu_sc as plsc

F32 = jnp.float32
BF16 = jnp.bfloat16
I32 = jnp.int32
HI = lax.Precision.HIGHEST

D_MODEL = 1024
HEAD_DIM = 64
N_HEADS = 4
D_FOX = N_HEADS * HEAD_DIM
D_NSA = N_HEADS * HEAD_DIM
GLA_DV = 128
D_GLA = N_HEADS * GLA_DV
GLA_RANK = 16
GLA_TAU = 16.0
GLA_CHUNK = 64
CMP_LEN = 32
SEL_LEN = 64
TOP_N = 16
WINDOW = 512
ROPE_THETA = 500000.0
ROPE_DIM = HEAD_DIM // 4
ROPE_HALF = ROPE_DIM // 2
D_FF = 3584
N_EXPERTS = 8
EPS = 1e-6
SEL_FORCE = 1e9
NEG = -1e30
SCALE = HEAD_DIM ** -0.5
LOG2E = 1.4426950408889634

LANES = 128
SUBLANES = 8
VMEM_LIMIT = 56 * 1024 * 1024

C_FQ = 0
C_FKV = 256
C_NQ = 768
C_NKV = 1024
C_GQK = 1408
C_GV = 1920
C_GOG = 2432
C_SMALL = 2944
C_END = 3072
SM_FF = 0
SM_NG = 4
SM_GLR = 16

NT = (((1,), (1,)), ((), ()))


def _params(sem):
    return pltpu.CompilerParams(dimension_semantics=sem, vmem_limit_bytes=VMEM_LIMIT)


def _rms(x, g):
    ms = jnp.mean(x * x, axis=-1, keepdims=True)
    return x * lax.rsqrt(ms + EPS) * g


def _sigmoid(x):
    return 1.0 / (1.0 + jnp.exp(-x))


def _log_sigmoid(x):
    return -(jnp.maximum(-x, 0.0) + jnp.log1p(jnp.exp(-jnp.abs(x))))


def _silu(x):
    return x * _sigmoid(x)


def _bdot(a, b):
    return jnp.dot(a.astype(BF16), b.astype(BF16), preferred_element_type=F32)


def _bdot_nt(a, b):
    return lax.dot_general(a.astype(BF16), b.astype(BF16), NT, preferred_element_type=F32)


def _wdot(a, w):
    if w.dtype == F32:
        return jnp.dot(a.astype(F32), w, precision=HI, preferred_element_type=F32)
    return jnp.dot(a.astype(BF16), w, preferred_element_type=F32)


def _rope128(x, a, bp, bm):
    return x * a + pltpu.roll(x, ROPE_HALF, 1) * bp + pltpu.roll(x, LANES - ROPE_HALF, 1) * bm


def _inproj_kernel(x_ref, g_ref, w_ref, sb_ref, wgk_ref, bgk_ref, tab_ref,
                   fq_ref, fkv_ref, nq_ref, nqr_ref, rows_ref, win_ref,
                   gqk_ref, gv_ref, gog_ref, glog_ref, small_ref):
    h = _rms(x_ref[...], g_ref[...]).astype(w_ref.dtype)

    def mm(a, b):
        return _wdot(h, w_ref[:, a:b])

    fq_ref[...] = mm(C_FQ, C_FKV)
    fkv_ref[...] = mm(C_FKV, C_NQ)
    tab = tab_ref[...]
    ab, pb, mb = tab[:, 0:128], tab[:, 128:256], tab[:, 256:384]
    af, pf, mf = tab[:, 384:512], tab[:, 512:640], tab[:, 640:768]
    nq = mm(C_NQ, C_NKV)
    nq_ref[...] = nq
    nqr_ref[:, 0:128] = _rope128(nq[:, 0:128], ab, pb, mb)
    nqr_ref[:, 128:256] = _rope128(nq[:, 128:256], ab, pb, mb)
    nkv = mm(C_NKV, C_GQK)
    rows_ref[:, 0:128] = nkv[:, 0:128]
    rows_ref[:, 128:256] = _rope128(nkv[:, 128:256], af, pf, mf)
    win_ref[...] = _rope128(nkv[:, 256:384], af, pf, mf)
    gqk_ref[...] = mm(C_GQK, C_GV)
    gv_ref[...] = mm(C_GV, C_GOG)
    gog_ref[...] = mm(C_GOG, C_SMALL)
    sm = mm(C_SMALL, C_END)
    glog_ref[...] = _log_sigmoid(_wdot(sm, wgk_ref[...]) + bgk_ref[...]) * (1.0 / GLA_TAU)
    smb = sm + sb_ref[...]
    lane = lax.broadcasted_iota(I32, smb.shape, 1)
    small_ref[...] = jnp.where(lane < SM_NG, _log_sigmoid(smb), _sigmoid(smb))


def _in_projection(x_all, row0, n_rows, tm, g, w_r, sb, wgk, bgk, tab, tab_period):
    assert n_rows % tm == 0 and row0 % tm == 0 and tab_period % tm == 0
    nt = n_rows // tm
    b0 = row0 // tm
    npd = tab_period // tm
    widths = (256, 512, 256, 256, 256, 128, 512, 512, 512, 256, 128)
    full = lambda shape: pl.BlockSpec(shape, lambda i: (0, 0))
    return pl.pallas_call(
        _inproj_kernel,
        grid=(nt,),
        in_specs=[
            pl.BlockSpec((tm, D_MODEL), lambda i: (b0 + i, 0)),
            full((1, D_MODEL)),
            full((D_MODEL, C_END)),
            full((1, LANES)),
            full((LANES, 256)),
            full((1, 256)),
            pl.BlockSpec((tm, 768), lambda i: (i % npd, 0)),
        ],
        out_specs=[pl.BlockSpec((tm, w), lambda i: (i, 0)) for w in widths],
        out_shape=[jax.ShapeDtypeStruct((n_rows, w), F32) for w in widths],
        compiler_params=_params(("parallel",)),
        name="in_projection",
    )(x_all, g, w_r, sb, wgk, bgk, tab)


def _cumsum_kernel(sm_ref, cr_ref, carry):
    t = pl.program_id(1)
    ts = sm_ref.shape[0]

    @pl.when(t == 0)
    def _():
        carry[...] = jnp.zeros_like(carry)

    r = lax.broadcasted_iota(I32, (ts, ts), 0)
    c = lax.broadcasted_iota(I32, (ts, ts), 1)
    tri = (c <= r).astype(F32)
    cs = jnp.dot(tri, sm_ref[...], precision=HI, preferred_element_type=F32) + carry[...]
    carry[...] = cs[ts - 1:ts, :]
    cr_ref[...] = cs.T[0:SUBLANES, :] * LOG2E


def _fox_cumsum(small, B, S, ts):
    ns = S // ts
    return pl.pallas_call(
        _cumsum_kernel,
        grid=(B, ns),
        in_specs=[pl.BlockSpec((ts, LANES), lambda b, t: (b * ns + t, 0))],
        out_specs=pl.BlockSpec((None, SUBLANES, ts), lambda b, t: (b, 0, t)),
        out_shape=jax.ShapeDtypeStruct((B, SUBLANES, S), F32),
        scratch_shapes=[pltpu.VMEM((1, LANES), F32)],
        compiler_params=_params(("parallel", "arbitrary")),
        name="fox_cumsum",
    )(small)


def _pair_mask(shape, h):
    return (lax.broadcasted_iota(I32, shape, 1) // HEAD_DIM) == (h % 2)


def _fox_prompt_kernel(q_ref, kv_ref, cr_ref, o_ref, *scratch):
    i = pl.program_id(1)
    j = pl.program_id(2)
    nk = pl.num_programs(2)
    tq = q_ref.shape[0]
    tk = kv_ref.shape[0]
    q_sc, m_sc, acc_sc = scratch[0:N_HEADS], scratch[N_HEADS:2 * N_HEADS], scratch[2 * N_HEADS:]

    @pl.when(j == 0)
    def _():
        for h in range(N_HEADS):
            m_sc[h][...] = jnp.full_like(m_sc[h], NEG)
            acc_sc[h][...] = jnp.zeros_like(acc_sc[h])
            slab = q_ref[:, (h // 2) * LANES:(h // 2 + 1) * LANES] * (SCALE * LOG2E)
            q_sc[h][...] = jnp.where(_pair_mask(slab.shape, h), slab, 0.0).astype(BF16)

    def tile(diagonal):
        if diagonal:
            mask = lax.broadcasted_iota(I32, (1, tk), 1) <= lax.broadcasted_iota(I32, (tq, 1), 0)
        for h in range(N_HEADS):
            c0 = (h // 2) * LANES
            k_slab = kv_ref[:, c0:c0 + LANES].astype(BF16)
            s = lax.dot_general(q_sc[h][...], k_slab, NT, preferred_element_type=F32) - cr_ref[h:h + 1, :]
            if diagonal:
                s = jnp.where(mask, s, NEG)
            m_old = m_sc[h][...]
            m_new = jnp.maximum(m_old, jnp.max(s, axis=-1, keepdims=True))
            p = jnp.exp2(s - m_new).astype(BF16)
            v_slab = kv_ref[:, D_FOX + c0:D_FOX + c0 + LANES]
            v_aug = jnp.where(_pair_mask(v_slab.shape, h), v_slab, 1.0).astype(BF16)
            acc_sc[h][...] = (jnp.exp2(m_old - m_new) * acc_sc[h][...]
                              + jnp.dot(p, v_aug, preferred_element_type=F32))
            m_sc[h][...] = m_new

    @pl.when(j < i)
    def _():
        tile(False)

    @pl.when(j == i)
    def _():
        tile(True)

    @pl.when(j == nk - 1)
    def _():
        for h in range(N_HEADS):
            a = acc_sc[h][...]
            lo = (h % 2) * HEAD_DIM
            den = a[:, HEAD_DIM - lo:HEAD_DIM - lo + 1]
            o_ref[:, h * HEAD_DIM:(h + 1) * HEAD_DIM] = a[:, lo:lo + HEAD_DIM] / den


def _fox_prompt(fq, fkv, cr, B, S, t):
    n = S // t
    return pl.pallas_call(
        _fox_prompt_kernel,
        grid=(B, n, n),
        in_specs=[
            pl.BlockSpec((t, D_FOX), lambda b, i, j: (b * n + i, 0)),
            pl.BlockSpec((t, 2 * D_FOX), lambda b, i, j: (b * n + jnp.minimum(i, j), 0)),
            pl.BlockSpec((None, SUBLANES, t), lambda b, i, j: (b, 0, jnp.minimum(i, j))),
        ],
        out_specs=pl.BlockSpec((t, D_FOX), lambda b, i, j: (b * n + i, 0)),
        out_shape=jax.ShapeDtypeStruct((B * S, D_FOX), F32),
        scratch_shapes=([pltpu.VMEM((t, LANES), BF16)] * N_HEADS + [pltpu.VMEM((t, 1), F32)] * N_HEADS
                        + [pltpu.VMEM((t, LANES), F32)] * N_HEADS),
        compiler_params=_params(("parallel", "parallel", "arbitrary")),
        name="fox_prompt",
    )(fq, fkv, cr)


def _compress_kernel(x_ref, pe_ref, w_ref, o_ref, *, exact):
    for s in range(2):
        x = x_ref[s] + pe_ref[s]
        w = w_ref[s]
        if exact:
            y = jnp.dot(x, w, precision=HI, preferred_element_type=F32)
        else:
            xh = x.astype(BF16)
            xl = (x - xh.astype(F32)).astype(BF16)
            wh = w.astype(BF16)
            wl = (w - wh.astype(F32)).astype(BF16)
            y = (jnp.dot(xh, wh, preferred_element_type=F32) + jnp.dot(xl, wh, preferred_element_type=F32)
                 + jnp.dot(xh, wl, preferred_element_type=F32))
        o_ref[:, s * HEAD_DIM:(s + 1) * HEAD_DIM] = y


def _compress(x3, pe, w, tr, exact, row0=0, n_rows=None):
    K = x3.shape[2]
    R = x3.shape[1] if n_rows is None else n_rows
    assert R % tr == 0 and row0 % tr == 0
    b0 = row0 // tr
    return pl.pallas_call(
        functools.partial(_compress_kernel, exact=exact),
        grid=(R // tr,),
        in_specs=[pl.BlockSpec((2, tr, K), lambda i: (0, b0 + i, 0)),
                  pl.BlockSpec((2, 1, K), lambda i: (0, 0, 0)),
                  pl.BlockSpec((2, K, HEAD_DIM), lambda i: (0, 0, 0))],
        out_specs=pl.BlockSpec((tr, LANES), lambda i: (i, 0)),
        out_shape=jax.ShapeDtypeStruct((R, LANES), F32),
        compiler_params=_params(("parallel",)),
        name="nsa_compress",
    )(x3, pe, w)


def _block_major(kv, n_blocks):
    return kv.reshape(n_blocks, CMP_LEN, 2, HEAD_DIM).transpose(2, 0, 1, 3).reshape(
        2, n_blocks, CMP_LEN * HEAD_DIM)


def _order_key(x):
    b = lax.bitcast_convert_type(x, I32)
    return jnp.where(b < 0, b ^ jnp.int32(0x7FFFFFFF), b)


def _nsa_prompt_kernel(nq_ref, nqr_ref, sm_ref, cmp_ref, rows_ref, win_ref, o_ref, qx_sc, *, tk):
    QB = nq_ref.shape[0]
    S = rows_ref.shape[0]
    nb = cmp_ref.shape[0]
    nsel = S // SEL_LEN
    i = pl.program_id(1)
    qs = i * QB
    qpos = qs + lax.broadcasted_iota(I32, (QB, 1), 0)

    cmp = cmp_ref[...]
    kc = cmp[:, 0:HEAD_DIM]
    vc = cmp[:, HEAD_DIM:2 * HEAD_DIM]
    n_l = lax.broadcasted_iota(I32, (1, nb), 1)
    complete = ((n_l + 1) * CMP_LEN - 1) <= qpos
    complete_f = complete.astype(F32)
    psum = jnp.zeros((QB, nb), F32)
    o_cmp = []
    for h in range(N_HEADS):
        qh = nq_ref[:, h * HEAD_DIM:(h + 1) * HEAD_DIM]
        s = lax.dot_general(qh, kc, NT, precision=HI, preferred_element_type=F32) * SCALE
        s = jnp.where(complete, s, NEG)
        e = jnp.exp(s - jnp.max(s, axis=-1, keepdims=True))
        p = e / jnp.sum(e, axis=-1, keepdims=True) * complete_f
        o_cmp.append(_bdot(p, cmp))
        psum = psum + p

    pj = lax.broadcasted_iota(I32, (nsel, nb), 0)
    pn = lax.broadcasted_iota(I32, (nsel, nb), 1)
    pair_t = (pn // (SEL_LEN // CMP_LEN) == pj).astype(F32)
    imp_t = lax.dot_general(pair_t, psum, NT, precision=HI, preferred_element_type=F32)
    jt = (qs + lax.broadcasted_iota(I32, (1, QB), 1)) // SEL_LEN
    jj = lax.broadcasted_iota(I32, (nsel, 1), 0)
    score = jnp.where(jj == jt, 2.0 * SEL_FORCE,
                      jnp.where((jj == 0) | (jj == jt - 1), SEL_FORCE,
                                jnp.where(jj <= jt, imp_t + 0.0, -1.0)))
    key = _order_key(score)
    key_m1 = key - 1
    ngrp = nsel // SUBLANES
    sub = lax.broadcasted_iota(I32, (SUBLANES, QB), 0)
    kg = [key[r * SUBLANES:(r + 1) * SUBLANES, :] for r in range(ngrp)]
    kg1 = [key_m1[r * SUBLANES:(r + 1) * SUBLANES, :] for r in range(ngrp)]
    cnt = [jnp.zeros((SUBLANES, QB), I32) for _ in range(ngrp)]
    for jp in range(nsel):
        g = jp // SUBLANES
        row = key[jp:jp + 1, :]
        mixed = jnp.where(sub > (jp % SUBLANES), kg1[g], kg[g])
        for r in range(ngrp):
            thr = kg[r] if r < g else (kg1[r] if r > g else mixed)
            cnt[r] = cnt[r] + (row > thr).astype(I32)
    sel_t = jnp.concatenate([(c < TOP_N).astype(F32) for c in cnt], axis=0)
    if nsel < QB:
        sel_t = jnp.concatenate([sel_t, jnp.zeros((QB - nsel, QB), F32)], axis=0)
    sel = sel_t.T.astype(BF16)

    lo_half = lax.broadcasted_iota(I32, (QB, LANES), 1) < HEAD_DIM
    for h in range(N_HEADS):
        slab = nqr_ref[:, (h // 2) * LANES:(h // 2 + 1) * LANES] * (SCALE * LOG2E)
        if h % 2:
            slab = pltpu.roll(slab, HEAD_DIM, 1)
        qx_sc[h * QB:(h + 1) * QB, :] = jnp.where(lo_half, slab, 0.0).astype(BF16)
    qx = qx_sc[...]
    HQ = N_HEADS * QB

    def attend(s, valid, slab, m_old, acc_old):
        n = s.shape[1]
        s = jnp.where(valid[None], s.reshape(N_HEADS, QB, n), NEG).reshape(HQ, n)
        m_new = jnp.maximum(m_old, jnp.max(s, axis=-1, keepdims=True))
        p = jnp.exp2(s - m_new).astype(BF16)
        ones_k = lax.broadcasted_iota(I32, slab.shape, 1) < HEAD_DIM
        v_aug = jnp.where(ones_k, 1.0, slab).astype(BF16)
        acc = jnp.exp2(m_old - m_new) * acc_old + jnp.dot(p, v_aug, preferred_element_type=F32)
        return m_new, acc

    jrow = lax.broadcasted_iota(I32, (QB, 1), 0)

    def sel_tile(k0, m_old, acc_old, diagonal):
        kpos = k0 + lax.broadcasted_iota(I32, (1, tk), 1)
        expand = (jrow == kpos // SEL_LEN).astype(BF16)
        valid = jnp.dot(sel, expand, preferred_element_type=F32) > 0.5
        if diagonal:
            valid = valid & (kpos <= qpos)
        slab = rows_ref[pl.ds(k0, tk), 2 * HEAD_DIM:4 * HEAD_DIM]
        s = lax.dot_general(qx, slab.astype(BF16), NT, preferred_element_type=F32)
        return attend(s, valid, slab, m_old, acc_old)

    n_full = qs // tk
    init = (jnp.full((HQ, 1), NEG, F32), jnp.zeros((HQ, LANES), F32))
    m_s, acc_s = lax.fori_loop(
        0, n_full, lambda t, c: sel_tile(pl.multiple_of(t * tk, tk), c[0], c[1], False), init)
    _, acc_s = sel_tile(pl.multiple_of(n_full * tk, tk), m_s, acc_s, True)

    wlen = WINDOW + QB
    w0 = pl.multiple_of(jnp.maximum(qs - WINDOW, 0), QB)
    wpos = w0 + lax.broadcasted_iota(I32, (1, wlen), 1)
    d = qpos - wpos
    wslab = win_ref[pl.ds(w0, wlen), :]
    sw = lax.dot_general(qx, wslab.astype(BF16), NT, preferred_element_type=F32)
    _, acc_w = attend(sw, (d >= 0) & (d < WINDOW), wslab,
                      jnp.full((HQ, 1), NEG, F32), jnp.zeros((HQ, LANES), F32))

    sm = sm_ref[...]
    for h in range(N_HEADS):
        rs = slice(h * QB, (h + 1) * QB)
        o_sel = acc_s[rs] * (1.0 / acc_s[rs, 0:1])
        o_win = acc_w[rs] * (1.0 / acc_w[rs, 0:1])
        c = SM_NG + 3 * h
        mix = sm[:, c:c + 1] * o_cmp[h] + sm[:, c + 1:c + 2] * o_sel + sm[:, c + 2:c + 3] * o_win
        if h % 2 == 0:
            mix = pltpu.roll(mix, HEAD_DIM, 1)
        lo = (h % 2) * HEAD_DIM
        o_ref[:, h * HEAD_DIM:(h + 1) * HEAD_DIM] = mix[:, lo:lo + HEAD_DIM]


def _nsa_prompt(nq, nqr, small, cmp, rows, win, B, S, tk):
    QB = 128
    nq_t = S // QB
    nb = S // CMP_LEN
    assert S % tk == 0 and S >= WINDOW + QB
    return pl.pallas_call(
        functools.partial(_nsa_prompt_kernel, tk=tk),
        grid=(B, nq_t),
        in_specs=[
            pl.BlockSpec((QB, D_NSA), lambda b, i: (b * nq_t + i, 0)),
            pl.BlockSpec((QB, D_NSA), lambda b, i: (b * nq_t + i, 0)),
            pl.BlockSpec((QB, LANES), lambda b, i: (b * nq_t + i, 0)),
            pl.BlockSpec((nb, LANES), lambda b, i: (b, 0)),
            pl.BlockSpec((S, 4 * HEAD_DIM), lambda b, i: (b, 0)),
            pl.BlockSpec((S, 2 * HEAD_DIM), lambda b, i: (b, 0)),
        ],
        out_specs=pl.BlockSpec((QB, D_NSA), lambda b, i: (b * nq_t + i, 0)),
        out_shape=jax.ShapeDtypeStruct((B * S, D_NSA), F32),
        scratch_shapes=[pltpu.VMEM((N_HEADS * QB, LANES), BF16)],
        compiler_params=_params(("parallel", "parallel")),
        name="nsa_prompt",
    )(nq, nqr, small, cmp, rows, win)


def _gla_prompt_kernel(qk_ref, v_ref, g_ref, o_ref, st_ref, s_sc):
    t = pl.program_id(1)
    nt = pl.num_programs(1)
    tc = qk_ref.shape[0]
    C = GLA_CHUNK

    @pl.when(t == 0)
    def _():
        s_sc[...] = jnp.zeros_like(s_sc)

    r = lax.broadcasted_iota(I32, (tc, tc), 0)
    c = lax.broadcasted_iota(I32, (tc, tc), 1)
    same = (r // C) == (c // C)
    causal = same & (c <= r)
    g = g_ref[...]
    gcum = jnp.dot(causal.astype(F32), g, precision=HI, preferred_element_type=F32)
    g_t = g.T
    gcum_t = jnp.dot(g_t, (same & (r <= c)).astype(F32), precision=HI, preferred_element_type=F32)
    gtot_t = jnp.dot(g_t, same.astype(F32), precision=HI, preferred_element_type=F32)
    q_e = (qk_ref[:, 0:D_FOX] * SCALE * jnp.exp(gcum)).astype(BF16)
    k_e = (qk_ref[:, D_FOX:2 * D_FOX] * jnp.exp(-gcum)).astype(BF16)
    kd_t = (qk_ref[:, D_FOX:2 * D_FOX].T * jnp.exp(gtot_t - gcum_t)).astype(BF16)
    decay_t = jnp.exp(gtot_t)
    for h in range(N_HEADS):
        hs = slice(h * HEAD_DIM, (h + 1) * HEAD_DIM)
        v = v_ref[:, h * GLA_DV:(h + 1) * GLA_DV].astype(BF16)
        a = jnp.where(causal, lax.dot_general(q_e[:, hs], k_e[:, hs], NT, preferred_element_type=F32), 0.0)
        o_intra = jnp.dot(a.astype(BF16), v, preferred_element_type=F32)
        state = s_sc[h]
        for ci in range(tc // C):
            rs = slice(ci * C, (ci + 1) * C)
            o_ref[rs, h * GLA_DV:(h + 1) * GLA_DV] = (
                o_intra[rs] + jnp.dot(q_e[rs, hs], state.astype(BF16), preferred_element_type=F32))
            state = (decay_t[hs, ci * C:ci * C + 1] * state
                     + jnp.dot(kd_t[hs, rs], v[rs], preferred_element_type=F32))
        s_sc[h] = state

    @pl.when(t == nt - 1)
    def _():
        st_ref[...] = s_sc[...]


def _gla_prompt(gqk, gv, glog, B, S, tc):
    nt = S // tc
    return pl.pallas_call(
        _gla_prompt_kernel,
        grid=(B, nt),
        in_specs=[pl.BlockSpec((tc, 2 * D_FOX), lambda b, t: (b * nt + t, 0)),
                  pl.BlockSpec((tc, D_GLA), lambda b, t: (b * nt + t, 0)),
                  pl.BlockSpec((tc, D_FOX), lambda b, t: (b * nt + t, 0))],
        out_specs=[pl.BlockSpec((tc, D_GLA), lambda b, t: (b * nt + t, 0)),
                   pl.BlockSpec((None, N_HEADS, HEAD_DIM, GLA_DV), lambda b, t: (b, 0, 0, 0))],
        out_shape=[jax.ShapeDtypeStruct((B * S, D_GLA), F32),
                   jax.ShapeDtypeStruct((B, N_HEADS, HEAD_DIM, GLA_DV), F32)],
        scratch_shapes=[pltpu.VMEM((N_HEADS, HEAD_DIM, GLA_DV), F32)],
        compiler_params=_params(("parallel", "arbitrary")),
        name="gla_prompt",
    )(gqk, gv, glog)


def _outproj_kernel(x_ref, of_ref, on_ref, og_ref, gog_ref, gn_ref, w_ref, o_ref):
    acc = _wdot(of_ref[...], w_ref[0:D_FOX, :])
    acc = acc + _wdot(on_ref[...], w_ref[D_FOX:D_FOX + D_NSA, :])
    for h in range(N_HEADS):
        hs = slice(h * GLA_DV, (h + 1) * GLA_DV)
        z = _rms(og_ref[:, hs], gn_ref[...]) * _silu(gog_ref[:, hs])
        w0 = D_FOX + D_NSA + h * GLA_DV
        acc = acc + _wdot(z, w_ref[w0:w0 + GLA_DV, :])
    o_ref[...] = x_ref[...] + acc


def _out_projection(x, o_fox, o_nsa, o_gla, gog, gn, w_out, tm):
    T = x.shape[0]
    assert T % tm == 0
    row = lambda w: pl.BlockSpec((tm, w), lambda i: (i, 0))
    return pl.pallas_call(
        _outproj_kernel,
        grid=(T // tm,),
        in_specs=[row(D_MODEL), row(D_FOX), row(D_NSA), row(D_GLA), row(D_GLA),
                  pl.BlockSpec((1, GLA_DV), lambda i: (0, 0)),
                  pl.BlockSpec((D_MODEL, D_MODEL), lambda i: (0, 0))],
        out_specs=row(D_MODEL),
        out_shape=jax.ShapeDtypeStruct((T, D_MODEL), F32),
        compiler_params=_params(("parallel",)),
        name="out_projection",
    )(x, o_fox, o_nsa, o_gla, gog, gn, w_out)


def _dense_ffn_kernel(x_ref, g_ref, wg_ref, wu_ref, wd_ref, o_ref, h_sc, acc_sc):
    f = pl.program_id(1)
    nf = pl.num_programs(1)

    @pl.when(f == 0)
    def _():
        h_sc[...] = _rms(x_ref[...], g_ref[...]).astype(h_sc.dtype)
        acc_sc[...] = jnp.zeros_like(acc_sc)

    h = h_sc[...]
    a = _wdot(h, wg_ref[...])
    u = _wdot(h, wu_ref[...])
    acc_sc[...] += _wdot(_silu(a) * u, wd_ref[...])

    @pl.when(f == nf - 1)
    def _():
        o_ref[...] = x_ref[...] + acc_sc[...]


def _dense_ffn(x, g, wg, wu, wd, tm, tf):
    T = x.shape[0]
    assert T % tm == 0 and D_FF % tf == 0
    return pl.pallas_call(
        _dense_ffn_kernel,
        grid=(T // tm, D_FF // tf),
        in_specs=[pl.BlockSpec((tm, D_MODEL), lambda i, f: (i, 0)),
                  pl.BlockSpec((1, D_MODEL), lambda i, f: (0, 0)),
                  pl.BlockSpec((D_MODEL, tf), lambda i, f: (0, f)),
                  pl.BlockSpec((D_MODEL, tf), lambda i, f: (0, f)),
                  pl.BlockSpec((tf, D_MODEL), lambda i, f: (f, 0))],
        out_specs=pl.BlockSpec((tm, D_MODEL), lambda i, f: (i, 0)),
        out_shape=jax.ShapeDtypeStruct((T, D_MODEL), F32),
        scratch_shapes=[pltpu.VMEM((tm, D_MODEL), wg.dtype), pltpu.VMEM((tm, D_MODEL), F32)],
        compiler_params=_params(("parallel", "arbitrary")),
        name="dense_ffn",
    )(x, g, wg, wu, wd)


def _router_kernel(x_ref, g_ref, wr_ref, h_ref, r_ref):
    h = _rms(x_ref[...], g_ref[...])
    h_ref[...] = h
    logits = jnp.dot(h, wr_ref[...], precision=HI, preferred_element_type=F32)
    lane = lax.broadcasted_iota(I32, logits.shape, 1)
    lg = jnp.where(lane < N_EXPERTS, logits, -jnp.inf)
    m1 = jnp.max(lg, axis=-1, keepdims=True)
    i1 = jnp.min(jnp.where(lg == m1, lane, LANES), axis=-1, keepdims=True)
    lg2 = jnp.where(lane == i1, -jnp.inf, lg)
    m2 = jnp.max(lg2, axis=-1, keepdims=True)
    i2 = jnp.min(jnp.where(lg2 == m2, lane, LANES), axis=-1, keepdims=True)
    e = jnp.exp(m2 - m1)
    den = 1.0 + e
    r_ref[...] = jnp.where(lane == 0, i1.astype(F32),
                           jnp.where(lane == 1, i2.astype(F32),
                                     jnp.where(lane == 2, 1.0 / den,
                                               jnp.where(lane == 3, e / den, 0.0))))


def _router(x, g, wr_pad, tm):
    T = x.shape[0]
    assert T % tm == 0
    return pl.pallas_call(
        _router_kernel,
        grid=(T // tm,),
        in_specs=[pl.BlockSpec((tm, D_MODEL), lambda i: (i, 0)),
                  pl.BlockSpec((1, D_MODEL), lambda i: (0, 0)),
                  pl.BlockSpec((D_MODEL, LANES), lambda i: (0, 0))],
        out_specs=[pl.BlockSpec((tm, D_MODEL), lambda i: (i, 0)),
                   pl.BlockSpec((tm, LANES), lambda i: (i, 0))],
        out_shape=[jax.ShapeDtypeStruct((T, D_MODEL), F32),
                   jax.ShapeDtypeStruct((T, LANES), F32)],
        compiler_params=_params(("parallel",)),
        name="moe_router",
    )(x, g, wr_pad)


GATHER_WINDOW = 32
SC_WORKERS = 32


def _row_gather(src, idx):
    n = idx.shape[0]
    step = GATHER_WINDOW * SC_WORKERS
    n_pad = -(-n // step) * step
    if n_pad != n:
        idx = jnp.concatenate([idx, jnp.zeros((n_pad - n,), idx.dtype)])
    width = src.shape[1]
    per_worker = n_pad // SC_WORKERS
    mesh = plsc.VectorSubcoreMesh(core_axis_name="core", subcore_axis_name="subcore")

    @functools.partial(pl.kernel, out_type=jax.ShapeDtypeStruct((n_pad, width), src.dtype), mesh=mesh,
                       scratch_types=[pltpu.VMEM((per_worker,), I32),
                                      pltpu.VMEM((GATHER_WINDOW, width), src.dtype)],
                       name="row_gather")
    def gather(src_hbm, idx_hbm, dst_hbm, idx_v, buf):
        worker = lax.axis_index("core") * (SC_WORKERS // 2) + lax.axis_index("subcore")
        base = worker * per_worker
        pltpu.sync_copy(idx_hbm.at[pl.ds(base, per_worker)], idx_v)

        @pl.loop(0, per_worker // GATHER_WINDOW)
        def _(j):
            pltpu.sync_copy(src_hbm.at[idx_v.at[pl.ds(j * GATHER_WINDOW, GATHER_WINDOW)]], buf)
            pltpu.sync_copy(buf, dst_hbm.at[pl.ds(base + j * GATHER_WINDOW, GATHER_WINDOW)])

    return gather(src, idx)


def _moe_ffn_kernel(be_ref, nu_ref, x_ref, wg_ref, wu_ref, wd_ref, o_ref, acc_sc):
    b = pl.program_id(0)
    f = pl.program_id(1)
    nf = pl.num_programs(1)
    used = b < nu_ref[0]

    @pl.when(used)
    def _():
        @pl.when(f == 0)
        def _():
            acc_sc[...] = jnp.zeros_like(acc_sc)

        x = x_ref[...].astype(BF16)
        a = jnp.dot(x, wg_ref[...].astype(BF16), preferred_element_type=F32)
        u = jnp.dot(x, wu_ref[...].astype(BF16), preferred_element_type=F32)
        acc_sc[...] += _bdot(_silu(a) * u, wd_ref[...])

        @pl.when(f == nf - 1)
        def _():
            o_ref[...] = acc_sc[...]

    @pl.when(jnp.logical_not(used) & (f == nf - 1))
    def _():
        o_ref[...] = jnp.zeros_like(o_ref)


def _moe_ffn(xb, block_e, n_used, wg, wu, wd, blk, tf):
    cap = xb.shape[0]
    nb = cap // blk
    nf = D_FF // tf

    def bsel(b, nu):
        return jnp.minimum(b, nu[0] - 1)

    def fsel(b, f, nu):
        return jnp.where(b < nu[0], f, nf - 1)

    return pl.pallas_call(
        _moe_ffn_kernel,
        grid_spec=pltpu.PrefetchScalarGridSpec(
            num_scalar_prefetch=2,
            grid=(nb, nf),
            in_specs=[
                pl.BlockSpec((blk, D_MODEL), lambda b, f, be, nu: (bsel(b, nu), 0)),
                pl.BlockSpec((None, D_MODEL, tf), lambda b, f, be, nu: (be[bsel(b, nu)], 0, fsel(b, f, nu))),
                pl.BlockSpec((None, D_MODEL, tf), lambda b, f, be, nu: (be[bsel(b, nu)], 0, fsel(b, f, nu))),
                pl.BlockSpec((None, tf, D_MODEL), lambda b, f, be, nu: (be[bsel(b, nu)], fsel(b, f, nu), 0)),
            ],
            out_specs=pl.BlockSpec((blk, D_MODEL), lambda b, f, be, nu: (b, 0)),
            scratch_shapes=[pltpu.VMEM((blk, D_MODEL), F32)],
        ),
        out_shape=jax.ShapeDtypeStruct((cap, D_MODEL), F32),
        compiler_params=_params(("arbitrary", "arbitrary")),
        name="moe_ffn",
    )(block_e, n_used, xb, wg, wu, wd)


def _moe_combine_kernel(x_ref, y1_ref, y2_ref, r_ref, g_ref, o_ref, *, final):
    r = r_ref[...]
    y = x_ref[...] + (r[:, 2:3] * y1_ref[...] + r[:, 3:4] * y2_ref[...])
    o_ref[...] = _rms(y, g_ref[...]) if final else y


def _moe_combine(x, y1, y2, route, tm, final_g):
    T = x.shape[0]
    g = jnp.ones((1, D_MODEL), F32) if final_g is None else final_g
    return pl.pallas_call(
        functools.partial(_moe_combine_kernel, final=final_g is not None),
        grid=(T // tm,),
        in_specs=[pl.BlockSpec((tm, D_MODEL), lambda i: (i, 0)),
                  pl.BlockSpec((tm, D_MODEL), lambda i: (i, 0)),
                  pl.BlockSpec((tm, D_MODEL), lambda i: (i, 0)),
                  pl.BlockSpec((tm, LANES), lambda i: (i, 0)),
                  pl.BlockSpec((1, D_MODEL), lambda i: (0, 0))],
        out_specs=pl.BlockSpec((tm, D_MODEL), lambda i: (i, 0)),
        out_shape=jax.ShapeDtypeStruct((T, D_MODEL), F32),
        compiler_params=_params(("parallel",)),
        name="moe_combine",
    )(x, y1, y2, route, g)


def _moe_plan(e_top, blk):
    T = e_top.shape[0]
    n = 2 * T
    flat_e = e_top.reshape(-1)
    onehot = (flat_e[:, None] == jnp.arange(N_EXPERTS, dtype=I32)[None, :]).astype(I32)
    csum = jnp.cumsum(onehot, axis=0)
    rank = jnp.sum((csum - onehot) * onehot, axis=1)
    counts = csum[-1]
    padded = (counts + blk - 1) // blk * blk
    ends = jnp.cumsum(padded)
    pstart = ends - padded
    dest = (pstart[flat_e] + rank).astype(I32)
    n_blocks = -(-n // blk) + N_EXPERTS
    cap = n_blocks * blk
    slot_tok = (jnp.arange(cap, dtype=I32) % T).at[dest].set(jnp.arange(n, dtype=I32) // 2)
    first = jnp.arange(n_blocks, dtype=I32) * blk
    block_e = jnp.minimum(jnp.sum((ends[None, :] <= first[:, None]).astype(I32), axis=1), N_EXPERTS - 1)
    n_used = (ends[-1] // blk).astype(I32).reshape(1)
    return dest, slot_tok, block_e, n_used


def _moe_dispatch(x, g, wr_pad, tm, blk):
    h, route = _router(x, g, wr_pad, tm)
    dest, slot_tok, block_e, n_used = _moe_plan(route[:, 0:2].astype(I32), blk)
    return dict(x=x, route=route, dest=dest, xb=_row_gather(h, slot_tok), block_e=block_e, n_used=n_used)


def _moe_finish(d, wg, wu, wd, tm, blk, tf, final_g):
    yb = _moe_ffn(d['xb'], d['block_e'], d['n_used'], wg, wu, wd, blk, tf)
    d2 = d['dest'].reshape(-1, 2)
    return _moe_combine(d['x'], _row_gather(yb, d2[:, 0]), _row_gather(yb, d2[:, 1]), d['route'], tm, final_g)


def _norm_kernel(x_ref, g_ref, o_ref):
    o_ref[...] = _rms(x_ref[...], g_ref[...])


def _final_norm(x, g, tm):
    T = x.shape[0]
    return pl.pallas_call(
        _norm_kernel,
        grid=(T // tm,),
        in_specs=[pl.BlockSpec((tm, D_MODEL), lambda i: (i, 0)),
                  pl.BlockSpec((1, D_MODEL), lambda i: (0, 0))],
        out_specs=pl.BlockSpec((tm, D_MODEL), lambda i: (i, 0)),
        out_shape=jax.ShapeDtypeStruct((T, D_MODEL), F32),
        compiler_params=_params(("parallel",)),
        name="final_norm",
    )(x, g)


_IN_SPLITS = (D_FOX, D_FOX, D_FOX, N_HEADS, D_NSA, 6 * HEAD_DIM, 3 * N_HEADS,
              D_FOX, D_FOX, D_GLA, GLA_RANK, D_GLA)


def _reorder_w_in(w):
    offs = [0]
    for s in _IN_SPLITS:
        offs.append(offs[-1] + s)
    seg = lambda k: w[:, offs[k]:offs[k + 1]]
    fq, fk, fv, ff, nq, nkv, ng, gq, gk, gv, glr, gog = [seg(k) for k in range(12)]
    pad = jnp.zeros((w.shape[0], LANES - SM_GLR - GLA_RANK), w.dtype)
    return jnp.concatenate([fq, fk, fv, nq, nkv, gq, gk, gv, gog, ff, ng, glr, pad], axis=1)


def _rope_table(pos):
    inv = ROPE_THETA ** (-jnp.arange(ROPE_HALF, dtype=F32) / ROPE_HALF)
    ang = pos.astype(F32)[:, None] * inv[None, :]
    cos, sin = jnp.cos(ang), jnp.sin(ang)
    P = pos.shape[0]
    one = jnp.ones((P, HEAD_DIM - ROPE_DIM), F32)
    zero = jnp.zeros((P, HEAD_DIM - ROPE_DIM), F32)
    z8 = jnp.zeros((P, ROPE_HALF), F32)
    a64 = jnp.concatenate([cos, cos, one], axis=1)
    p64 = jnp.concatenate([z8, sin, zero], axis=1)
    m64 = jnp.concatenate([-sin, z8, zero], axis=1)
    i64 = jnp.ones((P, HEAD_DIM), F32)
    o64 = jnp.zeros((P, HEAD_DIM), F32)
    return jnp.concatenate([a64, a64, p64, p64, m64, m64, a64, i64, p64, o64, m64, o64], axis=1)


def _layer_mix_params(l, norm_mix_g, w_in, b_fox_f, w_cmp, pe_cmp, w_gla_gk, b_gla_gk, g_gla_norm, w_out):
    sb = jnp.zeros((1, LANES), F32).at[0, SM_FF:SM_FF + N_HEADS].set(b_fox_f[l])
    wgk = jnp.zeros((LANES, D_FOX), F32).at[SM_GLR:SM_GLR + GLA_RANK].set(w_gla_gk[l])
    w_r = _reorder_w_in(w_in[l])
    return dict(g=norm_mix_g[l].reshape(1, D_MODEL), w_r=w_r.astype(BF16), w_r32=w_r, sb=sb,
                wgk=wgk.astype(BF16), wgk32=wgk, w_out32=w_out[l],
                bgk=b_gla_gk[l].reshape(1, D_FOX), w_cmp=w_cmp[l],
                pe_cmp=pe_cmp[l].reshape(2, 1, CMP_LEN * HEAD_DIM),
                gn=g_gla_norm[l].reshape(1, GLA_DV), w_out=w_out[l].astype(BF16))


def _mix_prompt(x, B, S, p, tab, tm, t_fox, tk_sel, tc_gla):
    (fq, fkv, nq, nqr, rows, win, gqk, gv, gog, glog, small) = _in_projection(
        x, 0, B * S, tm, p['g'], p['w_r'], p['sb'], p['wgk'], p['bgk'], tab, S)
    cr = _fox_cumsum(small, B, S, min(S, 512))
    o_fox = _fox_prompt(fq, fkv, cr, B, S, t_fox)
    n_blk = B * S // CMP_LEN
    blocks = _block_major(rows[:, 0:2 * HEAD_DIM].reshape(B * S, 2, HEAD_DIM), n_blk)
    cmp = _compress(blocks, p['pe_cmp'], p['w_cmp'], min(256, n_blk), True)
    o_nsa = _nsa_prompt(nq, nqr, small, cmp, rows, win, B, S, tk_sel)
    o_gla, g_state = _gla_prompt(gqk, gv, glog, B, S, tc_gla)
    x_new = _out_projection(x, o_fox, o_nsa, o_gla, gog, p['gn'], p['w_out'], tm)
    return x_new, dict(fkv=fkv, small=small, rows=rows, win=win, g_state=g_state)


def _per_head_col(vals):
    r = lax.broadcasted_iota(I32, (SUBLANES, 1), 0)
    out = jnp.zeros((SUBLANES, 1), F32)
    for h, v in enumerate(vals):
        out = out + jnp.where(r == h, v, 0.0)
    return out


def _per_head_row(vals, width):
    grp = lax.broadcasted_iota(I32, (1, width), 1) // HEAD_DIM
    out = jnp.zeros((1, width), F32)
    for h, v in enumerate(vals):
        out = out + jnp.where(grp == h, v, 0.0)
    return out


def _head_lane_sums(row):
    grp = lax.broadcasted_iota(I32, row.shape, 1) // HEAD_DIM
    return [jnp.sum(jnp.where(grp == h, row, 0.0), axis=1, keepdims=True) for h in range(N_HEADS)]


def _cols_of(row):
    return jnp.concatenate([jnp.broadcast_to(row[:, j:j + LANES], (LANES, LANES)).T
                            for j in range(0, row.shape[1], LANES)], axis=0)


def _row_of(col):
    return jnp.concatenate([jnp.broadcast_to(col[j:j + LANES], (LANES, LANES)).T[0:1, :]
                            for j in range(0, col.shape[0], LANES)], axis=1)


def _sublane_group_sum(x):
    return jnp.sum(x.reshape(x.shape[0] // SUBLANES, SUBLANES, x.shape[1]), axis=0)


def _fold_matrix(n_pages):
    r = lax.broadcasted_iota(I32, (n_pages * SUBLANES, n_pages * N_HEADS * SUBLANES), 0)
    c = lax.broadcasted_iota(I32, (n_pages * SUBLANES, n_pages * N_HEADS * SUBLANES), 1)
    blk = c // SUBLANES
    return ((blk // N_HEADS == r // SUBLANES) & (blk % N_HEADS == r % SUBLANES)).astype(F32)


def _fox_decode_kernel(pt_ref, q_ref, kvn_ref, smn_ref, *refs, n_pages):
    del pt_ref
    kv_refs = refs[0:n_pages]
    lf_refs = refs[n_pages:2 * n_pages]
    o_ref, lf_sc, part_sc = refs[2 * n_pages:]
    R = n_pages * SUBLANES
    PG = kv_refs[0].shape[-1]
    row = pl.ds(pl.program_id(0) % SUBLANES, 1)
    q_row = q_ref[row, :]
    kvn = kvn_ref[row, :]
    smn = smn_ref[row, :]
    q_cols = _cols_of(q_row)

    lf_sc[...] = jnp.zeros_like(lf_sc)
    for p in range(n_pages):
        lf_sc[p * SUBLANES:p * SUBLANES + N_HEADS, :] = lf_refs[p][...]
    lft = lf_sc[...]
    k0 = lax.broadcasted_iota(I32, (PG, PG), 0)
    k1 = lax.broadcasted_iota(I32, (PG, PG), 1)
    within = jnp.dot(lft, (k0 > k1).astype(F32), precision=HI, preferred_element_type=F32)
    tot = jnp.broadcast_to(jnp.sum(lft, axis=1, keepdims=True), (R, PG))
    r0 = lax.broadcasted_iota(I32, (R, R), 0)
    r1 = lax.broadcasted_iota(I32, (R, R), 1)
    later = ((r1 % SUBLANES == r0 % SUBLANES) & (r1 // SUBLANES > r0 // SUBLANES)).astype(F32)
    cross = jnp.dot(later, tot, precision=HI, preferred_element_type=F32)
    rr = lax.broadcasted_iota(I32, (R, 1), 0) % SUBLANES
    newcol = jnp.zeros((R, 1), F32)
    for h in range(N_HEADS):
        newcol = newcol + jnp.where(rr == h, smn[:, SM_FF + h:SM_FF + h + 1], 0.0)
    bias = (within + cross + newcol).reshape(n_pages, SUBLANES, PG)

    for p in range(n_pages):
        for h in range(N_HEADS):
            g = p * N_HEADS + h
            part_sc[g * SUBLANES:(g + 1) * SUBLANES, :] = _sublane_group_sum(
                kv_refs[p][0, h] * q_cols[h * HEAD_DIM:(h + 1) * HEAD_DIM])
    s = jnp.dot(_fold_matrix(n_pages), part_sc[...], precision=HI, preferred_element_type=F32)
    s3 = s.reshape(n_pages, SUBLANES, PG) * SCALE + bias
    s_new = _per_head_col(_head_lane_sums(q_row * kvn[:, 0:D_FOX])) * SCALE
    m = jnp.max(jnp.max(s3, axis=2, keepdims=True), axis=0)
    m = jnp.maximum(m, s_new)
    p3 = jnp.exp(s3 - m[None])
    pn = jnp.exp(s_new - m)
    inv = 1.0 / (jnp.sum(jnp.sum(p3, axis=2, keepdims=True), axis=0) + pn)
    o_cols = []
    for h in range(N_HEADS):
        acc = jnp.zeros((HEAD_DIM, PG), F32)
        for p in range(n_pages):
            acc = acc + kv_refs[p][1, h] * p3[p, h:h + 1, :]
        o_cols.append(jnp.sum(acc, axis=1, keepdims=True) * inv[h:h + 1])
    w_new = _per_head_row([pn[h:h + 1] * inv[h:h + 1] for h in range(N_HEADS)], D_FOX)
    o_ref[row, :] = _row_of(jnp.concatenate(o_cols, axis=0)) + w_new * kvn[:, D_FOX:2 * D_FOX]


def _fox_decode(l, pt_flat, n_pages, fq, fkv, small, kv_cache_t, lft_cache):
    DB = fq.shape[0]
    PG = kv_cache_t.shape[-1]
    page = lambda p, nz: (lambda b, pt: (l, pt[b * n_pages + p]) + (0,) * nz)
    rows8 = lambda w: pl.BlockSpec((SUBLANES, w), lambda b, pt: (b // SUBLANES, 0))
    return pl.pallas_call(
        functools.partial(_fox_decode_kernel, n_pages=n_pages),
        grid_spec=pltpu.PrefetchScalarGridSpec(
            num_scalar_prefetch=1,
            grid=(DB,),
            in_specs=[rows8(D_FOX), rows8(2 * D_FOX), rows8(LANES)]
            + [pl.BlockSpec((None, None, 2, N_HEADS, HEAD_DIM, PG), page(p, 4)) for p in range(n_pages)]
            + [pl.BlockSpec((None, None, N_HEADS, PG), page(p, 2)) for p in range(n_pages)],
            out_specs=rows8(D_FOX),
            scratch_shapes=[pltpu.VMEM((n_pages * SUBLANES, PG), F32),
                            pltpu.VMEM((n_pages * N_HEADS * SUBLANES, PG), F32)],
        ),
        out_shape=jax.ShapeDtypeStruct((DB, D_FOX), F32),
        compiler_params=_params(("arbitrary",)),
        name="fox_decode",
    )(pt_flat, fq, fkv, small, *([kv_cache_t] * n_pages), *([lft_cache] * n_pages))


def _nsa_decode_kernel(pt_ref, q_ref, qr_ref, rown_ref, winn_ref, sm_ref, win_ref, *refs,
                       n_pages, past_len):
    del pt_ref
    pg_refs = refs[0:n_pages]
    cmp_refs = refs[n_pages:2 * n_pages]
    o_ref, nw_ref, cmp_sc, qc_sc, part_sc, sw_sc = refs[2 * n_pages:]
    R = n_pages * SUBLANES
    PG = pg_refs[0].shape[-1]
    WB = win_ref.shape[-1]
    per_page = PG // CMP_LEN
    assert per_page <= SUBLANES and PG == 2 * SEL_LEN and R == LANES
    jt = past_len // SEL_LEN
    row = pl.ds(pl.program_id(0) % SUBLANES, 1)
    q_row = q_ref[row, :]
    qr_row = qr_ref[row, :]
    rown = rown_ref[row, :]
    winn = winn_ref[row, :]
    smn = sm_ref[row, :]
    qr_cols = _cols_of(qr_row)
    rep4 = lambda r64: jnp.concatenate([r64] * N_HEADS, axis=1)

    qc_sc[...] = jnp.zeros_like(qc_sc)
    for h in range(N_HEADS):
        qc_sc[h:h + 1, 0:HEAD_DIM] = q_row[:, h * HEAD_DIM:(h + 1) * HEAD_DIM]
    head_row = lax.broadcasted_iota(I32, (SUBLANES, 1), 0) < N_HEADS

    cmp_sc[...] = jnp.zeros_like(cmp_sc)
    for p in range(n_pages):
        cmp_sc[p * SUBLANES:p * SUBLANES + per_page, :] = cmp_refs[p][...]
    cmpa = cmp_sc[...]
    lane = lax.broadcasted_iota(I32, (1, R), 1)
    blk = per_page * (lane // SUBLANES) + lane % SUBLANES
    complete = (lane % SUBLANES < per_page) & ((blk + 1) * CMP_LEN - 1 <= past_len)
    s = lax.dot_general(qc_sc[...], cmpa, NT, precision=HI, preferred_element_type=F32) * SCALE
    s = jnp.where(complete, s, NEG)
    e = jnp.exp(s - jnp.max(s, axis=-1, keepdims=True))
    pc = e / jnp.sum(e, axis=-1, keepdims=True) * complete.astype(F32)
    vcb_t = cmpa.T[HEAD_DIM:2 * HEAD_DIM, :]
    o_cmp = [jnp.sum(vcb_t * pc[h:h + 1, :], axis=1, keepdims=True) for h in range(N_HEADS)]

    imp_c = jnp.sum(jnp.where(head_row, pc, 0.0), axis=0, keepdims=True)
    imp_s = imp_c + pltpu.roll(imp_c, R - 1, 1)
    cand = (lane % SUBLANES == 0) | (lane % SUBLANES == 2)
    jsel = 2 * (lane // SUBLANES) + (lane % SUBLANES) // 2
    score = jnp.where(jsel == jt, 2.0 * SEL_FORCE,
                      jnp.where((jsel == 0) | (jsel == jt - 1), SEL_FORCE,
                                jnp.where(jsel <= jt, imp_s + 0.0, -1.0)))
    score_b = jnp.broadcast_to(score, (R, R))
    key_row = _order_key(score_b)
    key_col = _order_key(score_b.T)
    l0 = lax.broadcasted_iota(I32, (R, R), 0)
    l1 = lax.broadcasted_iota(I32, (R, R), 1)
    cand_col = (l0 % SUBLANES == 0) | (l0 % SUBLANES == 2)
    beats = cand_col & (key_col > jnp.where(l0 < l1, key_row - 1, key_row))
    cnt = jnp.sum(beats.astype(I32), axis=0, keepdims=True)
    sel_row = (cand & (cnt < TOP_N - 1)).astype(F32)
    sel_col = jnp.broadcast_to(sel_row, (R, R)).T
    half = ((l0 % SUBLANES == 0) & (l1 < SEL_LEN)) | ((l0 % SUBLANES == 2) & (l1 >= SEL_LEN))
    z = jnp.where(half, sel_col, 0.0)
    same_page = (l1 // SUBLANES == l0 // SUBLANES).astype(BF16)
    picked = jnp.dot(same_page, z.astype(BF16), preferred_element_type=F32)
    picked = picked.reshape(n_pages, SUBLANES, PG) > 0.5

    for p in range(n_pages):
        ks_t = pg_refs[p][2]
        for h in range(N_HEADS):
            g = p * N_HEADS + h
            part_sc[g * SUBLANES:(g + 1) * SUBLANES, :] = _sublane_group_sum(
                ks_t * qr_cols[h * HEAD_DIM:(h + 1) * HEAD_DIM])
    s = jnp.dot(_fold_matrix(n_pages), part_sc[...], precision=HI, preferred_element_type=F32)
    s3 = jnp.where(picked, s.reshape(n_pages, SUBLANES, PG) * SCALE, NEG)
    s_new = _per_head_col(_head_lane_sums(qr_row * rep4(rown[:, 2 * HEAD_DIM:3 * HEAD_DIM]))) * SCALE
    m = jnp.maximum(jnp.max(jnp.max(s3, axis=2, keepdims=True), axis=0), s_new)
    p3 = jnp.exp(s3 - m[None])
    pn = jnp.exp(s_new - m)
    inv = 1.0 / (jnp.sum(jnp.sum(p3, axis=2, keepdims=True), axis=0) + pn)
    o_sel = []
    for h in range(N_HEADS):
        acc = jnp.zeros((HEAD_DIM, PG), F32)
        for p in range(n_pages):
            acc = acc + pg_refs[p][3] * p3[p, h:h + 1, :]
        o_sel.append(jnp.sum(acc, axis=1, keepdims=True) * inv[h:h + 1])

    kw_t = win_ref[0]
    vw_t = win_ref[1]
    wlane = lax.broadcasted_iota(I32, (1, WB), 1)
    wpos = past_len - WB + wlane
    wd = past_len - wpos
    wok = (wd >= 0) & (wd < WINDOW) & (wpos >= 0)
    sw_sc[...] = jnp.zeros_like(sw_sc)
    for h in range(N_HEADS):
        qh = qr_cols[h * HEAD_DIM:(h + 1) * HEAD_DIM]
        sw_sc[h:h + 1, :] = jnp.sum(kw_t * jnp.concatenate([qh] * (WB // LANES), axis=1), axis=0, keepdims=True)
    sw = jnp.where(wok, sw_sc[...] * SCALE, NEG)
    sw_new = _per_head_col(_head_lane_sums(qr_row * rep4(winn[:, 0:HEAD_DIM]))) * SCALE
    mw = jnp.maximum(jnp.max(sw, axis=-1, keepdims=True), sw_new)
    ew = jnp.exp(sw - mw)
    en = jnp.exp(sw_new - mw)
    invw = 1.0 / (jnp.sum(ew, axis=-1, keepdims=True) + en)

    gate = lambda h, c: smn[:, SM_NG + 3 * h + c:SM_NG + 3 * h + c + 1]
    o_cols = []
    for h in range(N_HEADS):
        o_win = jnp.sum(vw_t * ew[h:h + 1, :], axis=1, keepdims=True) * invw[h:h + 1]
        o_cols.append(gate(h, 0) * o_cmp[h] + gate(h, 1) * o_sel[h] + gate(h, 2) * o_win)
    w_sel = _per_head_row([gate(h, 1) * pn[h:h + 1] * inv[h:h + 1] for h in range(N_HEADS)], D_NSA)
    w_win = _per_head_row([gate(h, 2) * en[h:h + 1] * invw[h:h + 1] for h in range(N_HEADS)], D_NSA)
    o_ref[row, :] = (_row_of(jnp.concatenate(o_cols, axis=0))
                     + w_sel * rep4(rown[:, 3 * HEAD_DIM:4 * HEAD_DIM])
                     + w_win * rep4(winn[:, HEAD_DIM:2 * HEAD_DIM]))
    last = lax.broadcasted_iota(I32, (HEAD_DIM, WB), 1) == WB - 1
    winn_cols = _cols_of(winn)
    for s in range(2):
        new_col = winn_cols[s * HEAD_DIM:(s + 1) * HEAD_DIM, 0:1]
        nw_ref[s] = jnp.where(last, new_col, pltpu.roll(win_ref[s], WB - 1, 1))


def _nsa_decode(l, pt_flat, n_pages, past_len, nq, nqr, rows, win, small, nsa_cache_t, cmp_pool, win_state_t):
    DB = nq.shape[0]
    PG = nsa_cache_t.shape[-1]
    WB = win_state_t.shape[-1]
    page = lambda p: (lambda b, pt: (l, pt[b * n_pages + p], 0, 0, 0))
    cpage = lambda p: (lambda b, pt: (pt[b * n_pages + p], 0, 0))
    rows8 = lambda w: pl.BlockSpec((SUBLANES, w), lambda b, pt: (b // SUBLANES, 0))
    return pl.pallas_call(
        functools.partial(_nsa_decode_kernel, n_pages=n_pages, past_len=past_len),
        grid_spec=pltpu.PrefetchScalarGridSpec(
            num_scalar_prefetch=1,
            grid=(DB,),
            in_specs=[rows8(D_NSA), rows8(D_NSA), rows8(4 * HEAD_DIM), rows8(2 * HEAD_DIM), rows8(LANES),
                      pl.BlockSpec((None, None, 2, HEAD_DIM, WB), lambda b, pt: (l, b, 0, 0, 0))]
            + [pl.BlockSpec((None, None, 4, HEAD_DIM, PG), page(p)) for p in range(n_pages)]
            + [pl.BlockSpec((None, PG // CMP_LEN, LANES), cpage(p)) for p in range(n_pages)],
            out_specs=[rows8(D_NSA),
                       pl.BlockSpec((None, 2, HEAD_DIM, WB), lambda b, pt: (b, 0, 0, 0))],
            scratch_shapes=[pltpu.VMEM((n_pages * SUBLANES, LANES), F32),
                            pltpu.VMEM((SUBLANES, LANES), F32),
                            pltpu.VMEM((n_pages * N_HEADS * SUBLANES, PG), F32),
                            pltpu.VMEM((SUBLANES, WB), F32)],
        ),
        out_shape=[jax.ShapeDtypeStruct((DB, D_NSA), F32),
                   jax.ShapeDtypeStruct((DB, 2, HEAD_DIM, WB), F32)],
        compiler_params=_params(("arbitrary",)),
        name="nsa_decode",
    )(pt_flat, nq, nqr, rows, win, small, win_state_t,
      *([nsa_cache_t] * n_pages), *([cmp_pool] * n_pages))


def _gla_decode_kernel(q_ref, k_ref, g_ref, v_ref, s_ref, o_ref, so_ref):
    s_new = jnp.exp(g_ref[...]) * s_ref[...] + k_ref[...] * v_ref[...]
    so_ref[...] = s_new
    o_ref[...] = jnp.sum((q_ref[...] * SCALE) * s_new, axis=2, keepdims=True)


def _gla_decode(l, gqk, gv, glog, state, nb):
    DB = gqk.shape[0]
    col = lambda a: a.reshape(DB, N_HEADS, HEAD_DIM, 1)
    cspec = pl.BlockSpec((nb, N_HEADS, HEAD_DIM, 1), lambda i: (i, 0, 0, 0))
    vspec = pl.BlockSpec((nb, N_HEADS, 1, GLA_DV), lambda i: (i, 0, 0, 0))
    sspec = pl.BlockSpec((nb, N_HEADS, HEAD_DIM, GLA_DV), lambda i: (i, 0, 0, 0))
    o, s_new = pl.pallas_call(
        _gla_decode_kernel,
        grid=(DB // nb,),
        in_specs=[cspec, cspec, cspec, vspec,
                  pl.BlockSpec((None, nb, N_HEADS, HEAD_DIM, GLA_DV), lambda i: (l, i, 0, 0, 0))],
        out_specs=[vspec, sspec],
        out_shape=[jax.ShapeDtypeStruct((DB, N_HEADS, 1, GLA_DV), F32),
                   jax.ShapeDtypeStruct((DB, N_HEADS, HEAD_DIM, GLA_DV), F32)],
        compiler_params=_params(("parallel",)),
        name="gla_decode",
    )(col(gqk[:, 0:D_FOX]), col(gqk[:, D_FOX:2 * D_FOX]), col(glog),
      gv.reshape(DB, N_HEADS, 1, GLA_DV), state)
    return o.reshape(DB, D_GLA), s_new


def _mix_sample(x, l, p, tab, pt_flat, n_pages, past_len, fox_kv_t, fox_lft_c, nsa_t, cmp_blocks,
                win_state_t, gla_state):
    DB = x.shape[0]
    (fq, fkv, nq, nqr, rows, win, gqk, gv, gog, glog, small) = _in_projection(
        x, 0, DB, DB, p['g'], p['w_r32'], p['sb'], p['wgk32'], p['bgk'], tab, DB)
    o_fox = _fox_decode(l, pt_flat, n_pages, fq, fkv, small, fox_kv_t, fox_lft_c)
    n_pool, PG = nsa_t.shape[1], nsa_t.shape[-1]
    per_layer = n_pool * (PG // CMP_LEN)
    cmp_pool = _compress(cmp_blocks, p['pe_cmp'], p['w_cmp'], 256, False, row0=l * per_layer, n_rows=per_layer)
    cmp_pool = cmp_pool.reshape(n_pool, PG // CMP_LEN, LANES)
    o_nsa, new_win = _nsa_decode(l, pt_flat, n_pages, past_len, nq, nqr, rows, win, small,
                                 nsa_t, cmp_pool, win_state_t)
    o_gla, g_state = _gla_decode(l, gqk, gv, glog, gla_state, 8)
    x_new = _out_projection(x, o_fox, o_nsa, o_gla, gog, p['gn'], p['w_out32'], DB)
    return x_new, dict(fkv=fkv, small=small, rows=rows, win=new_win, g_state=g_state)


def kernel(x_prompt, x_sample, cache_fox_kv, cache_fox_logf, cache_nsa_kv, state_nsa_win, state_gla,
           page_table, norm_mix_g, w_in, b_fox_f, w_cmp, pe_cmp, w_gla_gk, b_gla_gk, g_gla_norm, w_out,
           norm_ffn_g, dense_w_gate, dense_w_up, dense_w_down, moe_w_router, moe_w_gate, moe_w_up,
           moe_w_down, final_norm_g):
    B, S, _ = x_prompt.shape
    DB, TN, _ = x_sample.shape
    assert TN == 1
    depth, n_pool, PG = cache_fox_kv.shape[0:3]
    n_pages = page_table.shape[1]
    past_len = n_pages * PG
    WB = state_nsa_win.shape[2]
    xp = x_prompt.reshape(B * S, D_MODEL)
    xs = x_sample.reshape(DB, D_MODEL)
    tab_p = _rope_table(jnp.arange(S))
    tab_s = _rope_table(jnp.full((DB,), past_len, I32))
    pt_flat = page_table.reshape(-1).astype(I32)
    fox_kv_t = jnp.transpose(cache_fox_kv, (0, 1, 3, 4, 5, 2))
    fox_lft_c = jnp.swapaxes(cache_fox_logf, 2, 3)
    nsa_t = jnp.transpose(cache_nsa_kv, (0, 1, 3, 4, 2))
    win_state_t = jnp.transpose(state_nsa_win, (0, 1, 3, 4, 2))
    n_cmp = depth * n_pool * (PG // CMP_LEN)
    cmp_blocks = _block_major(cache_nsa_kv[:, :, :, 0:2, :].reshape(n_cmp * CMP_LEN, 2, HEAD_DIM), n_cmp)
    gfin = final_norm_g.reshape(1, D_MODEL)
    cp, cs = [], []
    for l in range(depth):
        p = _layer_mix_params(l, norm_mix_g, w_in, b_fox_f, w_cmp, pe_cmp, w_gla_gk, b_gla_gk,
                              g_gla_norm, w_out)
        gf = norm_ffn_g[l].reshape(1, D_MODEL)
        i = l // 2
        moe = l % 2 == 1
        xp, c = _mix_prompt(xp, B, S, p, tab_p, 512, 512, 512, 256)
        cp.append(c)
        if moe:
            wr = jnp.zeros((D_MODEL, LANES), F32).at[:, 0:N_EXPERTS].set(moe_w_router[i])
            experts = (moe_w_gate[i], moe_w_up[i], moe_w_down[i])
            disp_p = _moe_dispatch(xp, gf, wr, 512, 1024)
        xs, c = _mix_sample(xs, l, p, tab_s, pt_flat, n_pages, past_len, fox_kv_t, fox_lft_c, nsa_t,
                            cmp_blocks, win_state_t, state_gla)
        cs.append(c)
        if moe:
            disp_s = _moe_dispatch(xs, gf, wr, DB, LANES)
            fin = gfin if l == depth - 1 else None
            xp = _moe_finish(disp_p, *experts, 512, 1024, 512, fin)
            xs = _moe_finish(disp_s, *experts, DB, LANES, 512, fin)
        else:
            wg, wu, wd = (dense_w_gate[i].astype(BF16), dense_w_up[i].astype(BF16),
                          dense_w_down[i].astype(BF16))
            xp = _dense_ffn(xp, gf, wg, wu, wd, 1024, 512)
            xs = _dense_ffn(xs, gf, dense_w_gate[i], dense_w_up[i], dense_w_down[i], DB, 512)
    if depth % 2 == 1:
        xp = _final_norm(xp, gfin, 512)
        xs = _final_norm(xs, gfin, DB)
    y_p = xp.reshape(B, S, D_MODEL)
    y_s = xs.reshape(DB, 1, D_MODEL)
    wp = min(WINDOW, S)
    st = lambda key, group: jnp.stack([c[key] for c in group])
    return (y_p, y_s,
            st('fkv', cp).reshape(depth, B, S, 2, N_HEADS, HEAD_DIM),
            st('small', cp)[:, :, SM_FF:SM_FF + N_HEADS].reshape(depth, B, S, N_HEADS),
            st('rows', cp).reshape(depth, B, S, 4, HEAD_DIM),
            st('win', cp).reshape(depth, B, S, 2, HEAD_DIM)[:, :, S - wp:],
            st('g_state', cp),
            st('fkv', cs).reshape(depth, DB, 1, 2, N_HEADS, HEAD_DIM),
            st('small', cs)[:, :, SM_FF:SM_FF + N_HEADS].reshape(depth, DB, 1, N_HEADS),
            st('rows', cs).reshape(depth, DB, 1, 4, HEAD_DIM),
            jnp.transpose(st('win', cs), (0, 1, 4, 2, 3)),
            st('g_state', cs))
```

```python
import functools

import jax
import jax.numpy as jnp
from jax import lax
from jax.experimental import pallas as pl
from jax.experimental.pallas import tpu as pltpu
from jax.experimental.pallas import tpu_sc as plsc

F32 = jnp.float32
BF16 = jnp.bfloat16
I32 = jnp.int32
HI = lax.Precision.HIGHEST

D_MODEL = 1024
HEAD_DIM = 64
N_HEADS = 4
D_FOX = N_HEADS * HEAD_DIM
D_NSA = N_HEADS * HEAD_DIM
GLA_DV = 128
D_GLA = N_HEADS * GLA_DV
GLA_RANK = 16
GLA_TAU = 16.0
GLA_CHUNK = 64
CMP_LEN = 32
SEL_LEN = 64
TOP_N = 16
WINDOW = 512
ROPE_THETA = 500000.0
ROPE_DIM = HEAD_DIM // 4
ROPE_HALF = ROPE_DIM // 2
D_FF = 3584
N_EXPERTS = 8
EPS = 1e-6
SEL_FORCE = 1e9
NEG = -1e30
SCALE = HEAD_DIM ** -0.5
LOG2E = 1.4426950408889634

LANES = 128
SUBLANES = 8
VMEM_LIMIT = 56 * 1024 * 1024

C_FQ = 0
C_FKV = 256
C_NQ = 768
C_NKV = 1024
C_GQK = 1408
C_GV = 1920
C_GOG = 2432
C_SMALL = 2944
C_END = 3072
SM_FF = 0
SM_NG = 4
SM_GLR = 16

NT = (((1,), (1,)), ((), ()))


def _params(sem):
    return pltpu.CompilerParams(dimension_semantics=sem, vmem_limit_bytes=VMEM_LIMIT)


def _rms(x, g):
    ms = jnp.mean(x * x, axis=-1, keepdims=True)
    return x * lax.rsqrt(ms + EPS) * g


def _sigmoid(x):
    return 1.0 / (1.0 + jnp.exp(-x))


def _log_sigmoid(x):
    return -(jnp.maximum(-x, 0.0) + jnp.log1p(jnp.exp(-jnp.abs(x))))


def _silu(x):
    return x * _sigmoid(x)


def _bdot(a, b):
    return jnp.dot(a.astype(BF16), b.astype(BF16), preferred_element_type=F32)


def _bdot_nt(a, b):
    return lax.dot_general(a.astype(BF16), b.astype(BF16), NT, preferred_element_type=F32)


def _split3(x):
    h = x.astype(BF16)
    r = x - h.astype(F32)
    m = r.astype(BF16)
    return h, m, (r - m.astype(F32)).astype(BF16)


def _dot01(m01, x):
    mb = m01.astype(BF16)
    h, m, l = _split3(x)
    return (jnp.dot(mb, h, preferred_element_type=F32) + jnp.dot(mb, m, preferred_element_type=F32)
            + jnp.dot(mb, l, preferred_element_type=F32))


def _dot01_r(x, m01):
    mb = m01.astype(BF16)
    h, m, l = _split3(x)
    return (jnp.dot(h, mb, preferred_element_type=F32) + jnp.dot(m, mb, preferred_element_type=F32)
            + jnp.dot(l, mb, preferred_element_type=F32))


def _dot01_nt(m01, x):
    mb = m01.astype(BF16)
    h, m, l = _split3(x)
    return (lax.dot_general(mb, h, NT, preferred_element_type=F32)
            + lax.dot_general(mb, m, NT, preferred_element_type=F32)
            + lax.dot_general(mb, l, NT, preferred_element_type=F32))


def _dot_nt_hilo(a, b):
    ah = a.astype(BF16)
    al = (a - ah.astype(F32)).astype(BF16)
    bh = b.astype(BF16)
    bl = (b - bh.astype(F32)).astype(BF16)
    return (lax.dot_general(ah, bh, NT, preferred_element_type=F32)
            + lax.dot_general(al, bh, NT, preferred_element_type=F32)
            + lax.dot_general(ah, bl, NT, preferred_element_type=F32))


def _wdot(a, w):
    if w.dtype == F32:
        return jnp.dot(a.astype(F32), w, precision=HI, preferred_element_type=F32)
    return jnp.dot(a.astype(BF16), w, preferred_element_type=F32)


def _rope128(x, a, bp, bm):
    return x * a + pltpu.roll(x, ROPE_HALF, 1) * bp + pltpu.roll(x, LANES - ROPE_HALF, 1) * bm


def _inproj_kernel(x_ref, g_ref, w_ref, sb_ref, wgk_ref, bgk_ref, tab_ref,
                   fq_ref, fkv_ref, nq_ref, nqr_ref, rows_ref, win_ref,
                   gqk_ref, gv_ref, gog_ref, glog_ref, small_ref):
    h = _rms(x_ref[...], g_ref[...]).astype(w_ref.dtype)

    def mm(a, b):
        return _wdot(h, w_ref[:, a:b])

    fq_ref[...] = mm(C_FQ, C_FKV)
    fkv_ref[...] = mm(C_FKV, C_NQ)
    tab = tab_ref[...]
    ab, pb, mb = tab[:, 0:128], tab[:, 128:256], tab[:, 256:384]
    af, pf, mf = tab[:, 384:512], tab[:, 512:640], tab[:, 640:768]
    nq = mm(C_NQ, C_NKV)
    nq_ref[...] = nq
    nqr_ref[:, 0:128] = _rope128(nq[:, 0:128], ab, pb, mb)
    nqr_ref[:, 128:256] = _rope128(nq[:, 128:256], ab, pb, mb)
    nkv = mm(C_NKV, C_GQK)
    rows_ref[:, 0:128] = nkv[:, 0:128]
    rows_ref[:, 128:256] = _rope128(nkv[:, 128:256], af, pf, mf)
    win_ref[...] = _rope128(nkv[:, 256:384], af, pf, mf)
    gqk_ref[...] = mm(C_GQK, C_GV)
    gv_ref[...] = mm(C_GV, C_GOG)
    gog_ref[...] = mm(C_GOG, C_SMALL)
    sm = mm(C_SMALL, C_END)
    glog_ref[...] = _log_sigmoid(_wdot(sm, wgk_ref[...]) + bgk_ref[...]) * (1.0 / GLA_TAU)
    smb = sm + sb_ref[...]
    lane = lax.broadcasted_iota(I32, smb.shape, 1)
    small_ref[...] = jnp.where(lane < SM_NG, _log_sigmoid(smb), _sigmoid(smb))


def _in_projection(x_all, row0, n_rows, tm, g, w_r, sb, wgk, bgk, tab, tab_period):
    assert n_rows % tm == 0 and row0 % tm == 0 and tab_period % tm == 0
    nt = n_rows // tm
    b0 = row0 // tm
    npd = tab_period // tm
    widths = (256, 512, 256, 256, 256, 128, 512, 512, 512, 256, 128)
    full = lambda shape: pl.BlockSpec(shape, lambda i: (0, 0))
    return pl.pallas_call(
        _inproj_kernel,
        grid=(nt,),
        in_specs=[
            pl.BlockSpec((tm, D_MODEL), lambda i: (b0 + i, 0)),
            full((1, D_MODEL)),
            full((D_MODEL, C_END)),
            full((1, LANES)),
            full((LANES, 256)),
            full((1, 256)),
            pl.BlockSpec((tm, 768), lambda i: (i % npd, 0)),
        ],
        out_specs=[pl.BlockSpec((tm, w), lambda i: (i, 0)) for w in widths],
        out_shape=[jax.ShapeDtypeStruct((n_rows, w), F32) for w in widths],
        compiler_params=_params(("parallel",)),
        name="in_projection",
    )(x_all, g, w_r, sb, wgk, bgk, tab)


def _cumsum_kernel(sm_ref, cr_ref, carry):
    t = pl.program_id(1)
    ts = sm_ref.shape[0]

    @pl.when(t == 0)
    def _():
        carry[...] = jnp.zeros_like(carry)

    r = lax.broadcasted_iota(I32, (ts, ts), 0)
    c = lax.broadcasted_iota(I32, (ts, ts), 1)
    cs = _dot01(c <= r, sm_ref[...]) + carry[...]
    carry[...] = cs[ts - 1:ts, :]
    cr_ref[...] = cs.T[0:SUBLANES, :] * LOG2E


def _fox_cumsum(small, B, S, ts):
    ns = S // ts
    return pl.pallas_call(
        _cumsum_kernel,
        grid=(B, ns),
        in_specs=[pl.BlockSpec((ts, LANES), lambda b, t: (b * ns + t, 0))],
        out_specs=pl.BlockSpec((None, SUBLANES, ts), lambda b, t: (b, 0, t)),
        out_shape=jax.ShapeDtypeStruct((B, SUBLANES, S), F32),
        scratch_shapes=[pltpu.VMEM((1, LANES), F32)],
        compiler_params=_params(("parallel", "arbitrary")),
        name="fox_cumsum",
    )(small)


def _pair_mask(shape, h):
    return (lax.broadcasted_iota(I32, shape, 1) // HEAD_DIM) == (h % 2)


def _fox_prompt_kernel(q_ref, kv_ref, cr_ref, o_ref, *scratch):
    i = pl.program_id(1)
    j = pl.program_id(2)
    nk = pl.num_programs(2)
    tq = q_ref.shape[0]
    tk = kv_ref.shape[0]
    q_sc, m_sc, acc_sc = scratch[0:N_HEADS], scratch[N_HEADS:2 * N_HEADS], scratch[2 * N_HEADS:]

    @pl.when(j == 0)
    def _():
        for h in range(N_HEADS):
            m_sc[h][...] = jnp.full_like(m_sc[h], NEG)
            acc_sc[h][...] = jnp.zeros_like(acc_sc[h])
            slab = q_ref[:, (h // 2) * LANES:(h // 2 + 1) * LANES] * (SCALE * LOG2E)
            q_sc[h][...] = jnp.where(_pair_mask(slab.shape, h), slab, 0.0).astype(BF16)

    def tile(diagonal):
        if diagonal:
            mask = lax.broadcasted_iota(I32, (1, tk), 1) <= lax.broadcasted_iota(I32, (tq, 1), 0)
        for h in range(N_HEADS):
            c0 = (h // 2) * LANES
            k_slab = kv_ref[:, c0:c0 + LANES].astype(BF16)
            s = lax.dot_general(q_sc[h][...], k_slab, NT, preferred_element_type=F32) - cr_ref[h:h + 1, :]
            if diagonal:
                s = jnp.where(mask, s, NEG)
            m_old = m_sc[h][...]
            m_new = jnp.maximum(m_old, jnp.max(s, axis=-1, keepdims=True))
            p = jnp.exp2(s - m_new).astype(BF16)
            v_slab = kv_ref[:, D_FOX + c0:D_FOX + c0 + LANES]
            v_aug = jnp.where(_pair_mask(v_slab.shape, h), v_slab, 1.0).astype(BF16)
            acc_sc[h][...] = (jnp.exp2(m_old - m_new) * acc_sc[h][...]
                              + jnp.dot(p, v_aug, preferred_element_type=F32))
            m_sc[h][...] = m_new

    @pl.when(j < i)
    def _():
        tile(False)

    @pl.when(j == i)
    def _():
        tile(True)

    @pl.when(j == nk - 1)
    def _():
        for h in range(N_HEADS):
            a = acc_sc[h][...]
            lo = (h % 2) * HEAD_DIM
            den = a[:, HEAD_DIM - lo:HEAD_DIM - lo + 1]
            o_ref[:, h * HEAD_DIM:(h + 1) * HEAD_DIM] = a[:, lo:lo + HEAD_DIM] / den


def _fox_prompt(fq, fkv, cr, B, S, t):
    n = S // t
    return pl.pallas_call(
        _fox_prompt_kernel,
        grid=(B, n, n),
        in_specs=[
            pl.BlockSpec((t, D_FOX), lambda b, i, j: (b * n + i, 0)),
            pl.BlockSpec((t, 2 * D_FOX), lambda b, i, j: (b * n + jnp.minimum(i, j), 0)),
            pl.BlockSpec((None, SUBLANES, t), lambda b, i, j: (b, 0, jnp.minimum(i, j))),
        ],
        out_specs=pl.BlockSpec((t, D_FOX), lambda b, i, j: (b * n + i, 0)),
        out_shape=jax.ShapeDtypeStruct((B * S, D_FOX), F32),
        scratch_shapes=([pltpu.VMEM((t, LANES), BF16)] * N_HEADS + [pltpu.VMEM((t, 1), F32)] * N_HEADS
                        + [pltpu.VMEM((t, LANES), F32)] * N_HEADS),
        compiler_params=_params(("parallel", "parallel", "arbitrary")),
        name="fox_prompt",
    )(fq, fkv, cr)


def _compress_kernel(x_ref, pe_ref, w_ref, o_ref, *, exact):
    for s in range(2):
        x = x_ref[s] + pe_ref[s]
        w = w_ref[s]
        if exact:
            y = jnp.dot(x, w, precision=HI, preferred_element_type=F32)
        else:
            xh = x.astype(BF16)
            xl = (x - xh.astype(F32)).astype(BF16)
            wh = w.astype(BF16)
            wl = (w - wh.astype(F32)).astype(BF16)
            y = (jnp.dot(xh, wh, preferred_element_type=F32) + jnp.dot(xl, wh, preferred_element_type=F32)
                 + jnp.dot(xh, wl, preferred_element_type=F32))
        o_ref[:, s * HEAD_DIM:(s + 1) * HEAD_DIM] = y


def _compress(x3, pe, w, tr, exact, row0=0, n_rows=None):
    K = x3.shape[2]
    R = x3.shape[1] if n_rows is None else n_rows
    assert R % tr == 0 and row0 % tr == 0
    b0 = row0 // tr
    return pl.pallas_call(
        functools.partial(_compress_kernel, exact=exact),
        grid=(R // tr,),
        in_specs=[pl.BlockSpec((2, tr, K), lambda i: (0, b0 + i, 0)),
                  pl.BlockSpec((2, 1, K), lambda i: (0, 0, 0)),
                  pl.BlockSpec((2, K, HEAD_DIM), lambda i: (0, 0, 0))],
        out_specs=pl.BlockSpec((tr, LANES), lambda i: (i, 0)),
        out_shape=jax.ShapeDtypeStruct((R, LANES), F32),
        compiler_params=_params(("parallel",)),
        name="nsa_compress",
    )(x3, pe, w)


def _block_major(kv, n_blocks):
    return kv.reshape(n_blocks, CMP_LEN, 2, HEAD_DIM).transpose(2, 0, 1, 3).reshape(
        2, n_blocks, CMP_LEN * HEAD_DIM)


def _order_key(x):
    b = lax.bitcast_convert_type(x, I32)
    return jnp.where(b < 0, b ^ jnp.int32(0x7FFFFFFF), b)


def _nsa_prompt_kernel(nq_ref, nqr_ref, sm_ref, cmp_ref, rows_ref, win_ref, o_ref, qc_sc, qx_sc, *, tk):
    QB = nq_ref.shape[0]
    S = rows_ref.shape[0]
    nb = cmp_ref.shape[0]
    nsel = S // SEL_LEN
    i = pl.program_id(1)
    qs = i * QB
    qpos = qs + lax.broadcasted_iota(I32, (QB, 1), 0)

    HQ = N_HEADS * QB
    lo_half = lax.broadcasted_iota(I32, (QB, LANES), 1) < HEAD_DIM

    def stack_heads(ref, scale, dst):
        for h in range(N_HEADS):
            slab = ref[:, (h // 2) * LANES:(h // 2 + 1) * LANES] * scale
            if h % 2:
                slab = pltpu.roll(slab, HEAD_DIM, 1)
            dst[h * QB:(h + 1) * QB, :] = jnp.where(lo_half, slab, 0.0).astype(dst.dtype)

    cmp = cmp_ref[...]
    n_l = lax.broadcasted_iota(I32, (1, nb), 1)
    complete = ((n_l + 1) * CMP_LEN - 1) <= qpos
    stack_heads(nq_ref, SCALE, qc_sc)
    s = _dot_nt_hilo(qc_sc[...], cmp).reshape(N_HEADS, QB, nb)
    s = jnp.where(complete[None], s, NEG)
    e = jnp.exp(s - jnp.max(s, axis=-1, keepdims=True))
    p = e / jnp.sum(e, axis=-1, keepdims=True) * complete.astype(F32)[None]
    o_cmp = _bdot(p.reshape(HQ, nb), cmp)
    psum = jnp.sum(p, axis=0)

    pj = lax.broadcasted_iota(I32, (nsel, nb), 0)
    pn = lax.broadcasted_iota(I32, (nsel, nb), 1)
    imp_t = _dot01_nt(pn // (SEL_LEN // CMP_LEN) == pj, psum)
    jt = (qs + lax.broadcasted_iota(I32, (1, QB), 1)) // SEL_LEN
    jj = lax.broadcasted_iota(I32, (nsel, 1), 0)
    score = jnp.where(jj == jt, 2.0 * SEL_FORCE,
                      jnp.where((jj == 0) | (jj == jt - 1), SEL_FORCE,
                                jnp.where(jj <= jt, imp_t + 0.0, -1.0)))
    key = _order_key(score)
    key_m1 = key - 1
    ngrp = nsel // SUBLANES
    sub = lax.broadcasted_iota(I32, (SUBLANES, QB), 0)
    kg = [key[r * SUBLANES:(r + 1) * SUBLANES, :] for r in range(ngrp)]
    kg1 = [key_m1[r * SUBLANES:(r + 1) * SUBLANES, :] for r in range(ngrp)]
    cnt = [jnp.zeros((SUBLANES, QB), I32) for _ in range(ngrp)]
    for jp in range(nsel):
        g = jp // SUBLANES
        row = key[jp:jp + 1, :]
        mixed = jnp.where(sub > (jp % SUBLANES), kg1[g], kg[g])
        for r in range(ngrp):
            thr = kg[r] if r < g else (kg1[r] if r > g else mixed)
            cnt[r] = cnt[r] + (row > thr).astype(I32)
    sel_t = jnp.concatenate([(c < TOP_N).astype(F32) for c in cnt], axis=0)
    if nsel < QB:
        sel_t = jnp.concatenate([sel_t, jnp.zeros((QB - nsel, QB), F32)], axis=0)
    sel = sel_t.T.astype(BF16)

    stack_heads(nqr_ref, SCALE * LOG2E, qx_sc)
    qx = qx_sc[...]

    def attend(s, valid, slab, m_old, acc_old):
        n = s.shape[1]
        s = jnp.where(valid[None], s.reshape(N_HEADS, QB, n), NEG).reshape(HQ, n)
        m_new = jnp.maximum(m_old, jnp.max(s, axis=-1, keepdims=True))
        p = jnp.exp2(s - m_new).astype(BF16)
        ones_k = lax.broadcasted_iota(I32, slab.shape, 1) < HEAD_DIM
        v_aug = jnp.where(ones_k, 1.0, slab).astype(BF16)
        acc = jnp.exp2(m_old - m_new) * acc_old + jnp.dot(p, v_aug, preferred_element_type=F32)
        return m_new, acc

    jrow = lax.broadcasted_iota(I32, (QB, 1), 0)

    def sel_tile(k0, m_old, acc_old, diagonal):
        kpos = k0 + lax.broadcasted_iota(I32, (1, tk), 1)
        expand = (jrow == kpos // SEL_LEN).astype(BF16)
        valid = jnp.dot(sel, expand, preferred_element_type=F32) > 0.5
        if diagonal:
            valid = valid & (kpos <= qpos)
        slab = rows_ref[pl.ds(k0, tk), 2 * HEAD_DIM:4 * HEAD_DIM]
        s = lax.dot_general(qx, slab.astype(BF16), NT, preferred_element_type=F32)
        return attend(s, valid, slab, m_old, acc_old)

    n_full = qs // tk
    init = (jnp.full((HQ, 1), NEG, F32), jnp.zeros((HQ, LANES), F32))
    m_s, acc_s = lax.fori_loop(
        0, n_full, lambda t, c: sel_tile(pl.multiple_of(t * tk, tk), c[0], c[1], False), init)
    _, acc_s = sel_tile(pl.multiple_of(n_full * tk, tk), m_s, acc_s, True)

    wlen = WINDOW + QB
    w0 = pl.multiple_of(jnp.maximum(qs - WINDOW, 0), QB)
    wpos = w0 + lax.broadcasted_iota(I32, (1, wlen), 1)
    d = qpos - wpos
    wslab = win_ref[pl.ds(w0, wlen), :]
    sw = lax.dot_general(qx, wslab.astype(BF16), NT, preferred_element_type=F32)
    _, acc_w = attend(sw, (d >= 0) & (d < WINDOW), wslab,
                      jnp.full((HQ, 1), NEG, F32), jnp.zeros((HQ, LANES), F32))

    sm = sm_ref[...]
    for h in range(N_HEADS):
        rs = slice(h * QB, (h + 1) * QB)
        o_sel = acc_s[rs] * (1.0 / acc_s[rs, 0:1])
        o_win = acc_w[rs] * (1.0 / acc_w[rs, 0:1])
        c = SM_NG + 3 * h
        mix = sm[:, c:c + 1] * o_cmp[rs] + sm[:, c + 1:c + 2] * o_sel + sm[:, c + 2:c + 3] * o_win
        if h % 2 == 0:
            mix = pltpu.roll(mix, HEAD_DIM, 1)
        lo = (h % 2) * HEAD_DIM
        o_ref[:, h * HEAD_DIM:(h + 1) * HEAD_DIM] = mix[:, lo:lo + HEAD_DIM]


def _nsa_prompt(nq, nqr, small, cmp, rows, win, B, S, tk):
    QB = 128
    nq_t = S // QB
    nb = S // CMP_LEN
    assert S % tk == 0 and S >= WINDOW + QB
    return pl.pallas_call(
        functools.partial(_nsa_prompt_kernel, tk=tk),
        grid=(B, nq_t),
        in_specs=[
            pl.BlockSpec((QB, D_NSA), lambda b, i: (b * nq_t + i, 0)),
            pl.BlockSpec((QB, D_NSA), lambda b, i: (b * nq_t + i, 0)),
            pl.BlockSpec((QB, LANES), lambda b, i: (b * nq_t + i, 0)),
            pl.BlockSpec((nb, LANES), lambda b, i: (b, 0)),
            pl.BlockSpec((S, 4 * HEAD_DIM), lambda b, i: (b, 0)),
            pl.BlockSpec((S, 2 * HEAD_DIM), lambda b, i: (b, 0)),
        ],
        out_specs=pl.BlockSpec((QB, D_NSA), lambda b, i: (b * nq_t + i, 0)),
        out_shape=jax.ShapeDtypeStruct((B * S, D_NSA), F32),
        scratch_shapes=[pltpu.VMEM((N_HEADS * QB, LANES), F32), pltpu.VMEM((N_HEADS * QB, LANES), BF16)],
        compiler_params=_params(("parallel", "parallel")),
        name="nsa_prompt",
    )(nq, nqr, small, cmp, rows, win)


def _gla_prompt_kernel(qk_ref, v_ref, g_ref, o_ref, st_ref, s_sc):
    t = pl.program_id(1)
    nt = pl.num_programs(1)
    tc = qk_ref.shape[0]
    C = GLA_CHUNK

    @pl.when(t == 0)
    def _():
        s_sc[...] = jnp.zeros_like(s_sc)

    r = lax.broadcasted_iota(I32, (tc, tc), 0)
    c = lax.broadcasted_iota(I32, (tc, tc), 1)
    same = (r // C) == (c // C)
    causal = same & (c <= r)
    g = g_ref[...]
    gcum = _dot01(causal, g)
    g_t = g.T
    gcum_t = _dot01_r(g_t, same & (r <= c))
    gtot_t = _dot01_r(g_t, same)
    q_e = (qk_ref[:, 0:D_FOX] * SCALE * jnp.exp(gcum)).astype(BF16)
    k_e = (qk_ref[:, D_FOX:2 * D_FOX] * jnp.exp(-gcum)).astype(BF16)
    kd_t = (qk_ref[:, D_FOX:2 * D_FOX].T * jnp.exp(gtot_t - gcum_t)).astype(BF16)
    decay_t = jnp.exp(gtot_t)
    for h in range(N_HEADS):
        hs = slice(h * HEAD_DIM, (h + 1) * HEAD_DIM)
        v = v_ref[:, h * GLA_DV:(h + 1) * GLA_DV].astype(BF16)
        a = jnp.where(causal, lax.dot_general(q_e[:, hs], k_e[:, hs], NT, preferred_element_type=F32), 0.0)
        o_intra = jnp.dot(a.astype(BF16), v, preferred_element_type=F32)
        state = s_sc[h]
        for ci in range(tc // C):
            rs = slice(ci * C, (ci + 1) * C)
            o_ref[rs, h * GLA_DV:(h + 1) * GLA_DV] = (
                o_intra[rs] + jnp.dot(q_e[rs, hs], state.astype(BF16), preferred_element_type=F32))
            state = (decay_t[hs, ci * C:ci * C + 1] * state
                     + jnp.dot(kd_t[hs, rs], v[rs], preferred_element_type=F32))
        s_sc[h] = state

    @pl.when(t == nt - 1)
    def _():
        st_ref[...] = s_sc[...]


def _gla_prompt(gqk, gv, glog, B, S, tc):
    nt = S // tc
    return pl.pallas_call(
        _gla_prompt_kernel,
        grid=(B, nt),
        in_specs=[pl.BlockSpec((tc, 2 * D_FOX), lambda b, t: (b * nt + t, 0)),
                  pl.BlockSpec((tc, D_GLA), lambda b, t: (b * nt + t, 0)),
                  pl.BlockSpec((tc, D_FOX), lambda b, t: (b * nt + t, 0))],
        out_specs=[pl.BlockSpec((tc, D_GLA), lambda b, t: (b * nt + t, 0)),
                   pl.BlockSpec((None, N_HEADS, HEAD_DIM, GLA_DV), lambda b, t: (b, 0, 0, 0))],
        out_shape=[jax.ShapeDtypeStruct((B * S, D_GLA), F32),
                   jax.ShapeDtypeStruct((B, N_HEADS, HEAD_DIM, GLA_DV), F32)],
        scratch_shapes=[pltpu.VMEM((N_HEADS, HEAD_DIM, GLA_DV), F32)],
        compiler_params=_params(("parallel", "arbitrary")),
        name="gla_prompt",
    )(gqk, gv, glog)


def _outproj_kernel(x_ref, of_ref, on_ref, og_ref, gog_ref, gn_ref, w_ref, o_ref):
    acc = _wdot(of_ref[...], w_ref[0:D_FOX, :])
    acc = acc + _wdot(on_ref[...], w_ref[D_FOX:D_FOX + D_NSA, :])
    for h in range(N_HEADS):
        hs = slice(h * GLA_DV, (h + 1) * GLA_DV)
        z = _rms(og_ref[:, hs], gn_ref[...]) * _silu(gog_ref[:, hs])
        w0 = D_FOX + D_NSA + h * GLA_DV
        acc = acc + _wdot(z, w_ref[w0:w0 + GLA_DV, :])
    o_ref[...] = x_ref[...] + acc


def _out_projection(x, o_fox, o_nsa, o_gla, gog, gn, w_out, tm):
    T = x.shape[0]
    assert T % tm == 0
    row = lambda w: pl.BlockSpec((tm, w), lambda i: (i, 0))
    return pl.pallas_call(
        _outproj_kernel,
        grid=(T // tm,),
        in_specs=[row(D_MODEL), row(D_FOX), row(D_NSA), row(D_GLA), row(D_GLA),
                  pl.BlockSpec((1, GLA_DV), lambda i: (0, 0)),
                  pl.BlockSpec((D_MODEL, D_MODEL), lambda i: (0, 0))],
        out_specs=row(D_MODEL),
        out_shape=jax.ShapeDtypeStruct((T, D_MODEL), F32),
        compiler_params=_params(("parallel",)),
        name="out_projection",
    )(x, o_fox, o_nsa, o_gla, gog, gn, w_out)


def _dense_ffn_kernel(x_ref, g_ref, wg_ref, wu_ref, wd_ref, o_ref, h_sc, acc_sc):
    f = pl.program_id(1)
    nf = pl.num_programs(1)

    @pl.when(f == 0)
    def _():
        h_sc[...] = _rms(x_ref[...], g_ref[...]).astype(h_sc.dtype)
        acc_sc[...] = jnp.zeros_like(acc_sc)

    h = h_sc[...]
    a = _wdot(h, wg_ref[...])
    u = _wdot(h, wu_ref[...])
    acc_sc[...] += _wdot(_silu(a) * u, wd_ref[...])

    @pl.when(f == nf - 1)
    def _():
        o_ref[...] = x_ref[...] + acc_sc[...]


def _dense_ffn(x, g, wg, wu, wd, tm, tf):
    T = x.shape[0]
    assert T % tm == 0 and D_FF % tf == 0
    return pl.pallas_call(
        _dense_ffn_kernel,
        grid=(T // tm, D_FF // tf),
        in_specs=[pl.BlockSpec((tm, D_MODEL), lambda i, f: (i, 0)),
                  pl.BlockSpec((1, D_MODEL), lambda i, f: (0, 0)),
                  pl.BlockSpec((D_MODEL, tf), lambda i, f: (0, f)),
                  pl.BlockSpec((D_MODEL, tf), lambda i, f: (0, f)),
                  pl.BlockSpec((tf, D_MODEL), lambda i, f: (f, 0))],
        out_specs=pl.BlockSpec((tm, D_MODEL), lambda i, f: (i, 0)),
        out_shape=jax.ShapeDtypeStruct((T, D_MODEL), F32),
        scratch_shapes=[pltpu.VMEM((tm, D_MODEL), wg.dtype), pltpu.VMEM((tm, D_MODEL), F32)],
        compiler_params=_params(("parallel", "arbitrary")),
        name="dense_ffn",
    )(x, g, wg, wu, wd)


def _router_kernel(x_ref, g_ref, wr_ref, h_ref, r_ref):
    h = _rms(x_ref[...], g_ref[...])
    h_ref[...] = h
    logits = jnp.dot(h, wr_ref[...], precision=HI, preferred_element_type=F32)
    lane = lax.broadcasted_iota(I32, logits.shape, 1)
    lg = jnp.where(lane < N_EXPERTS, logits, -jnp.inf)
    m1 = jnp.max(lg, axis=-1, keepdims=True)
    i1 = jnp.min(jnp.where(lg == m1, lane, LANES), axis=-1, keepdims=True)
    lg2 = jnp.where(lane == i1, -jnp.inf, lg)
    m2 = jnp.max(lg2, axis=-1, keepdims=True)
    i2 = jnp.min(jnp.where(lg2 == m2, lane, LANES), axis=-1, keepdims=True)
    e = jnp.exp(m2 - m1)
    den = 1.0 + e
    r_ref[...] = jnp.where(lane == 0, i1.astype(F32),
                           jnp.where(lane == 1, i2.astype(F32),
                                     jnp.where(lane == 2, 1.0 / den,
                                               jnp.where(lane == 3, e / den, 0.0))))


def _router(x, g, wr_pad, tm):
    T = x.shape[0]
    assert T % tm == 0
    return pl.pallas_call(
        _router_kernel,
        grid=(T // tm,),
        in_specs=[pl.BlockSpec((tm, D_MODEL), lambda i: (i, 0)),
                  pl.BlockSpec((1, D_MODEL), lambda i: (0, 0)),
                  pl.BlockSpec((D_MODEL, LANES), lambda i: (0, 0))],
        out_specs=[pl.BlockSpec((tm, D_MODEL), lambda i: (i, 0)),
                   pl.BlockSpec((tm, LANES), lambda i: (i, 0))],
        out_shape=[jax.ShapeDtypeStruct((T, D_MODEL), F32),
                   jax.ShapeDtypeStruct((T, LANES), F32)],
        compiler_params=_params(("parallel",)),
        name="moe_router",
    )(x, g, wr_pad)


GATHER_WINDOW = 32
SC_WORKERS = 32


def _row_gather(src, idx):
    n = idx.shape[0]
    step = GATHER_WINDOW * SC_WORKERS
    n_pad = -(-n // step) * step
    if n_pad != n:
        idx = jnp.concatenate([idx, jnp.zeros((n_pad - n,), idx.dtype)])
    width = src.shape[1]
    per_worker = n_pad // SC_WORKERS
    mesh = plsc.VectorSubcoreMesh(core_axis_name="core", subcore_axis_name="subcore")

    @functools.partial(pl.kernel, out_type=jax.ShapeDtypeStruct((n_pad, width), src.dtype), mesh=mesh,
                       scratch_types=[pltpu.VMEM((per_worker,), I32),
                                      pltpu.VMEM((GATHER_WINDOW, width), src.dtype)],
                       name="row_gather")
    def gather(src_hbm, idx_hbm, dst_hbm, idx_v, buf):
        worker = lax.axis_index("core") * (SC_WORKERS // 2) + lax.axis_index("subcore")
        base = worker * per_worker
        pltpu.sync_copy(idx_hbm.at[pl.ds(base, per_worker)], idx_v)

        @pl.loop(0, per_worker // GATHER_WINDOW)
        def _(j):
            pltpu.sync_copy(src_hbm.at[idx_v.at[pl.ds(j * GATHER_WINDOW, GATHER_WINDOW)]], buf)
            pltpu.sync_copy(buf, dst_hbm.at[pl.ds(base + j * GATHER_WINDOW, GATHER_WINDOW)])

    return gather(src, idx)


def _moe_ffn_kernel(be_ref, nu_ref, x_ref, wg_ref, wu_ref, wd_ref, o_ref, acc_sc):
    b = pl.program_id(0)
    f = pl.program_id(1)
    nf = pl.num_programs(1)
    used = b < nu_ref[0]

    @pl.when(used)
    def _():
        @pl.when(f == 0)
        def _():
            acc_sc[...] = jnp.zeros_like(acc_sc)

        x = x_ref[...].astype(BF16)
        a = jnp.dot(x, wg_ref[...].astype(BF16), preferred_element_type=F32)
        u = jnp.dot(x, wu_ref[...].astype(BF16), preferred_element_type=F32)
        acc_sc[...] += _bdot(_silu(a) * u, wd_ref[...])

        @pl.when(f == nf - 1)
        def _():
            o_ref[...] = acc_sc[...]

    @pl.when(jnp.logical_not(used) & (f == nf - 1))
    def _():
        o_ref[...] = jnp.zeros_like(o_ref)


def _moe_ffn(xb, block_e, n_used, wg, wu, wd, blk, tf):
    cap = xb.shape[0]
    nb = cap // blk
    nf = D_FF // tf

    def bsel(b, nu):
        return jnp.minimum(b, nu[0] - 1)

    def fsel(b, f, nu):
        return jnp.where(b < nu[0], f, nf - 1)

    return pl.pallas_call(
        _moe_ffn_kernel,
        grid_spec=pltpu.PrefetchScalarGridSpec(
            num_scalar_prefetch=2,
            grid=(nb, nf),
            in_specs=[
                pl.BlockSpec((blk, D_MODEL), lambda b, f, be, nu: (bsel(b, nu), 0)),
                pl.BlockSpec((None, D_MODEL, tf), lambda b, f, be, nu: (be[bsel(b, nu)], 0, fsel(b, f, nu))),
                pl.BlockSpec((None, D_MODEL, tf), lambda b, f, be, nu: (be[bsel(b, nu)], 0, fsel(b, f, nu))),
                pl.BlockSpec((None, tf, D_MODEL), lambda b, f, be, nu: (be[bsel(b, nu)], fsel(b, f, nu), 0)),
            ],
            out_specs=pl.BlockSpec((blk, D_MODEL), lambda b, f, be, nu: (b, 0)),
            scratch_shapes=[pltpu.VMEM((blk, D_MODEL), F32)],
        ),
        out_shape=jax.ShapeDtypeStruct((cap, D_MODEL), F32),
        compiler_params=_params(("arbitrary", "arbitrary")),
        name="moe_ffn",
    )(block_e, n_used, xb, wg, wu, wd)


def _moe_combine_kernel(x_ref, y1_ref, y2_ref, r_ref, g_ref, o_ref, *, final):
    r = r_ref[...]
    y = x_ref[...] + (r[:, 2:3] * y1_ref[...] + r[:, 3:4] * y2_ref[...])
    o_ref[...] = _rms(y, g_ref[...]) if final else y


def _moe_combine(x, y1, y2, route, tm, final_g):
    T = x.shape[0]
    g = jnp.ones((1, D_MODEL), F32) if final_g is None else final_g
    return pl.pallas_call(
        functools.partial(_moe_combine_kernel, final=final_g is not None),
        grid=(T // tm,),
        in_specs=[pl.BlockSpec((tm, D_MODEL), lambda i: (i, 0)),
                  pl.BlockSpec((tm, D_MODEL), lambda i: (i, 0)),
                  pl.BlockSpec((tm, D_MODEL), lambda i: (i, 0)),
                  pl.BlockSpec((tm, LANES), lambda i: (i, 0)),
                  pl.BlockSpec((1, D_MODEL), lambda i: (0, 0))],
        out_specs=pl.BlockSpec((tm, D_MODEL), lambda i: (i, 0)),
        out_shape=jax.ShapeDtypeStruct((T, D_MODEL), F32),
        compiler_params=_params(("parallel",)),
        name="moe_combine",
    )(x, y1, y2, route, g)


def _moe_plan(e_top, blk):
    T = e_top.shape[0]
    n = 2 * T
    flat_e = e_top.reshape(-1)
    onehot = (flat_e[:, None] == jnp.arange(N_EXPERTS, dtype=I32)[None, :]).astype(I32)
    csum = jnp.cumsum(onehot, axis=0)
    rank = jnp.sum((csum - onehot) * onehot, axis=1)
    counts = csum[-1]
    padded = (counts + blk - 1) // blk * blk
    ends = jnp.cumsum(padded)
    pstart = ends - padded
    dest = (pstart[flat_e] + rank).astype(I32)
    n_blocks = -(-n // blk) + N_EXPERTS
    cap = n_blocks * blk
    slot_tok = (jnp.arange(cap, dtype=I32) % T).at[dest].set(jnp.arange(n, dtype=I32) // 2)
    first = jnp.arange(n_blocks, dtype=I32) * blk
    block_e = jnp.minimum(jnp.sum((ends[None, :] <= first[:, None]).astype(I32), axis=1), N_EXPERTS - 1)
    n_used = (ends[-1] // blk).astype(I32).reshape(1)
    return dest, slot_tok, block_e, n_used


def _moe_dispatch(x, g, wr_pad, tm, blk):
    h, route = _router(x, g, wr_pad, tm)
    dest, slot_tok, block_e, n_used = _moe_plan(route[:, 0:2].astype(I32), blk)
    return dict(x=x, route=route, dest=dest, xb=_row_gather(h, slot_tok), block_e=block_e, n_used=n_used)


def _moe_finish(d, wg, wu, wd, tm, blk, tf, final_g):
    yb = _moe_ffn(d['xb'], d['block_e'], d['n_used'], wg, wu, wd, blk, tf)
    d2 = d['dest'].reshape(-1, 2)
    return _moe_combine(d['x'], _row_gather(yb, d2[:, 0]), _row_gather(yb, d2[:, 1]), d['route'], tm, final_g)


def _norm_kernel(x_ref, g_ref, o_ref):
    o_ref[...] = _rms(x_ref[...], g_ref[...])


def _final_norm(x, g, tm):
    T = x.shape[0]
    return pl.pallas_call(
        _norm_kernel,
        grid=(T // tm,),
        in_specs=[pl.BlockSpec((tm, D_MODEL), lambda i: (i, 0)),
                  pl.BlockSpec((1, D_MODEL), lambda i: (0, 0))],
        out_specs=pl.BlockSpec((tm, D_MODEL), lambda i: (i, 0)),
        out_shape=jax.ShapeDtypeStruct((T, D_MODEL), F32),
        compiler_params=_params(("parallel",)),
        name="final_norm",
    )(x, g)


_IN_SPLITS = (D_FOX, D_FOX, D_FOX, N_HEADS, D_NSA, 6 * HEAD_DIM, 3 * N_HEADS,
              D_FOX, D_FOX, D_GLA, GLA_RANK, D_GLA)


def _reorder_w_in(w):
    offs = [0]
    for s in _IN_SPLITS:
        offs.append(offs[-1] + s)
    seg = lambda k: w[:, offs[k]:offs[k + 1]]
    fq, fk, fv, ff, nq, nkv, ng, gq, gk, gv, glr, gog = [seg(k) for k in range(12)]
    pad = jnp.zeros((w.shape[0], LANES - SM_GLR - GLA_RANK), w.dtype)
    return jnp.concatenate([fq, fk, fv, nq, nkv, gq, gk, gv, gog, ff, ng, glr, pad], axis=1)


def _rope_table(pos):
    inv = ROPE_THETA ** (-jnp.arange(ROPE_HALF, dtype=F32) / ROPE_HALF)
    ang = pos.astype(F32)[:, None] * inv[None, :]
    cos, sin = jnp.cos(ang), jnp.sin(ang)
    P = pos.shape[0]
    one = jnp.ones((P, HEAD_DIM - ROPE_DIM), F32)
    zero = jnp.zeros((P, HEAD_DIM - ROPE_DIM), F32)
    z8 = jnp.zeros((P, ROPE_HALF), F32)
    a64 = jnp.concatenate([cos, cos, one], axis=1)
    p64 = jnp.concatenate([z8, sin, zero], axis=1)
    m64 = jnp.concatenate([-sin, z8, zero], axis=1)
    i64 = jnp.ones((P, HEAD_DIM), F32)
    o64 = jnp.zeros((P, HEAD_DIM), F32)
    return jnp.concatenate([a64, a64, p64, p64, m64, m64, a64, i64, p64, o64, m64, o64], axis=1)


def _layer_mix_params(l, norm_mix_g, w_in, b_fox_f, w_cmp, pe_cmp, w_gla_gk, b_gla_gk, g_gla_norm, w_out):
    sb = jnp.zeros((1, LANES), F32).at[0, SM_FF:SM_FF + N_HEADS].set(b_fox_f[l])
    wgk = jnp.zeros((LANES, D_FOX), F32).at[SM_GLR:SM_GLR + GLA_RANK].set(w_gla_gk[l])
    w_r = _reorder_w_in(w_in[l])
    return dict(g=norm_mix_g[l].reshape(1, D_MODEL), w_r=w_r.astype(BF16), w_r32=w_r, sb=sb,
                wgk=wgk.astype(BF16), wgk32=wgk, w_out32=w_out[l],
                bgk=b_gla_gk[l].reshape(1, D_FOX), w_cmp=w_cmp[l],
                pe_cmp=pe_cmp[l].reshape(2, 1, CMP_LEN * HEAD_DIM),
                gn=g_gla_norm[l].reshape(1, GLA_DV), w_out=w_out[l].astype(BF16))


def _mix_prompt(x, B, S, p, tab, tm, t_fox, tk_sel, tc_gla):
    (fq, fkv, nq, nqr, rows, win, gqk, gv, gog, glog, small) = _in_projection(
        x, 0, B * S, tm, p['g'], p['w_r'], p['sb'], p['wgk'], p['bgk'], tab, S)
    cr = _fox_cumsum(small, B, S, min(S, 512))
    o_fox = _fox_prompt(fq, fkv, cr, B, S, t_fox)
    n_blk = B * S // CMP_LEN
    blocks = _block_major(rows[:, 0:2 * HEAD_DIM].reshape(B * S, 2, HEAD_DIM), n_blk)
    cmp = _compress(blocks, p['pe_cmp'], p['w_cmp'], min(256, n_blk), True)
    o_nsa = _nsa_prompt(nq, nqr, small, cmp, rows, win, B, S, tk_sel)
    o_gla, g_state = _gla_prompt(gqk, gv, glog, B, S, tc_gla)
    x_new = _out_projection(x, o_fox, o_nsa, o_gla, gog, p['gn'], p['w_out'], tm)
    return x_new, dict(fkv=fkv, small=small, rows=rows, win=win, g_state=g_state)


def _per_head_col(vals):
    r = lax.broadcasted_iota(I32, (SUBLANES, 1), 0)
    out = jnp.zeros((SUBLANES, 1), F32)
    for h, v in enumerate(vals):
        out = out + jnp.where(r == h, v, 0.0)
    return out


def _per_head_row(vals, width):
    grp = lax.broadcasted_iota(I32, (1, width), 1) // HEAD_DIM
    out = jnp.zeros((1, width), F32)
    for h, v in enumerate(vals):
        out = out + jnp.where(grp == h, v, 0.0)
    return out


def _head_lane_sums(row):
    grp = lax.broadcasted_iota(I32, row.shape, 1) // HEAD_DIM
    return [jnp.sum(jnp.where(grp == h, row, 0.0), axis=1, keepdims=True) for h in range(N_HEADS)]


def _cols_of(row):
    return jnp.concatenate([jnp.broadcast_to(row[:, j:j + LANES], (LANES, LANES)).T
                            for j in range(0, row.shape[1], LANES)], axis=0)


def _row_of(col):
    return jnp.concatenate([jnp.broadcast_to(col[j:j + LANES], (LANES, LANES)).T[0:1, :]
                            for j in range(0, col.shape[0], LANES)], axis=1)


def _sublane_group_sum(x):
    return jnp.sum(x.reshape(x.shape[0] // SUBLANES, SUBLANES, x.shape[1]), axis=0)


def _fold_matrix(n_pages):
    r = lax.broadcasted_iota(I32, (n_pages * SUBLANES, n_pages * N_HEADS * SUBLANES), 0)
    c = lax.broadcasted_iota(I32, (n_pages * SUBLANES, n_pages * N_HEADS * SUBLANES), 1)
    blk = c // SUBLANES
    return ((blk // N_HEADS == r // SUBLANES) & (blk % N_HEADS == r % SUBLANES)).astype(F32)


def _fox_decode_kernel(pt_ref, q_ref, kvn_ref, smn_ref, *refs, n_pages):
    del pt_ref
    kv_refs = refs[0:n_pages]
    lf_refs = refs[n_pages:2 * n_pages]
    o_ref, lf_sc, part_sc = refs[2 * n_pages:]
    R = n_pages * SUBLANES
    PG = kv_refs[0].shape[-1]
    row = pl.ds(pl.program_id(0) % SUBLANES, 1)
    q_row = q_ref[row, :]
    kvn = kvn_ref[row, :]
    smn = smn_ref[row, :]
    q_cols = _cols_of(q_row)

    lf_sc[...] = jnp.zeros_like(lf_sc)
    for p in range(n_pages):
        lf_sc[p * SUBLANES:p * SUBLANES + N_HEADS, :] = lf_refs[p][...]
    lft = lf_sc[...]
    k0 = lax.broadcasted_iota(I32, (PG, PG), 0)
    k1 = lax.broadcasted_iota(I32, (PG, PG), 1)
    within = _dot01_r(lft, k0 > k1)
    tot = jnp.broadcast_to(jnp.sum(lft, axis=1, keepdims=True), (R, PG))
    r0 = lax.broadcasted_iota(I32, (R, R), 0)
    r1 = lax.broadcasted_iota(I32, (R, R), 1)
    later = (r1 % SUBLANES == r0 % SUBLANES) & (r1 // SUBLANES > r0 // SUBLANES)
    cross = _dot01(later, tot)
    rr = lax.broadcasted_iota(I32, (R, 1), 0) % SUBLANES
    newcol = jnp.zeros((R, 1), F32)
    for h in range(N_HEADS):
        newcol = newcol + jnp.where(rr == h, smn[:, SM_FF + h:SM_FF + h + 1], 0.0)
    bias = (within + cross + newcol).reshape(n_pages, SUBLANES, PG)

    for p in range(n_pages):
        for h in range(N_HEADS):
            g = p * N_HEADS + h
            part_sc[g * SUBLANES:(g + 1) * SUBLANES, :] = _sublane_group_sum(
                kv_refs[p][0, h] * q_cols[h * HEAD_DIM:(h + 1) * HEAD_DIM])
    s = _dot01(_fold_matrix(n_pages), part_sc[...])
    s3 = s.reshape(n_pages, SUBLANES, PG) * SCALE + bias
    s_new = _per_head_col(_head_lane_sums(q_row * kvn[:, 0:D_FOX])) * SCALE
    m = jnp.max(jnp.max(s3, axis=2, keepdims=True), axis=0)
    m = jnp.maximum(m, s_new)
    p3 = jnp.exp(s3 - m[None])
    pn = jnp.exp(s_new - m)
    inv = 1.0 / (jnp.sum(jnp.sum(p3, axis=2, keepdims=True), axis=0) + pn)
    o_cols = []
    for h in range(N_HEADS):
        acc = jnp.zeros((HEAD_DIM, PG), F32)
        for p in range(n_pages):
            acc = acc + kv_refs[p][1, h] * p3[p, h:h + 1, :]
        o_cols.append(jnp.sum(acc, axis=1, keepdims=True) * inv[h:h + 1])
    w_new = _per_head_row([pn[h:h + 1] * inv[h:h + 1] for h in range(N_HEADS)], D_FOX)
    o_ref[row, :] = _row_of(jnp.concatenate(o_cols, axis=0)) + w_new * kvn[:, D_FOX:2 * D_FOX]


def _fox_decode(l, pt_flat, n_pages, fq, fkv, small, kv_cache_t, lft_cache):
    DB = fq.shape[0]
    PG = kv_cache_t.shape[-1]
    page = lambda p, nz: (lambda b, pt: (l, pt[b * n_pages + p]) + (0,) * nz)
    rows8 = lambda w: pl.BlockSpec((SUBLANES, w), lambda b, pt: (b // SUBLANES, 0))
    return pl.pallas_call(
        functools.partial(_fox_decode_kernel, n_pages=n_pages),
        grid_spec=pltpu.PrefetchScalarGridSpec(
            num_scalar_prefetch=1,
            grid=(DB,),
            in_specs=[rows8(D_FOX), rows8(2 * D_FOX), rows8(LANES)]
            + [pl.BlockSpec((None, None, 2, N_HEADS, HEAD_DIM, PG), page(p, 4)) for p in range(n_pages)]
            + [pl.BlockSpec((None, None, N_HEADS, PG), page(p, 2)) for p in range(n_pages)],
            out_specs=rows8(D_FOX),
            scratch_shapes=[pltpu.VMEM((n_pages * SUBLANES, PG), F32),
                            pltpu.VMEM((n_pages * N_HEADS * SUBLANES, PG), F32)],
        ),
        out_shape=jax.ShapeDtypeStruct((DB, D_FOX), F32),
        compiler_params=_params(("arbitrary",)),
        name="fox_decode",
    )(pt_flat, fq, fkv, small, *([kv_cache_t] * n_pages), *([lft_cache] * n_pages))


def _nsa_decode_kernel(pt_ref, q_ref, qr_ref, rown_ref, winn_ref, sm_ref, win_ref, *refs,
                       n_pages, past_len):
    del pt_ref
    pg_refs = refs[0:n_pages]
    cmp_refs = refs[n_pages:2 * n_pages]
    o_ref, nw_ref, cmp_sc, qc_sc, part_sc, sw_sc = refs[2 * n_pages:]
    R = n_pages * SUBLANES
    PG = pg_refs[0].shape[-1]
    WB = win_ref.shape[-1]
    per_page = PG // CMP_LEN
    assert per_page <= SUBLANES and PG == 2 * SEL_LEN and R == LANES
    jt = past_len // SEL_LEN
    row = pl.ds(pl.program_id(0) % SUBLANES, 1)
    q_row = q_ref[row, :]
    qr_row = qr_ref[row, :]
    rown = rown_ref[row, :]
    winn = winn_ref[row, :]
    smn = sm_ref[row, :]
    qr_cols = _cols_of(qr_row)
    rep4 = lambda r64: jnp.concatenate([r64] * N_HEADS, axis=1)

    qc_sc[...] = jnp.zeros_like(qc_sc)
    for h in range(N_HEADS):
        qc_sc[h:h + 1, 0:HEAD_DIM] = q_row[:, h * HEAD_DIM:(h + 1) * HEAD_DIM]
    head_row = lax.broadcasted_iota(I32, (SUBLANES, 1), 0) < N_HEADS

    cmp_sc[...] = jnp.zeros_like(cmp_sc)
    for p in range(n_pages):
        cmp_sc[p * SUBLANES:p * SUBLANES + per_page, :] = cmp_refs[p][...]
    cmpa = cmp_sc[...]
    lane = lax.broadcasted_iota(I32, (1, R), 1)
    blk = per_page * (lane // SUBLANES) + lane % SUBLANES
    complete = (lane % SUBLANES < per_page) & ((blk + 1) * CMP_LEN - 1 <= past_len)
    s = _dot_nt_hilo(qc_sc[...], cmpa) * SCALE
    s = jnp.where(complete, s, NEG)
    e = jnp.exp(s - jnp.max(s, axis=-1, keepdims=True))
    pc = e / jnp.sum(e, axis=-1, keepdims=True) * complete.astype(F32)
    vcb_t = cmpa.T[HEAD_DIM:2 * HEAD_DIM, :]
    o_cmp = [jnp.sum(vcb_t * pc[h:h + 1, :], axis=1, keepdims=True) for h in range(N_HEADS)]

    imp_c = jnp.sum(jnp.where(head_row, pc, 0.0), axis=0, keepdims=True)
    imp_s = imp_c + pltpu.roll(imp_c, R - 1, 1)
    cand = (lane % SUBLANES == 0) | (lane % SUBLANES == 2)
    jsel = 2 * (lane // SUBLANES) + (lane % SUBLANES) // 2
    score = jnp.where(jsel == jt, 2.0 * SEL_FORCE,
                      jnp.where((jsel == 0) | (jsel == jt - 1), SEL_FORCE,
                                jnp.where(jsel <= jt, imp_s + 0.0, -1.0)))
    score_b = jnp.broadcast_to(score, (R, R))
    key_row = _order_key(score_b)
    key_col = _order_key(score_b.T)
    l0 = lax.broadcasted_iota(I32, (R, R), 0)
    l1 = lax.broadcasted_iota(I32, (R, R), 1)
    cand_col = (l0 % SUBLANES == 0) | (l0 % SUBLANES == 2)
    beats = cand_col & (key_col > jnp.where(l0 < l1, key_row - 1, key_row))
    cnt = jnp.sum(beats.astype(I32), axis=0, keepdims=True)
    sel_row = (cand & (cnt < TOP_N - 1)).astype(F32)
    sel_col = jnp.broadcast_to(sel_row, (R, R)).T
    half = ((l0 % SUBLANES == 0) & (l1 < SEL_LEN)) | ((l0 % SUBLANES == 2) & (l1 >= SEL_LEN))
    z = jnp.where(half, sel_col, 0.0)
    same_page = (l1 // SUBLANES == l0 // SUBLANES).astype(BF16)
    picked = jnp.dot(same_page, z.astype(BF16), preferred_element_type=F32)
    picked = picked.reshape(n_pages, SUBLANES, PG) > 0.5

    for p in range(n_pages):
        ks_t = pg_refs[p][2]
        for h in range(N_HEADS):
            g = p * N_HEADS + h
            part_sc[g * SUBLANES:(g + 1) * SUBLANES, :] = _sublane_group_sum(
                ks_t * qr_cols[h * HEAD_DIM:(h + 1) * HEAD_DIM])
    s = _dot01(_fold_matrix(n_pages), part_sc[...])
    s3 = jnp.where(picked, s.reshape(n_pages, SUBLANES, PG) * SCALE, NEG)
    s_new = _per_head_col(_head_lane_sums(qr_row * rep4(rown[:, 2 * HEAD_DIM:3 * HEAD_DIM]))) * SCALE
    m = jnp.maximum(jnp.max(jnp.max(s3, axis=2, keepdims=True), axis=0), s_new)
    p3 = jnp.exp(s3 - m[None])
    pn = jnp.exp(s_new - m)
    inv = 1.0 / (jnp.sum(jnp.sum(p3, axis=2, keepdims=True), axis=0) + pn)
    o_sel = []
    for h in range(N_HEADS):
        acc = jnp.zeros((HEAD_DIM, PG), F32)
        for p in range(n_pages):
            acc = acc + pg_refs[p][3] * p3[p, h:h + 1, :]
        o_sel.append(jnp.sum(acc, axis=1, keepdims=True) * inv[h:h + 1])

    kw_t = win_ref[0]
    vw_t = win_ref[1]
    wlane = lax.broadcasted_iota(I32, (1, WB), 1)
    wpos = past_len - WB + wlane
    wd = past_len - wpos
    wok = (wd >= 0) & (wd < WINDOW) & (wpos >= 0)
    sw_sc[...] = jnp.zeros_like(sw_sc)
    for h in range(N_HEADS):
        qh = qr_cols[h * HEAD_DIM:(h + 1) * HEAD_DIM]
        sw_sc[h:h + 1, :] = jnp.sum(kw_t * jnp.concatenate([qh] * (WB // LANES), axis=1), axis=0, keepdims=True)
    sw = jnp.where(wok, sw_sc[...] * SCALE, NEG)
    sw_new = _per_head_col(_head_lane_sums(qr_row * rep4(winn[:, 0:HEAD_DIM]))) * SCALE
    mw = jnp.maximum(jnp.max(sw, axis=-1, keepdims=True), sw_new)
    ew = jnp.exp(sw - mw)
    en = jnp.exp(sw_new - mw)
    invw = 1.0 / (jnp.sum(ew, axis=-1, keepdims=True) + en)

    gate = lambda h, c: smn[:, SM_NG + 3 * h + c:SM_NG + 3 * h + c + 1]
    o_cols = []
    for h in range(N_HEADS):
        o_win = jnp.sum(vw_t * ew[h:h + 1, :], axis=1, keepdims=True) * invw[h:h + 1]
        o_cols.append(gate(h, 0) * o_cmp[h] + gate(h, 1) * o_sel[h] + gate(h, 2) * o_win)
    w_sel = _per_head_row([gate(h, 1) * pn[h:h + 1] * inv[h:h + 1] for h in range(N_HEADS)], D_NSA)
    w_win = _per_head_row([gate(h, 2) * en[h:h + 1] * invw[h:h + 1] for h in range(N_HEADS)], D_NSA)
    o_ref[row, :] = (_row_of(jnp.concatenate(o_cols, axis=0))
                     + w_sel * rep4(rown[:, 3 * HEAD_DIM:4 * HEAD_DIM])
                     + w_win * rep4(winn[:, HEAD_DIM:2 * HEAD_DIM]))
    last = lax.broadcasted_iota(I32, (HEAD_DIM, WB), 1) == WB - 1
    winn_cols = _cols_of(winn)
    for s in range(2):
        new_col = winn_cols[s * HEAD_DIM:(s + 1) * HEAD_DIM, 0:1]
        nw_ref[s] = jnp.where(last, new_col, pltpu.roll(win_ref[s], WB - 1, 1))


def _nsa_decode(l, pt_flat, n_pages, past_len, nq, nqr, rows, win, small, nsa_cache_t, cmp_pool, win_state_t):
    DB = nq.shape[0]
    PG = nsa_cache_t.shape[-1]
    WB = win_state_t.shape[-1]
    page = lambda p: (lambda b, pt: (l, pt[b * n_pages + p], 0, 0, 0))
    cpage = lambda p: (lambda b, pt: (pt[b * n_pages + p], 0, 0))
    rows8 = lambda w: pl.BlockSpec((SUBLANES, w), lambda b, pt: (b // SUBLANES, 0))
    return pl.pallas_call(
        functools.partial(_nsa_decode_kernel, n_pages=n_pages, past_len=past_len),
        grid_spec=pltpu.PrefetchScalarGridSpec(
            num_scalar_prefetch=1,
            grid=(DB,),
            in_specs=[rows8(D_NSA), rows8(D_NSA), rows8(4 * HEAD_DIM), rows8(2 * HEAD_DIM), rows8(LANES),
                      pl.BlockSpec((None, None, 2, HEAD_DIM, WB), lambda b, pt: (l, b, 0, 0, 0))]
            + [pl.BlockSpec((None, None, 4, HEAD_DIM, PG), page(p)) for p in range(n_pages)]
            + [pl.BlockSpec((None, PG // CMP_LEN, LANES), cpage(p)) for p in range(n_pages)],
            out_specs=[rows8(D_NSA),
                       pl.BlockSpec((None, 2, HEAD_DIM, WB), lambda b, pt: (b, 0, 0, 0))],
            scratch_shapes=[pltpu.VMEM((n_pages * SUBLANES, LANES), F32),
                            pltpu.VMEM((SUBLANES, LANES), F32),
                            pltpu.VMEM((n_pages * N_HEADS * SUBLANES, PG), F32),
                            pltpu.VMEM((SUBLANES, WB), F32)],
        ),
        out_shape=[jax.ShapeDtypeStruct((DB, D_NSA), F32),
                   jax.ShapeDtypeStruct((DB, 2, HEAD_DIM, WB), F32)],
        compiler_params=_params(("arbitrary",)),
        name="nsa_decode",
    )(pt_flat, nq, nqr, rows, win, small, win_state_t,
      *([nsa_cache_t] * n_pages), *([cmp_pool] * n_pages))


def _gla_decode_kernel(q_ref, k_ref, g_ref, v_ref, s_ref, o_ref, so_ref):
    s_new = jnp.exp(g_ref[...]) * s_ref[...] + k_ref[...] * v_ref[...]
    so_ref[...] = s_new
    o_ref[...] = jnp.sum((q_ref[...] * SCALE) * s_new, axis=2, keepdims=True)


def _gla_decode(l, gqk, gv, glog, state, nb):
    DB = gqk.shape[0]
    col = lambda a: a.reshape(DB, N_HEADS, HEAD_DIM, 1)
    cspec = pl.BlockSpec((nb, N_HEADS, HEAD_DIM, 1), lambda i: (i, 0, 0, 0))
    vspec = pl.BlockSpec((nb, N_HEADS, 1, GLA_DV), lambda i: (i, 0, 0, 0))
    sspec = pl.BlockSpec((nb, N_HEADS, HEAD_DIM, GLA_DV), lambda i: (i, 0, 0, 0))
    o, s_new = pl.pallas_call(
        _gla_decode_kernel,
        grid=(DB // nb,),
        in_specs=[cspec, cspec, cspec, vspec,
                  pl.BlockSpec((None, nb, N_HEADS, HEAD_DIM, GLA_DV), lambda i: (l, i, 0, 0, 0))],
        out_specs=[vspec, sspec],
        out_shape=[jax.ShapeDtypeStruct((DB, N_HEADS, 1, GLA_DV), F32),
                   jax.ShapeDtypeStruct((DB, N_HEADS, HEAD_DIM, GLA_DV), F32)],
        compiler_params=_params(("parallel",)),
        name="gla_decode",
    )(col(gqk[:, 0:D_FOX]), col(gqk[:, D_FOX:2 * D_FOX]), col(glog),
      gv.reshape(DB, N_HEADS, 1, GLA_DV), state)
    return o.reshape(DB, D_GLA), s_new


def _mix_sample(x, l, p, tab, pt_flat, n_pages, past_len, fox_kv_t, fox_lft_c, nsa_t, cmp_blocks,
                win_state_t, gla_state):
    DB = x.shape[0]
    (fq, fkv, nq, nqr, rows, win, gqk, gv, gog, glog, small) = _in_projection(
        x, 0, DB, DB, p['g'], p['w_r32'], p['sb'], p['wgk32'], p['bgk'], tab, DB)
    o_fox = _fox_decode(l, pt_flat, n_pages, fq, fkv, small, fox_kv_t, fox_lft_c)
    n_pool, PG = nsa_t.shape[1], nsa_t.shape[-1]
    per_layer = n_pool * (PG // CMP_LEN)
    cmp_pool = _compress(cmp_blocks, p['pe_cmp'], p['w_cmp'], 256, False, row0=l * per_layer, n_rows=per_layer)
    cmp_pool = cmp_pool.reshape(n_pool, PG // CMP_LEN, LANES)
    o_nsa, new_win = _nsa_decode(l, pt_flat, n_pages, past_len, nq, nqr, rows, win, small,
                                 nsa_t, cmp_pool, win_state_t)
    o_gla, g_state = _gla_decode(l, gqk, gv, glog, gla_state, 8)
    x_new = _out_projection(x, o_fox, o_nsa, o_gla, gog, p['gn'], p['w_out32'], DB)
    return x_new, dict(fkv=fkv, small=small, rows=rows, win=new_win, g_state=g_state)


def kernel(x_prompt, x_sample, cache_fox_kv, cache_fox_logf, cache_nsa_kv, state_nsa_win, state_gla,
           page_table, norm_mix_g, w_in, b_fox_f, w_cmp, pe_cmp, w_gla_gk, b_gla_gk, g_gla_norm, w_out,
           norm_ffn_g, dense_w_gate, dense_w_up, dense_w_down, moe_w_router, moe_w_gate, moe_w_up,
           moe_w_down, final_norm_g):
    B, S, _ = x_prompt.shape
    DB, TN, _ = x_sample.shape
    assert TN == 1
    depth, n_pool, PG = cache_fox_kv.shape[0:3]
    n_pages = page_table.shape[1]
    past_len = n_pages * PG
    WB = state_nsa_win.shape[2]
    xp = x_prompt.reshape(B * S, D_MODEL)
    xs = x_sample.reshape(DB, D_MODEL)
    tab_p = _rope_table(jnp.arange(S))
    tab_s = _rope_table(jnp.full((DB,), past_len, I32))
    pt_flat = page_table.reshape(-1).astype(I32)
    fox_kv_t = jnp.transpose(cache_fox_kv, (0, 1, 3, 4, 5, 2))
    fox_lft_c = jnp.swapaxes(cache_fox_logf, 2, 3)
    nsa_t = jnp.transpose(cache_nsa_kv, (0, 1, 3, 4, 2))
    win_state_t = jnp.transpose(state_nsa_win, (0, 1, 3, 4, 2))
    n_cmp = depth * n_pool * (PG // CMP_LEN)
    cmp_blocks = _block_major(cache_nsa_kv[:, :, :, 0:2, :].reshape(n_cmp * CMP_LEN, 2, HEAD_DIM), n_cmp)
    gfin = final_norm_g.reshape(1, D_MODEL)
    cp, cs = [], []
    for l in range(depth):
        p = _layer_mix_params(l, norm_mix_g, w_in, b_fox_f, w_cmp, pe_cmp, w_gla_gk, b_gla_gk,
                              g_gla_norm, w_out)
        gf = norm_ffn_g[l].reshape(1, D_MODEL)
        i = l // 2
        moe = l % 2 == 1
        xp, c = _mix_prompt(xp, B, S, p, tab_p, 512, 512, 512, 256)
        cp.append(c)
        if moe:
            wr = jnp.zeros((D_MODEL, LANES), F32).at[:, 0:N_EXPERTS].set(moe_w_router[i])
            experts = (moe_w_gate[i], moe_w_up[i], moe_w_down[i])
            disp_p = _moe_dispatch(xp, gf, wr, 512, 1024)
        xs, c = _mix_sample(xs, l, p, tab_s, pt_flat, n_pages, past_len, fox_kv_t, fox_lft_c, nsa_t,
                            cmp_blocks, win_state_t, state_gla)
        cs.append(c)
        if moe:
            disp_s = _moe_dispatch(xs, gf, wr, DB, LANES)
            fin = gfin if l == depth - 1 else None
            xp = _moe_finish(disp_p, *experts, 512, 1024, 512, fin)
            xs = _moe_finish(disp_s, *experts, DB, LANES, 512, fin)
        else:
            wg, wu, wd = (dense_w_gate[i].astype(BF16), dense_w_up[i].astype(BF16),
                          dense_w_down[i].astype(BF16))
            xp = _dense_ffn(xp, gf, wg, wu, wd, 1024, 512)
            xs = _dense_ffn(xs, gf, dense_w_gate[i], dense_w_up[i], dense_w_down[i], DB, 512)
    if depth % 2 == 1:
        xp = _final_norm(xp, gfin, 512)
        xs = _final_norm(xs, gfin, DB)
    y_p = xp.reshape(B, S, D_MODEL)
    y_s = xs.reshape(DB, 1, D_MODEL)
    wp = min(WINDOW, S)
    st = lambda key, group: jnp.stack([c[key] for c in group])
    return (y_p, y_s,
            st('fkv', cp).reshape(depth, B, S, 2, N_HEADS, HEAD_DIM),
            st('small', cp)[:, :, SM_FF:SM_FF + N_HEADS].reshape(depth, B, S, N_HEADS),
            st('rows', cp).reshape(depth, B, S, 4, HEAD_DIM),
            st('win', cp).reshape(depth, B, S, 2, HEAD_DIM)[:, :, S - wp:],
            st('g_state', cp),
            st('fkv', cs).reshape(depth, DB, 1, 2, N_HEADS, HEAD_DIM),
            st('small', cs)[:, :, SM_FF:SM_FF + N_HEADS].reshape(depth, DB, 1, N_HEADS),
            st('rows', cs).reshape(depth, DB, 1, 4, HEAD_DIM),
            jnp.transpose(st('win', cs), (0, 1, 4, 2, 3)),
            st('g_state', cs))
```

```python
import functools

import jax
import jax.numpy as jnp
from jax import lax
from jax.experimental import pallas as pl
from jax.experimental.pallas import tpu as pltpu
from jax.experimental.pallas import tpu_sc as plsc

F32 = jnp.float32
BF16 = jnp.bfloat16
I32 = jnp.int32
HI = lax.Precision.HIGHEST

D_MODEL = 1024
HEAD_DIM = 64
N_HEADS = 4
D_FOX = N_HEADS * HEAD_DIM
D_NSA = N_HEADS * HEAD_DIM
GLA_DV = 128
D_GLA = N_HEADS * GLA_DV
GLA_RANK = 16
GLA_TAU = 16.0
GLA_CHUNK = 64
CMP_LEN = 32
SEL_LEN = 64
TOP_N = 16
WINDOW = 512
ROPE_THETA = 500000.0
ROPE_DIM = HEAD_DIM // 4
ROPE_HALF = ROPE_DIM // 2
D_FF = 3584
N_EXPERTS = 8
EPS = 1e-6
SEL_FORCE = 1e9
NEG = -1e30
SCALE = HEAD_DIM ** -0.5
LOG2E = 1.4426950408889634

LANES = 128
SUBLANES = 8
VMEM_LIMIT = 56 * 1024 * 1024

C_FQ = 0
C_FKV = 256
C_NQ = 768
C_NKV = 1024
C_GQK = 1408
C_GV = 1920
C_GOG = 2432
C_SMALL = 2944
C_END = 3072
SM_FF = 0
SM_NG = 4
SM_GLR = 16

NT = (((1,), (1,)), ((), ()))


def _params(sem):
    return pltpu.CompilerParams(dimension_semantics=sem, vmem_limit_bytes=VMEM_LIMIT)


def _rms(x, g):
    ms = jnp.mean(x * x, axis=-1, keepdims=True)
    return x * lax.rsqrt(ms + EPS) * g


def _sigmoid(x):
    return 1.0 / (1.0 + jnp.exp(-x))


def _log_sigmoid(x):
    return -(jnp.maximum(-x, 0.0) + jnp.log1p(jnp.exp(-jnp.abs(x))))


def _silu(x):
    return x * _sigmoid(x)


def _bdot(a, b):
    return jnp.dot(a.astype(BF16), b.astype(BF16), preferred_element_type=F32)


def _bdot_nt(a, b):
    return lax.dot_general(a.astype(BF16), b.astype(BF16), NT, preferred_element_type=F32)


def _split3(x):
    h = x.astype(BF16)
    r = x - h.astype(F32)
    m = r.astype(BF16)
    return h, m, (r - m.astype(F32)).astype(BF16)


def _dot01(m01, x):
    mb = m01.astype(BF16)
    h, m, l = _split3(x)
    return (jnp.dot(mb, h, preferred_element_type=F32) + jnp.dot(mb, m, preferred_element_type=F32)
            + jnp.dot(mb, l, preferred_element_type=F32))


def _dot01_r(x, m01):
    mb = m01.astype(BF16)
    h, m, l = _split3(x)
    return (jnp.dot(h, mb, preferred_element_type=F32) + jnp.dot(m, mb, preferred_element_type=F32)
            + jnp.dot(l, mb, preferred_element_type=F32))


def _dot01_nt(m01, x):
    mb = m01.astype(BF16)
    h, m, l = _split3(x)
    return (lax.dot_general(mb, h, NT, preferred_element_type=F32)
            + lax.dot_general(mb, m, NT, preferred_element_type=F32)
            + lax.dot_general(mb, l, NT, preferred_element_type=F32))


def _dot_nt_hilo(a, b):
    ah = a.astype(BF16)
    al = (a - ah.astype(F32)).astype(BF16)
    bh = b.astype(BF16)
    bl = (b - bh.astype(F32)).astype(BF16)
    return (lax.dot_general(ah, bh, NT, preferred_element_type=F32)
            + lax.dot_general(al, bh, NT, preferred_element_type=F32)
            + lax.dot_general(ah, bl, NT, preferred_element_type=F32))


def _wdot(a, w):
    if w.dtype == F32:
        return jnp.dot(a.astype(F32), w, precision=HI, preferred_element_type=F32)
    return jnp.dot(a.astype(BF16), w, preferred_element_type=F32)


def _rope128(x, a, bp, bm):
    return x * a + pltpu.roll(x, ROPE_HALF, 1) * bp + pltpu.roll(x, LANES - ROPE_HALF, 1) * bm


def _inproj_kernel(x_ref, g_ref, w_ref, sb_ref, wgk_ref, bgk_ref, tab_ref,
                   fq_ref, fkv_ref, nq_ref, nqr_ref, rows_ref, win_ref,
                   gqk_ref, gv_ref, gog_ref, glog_ref, small_ref, *extra, feature_major):
    h = _rms(x_ref[...], g_ref[...]).astype(w_ref.dtype)

    def put(ref, v):
        ref[...] = v.T if feature_major else v

    def mm(a, b):
        return _wdot(h, w_ref[:, a:b])

    fq_ref[...] = mm(C_FQ, C_FKV)
    put(fkv_ref, mm(C_FKV, C_NQ))
    tab = tab_ref[...]
    ab, pb, mb = tab[:, 0:128], tab[:, 128:256], tab[:, 256:384]
    af, pf, mf = tab[:, 384:512], tab[:, 512:640], tab[:, 640:768]
    nq = mm(C_NQ, C_NKV)
    nq_ref[...] = nq
    nqr_ref[:, 0:128] = _rope128(nq[:, 0:128], ab, pb, mb)
    nqr_ref[:, 128:256] = _rope128(nq[:, 128:256], ab, pb, mb)
    nkv = mm(C_NKV, C_GQK)
    put(rows_ref, jnp.concatenate([nkv[:, 0:128], _rope128(nkv[:, 128:256], af, pf, mf)], axis=1))
    put(win_ref, _rope128(nkv[:, 256:384], af, pf, mf))
    gqk_ref[...] = mm(C_GQK, C_GV)
    gv_ref[...] = mm(C_GV, C_GOG)
    gog_ref[...] = mm(C_GOG, C_SMALL)
    sm = mm(C_SMALL, C_END)
    glog_ref[...] = _log_sigmoid(_wdot(sm, wgk_ref[...]) + bgk_ref[...]) * (1.0 / GLA_TAU)
    smb = sm + sb_ref[...]
    lane = lax.broadcasted_iota(I32, smb.shape, 1)
    small = jnp.where(lane < SM_NG, _log_sigmoid(smb), _sigmoid(smb))
    small_ref[...] = small
    if feature_major:
        extra[0][...] = small.T[0:SUBLANES, :]


def _in_projection(x_all, row0, n_rows, tm, g, w_r, sb, wgk, bgk, tab, tab_period, batch=None):
    assert n_rows % tm == 0 and row0 % tm == 0 and tab_period % tm == 0
    nt = n_rows // tm
    b0 = row0 // tm
    npd = tab_period // tm
    widths = (256, 512, 256, 256, 256, 128, 512, 512, 512, 256, 128)
    fm = (1, 4, 5) if batch is not None else ()
    full = lambda shape: pl.BlockSpec(shape, lambda i: (0, 0))
    row_spec = lambda w: pl.BlockSpec((tm, w), lambda i: (i, 0))
    if batch is not None:
        B, S = batch
        per = S // tm
        assert n_rows == B * S and S % tm == 0
        t_spec = lambda w: pl.BlockSpec((None, w, tm), lambda i: (i // per, 0, i % per))
        out_specs = [t_spec(w) if k in fm else row_spec(w) for k, w in enumerate(widths)] + [t_spec(SUBLANES)]
        out_shape = [jax.ShapeDtypeStruct((B, w, S) if k in fm else (n_rows, w), F32)
                     for k, w in enumerate(widths)] + [jax.ShapeDtypeStruct((B, SUBLANES, S), F32)]
    else:
        out_specs = [row_spec(w) for w in widths]
        out_shape = [jax.ShapeDtypeStruct((n_rows, w), F32) for w in widths]
    return pl.pallas_call(
        functools.partial(_inproj_kernel, feature_major=batch is not None),
        grid=(nt,),
        in_specs=[
            pl.BlockSpec((tm, D_MODEL), lambda i: (b0 + i, 0)),
            full((1, D_MODEL)),
            full((D_MODEL, C_END)),
            full((1, LANES)),
            full((LANES, 256)),
            full((1, 256)),
            pl.BlockSpec((tm, 768), lambda i: (i % npd, 0)),
        ],
        out_specs=out_specs,
        out_shape=out_shape,
        compiler_params=_params(("parallel",)),
        name="in_projection",
    )(x_all, g, w_r, sb, wgk, bgk, tab)


def _cumsum_kernel(sm_ref, cr_ref, carry):
    t = pl.program_id(1)
    ts = sm_ref.shape[1]

    @pl.when(t == 0)
    def _():
        carry[...] = jnp.zeros_like(carry)

    r = lax.broadcasted_iota(I32, (ts, ts), 0)
    c = lax.broadcasted_iota(I32, (ts, ts), 1)
    cs = _dot01_r(sm_ref[...], r <= c) + carry[...]
    carry[...] = cs[:, ts - 1:ts]
    cr_ref[...] = cs * LOG2E


def _fox_cumsum(small_t, ts):
    B, _, S = small_t.shape
    spec = pl.BlockSpec((None, SUBLANES, ts), lambda b, t: (b, 0, t))
    return pl.pallas_call(
        _cumsum_kernel,
        grid=(B, S // ts),
        in_specs=[spec],
        out_specs=spec,
        out_shape=jax.ShapeDtypeStruct((B, SUBLANES, S), F32),
        scratch_shapes=[pltpu.VMEM((SUBLANES, 1), F32)],
        compiler_params=_params(("parallel", "arbitrary")),
        name="fox_cumsum",
    )(small_t)


def _pair_mask(shape, h, axis=1):
    return (lax.broadcasted_iota(I32, shape, axis) // HEAD_DIM) == (h % 2)


def _fox_prompt_kernel(q_ref, kv_ref, cr_ref, o_ref, *scratch):
    i = pl.program_id(1)
    j = pl.program_id(2)
    nk = pl.num_programs(2)
    tq = q_ref.shape[0]
    tk = kv_ref.shape[1]
    q_sc, m_sc, acc_sc = scratch[0:N_HEADS], scratch[N_HEADS:2 * N_HEADS], scratch[2 * N_HEADS:]

    @pl.when(j == 0)
    def _():
        for h in range(N_HEADS):
            m_sc[h][...] = jnp.full_like(m_sc[h], NEG)
            acc_sc[h][...] = jnp.zeros_like(acc_sc[h])
            slab = q_ref[:, (h // 2) * LANES:(h // 2 + 1) * LANES] * (SCALE * LOG2E)
            q_sc[h][...] = jnp.where(_pair_mask(slab.shape, h), slab, 0.0).astype(BF16)

    def tile(diagonal):
        k_slabs = [kv_ref[c0:c0 + LANES, :].astype(BF16) for c0 in (0, LANES)]
        if diagonal:
            mask = lax.broadcasted_iota(I32, (1, tk), 1) <= lax.broadcasted_iota(I32, (tq, 1), 0)
        for h in range(N_HEADS):
            v_slab = kv_ref[D_FOX + (h // 2) * LANES:D_FOX + (h // 2 + 1) * LANES, :]
            v_aug = jnp.where(_pair_mask(v_slab.shape, h, 0), v_slab, 1.0).astype(BF16)
            s = jnp.dot(q_sc[h][...], k_slabs[h // 2], preferred_element_type=F32) - cr_ref[h:h + 1, :]
            if diagonal:
                s = jnp.where(mask, s, NEG)
            m_old = m_sc[h][...]
            m_new = jnp.maximum(m_old, jnp.max(s, axis=-1, keepdims=True))
            p = jnp.exp2(s - m_new).astype(BF16)
            acc_sc[h][...] = (jnp.exp2(m_old - m_new) * acc_sc[h][...]
                              + lax.dot_general(p, v_aug, NT, preferred_element_type=F32))
            m_sc[h][...] = m_new

    @pl.when(j < i)
    def _():
        tile(False)

    @pl.when(j == i)
    def _():
        tile(True)

    @pl.when(j == nk - 1)
    def _():
        for h in range(N_HEADS):
            a = acc_sc[h][...]
            lo = (h % 2) * HEAD_DIM
            den = a[:, HEAD_DIM - lo:HEAD_DIM - lo + 1]
            o_ref[:, h * HEAD_DIM:(h + 1) * HEAD_DIM] = a[:, lo:lo + HEAD_DIM] / den


def _fox_prompt(fq, fkv_t, cr, B, S, t):
    n = S // t
    return pl.pallas_call(
        _fox_prompt_kernel,
        grid=(B, n, n),
        in_specs=[
            pl.BlockSpec((t, D_FOX), lambda b, i, j: (b * n + i, 0)),
            pl.BlockSpec((None, 2 * D_FOX, t), lambda b, i, j: (b, 0, jnp.minimum(i, j))),
            pl.BlockSpec((None, SUBLANES, t), lambda b, i, j: (b, 0, jnp.minimum(i, j))),
        ],
        out_specs=pl.BlockSpec((t, D_FOX), lambda b, i, j: (b * n + i, 0)),
        out_shape=jax.ShapeDtypeStruct((B * S, D_FOX), F32),
        scratch_shapes=([pltpu.VMEM((t, LANES), BF16)] * N_HEADS + [pltpu.VMEM((t, 1), F32)] * N_HEADS
                        + [pltpu.VMEM((t, LANES), F32)] * N_HEADS),
        compiler_params=_params(("parallel", "parallel", "arbitrary")),
        name="fox_prompt",
    )(fq, fkv_t, cr)


def _compress_kernel(x_ref, pe_ref, w_ref, o_ref, *, exact):
    for s in range(2):
        x = x_ref[s] + pe_ref[s]
        w = w_ref[s]
        if exact:
            y = jnp.dot(x, w, precision=HI, preferred_element_type=F32)
        else:
            xh = x.astype(BF16)
            xl = (x - xh.astype(F32)).astype(BF16)
            wh = w.astype(BF16)
            wl = (w - wh.astype(F32)).astype(BF16)
            y = (jnp.dot(xh, wh, preferred_element_type=F32) + jnp.dot(xl, wh, preferred_element_type=F32)
                 + jnp.dot(xh, wl, preferred_element_type=F32))
        o_ref[:, s * HEAD_DIM:(s + 1) * HEAD_DIM] = y


def _compress(x3, pe, w, tr, exact, row0=0, n_rows=None):
    K = x3.shape[2]
    R = x3.shape[1] if n_rows is None else n_rows
    assert R % tr == 0 and row0 % tr == 0
    b0 = row0 // tr
    return pl.pallas_call(
        functools.partial(_compress_kernel, exact=exact),
        grid=(R // tr,),
        in_specs=[pl.BlockSpec((2, tr, K), lambda i: (0, b0 + i, 0)),
                  pl.BlockSpec((2, 1, K), lambda i: (0, 0, 0)),
                  pl.BlockSpec((2, K, HEAD_DIM), lambda i: (0, 0, 0))],
        out_specs=pl.BlockSpec((tr, LANES), lambda i: (i, 0)),
        out_shape=jax.ShapeDtypeStruct((R, LANES), F32),
        compiler_params=_params(("parallel",)),
        name="nsa_compress",
    )(x3, pe, w)


def _block_major(kv, n_blocks):
    return kv.reshape(n_blocks, CMP_LEN, 2, HEAD_DIM).transpose(2, 0, 1, 3).reshape(
        2, n_blocks, CMP_LEN * HEAD_DIM)


def _order_key(x):
    b = lax.bitcast_convert_type(x, I32)
    return jnp.where(b < 0, b ^ jnp.int32(0x7FFFFFFF), b)


def _nsa_prompt_kernel(nq_ref, nqr_ref, sm_ref, cmp_ref, rows_ref, win_ref, o_ref, qc_sc, qx_sc, *, tk):
    QB = nq_ref.shape[0]
    S = rows_ref.shape[1]
    nb = cmp_ref.shape[0]
    nsel = S // SEL_LEN
    i = pl.program_id(1)
    qs = i * QB
    qpos = qs + lax.broadcasted_iota(I32, (QB, 1), 0)

    HQ = N_HEADS * QB
    lo_half = lax.broadcasted_iota(I32, (QB, LANES), 1) < HEAD_DIM

    def stack_heads(ref, scale, dst):
        for h in range(N_HEADS):
            slab = ref[:, (h // 2) * LANES:(h // 2 + 1) * LANES] * scale
            if h % 2:
                slab = pltpu.roll(slab, HEAD_DIM, 1)
            dst[h * QB:(h + 1) * QB, :] = jnp.where(lo_half, slab, 0.0).astype(dst.dtype)

    cmp = cmp_ref[...]
    n_l = lax.broadcasted_iota(I32, (1, nb), 1)
    complete = ((n_l + 1) * CMP_LEN - 1) <= qpos
    stack_heads(nq_ref, SCALE, qc_sc)
    s = _dot_nt_hilo(qc_sc[...], cmp).reshape(N_HEADS, QB, nb)
    s = jnp.where(complete[None], s, NEG)
    e = jnp.exp(s - jnp.max(s, axis=-1, keepdims=True))
    p = e / jnp.sum(e, axis=-1, keepdims=True) * complete.astype(F32)[None]
    o_cmp = _bdot(p.reshape(HQ, nb), cmp)
    psum = jnp.sum(p, axis=0)

    pj = lax.broadcasted_iota(I32, (nsel, nb), 0)
    pn = lax.broadcasted_iota(I32, (nsel, nb), 1)
    imp_t = _dot01_nt(pn // (SEL_LEN // CMP_LEN) == pj, psum)
    jt = (qs + lax.broadcasted_iota(I32, (1, QB), 1)) // SEL_LEN
    jj = lax.broadcasted_iota(I32, (nsel, 1), 0)
    score = jnp.where(jj == jt, 2.0 * SEL_FORCE,
                      jnp.where((jj == 0) | (jj == jt - 1), SEL_FORCE,
                                jnp.where(jj <= jt, imp_t + 0.0, -1.0)))
    key = _order_key(score)
    key_m1 = key - 1
    ngrp = nsel // SUBLANES
    sub = lax.broadcasted_iota(I32, (SUBLANES, QB), 0)
    kg = [key[r * SUBLANES:(r + 1) * SUBLANES, :] for r in range(ngrp)]
    kg1 = [key_m1[r * SUBLANES:(r + 1) * SUBLANES, :] for r in range(ngrp)]
    cnt = [jnp.zeros((SUBLANES, QB), I32) for _ in range(ngrp)]
    for jp in range(nsel):
        g = jp // SUBLANES
        row = key[jp:jp + 1, :]
        mixed = jnp.where(sub > (jp % SUBLANES), kg1[g], kg[g])
        for r in range(ngrp):
            thr = kg[r] if r < g else (kg1[r] if r > g else mixed)
            cnt[r] = cnt[r] + (row > thr).astype(I32)
    sel_t = jnp.concatenate([(c < TOP_N).astype(F32) for c in cnt], axis=0)
    if nsel < QB:
        sel_t = jnp.concatenate([sel_t, jnp.zeros((QB - nsel, QB), F32)], axis=0)
    sel = sel_t.T.astype(BF16)

    stack_heads(nqr_ref, SCALE * LOG2E, qx_sc)
    qx = qx_sc[...]

    def attend(valid, slab_t, m_old, acc_old):
        n = slab_t.shape[1]
        s = jnp.dot(qx, slab_t.astype(BF16), preferred_element_type=F32)
        s = jnp.where(valid[None], s.reshape(N_HEADS, QB, n), NEG).reshape(HQ, n)
        m_new = jnp.maximum(m_old, jnp.max(s, axis=-1, keepdims=True))
        p = jnp.exp2(s - m_new).astype(BF16)
        ones_k = lax.broadcasted_iota(I32, slab_t.shape, 0) < HEAD_DIM
        v_aug = jnp.where(ones_k, 1.0, slab_t).astype(BF16)
        acc = jnp.exp2(m_old - m_new) * acc_old + lax.dot_general(p, v_aug, NT, preferred_element_type=F32)
        return m_new, acc

    jrow = lax.broadcasted_iota(I32, (QB, 1), 0)

    def sel_tile(k0, m_old, acc_old, diagonal):
        kpos = k0 + lax.broadcasted_iota(I32, (1, tk), 1)
        expand = (jrow == kpos // SEL_LEN).astype(BF16)
        valid = jnp.dot(sel, expand, preferred_element_type=F32) > 0.5
        if diagonal:
            valid = valid & (kpos <= qpos)
        return attend(valid, rows_ref[2 * HEAD_DIM:4 * HEAD_DIM, pl.ds(k0, tk)], m_old, acc_old)

    n_full = qs // tk
    init = (jnp.full((HQ, 1), NEG, F32), jnp.zeros((HQ, LANES), F32))
    m_s, acc_s = lax.fori_loop(
        0, n_full, lambda t, c: sel_tile(pl.multiple_of(t * tk, tk), c[0], c[1], False), init)
    _, acc_s = sel_tile(pl.multiple_of(n_full * tk, tk), m_s, acc_s, True)

    wlen = WINDOW + QB
    w0 = pl.multiple_of(jnp.maximum(qs - WINDOW, 0), QB)
    wpos = w0 + lax.broadcasted_iota(I32, (1, wlen), 1)
    d = qpos - wpos
    _, acc_w = attend((d >= 0) & (d < WINDOW), win_ref[:, pl.ds(w0, wlen)],
                      jnp.full((HQ, 1), NEG, F32), jnp.zeros((HQ, LANES), F32))

    sm = sm_ref[...]
    for h in range(N_HEADS):
        rs = slice(h * QB, (h + 1) * QB)
        o_sel = acc_s[rs] * (1.0 / acc_s[rs, 0:1])
        o_win = acc_w[rs] * (1.0 / acc_w[rs, 0:1])
        c = SM_NG + 3 * h
        mix = sm[:, c:c + 1] * o_cmp[rs] + sm[:, c + 1:c + 2] * o_sel + sm[:, c + 2:c + 3] * o_win
        if h % 2 == 0:
            mix = pltpu.roll(mix, HEAD_DIM, 1)
        lo = (h % 2) * HEAD_DIM
        o_ref[:, h * HEAD_DIM:(h + 1) * HEAD_DIM] = mix[:, lo:lo + HEAD_DIM]


def _nsa_prompt(nq, nqr, small, cmp, rows, win, B, S, tk):
    QB = 128
    nq_t = S // QB
    nb = S // CMP_LEN
    assert S % tk == 0 and S >= WINDOW + QB
    return pl.pallas_call(
        functools.partial(_nsa_prompt_kernel, tk=tk),
        grid=(B, nq_t),
        in_specs=[
            pl.BlockSpec((QB, D_NSA), lambda b, i: (b * nq_t + i, 0)),
            pl.BlockSpec((QB, D_NSA), lambda b, i: (b * nq_t + i, 0)),
            pl.BlockSpec((QB, LANES), lambda b, i: (b * nq_t + i, 0)),
            pl.BlockSpec((nb, LANES), lambda b, i: (b, 0)),
            pl.BlockSpec((None, 4 * HEAD_DIM, S), lambda b, i: (b, 0, 0)),
            pl.BlockSpec((None, 2 * HEAD_DIM, S), lambda b, i: (b, 0, 0)),
        ],
        out_specs=pl.BlockSpec((QB, D_NSA), lambda b, i: (b * nq_t + i, 0)),
        out_shape=jax.ShapeDtypeStruct((B * S, D_NSA), F32),
        scratch_shapes=[pltpu.VMEM((N_HEADS * QB, LANES), F32), pltpu.VMEM((N_HEADS * QB, LANES), BF16)],
        compiler_params=_params(("parallel", "parallel")),
        name="nsa_prompt",
    )(nq, nqr, small, cmp, rows, win)


def _gla_prompt_kernel(qk_ref, v_ref, g_ref, o_ref, st_ref, s_sc):
    t = pl.program_id(1)
    nt = pl.num_programs(1)
    tc = qk_ref.shape[0]
    C = GLA_CHUNK

    @pl.when(t == 0)
    def _():
        s_sc[...] = jnp.zeros_like(s_sc)

    r = lax.broadcasted_iota(I32, (tc, tc), 0)
    c = lax.broadcasted_iota(I32, (tc, tc), 1)
    same = (r // C) == (c // C)
    causal = same & (c <= r)
    g = g_ref[...]
    gcum = _dot01(causal, g)
    g_t = g.T
    gcum_t = _dot01_r(g_t, same & (r <= c))
    gtot_t = _dot01_r(g_t, same)
    q_e = (qk_ref[:, 0:D_FOX] * SCALE * jnp.exp(gcum)).astype(BF16)
    k_e = (qk_ref[:, D_FOX:2 * D_FOX] * jnp.exp(-gcum)).astype(BF16)
    kd_t = (qk_ref[:, D_FOX:2 * D_FOX].T * jnp.exp(gtot_t - gcum_t)).astype(BF16)
    decay_t = jnp.exp(gtot_t)
    for h in range(N_HEADS):
        hs = slice(h * HEAD_DIM, (h + 1) * HEAD_DIM)
        v = v_ref[:, h * GLA_DV:(h + 1) * GLA_DV].astype(BF16)
        a = jnp.where(causal, lax.dot_general(q_e[:, hs], k_e[:, hs], NT, preferred_element_type=F32), 0.0)
        o_intra = jnp.dot(a.astype(BF16), v, preferred_element_type=F32)
        state = s_sc[h]
        for ci in range(tc // C):
            rs = slice(ci * C, (ci + 1) * C)
            o_ref[rs, h * GLA_DV:(h + 1) * GLA_DV] = (
                o_intra[rs] + jnp.dot(q_e[rs, hs], state.astype(BF16), preferred_element_type=F32))
            state = (decay_t[hs, ci * C:ci * C + 1] * state
                     + jnp.dot(kd_t[hs, rs], v[rs], preferred_element_type=F32))
        s_sc[h] = state

    @pl.when(t == nt - 1)
    def _():
        st_ref[...] = s_sc[...]


def _gla_prompt(gqk, gv, glog, B, S, tc):
    nt = S // tc
    return pl.pallas_call(
        _gla_prompt_kernel,
        grid=(B, nt),
        in_specs=[pl.BlockSpec((tc, 2 * D_FOX), lambda b, t: (b * nt + t, 0)),
                  pl.BlockSpec((tc, D_GLA), lambda b, t: (b * nt + t, 0)),
                  pl.BlockSpec((tc, D_FOX), lambda b, t: (b * nt + t, 0))],
        out_specs=[pl.BlockSpec((tc, D_GLA), lambda b, t: (b * nt + t, 0)),
                   pl.BlockSpec((None, N_HEADS, HEAD_DIM, GLA_DV), lambda b, t: (b, 0, 0, 0))],
        out_shape=[jax.ShapeDtypeStruct((B * S, D_GLA), F32),
                   jax.ShapeDtypeStruct((B, N_HEADS, HEAD_DIM, GLA_DV), F32)],
        scratch_shapes=[pltpu.VMEM((N_HEADS, HEAD_DIM, GLA_DV), F32)],
        compiler_params=_params(("parallel", "arbitrary")),
        name="gla_prompt",
    )(gqk, gv, glog)


def _outproj_kernel(x_ref, of_ref, on_ref, og_ref, gog_ref, gn_ref, w_ref, o_ref):
    acc = _wdot(of_ref[...], w_ref[0:D_FOX, :])
    acc = acc + _wdot(on_ref[...], w_ref[D_FOX:D_FOX + D_NSA, :])
    for h in range(N_HEADS):
        hs = slice(h * GLA_DV, (h + 1) * GLA_DV)
        z = _rms(og_ref[:, hs], gn_ref[...]) * _silu(gog_ref[:, hs])
        w0 = D_FOX + D_NSA + h * GLA_DV
        acc = acc + _wdot(z, w_ref[w0:w0 + GLA_DV, :])
    o_ref[...] = x_ref[...] + acc


def _out_projection(x, o_fox, o_nsa, o_gla, gog, gn, w_out, tm):
    T = x.shape[0]
    assert T % tm == 0
    row = lambda w: pl.BlockSpec((tm, w), lambda i: (i, 0))
    return pl.pallas_call(
        _outproj_kernel,
        grid=(T // tm,),
        in_specs=[row(D_MODEL), row(D_FOX), row(D_NSA), row(D_GLA), row(D_GLA),
                  pl.BlockSpec((1, GLA_DV), lambda i: (0, 0)),
                  pl.BlockSpec((D_MODEL, D_MODEL), lambda i: (0, 0))],
        out_specs=row(D_MODEL),
        out_shape=jax.ShapeDtypeStruct((T, D_MODEL), F32),
        compiler_params=_params(("parallel",)),
        name="out_projection",
    )(x, o_fox, o_nsa, o_gla, gog, gn, w_out)


def _dense_ffn_kernel(x_ref, g_ref, wg_ref, wu_ref, wd_ref, o_ref, h_sc, acc_sc):
    f = pl.program_id(1)
    nf = pl.num_programs(1)

    @pl.when(f == 0)
    def _():
        h_sc[...] = _rms(x_ref[...], g_ref[...]).astype(h_sc.dtype)
        acc_sc[...] = jnp.zeros_like(acc_sc)

    h = h_sc[...]
    a = _wdot(h, wg_ref[...])
    u = _wdot(h, wu_ref[...])
    acc_sc[...] += _wdot(_silu(a) * u, wd_ref[...])

    @pl.when(f == nf - 1)
    def _():
        o_ref[...] = x_ref[...] + acc_sc[...]


def _dense_ffn(x, g, wg, wu, wd, tm, tf):
    T = x.shape[0]
    assert T % tm == 0 and D_FF % tf == 0
    return pl.pallas_call(
        _dense_ffn_kernel,
        grid=(T // tm, D_FF // tf),
        in_specs=[pl.BlockSpec((tm, D_MODEL), lambda i, f: (i, 0)),
                  pl.BlockSpec((1, D_MODEL), lambda i, f: (0, 0)),
                  pl.BlockSpec((D_MODEL, tf), lambda i, f: (0, f)),
                  pl.BlockSpec((D_MODEL, tf), lambda i, f: (0, f)),
                  pl.BlockSpec((tf, D_MODEL), lambda i, f: (f, 0))],
        out_specs=pl.BlockSpec((tm, D_MODEL), lambda i, f: (i, 0)),
        out_shape=jax.ShapeDtypeStruct((T, D_MODEL), F32),
        scratch_shapes=[pltpu.VMEM((tm, D_MODEL), wg.dtype), pltpu.VMEM((tm, D_MODEL), F32)],
        compiler_params=_params(("parallel", "arbitrary")),
        name="dense_ffn",
    )(x, g, wg, wu, wd)


def _router_kernel(x_ref, g_ref, wr_ref, h_ref, r_ref):
    h = _rms(x_ref[...], g_ref[...])
    h_ref[...] = h
    logits = jnp.dot(h, wr_ref[...], precision=HI, preferred_element_type=F32)
    lane = lax.broadcasted_iota(I32, logits.shape, 1)
    lg = jnp.where(lane < N_EXPERTS, logits, -jnp.inf)
    m1 = jnp.max(lg, axis=-1, keepdims=True)
    i1 = jnp.min(jnp.where(lg == m1, lane, LANES), axis=-1, keepdims=True)
    lg2 = jnp.where(lane == i1, -jnp.inf, lg)
    m2 = jnp.max(lg2, axis=-1, keepdims=True)
    i2 = jnp.min(jnp.where(lg2 == m2, lane, LANES), axis=-1, keepdims=True)
    e = jnp.exp(m2 - m1)
    den = 1.0 + e
    r_ref[...] = jnp.where(lane == 0, i1.astype(F32),
                           jnp.where(lane == 1, i2.astype(F32),
                                     jnp.where(lane == 2, 1.0 / den,
                                               jnp.where(lane == 3, e / den, 0.0))))


def _router(x, g, wr_pad, tm):
    T = x.shape[0]
    assert T % tm == 0
    return pl.pallas_call(
        _router_kernel,
        grid=(T // tm,),
        in_specs=[pl.BlockSpec((tm, D_MODEL), lambda i: (i, 0)),
                  pl.BlockSpec((1, D_MODEL), lambda i: (0, 0)),
                  pl.BlockSpec((D_MODEL, LANES), lambda i: (0, 0))],
        out_specs=[pl.BlockSpec((tm, D_MODEL), lambda i: (i, 0)),
                   pl.BlockSpec((tm, LANES), lambda i: (i, 0))],
        out_shape=[jax.ShapeDtypeStruct((T, D_MODEL), F32),
                   jax.ShapeDtypeStruct((T, LANES), F32)],
        compiler_params=_params(("parallel",)),
        name="moe_router",
    )(x, g, wr_pad)


GATHER_WINDOW = 32
SC_WORKERS = 32


def _row_gather(src, idx):
    n = idx.shape[0]
    step = GATHER_WINDOW * SC_WORKERS
    n_pad = -(-n // step) * step
    if n_pad != n:
        idx = jnp.concatenate([idx, jnp.zeros((n_pad - n,), idx.dtype)])
    width = src.shape[1]
    per_worker = n_pad // SC_WORKERS
    mesh = plsc.VectorSubcoreMesh(core_axis_name="core", subcore_axis_name="subcore")

    @functools.partial(pl.kernel, out_type=jax.ShapeDtypeStruct((n_pad, width), src.dtype), mesh=mesh,
                       scratch_types=[pltpu.VMEM((per_worker,), I32),
                                      pltpu.VMEM((GATHER_WINDOW, width), src.dtype)],
                       name="row_gather")
    def gather(src_hbm, idx_hbm, dst_hbm, idx_v, buf):
        worker = lax.axis_index("core") * (SC_WORKERS // 2) + lax.axis_index("subcore")
        base = worker * per_worker
        pltpu.sync_copy(idx_hbm.at[pl.ds(base, per_worker)], idx_v)

        @pl.loop(0, per_worker // GATHER_WINDOW)
        def _(j):
            pltpu.sync_copy(src_hbm.at[idx_v.at[pl.ds(j * GATHER_WINDOW, GATHER_WINDOW)]], buf)
            pltpu.sync_copy(buf, dst_hbm.at[pl.ds(base + j * GATHER_WINDOW, GATHER_WINDOW)])

    return gather(src, idx)


def _moe_ffn_kernel(be_ref, nu_ref, x_ref, wg_ref, wu_ref, wd_ref, o_ref, acc_sc):
    b = pl.program_id(0)
    f = pl.program_id(1)
    nf = pl.num_programs(1)
    used = b < nu_ref[0]

    @pl.when(used)
    def _():
        @pl.when(f == 0)
        def _():
            acc_sc[...] = jnp.zeros_like(acc_sc)

        x = x_ref[...].astype(BF16)
        a = jnp.dot(x, wg_ref[...].astype(BF16), preferred_element_type=F32)
        u = jnp.dot(x, wu_ref[...].astype(BF16), preferred_element_type=F32)
        acc_sc[...] += _bdot(_silu(a) * u, wd_ref[...])

        @pl.when(f == nf - 1)
        def _():
            o_ref[...] = acc_sc[...]

    @pl.when(jnp.logical_not(used) & (f == nf - 1))
    def _():
        o_ref[...] = jnp.zeros_like(o_ref)


def _moe_ffn(xb, block_e, n_used, wg, wu, wd, blk, tf):
    cap = xb.shape[0]
    nb = cap // blk
    nf = D_FF // tf

    def bsel(b, nu):
        return jnp.minimum(b, nu[0] - 1)

    def fsel(b, f, nu):
        return jnp.where(b < nu[0], f, nf - 1)

    return pl.pallas_call(
        _moe_ffn_kernel,
        grid_spec=pltpu.PrefetchScalarGridSpec(
            num_scalar_prefetch=2,
            grid=(nb, nf),
            in_specs=[
                pl.BlockSpec((blk, D_MODEL), lambda b, f, be, nu: (bsel(b, nu), 0)),
                pl.BlockSpec((None, D_MODEL, tf), lambda b, f, be, nu: (be[bsel(b, nu)], 0, fsel(b, f, nu))),
                pl.BlockSpec((None, D_MODEL, tf), lambda b, f, be, nu: (be[bsel(b, nu)], 0, fsel(b, f, nu))),
                pl.BlockSpec((None, tf, D_MODEL), lambda b, f, be, nu: (be[bsel(b, nu)], fsel(b, f, nu), 0)),
            ],
            out_specs=pl.BlockSpec((blk, D_MODEL), lambda b, f, be, nu: (b, 0)),
            scratch_shapes=[pltpu.VMEM((blk, D_MODEL), F32)],
        ),
        out_shape=jax.ShapeDtypeStruct((cap, D_MODEL), F32),
        compiler_params=_params(("arbitrary", "arbitrary")),
        name="moe_ffn",
    )(block_e, n_used, xb, wg, wu, wd)


def _moe_combine_kernel(x_ref, y1_ref, y2_ref, r_ref, g_ref, o_ref, *, final):
    r = r_ref[...]
    y = x_ref[...] + (r[:, 2:3] * y1_ref[...] + r[:, 3:4] * y2_ref[...])
    o_ref[...] = _rms(y, g_ref[...]) if final else y


def _moe_combine(x, y1, y2, route, tm, final_g):
    T = x.shape[0]
    g = jnp.ones((1, D_MODEL), F32) if final_g is None else final_g
    return pl.pallas_call(
        functools.partial(_moe_combine_kernel, final=final_g is not None),
        grid=(T // tm,),
        in_specs=[pl.BlockSpec((tm, D_MODEL), lambda i: (i, 0)),
                  pl.BlockSpec((tm, D_MODEL), lambda i: (i, 0)),
                  pl.BlockSpec((tm, D_MODEL), lambda i: (i, 0)),
                  pl.BlockSpec((tm, LANES), lambda i: (i, 0)),
                  pl.BlockSpec((1, D_MODEL), lambda i: (0, 0))],
        out_specs=pl.BlockSpec((tm, D_MODEL), lambda i: (i, 0)),
        out_shape=jax.ShapeDtypeStruct((T, D_MODEL), F32),
        compiler_params=_params(("parallel",)),
        name="moe_combine",
    )(x, y1, y2, route, g)


def _moe_plan(e_top, blk):
    T = e_top.shape[0]
    n = 2 * T
    flat_e = e_top.reshape(-1)
    onehot = (flat_e[:, None] == jnp.arange(N_EXPERTS, dtype=I32)[None, :]).astype(I32)
    csum = jnp.cumsum(onehot, axis=0)
    rank = jnp.sum((csum - onehot) * onehot, axis=1)
    counts = csum[-1]
    padded = (counts + blk - 1) // blk * blk
    ends = jnp.cumsum(padded)
    pstart = ends - padded
    dest = (pstart[flat_e] + rank).astype(I32)
    n_blocks = -(-n // blk) + N_EXPERTS
    cap = n_blocks * blk
    slot_tok = (jnp.arange(cap, dtype=I32) % T).at[dest].set(jnp.arange(n, dtype=I32) // 2)
    first = jnp.arange(n_blocks, dtype=I32) * blk
    block_e = jnp.minimum(jnp.sum((ends[None, :] <= first[:, None]).astype(I32), axis=1), N_EXPERTS - 1)
    n_used = (ends[-1] // blk).astype(I32).reshape(1)
    return dest, slot_tok, block_e, n_used


def _moe_dispatch(x, g, wr_pad, tm, blk):
    h, route = _router(x, g, wr_pad, tm)
    dest, slot_tok, block_e, n_used = _moe_plan(route[:, 0:2].astype(I32), blk)
    return dict(x=x, route=route, dest=dest, xb=_row_gather(h, slot_tok), block_e=block_e, n_used=n_used)


def _moe_finish(d, wg, wu, wd, tm, blk, tf, final_g):
    yb = _moe_ffn(d['xb'], d['block_e'], d['n_used'], wg, wu, wd, blk, tf)
    d2 = d['dest'].reshape(-1, 2)
    return _moe_combine(d['x'], _row_gather(yb, d2[:, 0]), _row_gather(yb, d2[:, 1]), d['route'], tm, final_g)


def _norm_kernel(x_ref, g_ref, o_ref):
    o_ref[...] = _rms(x_ref[...], g_ref[...])


def _final_norm(x, g, tm):
    T = x.shape[0]
    return pl.pallas_call(
        _norm_kernel,
        grid=(T // tm,),
        in_specs=[pl.BlockSpec((tm, D_MODEL), lambda i: (i, 0)),
                  pl.BlockSpec((1, D_MODEL), lambda i: (0, 0))],
        out_specs=pl.BlockSpec((tm, D_MODEL), lambda i: (i, 0)),
        out_shape=jax.ShapeDtypeStruct((T, D_MODEL), F32),
        compiler_params=_params(("parallel",)),
        name="final_norm",
    )(x, g)


_IN_SPLITS = (D_FOX, D_FOX, D_FOX, N_HEADS, D_NSA, 6 * HEAD_DIM, 3 * N_HEADS,
              D_FOX, D_FOX, D_GLA, GLA_RANK, D_GLA)


def _reorder_w_in(w):
    offs = [0]
    for s in _IN_SPLITS:
        offs.append(offs[-1] + s)
    seg = lambda k: w[:, offs[k]:offs[k + 1]]
    fq, fk, fv, ff, nq, nkv, ng, gq, gk, gv, glr, gog = [seg(k) for k in range(12)]
    pad = jnp.zeros((w.shape[0], LANES - SM_GLR - GLA_RANK), w.dtype)
    return jnp.concatenate([fq, fk, fv, nq, nkv, gq, gk, gv, gog, ff, ng, glr, pad], axis=1)


def _rope_table(pos):
    inv = ROPE_THETA ** (-jnp.arange(ROPE_HALF, dtype=F32) / ROPE_HALF)
    ang = pos.astype(F32)[:, None] * inv[None, :]
    cos, sin = jnp.cos(ang), jnp.sin(ang)
    P = pos.shape[0]
    one = jnp.ones((P, HEAD_DIM - ROPE_DIM), F32)
    zero = jnp.zeros((P, HEAD_DIM - ROPE_DIM), F32)
    z8 = jnp.zeros((P, ROPE_HALF), F32)
    a64 = jnp.concatenate([cos, cos, one], axis=1)
    p64 = jnp.concatenate([z8, sin, zero], axis=1)
    m64 = jnp.concatenate([-sin, z8, zero], axis=1)
    i64 = jnp.ones((P, HEAD_DIM), F32)
    o64 = jnp.zeros((P, HEAD_DIM), F32)
    return jnp.concatenate([a64, a64, p64, p64, m64, m64, a64, i64, p64, o64, m64, o64], axis=1)


def _layer_mix_params(l, norm_mix_g, w_in, b_fox_f, w_cmp, pe_cmp, w_gla_gk, b_gla_gk, g_gla_norm, w_out):
    sb = jnp.zeros((1, LANES), F32).at[0, SM_FF:SM_FF + N_HEADS].set(b_fox_f[l])
    wgk = jnp.zeros((LANES, D_FOX), F32).at[SM_GLR:SM_GLR + GLA_RANK].set(w_gla_gk[l])
    w_r = _reorder_w_in(w_in[l])
    return dict(g=norm_mix_g[l].reshape(1, D_MODEL), w_r=w_r.astype(BF16), w_r32=w_r, sb=sb,
                wgk=wgk.astype(BF16), wgk32=wgk, w_out32=w_out[l],
                bgk=b_gla_gk[l].reshape(1, D_FOX), w_cmp=w_cmp[l],
                pe_cmp=pe_cmp[l].reshape(2, 1, CMP_LEN * HEAD_DIM),
                gn=g_gla_norm[l].reshape(1, GLA_DV), w_out=w_out[l].astype(BF16))


def _mix_prompt(x, B, S, p, tab, tm, t_fox, tk_sel, tc_gla):
    (fq, fkv_t, nq, nqr, rows_t, win_t, gqk, gv, gog, glog, small, small_t) = _in_projection(
        x, 0, B * S, tm, p['g'], p['w_r'], p['sb'], p['wgk'], p['bgk'], tab, S, batch=(B, S))
    cr = _fox_cumsum(small_t, min(S, 512))
    o_fox = _fox_prompt(fq, fkv_t, cr, B, S, t_fox)
    n_blk = S // CMP_LEN
    blocks = rows_t[:, 0:2 * HEAD_DIM, :].reshape(B, 2, HEAD_DIM, n_blk, CMP_LEN).transpose(1, 0, 3, 4, 2)
    blocks = blocks.reshape(2, B * n_blk, CMP_LEN * HEAD_DIM)
    cmp = _compress(blocks, p['pe_cmp'], p['w_cmp'], min(256, B * n_blk), True)
    o_nsa = _nsa_prompt(nq, nqr, small, cmp, rows_t, win_t, B, S, tk_sel)
    o_gla, g_state = _gla_prompt(gqk, gv, glog, B, S, tc_gla)
    x_new = _out_projection(x, o_fox, o_nsa, o_gla, gog, p['gn'], p['w_out'], tm)
    wp = min(WINDOW, S)
    return x_new, dict(fkv=fkv_t, small=small_t[:, 0:N_HEADS, :], rows=rows_t, win=win_t[:, :, S - wp:],
                       g_state=g_state)


def _per_head_col(vals):
    r = lax.broadcasted_iota(I32, (SUBLANES, 1), 0)
    out = jnp.zeros((SUBLANES, 1), F32)
    for h, v in enumerate(vals):
        out = out + jnp.where(r == h, v, 0.0)
    return out


def _per_head_row(vals, width):
    grp = lax.broadcasted_iota(I32, (1, width), 1) // HEAD_DIM
    out = jnp.zeros((1, width), F32)
    for h, v in enumerate(vals):
        out = out + jnp.where(grp == h, v, 0.0)
    return out


def _head_lane_sums(row):
    grp = lax.broadcasted_iota(I32, row.shape, 1) // HEAD_DIM
    return [jnp.sum(jnp.where(grp == h, row, 0.0), axis=1, keepdims=True) for h in range(N_HEADS)]


def _cols_of(row):
    return jnp.concatenate([jnp.broadcast_to(row[:, j:j + LANES], (LANES, LANES)).T
                            for j in range(0, row.shape[1], LANES)], axis=0)


def _row_of(col):
    return jnp.concatenate([jnp.broadcast_to(col[j:j + LANES], (LANES, LANES)).T[0:1, :]
                            for j in range(0, col.shape[0], LANES)], axis=1)


def _sublane_group_sum(x):
    return jnp.sum(x.reshape(x.shape[0] // SUBLANES, SUBLANES, x.shape[1]), axis=0)


def _fold_matrix(n_pages):
    r = lax.broadcasted_iota(I32, (n_pages * SUBLANES, n_pages * N_HEADS * SUBLANES), 0)
    c = lax.broadcasted_iota(I32, (n_pages * SUBLANES, n_pages * N_HEADS * SUBLANES), 1)
    blk = c // SUBLANES
    return ((blk // N_HEADS == r // SUBLANES) & (blk % N_HEADS == r % SUBLANES)).astype(F32)


def _fox_decode_kernel(pt_ref, q_ref, kvn_ref, smn_ref, *refs, n_pages):
    del pt_ref
    kv_refs = refs[0:n_pages]
    lf_refs = refs[n_pages:2 * n_pages]
    o_ref, lf_sc, part_sc = refs[2 * n_pages:]
    R = n_pages * SUBLANES
    PG = kv_refs[0].shape[-1]
    row = pl.ds(pl.program_id(0) % SUBLANES, 1)
    q_row = q_ref[row, :]
    kvn = kvn_ref[row, :]
    smn = smn_ref[row, :]
    q_cols = _cols_of(q_row)

    lf_sc[...] = jnp.zeros_like(lf_sc)
    for p in range(n_pages):
        lf_sc[p * SUBLANES:p * SUBLANES + N_HEADS, :] = lf_refs[p][...]
    lft = lf_sc[...]
    k0 = lax.broadcasted_iota(I32, (PG, PG), 0)
    k1 = lax.broadcasted_iota(I32, (PG, PG), 1)
    within = _dot01_r(lft, k0 > k1)
    tot = jnp.broadcast_to(jnp.sum(lft, axis=1, keepdims=True), (R, PG))
    r0 = lax.broadcasted_iota(I32, (R, R), 0)
    r1 = lax.broadcasted_iota(I32, (R, R), 1)
    later = (r1 % SUBLANES == r0 % SUBLANES) & (r1 // SUBLANES > r0 // SUBLANES)
    cross = _dot01(later, tot)
    rr = lax.broadcasted_iota(I32, (R, 1), 0) % SUBLANES
    newcol = jnp.zeros((R, 1), F32)
    for h in range(N_HEADS):
        newcol = newcol + jnp.where(rr == h, smn[:, SM_FF + h:SM_FF + h + 1], 0.0)
    bias = (within + cross + newcol).reshape(n_pages, SUBLANES, PG)

    for p in range(n_pages):
        for h in range(N_HEADS):
            g = p * N_HEADS + h
            part_sc[g * SUBLANES:(g + 1) * SUBLANES, :] = _sublane_group_sum(
                kv_refs[p][0, h] * q_cols[h * HEAD_DIM:(h + 1) * HEAD_DIM])
    s = _dot01(_fold_matrix(n_pages), part_sc[...])
    s3 = s.reshape(n_pages, SUBLANES, PG) * SCALE + bias
    s_new = _per_head_col(_head_lane_sums(q_row * kvn[:, 0:D_FOX])) * SCALE
    m = jnp.max(jnp.max(s3, axis=2, keepdims=True), axis=0)
    m = jnp.maximum(m, s_new)
    p3 = jnp.exp(s3 - m[None])
    pn = jnp.exp(s_new - m)
    inv = 1.0 / (jnp.sum(jnp.sum(p3, axis=2, keepdims=True), axis=0) + pn)
    o_cols = []
    for h in range(N_HEADS):
        acc = jnp.zeros((HEAD_DIM, PG), F32)
        for p in range(n_pages):
            acc = acc + kv_refs[p][1, h] * p3[p, h:h + 1, :]
        o_cols.append(jnp.sum(acc, axis=1, keepdims=True) * inv[h:h + 1])
    w_new = _per_head_row([pn[h:h + 1] * inv[h:h + 1] for h in range(N_HEADS)], D_FOX)
    o_ref[row, :] = _row_of(jnp.concatenate(o_cols, axis=0)) + w_new * kvn[:, D_FOX:2 * D_FOX]


def _fox_decode(l, pt_flat, n_pages, fq, fkv, small, kv_cache_t, lft_cache):
    DB = fq.shape[0]
    PG = kv_cache_t.shape[-1]
    page = lambda p, nz: (lambda b, pt: (l, pt[b * n_pages + p]) + (0,) * nz)
    rows8 = lambda w: pl.BlockSpec((SUBLANES, w), lambda b, pt: (b // SUBLANES, 0))
    return pl.pallas_call(
        functools.partial(_fox_decode_kernel, n_pages=n_pages),
        grid_spec=pltpu.PrefetchScalarGridSpec(
            num_scalar_prefetch=1,
            grid=(DB,),
            in_specs=[rows8(D_FOX), rows8(2 * D_FOX), rows8(LANES)]
            + [pl.BlockSpec((None, None, 2, N_HEADS, HEAD_DIM, PG), page(p, 4)) for p in range(n_pages)]
            + [pl.BlockSpec((None, None, N_HEADS, PG), page(p, 2)) for p in range(n_pages)],
            out_specs=rows8(D_FOX),
            scratch_shapes=[pltpu.VMEM((n_pages * SUBLANES, PG), F32),
                            pltpu.VMEM((n_pages * N_HEADS * SUBLANES, PG), F32)],
        ),
        out_shape=jax.ShapeDtypeStruct((DB, D_FOX), F32),
        compiler_params=_params(("arbitrary",)),
        name="fox_decode",
    )(pt_flat, fq, fkv, small, *([kv_cache_t] * n_pages), *([lft_cache] * n_pages))


def _nsa_decode_kernel(pt_ref, q_ref, qr_ref, rown_ref, winn_ref, sm_ref, win_ref, *refs,
                       n_pages, past_len):
    del pt_ref
    pg_refs = refs[0:n_pages]
    cmp_refs = refs[n_pages:2 * n_pages]
    o_ref, nw_ref, cmp_sc, qc_sc, part_sc, sw_sc = refs[2 * n_pages:]
    R = n_pages * SUBLANES
    PG = pg_refs[0].shape[-1]
    WB = win_ref.shape[-1]
    per_page = PG // CMP_LEN
    assert per_page <= SUBLANES and PG == 2 * SEL_LEN and R == LANES
    jt = past_len // SEL_LEN
    row = pl.ds(pl.program_id(0) % SUBLANES, 1)
    q_row = q_ref[row, :]
    qr_row = qr_ref[row, :]
    rown = rown_ref[row, :]
    winn = winn_ref[row, :]
    smn = sm_ref[row, :]
    qr_cols = _cols_of(qr_row)
    rep4 = lambda r64: jnp.concatenate([r64] * N_HEADS, axis=1)

    qc_sc[...] = jnp.zeros_like(qc_sc)
    for h in range(N_HEADS):
        qc_sc[h:h + 1, 0:HEAD_DIM] = q_row[:, h * HEAD_DIM:(h + 1) * HEAD_DIM]
    head_row = lax.broadcasted_iota(I32, (SUBLANES, 1), 0) < N_HEADS

    cmp_sc[...] = jnp.zeros_like(cmp_sc)
    for p in range(n_pages):
        cmp_sc[p * SUBLANES:p * SUBLANES + per_page, :] = cmp_refs[p][...]
    cmpa = cmp_sc[...]
    lane = lax.broadcasted_iota(I32, (1, R), 1)
    blk = per_page * (lane // SUBLANES) + lane % SUBLANES
    complete = (lane % SUBLANES < per_page) & ((blk + 1) * CMP_LEN - 1 <= past_len)
    s = _dot_nt_hilo(qc_sc[...], cmpa) * SCALE
    s = jnp.where(complete, s, NEG)
    e = jnp.exp(s - jnp.max(s, axis=-1, keepdims=True))
    pc = e / jnp.sum(e, axis=-1, keepdims=True) * complete.astype(F32)
    vcb_t = cmpa.T[HEAD_DIM:2 * HEAD_DIM, :]
    o_cmp = [jnp.sum(vcb_t * pc[h:h + 1, :], axis=1, keepdims=True) for h in range(N_HEADS)]

    imp_c = jnp.sum(jnp.where(head_row, pc, 0.0), axis=0, keepdims=True)
    imp_s = imp_c + pltpu.roll(imp_c, R - 1, 1)
    cand = (lane % SUBLANES == 0) | (lane % SUBLANES == 2)
    jsel = 2 * (lane // SUBLANES) + (lane % SUBLANES) // 2
    score = jnp.where(jsel == jt, 2.0 * SEL_FORCE,
                      jnp.where((jsel == 0) | (jsel == jt - 1), SEL_FORCE,
                                jnp.where(jsel <= jt, imp_s + 0.0, -1.0)))
    score_b = jnp.broadcast_to(score, (R, R))
    key_row = _order_key(score_b)
    key_col = _order_key(score_b.T)
    l0 = lax.broadcasted_iota(I32, (R, R), 0)
    l1 = lax.broadcasted_iota(I32, (R, R), 1)
    cand_col = (l0 % SUBLANES == 0) | (l0 % SUBLANES == 2)
    beats = cand_col & (key_col > jnp.where(l0 < l1, key_row - 1, key_row))
    cnt = jnp.sum(beats.astype(I32), axis=0, keepdims=True)
    sel_row = (cand & (cnt < TOP_N - 1)).astype(F32)
    sel_col = jnp.broadcast_to(sel_row, (R, R)).T
    half = ((l0 % SUBLANES == 0) & (l1 < SEL_LEN)) | ((l0 % SUBLANES == 2) & (l1 >= SEL_LEN))
    z = jnp.where(half, sel_col, 0.0)
    same_page = (l1 // SUBLANES == l0 // SUBLANES).astype(BF16)
    picked = jnp.dot(same_page, z.astype(BF16), preferred_element_type=F32)
    picked = picked.reshape(n_pages, SUBLANES, PG) > 0.5

    for p in range(n_pages):
        ks_t = pg_refs[p][2]
        for h in range(N_HEADS):
            g = p * N_HEADS + h
            part_sc[g * SUBLANES:(g + 1) * SUBLANES, :] = _sublane_group_sum(
                ks_t * qr_cols[h * HEAD_DIM:(h + 1) * HEAD_DIM])
    s = _dot01(_fold_matrix(n_pages), part_sc[...])
    s3 = jnp.where(picked, s.reshape(n_pages, SUBLANES, PG) * SCALE, NEG)
    s_new = _per_head_col(_head_lane_sums(qr_row * rep4(rown[:, 2 * HEAD_DIM:3 * HEAD_DIM]))) * SCALE
    m = jnp.maximum(jnp.max(jnp.max(s3, axis=2, keepdims=True), axis=0), s_new)
    p3 = jnp.exp(s3 - m[None])
    pn = jnp.exp(s_new - m)
    inv = 1.0 / (jnp.sum(jnp.sum(p3, axis=2, keepdims=True), axis=0) + pn)
    o_sel = []
    for h in range(N_HEADS):
        acc = jnp.zeros((HEAD_DIM, PG), F32)
        for p in range(n_pages):
            acc = acc + pg_refs[p][3] * p3[p, h:h + 1, :]
        o_sel.append(jnp.sum(acc, axis=1, keepdims=True) * inv[h:h + 1])

    kw_t = win_ref[0]
    vw_t = win_ref[1]
    wlane = lax.broadcasted_iota(I32, (1, WB), 1)
    wpos = past_len - WB + wlane
    wd = past_len - wpos
    wok = (wd >= 0) & (wd < WINDOW) & (wpos >= 0)
    sw_sc[...] = jnp.zeros_like(sw_sc)
    for h in range(N_HEADS):
        qh = qr_cols[h * HEAD_DIM:(h + 1) * HEAD_DIM]
        sw_sc[h:h + 1, :] = jnp.sum(kw_t * jnp.concatenate([qh] * (WB // LANES), axis=1), axis=0, keepdims=True)
    sw = jnp.where(wok, sw_sc[...] * SCALE, NEG)
    sw_new = _per_head_col(_head_lane_sums(qr_row * rep4(winn[:, 0:HEAD_DIM]))) * SCALE
    mw = jnp.maximum(jnp.max(sw, axis=-1, keepdims=True), sw_new)
    ew = jnp.exp(sw - mw)
    en = jnp.exp(sw_new - mw)
    invw = 1.0 / (jnp.sum(ew, axis=-1, keepdims=True) + en)

    gate = lambda h, c: smn[:, SM_NG + 3 * h + c:SM_NG + 3 * h + c + 1]
    o_cols = []
    for h in range(N_HEADS):
        o_win = jnp.sum(vw_t * ew[h:h + 1, :], axis=1, keepdims=True) * invw[h:h + 1]
        o_cols.append(gate(h, 0) * o_cmp[h] + gate(h, 1) * o_sel[h] + gate(h, 2) * o_win)
    w_sel = _per_head_row([gate(h, 1) * pn[h:h + 1] * inv[h:h + 1] for h in range(N_HEADS)], D_NSA)
    w_win = _per_head_row([gate(h, 2) * en[h:h + 1] * invw[h:h + 1] for h in range(N_HEADS)], D_NSA)
    o_ref[row, :] = (_row_of(jnp.concatenate(o_cols, axis=0))
                     + w_sel * rep4(rown[:, 3 * HEAD_DIM:4 * HEAD_DIM])
                     + w_win * rep4(winn[:, HEAD_DIM:2 * HEAD_DIM]))
    last = lax.broadcasted_iota(I32, (HEAD_DIM, WB), 1) == WB - 1
    winn_cols = _cols_of(winn)
    for s in range(2):
        new_col = winn_cols[s * HEAD_DIM:(s + 1) * HEAD_DIM, 0:1]
        nw_ref[s] = jnp.where(last, new_col, pltpu.roll(win_ref[s], WB - 1, 1))


def _nsa_decode(l, pt_flat, n_pages, past_len, nq, nqr, rows, win, small, nsa_cache_t, cmp_pool, win_state_t):
    DB = nq.shape[0]
    PG = nsa_cache_t.shape[-1]
    WB = win_state_t.shape[-1]
    page = lambda p: (lambda b, pt: (l, pt[b * n_pages + p], 0, 0, 0))
    cpage = lambda p: (lambda b, pt: (pt[b * n_pages + p], 0, 0))
    rows8 = lambda w: pl.BlockSpec((SUBLANES, w), lambda b, pt: (b // SUBLANES, 0))
    return pl.pallas_call(
        functools.partial(_nsa_decode_kernel, n_pages=n_pages, past_len=past_len),
        grid_spec=pltpu.PrefetchScalarGridSpec(
            num_scalar_prefetch=1,
            grid=(DB,),
            in_specs=[rows8(D_NSA), rows8(D_NSA), rows8(4 * HEAD_DIM), rows8(2 * HEAD_DIM), rows8(LANES),
                      pl.BlockSpec((None, None, 2, HEAD_DIM, WB), lambda b, pt: (l, b, 0, 0, 0))]
            + [pl.BlockSpec((None, None, 4, HEAD_DIM, PG), page(p)) for p in range(n_pages)]
            + [pl.BlockSpec((None, PG // CMP_LEN, LANES), cpage(p)) for p in range(n_pages)],
            out_specs=[rows8(D_NSA),
                       pl.BlockSpec((None, 2, HEAD_DIM, WB), lambda b, pt: (b, 0, 0, 0))],
            scratch_shapes=[pltpu.VMEM((n_pages * SUBLANES, LANES), F32),
                            pltpu.VMEM((SUBLANES, LANES), F32),
                            pltpu.VMEM((n_pages * N_HEADS * SUBLANES, PG), F32),
                            pltpu.VMEM((SUBLANES, WB), F32)],
        ),
        out_shape=[jax.ShapeDtypeStruct((DB, D_NSA), F32),
                   jax.ShapeDtypeStruct((DB, 2, HEAD_DIM, WB), F32)],
        compiler_params=_params(("arbitrary",)),
        name="nsa_decode",
    )(pt_flat, nq, nqr, rows, win, small, win_state_t,
      *([nsa_cache_t] * n_pages), *([cmp_pool] * n_pages))


def _gla_decode_kernel(q_ref, k_ref, g_ref, v_ref, s_ref, o_ref, so_ref):
    s_new = jnp.exp(g_ref[...]) * s_ref[...] + k_ref[...] * v_ref[...]
    so_ref[...] = s_new
    o_ref[...] = jnp.sum((q_ref[...] * SCALE) * s_new, axis=2, keepdims=True)


def _gla_decode(l, gqk, gv, glog, state, nb):
    DB = gqk.shape[0]
    col = lambda a: a.reshape(DB, N_HEADS, HEAD_DIM, 1)
    cspec = pl.BlockSpec((nb, N_HEADS, HEAD_DIM, 1), lambda i: (i, 0, 0, 0))
    vspec = pl.BlockSpec((nb, N_HEADS, 1, GLA_DV), lambda i: (i, 0, 0, 0))
    sspec = pl.BlockSpec((nb, N_HEADS, HEAD_DIM, GLA_DV), lambda i: (i, 0, 0, 0))
    o, s_new = pl.pallas_call(
        _gla_decode_kernel,
        grid=(DB // nb,),
        in_specs=[cspec, cspec, cspec, vspec,
                  pl.BlockSpec((None, nb, N_HEADS, HEAD_DIM, GLA_DV), lambda i: (l, i, 0, 0, 0))],
        out_specs=[vspec, sspec],
        out_shape=[jax.ShapeDtypeStruct((DB, N_HEADS, 1, GLA_DV), F32),
                   jax.ShapeDtypeStruct((DB, N_HEADS, HEAD_DIM, GLA_DV), F32)],
        compiler_params=_params(("parallel",)),
        name="gla_decode",
    )(col(gqk[:, 0:D_FOX]), col(gqk[:, D_FOX:2 * D_FOX]), col(glog),
      gv.reshape(DB, N_HEADS, 1, GLA_DV), state)
    return o.reshape(DB, D_GLA), s_new


def _mix_sample(x, l, p, tab, pt_flat, n_pages, past_len, fox_kv_t, fox_lft_c, nsa_t, cmp_blocks,
                win_state_t, gla_state):
    DB = x.shape[0]
    (fq, fkv, nq, nqr, rows, win, gqk, gv, gog, glog, small) = _in_projection(
        x, 0, DB, DB, p['g'], p['w_r32'], p['sb'], p['wgk32'], p['bgk'], tab, DB)
    o_fox = _fox_decode(l, pt_flat, n_pages, fq, fkv, small, fox_kv_t, fox_lft_c)
    n_pool, PG = nsa_t.shape[1], nsa_t.shape[-1]
    per_layer = n_pool * (PG // CMP_LEN)
    cmp_pool = _compress(cmp_blocks, p['pe_cmp'], p['w_cmp'], 256, False, row0=l * per_layer, n_rows=per_layer)
    cmp_pool = cmp_pool.reshape(n_pool, PG // CMP_LEN, LANES)
    o_nsa, new_win = _nsa_decode(l, pt_flat, n_pages, past_len, nq, nqr, rows, win, small,
                                 nsa_t, cmp_pool, win_state_t)
    o_gla, g_state = _gla_decode(l, gqk, gv, glog, gla_state, 8)
    x_new = _out_projection(x, o_fox, o_nsa, o_gla, gog, p['gn'], p['w_out32'], DB)
    return x_new, dict(fkv=fkv, small=small, rows=rows, win=new_win, g_state=g_state)


def kernel(x_prompt, x_sample, cache_fox_kv, cache_fox_logf, cache_nsa_kv, state_nsa_win, state_gla,
           page_table, norm_mix_g, w_in, b_fox_f, w_cmp, pe_cmp, w_gla_gk, b_gla_gk, g_gla_norm, w_out,
           norm_ffn_g, dense_w_gate, dense_w_up, dense_w_down, moe_w_router, moe_w_gate, moe_w_up,
           moe_w_down, final_norm_g):
    B, S, _ = x_prompt.shape
    DB, TN, _ = x_sample.shape
    assert TN == 1
    depth, n_pool, PG = cache_fox_kv.shape[0:3]
    n_pages = page_table.shape[1]
    past_len = n_pages * PG
    WB = state_nsa_win.shape[2]
    xp = x_prompt.reshape(B * S, D_MODEL)
    xs = x_sample.reshape(DB, D_MODEL)
    tab_p = _rope_table(jnp.arange(S))
    tab_s = _rope_table(jnp.full((DB,), past_len, I32))
    pt_flat = page_table.reshape(-1).astype(I32)
    fox_kv_t = jnp.transpose(cache_fox_kv, (0, 1, 3, 4, 5, 2))
    fox_lft_c = jnp.swapaxes(cache_fox_logf, 2, 3)
    nsa_t = jnp.transpose(cache_nsa_kv, (0, 1, 3, 4, 2))
    win_state_t = jnp.transpose(state_nsa_win, (0, 1, 3, 4, 2))
    n_cmp = depth * n_pool * (PG // CMP_LEN)
    cmp_blocks = _block_major(cache_nsa_kv[:, :, :, 0:2, :].reshape(n_cmp * CMP_LEN, 2, HEAD_DIM), n_cmp)
    gfin = final_norm_g.reshape(1, D_MODEL)
    cp, cs = [], []
    for l in range(depth):
        p = _layer_mix_params(l, norm_mix_g, w_in, b_fox_f, w_cmp, pe_cmp, w_gla_gk, b_gla_gk,
                              g_gla_norm, w_out)
        gf = norm_ffn_g[l].reshape(1, D_MODEL)
        i = l // 2
        moe = l % 2 == 1
        xp, c = _mix_prompt(xp, B, S, p, tab_p, 512, 512, 512, 256)
        cp.append(c)
        if moe:
            wr = jnp.zeros((D_MODEL, LANES), F32).at[:, 0:N_EXPERTS].set(moe_w_router[i])
            experts = (moe_w_gate[i], moe_w_up[i], moe_w_down[i])
            disp_p = _moe_dispatch(xp, gf, wr, 512, 1024)
        xs, c = _mix_sample(xs, l, p, tab_s, pt_flat, n_pages, past_len, fox_kv_t, fox_lft_c, nsa_t,
                            cmp_blocks, win_state_t, state_gla)
        cs.append(c)
        if moe:
            disp_s = _moe_dispatch(xs, gf, wr, DB, LANES)
            fin = gfin if l == depth - 1 else None
            xp = _moe_finish(disp_p, *experts, 512, 1024, 512, fin)
            xs = _moe_finish(disp_s, *experts, DB, LANES, 896, fin)
        else:
            wg, wu, wd = (dense_w_gate[i].astype(BF16), dense_w_up[i].astype(BF16),
                          dense_w_down[i].astype(BF16))
            xp = _dense_ffn(xp, gf, wg, wu, wd, 1024, 512)
            xs = _dense_ffn(xs, gf, dense_w_gate[i], dense_w_up[i], dense_w_down[i], DB, 512)
    if depth % 2 == 1:
        xp = _final_norm(xp, gfin, 512)
        xs = _final_norm(xs, gfin, DB)
    y_p = xp.reshape(B, S, D_MODEL)
    y_s = xs.reshape(DB, 1, D_MODEL)
    wp = min(WINDOW, S)
    st = lambda key, group: jnp.stack([c[key] for c in group])
    return (y_p, y_s,
            st('fkv', cp).reshape(depth, B, 2, N_HEADS, HEAD_DIM, S).transpose(0, 1, 5, 2, 3, 4),
            st('small', cp).transpose(0, 1, 3, 2),
            st('rows', cp).reshape(depth, B, 4, HEAD_DIM, S).transpose(0, 1, 4, 2, 3),
            st('win', cp).reshape(depth, B, 2, HEAD_DIM, wp).transpose(0, 1, 4, 2, 3),
            st('g_state', cp),
            st('fkv', cs).reshape(depth, DB, 1, 2, N_HEADS, HEAD_DIM),
            st('small', cs)[:, :, SM_FF:SM_FF + N_HEADS].reshape(depth, DB, 1, N_HEADS),
            st('rows', cs).reshape(depth, DB, 1, 4, HEAD_DIM),
            jnp.transpose(st('win', cs), (0, 1, 4, 2, 3)),
            st('g_state', cs))
```

```python
import functools

import jax
import jax.numpy as jnp
from jax import lax
from jax.experimental import pallas as pl
from jax.experimental.pallas import tpu as pltpu
from jax.experimental.pallas import tpu_sc as plsc

F32 = jnp.float32
BF16 = jnp.bfloat16
I32 = jnp.int32
HI = lax.Precision.HIGHEST

D_MODEL = 1024
HEAD_DIM = 64
N_HEADS = 4
D_FOX = N_HEADS * HEAD_DIM
D_NSA = N_HEADS * HEAD_DIM
GLA_DV = 128
D_GLA = N_HEADS * GLA_DV
GLA_RANK = 16
GLA_TAU = 16.0
GLA_CHUNK = 64
CMP_LEN = 32
SEL_LEN = 64
TOP_N = 16
WINDOW = 512
ROPE_THETA = 500000.0
ROPE_DIM = HEAD_DIM // 4
ROPE_HALF = ROPE_DIM // 2
D_FF = 3584
N_EXPERTS = 8
EPS = 1e-6
SEL_FORCE = 1e9
NEG = -1e30
SCALE = HEAD_DIM ** -0.5
LOG2E = 1.4426950408889634

LANES = 128
SUBLANES = 8
VMEM_LIMIT = 56 * 1024 * 1024

C_FQ = 0
C_FKV = 256
C_NQ = 768
C_NKV = 1024
C_GQK = 1408
C_GV = 1920
C_GOG = 2432
C_SMALL = 2944
C_END = 3072
SM_FF = 0
SM_NG = 4
SM_GLR = 16

NT = (((1,), (1,)), ((), ()))


def _params(sem):
    return pltpu.CompilerParams(dimension_semantics=sem, vmem_limit_bytes=VMEM_LIMIT)


def _rms(x, g):
    ms = jnp.mean(x * x, axis=-1, keepdims=True)
    return x * lax.rsqrt(ms + EPS) * g


def _sigmoid(x):
    return 1.0 / (1.0 + jnp.exp(-x))


def _log_sigmoid(x):
    return -(jnp.maximum(-x, 0.0) + jnp.log1p(jnp.exp(-jnp.abs(x))))


def _silu(x):
    return x * _sigmoid(x)


def _bdot(a, b):
    return jnp.dot(a.astype(BF16), b.astype(BF16), preferred_element_type=F32)


def _bdot_nt(a, b):
    return lax.dot_general(a.astype(BF16), b.astype(BF16), NT, preferred_element_type=F32)


def _split3(x):
    h = x.astype(BF16)
    r = x - h.astype(F32)
    m = r.astype(BF16)
    return h, m, (r - m.astype(F32)).astype(BF16)


def _dot01(m01, x):
    mb = m01.astype(BF16)
    h, m, l = _split3(x)
    return (jnp.dot(mb, h, preferred_element_type=F32) + jnp.dot(mb, m, preferred_element_type=F32)
            + jnp.dot(mb, l, preferred_element_type=F32))


def _dot01_r(x, m01):
    mb = m01.astype(BF16)
    h, m, l = _split3(x)
    return (jnp.dot(h, mb, preferred_element_type=F32) + jnp.dot(m, mb, preferred_element_type=F32)
            + jnp.dot(l, mb, preferred_element_type=F32))


def _dot01_nt(m01, x):
    mb = m01.astype(BF16)
    h, m, l = _split3(x)
    return (lax.dot_general(mb, h, NT, preferred_element_type=F32)
            + lax.dot_general(mb, m, NT, preferred_element_type=F32)
            + lax.dot_general(mb, l, NT, preferred_element_type=F32))


def _dot_nt_hilo(a, b):
    ah = a.astype(BF16)
    al = (a - ah.astype(F32)).astype(BF16)
    bh = b.astype(BF16)
    bl = (b - bh.astype(F32)).astype(BF16)
    return (lax.dot_general(ah, bh, NT, preferred_element_type=F32)
            + lax.dot_general(al, bh, NT, preferred_element_type=F32)
            + lax.dot_general(ah, bl, NT, preferred_element_type=F32))


def _wdot(a, w):
    if w.dtype == F32:
        return jnp.dot(a.astype(F32), w, precision=HI, preferred_element_type=F32)
    return jnp.dot(a.astype(BF16), w, preferred_element_type=F32)


def _rope128(x, a, bp, bm):
    return x * a + pltpu.roll(x, ROPE_HALF, 1) * bp + pltpu.roll(x, LANES - ROPE_HALF, 1) * bm


def _inproj_kernel(x_ref, g_ref, w_ref, sb_ref, wgk_ref, bgk_ref, tab_ref, *refs, feature_major, n_carried):
    (fq_ref, fkv_ref, nq_ref, nqr_ref, rows_ref, win_ref,
     gqk_ref, gv_ref, gog_ref, glog_ref, small_ref, *extra) = refs[n_carried:]
    h = _rms(x_ref[...], g_ref[...]).astype(w_ref.dtype)

    def put(ref, v):
        ref[...] = v.T if feature_major else v

    def mm(a, b):
        return _wdot(h, w_ref[:, a:b])

    fq_ref[...] = mm(C_FQ, C_FKV)
    put(fkv_ref, mm(C_FKV, C_NQ))
    tab = tab_ref[...]
    ab, pb, mb = tab[:, 0:128], tab[:, 128:256], tab[:, 256:384]
    af, pf, mf = tab[:, 384:512], tab[:, 512:640], tab[:, 640:768]
    nq = mm(C_NQ, C_NKV)
    nq_ref[...] = nq
    nqr_ref[:, 0:128] = _rope128(nq[:, 0:128], ab, pb, mb)
    nqr_ref[:, 128:256] = _rope128(nq[:, 128:256], ab, pb, mb)
    nkv = mm(C_NKV, C_GQK)
    put(rows_ref, jnp.concatenate([nkv[:, 0:128], _rope128(nkv[:, 128:256], af, pf, mf)], axis=1))
    put(win_ref, _rope128(nkv[:, 256:384], af, pf, mf))
    gqk_ref[...] = mm(C_GQK, C_GV)
    gv_ref[...] = mm(C_GV, C_GOG)
    gog_ref[...] = mm(C_GOG, C_SMALL)
    sm = mm(C_SMALL, C_END)
    glog_ref[...] = _log_sigmoid(_wdot(sm, wgk_ref[...]) + bgk_ref[...]) * (1.0 / GLA_TAU)
    smb = sm + sb_ref[...]
    lane = lax.broadcasted_iota(I32, smb.shape, 1)
    small = jnp.where(lane < SM_NG, _log_sigmoid(smb), _sigmoid(smb))
    small_ref[...] = small
    if feature_major:
        extra[0][...] = small.T[0:SUBLANES, :]


def _in_projection(x_all, row0, n_rows, tm, g, w_r, sb, wgk, bgk, tab, tab_period, batch=None, layer=None,
                   prev=None):
    assert n_rows % tm == 0 and row0 % tm == 0 and tab_period % tm == 0
    nt = n_rows // tm
    b0 = row0 // tm
    npd = tab_period // tm
    widths = (256, 512, 256, 256, 256, 128, 512, 512, 512, 256, 128)
    FKV, ROWS, WIN = 1, 4, 5
    full = lambda shape: pl.BlockSpec(shape, lambda i: (0, 0))
    row_spec = lambda w: pl.BlockSpec((tm, w), lambda i: (i, 0))
    carried = [] if prev is None else list(prev)
    if batch is not None:
        B, S, depth = batch
        per = S // tm
        assert n_rows == B * S and S % tm == 0
        t_spec = lambda w: pl.BlockSpec((None, w, tm), lambda i: (i // per, 0, i % per))
        l_spec = lambda w: pl.BlockSpec((None, None, w, tm), lambda i: (layer, i // per, 0, i % per))
        out_specs = [l_spec(w) if k in (FKV, ROWS) else t_spec(w) if k == WIN else row_spec(w)
                     for k, w in enumerate(widths)] + [t_spec(SUBLANES)]
        out_shape = [jax.ShapeDtypeStruct((depth, B, w, S) if k in (FKV, ROWS) else (B, w, S) if k == WIN
                                          else (n_rows, w), F32)
                     for k, w in enumerate(widths)] + [jax.ShapeDtypeStruct((B, SUBLANES, S), F32)]
    else:
        out_specs = [row_spec(w) for w in widths]
        out_shape = [jax.ShapeDtypeStruct((n_rows, w), F32) for w in widths]
    return pl.pallas_call(
        functools.partial(_inproj_kernel, feature_major=batch is not None, n_carried=len(carried)),
        grid=(nt,),
        in_specs=[
            pl.BlockSpec((tm, D_MODEL), lambda i: (b0 + i, 0)),
            full((1, D_MODEL)),
            full((D_MODEL, C_END)),
            full((1, LANES)),
            full((LANES, 256)),
            full((1, 256)),
            pl.BlockSpec((tm, 768), lambda i: (i % npd, 0)),
        ] + [pl.BlockSpec(memory_space=pl.ANY)] * len(carried),
        out_specs=out_specs,
        out_shape=out_shape,
        input_output_aliases={7: FKV, 8: ROWS} if carried else {},
        compiler_params=_params(("parallel",)),
        name="in_projection",
    )(x_all, g, w_r, sb, wgk, bgk, tab, *carried)


def _cumsum_kernel(sm_ref, cr_ref, carry):
    t = pl.program_id(1)
    ts = sm_ref.shape[1]

    @pl.when(t == 0)
    def _():
        carry[...] = jnp.zeros_like(carry)

    r = lax.broadcasted_iota(I32, (ts, ts), 0)
    c = lax.broadcasted_iota(I32, (ts, ts), 1)
    cs = _dot01_r(sm_ref[...], r <= c) + carry[...]
    carry[...] = cs[:, ts - 1:ts]
    cr_ref[...] = cs * LOG2E


def _fox_cumsum(small_t, ts):
    B, _, S = small_t.shape
    spec = pl.BlockSpec((None, SUBLANES, ts), lambda b, t: (b, 0, t))
    return pl.pallas_call(
        _cumsum_kernel,
        grid=(B, S // ts),
        in_specs=[spec],
        out_specs=spec,
        out_shape=jax.ShapeDtypeStruct((B, SUBLANES, S), F32),
        scratch_shapes=[pltpu.VMEM((SUBLANES, 1), F32)],
        compiler_params=_params(("parallel", "arbitrary")),
        name="fox_cumsum",
    )(small_t)


def _pair_mask(shape, h, axis=1):
    return (lax.broadcasted_iota(I32, shape, axis) // HEAD_DIM) == (h % 2)


def _fox_prompt_kernel(q_ref, kv_ref, cr_ref, o_ref, *scratch):
    i = pl.program_id(1)
    j = pl.program_id(2)
    nk = pl.num_programs(2)
    tq = q_ref.shape[0]
    tk = kv_ref.shape[1]
    q_sc, m_sc, acc_sc = scratch[0:N_HEADS], scratch[N_HEADS:2 * N_HEADS], scratch[2 * N_HEADS:]

    @pl.when(j == 0)
    def _():
        for h in range(N_HEADS):
            m_sc[h][...] = jnp.full_like(m_sc[h], NEG)
            acc_sc[h][...] = jnp.zeros_like(acc_sc[h])
            slab = q_ref[:, (h // 2) * LANES:(h // 2 + 1) * LANES] * (SCALE * LOG2E)
            q_sc[h][...] = jnp.where(_pair_mask(slab.shape, h), slab, 0.0).astype(BF16)

    def tile(diagonal):
        k_slabs = [kv_ref[c0:c0 + LANES, :].astype(BF16) for c0 in (0, LANES)]
        if diagonal:
            mask = lax.broadcasted_iota(I32, (1, tk), 1) <= lax.broadcasted_iota(I32, (tq, 1), 0)
        for h in range(N_HEADS):
            v_slab = kv_ref[D_FOX + (h // 2) * LANES:D_FOX + (h // 2 + 1) * LANES, :]
            v_aug = jnp.where(_pair_mask(v_slab.shape, h, 0), v_slab, 1.0).astype(BF16)
            s = jnp.dot(q_sc[h][...], k_slabs[h // 2], preferred_element_type=F32) - cr_ref[h:h + 1, :]
            if diagonal:
                s = jnp.where(mask, s, NEG)
            m_old = m_sc[h][...]
            m_new = jnp.maximum(m_old, jnp.max(s, axis=-1, keepdims=True))
            p = jnp.exp2(s - m_new).astype(BF16)
            acc_sc[h][...] = (jnp.exp2(m_old - m_new) * acc_sc[h][...]
                              + lax.dot_general(p, v_aug, NT, preferred_element_type=F32))
            m_sc[h][...] = m_new

    @pl.when(j < i)
    def _():
        tile(False)

    @pl.when(j == i)
    def _():
        tile(True)

    @pl.when(j == nk - 1)
    def _():
        for h in range(N_HEADS):
            a = acc_sc[h][...]
            lo = (h % 2) * HEAD_DIM
            den = a[:, HEAD_DIM - lo:HEAD_DIM - lo + 1]
            o_ref[:, h * HEAD_DIM:(h + 1) * HEAD_DIM] = a[:, lo:lo + HEAD_DIM] / den


def _fox_prompt(fq, fkv_all, l, cr, B, S, t):
    n = S // t
    return pl.pallas_call(
        _fox_prompt_kernel,
        grid=(B, n, n),
        in_specs=[
            pl.BlockSpec((t, D_FOX), lambda b, i, j: (b * n + i, 0)),
            pl.BlockSpec((None, None, 2 * D_FOX, t), lambda b, i, j: (l, b, 0, jnp.minimum(i, j))),
            pl.BlockSpec((None, SUBLANES, t), lambda b, i, j: (b, 0, jnp.minimum(i, j))),
        ],
        out_specs=pl.BlockSpec((t, D_FOX), lambda b, i, j: (b * n + i, 0)),
        out_shape=jax.ShapeDtypeStruct((B * S, D_FOX), F32),
        scratch_shapes=([pltpu.VMEM((t, LANES), BF16)] * N_HEADS + [pltpu.VMEM((t, 1), F32)] * N_HEADS
                        + [pltpu.VMEM((t, LANES), F32)] * N_HEADS),
        compiler_params=_params(("parallel", "parallel", "arbitrary")),
        name="fox_prompt",
    )(fq, fkv_all, cr)


def _compress_kernel(x_ref, pe_ref, w_ref, o_ref, *, exact):
    for s in range(2):
        x = x_ref[s] + pe_ref[s]
        w = w_ref[s]
        if exact:
            y = jnp.dot(x, w, precision=HI, preferred_element_type=F32)
        else:
            xh = x.astype(BF16)
            xl = (x - xh.astype(F32)).astype(BF16)
            wh = w.astype(BF16)
            wl = (w - wh.astype(F32)).astype(BF16)
            y = (jnp.dot(xh, wh, preferred_element_type=F32) + jnp.dot(xl, wh, preferred_element_type=F32)
                 + jnp.dot(xh, wl, preferred_element_type=F32))
        o_ref[:, s * HEAD_DIM:(s + 1) * HEAD_DIM] = y


def _compress(x3, pe, w, tr, exact, row0=0, n_rows=None):
    K = x3.shape[2]
    R = x3.shape[1] if n_rows is None else n_rows
    assert R % tr == 0 and row0 % tr == 0
    b0 = row0 // tr
    return pl.pallas_call(
        functools.partial(_compress_kernel, exact=exact),
        grid=(R // tr,),
        in_specs=[pl.BlockSpec((2, tr, K), lambda i: (0, b0 + i, 0)),
                  pl.BlockSpec((2, 1, K), lambda i: (0, 0, 0)),
                  pl.BlockSpec((2, K, HEAD_DIM), lambda i: (0, 0, 0))],
        out_specs=pl.BlockSpec((tr, LANES), lambda i: (i, 0)),
        out_shape=jax.ShapeDtypeStruct((R, LANES), F32),
        compiler_params=_params(("parallel",)),
        name="nsa_compress",
    )(x3, pe, w)


def _block_major(kv, n_blocks):
    return kv.reshape(n_blocks, CMP_LEN, 2, HEAD_DIM).transpose(2, 0, 1, 3).reshape(
        2, n_blocks, CMP_LEN * HEAD_DIM)


def _order_key(x):
    b = lax.bitcast_convert_type(x, I32)
    return jnp.where(b < 0, b ^ jnp.int32(0x7FFFFFFF), b)


def _nsa_prompt_kernel(nq_ref, nqr_ref, sm_ref, cmp_ref, rows_ref, win_ref, o_ref, qc_sc, qx_sc, *, tk):
    QB = nq_ref.shape[0]
    S = rows_ref.shape[1]
    nb = cmp_ref.shape[0]
    nsel = S // SEL_LEN
    i = pl.program_id(1)
    qs = i * QB
    qpos = qs + lax.broadcasted_iota(I32, (QB, 1), 0)

    HQ = N_HEADS * QB
    lo_half = lax.broadcasted_iota(I32, (QB, LANES), 1) < HEAD_DIM

    def stack_heads(ref, scale, dst):
        for h in range(N_HEADS):
            slab = ref[:, (h // 2) * LANES:(h // 2 + 1) * LANES] * scale
            if h % 2:
                slab = pltpu.roll(slab, HEAD_DIM, 1)
            dst[h * QB:(h + 1) * QB, :] = jnp.where(lo_half, slab, 0.0).astype(dst.dtype)

    cmp = cmp_ref[...]
    n_l = lax.broadcasted_iota(I32, (1, nb), 1)
    complete = ((n_l + 1) * CMP_LEN - 1) <= qpos
    stack_heads(nq_ref, SCALE, qc_sc)
    s = _dot_nt_hilo(qc_sc[...], cmp).reshape(N_HEADS, QB, nb)
    s = jnp.where(complete[None], s, NEG)
    e = jnp.exp(s - jnp.max(s, axis=-1, keepdims=True))
    p = e / jnp.sum(e, axis=-1, keepdims=True) * complete.astype(F32)[None]
    o_cmp = _bdot(p.reshape(HQ, nb), cmp)
    psum = jnp.sum(p, axis=0)

    pj = lax.broadcasted_iota(I32, (nsel, nb), 0)
    pn = lax.broadcasted_iota(I32, (nsel, nb), 1)
    imp_t = _dot01_nt(pn // (SEL_LEN // CMP_LEN) == pj, psum)
    jt = (qs + lax.broadcasted_iota(I32, (1, QB), 1)) // SEL_LEN
    jj = lax.broadcasted_iota(I32, (nsel, 1), 0)
    score = jnp.where(jj == jt, 2.0 * SEL_FORCE,
                      jnp.where((jj == 0) | (jj == jt - 1), SEL_FORCE,
                                jnp.where(jj <= jt, imp_t + 0.0, -1.0)))
    key = _order_key(score)
    key_m1 = key - 1
    ngrp = nsel // SUBLANES
    sub = lax.broadcasted_iota(I32, (SUBLANES, QB), 0)
    kg = [key[r * SUBLANES:(r + 1) * SUBLANES, :] for r in range(ngrp)]
    kg1 = [key_m1[r * SUBLANES:(r + 1) * SUBLANES, :] for r in range(ngrp)]
    cnt = [jnp.zeros((SUBLANES, QB), I32) for _ in range(ngrp)]
    for jp in range(nsel):
        g = jp // SUBLANES
        row = key[jp:jp + 1, :]
        mixed = jnp.where(sub > (jp % SUBLANES), kg1[g], kg[g])
        for r in range(ngrp):
            thr = kg[r] if r < g else (kg1[r] if r > g else mixed)
            cnt[r] = cnt[r] + (row > thr).astype(I32)
    sel_t = jnp.concatenate([(c < TOP_N).astype(F32) for c in cnt], axis=0)
    if nsel < QB:
        sel_t = jnp.concatenate([sel_t, jnp.zeros((QB - nsel, QB), F32)], axis=0)
    sel = sel_t.T.astype(BF16)

    stack_heads(nqr_ref, SCALE * LOG2E, qx_sc)
    qx = qx_sc[...]

    def attend(valid, slab_t, m_old, acc_old):
        n = slab_t.shape[1]
        s = jnp.dot(qx, slab_t.astype(BF16), preferred_element_type=F32)
        s = jnp.where(valid[None], s.reshape(N_HEADS, QB, n), NEG).reshape(HQ, n)
        m_new = jnp.maximum(m_old, jnp.max(s, axis=-1, keepdims=True))
        p = jnp.exp2(s - m_new).astype(BF16)
        ones_k = lax.broadcasted_iota(I32, slab_t.shape, 0) < HEAD_DIM
        v_aug = jnp.where(ones_k, 1.0, slab_t).astype(BF16)
        acc = jnp.exp2(m_old - m_new) * acc_old + lax.dot_general(p, v_aug, NT, preferred_element_type=F32)
        return m_new, acc

    jrow = lax.broadcasted_iota(I32, (QB, 1), 0)

    def sel_tile(k0, m_old, acc_old, diagonal):
        kpos = k0 + lax.broadcasted_iota(I32, (1, tk), 1)
        expand = (jrow == kpos // SEL_LEN).astype(BF16)
        valid = jnp.dot(sel, expand, preferred_element_type=F32) > 0.5
        if diagonal:
            valid = valid & (kpos <= qpos)
        return attend(valid, rows_ref[2 * HEAD_DIM:4 * HEAD_DIM, pl.ds(k0, tk)], m_old, acc_old)

    n_full = qs // tk
    init = (jnp.full((HQ, 1), NEG, F32), jnp.zeros((HQ, LANES), F32))
    m_s, acc_s = lax.fori_loop(
        0, n_full, lambda t, c: sel_tile(pl.multiple_of(t * tk, tk), c[0], c[1], False), init)
    _, acc_s = sel_tile(pl.multiple_of(n_full * tk, tk), m_s, acc_s, True)

    wlen = WINDOW + QB
    w0 = pl.multiple_of(jnp.maximum(qs - WINDOW, 0), QB)
    wpos = w0 + lax.broadcasted_iota(I32, (1, wlen), 1)
    d = qpos - wpos
    _, acc_w = attend((d >= 0) & (d < WINDOW), win_ref[:, pl.ds(w0, wlen)],
                      jnp.full((HQ, 1), NEG, F32), jnp.zeros((HQ, LANES), F32))

    sm = sm_ref[...]
    for h in range(N_HEADS):
        rs = slice(h * QB, (h + 1) * QB)
        o_sel = acc_s[rs] * (1.0 / acc_s[rs, 0:1])
        o_win = acc_w[rs] * (1.0 / acc_w[rs, 0:1])
        c = SM_NG + 3 * h
        mix = sm[:, c:c + 1] * o_cmp[rs] + sm[:, c + 1:c + 2] * o_sel + sm[:, c + 2:c + 3] * o_win
        if h % 2 == 0:
            mix = pltpu.roll(mix, HEAD_DIM, 1)
        lo = (h % 2) * HEAD_DIM
        o_ref[:, h * HEAD_DIM:(h + 1) * HEAD_DIM] = mix[:, lo:lo + HEAD_DIM]


def _nsa_prompt(nq, nqr, small, cmp, rows_all, l, win, B, S, tk):
    QB = 128
    nq_t = S // QB
    nb = S // CMP_LEN
    assert S % tk == 0 and S >= WINDOW + QB
    return pl.pallas_call(
        functools.partial(_nsa_prompt_kernel, tk=tk),
        grid=(B, nq_t),
        in_specs=[
            pl.BlockSpec((QB, D_NSA), lambda b, i: (b * nq_t + i, 0)),
            pl.BlockSpec((QB, D_NSA), lambda b, i: (b * nq_t + i, 0)),
            pl.BlockSpec((QB, LANES), lambda b, i: (b * nq_t + i, 0)),
            pl.BlockSpec((nb, LANES), lambda b, i: (b, 0)),
            pl.BlockSpec((None, None, 4 * HEAD_DIM, S), lambda b, i: (l, b, 0, 0)),
            pl.BlockSpec((None, 2 * HEAD_DIM, S), lambda b, i: (b, 0, 0)),
        ],
        out_specs=pl.BlockSpec((QB, D_NSA), lambda b, i: (b * nq_t + i, 0)),
        out_shape=jax.ShapeDtypeStruct((B * S, D_NSA), F32),
        scratch_shapes=[pltpu.VMEM((N_HEADS * QB, LANES), F32), pltpu.VMEM((N_HEADS * QB, LANES), BF16)],
        compiler_params=_params(("parallel", "parallel")),
        name="nsa_prompt",
    )(nq, nqr, small, cmp, rows_all, win)


def _gla_prompt_kernel(qk_ref, v_ref, g_ref, o_ref, st_ref, s_sc):
    t = pl.program_id(1)
    nt = pl.num_programs(1)
    tc = qk_ref.shape[0]
    C = GLA_CHUNK

    @pl.when(t == 0)
    def _():
        s_sc[...] = jnp.zeros_like(s_sc)

    r = lax.broadcasted_iota(I32, (tc, tc), 0)
    c = lax.broadcasted_iota(I32, (tc, tc), 1)
    same = (r // C) == (c // C)
    causal = same & (c <= r)
    g = g_ref[...]
    gcum = _dot01(causal, g)
    g_t = g.T
    gcum_t = _dot01_r(g_t, same & (r <= c))
    gtot_t = _dot01_r(g_t, same)
    q_e = (qk_ref[:, 0:D_FOX] * SCALE * jnp.exp(gcum)).astype(BF16)
    k_e = (qk_ref[:, D_FOX:2 * D_FOX] * jnp.exp(-gcum)).astype(BF16)
    kd_t = (qk_ref[:, D_FOX:2 * D_FOX].T * jnp.exp(gtot_t - gcum_t)).astype(BF16)
    decay_t = jnp.exp(gtot_t)
    for h in range(N_HEADS):
        hs = slice(h * HEAD_DIM, (h + 1) * HEAD_DIM)
        v = v_ref[:, h * GLA_DV:(h + 1) * GLA_DV].astype(BF16)
        a = jnp.where(causal, lax.dot_general(q_e[:, hs], k_e[:, hs], NT, preferred_element_type=F32), 0.0)
        o_intra = jnp.dot(a.astype(BF16), v, preferred_element_type=F32)
        state = s_sc[h]
        for ci in range(tc // C):
            rs = slice(ci * C, (ci + 1) * C)
            o_ref[rs, h * GLA_DV:(h + 1) * GLA_DV] = (
                o_intra[rs] + jnp.dot(q_e[rs, hs], state.astype(BF16), preferred_element_type=F32))
            state = (decay_t[hs, ci * C:ci * C + 1] * state
                     + jnp.dot(kd_t[hs, rs], v[rs], preferred_element_type=F32))
        s_sc[h] = state

    @pl.when(t == nt - 1)
    def _():
        st_ref[...] = s_sc[...]


def _gla_prompt(gqk, gv, glog, B, S, tc):
    nt = S // tc
    return pl.pallas_call(
        _gla_prompt_kernel,
        grid=(B, nt),
        in_specs=[pl.BlockSpec((tc, 2 * D_FOX), lambda b, t: (b * nt + t, 0)),
                  pl.BlockSpec((tc, D_GLA), lambda b, t: (b * nt + t, 0)),
                  pl.BlockSpec((tc, D_FOX), lambda b, t: (b * nt + t, 0))],
        out_specs=[pl.BlockSpec((tc, D_GLA), lambda b, t: (b * nt + t, 0)),
                   pl.BlockSpec((None, N_HEADS, HEAD_DIM, GLA_DV), lambda b, t: (b, 0, 0, 0))],
        out_shape=[jax.ShapeDtypeStruct((B * S, D_GLA), F32),
                   jax.ShapeDtypeStruct((B, N_HEADS, HEAD_DIM, GLA_DV), F32)],
        scratch_shapes=[pltpu.VMEM((N_HEADS, HEAD_DIM, GLA_DV), F32)],
        compiler_params=_params(("parallel", "arbitrary")),
        name="gla_prompt",
    )(gqk, gv, glog)


def _outproj_kernel(x_ref, of_ref, on_ref, og_ref, gog_ref, gn_ref, w_ref, o_ref):
    acc = _wdot(of_ref[...], w_ref[0:D_FOX, :])
    acc = acc + _wdot(on_ref[...], w_ref[D_FOX:D_FOX + D_NSA, :])
    for h in range(N_HEADS):
        hs = slice(h * GLA_DV, (h + 1) * GLA_DV)
        z = _rms(og_ref[:, hs], gn_ref[...]) * _silu(gog_ref[:, hs])
        w0 = D_FOX + D_NSA + h * GLA_DV
        acc = acc + _wdot(z, w_ref[w0:w0 + GLA_DV, :])
    o_ref[...] = x_ref[...] + acc


def _out_projection(x, o_fox, o_nsa, o_gla, gog, gn, w_out, tm):
    T = x.shape[0]
    assert T % tm == 0
    row = lambda w: pl.BlockSpec((tm, w), lambda i: (i, 0))
    return pl.pallas_call(
        _outproj_kernel,
        grid=(T // tm,),
        in_specs=[row(D_MODEL), row(D_FOX), row(D_NSA), row(D_GLA), row(D_GLA),
                  pl.BlockSpec((1, GLA_DV), lambda i: (0, 0)),
                  pl.BlockSpec((D_MODEL, D_MODEL), lambda i: (0, 0))],
        out_specs=row(D_MODEL),
        out_shape=jax.ShapeDtypeStruct((T, D_MODEL), F32),
        compiler_params=_params(("parallel",)),
        name="out_projection",
    )(x, o_fox, o_nsa, o_gla, gog, gn, w_out)


def _dense_ffn_kernel(x_ref, g_ref, wg_ref, wu_ref, wd_ref, o_ref, h_sc, acc_sc):
    f = pl.program_id(1)
    nf = pl.num_programs(1)

    @pl.when(f == 0)
    def _():
        h_sc[...] = _rms(x_ref[...], g_ref[...]).astype(h_sc.dtype)
        acc_sc[...] = jnp.zeros_like(acc_sc)

    h = h_sc[...]
    a = _wdot(h, wg_ref[...])
    u = _wdot(h, wu_ref[...])
    acc_sc[...] += _wdot(_silu(a) * u, wd_ref[...])

    @pl.when(f == nf - 1)
    def _():
        o_ref[...] = x_ref[...] + acc_sc[...]


def _dense_ffn(x, g, wg, wu, wd, tm, tf):
    T = x.shape[0]
    assert T % tm == 0 and D_FF % tf == 0
    return pl.pallas_call(
        _dense_ffn_kernel,
        grid=(T // tm, D_FF // tf),
        in_specs=[pl.BlockSpec((tm, D_MODEL), lambda i, f: (i, 0)),
                  pl.BlockSpec((1, D_MODEL), lambda i, f: (0, 0)),
                  pl.BlockSpec((D_MODEL, tf), lambda i, f: (0, f)),
                  pl.BlockSpec((D_MODEL, tf), lambda i, f: (0, f)),
                  pl.BlockSpec((tf, D_MODEL), lambda i, f: (f, 0))],
        out_specs=pl.BlockSpec((tm, D_MODEL), lambda i, f: (i, 0)),
        out_shape=jax.ShapeDtypeStruct((T, D_MODEL), F32),
        scratch_shapes=[pltpu.VMEM((tm, D_MODEL), wg.dtype), pltpu.VMEM((tm, D_MODEL), F32)],
        compiler_params=_params(("parallel", "arbitrary")),
        name="dense_ffn",
    )(x, g, wg, wu, wd)


def _router_kernel(x_ref, g_ref, wr_ref, h_ref, r_ref):
    h = _rms(x_ref[...], g_ref[...])
    h_ref[...] = h
    logits = jnp.dot(h, wr_ref[...], precision=HI, preferred_element_type=F32)
    lane = lax.broadcasted_iota(I32, logits.shape, 1)
    lg = jnp.where(lane < N_EXPERTS, logits, -jnp.inf)
    m1 = jnp.max(lg, axis=-1, keepdims=True)
    i1 = jnp.min(jnp.where(lg == m1, lane, LANES), axis=-1, keepdims=True)
    lg2 = jnp.where(lane == i1, -jnp.inf, lg)
    m2 = jnp.max(lg2, axis=-1, keepdims=True)
    i2 = jnp.min(jnp.where(lg2 == m2, lane, LANES), axis=-1, keepdims=True)
    e = jnp.exp(m2 - m1)
    den = 1.0 + e
    r_ref[...] = jnp.where(lane == 0, i1.astype(F32),
                           jnp.where(lane == 1, i2.astype(F32),
                                     jnp.where(lane == 2, 1.0 / den,
                                               jnp.where(lane == 3, e / den, 0.0))))


def _router(x, g, wr_pad, tm):
    T = x.shape[0]
    assert T % tm == 0
    return pl.pallas_call(
        _router_kernel,
        grid=(T // tm,),
        in_specs=[pl.BlockSpec((tm, D_MODEL), lambda i: (i, 0)),
                  pl.BlockSpec((1, D_MODEL), lambda i: (0, 0)),
                  pl.BlockSpec((D_MODEL, LANES), lambda i: (0, 0))],
        out_specs=[pl.BlockSpec((tm, D_MODEL), lambda i: (i, 0)),
                   pl.BlockSpec((tm, LANES), lambda i: (i, 0))],
        out_shape=[jax.ShapeDtypeStruct((T, D_MODEL), F32),
                   jax.ShapeDtypeStruct((T, LANES), F32)],
        compiler_params=_params(("parallel",)),
        name="moe_router",
    )(x, g, wr_pad)


GATHER_WINDOW = 32
SC_WORKERS = 32


def _row_gather(src, idx):
    n = idx.shape[0]
    step = GATHER_WINDOW * SC_WORKERS
    n_pad = -(-n // step) * step
    if n_pad != n:
        idx = jnp.concatenate([idx, jnp.zeros((n_pad - n,), idx.dtype)])
    width = src.shape[1]
    per_worker = n_pad // SC_WORKERS
    mesh = plsc.VectorSubcoreMesh(core_axis_name="core", subcore_axis_name="subcore")

    @functools.partial(pl.kernel, out_type=jax.ShapeDtypeStruct((n_pad, width), src.dtype), mesh=mesh,
                       scratch_types=[pltpu.VMEM((per_worker,), I32),
                                      pltpu.VMEM((GATHER_WINDOW, width), src.dtype)],
                       name="row_gather")
    def gather(src_hbm, idx_hbm, dst_hbm, idx_v, buf):
        worker = lax.axis_index("core") * (SC_WORKERS // 2) + lax.axis_index("subcore")
        base = worker * per_worker
        pltpu.sync_copy(idx_hbm.at[pl.ds(base, per_worker)], idx_v)

        @pl.loop(0, per_worker // GATHER_WINDOW)
        def _(j):
            pltpu.sync_copy(src_hbm.at[idx_v.at[pl.ds(j * GATHER_WINDOW, GATHER_WINDOW)]], buf)
            pltpu.sync_copy(buf, dst_hbm.at[pl.ds(base + j * GATHER_WINDOW, GATHER_WINDOW)])

    return gather(src, idx)


def _moe_ffn_kernel(be_ref, nu_ref, x_ref, wg_ref, wu_ref, wd_ref, o_ref, acc_sc):
    b = pl.program_id(0)
    f = pl.program_id(1)
    nf = pl.num_programs(1)
    used = b < nu_ref[0]

    @pl.when(used)
    def _():
        @pl.when(f == 0)
        def _():
            acc_sc[...] = jnp.zeros_like(acc_sc)

        x = x_ref[...].astype(BF16)
        a = jnp.dot(x, wg_ref[...].astype(BF16), preferred_element_type=F32)
        u = jnp.dot(x, wu_ref[...].astype(BF16), preferred_element_type=F32)
        acc_sc[...] += _bdot(_silu(a) * u, wd_ref[...])

        @pl.when(f == nf - 1)
        def _():
            o_ref[...] = acc_sc[...]

    @pl.when(jnp.logical_not(used) & (f == nf - 1))
    def _():
        o_ref[...] = jnp.zeros_like(o_ref)


def _moe_ffn(xb, block_e, n_used, wg, wu, wd, blk, tf):
    cap = xb.shape[0]
    nb = cap // blk
    nf = D_FF // tf

    def bsel(b, nu):
        return jnp.minimum(b, nu[0] - 1)

    def fsel(b, f, nu):
        return jnp.where(b < nu[0], f, nf - 1)

    return pl.pallas_call(
        _moe_ffn_kernel,
        grid_spec=pltpu.PrefetchScalarGridSpec(
            num_scalar_prefetch=2,
            grid=(nb, nf),
            in_specs=[
                pl.BlockSpec((blk, D_MODEL), lambda b, f, be, nu: (bsel(b, nu), 0)),
                pl.BlockSpec((None, D_MODEL, tf), lambda b, f, be, nu: (be[bsel(b, nu)], 0, fsel(b, f, nu))),
                pl.BlockSpec((None, D_MODEL, tf), lambda b, f, be, nu: (be[bsel(b, nu)], 0, fsel(b, f, nu))),
                pl.BlockSpec((None, tf, D_MODEL), lambda b, f, be, nu: (be[bsel(b, nu)], fsel(b, f, nu), 0)),
            ],
            out_specs=pl.BlockSpec((blk, D_MODEL), lambda b, f, be, nu: (b, 0)),
            scratch_shapes=[pltpu.VMEM((blk, D_MODEL), F32)],
        ),
        out_shape=jax.ShapeDtypeStruct((cap, D_MODEL), F32),
        compiler_params=_params(("arbitrary", "arbitrary")),
        name="moe_ffn",
    )(block_e, n_used, xb, wg, wu, wd)


def _moe_combine_kernel(x_ref, y1_ref, y2_ref, r_ref, g_ref, o_ref, *, final):
    r = r_ref[...]
    y = x_ref[...] + (r[:, 2:3] * y1_ref[...] + r[:, 3:4] * y2_ref[...])
    o_ref[...] = _rms(y, g_ref[...]) if final else y


def _moe_combine(x, y1, y2, route, tm, final_g):
    T = x.shape[0]
    g = jnp.ones((1, D_MODEL), F32) if final_g is None else final_g
    return pl.pallas_call(
        functools.partial(_moe_combine_kernel, final=final_g is not None),
        grid=(T // tm,),
        in_specs=[pl.BlockSpec((tm, D_MODEL), lambda i: (i, 0)),
                  pl.BlockSpec((tm, D_MODEL), lambda i: (i, 0)),
                  pl.BlockSpec((tm, D_MODEL), lambda i: (i, 0)),
                  pl.BlockSpec((tm, LANES), lambda i: (i, 0)),
                  pl.BlockSpec((1, D_MODEL), lambda i: (0, 0))],
        out_specs=pl.BlockSpec((tm, D_MODEL), lambda i: (i, 0)),
        out_shape=jax.ShapeDtypeStruct((T, D_MODEL), F32),
        compiler_params=_params(("parallel",)),
        name="moe_combine",
    )(x, y1, y2, route, g)


def _moe_plan(e_top, blk):
    T = e_top.shape[0]
    n = 2 * T
    flat_e = e_top.reshape(-1)
    onehot = (flat_e[:, None] == jnp.arange(N_EXPERTS, dtype=I32)[None, :]).astype(I32)
    csum = jnp.cumsum(onehot, axis=0)
    rank = jnp.sum((csum - onehot) * onehot, axis=1)
    counts = csum[-1]
    padded = (counts + blk - 1) // blk * blk
    ends = jnp.cumsum(padded)
    pstart = ends - padded
    dest = (pstart[flat_e] + rank).astype(I32)
    n_blocks = -(-n // blk) + N_EXPERTS
    cap = n_blocks * blk
    slot_tok = (jnp.arange(cap, dtype=I32) % T).at[dest].set(jnp.arange(n, dtype=I32) // 2)
    first = jnp.arange(n_blocks, dtype=I32) * blk
    block_e = jnp.minimum(jnp.sum((ends[None, :] <= first[:, None]).astype(I32), axis=1), N_EXPERTS - 1)
    n_used = (ends[-1] // blk).astype(I32).reshape(1)
    return dest, slot_tok, block_e, n_used


def _moe_dispatch(x, g, wr_pad, tm, blk):
    h, route = _router(x, g, wr_pad, tm)
    dest, slot_tok, block_e, n_used = _moe_plan(route[:, 0:2].astype(I32), blk)
    return dict(x=x, route=route, dest=dest, xb=_row_gather(h, slot_tok), block_e=block_e, n_used=n_used)


def _moe_finish(d, wg, wu, wd, tm, blk, tf, final_g):
    yb = _moe_ffn(d['xb'], d['block_e'], d['n_used'], wg, wu, wd, blk, tf)
    d2 = d['dest'].reshape(-1, 2)
    return _moe_combine(d['x'], _row_gather(yb, d2[:, 0]), _row_gather(yb, d2[:, 1]), d['route'], tm, final_g)


def _norm_kernel(x_ref, g_ref, o_ref):
    o_ref[...] = _rms(x_ref[...], g_ref[...])


def _final_norm(x, g, tm):
    T = x.shape[0]
    return pl.pallas_call(
        _norm_kernel,
        grid=(T // tm,),
        in_specs=[pl.BlockSpec((tm, D_MODEL), lambda i: (i, 0)),
                  pl.BlockSpec((1, D_MODEL), lambda i: (0, 0))],
        out_specs=pl.BlockSpec((tm, D_MODEL), lambda i: (i, 0)),
        out_shape=jax.ShapeDtypeStruct((T, D_MODEL), F32),
        compiler_params=_params(("parallel",)),
        name="final_norm",
    )(x, g)


_IN_SPLITS = (D_FOX, D_FOX, D_FOX, N_HEADS, D_NSA, 6 * HEAD_DIM, 3 * N_HEADS,
              D_FOX, D_FOX, D_GLA, GLA_RANK, D_GLA)


def _reorder_w_in(w):
    offs = [0]
    for s in _IN_SPLITS:
        offs.append(offs[-1] + s)
    seg = lambda k: w[:, offs[k]:offs[k + 1]]
    fq, fk, fv, ff, nq, nkv, ng, gq, gk, gv, glr, gog = [seg(k) for k in range(12)]
    pad = jnp.zeros((w.shape[0], LANES - SM_GLR - GLA_RANK), w.dtype)
    return jnp.concatenate([fq, fk, fv, nq, nkv, gq, gk, gv, gog, ff, ng, glr, pad], axis=1)


def _rope_table(pos):
    inv = ROPE_THETA ** (-jnp.arange(ROPE_HALF, dtype=F32) / ROPE_HALF)
    ang = pos.astype(F32)[:, None] * inv[None, :]
    cos, sin = jnp.cos(ang), jnp.sin(ang)
    P = pos.shape[0]
    one = jnp.ones((P, HEAD_DIM - ROPE_DIM), F32)
    zero = jnp.zeros((P, HEAD_DIM - ROPE_DIM), F32)
    z8 = jnp.zeros((P, ROPE_HALF), F32)
    a64 = jnp.concatenate([cos, cos, one], axis=1)
    p64 = jnp.concatenate([z8, sin, zero], axis=1)
    m64 = jnp.concatenate([-sin, z8, zero], axis=1)
    i64 = jnp.ones((P, HEAD_DIM), F32)
    o64 = jnp.zeros((P, HEAD_DIM), F32)
    return jnp.concatenate([a64, a64, p64, p64, m64, m64, a64, i64, p64, o64, m64, o64], axis=1)


def _layer_mix_params(l, norm_mix_g, w_in, b_fox_f, w_cmp, pe_cmp, w_gla_gk, b_gla_gk, g_gla_norm, w_out):
    sb = jnp.zeros((1, LANES), F32).at[0, SM_FF:SM_FF + N_HEADS].set(b_fox_f[l])
    wgk = jnp.zeros((LANES, D_FOX), F32).at[SM_GLR:SM_GLR + GLA_RANK].set(w_gla_gk[l])
    w_r = _reorder_w_in(w_in[l])
    return dict(g=norm_mix_g[l].reshape(1, D_MODEL), w_r=w_r.astype(BF16), w_r32=w_r, sb=sb,
                wgk=wgk.astype(BF16), wgk32=wgk, w_out32=w_out[l],
                bgk=b_gla_gk[l].reshape(1, D_FOX), w_cmp=w_cmp[l],
                pe_cmp=pe_cmp[l].reshape(2, 1, CMP_LEN * HEAD_DIM),
                gn=g_gla_norm[l].reshape(1, GLA_DV), w_out=w_out[l].astype(BF16))


def _mix_prompt(x, B, S, l, depth, p, tab, tm, t_fox, tk_sel, tc_gla, carry):
    (fq, fkv_all, nq, nqr, rows_all, win_t, gqk, gv, gog, glog, small, small_t) = _in_projection(
        x, 0, B * S, tm, p['g'], p['w_r'], p['sb'], p['wgk'], p['bgk'], tab, S, batch=(B, S, depth), layer=l,
        prev=(carry['fkv'], carry['rows']) if carry else None)
    carry.update(fkv=fkv_all, rows=rows_all)
    cr = _fox_cumsum(small_t, min(S, 512))
    o_fox = _fox_prompt(fq, fkv_all, l, cr, B, S, t_fox)
    n_blk = S // CMP_LEN
    blocks = rows_all[l, :, 0:2 * HEAD_DIM, :].reshape(B, 2, HEAD_DIM, n_blk, CMP_LEN).transpose(1, 0, 3, 4, 2)
    blocks = blocks.reshape(2, B * n_blk, CMP_LEN * HEAD_DIM)
    cmp = _compress(blocks, p['pe_cmp'], p['w_cmp'], min(256, B * n_blk), True)
    o_nsa = _nsa_prompt(nq, nqr, small, cmp, rows_all, l, win_t, B, S, tk_sel)
    o_gla, g_state = _gla_prompt(gqk, gv, glog, B, S, tc_gla)
    x_new = _out_projection(x, o_fox, o_nsa, o_gla, gog, p['gn'], p['w_out'], tm)
    wp = min(WINDOW, S)
    return x_new, dict(small=small_t[:, 0:N_HEADS, :], win=win_t[:, :, S - wp:], g_state=g_state)


def _per_head_col(vals):
    r = lax.broadcasted_iota(I32, (SUBLANES, 1), 0)
    out = jnp.zeros((SUBLANES, 1), F32)
    for h, v in enumerate(vals):
        out = out + jnp.where(r == h, v, 0.0)
    return out


def _per_head_row(vals, width):
    grp = lax.broadcasted_iota(I32, (1, width), 1) // HEAD_DIM
    out = jnp.zeros((1, width), F32)
    for h, v in enumerate(vals):
        out = out + jnp.where(grp == h, v, 0.0)
    return out


def _head_lane_sums(row):
    grp = lax.broadcasted_iota(I32, row.shape, 1) // HEAD_DIM
    return [jnp.sum(jnp.where(grp == h, row, 0.0), axis=1, keepdims=True) for h in range(N_HEADS)]


def _cols_of(row):
    return jnp.concatenate([jnp.broadcast_to(row[:, j:j + LANES], (LANES, LANES)).T
                            for j in range(0, row.shape[1], LANES)], axis=0)


def _row_of(col):
    return jnp.concatenate([jnp.broadcast_to(col[j:j + LANES], (LANES, LANES)).T[0:1, :]
                            for j in range(0, col.shape[0], LANES)], axis=1)


def _sublane_group_sum(x):
    return jnp.sum(x.reshape(x.shape[0] // SUBLANES, SUBLANES, x.shape[1]), axis=0)


def _fold_matrix(n_pages):
    r = lax.broadcasted_iota(I32, (n_pages * SUBLANES, n_pages * N_HEADS * SUBLANES), 0)
    c = lax.broadcasted_iota(I32, (n_pages * SUBLANES, n_pages * N_HEADS * SUBLANES), 1)
    blk = c // SUBLANES
    return ((blk // N_HEADS == r // SUBLANES) & (blk % N_HEADS == r % SUBLANES)).astype(F32)


def _fox_decode_kernel(pt_ref, q_ref, kvn_ref, smn_ref, *refs, n_pages):
    del pt_ref
    kv_refs = refs[0:n_pages]
    lf_refs = refs[n_pages:2 * n_pages]
    o_ref, lf_sc, part_sc = refs[2 * n_pages:]
    R = n_pages * SUBLANES
    PG = kv_refs[0].shape[-1]
    row = pl.ds(pl.program_id(0) % SUBLANES, 1)
    q_row = q_ref[row, :]
    kvn = kvn_ref[row, :]
    smn = smn_ref[row, :]
    q_cols = _cols_of(q_row)

    lf_sc[...] = jnp.zeros_like(lf_sc)
    for p in range(n_pages):
        lf_sc[p * SUBLANES:p * SUBLANES + N_HEADS, :] = lf_refs[p][...]
    lft = lf_sc[...]
    k0 = lax.broadcasted_iota(I32, (PG, PG), 0)
    k1 = lax.broadcasted_iota(I32, (PG, PG), 1)
    within = _dot01_r(lft, k0 > k1)
    tot = jnp.broadcast_to(jnp.sum(lft, axis=1, keepdims=True), (R, PG))
    r0 = lax.broadcasted_iota(I32, (R, R), 0)
    r1 = lax.broadcasted_iota(I32, (R, R), 1)
    later = (r1 % SUBLANES == r0 % SUBLANES) & (r1 // SUBLANES > r0 // SUBLANES)
    cross = _dot01(later, tot)
    rr = lax.broadcasted_iota(I32, (R, 1), 0) % SUBLANES
    newcol = jnp.zeros((R, 1), F32)
    for h in range(N_HEADS):
        newcol = newcol + jnp.where(rr == h, smn[:, SM_FF + h:SM_FF + h + 1], 0.0)
    bias = (within + cross + newcol).reshape(n_pages, SUBLANES, PG)

    for p in range(n_pages):
        for h in range(N_HEADS):
            g = p * N_HEADS + h
            part_sc[g * SUBLANES:(g + 1) * SUBLANES, :] = _sublane_group_sum(
                kv_refs[p][0, h] * q_cols[h * HEAD_DIM:(h + 1) * HEAD_DIM])
    s = _dot01(_fold_matrix(n_pages), part_sc[...])
    s3 = s.reshape(n_pages, SUBLANES, PG) * SCALE + bias
    s_new = _per_head_col(_head_lane_sums(q_row * kvn[:, 0:D_FOX])) * SCALE
    m = jnp.max(jnp.max(s3, axis=2, keepdims=True), axis=0)
    m = jnp.maximum(m, s_new)
    p3 = jnp.exp(s3 - m[None])
    pn = jnp.exp(s_new - m)
    inv = 1.0 / (jnp.sum(jnp.sum(p3, axis=2, keepdims=True), axis=0) + pn)
    o_cols = []
    for h in range(N_HEADS):
        acc = jnp.zeros((HEAD_DIM, PG), F32)
        for p in range(n_pages):
            acc = acc + kv_refs[p][1, h] * p3[p, h:h + 1, :]
        o_cols.append(jnp.sum(acc, axis=1, keepdims=True) * inv[h:h + 1])
    w_new = _per_head_row([pn[h:h + 1] * inv[h:h + 1] for h in range(N_HEADS)], D_FOX)
    o_ref[row, :] = _row_of(jnp.concatenate(o_cols, axis=0)) + w_new * kvn[:, D_FOX:2 * D_FOX]


def _fox_decode(l, pt_flat, n_pages, fq, fkv, small, kv_cache_t, lft_cache):
    DB = fq.shape[0]
    PG = kv_cache_t.shape[-1]
    page = lambda p, nz: (lambda b, pt: (l, pt[b * n_pages + p]) + (0,) * nz)
    rows8 = lambda w: pl.BlockSpec((SUBLANES, w), lambda b, pt: (b // SUBLANES, 0))
    return pl.pallas_call(
        functools.partial(_fox_decode_kernel, n_pages=n_pages),
        grid_spec=pltpu.PrefetchScalarGridSpec(
            num_scalar_prefetch=1,
            grid=(DB,),
            in_specs=[rows8(D_FOX), rows8(2 * D_FOX), rows8(LANES)]
            + [pl.BlockSpec((None, None, 2, N_HEADS, HEAD_DIM, PG), page(p, 4)) for p in range(n_pages)]
            + [pl.BlockSpec((None, None, N_HEADS, PG), page(p, 2)) for p in range(n_pages)],
            out_specs=rows8(D_FOX),
            scratch_shapes=[pltpu.VMEM((n_pages * SUBLANES, PG), F32),
                            pltpu.VMEM((n_pages * N_HEADS * SUBLANES, PG), F32)],
        ),
        out_shape=jax.ShapeDtypeStruct((DB, D_FOX), F32),
        compiler_params=_params(("arbitrary",)),
        name="fox_decode",
    )(pt_flat, fq, fkv, small, *([kv_cache_t] * n_pages), *([lft_cache] * n_pages))


def _nsa_decode_kernel(pt_ref, q_ref, qr_ref, rown_ref, winn_ref, sm_ref, win_ref, *refs,
                       n_pages, past_len):
    del pt_ref
    pg_refs = refs[0:n_pages]
    cmp_refs = refs[n_pages:2 * n_pages]
    o_ref, nw_ref, cmp_sc, qc_sc, part_sc, sw_sc = refs[-6:]
    R = n_pages * SUBLANES
    PG = pg_refs[0].shape[-1]
    WB = win_ref.shape[-1]
    per_page = PG // CMP_LEN
    assert per_page <= SUBLANES and PG == 2 * SEL_LEN and R == LANES
    jt = past_len // SEL_LEN
    row = pl.ds(pl.program_id(0) % SUBLANES, 1)
    q_row = q_ref[row, :]
    qr_row = qr_ref[row, :]
    rown = rown_ref[row, :]
    winn = winn_ref[row, :]
    smn = sm_ref[row, :]
    qr_cols = _cols_of(qr_row)
    rep4 = lambda r64: jnp.concatenate([r64] * N_HEADS, axis=1)

    qc_sc[...] = jnp.zeros_like(qc_sc)
    for h in range(N_HEADS):
        qc_sc[h:h + 1, 0:HEAD_DIM] = q_row[:, h * HEAD_DIM:(h + 1) * HEAD_DIM]
    head_row = lax.broadcasted_iota(I32, (SUBLANES, 1), 0) < N_HEADS

    cmp_sc[...] = jnp.zeros_like(cmp_sc)
    for p in range(n_pages):
        cmp_sc[p * SUBLANES:p * SUBLANES + per_page, :] = cmp_refs[p][...]
    cmpa = cmp_sc[...]
    lane = lax.broadcasted_iota(I32, (1, R), 1)
    blk = per_page * (lane // SUBLANES) + lane % SUBLANES
    complete = (lane % SUBLANES < per_page) & ((blk + 1) * CMP_LEN - 1 <= past_len)
    s = _dot_nt_hilo(qc_sc[...], cmpa) * SCALE
    s = jnp.where(complete, s, NEG)
    e = jnp.exp(s - jnp.max(s, axis=-1, keepdims=True))
    pc = e / jnp.sum(e, axis=-1, keepdims=True) * complete.astype(F32)
    vcb_t = cmpa.T[HEAD_DIM:2 * HEAD_DIM, :]
    o_cmp = [jnp.sum(vcb_t * pc[h:h + 1, :], axis=1, keepdims=True) for h in range(N_HEADS)]

    imp_c = jnp.sum(jnp.where(head_row, pc, 0.0), axis=0, keepdims=True)
    imp_s = imp_c + pltpu.roll(imp_c, R - 1, 1)
    cand = (lane % SUBLANES == 0) | (lane % SUBLANES == 2)
    jsel = 2 * (lane // SUBLANES) + (lane % SUBLANES) // 2
    score = jnp.where(jsel == jt, 2.0 * SEL_FORCE,
                      jnp.where((jsel == 0) | (jsel == jt - 1), SEL_FORCE,
                                jnp.where(jsel <= jt, imp_s + 0.0, -1.0)))
    score_b = jnp.broadcast_to(score, (R, R))
    key_row = _order_key(score_b)
    key_col = _order_key(score_b.T)
    l0 = lax.broadcasted_iota(I32, (R, R), 0)
    l1 = lax.broadcasted_iota(I32, (R, R), 1)
    cand_col = (l0 % SUBLANES == 0) | (l0 % SUBLANES == 2)
    beats = cand_col & (key_col > jnp.where(l0 < l1, key_row - 1, key_row))
    cnt = jnp.sum(beats.astype(I32), axis=0, keepdims=True)
    sel_row = (cand & (cnt < TOP_N - 1)).astype(F32)
    sel_col = jnp.broadcast_to(sel_row, (R, R)).T
    half = ((l0 % SUBLANES == 0) & (l1 < SEL_LEN)) | ((l0 % SUBLANES == 2) & (l1 >= SEL_LEN))
    z = jnp.where(half, sel_col, 0.0)
    same_page = (l1 // SUBLANES == l0 // SUBLANES).astype(BF16)
    picked = jnp.dot(same_page, z.astype(BF16), preferred_element_type=F32)
    picked = picked.reshape(n_pages, SUBLANES, PG) > 0.5

    for p in range(n_pages):
        ks_t = pg_refs[p][2]
        for h in range(N_HEADS):
            g = p * N_HEADS + h
            part_sc[g * SUBLANES:(g + 1) * SUBLANES, :] = _sublane_group_sum(
                ks_t * qr_cols[h * HEAD_DIM:(h + 1) * HEAD_DIM])
    s = _dot01(_fold_matrix(n_pages), part_sc[...])
    s3 = jnp.where(picked, s.reshape(n_pages, SUBLANES, PG) * SCALE, NEG)
    s_new = _per_head_col(_head_lane_sums(qr_row * rep4(rown[:, 2 * HEAD_DIM:3 * HEAD_DIM]))) * SCALE
    m = jnp.maximum(jnp.max(jnp.max(s3, axis=2, keepdims=True), axis=0), s_new)
    p3 = jnp.exp(s3 - m[None])
    pn = jnp.exp(s_new - m)
    inv = 1.0 / (jnp.sum(jnp.sum(p3, axis=2, keepdims=True), axis=0) + pn)
    o_sel = []
    for h in range(N_HEADS):
        acc = jnp.zeros((HEAD_DIM, PG), F32)
        for p in range(n_pages):
            acc = acc + pg_refs[p][3] * p3[p, h:h + 1, :]
        o_sel.append(jnp.sum(acc, axis=1, keepdims=True) * inv[h:h + 1])

    kw_t = win_ref[0]
    vw_t = win_ref[1]
    wlane = lax.broadcasted_iota(I32, (1, WB), 1)
    wpos = past_len - WB + wlane
    wd = past_len - wpos
    wok = (wd >= 0) & (wd < WINDOW) & (wpos >= 0)
    sw_sc[...] = jnp.zeros_like(sw_sc)
    for h in range(N_HEADS):
        qh = qr_cols[h * HEAD_DIM:(h + 1) * HEAD_DIM]
        sw_sc[h:h + 1, :] = jnp.sum(kw_t * jnp.concatenate([qh] * (WB // LANES), axis=1), axis=0, keepdims=True)
    sw = jnp.where(wok, sw_sc[...] * SCALE, NEG)
    sw_new = _per_head_col(_head_lane_sums(qr_row * rep4(winn[:, 0:HEAD_DIM]))) * SCALE
    mw = jnp.maximum(jnp.max(sw, axis=-1, keepdims=True), sw_new)
    ew = jnp.exp(sw - mw)
    en = jnp.exp(sw_new - mw)
    invw = 1.0 / (jnp.sum(ew, axis=-1, keepdims=True) + en)

    gate = lambda h, c: smn[:, SM_NG + 3 * h + c:SM_NG + 3 * h + c + 1]
    o_cols = []
    for h in range(N_HEADS):
        o_win = jnp.sum(vw_t * ew[h:h + 1, :], axis=1, keepdims=True) * invw[h:h + 1]
        o_cols.append(gate(h, 0) * o_cmp[h] + gate(h, 1) * o_sel[h] + gate(h, 2) * o_win)
    w_sel = _per_head_row([gate(h, 1) * pn[h:h + 1] * inv[h:h + 1] for h in range(N_HEADS)], D_NSA)
    w_win = _per_head_row([gate(h, 2) * en[h:h + 1] * invw[h:h + 1] for h in range(N_HEADS)], D_NSA)
    o_ref[row, :] = (_row_of(jnp.concatenate(o_cols, axis=0))
                     + w_sel * rep4(rown[:, 3 * HEAD_DIM:4 * HEAD_DIM])
                     + w_win * rep4(winn[:, HEAD_DIM:2 * HEAD_DIM]))
    last = lax.broadcasted_iota(I32, (HEAD_DIM, WB), 1) == WB - 1
    winn_cols = _cols_of(winn)
    for s in range(2):
        new_col = winn_cols[s * HEAD_DIM:(s + 1) * HEAD_DIM, 0:1]
        nw_ref[s] = jnp.where(last, new_col, pltpu.roll(win_ref[s], WB - 1, 1))


def _nsa_decode(l, pt_flat, n_pages, past_len, nq, nqr, rows, win, small, nsa_cache_t, cmp_pool, win_state_t,
                prev):
    DB = nq.shape[0]
    PG = nsa_cache_t.shape[-1]
    WB = win_state_t.shape[-1]
    page = lambda p: (lambda b, pt: (l, pt[b * n_pages + p], 0, 0, 0))
    cpage = lambda p: (lambda b, pt: (pt[b * n_pages + p], 0, 0))
    carried = [] if prev is None else [prev]
    rows8 = lambda w: pl.BlockSpec((SUBLANES, w), lambda b, pt: (b // SUBLANES, 0))
    return pl.pallas_call(
        functools.partial(_nsa_decode_kernel, n_pages=n_pages, past_len=past_len),
        grid_spec=pltpu.PrefetchScalarGridSpec(
            num_scalar_prefetch=1,
            grid=(DB,),
            in_specs=[rows8(D_NSA), rows8(D_NSA), rows8(4 * HEAD_DIM), rows8(2 * HEAD_DIM), rows8(LANES),
                      pl.BlockSpec((None, None, 2, HEAD_DIM, WB), lambda b, pt: (l, b, 0, 0, 0))]
            + [pl.BlockSpec((None, None, 4, HEAD_DIM, PG), page(p)) for p in range(n_pages)]
            + [pl.BlockSpec((None, PG // CMP_LEN, LANES), cpage(p)) for p in range(n_pages)]
            + [pl.BlockSpec(memory_space=pl.ANY)] * len(carried),
            out_specs=[rows8(D_NSA),
                       pl.BlockSpec((None, None, 2, HEAD_DIM, WB), lambda b, pt: (l, b, 0, 0, 0))],
            scratch_shapes=[pltpu.VMEM((n_pages * SUBLANES, LANES), F32),
                            pltpu.VMEM((SUBLANES, LANES), F32),
                            pltpu.VMEM((n_pages * N_HEADS * SUBLANES, PG), F32),
                            pltpu.VMEM((SUBLANES, WB), F32)],
        ),
        out_shape=[jax.ShapeDtypeStruct((DB, D_NSA), F32),
                   jax.ShapeDtypeStruct(win_state_t.shape, F32)],
        input_output_aliases={7 + 2 * n_pages: 1} if carried else {},
        compiler_params=_params(("arbitrary",)),
        name="nsa_decode",
    )(pt_flat, nq, nqr, rows, win, small, win_state_t,
      *([nsa_cache_t] * n_pages), *([cmp_pool] * n_pages), *carried)


def _gla_decode_kernel(qk_ref, g_ref, v_ref, s_ref, *rest):
    o_ref, so_ref = rest[-2:]
    for j in range(SUBLANES):
        qk = qk_ref[j:j + 1, :]
        q_cols = _cols_of(qk[:, 0:D_FOX] * SCALE)
        k_cols = _cols_of(qk[:, D_FOX:2 * D_FOX])
        decay = jnp.exp(_cols_of(g_ref[j:j + 1, :]))
        for h in range(N_HEADS):
            hs = slice(h * HEAD_DIM, (h + 1) * HEAD_DIM)
            v_row = v_ref[j:j + 1, h * GLA_DV:(h + 1) * GLA_DV]
            s_new = decay[hs] * s_ref[j, h] + k_cols[hs] * v_row
            so_ref[j, h] = s_new
            o_ref[j:j + 1, h * GLA_DV:(h + 1) * GLA_DV] = jnp.sum(q_cols[hs] * s_new, axis=0, keepdims=True)


def _gla_decode(l, gqk, gv, glog, state, prev):
    depth, DB = state.shape[0:2]
    nb = SUBLANES
    rows = lambda w: pl.BlockSpec((nb, w), lambda i: (i, 0))
    sspec = pl.BlockSpec((None, nb, N_HEADS, HEAD_DIM, GLA_DV), lambda i: (l, i, 0, 0, 0))
    carried = [] if prev is None else [prev]
    return pl.pallas_call(
        _gla_decode_kernel,
        grid=(DB // nb,),
        in_specs=[rows(2 * D_FOX), rows(D_FOX), rows(D_GLA), sspec]
        + [pl.BlockSpec(memory_space=pl.ANY)] * len(carried),
        out_specs=[rows(D_GLA), sspec],
        out_shape=[jax.ShapeDtypeStruct((DB, D_GLA), F32), jax.ShapeDtypeStruct(state.shape, F32)],
        input_output_aliases={4: 1} if carried else {},
        compiler_params=_params(("parallel",)),
        name="gla_decode",
    )(gqk, glog, gv, state, *carried)


def _mix_sample(x, l, p, tab, pt_flat, n_pages, past_len, fox_kv_t, fox_lft_c, nsa_t, cmp_blocks,
                win_state_t, gla_state, carry):
    DB = x.shape[0]
    (fq, fkv, nq, nqr, rows, win, gqk, gv, gog, glog, small) = _in_projection(
        x, 0, DB, DB, p['g'], p['w_r32'], p['sb'], p['wgk32'], p['bgk'], tab, DB)
    o_fox = _fox_decode(l, pt_flat, n_pages, fq, fkv, small, fox_kv_t, fox_lft_c)
    n_pool, PG = nsa_t.shape[1], nsa_t.shape[-1]
    per_layer = n_pool * (PG // CMP_LEN)
    cmp_pool = _compress(cmp_blocks, p['pe_cmp'], p['w_cmp'], 256, False, row0=l * per_layer, n_rows=per_layer)
    cmp_pool = cmp_pool.reshape(n_pool, PG // CMP_LEN, LANES)
    o_nsa, new_win = _nsa_decode(l, pt_flat, n_pages, past_len, nq, nqr, rows, win, small,
                                 nsa_t, cmp_pool, win_state_t, carry.get('win'))
    o_gla, g_state = _gla_decode(l, gqk, gv, glog, gla_state, carry.get('gla'))
    carry.update(win=new_win, gla=g_state)
    x_new = _out_projection(x, o_fox, o_nsa, o_gla, gog, p['gn'], p['w_out32'], DB)
    return x_new, dict(fkv=fkv, small=small, rows=rows)


def kernel(x_prompt, x_sample, cache_fox_kv, cache_fox_logf, cache_nsa_kv, state_nsa_win, state_gla,
           page_table, norm_mix_g, w_in, b_fox_f, w_cmp, pe_cmp, w_gla_gk, b_gla_gk, g_gla_norm, w_out,
           norm_ffn_g, dense_w_gate, dense_w_up, dense_w_down, moe_w_router, moe_w_gate, moe_w_up,
           moe_w_down, final_norm_g):
    B, S, _ = x_prompt.shape
    DB, TN, _ = x_sample.shape
    assert TN == 1
    depth, n_pool, PG = cache_fox_kv.shape[0:3]
    n_pages = page_table.shape[1]
    past_len = n_pages * PG
    WB = state_nsa_win.shape[2]
    xp = x_prompt.reshape(B * S, D_MODEL)
    xs = x_sample.reshape(DB, D_MODEL)
    tab_p = _rope_table(jnp.arange(S))
    tab_s = _rope_table(jnp.full((DB,), past_len, I32))
    pt_flat = page_table.reshape(-1).astype(I32)
    fox_kv_t = jnp.transpose(cache_fox_kv, (0, 1, 3, 4, 5, 2))
    fox_lft_c = jnp.swapaxes(cache_fox_logf, 2, 3)
    nsa_t = jnp.transpose(cache_nsa_kv, (0, 1, 3, 4, 2))
    win_state_t = jnp.transpose(state_nsa_win, (0, 1, 3, 4, 2))
    n_cmp = depth * n_pool * (PG // CMP_LEN)
    cmp_blocks = _block_major(cache_nsa_kv[:, :, :, 0:2, :].reshape(n_cmp * CMP_LEN, 2, HEAD_DIM), n_cmp)
    gfin = final_norm_g.reshape(1, D_MODEL)
    cp, cs = [], []
    carry_p, carry_s = {}, {}
    for l in range(depth):
        p = _layer_mix_params(l, norm_mix_g, w_in, b_fox_f, w_cmp, pe_cmp, w_gla_gk, b_gla_gk,
                              g_gla_norm, w_out)
        gf = norm_ffn_g[l].reshape(1, D_MODEL)
        i = l // 2
        moe = l % 2 == 1
        xp, c = _mix_prompt(xp, B, S, l, depth, p, tab_p, 512, 512, 512, 256, carry_p)
        cp.append(c)
        if moe:
            wr = jnp.zeros((D_MODEL, LANES), F32).at[:, 0:N_EXPERTS].set(moe_w_router[i])
            experts = (moe_w_gate[i], moe_w_up[i], moe_w_down[i])
            disp_p = _moe_dispatch(xp, gf, wr, 512, 1024)
        xs, c = _mix_sample(xs, l, p, tab_s, pt_flat, n_pages, past_len, fox_kv_t, fox_lft_c, nsa_t,
                            cmp_blocks, win_state_t, state_gla, carry_s)
        cs.append(c)
        if moe:
            disp_s = _moe_dispatch(xs, gf, wr, DB, LANES)
            fin = gfin if l == depth - 1 else None
            xp = _moe_finish(disp_p, *experts, 512, 1024, 896, fin)
            xs = _moe_finish(disp_s, *experts, DB, LANES, 896, fin)
        else:
            wg, wu, wd = (dense_w_gate[i].astype(BF16), dense_w_up[i].astype(BF16),
                          dense_w_down[i].astype(BF16))
            xp = _dense_ffn(xp, gf, wg, wu, wd, 1024, 896)
            xs = _dense_ffn(xs, gf, dense_w_gate[i], dense_w_up[i], dense_w_down[i], DB, 896)
    if depth % 2 == 1:
        xp = _final_norm(xp, gfin, 512)
        xs = _final_norm(xs, gfin, DB)
    y_p = xp.reshape(B, S, D_MODEL)
    y_s = xs.reshape(DB, 1, D_MODEL)
    wp = min(WINDOW, S)
    st = lambda key, group: jnp.stack([c[key] for c in group])
    return (y_p, y_s,
            carry_p['fkv'].reshape(depth, B, 2, N_HEADS, HEAD_DIM, S).transpose(0, 1, 5, 2, 3, 4),
            st('small', cp).transpose(0, 1, 3, 2),
            carry_p['rows'].reshape(depth, B, 4, HEAD_DIM, S).transpose(0, 1, 4, 2, 3),
            st('win', cp).reshape(depth, B, 2, HEAD_DIM, wp).transpose(0, 1, 4, 2, 3),
            st('g_state', cp),
            st('fkv', cs).reshape(depth, DB, 1, 2, N_HEADS, HEAD_DIM),
            st('small', cs)[:, :, SM_FF:SM_FF + N_HEADS].reshape(depth, DB, 1, N_HEADS),
            st('rows', cs).reshape(depth, DB, 1, 4, HEAD_DIM),
            jnp.transpose(carry_s['win'], (0, 1, 4, 2, 3)),
            carry_s['gla'])
```

```python
import functools

import jax
import jax.numpy as jnp
from jax import lax
from jax.experimental import pallas as pl
from jax.experimental.pallas import tpu as pltpu
from jax.experimental.pallas import tpu_sc as plsc

F32 = jnp.float32
BF16 = jnp.bfloat16
I32 = jnp.int32
HI = lax.Precision.HIGHEST

D_MODEL = 1024
HEAD_DIM = 64
N_HEADS = 4
D_FOX = N_HEADS * HEAD_DIM
D_NSA = N_HEADS * HEAD_DIM
GLA_DV = 128
D_GLA = N_HEADS * GLA_DV
GLA_RANK = 16
GLA_TAU = 16.0
GLA_CHUNK = 64
CMP_LEN = 32
SEL_LEN = 64
TOP_N = 16
WINDOW = 512
ROPE_THETA = 500000.0
ROPE_DIM = HEAD_DIM // 4
ROPE_HALF = ROPE_DIM // 2
D_FF = 3584
N_EXPERTS = 8
EPS = 1e-6
SEL_FORCE = 1e9
NEG = -1e30
SCALE = HEAD_DIM ** -0.5
LOG2E = 1.4426950408889634

LANES = 128
SUBLANES = 8
VMEM_BYTES_V7X = 64 * 1024 * 1024
VMEM_LIMIT = VMEM_BYTES_V7X - 8 * 1024 * 1024


class Tiles:
    prompt_rows = 512
    fox = 512
    nsa_keys = 512
    gla_rows = 256
    cumsum = 512
    compress_rows = 256
    dense_rows = 1024
    ffn_cols = 512
    ffn_cols_sample = 896
    expert_block = 1024

C_FQ = 0
C_FKV = 256
C_NQ = 768
C_NKV = 1024
C_GQK = 1408
C_GV = 1920
C_GOG = 2432
C_SMALL = 2944
C_END = 3072
SM_FF = 0
SM_NG = 4
SM_GLR = 16

NT = (((1,), (1,)), ((), ()))


def _params(sem):
    return pltpu.CompilerParams(dimension_semantics=sem, vmem_limit_bytes=VMEM_LIMIT)


def _rms(x, g):
    ms = jnp.mean(x * x, axis=-1, keepdims=True)
    return x * lax.rsqrt(ms + EPS) * g


def _sigmoid(x):
    return 1.0 / (1.0 + jnp.exp(-x))


def _log_sigmoid(x):
    return -(jnp.maximum(-x, 0.0) + jnp.log1p(jnp.exp(-jnp.abs(x))))


def _silu(x):
    return x * _sigmoid(x)


def _bdot(a, b):
    return jnp.dot(a.astype(BF16), b.astype(BF16), preferred_element_type=F32)


def _split3(x):
    h = x.astype(BF16)
    r = x - h.astype(F32)
    m = r.astype(BF16)
    return h, m, (r - m.astype(F32)).astype(BF16)


def _dot01(m01, x):
    mb = m01.astype(BF16)
    h, m, l = _split3(x)
    return (jnp.dot(mb, h, preferred_element_type=F32) + jnp.dot(mb, m, preferred_element_type=F32)
            + jnp.dot(mb, l, preferred_element_type=F32))


def _dot01_r(x, m01):
    mb = m01.astype(BF16)
    h, m, l = _split3(x)
    return (jnp.dot(h, mb, preferred_element_type=F32) + jnp.dot(m, mb, preferred_element_type=F32)
            + jnp.dot(l, mb, preferred_element_type=F32))


def _dot01_nt(m01, x):
    mb = m01.astype(BF16)
    h, m, l = _split3(x)
    return (lax.dot_general(mb, h, NT, preferred_element_type=F32)
            + lax.dot_general(mb, m, NT, preferred_element_type=F32)
            + lax.dot_general(mb, l, NT, preferred_element_type=F32))


def _dot_nt_hilo(a, b):
    ah = a.astype(BF16)
    al = (a - ah.astype(F32)).astype(BF16)
    bh = b.astype(BF16)
    bl = (b - bh.astype(F32)).astype(BF16)
    return (lax.dot_general(ah, bh, NT, preferred_element_type=F32)
            + lax.dot_general(al, bh, NT, preferred_element_type=F32)
            + lax.dot_general(ah, bl, NT, preferred_element_type=F32))


def _wdot(a, w):
    if w.dtype == F32:
        return jnp.dot(a.astype(F32), w, precision=HI, preferred_element_type=F32)
    return jnp.dot(a.astype(BF16), w, preferred_element_type=F32)


def _rope128(x, a, bp, bm):
    return x * a + pltpu.roll(x, ROPE_HALF, 1) * bp + pltpu.roll(x, LANES - ROPE_HALF, 1) * bm


def _inproj_kernel(x_ref, g_ref, w_ref, sb_ref, wgk_ref, bgk_ref, tab_ref, *refs, feature_major, n_carried):
    (fq_ref, fkv_ref, nq_ref, nqr_ref, rows_ref, win_ref,
     gqk_ref, gv_ref, gog_ref, glog_ref, small_ref, *extra) = refs[n_carried:]
    h = _rms(x_ref[...], g_ref[...]).astype(w_ref.dtype)

    def put(ref, v):
        ref[...] = v.T if feature_major else v

    def mm(a, b):
        return _wdot(h, w_ref[:, a:b])

    fq_ref[...] = mm(C_FQ, C_FKV)
    put(fkv_ref, mm(C_FKV, C_NQ))
    tab = tab_ref[...]
    ab, pb, mb = tab[:, 0:128], tab[:, 128:256], tab[:, 256:384]
    af, pf, mf = tab[:, 384:512], tab[:, 512:640], tab[:, 640:768]
    nq = mm(C_NQ, C_NKV)
    nq_ref[...] = nq
    nqr_ref[:, 0:128] = _rope128(nq[:, 0:128], ab, pb, mb)
    nqr_ref[:, 128:256] = _rope128(nq[:, 128:256], ab, pb, mb)
    nkv = mm(C_NKV, C_GQK)
    put(rows_ref, jnp.concatenate([nkv[:, 0:128], _rope128(nkv[:, 128:256], af, pf, mf)], axis=1))
    put(win_ref, _rope128(nkv[:, 256:384], af, pf, mf))
    gqk_ref[...] = mm(C_GQK, C_GV)
    gv_ref[...] = mm(C_GV, C_GOG)
    gog_ref[...] = mm(C_GOG, C_SMALL)
    sm = mm(C_SMALL, C_END)
    glog_ref[...] = _log_sigmoid(_wdot(sm, wgk_ref[...]) + bgk_ref[...]) * (1.0 / GLA_TAU)
    smb = sm + sb_ref[...]
    lane = lax.broadcasted_iota(I32, smb.shape, 1)
    small = jnp.where(lane < SM_NG, _log_sigmoid(smb), _sigmoid(smb))
    small_ref[...] = small
    if feature_major:
        extra[0][...] = small.T[0:SUBLANES, :]


def _in_projection(x_all, row0, n_rows, tm, g, w_r, sb, wgk, bgk, tab, tab_period, batch=None, layer=None,
                   prev=None):
    assert n_rows % tm == 0 and row0 % tm == 0 and tab_period % tm == 0
    nt = n_rows // tm
    b0 = row0 // tm
    npd = tab_period // tm
    widths = (256, 512, 256, 256, 256, 128, 512, 512, 512, 256, 128)
    FKV, ROWS, WIN = 1, 4, 5
    full = lambda shape: pl.BlockSpec(shape, lambda i: (0, 0))
    row_spec = lambda w: pl.BlockSpec((tm, w), lambda i: (i, 0))
    carried = [] if prev is None else list(prev)
    if batch is not None:
        B, S, depth = batch
        per = S // tm
        assert n_rows == B * S and S % tm == 0
        t_spec = lambda w: pl.BlockSpec((None, w, tm), lambda i: (i // per, 0, i % per))
        l_spec = lambda w: pl.BlockSpec((None, None, w, tm), lambda i: (layer, i // per, 0, i % per))
        out_specs = [l_spec(w) if k in (FKV, ROWS) else t_spec(w) if k == WIN else row_spec(w)
                     for k, w in enumerate(widths)] + [t_spec(SUBLANES)]
        out_shape = [jax.ShapeDtypeStruct((depth, B, w, S) if k in (FKV, ROWS) else (B, w, S) if k == WIN
                                          else (n_rows, w), F32)
                     for k, w in enumerate(widths)] + [jax.ShapeDtypeStruct((B, SUBLANES, S), F32)]
    else:
        out_specs = [row_spec(w) for w in widths]
        out_shape = [jax.ShapeDtypeStruct((n_rows, w), F32) for w in widths]
    return pl.pallas_call(
        functools.partial(_inproj_kernel, feature_major=batch is not None, n_carried=len(carried)),
        grid=(nt,),
        in_specs=[
            pl.BlockSpec((tm, D_MODEL), lambda i: (b0 + i, 0)),
            full((1, D_MODEL)),
            full((D_MODEL, C_END)),
            full((1, LANES)),
            full((LANES, 256)),
            full((1, 256)),
            pl.BlockSpec((tm, 768), lambda i: (i % npd, 0)),
        ] + [pl.BlockSpec(memory_space=pl.ANY)] * len(carried),
        out_specs=out_specs,
        out_shape=out_shape,
        input_output_aliases={7: FKV, 8: ROWS} if carried else {},
        compiler_params=_params(("parallel",)),
        name="in_projection",
    )(x_all, g, w_r, sb, wgk, bgk, tab, *carried)


def _cumsum_kernel(sm_ref, cr_ref, carry):
    t = pl.program_id(1)
    ts = sm_ref.shape[1]

    @pl.when(t == 0)
    def _():
        carry[...] = jnp.zeros_like(carry)

    r = lax.broadcasted_iota(I32, (ts, ts), 0)
    c = lax.broadcasted_iota(I32, (ts, ts), 1)
    cs = _dot01_r(sm_ref[...], r <= c) + carry[...]
    carry[...] = cs[:, ts - 1:ts]
    cr_ref[...] = cs * LOG2E


def _fox_cumsum(small_t, ts):
    B, _, S = small_t.shape
    spec = pl.BlockSpec((None, SUBLANES, ts), lambda b, t: (b, 0, t))
    return pl.pallas_call(
        _cumsum_kernel,
        grid=(B, S // ts),
        in_specs=[spec],
        out_specs=spec,
        out_shape=jax.ShapeDtypeStruct((B, SUBLANES, S), F32),
        scratch_shapes=[pltpu.VMEM((SUBLANES, 1), F32)],
        compiler_params=_params(("parallel", "arbitrary")),
        name="fox_cumsum",
    )(small_t)


def _pair_mask(shape, h, axis=1):
    return (lax.broadcasted_iota(I32, shape, axis) // HEAD_DIM) == (h % 2)


def _fox_prompt_kernel(q_ref, kv_ref, cr_ref, o_ref, *scratch):
    i = pl.program_id(1)
    j = pl.program_id(2)
    nk = pl.num_programs(2)
    tq = q_ref.shape[0]
    tk = kv_ref.shape[1]
    q_sc, m_sc, acc_sc = scratch[0:N_HEADS], scratch[N_HEADS:2 * N_HEADS], scratch[2 * N_HEADS:]

    @pl.when(j == 0)
    def _():
        for h in range(N_HEADS):
            m_sc[h][...] = jnp.full_like(m_sc[h], NEG)
            acc_sc[h][...] = jnp.zeros_like(acc_sc[h])
            slab = q_ref[:, (h // 2) * LANES:(h // 2 + 1) * LANES] * (SCALE * LOG2E)
            q_sc[h][...] = jnp.where(_pair_mask(slab.shape, h), slab, 0.0).astype(BF16)

    def tile(diagonal):
        k_slabs = [kv_ref[c0:c0 + LANES, :].astype(BF16) for c0 in (0, LANES)]
        if diagonal:
            mask = lax.broadcasted_iota(I32, (1, tk), 1) <= lax.broadcasted_iota(I32, (tq, 1), 0)
        for h in range(N_HEADS):
            v_slab = kv_ref[D_FOX + (h // 2) * LANES:D_FOX + (h // 2 + 1) * LANES, :]
            v_aug = jnp.where(_pair_mask(v_slab.shape, h, 0), v_slab, 1.0).astype(BF16)
            s = jnp.dot(q_sc[h][...], k_slabs[h // 2], preferred_element_type=F32) - cr_ref[h:h + 1, :]
            if diagonal:
                s = jnp.where(mask, s, NEG)
            m_old = m_sc[h][...]
            m_new = jnp.maximum(m_old, jnp.max(s, axis=-1, keepdims=True))
            p = jnp.exp2(s - m_new).astype(BF16)
            acc_sc[h][...] = (jnp.exp2(m_old - m_new) * acc_sc[h][...]
                              + lax.dot_general(p, v_aug, NT, preferred_element_type=F32))
            m_sc[h][...] = m_new

    @pl.when(j < i)
    def _():
        tile(False)

    @pl.when(j == i)
    def _():
        tile(True)

    @pl.when(j == nk - 1)
    def _():
        for h in range(N_HEADS):
            a = acc_sc[h][...]
            lo = (h % 2) * HEAD_DIM
            den = a[:, HEAD_DIM - lo:HEAD_DIM - lo + 1]
            o_ref[:, h * HEAD_DIM:(h + 1) * HEAD_DIM] = a[:, lo:lo + HEAD_DIM] / den


def _fox_prompt(fq, fkv_all, l, cr, B, S, t):
    n = S // t
    return pl.pallas_call(
        _fox_prompt_kernel,
        grid=(B, n, n),
        in_specs=[
            pl.BlockSpec((t, D_FOX), lambda b, i, j: (b * n + i, 0)),
            pl.BlockSpec((None, None, 2 * D_FOX, t), lambda b, i, j: (l, b, 0, jnp.minimum(i, j))),
            pl.BlockSpec((None, SUBLANES, t), lambda b, i, j: (b, 0, jnp.minimum(i, j))),
        ],
        out_specs=pl.BlockSpec((t, D_FOX), lambda b, i, j: (b * n + i, 0)),
        out_shape=jax.ShapeDtypeStruct((B * S, D_FOX), F32),
        scratch_shapes=([pltpu.VMEM((t, LANES), BF16)] * N_HEADS + [pltpu.VMEM((t, 1), F32)] * N_HEADS
                        + [pltpu.VMEM((t, LANES), F32)] * N_HEADS),
        compiler_params=_params(("parallel", "parallel", "arbitrary")),
        name="fox_prompt",
    )(fq, fkv_all, cr)


def _compress_kernel(x_ref, pe_ref, w_ref, o_ref, *, exact):
    for s in range(2):
        x = x_ref[s] + pe_ref[s]
        w = w_ref[s]
        if exact:
            y = jnp.dot(x, w, precision=HI, preferred_element_type=F32)
        else:
            xh = x.astype(BF16)
            xl = (x - xh.astype(F32)).astype(BF16)
            wh = w.astype(BF16)
            wl = (w - wh.astype(F32)).astype(BF16)
            y = (jnp.dot(xh, wh, preferred_element_type=F32) + jnp.dot(xl, wh, preferred_element_type=F32)
                 + jnp.dot(xh, wl, preferred_element_type=F32))
        o_ref[:, s * HEAD_DIM:(s + 1) * HEAD_DIM] = y


def _compress(x3, pe, w, tr, exact, row0=0, n_rows=None):
    K = x3.shape[2]
    R = x3.shape[1] if n_rows is None else n_rows
    assert R % tr == 0 and row0 % tr == 0
    b0 = row0 // tr
    return pl.pallas_call(
        functools.partial(_compress_kernel, exact=exact),
        grid=(R // tr,),
        in_specs=[pl.BlockSpec((2, tr, K), lambda i: (0, b0 + i, 0)),
                  pl.BlockSpec((2, 1, K), lambda i: (0, 0, 0)),
                  pl.BlockSpec((2, K, HEAD_DIM), lambda i: (0, 0, 0))],
        out_specs=pl.BlockSpec((tr, LANES), lambda i: (i, 0)),
        out_shape=jax.ShapeDtypeStruct((R, LANES), F32),
        compiler_params=_params(("parallel",)),
        name="nsa_compress",
    )(x3, pe, w)


def _block_major(kv, n_blocks):
    return kv.reshape(n_blocks, CMP_LEN, 2, HEAD_DIM).transpose(2, 0, 1, 3).reshape(
        2, n_blocks, CMP_LEN * HEAD_DIM)


def _order_key(x):
    b = lax.bitcast_convert_type(x, I32)
    return jnp.where(b < 0, b ^ jnp.int32(0x7FFFFFFF), b)


def _nsa_prompt_kernel(nq_ref, nqr_ref, sm_ref, cmp_ref, rows_ref, win_ref, o_ref, qc_sc, qx_sc, *, tk):
    QB = nq_ref.shape[0]
    S = rows_ref.shape[1]
    nb = cmp_ref.shape[0]
    nsel = S // SEL_LEN
    i = pl.program_id(1)
    qs = i * QB
    qpos = qs + lax.broadcasted_iota(I32, (QB, 1), 0)

    HQ = N_HEADS * QB
    lo_half = lax.broadcasted_iota(I32, (QB, LANES), 1) < HEAD_DIM

    def stack_heads(ref, scale, dst):
        for h in range(N_HEADS):
            slab = ref[:, (h // 2) * LANES:(h // 2 + 1) * LANES] * scale
            if h % 2:
                slab = pltpu.roll(slab, HEAD_DIM, 1)
            dst[h * QB:(h + 1) * QB, :] = jnp.where(lo_half, slab, 0.0).astype(dst.dtype)

    cmp = cmp_ref[...]
    n_l = lax.broadcasted_iota(I32, (1, nb), 1)
    complete = ((n_l + 1) * CMP_LEN - 1) <= qpos
    stack_heads(nq_ref, SCALE, qc_sc)
    s = _dot_nt_hilo(qc_sc[...], cmp).reshape(N_HEADS, QB, nb)
    s = jnp.where(complete[None], s, NEG)
    e = jnp.exp(s - jnp.max(s, axis=-1, keepdims=True))
    p = e / jnp.sum(e, axis=-1, keepdims=True) * complete.astype(F32)[None]
    o_cmp = _bdot(p.reshape(HQ, nb), cmp)
    psum = jnp.sum(p, axis=0)

    pj = lax.broadcasted_iota(I32, (nsel, nb), 0)
    pn = lax.broadcasted_iota(I32, (nsel, nb), 1)
    imp_t = _dot01_nt(pn // (SEL_LEN // CMP_LEN) == pj, psum)
    jt = (qs + lax.broadcasted_iota(I32, (1, QB), 1)) // SEL_LEN
    jj = lax.broadcasted_iota(I32, (nsel, 1), 0)
    score = jnp.where(jj == jt, 2.0 * SEL_FORCE,
                      jnp.where((jj == 0) | (jj == jt - 1), SEL_FORCE,
                                jnp.where(jj <= jt, imp_t + 0.0, -1.0)))
    key = _order_key(score)
    key_m1 = key - 1
    ngrp = nsel // SUBLANES
    sub = lax.broadcasted_iota(I32, (SUBLANES, QB), 0)
    kg = [key[r * SUBLANES:(r + 1) * SUBLANES, :] for r in range(ngrp)]
    kg1 = [key_m1[r * SUBLANES:(r + 1) * SUBLANES, :] for r in range(ngrp)]
    cnt = [jnp.zeros((SUBLANES, QB), I32) for _ in range(ngrp)]
    for jp in range(nsel):
        g = jp // SUBLANES
        row = key[jp:jp + 1, :]
        mixed = jnp.where(sub > (jp % SUBLANES), kg1[g], kg[g])
        for r in range(ngrp):
            thr = kg[r] if r < g else (kg1[r] if r > g else mixed)
            cnt[r] = cnt[r] + (row > thr).astype(I32)
    sel_t = jnp.concatenate([(c < TOP_N).astype(F32) for c in cnt], axis=0)
    if nsel < QB:
        sel_t = jnp.concatenate([sel_t, jnp.zeros((QB - nsel, QB), F32)], axis=0)
    sel = sel_t.T.astype(BF16)

    stack_heads(nqr_ref, SCALE * LOG2E, qx_sc)
    qx = qx_sc[...]

    def attend(valid, slab_t, m_old, acc_old):
        n = slab_t.shape[1]
        s = jnp.dot(qx, slab_t.astype(BF16), preferred_element_type=F32)
        s = jnp.where(valid[None], s.reshape(N_HEADS, QB, n), NEG).reshape(HQ, n)
        m_new = jnp.maximum(m_old, jnp.max(s, axis=-1, keepdims=True))
        p = jnp.exp2(s - m_new).astype(BF16)
        ones_k = lax.broadcasted_iota(I32, slab_t.shape, 0) < HEAD_DIM
        v_aug = jnp.where(ones_k, 1.0, slab_t).astype(BF16)
        acc = jnp.exp2(m_old - m_new) * acc_old + lax.dot_general(p, v_aug, NT, preferred_element_type=F32)
        return m_new, acc

    jrow = lax.broadcasted_iota(I32, (QB, 1), 0)

    def sel_tile(k0, m_old, acc_old, diagonal):
        kpos = k0 + lax.broadcasted_iota(I32, (1, tk), 1)
        expand = (jrow == kpos // SEL_LEN).astype(BF16)
        valid = jnp.dot(sel, expand, preferred_element_type=F32) > 0.5
        if diagonal:
            valid = valid & (kpos <= qpos)
        return attend(valid, rows_ref[2 * HEAD_DIM:4 * HEAD_DIM, pl.ds(k0, tk)], m_old, acc_old)

    n_full = qs // tk
    init = (jnp.full((HQ, 1), NEG, F32), jnp.zeros((HQ, LANES), F32))
    m_s, acc_s = lax.fori_loop(
        0, n_full, lambda t, c: sel_tile(pl.multiple_of(t * tk, tk), c[0], c[1], False), init)
    _, acc_s = sel_tile(pl.multiple_of(n_full * tk, tk), m_s, acc_s, True)

    wlen = WINDOW + QB
    w0 = pl.multiple_of(jnp.maximum(qs - WINDOW, 0), QB)
    wpos = w0 + lax.broadcasted_iota(I32, (1, wlen), 1)
    d = qpos - wpos
    _, acc_w = attend((d >= 0) & (d < WINDOW), win_ref[:, pl.ds(w0, wlen)],
                      jnp.full((HQ, 1), NEG, F32), jnp.zeros((HQ, LANES), F32))

    sm = sm_ref[...]
    for h in range(N_HEADS):
        rs = slice(h * QB, (h + 1) * QB)
        o_sel = acc_s[rs] * (1.0 / acc_s[rs, 0:1])
        o_win = acc_w[rs] * (1.0 / acc_w[rs, 0:1])
        c = SM_NG + 3 * h
        mix = sm[:, c:c + 1] * o_cmp[rs] + sm[:, c + 1:c + 2] * o_sel + sm[:, c + 2:c + 3] * o_win
        if h % 2 == 0:
            mix = pltpu.roll(mix, HEAD_DIM, 1)
        lo = (h % 2) * HEAD_DIM
        o_ref[:, h * HEAD_DIM:(h + 1) * HEAD_DIM] = mix[:, lo:lo + HEAD_DIM]


def _nsa_prompt(nq, nqr, small, cmp, rows_all, l, win, B, S, tk):
    QB = 128
    nq_t = S // QB
    nb = S // CMP_LEN
    assert S % tk == 0 and S >= WINDOW + QB
    return pl.pallas_call(
        functools.partial(_nsa_prompt_kernel, tk=tk),
        grid=(B, nq_t),
        in_specs=[
            pl.BlockSpec((QB, D_NSA), lambda b, i: (b * nq_t + i, 0)),
            pl.BlockSpec((QB, D_NSA), lambda b, i: (b * nq_t + i, 0)),
            pl.BlockSpec((QB, LANES), lambda b, i: (b * nq_t + i, 0)),
            pl.BlockSpec((nb, LANES), lambda b, i: (b, 0)),
            pl.BlockSpec((None, None, 4 * HEAD_DIM, S), lambda b, i: (l, b, 0, 0)),
            pl.BlockSpec((None, 2 * HEAD_DIM, S), lambda b, i: (b, 0, 0)),
        ],
        out_specs=pl.BlockSpec((QB, D_NSA), lambda b, i: (b * nq_t + i, 0)),
        out_shape=jax.ShapeDtypeStruct((B * S, D_NSA), F32),
        scratch_shapes=[pltpu.VMEM((N_HEADS * QB, LANES), F32), pltpu.VMEM((N_HEADS * QB, LANES), BF16)],
        compiler_params=_params(("parallel", "parallel")),
        name="nsa_prompt",
    )(nq, nqr, small, cmp, rows_all, win)


def _gla_prompt_kernel(qk_ref, v_ref, g_ref, o_ref, st_ref, s_sc):
    t = pl.program_id(1)
    nt = pl.num_programs(1)
    tc = qk_ref.shape[0]
    C = GLA_CHUNK

    @pl.when(t == 0)
    def _():
        s_sc[...] = jnp.zeros_like(s_sc)

    r = lax.broadcasted_iota(I32, (tc, tc), 0)
    c = lax.broadcasted_iota(I32, (tc, tc), 1)
    same = (r // C) == (c // C)
    causal = same & (c <= r)
    g = g_ref[...]
    gcum = _dot01(causal, g)
    g_t = g.T
    gcum_t = _dot01_r(g_t, same & (r <= c))
    gtot_t = _dot01_r(g_t, same)
    q_e = (qk_ref[:, 0:D_FOX] * SCALE * jnp.exp(gcum)).astype(BF16)
    k_e = (qk_ref[:, D_FOX:2 * D_FOX] * jnp.exp(-gcum)).astype(BF16)
    kd_t = (qk_ref[:, D_FOX:2 * D_FOX].T * jnp.exp(gtot_t - gcum_t)).astype(BF16)
    decay_t = jnp.exp(gtot_t)
    for h in range(N_HEADS):
        hs = slice(h * HEAD_DIM, (h + 1) * HEAD_DIM)
        v = v_ref[:, h * GLA_DV:(h + 1) * GLA_DV].astype(BF16)
        a = jnp.where(causal, lax.dot_general(q_e[:, hs], k_e[:, hs], NT, preferred_element_type=F32), 0.0)
        o_intra = jnp.dot(a.astype(BF16), v, preferred_element_type=F32)
        state = s_sc[h]
        for ci in range(tc // C):
            rs = slice(ci * C, (ci + 1) * C)
            o_ref[rs, h * GLA_DV:(h + 1) * GLA_DV] = (
                o_intra[rs] + jnp.dot(q_e[rs, hs], state.astype(BF16), preferred_element_type=F32))
            state = (decay_t[hs, ci * C:ci * C + 1] * state
                     + jnp.dot(kd_t[hs, rs], v[rs], preferred_element_type=F32))
        s_sc[h] = state

    @pl.when(t == nt - 1)
    def _():
        st_ref[...] = s_sc[...]


def _gla_prompt(gqk, gv, glog, B, S, tc):
    nt = S // tc
    return pl.pallas_call(
        _gla_prompt_kernel,
        grid=(B, nt),
        in_specs=[pl.BlockSpec((tc, 2 * D_FOX), lambda b, t: (b * nt + t, 0)),
                  pl.BlockSpec((tc, D_GLA), lambda b, t: (b * nt + t, 0)),
                  pl.BlockSpec((tc, D_FOX), lambda b, t: (b * nt + t, 0))],
        out_specs=[pl.BlockSpec((tc, D_GLA), lambda b, t: (b * nt + t, 0)),
                   pl.BlockSpec((None, N_HEADS, HEAD_DIM, GLA_DV), lambda b, t: (b, 0, 0, 0))],
        out_shape=[jax.ShapeDtypeStruct((B * S, D_GLA), F32),
                   jax.ShapeDtypeStruct((B, N_HEADS, HEAD_DIM, GLA_DV), F32)],
        scratch_shapes=[pltpu.VMEM((N_HEADS, HEAD_DIM, GLA_DV), F32)],
        compiler_params=_params(("parallel", "arbitrary")),
        name="gla_prompt",
    )(gqk, gv, glog)


def _outproj_kernel(x_ref, of_ref, on_ref, og_ref, gog_ref, gn_ref, w_ref, o_ref):
    acc = _wdot(of_ref[...], w_ref[0:D_FOX, :])
    acc = acc + _wdot(on_ref[...], w_ref[D_FOX:D_FOX + D_NSA, :])
    for h in range(N_HEADS):
        hs = slice(h * GLA_DV, (h + 1) * GLA_DV)
        z = _rms(og_ref[:, hs], gn_ref[...]) * _silu(gog_ref[:, hs])
        w0 = D_FOX + D_NSA + h * GLA_DV
        acc = acc + _wdot(z, w_ref[w0:w0 + GLA_DV, :])
    o_ref[...] = x_ref[...] + acc


def _out_projection(x, o_fox, o_nsa, o_gla, gog, gn, w_out, tm):
    T = x.shape[0]
    assert T % tm == 0
    row = lambda w: pl.BlockSpec((tm, w), lambda i: (i, 0))
    return pl.pallas_call(
        _outproj_kernel,
        grid=(T // tm,),
        in_specs=[row(D_MODEL), row(D_FOX), row(D_NSA), row(D_GLA), row(D_GLA),
                  pl.BlockSpec((1, GLA_DV), lambda i: (0, 0)),
                  pl.BlockSpec((D_MODEL, D_MODEL), lambda i: (0, 0))],
        out_specs=row(D_MODEL),
        out_shape=jax.ShapeDtypeStruct((T, D_MODEL), F32),
        compiler_params=_params(("parallel",)),
        name="out_projection",
    )(x, o_fox, o_nsa, o_gla, gog, gn, w_out)


def _dense_ffn_kernel(x_ref, g_ref, wg_ref, wu_ref, wd_ref, o_ref, h_sc, acc_sc):
    f = pl.program_id(1)
    nf = pl.num_programs(1)

    @pl.when(f == 0)
    def _():
        h_sc[...] = _rms(x_ref[...], g_ref[...]).astype(h_sc.dtype)
        acc_sc[...] = jnp.zeros_like(acc_sc)

    h = h_sc[...]
    a = _wdot(h, wg_ref[...])
    u = _wdot(h, wu_ref[...])
    acc_sc[...] += _wdot(_silu(a) * u, wd_ref[...])

    @pl.when(f == nf - 1)
    def _():
        o_ref[...] = x_ref[...] + acc_sc[...]


def _dense_ffn(x, g, wg, wu, wd, tm, tf):
    T = x.shape[0]
    assert T % tm == 0 and D_FF % tf == 0
    return pl.pallas_call(
        _dense_ffn_kernel,
        grid=(T // tm, D_FF // tf),
        in_specs=[pl.BlockSpec((tm, D_MODEL), lambda i, f: (i, 0)),
                  pl.BlockSpec((1, D_MODEL), lambda i, f: (0, 0)),
                  pl.BlockSpec((D_MODEL, tf), lambda i, f: (0, f)),
                  pl.BlockSpec((D_MODEL, tf), lambda i, f: (0, f)),
                  pl.BlockSpec((tf, D_MODEL), lambda i, f: (f, 0))],
        out_specs=pl.BlockSpec((tm, D_MODEL), lambda i, f: (i, 0)),
        out_shape=jax.ShapeDtypeStruct((T, D_MODEL), F32),
        scratch_shapes=[pltpu.VMEM((tm, D_MODEL), wg.dtype), pltpu.VMEM((tm, D_MODEL), F32)],
        compiler_params=_params(("parallel", "arbitrary")),
        name="dense_ffn",
    )(x, g, wg, wu, wd)


def _router_kernel(x_ref, g_ref, wr_ref, h_ref, r_ref):
    h = _rms(x_ref[...], g_ref[...])
    h_ref[...] = h
    logits = jnp.dot(h, wr_ref[...], precision=HI, preferred_element_type=F32)
    lane = lax.broadcasted_iota(I32, logits.shape, 1)
    lg = jnp.where(lane < N_EXPERTS, logits, -jnp.inf)
    m1 = jnp.max(lg, axis=-1, keepdims=True)
    i1 = jnp.min(jnp.where(lg == m1, lane, LANES), axis=-1, keepdims=True)
    lg2 = jnp.where(lane == i1, -jnp.inf, lg)
    m2 = jnp.max(lg2, axis=-1, keepdims=True)
    i2 = jnp.min(jnp.where(lg2 == m2, lane, LANES), axis=-1, keepdims=True)
    e = jnp.exp(m2 - m1)
    den = 1.0 + e
    r_ref[...] = jnp.where(lane == 0, i1.astype(F32),
                           jnp.where(lane == 1, i2.astype(F32),
                                     jnp.where(lane == 2, 1.0 / den,
                                               jnp.where(lane == 3, e / den, 0.0))))


def _router(x, g, wr_pad, tm):
    T = x.shape[0]
    assert T % tm == 0
    return pl.pallas_call(
        _router_kernel,
        grid=(T // tm,),
        in_specs=[pl.BlockSpec((tm, D_MODEL), lambda i: (i, 0)),
                  pl.BlockSpec((1, D_MODEL), lambda i: (0, 0)),
                  pl.BlockSpec((D_MODEL, LANES), lambda i: (0, 0))],
        out_specs=[pl.BlockSpec((tm, D_MODEL), lambda i: (i, 0)),
                   pl.BlockSpec((tm, LANES), lambda i: (i, 0))],
        out_shape=[jax.ShapeDtypeStruct((T, D_MODEL), F32),
                   jax.ShapeDtypeStruct((T, LANES), F32)],
        compiler_params=_params(("parallel",)),
        name="moe_router",
    )(x, g, wr_pad)


GATHER_WINDOW = 32
SC_WORKERS = 32


def _row_gather(src, idx):
    n = idx.shape[0]
    step = GATHER_WINDOW * SC_WORKERS
    n_pad = -(-n // step) * step
    if n_pad != n:
        idx = jnp.concatenate([idx, jnp.zeros((n_pad - n,), idx.dtype)])
    width = src.shape[1]
    per_worker = n_pad // SC_WORKERS
    mesh = plsc.VectorSubcoreMesh(core_axis_name="core", subcore_axis_name="subcore")

    @functools.partial(pl.kernel, out_type=jax.ShapeDtypeStruct((n_pad, width), src.dtype), mesh=mesh,
                       scratch_types=[pltpu.VMEM((per_worker,), I32),
                                      pltpu.VMEM((GATHER_WINDOW, width), src.dtype)],
                       name="row_gather")
    def gather(src_hbm, idx_hbm, dst_hbm, idx_v, buf):
        worker = lax.axis_index("core") * (SC_WORKERS // 2) + lax.axis_index("subcore")
        base = worker * per_worker
        pltpu.sync_copy(idx_hbm.at[pl.ds(base, per_worker)], idx_v)

        @pl.loop(0, per_worker // GATHER_WINDOW)
        def _(j):
            pltpu.sync_copy(src_hbm.at[idx_v.at[pl.ds(j * GATHER_WINDOW, GATHER_WINDOW)]], buf)
            pltpu.sync_copy(buf, dst_hbm.at[pl.ds(base + j * GATHER_WINDOW, GATHER_WINDOW)])

    return gather(src, idx)


def _moe_ffn_kernel(be_ref, nu_ref, x_ref, wg_ref, wu_ref, wd_ref, o_ref, acc_sc):
    b = pl.program_id(0)
    f = pl.program_id(1)
    nf = pl.num_programs(1)
    used = b < nu_ref[0]

    @pl.when(used)
    def _():
        @pl.when(f == 0)
        def _():
            acc_sc[...] = jnp.zeros_like(acc_sc)

        x = x_ref[...].astype(BF16)
        a = jnp.dot(x, wg_ref[...].astype(BF16), preferred_element_type=F32)
        u = jnp.dot(x, wu_ref[...].astype(BF16), preferred_element_type=F32)
        acc_sc[...] += _bdot(_silu(a) * u, wd_ref[...])

        @pl.when(f == nf - 1)
        def _():
            o_ref[...] = acc_sc[...]

    @pl.when(jnp.logical_not(used) & (f == nf - 1))
    def _():
        o_ref[...] = jnp.zeros_like(o_ref)


def _moe_ffn(xb, block_e, n_used, wg, wu, wd, blk, tf):
    cap = xb.shape[0]
    nb = cap // blk
    nf = D_FF // tf

    def bsel(b, nu):
        return jnp.minimum(b, nu[0] - 1)

    def fsel(b, f, nu):
        return jnp.where(b < nu[0], f, nf - 1)

    return pl.pallas_call(
        _moe_ffn_kernel,
        grid_spec=pltpu.PrefetchScalarGridSpec(
            num_scalar_prefetch=2,
            grid=(nb, nf),
            in_specs=[
                pl.BlockSpec((blk, D_MODEL), lambda b, f, be, nu: (bsel(b, nu), 0)),
                pl.BlockSpec((None, D_MODEL, tf), lambda b, f, be, nu: (be[bsel(b, nu)], 0, fsel(b, f, nu))),
                pl.BlockSpec((None, D_MODEL, tf), lambda b, f, be, nu: (be[bsel(b, nu)], 0, fsel(b, f, nu))),
                pl.BlockSpec((None, tf, D_MODEL), lambda b, f, be, nu: (be[bsel(b, nu)], fsel(b, f, nu), 0)),
            ],
            out_specs=pl.BlockSpec((blk, D_MODEL), lambda b, f, be, nu: (b, 0)),
            scratch_shapes=[pltpu.VMEM((blk, D_MODEL), F32)],
        ),
        out_shape=jax.ShapeDtypeStruct((cap, D_MODEL), F32),
        compiler_params=_params(("arbitrary", "arbitrary")),
        name="moe_ffn",
    )(block_e, n_used, xb, wg, wu, wd)


def _moe_combine_kernel(x_ref, y1_ref, y2_ref, r_ref, g_ref, o_ref, *, final):
    r = r_ref[...]
    y = x_ref[...] + (r[:, 2:3] * y1_ref[...] + r[:, 3:4] * y2_ref[...])
    o_ref[...] = _rms(y, g_ref[...]) if final else y


def _moe_combine(x, y1, y2, route, tm, final_g):
    T = x.shape[0]
    g = jnp.ones((1, D_MODEL), F32) if final_g is None else final_g
    return pl.pallas_call(
        functools.partial(_moe_combine_kernel, final=final_g is not None),
        grid=(T // tm,),
        in_specs=[pl.BlockSpec((tm, D_MODEL), lambda i: (i, 0)),
                  pl.BlockSpec((tm, D_MODEL), lambda i: (i, 0)),
                  pl.BlockSpec((tm, D_MODEL), lambda i: (i, 0)),
                  pl.BlockSpec((tm, LANES), lambda i: (i, 0)),
                  pl.BlockSpec((1, D_MODEL), lambda i: (0, 0))],
        out_specs=pl.BlockSpec((tm, D_MODEL), lambda i: (i, 0)),
        out_shape=jax.ShapeDtypeStruct((T, D_MODEL), F32),
        compiler_params=_params(("parallel",)),
        name="moe_combine",
    )(x, y1, y2, route, g)


def _moe_plan(e_top, blk):
    T = e_top.shape[0]
    n = 2 * T
    flat_e = e_top.reshape(-1)
    onehot = (flat_e[:, None] == jnp.arange(N_EXPERTS, dtype=I32)[None, :]).astype(I32)
    csum = jnp.cumsum(onehot, axis=0)
    rank = jnp.sum((csum - onehot) * onehot, axis=1)
    counts = csum[-1]
    padded = (counts + blk - 1) // blk * blk
    ends = jnp.cumsum(padded)
    pstart = ends - padded
    dest = (pstart[flat_e] + rank).astype(I32)
    n_blocks = -(-n // blk) + N_EXPERTS
    cap = n_blocks * blk
    slot_tok = (jnp.arange(cap, dtype=I32) % T).at[dest].set(jnp.arange(n, dtype=I32) // 2)
    first = jnp.arange(n_blocks, dtype=I32) * blk
    block_e = jnp.minimum(jnp.sum((ends[None, :] <= first[:, None]).astype(I32), axis=1), N_EXPERTS - 1)
    n_used = (ends[-1] // blk).astype(I32).reshape(1)
    return dest, slot_tok, block_e, n_used


def _moe_dispatch(x, g, wr_pad, tm, blk):
    h, route = _router(x, g, wr_pad, tm)
    dest, slot_tok, block_e, n_used = _moe_plan(route[:, 0:2].astype(I32), blk)
    return dict(x=x, route=route, dest=dest, xb=_row_gather(h, slot_tok), block_e=block_e, n_used=n_used)


def _moe_finish(d, wg, wu, wd, tm, blk, tf, final_g):
    yb = _moe_ffn(d['xb'], d['block_e'], d['n_used'], wg, wu, wd, blk, tf)
    d2 = d['dest'].reshape(-1, 2)
    return _moe_combine(d['x'], _row_gather(yb, d2[:, 0]), _row_gather(yb, d2[:, 1]), d['route'], tm, final_g)


def _norm_kernel(x_ref, g_ref, o_ref):
    o_ref[...] = _rms(x_ref[...], g_ref[...])


def _final_norm(x, g, tm):
    T = x.shape[0]
    return pl.pallas_call(
        _norm_kernel,
        grid=(T // tm,),
        in_specs=[pl.BlockSpec((tm, D_MODEL), lambda i: (i, 0)),
                  pl.BlockSpec((1, D_MODEL), lambda i: (0, 0))],
        out_specs=pl.BlockSpec((tm, D_MODEL), lambda i: (i, 0)),
        out_shape=jax.ShapeDtypeStruct((T, D_MODEL), F32),
        compiler_params=_params(("parallel",)),
        name="final_norm",
    )(x, g)


_IN_SPLITS = (D_FOX, D_FOX, D_FOX, N_HEADS, D_NSA, 6 * HEAD_DIM, 3 * N_HEADS,
              D_FOX, D_FOX, D_GLA, GLA_RANK, D_GLA)


def _reorder_w_in(w):
    offs = [0]
    for s in _IN_SPLITS:
        offs.append(offs[-1] + s)
    seg = lambda k: w[:, offs[k]:offs[k + 1]]
    fq, fk, fv, ff, nq, nkv, ng, gq, gk, gv, glr, gog = [seg(k) for k in range(12)]
    pad = jnp.zeros((w.shape[0], LANES - SM_GLR - GLA_RANK), w.dtype)
    return jnp.concatenate([fq, fk, fv, nq, nkv, gq, gk, gv, gog, ff, ng, glr, pad], axis=1)


def _rope_table(pos):
    inv = ROPE_THETA ** (-jnp.arange(ROPE_HALF, dtype=F32) / ROPE_HALF)
    ang = pos.astype(F32)[:, None] * inv[None, :]
    cos, sin = jnp.cos(ang), jnp.sin(ang)
    P = pos.shape[0]
    one = jnp.ones((P, HEAD_DIM - ROPE_DIM), F32)
    zero = jnp.zeros((P, HEAD_DIM - ROPE_DIM), F32)
    z8 = jnp.zeros((P, ROPE_HALF), F32)
    a64 = jnp.concatenate([cos, cos, one], axis=1)
    p64 = jnp.concatenate([z8, sin, zero], axis=1)
    m64 = jnp.concatenate([-sin, z8, zero], axis=1)
    i64 = jnp.ones((P, HEAD_DIM), F32)
    o64 = jnp.zeros((P, HEAD_DIM), F32)
    return jnp.concatenate([a64, a64, p64, p64, m64, m64, a64, i64, p64, o64, m64, o64], axis=1)


def _layer_mix_params(l, norm_mix_g, w_in, b_fox_f, w_cmp, pe_cmp, w_gla_gk, b_gla_gk, g_gla_norm, w_out):
    sb = jnp.zeros((1, LANES), F32).at[0, SM_FF:SM_FF + N_HEADS].set(b_fox_f[l])
    wgk = jnp.zeros((LANES, D_FOX), F32).at[SM_GLR:SM_GLR + GLA_RANK].set(w_gla_gk[l])
    w_r = _reorder_w_in(w_in[l])
    return dict(g=norm_mix_g[l].reshape(1, D_MODEL), w_r=w_r.astype(BF16), w_r32=w_r, sb=sb,
                wgk=wgk.astype(BF16), wgk32=wgk, w_out32=w_out[l],
                bgk=b_gla_gk[l].reshape(1, D_FOX), w_cmp=w_cmp[l],
                pe_cmp=pe_cmp[l].reshape(2, 1, CMP_LEN * HEAD_DIM),
                gn=g_gla_norm[l].reshape(1, GLA_DV), w_out=w_out[l].astype(BF16))


def _mix_prompt(x, B, S, l, depth, p, tab, tm, t_fox, tk_sel, tc_gla, carry):
    (fq, fkv_all, nq, nqr, rows_all, win_t, gqk, gv, gog, glog, small, small_t) = _in_projection(
        x, 0, B * S, tm, p['g'], p['w_r'], p['sb'], p['wgk'], p['bgk'], tab, S, batch=(B, S, depth), layer=l,
        prev=(carry['fkv'], carry['rows']) if carry else None)
    carry.update(fkv=fkv_all, rows=rows_all)
    cr = _fox_cumsum(small_t, min(S, Tiles.cumsum))
    o_fox = _fox_prompt(fq, fkv_all, l, cr, B, S, t_fox)
    n_blk = S // CMP_LEN
    blocks = rows_all[l, :, 0:2 * HEAD_DIM, :].reshape(B, 2, HEAD_DIM, n_blk, CMP_LEN).transpose(1, 0, 3, 4, 2)
    blocks = blocks.reshape(2, B * n_blk, CMP_LEN * HEAD_DIM)
    cmp = _compress(blocks, p['pe_cmp'], p['w_cmp'], min(Tiles.compress_rows, B * n_blk), True)
    o_nsa = _nsa_prompt(nq, nqr, small, cmp, rows_all, l, win_t, B, S, tk_sel)
    o_gla, g_state = _gla_prompt(gqk, gv, glog, B, S, tc_gla)
    x_new = _out_projection(x, o_fox, o_nsa, o_gla, gog, p['gn'], p['w_out'], tm)
    wp = min(WINDOW, S)
    return x_new, dict(small=small_t[:, 0:N_HEADS, :], win=win_t[:, :, S - wp:], g_state=g_state)


def _per_head_col(vals):
    r = lax.broadcasted_iota(I32, (SUBLANES, 1), 0)
    out = jnp.zeros((SUBLANES, 1), F32)
    for h, v in enumerate(vals):
        out = out + jnp.where(r == h, v, 0.0)
    return out


def _per_head_row(vals, width):
    grp = lax.broadcasted_iota(I32, (1, width), 1) // HEAD_DIM
    out = jnp.zeros((1, width), F32)
    for h, v in enumerate(vals):
        out = out + jnp.where(grp == h, v, 0.0)
    return out


def _head_lane_sums(row):
    grp = lax.broadcasted_iota(I32, row.shape, 1) // HEAD_DIM
    return [jnp.sum(jnp.where(grp == h, row, 0.0), axis=1, keepdims=True) for h in range(N_HEADS)]


def _cols_of(row):
    return jnp.concatenate([jnp.broadcast_to(row[:, j:j + LANES], (LANES, LANES)).T
                            for j in range(0, row.shape[1], LANES)], axis=0)


def _row_of(col):
    return jnp.concatenate([jnp.broadcast_to(col[j:j + LANES], (LANES, LANES)).T[0:1, :]
                            for j in range(0, col.shape[0], LANES)], axis=1)


def _sublane_group_sum(x):
    return jnp.sum(x.reshape(x.shape[0] // SUBLANES, SUBLANES, x.shape[1]), axis=0)


def _fold_matrix(n_pages):
    r = lax.broadcasted_iota(I32, (n_pages * SUBLANES, n_pages * N_HEADS * SUBLANES), 0)
    c = lax.broadcasted_iota(I32, (n_pages * SUBLANES, n_pages * N_HEADS * SUBLANES), 1)
    blk = c // SUBLANES
    return ((blk // N_HEADS == r // SUBLANES) & (blk % N_HEADS == r % SUBLANES)).astype(F32)


def _fox_decode_kernel(pt_ref, q_ref, kvn_ref, smn_ref, *refs, n_pages):
    del pt_ref
    kv_refs = refs[0:n_pages]
    lf_refs = refs[n_pages:2 * n_pages]
    o_ref, lf_sc, part_sc = refs[2 * n_pages:]
    R = n_pages * SUBLANES
    PG = kv_refs[0].shape[-1]
    row = pl.ds(pl.program_id(0) % SUBLANES, 1)
    q_row = q_ref[row, :]
    kvn = kvn_ref[row, :]
    smn = smn_ref[row, :]
    q_cols = _cols_of(q_row)

    lf_sc[...] = jnp.zeros_like(lf_sc)
    for p in range(n_pages):
        lf_sc[p * SUBLANES:p * SUBLANES + N_HEADS, :] = lf_refs[p][...]
    lft = lf_sc[...]
    k0 = lax.broadcasted_iota(I32, (PG, PG), 0)
    k1 = lax.broadcasted_iota(I32, (PG, PG), 1)
    within = _dot01_r(lft, k0 > k1)
    tot = jnp.broadcast_to(jnp.sum(lft, axis=1, keepdims=True), (R, PG))
    r0 = lax.broadcasted_iota(I32, (R, R), 0)
    r1 = lax.broadcasted_iota(I32, (R, R), 1)
    later = (r1 % SUBLANES == r0 % SUBLANES) & (r1 // SUBLANES > r0 // SUBLANES)
    cross = _dot01(later, tot)
    rr = lax.broadcasted_iota(I32, (R, 1), 0) % SUBLANES
    newcol = jnp.zeros((R, 1), F32)
    for h in range(N_HEADS):
        newcol = newcol + jnp.where(rr == h, smn[:, SM_FF + h:SM_FF + h + 1], 0.0)
    bias = (within + cross + newcol).reshape(n_pages, SUBLANES, PG)

    for p in range(n_pages):
        for h in range(N_HEADS):
            g = p * N_HEADS + h
            part_sc[g * SUBLANES:(g + 1) * SUBLANES, :] = _sublane_group_sum(
                kv_refs[p][0, h] * q_cols[h * HEAD_DIM:(h + 1) * HEAD_DIM])
    s = _dot01(_fold_matrix(n_pages), part_sc[...])
    s3 = s.reshape(n_pages, SUBLANES, PG) * SCALE + bias
    s_new = _per_head_col(_head_lane_sums(q_row * kvn[:, 0:D_FOX])) * SCALE
    m = jnp.max(jnp.max(s3, axis=2, keepdims=True), axis=0)
    m = jnp.maximum(m, s_new)
    p3 = jnp.exp(s3 - m[None])
    pn = jnp.exp(s_new - m)
    inv = 1.0 / (jnp.sum(jnp.sum(p3, axis=2, keepdims=True), axis=0) + pn)
    o_cols = []
    for h in range(N_HEADS):
        acc = jnp.zeros((HEAD_DIM, PG), F32)
        for p in range(n_pages):
            acc = acc + kv_refs[p][1, h] * p3[p, h:h + 1, :]
        o_cols.append(jnp.sum(acc, axis=1, keepdims=True) * inv[h:h + 1])
    w_new = _per_head_row([pn[h:h + 1] * inv[h:h + 1] for h in range(N_HEADS)], D_FOX)
    o_ref[row, :] = _row_of(jnp.concatenate(o_cols, axis=0)) + w_new * kvn[:, D_FOX:2 * D_FOX]


def _fox_decode(l, pt_flat, n_pages, fq, fkv, small, kv_cache_t, lft_cache):
    DB = fq.shape[0]
    PG = kv_cache_t.shape[-1]
    page = lambda p, nz: (lambda b, pt: (l, pt[b * n_pages + p]) + (0,) * nz)
    rows8 = lambda w: pl.BlockSpec((SUBLANES, w), lambda b, pt: (b // SUBLANES, 0))
    return pl.pallas_call(
        functools.partial(_fox_decode_kernel, n_pages=n_pages),
        grid_spec=pltpu.PrefetchScalarGridSpec(
            num_scalar_prefetch=1,
            grid=(DB,),
            in_specs=[rows8(D_FOX), rows8(2 * D_FOX), rows8(LANES)]
            + [pl.BlockSpec((None, None, 2, N_HEADS, HEAD_DIM, PG), page(p, 4)) for p in range(n_pages)]
            + [pl.BlockSpec((None, None, N_HEADS, PG), page(p, 2)) for p in range(n_pages)],
            out_specs=rows8(D_FOX),
            scratch_shapes=[pltpu.VMEM((n_pages * SUBLANES, PG), F32),
                            pltpu.VMEM((n_pages * N_HEADS * SUBLANES, PG), F32)],
        ),
        out_shape=jax.ShapeDtypeStruct((DB, D_FOX), F32),
        compiler_params=_params(("arbitrary",)),
        name="fox_decode",
    )(pt_flat, fq, fkv, small, *([kv_cache_t] * n_pages), *([lft_cache] * n_pages))


def _nsa_decode_kernel(pt_ref, q_ref, qr_ref, rown_ref, winn_ref, sm_ref, win_ref, *refs,
                       n_pages, past_len):
    del pt_ref
    pg_refs = refs[0:n_pages]
    cmp_refs = refs[n_pages:2 * n_pages]
    o_ref, nw_ref, cmp_sc, qc_sc, part_sc, sw_sc = refs[-6:]
    R = n_pages * SUBLANES
    PG = pg_refs[0].shape[-1]
    WB = win_ref.shape[-1]
    per_page = PG // CMP_LEN
    assert per_page <= SUBLANES and PG == 2 * SEL_LEN and R == LANES
    jt = past_len // SEL_LEN
    row = pl.ds(pl.program_id(0) % SUBLANES, 1)
    q_row = q_ref[row, :]
    qr_row = qr_ref[row, :]
    rown = rown_ref[row, :]
    winn = winn_ref[row, :]
    smn = sm_ref[row, :]
    qr_cols = _cols_of(qr_row)
    rep4 = lambda r64: jnp.concatenate([r64] * N_HEADS, axis=1)

    qc_sc[...] = jnp.zeros_like(qc_sc)
    for h in range(N_HEADS):
        qc_sc[h:h + 1, 0:HEAD_DIM] = q_row[:, h * HEAD_DIM:(h + 1) * HEAD_DIM]
    head_row = lax.broadcasted_iota(I32, (SUBLANES, 1), 0) < N_HEADS

    cmp_sc[...] = jnp.zeros_like(cmp_sc)
    for p in range(n_pages):
        cmp_sc[p * SUBLANES:p * SUBLANES + per_page, :] = cmp_refs[p][...]
    cmpa = cmp_sc[...]
    lane = lax.broadcasted_iota(I32, (1, R), 1)
    blk = per_page * (lane // SUBLANES) + lane % SUBLANES
    complete = (lane % SUBLANES < per_page) & ((blk + 1) * CMP_LEN - 1 <= past_len)
    s = _dot_nt_hilo(qc_sc[...], cmpa) * SCALE
    s = jnp.where(complete, s, NEG)
    e = jnp.exp(s - jnp.max(s, axis=-1, keepdims=True))
    pc = e / jnp.sum(e, axis=-1, keepdims=True) * complete.astype(F32)
    vcb_t = cmpa.T[HEAD_DIM:2 * HEAD_DIM, :]
    o_cmp = [jnp.sum(vcb_t * pc[h:h + 1, :], axis=1, keepdims=True) for h in range(N_HEADS)]

    imp_c = jnp.sum(jnp.where(head_row, pc, 0.0), axis=0, keepdims=True)
    imp_s = imp_c + pltpu.roll(imp_c, R - 1, 1)
    cand = (lane % SUBLANES == 0) | (lane % SUBLANES == 2)
    jsel = 2 * (lane // SUBLANES) + (lane % SUBLANES) // 2
    score = jnp.where(jsel == jt, 2.0 * SEL_FORCE,
                      jnp.where((jsel == 0) | (jsel == jt - 1), SEL_FORCE,
                                jnp.where(jsel <= jt, imp_s + 0.0, -1.0)))
    score_b = jnp.broadcast_to(score, (R, R))
    key_row = _order_key(score_b)
    key_col = _order_key(score_b.T)
    l0 = lax.broadcasted_iota(I32, (R, R), 0)
    l1 = lax.broadcasted_iota(I32, (R, R), 1)
    cand_col = (l0 % SUBLANES == 0) | (l0 % SUBLANES == 2)
    beats = cand_col & (key_col > jnp.where(l0 < l1, key_row - 1, key_row))
    cnt = jnp.sum(beats.astype(I32), axis=0, keepdims=True)
    sel_row = (cand & (cnt < TOP_N - 1)).astype(F32)
    sel_col = jnp.broadcast_to(sel_row, (R, R)).T
    half = ((l0 % SUBLANES == 0) & (l1 < SEL_LEN)) | ((l0 % SUBLANES == 2) & (l1 >= SEL_LEN))
    z = jnp.where(half, sel_col, 0.0)
    same_page = (l1 // SUBLANES == l0 // SUBLANES).astype(BF16)
    picked = jnp.dot(same_page, z.astype(BF16), preferred_element_type=F32)
    picked = picked.reshape(n_pages, SUBLANES, PG) > 0.5

    for p in range(n_pages):
        ks_t = pg_refs[p][2]
        for h in range(N_HEADS):
            g = p * N_HEADS + h
            part_sc[g * SUBLANES:(g + 1) * SUBLANES, :] = _sublane_group_sum(
                ks_t * qr_cols[h * HEAD_DIM:(h + 1) * HEAD_DIM])
    s = _dot01(_fold_matrix(n_pages), part_sc[...])
    s3 = jnp.where(picked, s.reshape(n_pages, SUBLANES, PG) * SCALE, NEG)
    s_new = _per_head_col(_head_lane_sums(qr_row * rep4(rown[:, 2 * HEAD_DIM:3 * HEAD_DIM]))) * SCALE
    m = jnp.maximum(jnp.max(jnp.max(s3, axis=2, keepdims=True), axis=0), s_new)
    p3 = jnp.exp(s3 - m[None])
    pn = jnp.exp(s_new - m)
    inv = 1.0 / (jnp.sum(jnp.sum(p3, axis=2, keepdims=True), axis=0) + pn)
    o_sel = []
    for h in range(N_HEADS):
        acc = jnp.zeros((HEAD_DIM, PG), F32)
        for p in range(n_pages):
            acc = acc + pg_refs[p][3] * p3[p, h:h + 1, :]
        o_sel.append(jnp.sum(acc, axis=1, keepdims=True) * inv[h:h + 1])

    kw_t = win_ref[0]
    vw_t = win_ref[1]
    wlane = lax.broadcasted_iota(I32, (1, WB), 1)
    wpos = past_len - WB + wlane
    wd = past_len - wpos
    wok = (wd >= 0) & (wd < WINDOW) & (wpos >= 0)
    sw_sc[...] = jnp.zeros_like(sw_sc)
    for h in range(N_HEADS):
        qh = qr_cols[h * HEAD_DIM:(h + 1) * HEAD_DIM]
        sw_sc[h:h + 1, :] = jnp.sum(kw_t * jnp.concatenate([qh] * (WB // LANES), axis=1), axis=0, keepdims=True)
    sw = jnp.where(wok, sw_sc[...] * SCALE, NEG)
    sw_new = _per_head_col(_head_lane_sums(qr_row * rep4(winn[:, 0:HEAD_DIM]))) * SCALE
    mw = jnp.maximum(jnp.max(sw, axis=-1, keepdims=True), sw_new)
    ew = jnp.exp(sw - mw)
    en = jnp.exp(sw_new - mw)
    invw = 1.0 / (jnp.sum(ew, axis=-1, keepdims=True) + en)

    gate = lambda h, c: smn[:, SM_NG + 3 * h + c:SM_NG + 3 * h + c + 1]
    o_cols = []
    for h in range(N_HEADS):
        o_win = jnp.sum(vw_t * ew[h:h + 1, :], axis=1, keepdims=True) * invw[h:h + 1]
        o_cols.append(gate(h, 0) * o_cmp[h] + gate(h, 1) * o_sel[h] + gate(h, 2) * o_win)
    w_sel = _per_head_row([gate(h, 1) * pn[h:h + 1] * inv[h:h + 1] for h in range(N_HEADS)], D_NSA)
    w_win = _per_head_row([gate(h, 2) * en[h:h + 1] * invw[h:h + 1] for h in range(N_HEADS)], D_NSA)
    o_ref[row, :] = (_row_of(jnp.concatenate(o_cols, axis=0))
                     + w_sel * rep4(rown[:, 3 * HEAD_DIM:4 * HEAD_DIM])
                     + w_win * rep4(winn[:, HEAD_DIM:2 * HEAD_DIM]))
    last = lax.broadcasted_iota(I32, (HEAD_DIM, WB), 1) == WB - 1
    winn_cols = _cols_of(winn)
    for s in range(2):
        new_col = winn_cols[s * HEAD_DIM:(s + 1) * HEAD_DIM, 0:1]
        nw_ref[s] = jnp.where(last, new_col, pltpu.roll(win_ref[s], WB - 1, 1))


def _nsa_decode(l, pt_flat, n_pages, past_len, nq, nqr, rows, win, small, nsa_cache_t, cmp_pool, win_state_t,
                prev):
    DB = nq.shape[0]
    PG = nsa_cache_t.shape[-1]
    WB = win_state_t.shape[-1]
    page = lambda p: (lambda b, pt: (l, pt[b * n_pages + p], 0, 0, 0))
    cpage = lambda p: (lambda b, pt: (pt[b * n_pages + p], 0, 0))
    carried = [] if prev is None else [prev]
    rows8 = lambda w: pl.BlockSpec((SUBLANES, w), lambda b, pt: (b // SUBLANES, 0))
    return pl.pallas_call(
        functools.partial(_nsa_decode_kernel, n_pages=n_pages, past_len=past_len),
        grid_spec=pltpu.PrefetchScalarGridSpec(
            num_scalar_prefetch=1,
            grid=(DB,),
            in_specs=[rows8(D_NSA), rows8(D_NSA), rows8(4 * HEAD_DIM), rows8(2 * HEAD_DIM), rows8(LANES),
                      pl.BlockSpec((None, None, 2, HEAD_DIM, WB), lambda b, pt: (l, b, 0, 0, 0))]
            + [pl.BlockSpec((None, None, 4, HEAD_DIM, PG), page(p)) for p in range(n_pages)]
            + [pl.BlockSpec((None, PG // CMP_LEN, LANES), cpage(p)) for p in range(n_pages)]
            + [pl.BlockSpec(memory_space=pl.ANY)] * len(carried),
            out_specs=[rows8(D_NSA),
                       pl.BlockSpec((None, None, 2, HEAD_DIM, WB), lambda b, pt: (l, b, 0, 0, 0))],
            scratch_shapes=[pltpu.VMEM((n_pages * SUBLANES, LANES), F32),
                            pltpu.VMEM((SUBLANES, LANES), F32),
                            pltpu.VMEM((n_pages * N_HEADS * SUBLANES, PG), F32),
                            pltpu.VMEM((SUBLANES, WB), F32)],
        ),
        out_shape=[jax.ShapeDtypeStruct((DB, D_NSA), F32),
                   jax.ShapeDtypeStruct(win_state_t.shape, F32)],
        input_output_aliases={7 + 2 * n_pages: 1} if carried else {},
        compiler_params=_params(("arbitrary",)),
        name="nsa_decode",
    )(pt_flat, nq, nqr, rows, win, small, win_state_t,
      *([nsa_cache_t] * n_pages), *([cmp_pool] * n_pages), *carried)


def _gla_decode_kernel(qk_ref, g_ref, v_ref, s_ref, *rest):
    o_ref, so_ref = rest[-2:]
    for j in range(SUBLANES):
        qk = qk_ref[j:j + 1, :]
        q_cols = _cols_of(qk[:, 0:D_FOX] * SCALE)
        k_cols = _cols_of(qk[:, D_FOX:2 * D_FOX])
        decay = jnp.exp(_cols_of(g_ref[j:j + 1, :]))
        for h in range(N_HEADS):
            hs = slice(h * HEAD_DIM, (h + 1) * HEAD_DIM)
            v_row = v_ref[j:j + 1, h * GLA_DV:(h + 1) * GLA_DV]
            s_new = decay[hs] * s_ref[j, h] + k_cols[hs] * v_row
            so_ref[j, h] = s_new
            o_ref[j:j + 1, h * GLA_DV:(h + 1) * GLA_DV] = jnp.sum(q_cols[hs] * s_new, axis=0, keepdims=True)


def _gla_decode(l, gqk, gv, glog, state, prev):
    depth, DB = state.shape[0:2]
    nb = SUBLANES
    rows = lambda w: pl.BlockSpec((nb, w), lambda i: (i, 0))
    sspec = pl.BlockSpec((None, nb, N_HEADS, HEAD_DIM, GLA_DV), lambda i: (l, i, 0, 0, 0))
    carried = [] if prev is None else [prev]
    return pl.pallas_call(
        _gla_decode_kernel,
        grid=(DB // nb,),
        in_specs=[rows(2 * D_FOX), rows(D_FOX), rows(D_GLA), sspec]
        + [pl.BlockSpec(memory_space=pl.ANY)] * len(carried),
        out_specs=[rows(D_GLA), sspec],
        out_shape=[jax.ShapeDtypeStruct((DB, D_GLA), F32), jax.ShapeDtypeStruct(state.shape, F32)],
        input_output_aliases={4: 1} if carried else {},
        compiler_params=_params(("parallel",)),
        name="gla_decode",
    )(gqk, glog, gv, state, *carried)


def _mix_sample(x, l, p, tab, pt_flat, n_pages, past_len, fox_kv_t, fox_lft_c, nsa_t, cmp_blocks,
                win_state_t, gla_state, carry):
    DB = x.shape[0]
    (fq, fkv, nq, nqr, rows, win, gqk, gv, gog, glog, small) = _in_projection(
        x, 0, DB, DB, p['g'], p['w_r32'], p['sb'], p['wgk32'], p['bgk'], tab, DB)
    o_fox = _fox_decode(l, pt_flat, n_pages, fq, fkv, small, fox_kv_t, fox_lft_c)
    n_pool, PG = nsa_t.shape[1], nsa_t.shape[-1]
    per_layer = n_pool * (PG // CMP_LEN)
    cmp_pool = _compress(cmp_blocks, p['pe_cmp'], p['w_cmp'], Tiles.compress_rows, False,
                         row0=l * per_layer, n_rows=per_layer)
    cmp_pool = cmp_pool.reshape(n_pool, PG // CMP_LEN, LANES)
    o_nsa, new_win = _nsa_decode(l, pt_flat, n_pages, past_len, nq, nqr, rows, win, small,
                                 nsa_t, cmp_pool, win_state_t, carry.get('win'))
    o_gla, g_state = _gla_decode(l, gqk, gv, glog, gla_state, carry.get('gla'))
    carry.update(win=new_win, gla=g_state)
    x_new = _out_projection(x, o_fox, o_nsa, o_gla, gog, p['gn'], p['w_out32'], DB)
    return x_new, dict(fkv=fkv, small=small, rows=rows)


def kernel(x_prompt, x_sample, cache_fox_kv, cache_fox_logf, cache_nsa_kv, state_nsa_win, state_gla,
           page_table, norm_mix_g, w_in, b_fox_f, w_cmp, pe_cmp, w_gla_gk, b_gla_gk, g_gla_norm, w_out,
           norm_ffn_g, dense_w_gate, dense_w_up, dense_w_down, moe_w_router, moe_w_gate, moe_w_up,
           moe_w_down, final_norm_g):
    B, S, _ = x_prompt.shape
    DB, TN, _ = x_sample.shape
    assert TN == 1
    depth, n_pool, PG = cache_fox_kv.shape[0:3]
    n_pages = page_table.shape[1]
    past_len = n_pages * PG
    WB = state_nsa_win.shape[2]
    xp = x_prompt.reshape(B * S, D_MODEL)
    xs = x_sample.reshape(DB, D_MODEL)
    tab_p = _rope_table(jnp.arange(S))
    tab_s = _rope_table(jnp.full((DB,), past_len, I32))
    pt_flat = page_table.reshape(-1).astype(I32)
    fox_kv_t = jnp.transpose(cache_fox_kv, (0, 1, 3, 4, 5, 2))
    fox_lft_c = jnp.swapaxes(cache_fox_logf, 2, 3)
    nsa_t = jnp.transpose(cache_nsa_kv, (0, 1, 3, 4, 2))
    win_state_t = jnp.transpose(state_nsa_win, (0, 1, 3, 4, 2))
    n_cmp = depth * n_pool * (PG // CMP_LEN)
    cmp_blocks = _block_major(cache_nsa_kv[:, :, :, 0:2, :].reshape(n_cmp * CMP_LEN, 2, HEAD_DIM), n_cmp)
    gfin = final_norm_g.reshape(1, D_MODEL)
    cp, cs = [], []
    carry_p, carry_s = {}, {}
    for l in range(depth):
        p = _layer_mix_params(l, norm_mix_g, w_in, b_fox_f, w_cmp, pe_cmp, w_gla_gk, b_gla_gk,
                              g_gla_norm, w_out)
        gf = norm_ffn_g[l].reshape(1, D_MODEL)
        i = l // 2
        moe = l % 2 == 1
        xp, c = _mix_prompt(xp, B, S, l, depth, p, tab_p, Tiles.prompt_rows, Tiles.fox, Tiles.nsa_keys,
                            Tiles.gla_rows, carry_p)
        cp.append(c)
        if moe:
            wr = jnp.zeros((D_MODEL, LANES), F32).at[:, 0:N_EXPERTS].set(moe_w_router[i])
            experts = (moe_w_gate[i], moe_w_up[i], moe_w_down[i])
            disp_p = _moe_dispatch(xp, gf, wr, Tiles.prompt_rows, Tiles.expert_block)
        xs, c = _mix_sample(xs, l, p, tab_s, pt_flat, n_pages, past_len, fox_kv_t, fox_lft_c, nsa_t,
                            cmp_blocks, win_state_t, state_gla, carry_s)
        cs.append(c)
        if moe:
            disp_s = _moe_dispatch(xs, gf, wr, DB, LANES)
            fin = gfin if l == depth - 1 else None
            xp = _moe_finish(disp_p, *experts, Tiles.prompt_rows, Tiles.expert_block, Tiles.ffn_cols, fin)
            xs = _moe_finish(disp_s, *experts, DB, LANES, Tiles.ffn_cols_sample, fin)
        else:
            wg, wu, wd = (dense_w_gate[i].astype(BF16), dense_w_up[i].astype(BF16),
                          dense_w_down[i].astype(BF16))
            xp = _dense_ffn(xp, gf, wg, wu, wd, Tiles.dense_rows, Tiles.ffn_cols)
            xs = _dense_ffn(xs, gf, dense_w_gate[i], dense_w_up[i], dense_w_down[i], DB, Tiles.ffn_cols_sample)
    if depth % 2 == 1:
        xp = _final_norm(xp, gfin, Tiles.prompt_rows)
        xs = _final_norm(xs, gfin, DB)
    y_p = xp.reshape(B, S, D_MODEL)
    y_s = xs.reshape(DB, 1, D_MODEL)
    wp = min(WINDOW, S)
    st = lambda key, group: jnp.stack([c[key] for c in group])
    return (y_p, y_s,
            carry_p['fkv'].reshape(depth, B, 2, N_HEADS, HEAD_DIM, S).transpose(0, 1, 5, 2, 3, 4),
            st('small', cp).transpose(0, 1, 3, 2),
            carry_p['rows'].reshape(depth, B, 4, HEAD_DIM, S).transpose(0, 1, 4, 2, 3),
            st('win', cp).reshape(depth, B, 2, HEAD_DIM, wp).transpose(0, 1, 4, 2, 3),
            st('g_state', cp),
            st('fkv', cs).reshape(depth, DB, 1, 2, N_HEADS, HEAD_DIM),
            st('small', cs)[:, :, SM_FF:SM_FF + N_HEADS].reshape(depth, DB, 1, N_HEADS),
            st('rows', cs).reshape(depth, DB, 1, 4, HEAD_DIM),
            jnp.transpose(carry_s['win'], (0, 1, 4, 2, 3)),
            carry_s['gla'])
```

```python
import functools

import jax
import jax.numpy as jnp
from jax import lax
from jax.experimental import pallas as pl
from jax.experimental.pallas import tpu as pltpu
from jax.experimental.pallas import tpu_sc as plsc

F32 = jnp.float32
BF16 = jnp.bfloat16
I32 = jnp.int32
HI = lax.Precision.HIGHEST

D_MODEL = 1024
HEAD_DIM = 64
N_HEADS = 4
D_FOX = N_HEADS * HEAD_DIM
D_NSA = N_HEADS * HEAD_DIM
GLA_DV = 128
D_GLA = N_HEADS * GLA_DV
GLA_RANK = 16
GLA_TAU = 16.0
GLA_CHUNK = 64
CMP_LEN = 32
SEL_LEN = 64
TOP_N = 16
WINDOW = 512
ROPE_THETA = 500000.0
ROPE_DIM = HEAD_DIM // 4
ROPE_HALF = ROPE_DIM // 2
D_FF = 3584
N_EXPERTS = 8
EPS = 1e-6
SEL_FORCE = 1e9
NEG = -1e30
SCALE = HEAD_DIM ** -0.5
LOG2E = 1.4426950408889634

LANES = 128
SUBLANES = 8
VMEM_BYTES_V7X = 64 * 1024 * 1024
VMEM_LIMIT = VMEM_BYTES_V7X - 8 * 1024 * 1024


class Tiles:
    prompt_rows = 512
    fox = 512
    nsa_keys = 512
    gla_rows = 256
    cumsum = 512
    compress_rows = 256
    dense_rows = 1024
    ffn_cols = 512
    ffn_cols_sample = 896
    expert_block = 1024

C_FQ = 0
C_FKV = 256
C_NQ = 768
C_NKV = 1024
C_GQK = 1408
C_GV = 1920
C_GOG = 2432
C_SMALL = 2944
C_END = 3072
SM_FF = 0
SM_NG = 4
SM_GLR = 16

NT = (((1,), (1,)), ((), ()))


def _params(sem):
    return pltpu.CompilerParams(dimension_semantics=sem, vmem_limit_bytes=VMEM_LIMIT)


def _rms(x, g):
    ms = jnp.mean(x * x, axis=-1, keepdims=True)
    return x * lax.rsqrt(ms + EPS) * g


def _sigmoid(x):
    return 1.0 / (1.0 + jnp.exp(-x))


def _log_sigmoid(x):
    return -(jnp.maximum(-x, 0.0) + jnp.log1p(jnp.exp(-jnp.abs(x))))


def _silu(x):
    return x * _sigmoid(x)


def _bdot(a, b):
    return jnp.dot(a.astype(BF16), b.astype(BF16), preferred_element_type=F32)


def _split3(x):
    h = x.astype(BF16)
    r = x - h.astype(F32)
    m = r.astype(BF16)
    return h, m, (r - m.astype(F32)).astype(BF16)


def _dot01(m01, x):
    mb = m01.astype(BF16)
    h, m, l = _split3(x)
    return (jnp.dot(mb, h, preferred_element_type=F32) + jnp.dot(mb, m, preferred_element_type=F32)
            + jnp.dot(mb, l, preferred_element_type=F32))


def _dot01_r(x, m01):
    mb = m01.astype(BF16)
    h, m, l = _split3(x)
    return (jnp.dot(h, mb, preferred_element_type=F32) + jnp.dot(m, mb, preferred_element_type=F32)
            + jnp.dot(l, mb, preferred_element_type=F32))


def _dot01_nt(m01, x):
    mb = m01.astype(BF16)
    h, m, l = _split3(x)
    return (lax.dot_general(mb, h, NT, preferred_element_type=F32)
            + lax.dot_general(mb, m, NT, preferred_element_type=F32)
            + lax.dot_general(mb, l, NT, preferred_element_type=F32))


def _dot_nt_hilo(a, b):
    ah = a.astype(BF16)
    al = (a - ah.astype(F32)).astype(BF16)
    bh = b.astype(BF16)
    bl = (b - bh.astype(F32)).astype(BF16)
    return (lax.dot_general(ah, bh, NT, preferred_element_type=F32)
            + lax.dot_general(al, bh, NT, preferred_element_type=F32)
            + lax.dot_general(ah, bl, NT, preferred_element_type=F32))


def _wdot(a, w):
    if w.dtype == F32:
        return jnp.dot(a.astype(F32), w, precision=HI, preferred_element_type=F32)
    return jnp.dot(a.astype(BF16), w, preferred_element_type=F32)


def _rope128(x, a, bp, bm):
    return x * a + pltpu.roll(x, ROPE_HALF, 1) * bp + pltpu.roll(x, LANES - ROPE_HALF, 1) * bm


def _inproj_kernel(x_ref, g_ref, w_ref, sb_ref, wgk_ref, bgk_ref, tab_ref, *refs, feature_major, n_carried):
    (fq_ref, fkv_ref, nq_ref, nqr_ref, rows_ref, win_ref,
     gqk_ref, gv_ref, gog_ref, glog_ref, small_ref, *extra) = refs[n_carried:]
    h = _rms(x_ref[...], g_ref[...]).astype(w_ref.dtype)

    def put(ref, v):
        ref[...] = v.T if feature_major else v

    def mm(a, b):
        return _wdot(h, w_ref[:, a:b])

    fq_ref[...] = mm(C_FQ, C_FKV)
    put(fkv_ref, mm(C_FKV, C_NQ))
    tab = tab_ref[...]
    ab, pb, mb = tab[:, 0:128], tab[:, 128:256], tab[:, 256:384]
    af, pf, mf = tab[:, 384:512], tab[:, 512:640], tab[:, 640:768]
    nq = mm(C_NQ, C_NKV)
    nq_ref[...] = nq
    nqr_ref[:, 0:128] = _rope128(nq[:, 0:128], ab, pb, mb)
    nqr_ref[:, 128:256] = _rope128(nq[:, 128:256], ab, pb, mb)
    nkv = mm(C_NKV, C_GQK)
    put(rows_ref, jnp.concatenate([nkv[:, 0:128], _rope128(nkv[:, 128:256], af, pf, mf)], axis=1))
    put(win_ref, _rope128(nkv[:, 256:384], af, pf, mf))
    gqk_ref[...] = mm(C_GQK, C_GV)
    gv_ref[...] = mm(C_GV, C_GOG)
    gog_ref[...] = mm(C_GOG, C_SMALL)
    sm = mm(C_SMALL, C_END)
    glog_ref[...] = _log_sigmoid(_wdot(sm, wgk_ref[...]) + bgk_ref[...]) * (1.0 / GLA_TAU)
    smb = sm + sb_ref[...]
    lane = lax.broadcasted_iota(I32, smb.shape, 1)
    small = jnp.where(lane < SM_NG, _log_sigmoid(smb), _sigmoid(smb))
    small_ref[...] = small
    if feature_major:
        extra[0][...] = small.T[0:SUBLANES, :]


def _in_projection(x_all, row0, n_rows, tm, g, w_r, sb, wgk, bgk, tab, tab_period, batch=None, layer=None,
                   prev=None):
    assert n_rows % tm == 0 and row0 % tm == 0 and tab_period % tm == 0
    nt = n_rows // tm
    b0 = row0 // tm
    npd = tab_period // tm
    widths = (256, 512, 256, 256, 256, 128, 512, 512, 512, 256, 128)
    FKV, ROWS, WIN = 1, 4, 5
    full = lambda shape: pl.BlockSpec(shape, lambda i: (0, 0))
    row_spec = lambda w: pl.BlockSpec((tm, w), lambda i: (i, 0))
    carried = [] if prev is None else list(prev)
    if batch is not None:
        B, S, depth = batch
        per = S // tm
        assert n_rows == B * S and S % tm == 0
        t_spec = lambda w: pl.BlockSpec((None, w, tm), lambda i: (i // per, 0, i % per))
        l_spec = lambda w: pl.BlockSpec((None, None, w, tm), lambda i: (layer, i // per, 0, i % per))
        out_specs = [l_spec(w) if k in (FKV, ROWS) else t_spec(w) if k == WIN else row_spec(w)
                     for k, w in enumerate(widths)] + [t_spec(SUBLANES)]
        out_shape = [jax.ShapeDtypeStruct((depth, B, w, S) if k in (FKV, ROWS) else (B, w, S) if k == WIN
                                          else (n_rows, w), F32)
                     for k, w in enumerate(widths)] + [jax.ShapeDtypeStruct((B, SUBLANES, S), F32)]
    else:
        out_specs = [row_spec(w) for w in widths]
        out_shape = [jax.ShapeDtypeStruct((n_rows, w), F32) for w in widths]
    return pl.pallas_call(
        functools.partial(_inproj_kernel, feature_major=batch is not None, n_carried=len(carried)),
        grid=(nt,),
        in_specs=[
            pl.BlockSpec((tm, D_MODEL), lambda i: (b0 + i, 0)),
            full((1, D_MODEL)),
            full((D_MODEL, C_END)),
            full((1, LANES)),
            full((LANES, 256)),
            full((1, 256)),
            pl.BlockSpec((tm, 768), lambda i: (i % npd, 0)),
        ] + [pl.BlockSpec(memory_space=pl.ANY)] * len(carried),
        out_specs=out_specs,
        out_shape=out_shape,
        input_output_aliases={7: FKV, 8: ROWS} if carried else {},
        compiler_params=_params(("parallel",)),
        name="in_projection",
    )(x_all, g, w_r, sb, wgk, bgk, tab, *carried)


def _cumsum_kernel(sm_ref, cr_ref, carry):
    t = pl.program_id(1)
    ts = sm_ref.shape[1]

    @pl.when(t == 0)
    def _():
        carry[...] = jnp.zeros_like(carry)

    r = lax.broadcasted_iota(I32, (ts, ts), 0)
    c = lax.broadcasted_iota(I32, (ts, ts), 1)
    cs = _dot01_r(sm_ref[...], r <= c) + carry[...]
    carry[...] = cs[:, ts - 1:ts]
    cr_ref[...] = cs * LOG2E


def _fox_cumsum(small_t, ts):
    B, _, S = small_t.shape
    spec = pl.BlockSpec((None, SUBLANES, ts), lambda b, t: (b, 0, t))
    return pl.pallas_call(
        _cumsum_kernel,
        grid=(B, S // ts),
        in_specs=[spec],
        out_specs=spec,
        out_shape=jax.ShapeDtypeStruct((B, SUBLANES, S), F32),
        scratch_shapes=[pltpu.VMEM((SUBLANES, 1), F32)],
        compiler_params=_params(("parallel", "arbitrary")),
        name="fox_cumsum",
    )(small_t)


def _pair_mask(shape, h, axis=1):
    return (lax.broadcasted_iota(I32, shape, axis) // HEAD_DIM) == (h % 2)


def _fox_prompt_kernel(q_ref, kv_ref, cr_ref, o_ref, *scratch):
    i = pl.program_id(1)
    j = pl.program_id(2)
    nk = pl.num_programs(2)
    tq = q_ref.shape[0]
    tk = kv_ref.shape[1]
    q_sc, m_sc, acc_sc = scratch[0:N_HEADS], scratch[N_HEADS:2 * N_HEADS], scratch[2 * N_HEADS:]

    @pl.when(j == 0)
    def _():
        for h in range(N_HEADS):
            m_sc[h][...] = jnp.full_like(m_sc[h], NEG)
            acc_sc[h][...] = jnp.zeros_like(acc_sc[h])
            slab = q_ref[:, (h // 2) * LANES:(h // 2 + 1) * LANES] * (SCALE * LOG2E)
            q_sc[h][...] = jnp.where(_pair_mask(slab.shape, h), slab, 0.0).astype(BF16)

    def tile(diagonal):
        k_slabs = [kv_ref[c0:c0 + LANES, :].astype(BF16) for c0 in (0, LANES)]
        if diagonal:
            mask = lax.broadcasted_iota(I32, (1, tk), 1) <= lax.broadcasted_iota(I32, (tq, 1), 0)
        for h in range(N_HEADS):
            v_slab = kv_ref[D_FOX + (h // 2) * LANES:D_FOX + (h // 2 + 1) * LANES, :]
            v_aug = jnp.where(_pair_mask(v_slab.shape, h, 0), v_slab, 1.0).astype(BF16)
            s = jnp.dot(q_sc[h][...], k_slabs[h // 2], preferred_element_type=F32) - cr_ref[h:h + 1, :]
            if diagonal:
                s = jnp.where(mask, s, NEG)
            m_old = m_sc[h][...]
            m_new = jnp.maximum(m_old, jnp.max(s, axis=-1, keepdims=True))
            p = jnp.exp2(s - m_new).astype(BF16)
            acc_sc[h][...] = (jnp.exp2(m_old - m_new) * acc_sc[h][...]
                              + lax.dot_general(p, v_aug, NT, preferred_element_type=F32))
            m_sc[h][...] = m_new

    @pl.when(j < i)
    def _():
        tile(False)

    @pl.when(j == i)
    def _():
        tile(True)

    @pl.when(j == nk - 1)
    def _():
        for h in range(N_HEADS):
            a = acc_sc[h][...]
            lo = (h % 2) * HEAD_DIM
            den = a[:, HEAD_DIM - lo:HEAD_DIM - lo + 1]
            o_ref[:, h * HEAD_DIM:(h + 1) * HEAD_DIM] = a[:, lo:lo + HEAD_DIM] / den


def _fox_prompt(fq, fkv_all, l, cr, B, S, t):
    n = S // t
    return pl.pallas_call(
        _fox_prompt_kernel,
        grid=(B, n, n),
        in_specs=[
            pl.BlockSpec((t, D_FOX), lambda b, i, j: (b * n + i, 0)),
            pl.BlockSpec((None, None, 2 * D_FOX, t), lambda b, i, j: (l, b, 0, jnp.minimum(i, j))),
            pl.BlockSpec((None, SUBLANES, t), lambda b, i, j: (b, 0, jnp.minimum(i, j))),
        ],
        out_specs=pl.BlockSpec((t, D_FOX), lambda b, i, j: (b * n + i, 0)),
        out_shape=jax.ShapeDtypeStruct((B * S, D_FOX), F32),
        scratch_shapes=([pltpu.VMEM((t, LANES), BF16)] * N_HEADS + [pltpu.VMEM((t, 1), F32)] * N_HEADS
                        + [pltpu.VMEM((t, LANES), F32)] * N_HEADS),
        compiler_params=_params(("parallel", "parallel", "arbitrary")),
        name="fox_prompt",
    )(fq, fkv_all, cr)


def _compress_kernel(x_ref, pe_ref, w_ref, o_ref, *, exact):
    for s in range(2):
        x = x_ref[s] + pe_ref[s]
        w = w_ref[s]
        if exact:
            y = jnp.dot(x, w, precision=HI, preferred_element_type=F32)
        else:
            xh = x.astype(BF16)
            xl = (x - xh.astype(F32)).astype(BF16)
            wh = w.astype(BF16)
            wl = (w - wh.astype(F32)).astype(BF16)
            y = (jnp.dot(xh, wh, preferred_element_type=F32) + jnp.dot(xl, wh, preferred_element_type=F32)
                 + jnp.dot(xh, wl, preferred_element_type=F32))
        o_ref[:, s * HEAD_DIM:(s + 1) * HEAD_DIM] = y


def _compress(x3, pe, w, tr, exact, row0=0, n_rows=None):
    K = x3.shape[2]
    R = x3.shape[1] if n_rows is None else n_rows
    assert R % tr == 0 and row0 % tr == 0
    b0 = row0 // tr
    return pl.pallas_call(
        functools.partial(_compress_kernel, exact=exact),
        grid=(R // tr,),
        in_specs=[pl.BlockSpec((2, tr, K), lambda i: (0, b0 + i, 0)),
                  pl.BlockSpec((2, 1, K), lambda i: (0, 0, 0)),
                  pl.BlockSpec((2, K, HEAD_DIM), lambda i: (0, 0, 0))],
        out_specs=pl.BlockSpec((tr, LANES), lambda i: (i, 0)),
        out_shape=jax.ShapeDtypeStruct((R, LANES), F32),
        compiler_params=_params(("parallel",)),
        name="nsa_compress",
    )(x3, pe, w)


def _block_major(kv, n_blocks):
    return kv.reshape(n_blocks, CMP_LEN, 2, HEAD_DIM).transpose(2, 0, 1, 3).reshape(
        2, n_blocks, CMP_LEN * HEAD_DIM)


def _order_key(x):
    b = lax.bitcast_convert_type(x, I32)
    return jnp.where(b < 0, b ^ jnp.int32(0x7FFFFFFF), b)


def _nsa_prompt_kernel(nq_ref, nqr_ref, sm_ref, cmp_ref, rows_ref, win_ref, o_ref, qc_sc, qx_sc, *, tk):
    QB = nq_ref.shape[0]
    S = rows_ref.shape[1]
    nb = cmp_ref.shape[0]
    nsel = S // SEL_LEN
    i = pl.program_id(1)
    qs = i * QB
    qpos = qs + lax.broadcasted_iota(I32, (QB, 1), 0)

    HQ = N_HEADS * QB
    lo_half = lax.broadcasted_iota(I32, (QB, LANES), 1) < HEAD_DIM

    def stack_heads(ref, scale, dst):
        for h in range(N_HEADS):
            slab = ref[:, (h // 2) * LANES:(h // 2 + 1) * LANES] * scale
            if h % 2:
                slab = pltpu.roll(slab, HEAD_DIM, 1)
            dst[h * QB:(h + 1) * QB, :] = jnp.where(lo_half, slab, 0.0).astype(dst.dtype)

    cmp = cmp_ref[...]
    n_l = lax.broadcasted_iota(I32, (1, nb), 1)
    complete = ((n_l + 1) * CMP_LEN - 1) <= qpos
    stack_heads(nq_ref, SCALE, qc_sc)
    s = _dot_nt_hilo(qc_sc[...], cmp).reshape(N_HEADS, QB, nb)
    s = jnp.where(complete[None], s, NEG)
    e = jnp.exp(s - jnp.max(s, axis=-1, keepdims=True))
    p = e / jnp.sum(e, axis=-1, keepdims=True) * complete.astype(F32)[None]
    o_cmp = _bdot(p.reshape(HQ, nb), cmp)
    psum = jnp.sum(p, axis=0)

    pj = lax.broadcasted_iota(I32, (nsel, nb), 0)
    pn = lax.broadcasted_iota(I32, (nsel, nb), 1)
    imp_t = _dot01_nt(pn // (SEL_LEN // CMP_LEN) == pj, psum)
    jt = (qs + lax.broadcasted_iota(I32, (1, QB), 1)) // SEL_LEN
    jj = lax.broadcasted_iota(I32, (nsel, 1), 0)
    score = jnp.where(jj == jt, 2.0 * SEL_FORCE,
                      jnp.where((jj == 0) | (jj == jt - 1), SEL_FORCE,
                                jnp.where(jj <= jt, imp_t + 0.0, -1.0)))
    key = _order_key(score)
    key_m1 = key - 1
    ngrp = nsel // SUBLANES
    sub = lax.broadcasted_iota(I32, (SUBLANES, QB), 0)
    kg = [key[r * SUBLANES:(r + 1) * SUBLANES, :] for r in range(ngrp)]
    kg1 = [key_m1[r * SUBLANES:(r + 1) * SUBLANES, :] for r in range(ngrp)]
    cnt = [jnp.zeros((SUBLANES, QB), I32) for _ in range(ngrp)]
    for jp in range(nsel):
        g = jp // SUBLANES
        row = key[jp:jp + 1, :]
        mixed = jnp.where(sub > (jp % SUBLANES), kg1[g], kg[g])
        for r in range(ngrp):
            thr = kg[r] if r < g else (kg1[r] if r > g else mixed)
            cnt[r] = cnt[r] + (row > thr).astype(I32)
    sel_t = jnp.concatenate([(c < TOP_N).astype(F32) for c in cnt], axis=0)
    if nsel < QB:
        sel_t = jnp.concatenate([sel_t, jnp.zeros((QB - nsel, QB), F32)], axis=0)
    sel = sel_t.T.astype(BF16)

    stack_heads(nqr_ref, SCALE * LOG2E, qx_sc)
    qx = qx_sc[...]

    def attend(valid, slab_t, m_old, acc_old):
        n = slab_t.shape[1]
        s = jnp.dot(qx, slab_t.astype(BF16), preferred_element_type=F32)
        s = jnp.where(valid[None], s.reshape(N_HEADS, QB, n), NEG).reshape(HQ, n)
        m_new = jnp.maximum(m_old, jnp.max(s, axis=-1, keepdims=True))
        p = jnp.exp2(s - m_new).astype(BF16)
        ones_k = lax.broadcasted_iota(I32, slab_t.shape, 0) < HEAD_DIM
        v_aug = jnp.where(ones_k, 1.0, slab_t).astype(BF16)
        acc = jnp.exp2(m_old - m_new) * acc_old + lax.dot_general(p, v_aug, NT, preferred_element_type=F32)
        return m_new, acc

    jrow = lax.broadcasted_iota(I32, (QB, 1), 0)

    def sel_tile(k0, m_old, acc_old, diagonal):
        kpos = k0 + lax.broadcasted_iota(I32, (1, tk), 1)
        expand = (jrow == kpos // SEL_LEN).astype(BF16)
        valid = jnp.dot(sel, expand, preferred_element_type=F32) > 0.5
        if diagonal:
            valid = valid & (kpos <= qpos)
        return attend(valid, rows_ref[2 * HEAD_DIM:4 * HEAD_DIM, pl.ds(k0, tk)], m_old, acc_old)

    n_full = qs // tk
    init = (jnp.full((HQ, 1), NEG, F32), jnp.zeros((HQ, LANES), F32))
    m_s, acc_s = lax.fori_loop(
        0, n_full, lambda t, c: sel_tile(pl.multiple_of(t * tk, tk), c[0], c[1], False), init)
    _, acc_s = sel_tile(pl.multiple_of(n_full * tk, tk), m_s, acc_s, True)

    wlen = WINDOW + QB
    w0 = pl.multiple_of(jnp.maximum(qs - WINDOW, 0), QB)
    wpos = w0 + lax.broadcasted_iota(I32, (1, wlen), 1)
    d = qpos - wpos
    _, acc_w = attend((d >= 0) & (d < WINDOW), win_ref[:, pl.ds(w0, wlen)],
                      jnp.full((HQ, 1), NEG, F32), jnp.zeros((HQ, LANES), F32))

    sm = sm_ref[...]
    for h in range(N_HEADS):
        rs = slice(h * QB, (h + 1) * QB)
        o_sel = acc_s[rs] * (1.0 / acc_s[rs, 0:1])
        o_win = acc_w[rs] * (1.0 / acc_w[rs, 0:1])
        c = SM_NG + 3 * h
        mix = sm[:, c:c + 1] * o_cmp[rs] + sm[:, c + 1:c + 2] * o_sel + sm[:, c + 2:c + 3] * o_win
        if h % 2 == 0:
            mix = pltpu.roll(mix, HEAD_DIM, 1)
        lo = (h % 2) * HEAD_DIM
        o_ref[:, h * HEAD_DIM:(h + 1) * HEAD_DIM] = mix[:, lo:lo + HEAD_DIM]


def _nsa_prompt(nq, nqr, small, cmp, rows_all, l, win, B, S, tk):
    QB = 128
    nq_t = S // QB
    nb = S // CMP_LEN
    assert S % tk == 0 and S >= WINDOW + QB
    return pl.pallas_call(
        functools.partial(_nsa_prompt_kernel, tk=tk),
        grid=(B, nq_t),
        in_specs=[
            pl.BlockSpec((QB, D_NSA), lambda b, i: (b * nq_t + i, 0)),
            pl.BlockSpec((QB, D_NSA), lambda b, i: (b * nq_t + i, 0)),
            pl.BlockSpec((QB, LANES), lambda b, i: (b * nq_t + i, 0)),
            pl.BlockSpec((nb, LANES), lambda b, i: (b, 0)),
            pl.BlockSpec((None, None, 4 * HEAD_DIM, S), lambda b, i: (l, b, 0, 0)),
            pl.BlockSpec((None, 2 * HEAD_DIM, S), lambda b, i: (b, 0, 0)),
        ],
        out_specs=pl.BlockSpec((QB, D_NSA), lambda b, i: (b * nq_t + i, 0)),
        out_shape=jax.ShapeDtypeStruct((B * S, D_NSA), F32),
        scratch_shapes=[pltpu.VMEM((N_HEADS * QB, LANES), F32), pltpu.VMEM((N_HEADS * QB, LANES), BF16)],
        compiler_params=_params(("parallel", "parallel")),
        name="nsa_prompt",
    )(nq, nqr, small, cmp, rows_all, win)


def _gla_prompt_kernel(qk_ref, v_ref, g_ref, o_ref, st_ref, s_sc):
    t = pl.program_id(1)
    nt = pl.num_programs(1)
    tc = qk_ref.shape[0]
    C = GLA_CHUNK

    @pl.when(t == 0)
    def _():
        s_sc[...] = jnp.zeros_like(s_sc)

    r = lax.broadcasted_iota(I32, (tc, tc), 0)
    c = lax.broadcasted_iota(I32, (tc, tc), 1)
    same = (r // C) == (c // C)
    causal = same & (c <= r)
    g = g_ref[...]
    gcum = _dot01(causal, g)
    g_t = g.T
    gcum_t = _dot01_r(g_t, same & (r <= c))
    gtot_t = _dot01_r(g_t, same)
    q_e = (qk_ref[:, 0:D_FOX] * SCALE * jnp.exp(gcum)).astype(BF16)
    k_e = (qk_ref[:, D_FOX:2 * D_FOX] * jnp.exp(-gcum)).astype(BF16)
    kd_t = (qk_ref[:, D_FOX:2 * D_FOX].T * jnp.exp(gtot_t - gcum_t)).astype(BF16)
    decay_t = jnp.exp(gtot_t)
    for h in range(N_HEADS):
        hs = slice(h * HEAD_DIM, (h + 1) * HEAD_DIM)
        v = v_ref[:, h * GLA_DV:(h + 1) * GLA_DV].astype(BF16)
        a = jnp.where(causal, lax.dot_general(q_e[:, hs], k_e[:, hs], NT, preferred_element_type=F32), 0.0)
        o_intra = jnp.dot(a.astype(BF16), v, preferred_element_type=F32)
        state = s_sc[h]
        for ci in range(tc // C):
            rs = slice(ci * C, (ci + 1) * C)
            o_ref[rs, h * GLA_DV:(h + 1) * GLA_DV] = (
                o_intra[rs] + jnp.dot(q_e[rs, hs], state.astype(BF16), preferred_element_type=F32))
            state = (decay_t[hs, ci * C:ci * C + 1] * state
                     + jnp.dot(kd_t[hs, rs], v[rs], preferred_element_type=F32))
        s_sc[h] = state

    @pl.when(t == nt - 1)
    def _():
        st_ref[...] = s_sc[...]


def _gla_prompt(gqk, gv, glog, B, S, tc):
    nt = S // tc
    return pl.pallas_call(
        _gla_prompt_kernel,
        grid=(B, nt),
        in_specs=[pl.BlockSpec((tc, 2 * D_FOX), lambda b, t: (b * nt + t, 0)),
                  pl.BlockSpec((tc, D_GLA), lambda b, t: (b * nt + t, 0)),
                  pl.BlockSpec((tc, D_FOX), lambda b, t: (b * nt + t, 0))],
        out_specs=[pl.BlockSpec((tc, D_GLA), lambda b, t: (b * nt + t, 0)),
                   pl.BlockSpec((None, N_HEADS, HEAD_DIM, GLA_DV), lambda b, t: (b, 0, 0, 0))],
        out_shape=[jax.ShapeDtypeStruct((B * S, D_GLA), F32),
                   jax.ShapeDtypeStruct((B, N_HEADS, HEAD_DIM, GLA_DV), F32)],
        scratch_shapes=[pltpu.VMEM((N_HEADS, HEAD_DIM, GLA_DV), F32)],
        compiler_params=_params(("parallel", "arbitrary")),
        name="gla_prompt",
    )(gqk, gv, glog)


def _outproj_kernel(x_ref, of_ref, on_ref, og_ref, gog_ref, gn_ref, w_ref, o_ref):
    acc = _wdot(of_ref[...], w_ref[0:D_FOX, :])
    acc = acc + _wdot(on_ref[...], w_ref[D_FOX:D_FOX + D_NSA, :])
    for h in range(N_HEADS):
        hs = slice(h * GLA_DV, (h + 1) * GLA_DV)
        z = _rms(og_ref[:, hs], gn_ref[...]) * _silu(gog_ref[:, hs])
        w0 = D_FOX + D_NSA + h * GLA_DV
        acc = acc + _wdot(z, w_ref[w0:w0 + GLA_DV, :])
    o_ref[...] = x_ref[...] + acc


def _out_projection(x, o_fox, o_nsa, o_gla, gog, gn, w_out, tm):
    T = x.shape[0]
    assert T % tm == 0
    row = lambda w: pl.BlockSpec((tm, w), lambda i: (i, 0))
    return pl.pallas_call(
        _outproj_kernel,
        grid=(T // tm,),
        in_specs=[row(D_MODEL), row(D_FOX), row(D_NSA), row(D_GLA), row(D_GLA),
                  pl.BlockSpec((1, GLA_DV), lambda i: (0, 0)),
                  pl.BlockSpec((D_MODEL, D_MODEL), lambda i: (0, 0))],
        out_specs=row(D_MODEL),
        out_shape=jax.ShapeDtypeStruct((T, D_MODEL), F32),
        compiler_params=_params(("parallel",)),
        name="out_projection",
    )(x, o_fox, o_nsa, o_gla, gog, gn, w_out)


def _dense_ffn_kernel(x_ref, g_ref, wg_ref, wu_ref, wd_ref, o_ref, h_sc, acc_sc):
    f = pl.program_id(1)
    nf = pl.num_programs(1)

    @pl.when(f == 0)
    def _():
        h_sc[...] = _rms(x_ref[...], g_ref[...]).astype(h_sc.dtype)
        acc_sc[...] = jnp.zeros_like(acc_sc)

    h = h_sc[...]
    a = _wdot(h, wg_ref[...])
    u = _wdot(h, wu_ref[...])
    acc_sc[...] += _wdot(_silu(a) * u, wd_ref[...])

    @pl.when(f == nf - 1)
    def _():
        o_ref[...] = x_ref[...] + acc_sc[...]


def _dense_ffn(x, g, wg, wu, wd, tm, tf):
    T = x.shape[0]
    assert T % tm == 0 and D_FF % tf == 0
    return pl.pallas_call(
        _dense_ffn_kernel,
        grid=(T // tm, D_FF // tf),
        in_specs=[pl.BlockSpec((tm, D_MODEL), lambda i, f: (i, 0)),
                  pl.BlockSpec((1, D_MODEL), lambda i, f: (0, 0)),
                  pl.BlockSpec((D_MODEL, tf), lambda i, f: (0, f)),
                  pl.BlockSpec((D_MODEL, tf), lambda i, f: (0, f)),
                  pl.BlockSpec((tf, D_MODEL), lambda i, f: (f, 0))],
        out_specs=pl.BlockSpec((tm, D_MODEL), lambda i, f: (i, 0)),
        out_shape=jax.ShapeDtypeStruct((T, D_MODEL), F32),
        scratch_shapes=[pltpu.VMEM((tm, D_MODEL), wg.dtype), pltpu.VMEM((tm, D_MODEL), F32)],
        compiler_params=_params(("parallel", "arbitrary")),
        name="dense_ffn",
    )(x, g, wg, wu, wd)


def _router_kernel(x_ref, g_ref, wr_ref, h_ref, r_ref):
    h = _rms(x_ref[...], g_ref[...])
    h_ref[...] = h
    logits = jnp.dot(h, wr_ref[...], precision=HI, preferred_element_type=F32)
    lane = lax.broadcasted_iota(I32, logits.shape, 1)
    lg = jnp.where(lane < N_EXPERTS, logits, -jnp.inf)
    m1 = jnp.max(lg, axis=-1, keepdims=True)
    i1 = jnp.min(jnp.where(lg == m1, lane, LANES), axis=-1, keepdims=True)
    lg2 = jnp.where(lane == i1, -jnp.inf, lg)
    m2 = jnp.max(lg2, axis=-1, keepdims=True)
    i2 = jnp.min(jnp.where(lg2 == m2, lane, LANES), axis=-1, keepdims=True)
    e = jnp.exp(m2 - m1)
    den = 1.0 + e
    r_ref[...] = jnp.where(lane == 0, i1.astype(F32),
                           jnp.where(lane == 1, i2.astype(F32),
                                     jnp.where(lane == 2, 1.0 / den,
                                               jnp.where(lane == 3, e / den, 0.0))))


def _router(x, g, wr_pad, tm):
    T = x.shape[0]
    assert T % tm == 0
    return pl.pallas_call(
        _router_kernel,
        grid=(T // tm,),
        in_specs=[pl.BlockSpec((tm, D_MODEL), lambda i: (i, 0)),
                  pl.BlockSpec((1, D_MODEL), lambda i: (0, 0)),
                  pl.BlockSpec((D_MODEL, LANES), lambda i: (0, 0))],
        out_specs=[pl.BlockSpec((tm, D_MODEL), lambda i: (i, 0)),
                   pl.BlockSpec((tm, LANES), lambda i: (i, 0))],
        out_shape=[jax.ShapeDtypeStruct((T, D_MODEL), F32),
                   jax.ShapeDtypeStruct((T, LANES), F32)],
        compiler_params=_params(("parallel",)),
        name="moe_router",
    )(x, g, wr_pad)


GATHER_WINDOW = 32
SC_WORKERS = 32


def _row_gather(src, idx):
    n = idx.shape[0]
    step = GATHER_WINDOW * SC_WORKERS
    n_pad = -(-n // step) * step
    if n_pad != n:
        idx = jnp.concatenate([idx, jnp.zeros((n_pad - n,), idx.dtype)])
    width = src.shape[1]
    per_worker = n_pad // SC_WORKERS
    mesh = plsc.VectorSubcoreMesh(core_axis_name="core", subcore_axis_name="subcore")

    @functools.partial(pl.kernel, out_type=jax.ShapeDtypeStruct((n_pad, width), src.dtype), mesh=mesh,
                       scratch_types=[pltpu.VMEM((per_worker,), I32),
                                      pltpu.VMEM((GATHER_WINDOW, width), src.dtype)],
                       name="row_gather")
    def gather(src_hbm, idx_hbm, dst_hbm, idx_v, buf):
        worker = lax.axis_index("core") * (SC_WORKERS // 2) + lax.axis_index("subcore")
        base = worker * per_worker
        pltpu.sync_copy(idx_hbm.at[pl.ds(base, per_worker)], idx_v)

        @pl.loop(0, per_worker // GATHER_WINDOW)
        def _(j):
            pltpu.sync_copy(src_hbm.at[idx_v.at[pl.ds(j * GATHER_WINDOW, GATHER_WINDOW)]], buf)
            pltpu.sync_copy(buf, dst_hbm.at[pl.ds(base + j * GATHER_WINDOW, GATHER_WINDOW)])

    return gather(src, idx)


def _moe_ffn_kernel(be_ref, nu_ref, x_ref, wg_ref, wu_ref, wd_ref, o_ref, acc_sc, x_sc):
    b = pl.program_id(0)
    f = pl.program_id(1)
    nf = pl.num_programs(1)
    used = b < nu_ref[0]

    @pl.when(used)
    def _():
        @pl.when(f == 0)
        def _():
            acc_sc[...] = jnp.zeros_like(acc_sc)
            x_sc[...] = x_ref[...].astype(BF16)

        x = x_sc[...]
        a = jnp.dot(x, wg_ref[...].astype(BF16), preferred_element_type=F32)
        u = jnp.dot(x, wu_ref[...].astype(BF16), preferred_element_type=F32)
        acc_sc[...] += _bdot(_silu(a) * u, wd_ref[...])

        @pl.when(f == nf - 1)
        def _():
            o_ref[...] = acc_sc[...]

    @pl.when(jnp.logical_not(used) & (f == nf - 1))
    def _():
        o_ref[...] = jnp.zeros_like(o_ref)


def _moe_ffn(xb, block_e, n_used, wg, wu, wd, blk, tf):
    cap = xb.shape[0]
    nb = cap // blk
    nf = D_FF // tf

    def bsel(b, nu):
        return jnp.minimum(b, nu[0] - 1)

    def fsel(b, f, nu):
        return jnp.where(b < nu[0], f, nf - 1)

    return pl.pallas_call(
        _moe_ffn_kernel,
        grid_spec=pltpu.PrefetchScalarGridSpec(
            num_scalar_prefetch=2,
            grid=(nb, nf),
            in_specs=[
                pl.BlockSpec((blk, D_MODEL), lambda b, f, be, nu: (bsel(b, nu), 0)),
                pl.BlockSpec((None, D_MODEL, tf), lambda b, f, be, nu: (be[bsel(b, nu)], 0, fsel(b, f, nu))),
                pl.BlockSpec((None, D_MODEL, tf), lambda b, f, be, nu: (be[bsel(b, nu)], 0, fsel(b, f, nu))),
                pl.BlockSpec((None, tf, D_MODEL), lambda b, f, be, nu: (be[bsel(b, nu)], fsel(b, f, nu), 0)),
            ],
            out_specs=pl.BlockSpec((blk, D_MODEL), lambda b, f, be, nu: (b, 0)),
            scratch_shapes=[pltpu.VMEM((blk, D_MODEL), F32), pltpu.VMEM((blk, D_MODEL), BF16)],
        ),
        out_shape=jax.ShapeDtypeStruct((cap, D_MODEL), F32),
        compiler_params=_params(("arbitrary", "arbitrary")),
        name="moe_ffn",
    )(block_e, n_used, xb, wg, wu, wd)


def _moe_combine_kernel(x_ref, y1_ref, y2_ref, r_ref, g_ref, o_ref, *, final):
    r = r_ref[...]
    y = x_ref[...] + (r[:, 2:3] * y1_ref[...] + r[:, 3:4] * y2_ref[...])
    o_ref[...] = _rms(y, g_ref[...]) if final else y


def _moe_combine(x, y1, y2, route, tm, final_g):
    T = x.shape[0]
    g = jnp.ones((1, D_MODEL), F32) if final_g is None else final_g
    return pl.pallas_call(
        functools.partial(_moe_combine_kernel, final=final_g is not None),
        grid=(T // tm,),
        in_specs=[pl.BlockSpec((tm, D_MODEL), lambda i: (i, 0)),
                  pl.BlockSpec((tm, D_MODEL), lambda i: (i, 0)),
                  pl.BlockSpec((tm, D_MODEL), lambda i: (i, 0)),
                  pl.BlockSpec((tm, LANES), lambda i: (i, 0)),
                  pl.BlockSpec((1, D_MODEL), lambda i: (0, 0))],
        out_specs=pl.BlockSpec((tm, D_MODEL), lambda i: (i, 0)),
        out_shape=jax.ShapeDtypeStruct((T, D_MODEL), F32),
        compiler_params=_params(("parallel",)),
        name="moe_combine",
    )(x, y1, y2, route, g)


def _moe_plan(e_top, blk):
    T = e_top.shape[0]
    n = 2 * T
    flat_e = e_top.reshape(-1)
    onehot = (flat_e[:, None] == jnp.arange(N_EXPERTS, dtype=I32)[None, :]).astype(I32)
    csum = jnp.cumsum(onehot, axis=0)
    rank = jnp.sum((csum - onehot) * onehot, axis=1)
    counts = csum[-1]
    padded = (counts + blk - 1) // blk * blk
    ends = jnp.cumsum(padded)
    pstart = ends - padded
    dest = (pstart[flat_e] + rank).astype(I32)
    n_blocks = -(-n // blk) + N_EXPERTS
    cap = n_blocks * blk
    slot_tok = (jnp.arange(cap, dtype=I32) % T).at[dest].set(jnp.arange(n, dtype=I32) // 2)
    first = jnp.arange(n_blocks, dtype=I32) * blk
    block_e = jnp.minimum(jnp.sum((ends[None, :] <= first[:, None]).astype(I32), axis=1), N_EXPERTS - 1)
    n_used = (ends[-1] // blk).astype(I32).reshape(1)
    return dest, slot_tok, block_e, n_used


def _moe_dispatch(x, g, wr_pad, tm, blk):
    h, route = _router(x, g, wr_pad, tm)
    dest, slot_tok, block_e, n_used = _moe_plan(route[:, 0:2].astype(I32), blk)
    return dict(x=x, route=route, dest=dest, xb=_row_gather(h, slot_tok), block_e=block_e, n_used=n_used)


def _moe_experts(d, wg, wu, wd, blk, tf):
    yb = _moe_ffn(d['xb'], d['block_e'], d['n_used'], wg, wu, wd, blk, tf)
    d2 = d['dest'].reshape(-1, 2)
    return _row_gather(yb, d2[:, 0]), _row_gather(yb, d2[:, 1])


def _moe_merge(d, y1, y2, tm, final_g):
    return _moe_combine(d['x'], y1, y2, d['route'], tm, final_g)


def _norm_kernel(x_ref, g_ref, o_ref):
    o_ref[...] = _rms(x_ref[...], g_ref[...])


def _final_norm(x, g, tm):
    T = x.shape[0]
    return pl.pallas_call(
        _norm_kernel,
        grid=(T // tm,),
        in_specs=[pl.BlockSpec((tm, D_MODEL), lambda i: (i, 0)),
                  pl.BlockSpec((1, D_MODEL), lambda i: (0, 0))],
        out_specs=pl.BlockSpec((tm, D_MODEL), lambda i: (i, 0)),
        out_shape=jax.ShapeDtypeStruct((T, D_MODEL), F32),
        compiler_params=_params(("parallel",)),
        name="final_norm",
    )(x, g)


_IN_SPLITS = (D_FOX, D_FOX, D_FOX, N_HEADS, D_NSA, 6 * HEAD_DIM, 3 * N_HEADS,
              D_FOX, D_FOX, D_GLA, GLA_RANK, D_GLA)


def _reorder_w_in(w):
    offs = [0]
    for s in _IN_SPLITS:
        offs.append(offs[-1] + s)
    seg = lambda k: w[:, offs[k]:offs[k + 1]]
    fq, fk, fv, ff, nq, nkv, ng, gq, gk, gv, glr, gog = [seg(k) for k in range(12)]
    pad = jnp.zeros((w.shape[0], LANES - SM_GLR - GLA_RANK), w.dtype)
    return jnp.concatenate([fq, fk, fv, nq, nkv, gq, gk, gv, gog, ff, ng, glr, pad], axis=1)


def _rope_table(pos):
    inv = ROPE_THETA ** (-jnp.arange(ROPE_HALF, dtype=F32) / ROPE_HALF)
    ang = pos.astype(F32)[:, None] * inv[None, :]
    cos, sin = jnp.cos(ang), jnp.sin(ang)
    P = pos.shape[0]
    one = jnp.ones((P, HEAD_DIM - ROPE_DIM), F32)
    zero = jnp.zeros((P, HEAD_DIM - ROPE_DIM), F32)
    z8 = jnp.zeros((P, ROPE_HALF), F32)
    a64 = jnp.concatenate([cos, cos, one], axis=1)
    p64 = jnp.concatenate([z8, sin, zero], axis=1)
    m64 = jnp.concatenate([-sin, z8, zero], axis=1)
    i64 = jnp.ones((P, HEAD_DIM), F32)
    o64 = jnp.zeros((P, HEAD_DIM), F32)
    return jnp.concatenate([a64, a64, p64, p64, m64, m64, a64, i64, p64, o64, m64, o64], axis=1)


def _layer_mix_params(l, norm_mix_g, w_in, b_fox_f, w_cmp, pe_cmp, w_gla_gk, b_gla_gk, g_gla_norm, w_out):
    sb = jnp.zeros((1, LANES), F32).at[0, SM_FF:SM_FF + N_HEADS].set(b_fox_f[l])
    wgk = jnp.zeros((LANES, D_FOX), F32).at[SM_GLR:SM_GLR + GLA_RANK].set(w_gla_gk[l])
    w_r = _reorder_w_in(w_in[l])
    return dict(g=norm_mix_g[l].reshape(1, D_MODEL), w_r=w_r.astype(BF16), w_r32=w_r, sb=sb,
                wgk=wgk.astype(BF16), wgk32=wgk, w_out32=w_out[l],
                bgk=b_gla_gk[l].reshape(1, D_FOX), w_cmp=w_cmp[l],
                pe_cmp=pe_cmp[l].reshape(2, 1, CMP_LEN * HEAD_DIM),
                gn=g_gla_norm[l].reshape(1, GLA_DV), w_out=w_out[l].astype(BF16))


def _mix_prompt(x, B, S, l, depth, p, tab, tm, t_fox, tk_sel, tc_gla, carry):
    (fq, fkv_all, nq, nqr, rows_all, win_t, gqk, gv, gog, glog, small, small_t) = _in_projection(
        x, 0, B * S, tm, p['g'], p['w_r'], p['sb'], p['wgk'], p['bgk'], tab, S, batch=(B, S, depth), layer=l,
        prev=(carry['fkv'], carry['rows']) if carry else None)
    carry.update(fkv=fkv_all, rows=rows_all)
    cr = _fox_cumsum(small_t, min(S, Tiles.cumsum))
    o_fox = _fox_prompt(fq, fkv_all, l, cr, B, S, t_fox)
    n_blk = S // CMP_LEN
    blocks = rows_all[l, :, 0:2 * HEAD_DIM, :].reshape(B, 2, HEAD_DIM, n_blk, CMP_LEN).transpose(1, 0, 3, 4, 2)
    blocks = blocks.reshape(2, B * n_blk, CMP_LEN * HEAD_DIM)
    cmp = _compress(blocks, p['pe_cmp'], p['w_cmp'], min(Tiles.compress_rows, B * n_blk), True)
    o_nsa = _nsa_prompt(nq, nqr, small, cmp, rows_all, l, win_t, B, S, tk_sel)
    o_gla, g_state = _gla_prompt(gqk, gv, glog, B, S, tc_gla)
    x_new = _out_projection(x, o_fox, o_nsa, o_gla, gog, p['gn'], p['w_out'], tm)
    wp = min(WINDOW, S)
    return x_new, dict(small=small_t[:, 0:N_HEADS, :], win=win_t[:, :, S - wp:], g_state=g_state)


def _per_head_col(vals):
    r = lax.broadcasted_iota(I32, (SUBLANES, 1), 0)
    out = jnp.zeros((SUBLANES, 1), F32)
    for h, v in enumerate(vals):
        out = out + jnp.where(r == h, v, 0.0)
    return out


def _per_head_row(vals, width):
    grp = lax.broadcasted_iota(I32, (1, width), 1) // HEAD_DIM
    out = jnp.zeros((1, width), F32)
    for h, v in enumerate(vals):
        out = out + jnp.where(grp == h, v, 0.0)
    return out


def _head_lane_sums(row):
    grp = lax.broadcasted_iota(I32, row.shape, 1) // HEAD_DIM
    return [jnp.sum(jnp.where(grp == h, row, 0.0), axis=1, keepdims=True) for h in range(N_HEADS)]


def _cols_of(row):
    return jnp.concatenate([jnp.broadcast_to(row[:, j:j + LANES], (LANES, LANES)).T
                            for j in range(0, row.shape[1], LANES)], axis=0)


def _row_of(col):
    return jnp.concatenate([jnp.broadcast_to(col[j:j + LANES], (LANES, LANES)).T[0:1, :]
                            for j in range(0, col.shape[0], LANES)], axis=1)


def _sublane_group_sum(x):
    return jnp.sum(x.reshape(x.shape[0] // SUBLANES, SUBLANES, x.shape[1]), axis=0)


def _fold_matrix(n_pages):
    r = lax.broadcasted_iota(I32, (n_pages * SUBLANES, n_pages * N_HEADS * SUBLANES), 0)
    c = lax.broadcasted_iota(I32, (n_pages * SUBLANES, n_pages * N_HEADS * SUBLANES), 1)
    blk = c // SUBLANES
    return ((blk // N_HEADS == r // SUBLANES) & (blk % N_HEADS == r % SUBLANES)).astype(F32)


def _fox_decode_kernel(pt_ref, q_ref, kvn_ref, smn_ref, *refs, n_pages):
    del pt_ref
    kv_refs = refs[0:n_pages]
    lf_refs = refs[n_pages:2 * n_pages]
    o_ref, lf_sc, part_sc = refs[2 * n_pages:]
    R = n_pages * SUBLANES
    PG = kv_refs[0].shape[-1]
    row = pl.ds(pl.program_id(0) % SUBLANES, 1)
    q_row = q_ref[row, :]
    kvn = kvn_ref[row, :]
    smn = smn_ref[row, :]
    q_cols = _cols_of(q_row)

    lf_sc[...] = jnp.zeros_like(lf_sc)
    for p in range(n_pages):
        lf_sc[p * SUBLANES:p * SUBLANES + N_HEADS, :] = lf_refs[p][...]
    lft = lf_sc[...]
    k0 = lax.broadcasted_iota(I32, (PG, PG), 0)
    k1 = lax.broadcasted_iota(I32, (PG, PG), 1)
    within = _dot01_r(lft, k0 > k1)
    tot = jnp.broadcast_to(jnp.sum(lft, axis=1, keepdims=True), (R, PG))
    r0 = lax.broadcasted_iota(I32, (R, R), 0)
    r1 = lax.broadcasted_iota(I32, (R, R), 1)
    later = (r1 % SUBLANES == r0 % SUBLANES) & (r1 // SUBLANES > r0 // SUBLANES)
    cross = _dot01(later, tot)
    rr = lax.broadcasted_iota(I32, (R, 1), 0) % SUBLANES
    newcol = jnp.zeros((R, 1), F32)
    for h in range(N_HEADS):
        newcol = newcol + jnp.where(rr == h, smn[:, SM_FF + h:SM_FF + h + 1], 0.0)
    bias = (within + cross + newcol).reshape(n_pages, SUBLANES, PG)

    for p in range(n_pages):
        for h in range(N_HEADS):
            g = p * N_HEADS + h
            part_sc[g * SUBLANES:(g + 1) * SUBLANES, :] = _sublane_group_sum(
                kv_refs[p][0, h] * q_cols[h * HEAD_DIM:(h + 1) * HEAD_DIM])
    s = _dot01(_fold_matrix(n_pages), part_sc[...])
    s3 = s.reshape(n_pages, SUBLANES, PG) * SCALE + bias
    s_new = _per_head_col(_head_lane_sums(q_row * kvn[:, 0:D_FOX])) * SCALE
    m = jnp.max(jnp.max(s3, axis=2, keepdims=True), axis=0)
    m = jnp.maximum(m, s_new)
    p3 = jnp.exp(s3 - m[None])
    pn = jnp.exp(s_new - m)
    inv = 1.0 / (jnp.sum(jnp.sum(p3, axis=2, keepdims=True), axis=0) + pn)
    o_cols = []
    for h in range(N_HEADS):
        acc = jnp.zeros((HEAD_DIM, PG), F32)
        for p in range(n_pages):
            acc = acc + kv_refs[p][1, h] * p3[p, h:h + 1, :]
        o_cols.append(jnp.sum(acc, axis=1, keepdims=True) * inv[h:h + 1])
    w_new = _per_head_row([pn[h:h + 1] * inv[h:h + 1] for h in range(N_HEADS)], D_FOX)
    o_ref[row, :] = _row_of(jnp.concatenate(o_cols, axis=0)) + w_new * kvn[:, D_FOX:2 * D_FOX]


def _fox_decode(l, pt_flat, n_pages, fq, fkv, small, kv_cache_t, lft_cache):
    DB = fq.shape[0]
    PG = kv_cache_t.shape[-1]
    page = lambda p, nz: (lambda b, pt: (l, pt[b * n_pages + p]) + (0,) * nz)
    rows8 = lambda w: pl.BlockSpec((SUBLANES, w), lambda b, pt: (b // SUBLANES, 0))
    return pl.pallas_call(
        functools.partial(_fox_decode_kernel, n_pages=n_pages),
        grid_spec=pltpu.PrefetchScalarGridSpec(
            num_scalar_prefetch=1,
            grid=(DB,),
            in_specs=[rows8(D_FOX), rows8(2 * D_FOX), rows8(LANES)]
            + [pl.BlockSpec((None, None, 2, N_HEADS, HEAD_DIM, PG), page(p, 4)) for p in range(n_pages)]
            + [pl.BlockSpec((None, None, N_HEADS, PG), page(p, 2)) for p in range(n_pages)],
            out_specs=rows8(D_FOX),
            scratch_shapes=[pltpu.VMEM((n_pages * SUBLANES, PG), F32),
                            pltpu.VMEM((n_pages * N_HEADS * SUBLANES, PG), F32)],
        ),
        out_shape=jax.ShapeDtypeStruct((DB, D_FOX), F32),
        compiler_params=_params(("arbitrary",)),
        name="fox_decode",
    )(pt_flat, fq, fkv, small, *([kv_cache_t] * n_pages), *([lft_cache] * n_pages))


def _nsa_decode_kernel(pt_ref, q_ref, qr_ref, rown_ref, winn_ref, sm_ref, win_ref, *refs,
                       n_pages, past_len):
    del pt_ref
    pg_refs = refs[0:n_pages]
    cmp_refs = refs[n_pages:2 * n_pages]
    o_ref, nw_ref, cmp_sc, qc_sc, part_sc, sw_sc = refs[-6:]
    R = n_pages * SUBLANES
    PG = pg_refs[0].shape[-1]
    WB = win_ref.shape[-1]
    per_page = PG // CMP_LEN
    assert per_page <= SUBLANES and PG == 2 * SEL_LEN and R == LANES
    jt = past_len // SEL_LEN
    row = pl.ds(pl.program_id(0) % SUBLANES, 1)
    q_row = q_ref[row, :]
    qr_row = qr_ref[row, :]
    rown = rown_ref[row, :]
    winn = winn_ref[row, :]
    smn = sm_ref[row, :]
    qr_cols = _cols_of(qr_row)
    rep4 = lambda r64: jnp.concatenate([r64] * N_HEADS, axis=1)

    qc_sc[...] = jnp.zeros_like(qc_sc)
    for h in range(N_HEADS):
        qc_sc[h:h + 1, 0:HEAD_DIM] = q_row[:, h * HEAD_DIM:(h + 1) * HEAD_DIM]
    head_row = lax.broadcasted_iota(I32, (SUBLANES, 1), 0) < N_HEADS

    cmp_sc[...] = jnp.zeros_like(cmp_sc)
    for p in range(n_pages):
        cmp_sc[p * SUBLANES:p * SUBLANES + per_page, :] = cmp_refs[p][...]
    cmpa = cmp_sc[...]
    lane = lax.broadcasted_iota(I32, (1, R), 1)
    blk = per_page * (lane // SUBLANES) + lane % SUBLANES
    complete = (lane % SUBLANES < per_page) & ((blk + 1) * CMP_LEN - 1 <= past_len)
    s = _dot_nt_hilo(qc_sc[...], cmpa) * SCALE
    s = jnp.where(complete, s, NEG)
    e = jnp.exp(s - jnp.max(s, axis=-1, keepdims=True))
    pc = e / jnp.sum(e, axis=-1, keepdims=True) * complete.astype(F32)
    vcb_t = cmpa.T[HEAD_DIM:2 * HEAD_DIM, :]
    o_cmp = [jnp.sum(vcb_t * pc[h:h + 1, :], axis=1, keepdims=True) for h in range(N_HEADS)]

    imp_c = jnp.sum(jnp.where(head_row, pc, 0.0), axis=0, keepdims=True)
    imp_s = imp_c + pltpu.roll(imp_c, R - 1, 1)
    cand = (lane % SUBLANES == 0) | (lane % SUBLANES == 2)
    jsel = 2 * (lane // SUBLANES) + (lane % SUBLANES) // 2
    score = jnp.where(jsel == jt, 2.0 * SEL_FORCE,
                      jnp.where((jsel == 0) | (jsel == jt - 1), SEL_FORCE,
                                jnp.where(jsel <= jt, imp_s + 0.0, -1.0)))
    score_b = jnp.broadcast_to(score, (R, R))
    key_row = _order_key(score_b)
    key_col = _order_key(score_b.T)
    l0 = lax.broadcasted_iota(I32, (R, R), 0)
    l1 = lax.broadcasted_iota(I32, (R, R), 1)
    cand_col = (l0 % SUBLANES == 0) | (l0 % SUBLANES == 2)
    beats = cand_col & (key_col > jnp.where(l0 < l1, key_row - 1, key_row))
    cnt = jnp.sum(beats.astype(I32), axis=0, keepdims=True)
    sel_row = (cand & (cnt < TOP_N - 1)).astype(F32)
    sel_col = jnp.broadcast_to(sel_row, (R, R)).T
    half = ((l0 % SUBLANES == 0) & (l1 < SEL_LEN)) | ((l0 % SUBLANES == 2) & (l1 >= SEL_LEN))
    z = jnp.where(half, sel_col, 0.0)
    same_page = (l1 // SUBLANES == l0 // SUBLANES).astype(BF16)
    picked = jnp.dot(same_page, z.astype(BF16), preferred_element_type=F32)
    picked = picked.reshape(n_pages, SUBLANES, PG) > 0.5

    for p in range(n_pages):
        ks_t = pg_refs[p][2]
        for h in range(N_HEADS):
            g = p * N_HEADS + h
            part_sc[g * SUBLANES:(g + 1) * SUBLANES, :] = _sublane_group_sum(
                ks_t * qr_cols[h * HEAD_DIM:(h + 1) * HEAD_DIM])
    s = _dot01(_fold_matrix(n_pages), part_sc[...])
    s3 = jnp.where(picked, s.reshape(n_pages, SUBLANES, PG) * SCALE, NEG)
    s_new = _per_head_col(_head_lane_sums(qr_row * rep4(rown[:, 2 * HEAD_DIM:3 * HEAD_DIM]))) * SCALE
    m = jnp.maximum(jnp.max(jnp.max(s3, axis=2, keepdims=True), axis=0), s_new)
    p3 = jnp.exp(s3 - m[None])
    pn = jnp.exp(s_new - m)
    inv = 1.0 / (jnp.sum(jnp.sum(p3, axis=2, keepdims=True), axis=0) + pn)
    o_sel = []
    for h in range(N_HEADS):
        acc = jnp.zeros((HEAD_DIM, PG), F32)
        for p in range(n_pages):
            acc = acc + pg_refs[p][3] * p3[p, h:h + 1, :]
        o_sel.append(jnp.sum(acc, axis=1, keepdims=True) * inv[h:h + 1])

    kw_t = win_ref[0]
    vw_t = win_ref[1]
    wlane = lax.broadcasted_iota(I32, (1, WB), 1)
    wpos = past_len - WB + wlane
    wd = past_len - wpos
    wok = (wd >= 0) & (wd < WINDOW) & (wpos >= 0)
    sw_sc[...] = jnp.zeros_like(sw_sc)
    for h in range(N_HEADS):
        qh = qr_cols[h * HEAD_DIM:(h + 1) * HEAD_DIM]
        sw_sc[h:h + 1, :] = jnp.sum(kw_t * jnp.concatenate([qh] * (WB // LANES), axis=1), axis=0, keepdims=True)
    sw = jnp.where(wok, sw_sc[...] * SCALE, NEG)
    sw_new = _per_head_col(_head_lane_sums(qr_row * rep4(winn[:, 0:HEAD_DIM]))) * SCALE
    mw = jnp.maximum(jnp.max(sw, axis=-1, keepdims=True), sw_new)
    ew = jnp.exp(sw - mw)
    en = jnp.exp(sw_new - mw)
    invw = 1.0 / (jnp.sum(ew, axis=-1, keepdims=True) + en)

    gate = lambda h, c: smn[:, SM_NG + 3 * h + c:SM_NG + 3 * h + c + 1]
    o_cols = []
    for h in range(N_HEADS):
        o_win = jnp.sum(vw_t * ew[h:h + 1, :], axis=1, keepdims=True) * invw[h:h + 1]
        o_cols.append(gate(h, 0) * o_cmp[h] + gate(h, 1) * o_sel[h] + gate(h, 2) * o_win)
    w_sel = _per_head_row([gate(h, 1) * pn[h:h + 1] * inv[h:h + 1] for h in range(N_HEADS)], D_NSA)
    w_win = _per_head_row([gate(h, 2) * en[h:h + 1] * invw[h:h + 1] for h in range(N_HEADS)], D_NSA)
    o_ref[row, :] = (_row_of(jnp.concatenate(o_cols, axis=0))
                     + w_sel * rep4(rown[:, 3 * HEAD_DIM:4 * HEAD_DIM])
                     + w_win * rep4(winn[:, HEAD_DIM:2 * HEAD_DIM]))
    last = lax.broadcasted_iota(I32, (HEAD_DIM, WB), 1) == WB - 1
    winn_cols = _cols_of(winn)
    for s in range(2):
        new_col = winn_cols[s * HEAD_DIM:(s + 1) * HEAD_DIM, 0:1]
        nw_ref[s] = jnp.where(last, new_col, pltpu.roll(win_ref[s], WB - 1, 1))


def _nsa_decode(l, pt_flat, n_pages, past_len, nq, nqr, rows, win, small, nsa_cache_t, cmp_pool, win_state_t,
                prev):
    DB = nq.shape[0]
    PG = nsa_cache_t.shape[-1]
    WB = win_state_t.shape[-1]
    page = lambda p: (lambda b, pt: (l, pt[b * n_pages + p], 0, 0, 0))
    cpage = lambda p: (lambda b, pt: (pt[b * n_pages + p], 0, 0))
    carried = [] if prev is None else [prev]
    rows8 = lambda w: pl.BlockSpec((SUBLANES, w), lambda b, pt: (b // SUBLANES, 0))
    return pl.pallas_call(
        functools.partial(_nsa_decode_kernel, n_pages=n_pages, past_len=past_len),
        grid_spec=pltpu.PrefetchScalarGridSpec(
            num_scalar_prefetch=1,
            grid=(DB,),
            in_specs=[rows8(D_NSA), rows8(D_NSA), rows8(4 * HEAD_DIM), rows8(2 * HEAD_DIM), rows8(LANES),
                      pl.BlockSpec((None, None, 2, HEAD_DIM, WB), lambda b, pt: (l, b, 0, 0, 0))]
            + [pl.BlockSpec((None, None, 4, HEAD_DIM, PG), page(p)) for p in range(n_pages)]
            + [pl.BlockSpec((None, PG // CMP_LEN, LANES), cpage(p)) for p in range(n_pages)]
            + [pl.BlockSpec(memory_space=pl.ANY)] * len(carried),
            out_specs=[rows8(D_NSA),
                       pl.BlockSpec((None, None, 2, HEAD_DIM, WB), lambda b, pt: (l, b, 0, 0, 0))],
            scratch_shapes=[pltpu.VMEM((n_pages * SUBLANES, LANES), F32),
                            pltpu.VMEM((SUBLANES, LANES), F32),
                            pltpu.VMEM((n_pages * N_HEADS * SUBLANES, PG), F32),
                            pltpu.VMEM((SUBLANES, WB), F32)],
        ),
        out_shape=[jax.ShapeDtypeStruct((DB, D_NSA), F32),
                   jax.ShapeDtypeStruct(win_state_t.shape, F32)],
        input_output_aliases={7 + 2 * n_pages: 1} if carried else {},
        compiler_params=_params(("arbitrary",)),
        name="nsa_decode",
    )(pt_flat, nq, nqr, rows, win, small, win_state_t,
      *([nsa_cache_t] * n_pages), *([cmp_pool] * n_pages), *carried)


def _gla_decode_kernel(qk_ref, g_ref, v_ref, s_ref, *rest):
    o_ref, so_ref = rest[-2:]
    for j in range(SUBLANES):
        qk = qk_ref[j:j + 1, :]
        q_cols = _cols_of(qk[:, 0:D_FOX] * SCALE)
        k_cols = _cols_of(qk[:, D_FOX:2 * D_FOX])
        decay = jnp.exp(_cols_of(g_ref[j:j + 1, :]))
        for h in range(N_HEADS):
            hs = slice(h * HEAD_DIM, (h + 1) * HEAD_DIM)
            v_row = v_ref[j:j + 1, h * GLA_DV:(h + 1) * GLA_DV]
            s_new = decay[hs] * s_ref[j, h] + k_cols[hs] * v_row
            so_ref[j, h] = s_new
            o_ref[j:j + 1, h * GLA_DV:(h + 1) * GLA_DV] = jnp.sum(q_cols[hs] * s_new, axis=0, keepdims=True)


def _gla_decode(l, gqk, gv, glog, state, prev):
    depth, DB = state.shape[0:2]
    nb = SUBLANES
    rows = lambda w: pl.BlockSpec((nb, w), lambda i: (i, 0))
    sspec = pl.BlockSpec((None, nb, N_HEADS, HEAD_DIM, GLA_DV), lambda i: (l, i, 0, 0, 0))
    carried = [] if prev is None else [prev]
    return pl.pallas_call(
        _gla_decode_kernel,
        grid=(DB // nb,),
        in_specs=[rows(2 * D_FOX), rows(D_FOX), rows(D_GLA), sspec]
        + [pl.BlockSpec(memory_space=pl.ANY)] * len(carried),
        out_specs=[rows(D_GLA), sspec],
        out_shape=[jax.ShapeDtypeStruct((DB, D_GLA), F32), jax.ShapeDtypeStruct(state.shape, F32)],
        input_output_aliases={4: 1} if carried else {},
        compiler_params=_params(("parallel",)),
        name="gla_decode",
    )(gqk, glog, gv, state, *carried)


def _mix_sample(x, l, p, tab, pt_flat, n_pages, past_len, fox_kv_t, fox_lft_c, nsa_t, cmp_blocks,
                win_state_t, gla_state, carry):
    DB = x.shape[0]
    (fq, fkv, nq, nqr, rows, win, gqk, gv, gog, glog, small) = _in_projection(
        x, 0, DB, DB, p['g'], p['w_r32'], p['sb'], p['wgk32'], p['bgk'], tab, DB)
    o_fox = _fox_decode(l, pt_flat, n_pages, fq, fkv, small, fox_kv_t, fox_lft_c)
    n_pool, PG = nsa_t.shape[1], nsa_t.shape[-1]
    per_layer = n_pool * (PG // CMP_LEN)
    cmp_pool = _compress(cmp_blocks, p['pe_cmp'], p['w_cmp'], Tiles.compress_rows, False,
                         row0=l * per_layer, n_rows=per_layer)
    cmp_pool = cmp_pool.reshape(n_pool, PG // CMP_LEN, LANES)
    o_nsa, new_win = _nsa_decode(l, pt_flat, n_pages, past_len, nq, nqr, rows, win, small,
                                 nsa_t, cmp_pool, win_state_t, carry.get('win'))
    o_gla, g_state = _gla_decode(l, gqk, gv, glog, gla_state, carry.get('gla'))
    carry.update(win=new_win, gla=g_state)
    x_new = _out_projection(x, o_fox, o_nsa, o_gla, gog, p['gn'], p['w_out32'], DB)
    return x_new, dict(fkv=fkv, small=small, rows=rows)


def kernel(x_prompt, x_sample, cache_fox_kv, cache_fox_logf, cache_nsa_kv, state_nsa_win, state_gla,
           page_table, norm_mix_g, w_in, b_fox_f, w_cmp, pe_cmp, w_gla_gk, b_gla_gk, g_gla_norm, w_out,
           norm_ffn_g, dense_w_gate, dense_w_up, dense_w_down, moe_w_router, moe_w_gate, moe_w_up,
           moe_w_down, final_norm_g):
    B, S, _ = x_prompt.shape
    DB, TN, _ = x_sample.shape
    assert TN == 1
    depth, n_pool, PG = cache_fox_kv.shape[0:3]
    n_pages = page_table.shape[1]
    past_len = n_pages * PG
    WB = state_nsa_win.shape[2]
    xp = x_prompt.reshape(B * S, D_MODEL)
    xs = x_sample.reshape(DB, D_MODEL)
    tab_p = _rope_table(jnp.arange(S))
    tab_s = _rope_table(jnp.full((DB,), past_len, I32))
    pt_flat = page_table.reshape(-1).astype(I32)
    fox_kv_t = jnp.transpose(cache_fox_kv, (0, 1, 3, 4, 5, 2))
    fox_lft_c = jnp.swapaxes(cache_fox_logf, 2, 3)
    nsa_t = jnp.transpose(cache_nsa_kv, (0, 1, 3, 4, 2))
    win_state_t = jnp.transpose(state_nsa_win, (0, 1, 3, 4, 2))
    n_cmp = depth * n_pool * (PG // CMP_LEN)
    cmp_blocks = _block_major(cache_nsa_kv[:, :, :, 0:2, :].reshape(n_cmp * CMP_LEN, 2, HEAD_DIM), n_cmp)
    gfin = final_norm_g.reshape(1, D_MODEL)
    cp, cs = [], []
    carry_p, carry_s = {}, {}
    for l in range(depth):
        p = _layer_mix_params(l, norm_mix_g, w_in, b_fox_f, w_cmp, pe_cmp, w_gla_gk, b_gla_gk,
                              g_gla_norm, w_out)
        gf = norm_ffn_g[l].reshape(1, D_MODEL)
        i = l // 2
        moe = l % 2 == 1
        xp, c = _mix_prompt(xp, B, S, l, depth, p, tab_p, Tiles.prompt_rows, Tiles.fox, Tiles.nsa_keys,
                            Tiles.gla_rows, carry_p)
        cp.append(c)
        if moe:
            wr = jnp.zeros((D_MODEL, LANES), F32).at[:, 0:N_EXPERTS].set(moe_w_router[i])
            experts = (moe_w_gate[i], moe_w_up[i], moe_w_down[i])
            disp_p = _moe_dispatch(xp, gf, wr, Tiles.prompt_rows, Tiles.expert_block)
        xs, c = _mix_sample(xs, l, p, tab_s, pt_flat, n_pages, past_len, fox_kv_t, fox_lft_c, nsa_t,
                            cmp_blocks, win_state_t, state_gla, carry_s)
        cs.append(c)
        if moe:
            disp_s = _moe_dispatch(xs, gf, wr, DB, LANES)
            fin = gfin if l == depth - 1 else None
            y_p = _moe_experts(disp_p, *experts, Tiles.expert_block, Tiles.ffn_cols)
            y_s = _moe_experts(disp_s, *experts, LANES, Tiles.ffn_cols_sample)
            xp = _moe_merge(disp_p, *y_p, Tiles.prompt_rows, fin)
            xs = _moe_merge(disp_s, *y_s, DB, fin)
        else:
            wg, wu, wd = (dense_w_gate[i].astype(BF16), dense_w_up[i].astype(BF16),
                          dense_w_down[i].astype(BF16))
            xp = _dense_ffn(xp, gf, wg, wu, wd, Tiles.dense_rows, Tiles.ffn_cols)
            xs = _dense_ffn(xs, gf, dense_w_gate[i], dense_w_up[i], dense_w_down[i], DB, Tiles.ffn_cols_sample)
    if depth % 2 == 1:
        xp = _final_norm(xp, gfin, Tiles.prompt_rows)
        xs = _final_norm(xs, gfin, DB)
    y_p = xp.reshape(B, S, D_MODEL)
    y_s = xs.reshape(DB, 1, D_MODEL)
    wp = min(WINDOW, S)
    st = lambda key, group: jnp.stack([c[key] for c in group])
    return (y_p, y_s,
            carry_p['fkv'].reshape(depth, B, 2, N_HEADS, HEAD_DIM, S).transpose(0, 1, 5, 2, 3, 4),
            st('small', cp).transpose(0, 1, 3, 2),
            carry_p['rows'].reshape(depth, B, 4, HEAD_DIM, S).transpose(0, 1, 4, 2, 3),
            st('win', cp).reshape(depth, B, 2, HEAD_DIM, wp).transpose(0, 1, 4, 2, 3),
            st('g_state', cp),
            st('fkv', cs).reshape(depth, DB, 1, 2, N_HEADS, HEAD_DIM),
            st('small', cs)[:, :, SM_FF:SM_FF + N_HEADS].reshape(depth, DB, 1, N_HEADS),
            st('rows', cs).reshape(depth, DB, 1, 4, HEAD_DIM),
            jnp.transpose(carry_s['win'], (0, 1, 4, 2, 3)),
            carry_s['gla'])
```

```python
import functools

import jax
import jax.numpy as jnp
from jax import lax
from jax.experimental import pallas as pl
from jax.experimental.pallas import tpu as pltpu
from jax.experimental.pallas import tpu_sc as plsc

F32 = jnp.float32
BF16 = jnp.bfloat16
I32 = jnp.int32
HI = lax.Precision.HIGHEST

D_MODEL = 1024
HEAD_DIM = 64
N_HEADS = 4
D_FOX = N_HEADS * HEAD_DIM
D_NSA = N_HEADS * HEAD_DIM
GLA_DV = 128
D_GLA = N_HEADS * GLA_DV
GLA_RANK = 16
GLA_TAU = 16.0
GLA_CHUNK = 64
CMP_LEN = 32
SEL_LEN = 64
TOP_N = 16
WINDOW = 512
ROPE_THETA = 500000.0
ROPE_DIM = HEAD_DIM // 4
ROPE_HALF = ROPE_DIM // 2
D_FF = 3584
N_EXPERTS = 8
EPS = 1e-6
SEL_FORCE = 1e9
NEG = -1e30
SCALE = HEAD_DIM ** -0.5
LOG2E = 1.4426950408889634

LANES = 128
SUBLANES = 8
VMEM_BYTES_V7X = 64 * 1024 * 1024
VMEM_LIMIT = VMEM_BYTES_V7X - 8 * 1024 * 1024


class Tiles:
    prompt_rows = 512
    fox = 512
    nsa_keys = 512
    gla_rows = 256
    cumsum = 512
    compress_rows = 256
    dense_rows = 1024
    ffn_cols = 512
    ffn_cols_sample = 896
    expert_block = 1024

C_FQ = 0
C_FKV = 256
C_NQ = 768
C_NKV = 1024
C_GQK = 1408
C_GV = 1920
C_GOG = 2432
C_SMALL = 2944
C_END = 3072
SM_FF = 0
SM_NG = 4
SM_GLR = 16

NT = (((1,), (1,)), ((), ()))


def _params(sem):
    return pltpu.CompilerParams(dimension_semantics=sem, vmem_limit_bytes=VMEM_LIMIT)


def _rms(x, g):
    ms = jnp.mean(x * x, axis=-1, keepdims=True)
    return x * lax.rsqrt(ms + EPS) * g


def _sigmoid(x):
    return 1.0 / (1.0 + jnp.exp(-x))


def _log_sigmoid(x):
    return -(jnp.maximum(-x, 0.0) + jnp.log1p(jnp.exp(-jnp.abs(x))))


def _silu(x):
    return x * _sigmoid(x)


def _bdot(a, b):
    return jnp.dot(a.astype(BF16), b.astype(BF16), preferred_element_type=F32)


def _split3(x):
    h = x.astype(BF16)
    r = x - h.astype(F32)
    m = r.astype(BF16)
    return h, m, (r - m.astype(F32)).astype(BF16)


def _dot01(m01, x):
    mb = m01.astype(BF16)
    h, m, l = _split3(x)
    return (jnp.dot(mb, h, preferred_element_type=F32) + jnp.dot(mb, m, preferred_element_type=F32)
            + jnp.dot(mb, l, preferred_element_type=F32))


def _dot01_r(x, m01):
    mb = m01.astype(BF16)
    h, m, l = _split3(x)
    return (jnp.dot(h, mb, preferred_element_type=F32) + jnp.dot(m, mb, preferred_element_type=F32)
            + jnp.dot(l, mb, preferred_element_type=F32))


def _dot01_nt(m01, x):
    mb = m01.astype(BF16)
    h, m, l = _split3(x)
    return (lax.dot_general(mb, h, NT, preferred_element_type=F32)
            + lax.dot_general(mb, m, NT, preferred_element_type=F32)
            + lax.dot_general(mb, l, NT, preferred_element_type=F32))


def _dot_nt_hilo(a, b):
    ah = a.astype(BF16)
    al = (a - ah.astype(F32)).astype(BF16)
    bh = b.astype(BF16)
    bl = (b - bh.astype(F32)).astype(BF16)
    return (lax.dot_general(ah, bh, NT, preferred_element_type=F32)
            + lax.dot_general(al, bh, NT, preferred_element_type=F32)
            + lax.dot_general(ah, bl, NT, preferred_element_type=F32))


def _wdot(a, w):
    if w.dtype == F32:
        return jnp.dot(a.astype(F32), w, precision=HI, preferred_element_type=F32)
    return jnp.dot(a.astype(BF16), w, preferred_element_type=F32)


def _rope128(x, a, bp, bm):
    return x * a + pltpu.roll(x, ROPE_HALF, 1) * bp + pltpu.roll(x, LANES - ROPE_HALF, 1) * bm


def _inproj_kernel(x_ref, g_ref, w_ref, sb_ref, wgk_ref, bgk_ref, tab_ref, *refs, feature_major, n_carried):
    (fq_ref, fkv_ref, nq_ref, nqr_ref, rows_ref, win_ref,
     gqk_ref, gv_ref, gog_ref, glog_ref, small_ref, *extra) = refs[n_carried:]
    h = _rms(x_ref[...], g_ref[...]).astype(w_ref.dtype)

    def put(ref, v):
        ref[...] = v.T if feature_major else v

    def mm(a, b):
        return _wdot(h, w_ref[:, a:b])

    fq_ref[...] = mm(C_FQ, C_FKV)
    put(fkv_ref, mm(C_FKV, C_NQ))
    tab = tab_ref[...]
    ab, pb, mb = tab[:, 0:128], tab[:, 128:256], tab[:, 256:384]
    af, pf, mf = tab[:, 384:512], tab[:, 512:640], tab[:, 640:768]
    nq = mm(C_NQ, C_NKV)
    nq_ref[...] = nq
    nqr_ref[:, 0:128] = _rope128(nq[:, 0:128], ab, pb, mb)
    nqr_ref[:, 128:256] = _rope128(nq[:, 128:256], ab, pb, mb)
    nkv = mm(C_NKV, C_GQK)
    put(rows_ref, jnp.concatenate([nkv[:, 0:128], _rope128(nkv[:, 128:256], af, pf, mf)], axis=1))
    put(win_ref, _rope128(nkv[:, 256:384], af, pf, mf))
    gqk_ref[...] = mm(C_GQK, C_GV)
    gv_ref[...] = mm(C_GV, C_GOG)
    gog_ref[...] = mm(C_GOG, C_SMALL)
    sm = mm(C_SMALL, C_END)
    glog_ref[...] = _log_sigmoid(_wdot(sm, wgk_ref[...]) + bgk_ref[...]) * (1.0 / GLA_TAU)
    smb = sm + sb_ref[...]
    lane = lax.broadcasted_iota(I32, smb.shape, 1)
    small = jnp.where(lane < SM_NG, _log_sigmoid(smb), _sigmoid(smb))
    small_ref[...] = small
    if feature_major:
        extra[0][...] = small.T[0:SUBLANES, :]


def _in_projection(x_all, row0, n_rows, tm, g, w_r, sb, wgk, bgk, tab, tab_period, batch=None, layer=None,
                   prev=None):
    assert n_rows % tm == 0 and row0 % tm == 0 and tab_period % tm == 0
    nt = n_rows // tm
    b0 = row0 // tm
    npd = tab_period // tm
    widths = (256, 512, 256, 256, 256, 128, 512, 512, 512, 256, 128)
    FKV, ROWS, WIN = 1, 4, 5
    full = lambda shape: pl.BlockSpec(shape, lambda i: (0, 0))
    row_spec = lambda w: pl.BlockSpec((tm, w), lambda i: (i, 0))
    carried = [] if prev is None else list(prev)
    if batch is not None:
        B, S, depth = batch
        per = S // tm
        assert n_rows == B * S and S % tm == 0
        t_spec = lambda w: pl.BlockSpec((None, w, tm), lambda i: (i // per, 0, i % per))
        l_spec = lambda w: pl.BlockSpec((None, None, w, tm), lambda i: (layer, i // per, 0, i % per))
        out_specs = [l_spec(w) if k in (FKV, ROWS) else t_spec(w) if k == WIN else row_spec(w)
                     for k, w in enumerate(widths)] + [t_spec(SUBLANES)]
        out_shape = [jax.ShapeDtypeStruct((depth, B, w, S) if k in (FKV, ROWS) else (B, w, S) if k == WIN
                                          else (n_rows, w), F32)
                     for k, w in enumerate(widths)] + [jax.ShapeDtypeStruct((B, SUBLANES, S), F32)]
    else:
        out_specs = [row_spec(w) for w in widths]
        out_shape = [jax.ShapeDtypeStruct((n_rows, w), F32) for w in widths]
    return pl.pallas_call(
        functools.partial(_inproj_kernel, feature_major=batch is not None, n_carried=len(carried)),
        grid=(nt,),
        in_specs=[
            pl.BlockSpec((tm, D_MODEL), lambda i: (b0 + i, 0)),
            full((1, D_MODEL)),
            full((D_MODEL, C_END)),
            full((1, LANES)),
            full((LANES, 256)),
            full((1, 256)),
            pl.BlockSpec((tm, 768), lambda i: (i % npd, 0)),
        ] + [pl.BlockSpec(memory_space=pl.ANY)] * len(carried),
        out_specs=out_specs,
        out_shape=out_shape,
        input_output_aliases={7: FKV, 8: ROWS} if carried else {},
        compiler_params=_params(("parallel",)),
        name="in_projection",
    )(x_all, g, w_r, sb, wgk, bgk, tab, *carried)


def _cumsum_kernel(sm_ref, cr_ref, carry):
    t = pl.program_id(1)
    ts = sm_ref.shape[1]

    @pl.when(t == 0)
    def _():
        carry[...] = jnp.zeros_like(carry)

    r = lax.broadcasted_iota(I32, (ts, ts), 0)
    c = lax.broadcasted_iota(I32, (ts, ts), 1)
    cs = _dot01_r(sm_ref[...], r <= c) + carry[...]
    carry[...] = cs[:, ts - 1:ts]
    cr_ref[...] = cs * LOG2E


def _fox_cumsum(small_t, ts):
    B, _, S = small_t.shape
    spec = pl.BlockSpec((None, SUBLANES, ts), lambda b, t: (b, 0, t))
    return pl.pallas_call(
        _cumsum_kernel,
        grid=(B, S // ts),
        in_specs=[spec],
        out_specs=spec,
        out_shape=jax.ShapeDtypeStruct((B, SUBLANES, S), F32),
        scratch_shapes=[pltpu.VMEM((SUBLANES, 1), F32)],
        compiler_params=_params(("parallel", "arbitrary")),
        name="fox_cumsum",
    )(small_t)


def _pair_mask(shape, h, axis=1):
    return (lax.broadcasted_iota(I32, shape, axis) // HEAD_DIM) == (h % 2)


def _fox_prompt_kernel(qi_ref, kj_ref, q_ref, kv_ref, cr_ref, o_ref, *scratch):
    i = qi_ref[pl.program_id(1)]
    j = kj_ref[pl.program_id(1)]
    tq = q_ref.shape[0]
    tk = kv_ref.shape[1]
    q_sc, m_sc, acc_sc = scratch[0:N_HEADS], scratch[N_HEADS:2 * N_HEADS], scratch[2 * N_HEADS:]

    @pl.when(j == 0)
    def _():
        for h in range(N_HEADS):
            m_sc[h][...] = jnp.full_like(m_sc[h], NEG)
            acc_sc[h][...] = jnp.zeros_like(acc_sc[h])
            slab = q_ref[:, (h // 2) * LANES:(h // 2 + 1) * LANES] * (SCALE * LOG2E)
            q_sc[h][...] = jnp.where(_pair_mask(slab.shape, h), slab, 0.0).astype(BF16)

    def tile(diagonal):
        k_slabs = [kv_ref[c0:c0 + LANES, :].astype(BF16) for c0 in (0, LANES)]
        if diagonal:
            mask = lax.broadcasted_iota(I32, (1, tk), 1) <= lax.broadcasted_iota(I32, (tq, 1), 0)
        for h in range(N_HEADS):
            v_slab = kv_ref[D_FOX + (h // 2) * LANES:D_FOX + (h // 2 + 1) * LANES, :]
            v_aug = jnp.where(_pair_mask(v_slab.shape, h, 0), v_slab, 1.0).astype(BF16)
            s = jnp.dot(q_sc[h][...], k_slabs[h // 2], preferred_element_type=F32) - cr_ref[h:h + 1, :]
            if diagonal:
                s = jnp.where(mask, s, NEG)
            m_old = m_sc[h][...]
            m_new = jnp.maximum(m_old, jnp.max(s, axis=-1, keepdims=True))
            p = jnp.exp2(s - m_new).astype(BF16)
            acc_sc[h][...] = (jnp.exp2(m_old - m_new) * acc_sc[h][...]
                              + lax.dot_general(p, v_aug, NT, preferred_element_type=F32))
            m_sc[h][...] = m_new

    @pl.when(j < i)
    def _():
        tile(False)

    @pl.when(j == i)
    def _():
        tile(True)
        for h in range(N_HEADS):
            a = acc_sc[h][...]
            lo = (h % 2) * HEAD_DIM
            den = a[:, HEAD_DIM - lo:HEAD_DIM - lo + 1]
            o_ref[:, h * HEAD_DIM:(h + 1) * HEAD_DIM] = a[:, lo:lo + HEAD_DIM] / den


def _fox_prompt(fq, fkv_all, l, cr, B, S, t):
    n = S // t
    pairs = [(i, j) for i in range(n) for j in range(i + 1)]
    qi = jnp.asarray([p[0] for p in pairs], I32)
    kj = jnp.asarray([p[1] for p in pairs], I32)
    return pl.pallas_call(
        _fox_prompt_kernel,
        grid_spec=pltpu.PrefetchScalarGridSpec(
            num_scalar_prefetch=2,
            grid=(B, len(pairs)),
            in_specs=[
                pl.BlockSpec((t, D_FOX), lambda b, s, qi, kj: (b * n + qi[s], 0)),
                pl.BlockSpec((None, None, 2 * D_FOX, t), lambda b, s, qi, kj: (l, b, 0, kj[s])),
                pl.BlockSpec((None, SUBLANES, t), lambda b, s, qi, kj: (b, 0, kj[s])),
            ],
            out_specs=pl.BlockSpec((t, D_FOX), lambda b, s, qi, kj: (b * n + qi[s], 0)),
            scratch_shapes=([pltpu.VMEM((t, LANES), BF16)] * N_HEADS + [pltpu.VMEM((t, 1), F32)] * N_HEADS
                            + [pltpu.VMEM((t, LANES), F32)] * N_HEADS),
        ),
        out_shape=jax.ShapeDtypeStruct((B * S, D_FOX), F32),
        compiler_params=_params(("parallel", "arbitrary")),
        name="fox_prompt",
    )(qi, kj, fq, fkv_all, cr)


def _compress_kernel(x_ref, pe_ref, w_ref, o_ref, *, exact):
    for s in range(2):
        x = x_ref[s] + pe_ref[s]
        w = w_ref[s]
        if exact:
            y = jnp.dot(x, w, precision=HI, preferred_element_type=F32)
        else:
            xh = x.astype(BF16)
            xl = (x - xh.astype(F32)).astype(BF16)
            wh = w.astype(BF16)
            wl = (w - wh.astype(F32)).astype(BF16)
            y = (jnp.dot(xh, wh, preferred_element_type=F32) + jnp.dot(xl, wh, preferred_element_type=F32)
                 + jnp.dot(xh, wl, preferred_element_type=F32))
        o_ref[:, s * HEAD_DIM:(s + 1) * HEAD_DIM] = y


def _compress(x3, pe, w, tr, exact, row0=0, n_rows=None):
    K = x3.shape[2]
    R = x3.shape[1] if n_rows is None else n_rows
    assert R % tr == 0 and row0 % tr == 0
    b0 = row0 // tr
    return pl.pallas_call(
        functools.partial(_compress_kernel, exact=exact),
        grid=(R // tr,),
        in_specs=[pl.BlockSpec((2, tr, K), lambda i: (0, b0 + i, 0)),
                  pl.BlockSpec((2, 1, K), lambda i: (0, 0, 0)),
                  pl.BlockSpec((2, K, HEAD_DIM), lambda i: (0, 0, 0))],
        out_specs=pl.BlockSpec((tr, LANES), lambda i: (i, 0)),
        out_shape=jax.ShapeDtypeStruct((R, LANES), F32),
        compiler_params=_params(("parallel",)),
        name="nsa_compress",
    )(x3, pe, w)


def _block_major(kv, n_blocks):
    return kv.reshape(n_blocks, CMP_LEN, 2, HEAD_DIM).transpose(2, 0, 1, 3).reshape(
        2, n_blocks, CMP_LEN * HEAD_DIM)


def _order_key(x):
    b = lax.bitcast_convert_type(x, I32)
    return jnp.where(b < 0, b ^ jnp.int32(0x7FFFFFFF), b)


def _nsa_prompt_kernel(nq_ref, nqr_ref, sm_ref, cmp_ref, rows_ref, win_ref, o_ref, qc_sc, qx_sc, *, tk):
    QB = nq_ref.shape[0]
    S = rows_ref.shape[1]
    nb = cmp_ref.shape[0]
    nsel = S // SEL_LEN
    i = pl.program_id(1)
    qs = i * QB
    qpos = qs + lax.broadcasted_iota(I32, (QB, 1), 0)

    HQ = N_HEADS * QB
    lo_half = lax.broadcasted_iota(I32, (QB, LANES), 1) < HEAD_DIM

    def stack_heads(ref, scale, dst):
        for h in range(N_HEADS):
            slab = ref[:, (h // 2) * LANES:(h // 2 + 1) * LANES] * scale
            if h % 2:
                slab = pltpu.roll(slab, HEAD_DIM, 1)
            dst[h * QB:(h + 1) * QB, :] = jnp.where(lo_half, slab, 0.0).astype(dst.dtype)

    cmp = cmp_ref[...]
    n_l = lax.broadcasted_iota(I32, (1, nb), 1)
    complete = ((n_l + 1) * CMP_LEN - 1) <= qpos
    stack_heads(nq_ref, SCALE, qc_sc)
    s = _dot_nt_hilo(qc_sc[...], cmp).reshape(N_HEADS, QB, nb)
    s = jnp.where(complete[None], s, NEG)
    e = jnp.exp(s - jnp.max(s, axis=-1, keepdims=True))
    p = e / jnp.sum(e, axis=-1, keepdims=True) * complete.astype(F32)[None]
    o_cmp = _bdot(p.reshape(HQ, nb), cmp)
    psum = jnp.sum(p, axis=0)

    pj = lax.broadcasted_iota(I32, (nsel, nb), 0)
    pn = lax.broadcasted_iota(I32, (nsel, nb), 1)
    imp_t = _dot01_nt(pn // (SEL_LEN // CMP_LEN) == pj, psum)
    jt = (qs + lax.broadcasted_iota(I32, (1, QB), 1)) // SEL_LEN
    jj = lax.broadcasted_iota(I32, (nsel, 1), 0)
    score = jnp.where(jj == jt, 2.0 * SEL_FORCE,
                      jnp.where((jj == 0) | (jj == jt - 1), SEL_FORCE,
                                jnp.where(jj <= jt, imp_t + 0.0, -1.0)))
    key = _order_key(score)
    key_m1 = key - 1
    ngrp = nsel // SUBLANES
    sub = lax.broadcasted_iota(I32, (SUBLANES, QB), 0)
    kg = [key[r * SUBLANES:(r + 1) * SUBLANES, :] for r in range(ngrp)]
    kg1 = [key_m1[r * SUBLANES:(r + 1) * SUBLANES, :] for r in range(ngrp)]
    cnt = [jnp.zeros((SUBLANES, QB), I32) for _ in range(ngrp)]
    for jp in range(nsel):
        g = jp // SUBLANES
        row = key[jp:jp + 1, :]
        mixed = jnp.where(sub > (jp % SUBLANES), kg1[g], kg[g])
        for r in range(ngrp):
            thr = kg[r] if r < g else (kg1[r] if r > g else mixed)
            cnt[r] = cnt[r] + (row > thr).astype(I32)
    sel_t = jnp.concatenate([(c < TOP_N).astype(F32) for c in cnt], axis=0)
    if nsel < QB:
        sel_t = jnp.concatenate([sel_t, jnp.zeros((QB - nsel, QB), F32)], axis=0)
    sel = sel_t.T.astype(BF16)

    stack_heads(nqr_ref, SCALE * LOG2E, qx_sc)
    qx = qx_sc[...]

    def attend(valid, slab_t, m_old, acc_old):
        n = slab_t.shape[1]
        s = jnp.dot(qx, slab_t.astype(BF16), preferred_element_type=F32)
        s = jnp.where(valid[None], s.reshape(N_HEADS, QB, n), NEG).reshape(HQ, n)
        m_new = jnp.maximum(m_old, jnp.max(s, axis=-1, keepdims=True))
        p = jnp.exp2(s - m_new).astype(BF16)
        ones_k = lax.broadcasted_iota(I32, slab_t.shape, 0) < HEAD_DIM
        v_aug = jnp.where(ones_k, 1.0, slab_t).astype(BF16)
        acc = jnp.exp2(m_old - m_new) * acc_old + lax.dot_general(p, v_aug, NT, preferred_element_type=F32)
        return m_new, acc

    jrow = lax.broadcasted_iota(I32, (QB, 1), 0)

    def sel_tile(k0, m_old, acc_old, diagonal):
        kpos = k0 + lax.broadcasted_iota(I32, (1, tk), 1)
        expand = (jrow == kpos // SEL_LEN).astype(BF16)
        valid = jnp.dot(sel, expand, preferred_element_type=F32) > 0.5
        if diagonal:
            valid = valid & (kpos <= qpos)
        return attend(valid, rows_ref[2 * HEAD_DIM:4 * HEAD_DIM, pl.ds(k0, tk)], m_old, acc_old)

    n_full = qs // tk
    init = (jnp.full((HQ, 1), NEG, F32), jnp.zeros((HQ, LANES), F32))
    m_s, acc_s = lax.fori_loop(
        0, n_full, lambda t, c: sel_tile(pl.multiple_of(t * tk, tk), c[0], c[1], False), init)
    _, acc_s = sel_tile(pl.multiple_of(n_full * tk, tk), m_s, acc_s, True)

    wlen = WINDOW + QB
    w0 = pl.multiple_of(jnp.maximum(qs - WINDOW, 0), QB)
    wpos = w0 + lax.broadcasted_iota(I32, (1, wlen), 1)
    d = qpos - wpos
    _, acc_w = attend((d >= 0) & (d < WINDOW), win_ref[:, pl.ds(w0, wlen)],
                      jnp.full((HQ, 1), NEG, F32), jnp.zeros((HQ, LANES), F32))

    sm = sm_ref[...]
    for h in range(N_HEADS):
        rs = slice(h * QB, (h + 1) * QB)
        o_sel = acc_s[rs] * (1.0 / acc_s[rs, 0:1])
        o_win = acc_w[rs] * (1.0 / acc_w[rs, 0:1])
        c = SM_NG + 3 * h
        mix = sm[:, c:c + 1] * o_cmp[rs] + sm[:, c + 1:c + 2] * o_sel + sm[:, c + 2:c + 3] * o_win
        if h % 2 == 0:
            mix = pltpu.roll(mix, HEAD_DIM, 1)
        lo = (h % 2) * HEAD_DIM
        o_ref[:, h * HEAD_DIM:(h + 1) * HEAD_DIM] = mix[:, lo:lo + HEAD_DIM]


def _nsa_prompt(nq, nqr, small, cmp, rows_all, l, win, B, S, tk):
    QB = 128
    nq_t = S // QB
    nb = S // CMP_LEN
    assert S % tk == 0 and S >= WINDOW + QB
    return pl.pallas_call(
        functools.partial(_nsa_prompt_kernel, tk=tk),
        grid=(B, nq_t),
        in_specs=[
            pl.BlockSpec((QB, D_NSA), lambda b, i: (b * nq_t + i, 0)),
            pl.BlockSpec((QB, D_NSA), lambda b, i: (b * nq_t + i, 0)),
            pl.BlockSpec((QB, LANES), lambda b, i: (b * nq_t + i, 0)),
            pl.BlockSpec((nb, LANES), lambda b, i: (b, 0)),
            pl.BlockSpec((None, None, 4 * HEAD_DIM, S), lambda b, i: (l, b, 0, 0)),
            pl.BlockSpec((None, 2 * HEAD_DIM, S), lambda b, i: (b, 0, 0)),
        ],
        out_specs=pl.BlockSpec((QB, D_NSA), lambda b, i: (b * nq_t + i, 0)),
        out_shape=jax.ShapeDtypeStruct((B * S, D_NSA), F32),
        scratch_shapes=[pltpu.VMEM((N_HEADS * QB, LANES), F32), pltpu.VMEM((N_HEADS * QB, LANES), BF16)],
        compiler_params=_params(("parallel", "parallel")),
        name="nsa_prompt",
    )(nq, nqr, small, cmp, rows_all, win)


def _gla_prompt_kernel(qk_ref, v_ref, g_ref, o_ref, st_ref, s_sc):
    t = pl.program_id(1)
    nt = pl.num_programs(1)
    tc = qk_ref.shape[0]
    C = GLA_CHUNK

    @pl.when(t == 0)
    def _():
        s_sc[...] = jnp.zeros_like(s_sc)

    r = lax.broadcasted_iota(I32, (tc, tc), 0)
    c = lax.broadcasted_iota(I32, (tc, tc), 1)
    same = (r // C) == (c // C)
    causal = same & (c <= r)
    g = g_ref[...]
    gcum = _dot01(causal, g)
    g_t = g.T
    gcum_t = _dot01_r(g_t, same & (r <= c))
    gtot_t = _dot01_r(g_t, same)
    q_e = (qk_ref[:, 0:D_FOX] * SCALE * jnp.exp(gcum)).astype(BF16)
    k_e = (qk_ref[:, D_FOX:2 * D_FOX] * jnp.exp(-gcum)).astype(BF16)
    kd_t = (qk_ref[:, D_FOX:2 * D_FOX].T * jnp.exp(gtot_t - gcum_t)).astype(BF16)
    decay_t = jnp.exp(gtot_t)
    for h in range(N_HEADS):
        hs = slice(h * HEAD_DIM, (h + 1) * HEAD_DIM)
        v = v_ref[:, h * GLA_DV:(h + 1) * GLA_DV].astype(BF16)
        a = jnp.where(causal, lax.dot_general(q_e[:, hs], k_e[:, hs], NT, preferred_element_type=F32), 0.0)
        o_intra = jnp.dot(a.astype(BF16), v, preferred_element_type=F32)
        state = s_sc[h]
        for ci in range(tc // C):
            rs = slice(ci * C, (ci + 1) * C)
            o_ref[rs, h * GLA_DV:(h + 1) * GLA_DV] = (
                o_intra[rs] + jnp.dot(q_e[rs, hs], state.astype(BF16), preferred_element_type=F32))
            state = (decay_t[hs, ci * C:ci * C + 1] * state
                     + jnp.dot(kd_t[hs, rs], v[rs], preferred_element_type=F32))
        s_sc[h] = state

    @pl.when(t == nt - 1)
    def _():
        st_ref[...] = s_sc[...]


def _gla_prompt(gqk, gv, glog, B, S, tc):
    nt = S // tc
    return pl.pallas_call(
        _gla_prompt_kernel,
        grid=(B, nt),
        in_specs=[pl.BlockSpec((tc, 2 * D_FOX), lambda b, t: (b * nt + t, 0)),
                  pl.BlockSpec((tc, D_GLA), lambda b, t: (b * nt + t, 0)),
                  pl.BlockSpec((tc, D_FOX), lambda b, t: (b * nt + t, 0))],
        out_specs=[pl.BlockSpec((tc, D_GLA), lambda b, t: (b * nt + t, 0)),
                   pl.BlockSpec((None, N_HEADS, HEAD_DIM, GLA_DV), lambda b, t: (b, 0, 0, 0))],
        out_shape=[jax.ShapeDtypeStruct((B * S, D_GLA), F32),
                   jax.ShapeDtypeStruct((B, N_HEADS, HEAD_DIM, GLA_DV), F32)],
        scratch_shapes=[pltpu.VMEM((N_HEADS, HEAD_DIM, GLA_DV), F32)],
        compiler_params=_params(("parallel", "arbitrary")),
        name="gla_prompt",
    )(gqk, gv, glog)


def _outproj_kernel(x_ref, of_ref, on_ref, og_ref, gog_ref, gn_ref, w_ref, o_ref):
    acc = _wdot(of_ref[...], w_ref[0:D_FOX, :])
    acc = acc + _wdot(on_ref[...], w_ref[D_FOX:D_FOX + D_NSA, :])
    for h in range(N_HEADS):
        hs = slice(h * GLA_DV, (h + 1) * GLA_DV)
        z = _rms(og_ref[:, hs], gn_ref[...]) * _silu(gog_ref[:, hs])
        w0 = D_FOX + D_NSA + h * GLA_DV
        acc = acc + _wdot(z, w_ref[w0:w0 + GLA_DV, :])
    o_ref[...] = x_ref[...] + acc


def _out_projection(x, o_fox, o_nsa, o_gla, gog, gn, w_out, tm):
    T = x.shape[0]
    assert T % tm == 0
    row = lambda w: pl.BlockSpec((tm, w), lambda i: (i, 0))
    return pl.pallas_call(
        _outproj_kernel,
        grid=(T // tm,),
        in_specs=[row(D_MODEL), row(D_FOX), row(D_NSA), row(D_GLA), row(D_GLA),
                  pl.BlockSpec((1, GLA_DV), lambda i: (0, 0)),
                  pl.BlockSpec((D_MODEL, D_MODEL), lambda i: (0, 0))],
        out_specs=row(D_MODEL),
        out_shape=jax.ShapeDtypeStruct((T, D_MODEL), F32),
        compiler_params=_params(("parallel",)),
        name="out_projection",
    )(x, o_fox, o_nsa, o_gla, gog, gn, w_out)


def _dense_ffn_kernel(x_ref, g_ref, wg_ref, wu_ref, wd_ref, o_ref, h_sc, acc_sc):
    f = pl.program_id(1)
    nf = pl.num_programs(1)

    @pl.when(f == 0)
    def _():
        h_sc[...] = _rms(x_ref[...], g_ref[...]).astype(h_sc.dtype)
        acc_sc[...] = jnp.zeros_like(acc_sc)

    h = h_sc[...]
    a = _wdot(h, wg_ref[...])
    u = _wdot(h, wu_ref[...])
    acc_sc[...] += _wdot(_silu(a) * u, wd_ref[...])

    @pl.when(f == nf - 1)
    def _():
        o_ref[...] = x_ref[...] + acc_sc[...]


def _dense_ffn(x, g, wg, wu, wd, tm, tf):
    T = x.shape[0]
    assert T % tm == 0 and D_FF % tf == 0
    return pl.pallas_call(
        _dense_ffn_kernel,
        grid=(T // tm, D_FF // tf),
        in_specs=[pl.BlockSpec((tm, D_MODEL), lambda i, f: (i, 0)),
                  pl.BlockSpec((1, D_MODEL), lambda i, f: (0, 0)),
                  pl.BlockSpec((D_MODEL, tf), lambda i, f: (0, f)),
                  pl.BlockSpec((D_MODEL, tf), lambda i, f: (0, f)),
                  pl.BlockSpec((tf, D_MODEL), lambda i, f: (f, 0))],
        out_specs=pl.BlockSpec((tm, D_MODEL), lambda i, f: (i, 0)),
        out_shape=jax.ShapeDtypeStruct((T, D_MODEL), F32),
        scratch_shapes=[pltpu.VMEM((tm, D_MODEL), wg.dtype), pltpu.VMEM((tm, D_MODEL), F32)],
        compiler_params=_params(("parallel", "arbitrary")),
        name="dense_ffn",
    )(x, g, wg, wu, wd)


def _router_kernel(x_ref, g_ref, wr_ref, h_ref, r_ref):
    h = _rms(x_ref[...], g_ref[...])
    h_ref[...] = h
    logits = jnp.dot(h, wr_ref[...], precision=HI, preferred_element_type=F32)
    lane = lax.broadcasted_iota(I32, logits.shape, 1)
    lg = jnp.where(lane < N_EXPERTS, logits, -jnp.inf)
    m1 = jnp.max(lg, axis=-1, keepdims=True)
    i1 = jnp.min(jnp.where(lg == m1, lane, LANES), axis=-1, keepdims=True)
    lg2 = jnp.where(lane == i1, -jnp.inf, lg)
    m2 = jnp.max(lg2, axis=-1, keepdims=True)
    i2 = jnp.min(jnp.where(lg2 == m2, lane, LANES), axis=-1, keepdims=True)
    e = jnp.exp(m2 - m1)
    den = 1.0 + e
    r_ref[...] = jnp.where(lane == 0, i1.astype(F32),
                           jnp.where(lane == 1, i2.astype(F32),
                                     jnp.where(lane == 2, 1.0 / den,
                                               jnp.where(lane == 3, e / den, 0.0))))


def _router(x, g, wr_pad, tm):
    T = x.shape[0]
    assert T % tm == 0
    return pl.pallas_call(
        _router_kernel,
        grid=(T // tm,),
        in_specs=[pl.BlockSpec((tm, D_MODEL), lambda i: (i, 0)),
                  pl.BlockSpec((1, D_MODEL), lambda i: (0, 0)),
                  pl.BlockSpec((D_MODEL, LANES), lambda i: (0, 0))],
        out_specs=[pl.BlockSpec((tm, D_MODEL), lambda i: (i, 0)),
                   pl.BlockSpec((tm, LANES), lambda i: (i, 0))],
        out_shape=[jax.ShapeDtypeStruct((T, D_MODEL), F32),
                   jax.ShapeDtypeStruct((T, LANES), F32)],
        compiler_params=_params(("parallel",)),
        name="moe_router",
    )(x, g, wr_pad)


GATHER_WINDOW = 32
SC_WORKERS = 32


def _row_gather(src, idx):
    n = idx.shape[0]
    step = GATHER_WINDOW * SC_WORKERS
    n_pad = -(-n // step) * step
    if n_pad != n:
        idx = jnp.concatenate([idx, jnp.zeros((n_pad - n,), idx.dtype)])
    width = src.shape[1]
    per_worker = n_pad // SC_WORKERS
    mesh = plsc.VectorSubcoreMesh(core_axis_name="core", subcore_axis_name="subcore")

    @functools.partial(pl.kernel, out_type=jax.ShapeDtypeStruct((n_pad, width), src.dtype), mesh=mesh,
                       scratch_types=[pltpu.VMEM((per_worker,), I32),
                                      pltpu.VMEM((GATHER_WINDOW, width), src.dtype)],
                       name="row_gather")
    def gather(src_hbm, idx_hbm, dst_hbm, idx_v, buf):
        worker = lax.axis_index("core") * (SC_WORKERS // 2) + lax.axis_index("subcore")
        base = worker * per_worker
        pltpu.sync_copy(idx_hbm.at[pl.ds(base, per_worker)], idx_v)

        @pl.loop(0, per_worker // GATHER_WINDOW)
        def _(j):
            pltpu.sync_copy(src_hbm.at[idx_v.at[pl.ds(j * GATHER_WINDOW, GATHER_WINDOW)]], buf)
            pltpu.sync_copy(buf, dst_hbm.at[pl.ds(base + j * GATHER_WINDOW, GATHER_WINDOW)])

    return gather(src, idx)


def _moe_ffn_kernel(be_ref, nu_ref, x_ref, wg_ref, wu_ref, wd_ref, o_ref, acc_sc, x_sc):
    b = pl.program_id(0)
    f = pl.program_id(1)
    nf = pl.num_programs(1)
    used = b < nu_ref[0]

    @pl.when(used)
    def _():
        @pl.when(f == 0)
        def _():
            acc_sc[...] = jnp.zeros_like(acc_sc)
            x_sc[...] = x_ref[...].astype(BF16)

        x = x_sc[...]
        a = jnp.dot(x, wg_ref[...].astype(BF16), preferred_element_type=F32)
        u = jnp.dot(x, wu_ref[...].astype(BF16), preferred_element_type=F32)
        acc_sc[...] += _bdot(_silu(a) * u, wd_ref[...])

        @pl.when(f == nf - 1)
        def _():
            o_ref[...] = acc_sc[...]

    @pl.when(jnp.logical_not(used) & (f == nf - 1))
    def _():
        o_ref[...] = jnp.zeros_like(o_ref)


def _moe_ffn(xb, block_e, n_used, wg, wu, wd, blk, tf):
    cap = xb.shape[0]
    nb = cap // blk
    nf = D_FF // tf

    def bsel(b, nu):
        return jnp.minimum(b, nu[0] - 1)

    def fsel(b, f, nu):
        return jnp.where(b < nu[0], f, nf - 1)

    return pl.pallas_call(
        _moe_ffn_kernel,
        grid_spec=pltpu.PrefetchScalarGridSpec(
            num_scalar_prefetch=2,
            grid=(nb, nf),
            in_specs=[
                pl.BlockSpec((blk, D_MODEL), lambda b, f, be, nu: (bsel(b, nu), 0)),
                pl.BlockSpec((None, D_MODEL, tf), lambda b, f, be, nu: (be[bsel(b, nu)], 0, fsel(b, f, nu))),
                pl.BlockSpec((None, D_MODEL, tf), lambda b, f, be, nu: (be[bsel(b, nu)], 0, fsel(b, f, nu))),
                pl.BlockSpec((None, tf, D_MODEL), lambda b, f, be, nu: (be[bsel(b, nu)], fsel(b, f, nu), 0)),
            ],
            out_specs=pl.BlockSpec((blk, D_MODEL), lambda b, f, be, nu: (b, 0)),
            scratch_shapes=[pltpu.VMEM((blk, D_MODEL), F32), pltpu.VMEM((blk, D_MODEL), BF16)],
        ),
        out_shape=jax.ShapeDtypeStruct((cap, D_MODEL), F32),
        compiler_params=_params(("arbitrary", "arbitrary")),
        name="moe_ffn",
    )(block_e, n_used, xb, wg, wu, wd)


def _moe_combine_kernel(x_ref, y1_ref, y2_ref, r_ref, g_ref, o_ref, *, final):
    r = r_ref[...]
    y = x_ref[...] + (r[:, 2:3] * y1_ref[...] + r[:, 3:4] * y2_ref[...])
    o_ref[...] = _rms(y, g_ref[...]) if final else y


def _moe_combine(x, y1, y2, route, tm, final_g):
    T = x.shape[0]
    g = jnp.ones((1, D_MODEL), F32) if final_g is None else final_g
    return pl.pallas_call(
        functools.partial(_moe_combine_kernel, final=final_g is not None),
        grid=(T // tm,),
        in_specs=[pl.BlockSpec((tm, D_MODEL), lambda i: (i, 0)),
                  pl.BlockSpec((tm, D_MODEL), lambda i: (i, 0)),
                  pl.BlockSpec((tm, D_MODEL), lambda i: (i, 0)),
                  pl.BlockSpec((tm, LANES), lambda i: (i, 0)),
                  pl.BlockSpec((1, D_MODEL), lambda i: (0, 0))],
        out_specs=pl.BlockSpec((tm, D_MODEL), lambda i: (i, 0)),
        out_shape=jax.ShapeDtypeStruct((T, D_MODEL), F32),
        compiler_params=_params(("parallel",)),
        name="moe_combine",
    )(x, y1, y2, route, g)


def _moe_plan(e_top, blk):
    T = e_top.shape[0]
    n = 2 * T
    flat_e = e_top.reshape(-1)
    onehot = (flat_e[:, None] == jnp.arange(N_EXPERTS, dtype=I32)[None, :]).astype(I32)
    csum = jnp.cumsum(onehot, axis=0)
    rank = jnp.sum((csum - onehot) * onehot, axis=1)
    counts = csum[-1]
    padded = (counts + blk - 1) // blk * blk
    ends = jnp.cumsum(padded)
    pstart = ends - padded
    dest = (pstart[flat_e] + rank).astype(I32)
    n_blocks = -(-n // blk) + N_EXPERTS
    cap = n_blocks * blk
    slot_tok = (jnp.arange(cap, dtype=I32) % T).at[dest].set(jnp.arange(n, dtype=I32) // 2)
    first = jnp.arange(n_blocks, dtype=I32) * blk
    block_e = jnp.minimum(jnp.sum((ends[None, :] <= first[:, None]).astype(I32), axis=1), N_EXPERTS - 1)
    n_used = (ends[-1] // blk).astype(I32).reshape(1)
    return dest, slot_tok, block_e, n_used


def _moe_dispatch(x, g, wr_pad, tm, blk):
    h, route = _router(x, g, wr_pad, tm)
    dest, slot_tok, block_e, n_used = _moe_plan(route[:, 0:2].astype(I32), blk)
    return dict(x=x, route=route, dest=dest, xb=_row_gather(h, slot_tok), block_e=block_e, n_used=n_used)


def _moe_experts(d, wg, wu, wd, blk, tf):
    yb = _moe_ffn(d['xb'], d['block_e'], d['n_used'], wg, wu, wd, blk, tf)
    d2 = d['dest'].reshape(-1, 2)
    return _row_gather(yb, d2[:, 0]), _row_gather(yb, d2[:, 1])


def _moe_merge(d, y1, y2, tm, final_g):
    return _moe_combine(d['x'], y1, y2, d['route'], tm, final_g)


def _norm_kernel(x_ref, g_ref, o_ref):
    o_ref[...] = _rms(x_ref[...], g_ref[...])


def _final_norm(x, g, tm):
    T = x.shape[0]
    return pl.pallas_call(
        _norm_kernel,
        grid=(T // tm,),
        in_specs=[pl.BlockSpec((tm, D_MODEL), lambda i: (i, 0)),
                  pl.BlockSpec((1, D_MODEL), lambda i: (0, 0))],
        out_specs=pl.BlockSpec((tm, D_MODEL), lambda i: (i, 0)),
        out_shape=jax.ShapeDtypeStruct((T, D_MODEL), F32),
        compiler_params=_params(("parallel",)),
        name="final_norm",
    )(x, g)


_IN_SPLITS = (D_FOX, D_FOX, D_FOX, N_HEADS, D_NSA, 6 * HEAD_DIM, 3 * N_HEADS,
              D_FOX, D_FOX, D_GLA, GLA_RANK, D_GLA)


def _reorder_w_in(w):
    offs = [0]
    for s in _IN_SPLITS:
        offs.append(offs[-1] + s)
    seg = lambda k: w[:, offs[k]:offs[k + 1]]
    fq, fk, fv, ff, nq, nkv, ng, gq, gk, gv, glr, gog = [seg(k) for k in range(12)]
    pad = jnp.zeros((w.shape[0], LANES - SM_GLR - GLA_RANK), w.dtype)
    return jnp.concatenate([fq, fk, fv, nq, nkv, gq, gk, gv, gog, ff, ng, glr, pad], axis=1)


def _rope_table(pos):
    inv = ROPE_THETA ** (-jnp.arange(ROPE_HALF, dtype=F32) / ROPE_HALF)
    ang = pos.astype(F32)[:, None] * inv[None, :]
    cos, sin = jnp.cos(ang), jnp.sin(ang)
    P = pos.shape[0]
    one = jnp.ones((P, HEAD_DIM - ROPE_DIM), F32)
    zero = jnp.zeros((P, HEAD_DIM - ROPE_DIM), F32)
    z8 = jnp.zeros((P, ROPE_HALF), F32)
    a64 = jnp.concatenate([cos, cos, one], axis=1)
    p64 = jnp.concatenate([z8, sin, zero], axis=1)
    m64 = jnp.concatenate([-sin, z8, zero], axis=1)
    i64 = jnp.ones((P, HEAD_DIM), F32)
    o64 = jnp.zeros((P, HEAD_DIM), F32)
    return jnp.concatenate([a64, a64, p64, p64, m64, m64, a64, i64, p64, o64, m64, o64], axis=1)


def _layer_mix_params(l, norm_mix_g, w_in, b_fox_f, w_cmp, pe_cmp, w_gla_gk, b_gla_gk, g_gla_norm, w_out):
    sb = jnp.zeros((1, LANES), F32).at[0, SM_FF:SM_FF + N_HEADS].set(b_fox_f[l])
    wgk = jnp.zeros((LANES, D_FOX), F32).at[SM_GLR:SM_GLR + GLA_RANK].set(w_gla_gk[l])
    w_r = _reorder_w_in(w_in[l])
    return dict(g=norm_mix_g[l].reshape(1, D_MODEL), w_r=w_r.astype(BF16), w_r32=w_r, sb=sb,
                wgk=wgk.astype(BF16), wgk32=wgk, w_out32=w_out[l],
                bgk=b_gla_gk[l].reshape(1, D_FOX), w_cmp=w_cmp[l],
                pe_cmp=pe_cmp[l].reshape(2, 1, CMP_LEN * HEAD_DIM),
                gn=g_gla_norm[l].reshape(1, GLA_DV), w_out=w_out[l].astype(BF16))


def _mix_prompt(x, B, S, l, depth, p, tab, tm, t_fox, tk_sel, tc_gla, carry):
    (fq, fkv_all, nq, nqr, rows_all, win_t, gqk, gv, gog, glog, small, small_t) = _in_projection(
        x, 0, B * S, tm, p['g'], p['w_r'], p['sb'], p['wgk'], p['bgk'], tab, S, batch=(B, S, depth), layer=l,
        prev=(carry['fkv'], carry['rows']) if carry else None)
    carry.update(fkv=fkv_all, rows=rows_all)
    cr = _fox_cumsum(small_t, min(S, Tiles.cumsum))
    o_fox = _fox_prompt(fq, fkv_all, l, cr, B, S, t_fox)
    n_blk = S // CMP_LEN
    blocks = rows_all[l, :, 0:2 * HEAD_DIM, :].reshape(B, 2, HEAD_DIM, n_blk, CMP_LEN).transpose(1, 0, 3, 4, 2)
    blocks = blocks.reshape(2, B * n_blk, CMP_LEN * HEAD_DIM)
    cmp = _compress(blocks, p['pe_cmp'], p['w_cmp'], min(Tiles.compress_rows, B * n_blk), True)
    o_nsa = _nsa_prompt(nq, nqr, small, cmp, rows_all, l, win_t, B, S, tk_sel)
    o_gla, g_state = _gla_prompt(gqk, gv, glog, B, S, tc_gla)
    x_new = _out_projection(x, o_fox, o_nsa, o_gla, gog, p['gn'], p['w_out'], tm)
    wp = min(WINDOW, S)
    return x_new, dict(small=small_t[:, 0:N_HEADS, :], win=win_t[:, :, S - wp:], g_state=g_state)


def _per_head_col(vals):
    r = lax.broadcasted_iota(I32, (SUBLANES, 1), 0)
    out = jnp.zeros((SUBLANES, 1), F32)
    for h, v in enumerate(vals):
        out = out + jnp.where(r == h, v, 0.0)
    return out


def _per_head_row(vals, width):
    grp = lax.broadcasted_iota(I32, (1, width), 1) // HEAD_DIM
    out = jnp.zeros((1, width), F32)
    for h, v in enumerate(vals):
        out = out + jnp.where(grp == h, v, 0.0)
    return out


def _head_lane_sums(row):
    grp = lax.broadcasted_iota(I32, row.shape, 1) // HEAD_DIM
    return [jnp.sum(jnp.where(grp == h, row, 0.0), axis=1, keepdims=True) for h in range(N_HEADS)]


def _cols_of(row):
    return jnp.concatenate([jnp.broadcast_to(row[:, j:j + LANES], (LANES, LANES)).T
                            for j in range(0, row.shape[1], LANES)], axis=0)


def _row_of(col):
    return jnp.concatenate([jnp.broadcast_to(col[j:j + LANES], (LANES, LANES)).T[0:1, :]
                            for j in range(0, col.shape[0], LANES)], axis=1)


def _sublane_group_sum(x):
    return jnp.sum(x.reshape(x.shape[0] // SUBLANES, SUBLANES, x.shape[1]), axis=0)


def _fold_matrix(n_pages):
    r = lax.broadcasted_iota(I32, (n_pages * SUBLANES, n_pages * N_HEADS * SUBLANES), 0)
    c = lax.broadcasted_iota(I32, (n_pages * SUBLANES, n_pages * N_HEADS * SUBLANES), 1)
    blk = c // SUBLANES
    return ((blk // N_HEADS == r // SUBLANES) & (blk % N_HEADS == r % SUBLANES)).astype(F32)


def _fox_decode_kernel(pt_ref, q_ref, kvn_ref, smn_ref, *refs, n_pages):
    del pt_ref
    kv_refs = refs[0:n_pages]
    lf_refs = refs[n_pages:2 * n_pages]
    o_ref, lf_sc, part_sc = refs[2 * n_pages:]
    R = n_pages * SUBLANES
    PG = kv_refs[0].shape[-1]
    row = pl.ds(pl.program_id(0) % SUBLANES, 1)
    q_row = q_ref[row, :]
    kvn = kvn_ref[row, :]
    smn = smn_ref[row, :]
    q_cols = _cols_of(q_row)

    lf_sc[...] = jnp.zeros_like(lf_sc)
    for p in range(n_pages):
        lf_sc[p * SUBLANES:p * SUBLANES + N_HEADS, :] = lf_refs[p][...]
    lft = lf_sc[...]
    k0 = lax.broadcasted_iota(I32, (PG, PG), 0)
    k1 = lax.broadcasted_iota(I32, (PG, PG), 1)
    within = _dot01_r(lft, k0 > k1)
    tot = jnp.broadcast_to(jnp.sum(lft, axis=1, keepdims=True), (R, PG))
    r0 = lax.broadcasted_iota(I32, (R, R), 0)
    r1 = lax.broadcasted_iota(I32, (R, R), 1)
    later = (r1 % SUBLANES == r0 % SUBLANES) & (r1 // SUBLANES > r0 // SUBLANES)
    cross = _dot01(later, tot)
    rr = lax.broadcasted_iota(I32, (R, 1), 0) % SUBLANES
    newcol = jnp.zeros((R, 1), F32)
    for h in range(N_HEADS):
        newcol = newcol + jnp.where(rr == h, smn[:, SM_FF + h:SM_FF + h + 1], 0.0)
    bias = (within + cross + newcol).reshape(n_pages, SUBLANES, PG)

    for p in range(n_pages):
        for h in range(N_HEADS):
            g = p * N_HEADS + h
            part_sc[g * SUBLANES:(g + 1) * SUBLANES, :] = _sublane_group_sum(
                kv_refs[p][0, h] * q_cols[h * HEAD_DIM:(h + 1) * HEAD_DIM])
    s = _dot01(_fold_matrix(n_pages), part_sc[...])
    s3 = s.reshape(n_pages, SUBLANES, PG) * SCALE + bias
    s_new = _per_head_col(_head_lane_sums(q_row * kvn[:, 0:D_FOX])) * SCALE
    m = jnp.max(jnp.max(s3, axis=2, keepdims=True), axis=0)
    m = jnp.maximum(m, s_new)
    p3 = jnp.exp(s3 - m[None])
    pn = jnp.exp(s_new - m)
    inv = 1.0 / (jnp.sum(jnp.sum(p3, axis=2, keepdims=True), axis=0) + pn)
    o_cols = []
    for h in range(N_HEADS):
        acc = jnp.zeros((HEAD_DIM, PG), F32)
        for p in range(n_pages):
            acc = acc + kv_refs[p][1, h] * p3[p, h:h + 1, :]
        o_cols.append(jnp.sum(acc, axis=1, keepdims=True) * inv[h:h + 1])
    w_new = _per_head_row([pn[h:h + 1] * inv[h:h + 1] for h in range(N_HEADS)], D_FOX)
    o_ref[row, :] = _row_of(jnp.concatenate(o_cols, axis=0)) + w_new * kvn[:, D_FOX:2 * D_FOX]


def _fox_decode(l, pt_flat, n_pages, fq, fkv, small, kv_cache_t, lft_cache):
    DB = fq.shape[0]
    PG = kv_cache_t.shape[-1]
    page = lambda p, nz: (lambda b, pt: (l, pt[b * n_pages + p]) + (0,) * nz)
    rows8 = lambda w: pl.BlockSpec((SUBLANES, w), lambda b, pt: (b // SUBLANES, 0))
    return pl.pallas_call(
        functools.partial(_fox_decode_kernel, n_pages=n_pages),
        grid_spec=pltpu.PrefetchScalarGridSpec(
            num_scalar_prefetch=1,
            grid=(DB,),
            in_specs=[rows8(D_FOX), rows8(2 * D_FOX), rows8(LANES)]
            + [pl.BlockSpec((None, None, 2, N_HEADS, HEAD_DIM, PG), page(p, 4)) for p in range(n_pages)]
            + [pl.BlockSpec((None, None, N_HEADS, PG), page(p, 2)) for p in range(n_pages)],
            out_specs=rows8(D_FOX),
            scratch_shapes=[pltpu.VMEM((n_pages * SUBLANES, PG), F32),
                            pltpu.VMEM((n_pages * N_HEADS * SUBLANES, PG), F32)],
        ),
        out_shape=jax.ShapeDtypeStruct((DB, D_FOX), F32),
        compiler_params=_params(("arbitrary",)),
        name="fox_decode",
    )(pt_flat, fq, fkv, small, *([kv_cache_t] * n_pages), *([lft_cache] * n_pages))


def _nsa_decode_kernel(pt_ref, q_ref, qr_ref, rown_ref, winn_ref, sm_ref, win_ref, *refs,
                       n_pages, past_len):
    del pt_ref
    pg_refs = refs[0:n_pages]
    cmp_refs = refs[n_pages:2 * n_pages]
    o_ref, nw_ref, cmp_sc, qc_sc, part_sc, sw_sc = refs[-6:]
    R = n_pages * SUBLANES
    PG = pg_refs[0].shape[-1]
    WB = win_ref.shape[-1]
    per_page = PG // CMP_LEN
    assert per_page <= SUBLANES and PG == 2 * SEL_LEN and R == LANES
    jt = past_len // SEL_LEN
    row = pl.ds(pl.program_id(0) % SUBLANES, 1)
    q_row = q_ref[row, :]
    qr_row = qr_ref[row, :]
    rown = rown_ref[row, :]
    winn = winn_ref[row, :]
    smn = sm_ref[row, :]
    qr_cols = _cols_of(qr_row)
    rep4 = lambda r64: jnp.concatenate([r64] * N_HEADS, axis=1)

    qc_sc[...] = jnp.zeros_like(qc_sc)
    for h in range(N_HEADS):
        qc_sc[h:h + 1, 0:HEAD_DIM] = q_row[:, h * HEAD_DIM:(h + 1) * HEAD_DIM]
    head_row = lax.broadcasted_iota(I32, (SUBLANES, 1), 0) < N_HEADS

    cmp_sc[...] = jnp.zeros_like(cmp_sc)
    for p in range(n_pages):
        cmp_sc[p * SUBLANES:p * SUBLANES + per_page, :] = cmp_refs[p][...]
    cmpa = cmp_sc[...]
    lane = lax.broadcasted_iota(I32, (1, R), 1)
    blk = per_page * (lane // SUBLANES) + lane % SUBLANES
    complete = (lane % SUBLANES < per_page) & ((blk + 1) * CMP_LEN - 1 <= past_len)
    s = _dot_nt_hilo(qc_sc[...], cmpa) * SCALE
    s = jnp.where(complete, s, NEG)
    e = jnp.exp(s - jnp.max(s, axis=-1, keepdims=True))
    pc = e / jnp.sum(e, axis=-1, keepdims=True) * complete.astype(F32)
    vcb_t = cmpa.T[HEAD_DIM:2 * HEAD_DIM, :]
    o_cmp = [jnp.sum(vcb_t * pc[h:h + 1, :], axis=1, keepdims=True) for h in range(N_HEADS)]

    imp_c = jnp.sum(jnp.where(head_row, pc, 0.0), axis=0, keepdims=True)
    imp_s = imp_c + pltpu.roll(imp_c, R - 1, 1)
    cand = (lane % SUBLANES == 0) | (lane % SUBLANES == 2)
    jsel = 2 * (lane // SUBLANES) + (lane % SUBLANES) // 2
    score = jnp.where(jsel == jt, 2.0 * SEL_FORCE,
                      jnp.where((jsel == 0) | (jsel == jt - 1), SEL_FORCE,
                                jnp.where(jsel <= jt, imp_s + 0.0, -1.0)))
    score_b = jnp.broadcast_to(score, (R, R))
    key_row = _order_key(score_b)
    key_col = _order_key(score_b.T)
    l0 = lax.broadcasted_iota(I32, (R, R), 0)
    l1 = lax.broadcasted_iota(I32, (R, R), 1)
    cand_col = (l0 % SUBLANES == 0) | (l0 % SUBLANES == 2)
    beats = cand_col & (key_col > jnp.where(l0 < l1, key_row - 1, key_row))
    cnt = jnp.sum(beats.astype(I32), axis=0, keepdims=True)
    sel_row = (cand & (cnt < TOP_N - 1)).astype(F32)
    sel_col = jnp.broadcast_to(sel_row, (R, R)).T
    half = ((l0 % SUBLANES == 0) & (l1 < SEL_LEN)) | ((l0 % SUBLANES == 2) & (l1 >= SEL_LEN))
    z = jnp.where(half, sel_col, 0.0)
    same_page = (l1 // SUBLANES == l0 // SUBLANES).astype(BF16)
    picked = jnp.dot(same_page, z.astype(BF16), preferred_element_type=F32)
    picked = picked.reshape(n_pages, SUBLANES, PG) > 0.5

    for p in range(n_pages):
        ks_t = pg_refs[p][2]
        for h in range(N_HEADS):
            g = p * N_HEADS + h
            part_sc[g * SUBLANES:(g + 1) * SUBLANES, :] = _sublane_group_sum(
                ks_t * qr_cols[h * HEAD_DIM:(h + 1) * HEAD_DIM])
    s = _dot01(_fold_matrix(n_pages), part_sc[...])
    s3 = jnp.where(picked, s.reshape(n_pages, SUBLANES, PG) * SCALE, NEG)
    s_new = _per_head_col(_head_lane_sums(qr_row * rep4(rown[:, 2 * HEAD_DIM:3 * HEAD_DIM]))) * SCALE
    m = jnp.maximum(jnp.max(jnp.max(s3, axis=2, keepdims=True), axis=0), s_new)
    p3 = jnp.exp(s3 - m[None])
    pn = jnp.exp(s_new - m)
    inv = 1.0 / (jnp.sum(jnp.sum(p3, axis=2, keepdims=True), axis=0) + pn)
    o_sel = []
    for h in range(N_HEADS):
        acc = jnp.zeros((HEAD_DIM, PG), F32)
        for p in range(n_pages):
            acc = acc + pg_refs[p][3] * p3[p, h:h + 1, :]
        o_sel.append(jnp.sum(acc, axis=1, keepdims=True) * inv[h:h + 1])

    kw_t = win_ref[0]
    vw_t = win_ref[1]
    wlane = lax.broadcasted_iota(I32, (1, WB), 1)
    wpos = past_len - WB + wlane
    wd = past_len - wpos
    wok = (wd >= 0) & (wd < WINDOW) & (wpos >= 0)
    sw_sc[...] = jnp.zeros_like(sw_sc)
    for h in range(N_HEADS):
        qh = qr_cols[h * HEAD_DIM:(h + 1) * HEAD_DIM]
        sw_sc[h:h + 1, :] = jnp.sum(kw_t * jnp.concatenate([qh] * (WB // LANES), axis=1), axis=0, keepdims=True)
    sw = jnp.where(wok, sw_sc[...] * SCALE, NEG)
    sw_new = _per_head_col(_head_lane_sums(qr_row * rep4(winn[:, 0:HEAD_DIM]))) * SCALE
    mw = jnp.maximum(jnp.max(sw, axis=-1, keepdims=True), sw_new)
    ew = jnp.exp(sw - mw)
    en = jnp.exp(sw_new - mw)
    invw = 1.0 / (jnp.sum(ew, axis=-1, keepdims=True) + en)

    gate = lambda h, c: smn[:, SM_NG + 3 * h + c:SM_NG + 3 * h + c + 1]
    o_cols = []
    for h in range(N_HEADS):
        o_win = jnp.sum(vw_t * ew[h:h + 1, :], axis=1, keepdims=True) * invw[h:h + 1]
        o_cols.append(gate(h, 0) * o_cmp[h] + gate(h, 1) * o_sel[h] + gate(h, 2) * o_win)
    w_sel = _per_head_row([gate(h, 1) * pn[h:h + 1] * inv[h:h + 1] for h in range(N_HEADS)], D_NSA)
    w_win = _per_head_row([gate(h, 2) * en[h:h + 1] * invw[h:h + 1] for h in range(N_HEADS)], D_NSA)
    o_ref[row, :] = (_row_of(jnp.concatenate(o_cols, axis=0))
                     + w_sel * rep4(rown[:, 3 * HEAD_DIM:4 * HEAD_DIM])
                     + w_win * rep4(winn[:, HEAD_DIM:2 * HEAD_DIM]))
    last = lax.broadcasted_iota(I32, (HEAD_DIM, WB), 1) == WB - 1
    winn_cols = _cols_of(winn)
    for s in range(2):
        new_col = winn_cols[s * HEAD_DIM:(s + 1) * HEAD_DIM, 0:1]
        nw_ref[s] = jnp.where(last, new_col, pltpu.roll(win_ref[s], WB - 1, 1))


def _nsa_decode(l, pt_flat, n_pages, past_len, nq, nqr, rows, win, small, nsa_cache_t, cmp_pool, win_state_t,
                prev):
    DB = nq.shape[0]
    PG = nsa_cache_t.shape[-1]
    WB = win_state_t.shape[-1]
    page = lambda p: (lambda b, pt: (l, pt[b * n_pages + p], 0, 0, 0))
    cpage = lambda p: (lambda b, pt: (pt[b * n_pages + p], 0, 0))
    carried = [] if prev is None else [prev]
    rows8 = lambda w: pl.BlockSpec((SUBLANES, w), lambda b, pt: (b // SUBLANES, 0))
    return pl.pallas_call(
        functools.partial(_nsa_decode_kernel, n_pages=n_pages, past_len=past_len),
        grid_spec=pltpu.PrefetchScalarGridSpec(
            num_scalar_prefetch=1,
            grid=(DB,),
            in_specs=[rows8(D_NSA), rows8(D_NSA), rows8(4 * HEAD_DIM), rows8(2 * HEAD_DIM), rows8(LANES),
                      pl.BlockSpec((None, None, 2, HEAD_DIM, WB), lambda b, pt: (l, b, 0, 0, 0))]
            + [pl.BlockSpec((None, None, 4, HEAD_DIM, PG), page(p)) for p in range(n_pages)]
            + [pl.BlockSpec((None, PG // CMP_LEN, LANES), cpage(p)) for p in range(n_pages)]
            + [pl.BlockSpec(memory_space=pl.ANY)] * len(carried),
            out_specs=[rows8(D_NSA),
                       pl.BlockSpec((None, None, 2, HEAD_DIM, WB), lambda b, pt: (l, b, 0, 0, 0))],
            scratch_shapes=[pltpu.VMEM((n_pages * SUBLANES, LANES), F32),
                            pltpu.VMEM((SUBLANES, LANES), F32),
                            pltpu.VMEM((n_pages * N_HEADS * SUBLANES, PG), F32),
                            pltpu.VMEM((SUBLANES, WB), F32)],
        ),
        out_shape=[jax.ShapeDtypeStruct((DB, D_NSA), F32),
                   jax.ShapeDtypeStruct(win_state_t.shape, F32)],
        input_output_aliases={7 + 2 * n_pages: 1} if carried else {},
        compiler_params=_params(("arbitrary",)),
        name="nsa_decode",
    )(pt_flat, nq, nqr, rows, win, small, win_state_t,
      *([nsa_cache_t] * n_pages), *([cmp_pool] * n_pages), *carried)


def _gla_decode_kernel(qk_ref, g_ref, v_ref, s_ref, *rest):
    o_ref, so_ref = rest[-2:]
    for j in range(SUBLANES):
        qk = qk_ref[j:j + 1, :]
        q_cols = _cols_of(qk[:, 0:D_FOX] * SCALE)
        k_cols = _cols_of(qk[:, D_FOX:2 * D_FOX])
        decay = jnp.exp(_cols_of(g_ref[j:j + 1, :]))
        for h in range(N_HEADS):
            hs = slice(h * HEAD_DIM, (h + 1) * HEAD_DIM)
            v_row = v_ref[j:j + 1, h * GLA_DV:(h + 1) * GLA_DV]
            s_new = decay[hs] * s_ref[j, h] + k_cols[hs] * v_row
            so_ref[j, h] = s_new
            o_ref[j:j + 1, h * GLA_DV:(h + 1) * GLA_DV] = jnp.sum(q_cols[hs] * s_new, axis=0, keepdims=True)


def _gla_decode(l, gqk, gv, glog, state, prev):
    depth, DB = state.shape[0:2]
    nb = SUBLANES
    rows = lambda w: pl.BlockSpec((nb, w), lambda i: (i, 0))
    sspec = pl.BlockSpec((None, nb, N_HEADS, HEAD_DIM, GLA_DV), lambda i: (l, i, 0, 0, 0))
    carried = [] if prev is None else [prev]
    return pl.pallas_call(
        _gla_decode_kernel,
        grid=(DB // nb,),
        in_specs=[rows(2 * D_FOX), rows(D_FOX), rows(D_GLA), sspec]
        + [pl.BlockSpec(memory_space=pl.ANY)] * len(carried),
        out_specs=[rows(D_GLA), sspec],
        out_shape=[jax.ShapeDtypeStruct((DB, D_GLA), F32), jax.ShapeDtypeStruct(state.shape, F32)],
        input_output_aliases={4: 1} if carried else {},
        compiler_params=_params(("parallel",)),
        name="gla_decode",
    )(gqk, glog, gv, state, *carried)


def _mix_sample(x, l, p, tab, pt_flat, n_pages, past_len, fox_kv_t, fox_lft_c, nsa_t, cmp_blocks,
                win_state_t, gla_state, carry):
    DB = x.shape[0]
    (fq, fkv, nq, nqr, rows, win, gqk, gv, gog, glog, small) = _in_projection(
        x, 0, DB, DB, p['g'], p['w_r32'], p['sb'], p['wgk32'], p['bgk'], tab, DB)
    o_fox = _fox_decode(l, pt_flat, n_pages, fq, fkv, small, fox_kv_t, fox_lft_c)
    n_pool, PG = nsa_t.shape[1], nsa_t.shape[-1]
    per_layer = n_pool * (PG // CMP_LEN)
    cmp_pool = _compress(cmp_blocks, p['pe_cmp'], p['w_cmp'], Tiles.compress_rows, False,
                         row0=l * per_layer, n_rows=per_layer)
    cmp_pool = cmp_pool.reshape(n_pool, PG // CMP_LEN, LANES)
    o_nsa, new_win = _nsa_decode(l, pt_flat, n_pages, past_len, nq, nqr, rows, win, small,
                                 nsa_t, cmp_pool, win_state_t, carry.get('win'))
    o_gla, g_state = _gla_decode(l, gqk, gv, glog, gla_state, carry.get('gla'))
    carry.update(win=new_win, gla=g_state)
    x_new = _out_projection(x, o_fox, o_nsa, o_gla, gog, p['gn'], p['w_out32'], DB)
    return x_new, dict(fkv=fkv, small=small, rows=rows)


def kernel(x_prompt, x_sample, cache_fox_kv, cache_fox_logf, cache_nsa_kv, state_nsa_win, state_gla,
           page_table, norm_mix_g, w_in, b_fox_f, w_cmp, pe_cmp, w_gla_gk, b_gla_gk, g_gla_norm, w_out,
           norm_ffn_g, dense_w_gate, dense_w_up, dense_w_down, moe_w_router, moe_w_gate, moe_w_up,
           moe_w_down, final_norm_g):
    B, S, _ = x_prompt.shape
    DB, TN, _ = x_sample.shape
    assert TN == 1
    depth, n_pool, PG = cache_fox_kv.shape[0:3]
    n_pages = page_table.shape[1]
    past_len = n_pages * PG
    WB = state_nsa_win.shape[2]
    xp = x_prompt.reshape(B * S, D_MODEL)
    xs = x_sample.reshape(DB, D_MODEL)
    tab_p = _rope_table(jnp.arange(S))
    tab_s = _rope_table(jnp.full((DB,), past_len, I32))
    pt_flat = page_table.reshape(-1).astype(I32)
    fox_kv_t = jnp.transpose(cache_fox_kv, (0, 1, 3, 4, 5, 2))
    fox_lft_c = jnp.swapaxes(cache_fox_logf, 2, 3)
    nsa_t = jnp.transpose(cache_nsa_kv, (0, 1, 3, 4, 2))
    win_state_t = jnp.transpose(state_nsa_win, (0, 1, 3, 4, 2))
    n_cmp = depth * n_pool * (PG // CMP_LEN)
    cmp_blocks = _block_major(cache_nsa_kv[:, :, :, 0:2, :].reshape(n_cmp * CMP_LEN, 2, HEAD_DIM), n_cmp)
    gfin = final_norm_g.reshape(1, D_MODEL)
    cp, cs = [], []
    carry_p, carry_s = {}, {}
    for l in range(depth):
        p = _layer_mix_params(l, norm_mix_g, w_in, b_fox_f, w_cmp, pe_cmp, w_gla_gk, b_gla_gk,
                              g_gla_norm, w_out)
        gf = norm_ffn_g[l].reshape(1, D_MODEL)
        i = l // 2
        moe = l % 2 == 1
        xp, c = _mix_prompt(xp, B, S, l, depth, p, tab_p, Tiles.prompt_rows, Tiles.fox, Tiles.nsa_keys,
                            Tiles.gla_rows, carry_p)
        cp.append(c)
        if moe:
            wr = jnp.zeros((D_MODEL, LANES), F32).at[:, 0:N_EXPERTS].set(moe_w_router[i])
            experts = (moe_w_gate[i], moe_w_up[i], moe_w_down[i])
            disp_p = _moe_dispatch(xp, gf, wr, Tiles.prompt_rows, Tiles.expert_block)
        xs, c = _mix_sample(xs, l, p, tab_s, pt_flat, n_pages, past_len, fox_kv_t, fox_lft_c, nsa_t,
                            cmp_blocks, win_state_t, state_gla, carry_s)
        cs.append(c)
        if moe:
            disp_s = _moe_dispatch(xs, gf, wr, DB, LANES)
            fin = gfin if l == depth - 1 else None
            y_p = _moe_experts(disp_p, *experts, Tiles.expert_block, Tiles.ffn_cols)
            y_s = _moe_experts(disp_s, *experts, LANES, Tiles.ffn_cols_sample)
            xp = _moe_merge(disp_p, *y_p, Tiles.prompt_rows, fin)
            xs = _moe_merge(disp_s, *y_s, DB, fin)
        else:
            wg, wu, wd = (dense_w_gate[i].astype(BF16), dense_w_up[i].astype(BF16),
                          dense_w_down[i].astype(BF16))
            xp = _dense_ffn(xp, gf, wg, wu, wd, Tiles.dense_rows, Tiles.ffn_cols)
            xs = _dense_ffn(xs, gf, dense_w_gate[i], dense_w_up[i], dense_w_down[i], DB, Tiles.ffn_cols_sample)
    if depth % 2 == 1:
        xp = _final_norm(xp, gfin, Tiles.prompt_rows)
        xs = _final_norm(xs, gfin, DB)
    y_p = xp.reshape(B, S, D_MODEL)
    y_s = xs.reshape(DB, 1, D_MODEL)
    wp = min(WINDOW, S)
    st = lambda key, group: jnp.stack([c[key] for c in group])
    return (y_p, y_s,
            carry_p['fkv'].reshape(depth, B, 2, N_HEADS, HEAD_DIM, S).transpose(0, 1, 5, 2, 3, 4),
            st('small', cp).transpose(0, 1, 3, 2),
            carry_p['rows'].reshape(depth, B, 4, HEAD_DIM, S).transpose(0, 1, 4, 2, 3),
            st('win', cp).reshape(depth, B, 2, HEAD_DIM, wp).transpose(0, 1, 4, 2, 3),
            st('g_state', cp),
            st('fkv', cs).reshape(depth, DB, 1, 2, N_HEADS, HEAD_DIM),
            st('small', cs)[:, :, SM_FF:SM_FF + N_HEADS].reshape(depth, DB, 1, N_HEADS),
            st('rows', cs).reshape(depth, DB, 1, 4, HEAD_DIM),
            jnp.transpose(carry_s['win'], (0, 1, 4, 2, 3)),
            carry_s['gla'])
```

```python
import functools

import jax
import jax.numpy as jnp
from jax import lax
from jax.experimental import pallas as pl
from jax.experimental.pallas import tpu as pltpu
from jax.experimental.pallas import tpu_sc as plsc

F32 = jnp.float32
BF16 = jnp.bfloat16
I32 = jnp.int32
HI = lax.Precision.HIGHEST

D_MODEL = 1024
HEAD_DIM = 64
N_HEADS = 4
D_FOX = N_HEADS * HEAD_DIM
D_NSA = N_HEADS * HEAD_DIM
GLA_DV = 128
D_GLA = N_HEADS * GLA_DV
GLA_RANK = 16
GLA_TAU = 16.0
GLA_CHUNK = 64
CMP_LEN = 32
SEL_LEN = 64
TOP_N = 16
WINDOW = 512
ROPE_THETA = 500000.0
ROPE_DIM = HEAD_DIM // 4
ROPE_HALF = ROPE_DIM // 2
D_FF = 3584
N_EXPERTS = 8
EPS = 1e-6
SEL_FORCE = 1e9
NEG = -1e30
SCALE = HEAD_DIM ** -0.5
LOG2E = 1.4426950408889634

LANES = 128
SUBLANES = 8
VMEM_BYTES_V7X = 64 * 1024 * 1024
VMEM_LIMIT = VMEM_BYTES_V7X - 8 * 1024 * 1024


class Tiles:
    prompt_rows = 512
    fox = 512
    nsa_keys = 512
    gla_rows = 256
    cumsum = 512
    compress_rows = 256
    dense_rows = 1024
    ffn_cols = 512
    ffn_cols_sample = 896
    expert_block = 1024

C_FQ = 0
C_FKV = 256
C_NQ = 768
C_NKV = 1024
C_GQK = 1408
C_GV = 1920
C_GOG = 2432
C_SMALL = 2944
C_END = 3072
SM_FF = 0
SM_NG = 4
SM_GLR = 16

NT = (((1,), (1,)), ((), ()))


def _params(sem):
    return pltpu.CompilerParams(dimension_semantics=sem, vmem_limit_bytes=VMEM_LIMIT)


def _rms(x, g):
    ms = jnp.mean(x * x, axis=-1, keepdims=True)
    return x * lax.rsqrt(ms + EPS) * g


def _sigmoid(x):
    return 1.0 / (1.0 + jnp.exp(-x))


def _log_sigmoid(x):
    return -(jnp.maximum(-x, 0.0) + jnp.log1p(jnp.exp(-jnp.abs(x))))


def _silu(x):
    return x * _sigmoid(x)


def _bdot(a, b):
    return jnp.dot(a.astype(BF16), b.astype(BF16), preferred_element_type=F32)


def _split3(x):
    h = x.astype(BF16)
    r = x - h.astype(F32)
    m = r.astype(BF16)
    return h, m, (r - m.astype(F32)).astype(BF16)


def _dot01(m01, x):
    mb = m01.astype(BF16)
    h, m, l = _split3(x)
    return (jnp.dot(mb, h, preferred_element_type=F32) + jnp.dot(mb, m, preferred_element_type=F32)
            + jnp.dot(mb, l, preferred_element_type=F32))


def _dot01_r(x, m01):
    mb = m01.astype(BF16)
    h, m, l = _split3(x)
    return (jnp.dot(h, mb, preferred_element_type=F32) + jnp.dot(m, mb, preferred_element_type=F32)
            + jnp.dot(l, mb, preferred_element_type=F32))


def _dot01_nt(m01, x):
    mb = m01.astype(BF16)
    h, m, l = _split3(x)
    return (lax.dot_general(mb, h, NT, preferred_element_type=F32)
            + lax.dot_general(mb, m, NT, preferred_element_type=F32)
            + lax.dot_general(mb, l, NT, preferred_element_type=F32))


def _dot_nt_hilo(a, b):
    ah = a.astype(BF16)
    al = (a - ah.astype(F32)).astype(BF16)
    bh = b.astype(BF16)
    bl = (b - bh.astype(F32)).astype(BF16)
    return (lax.dot_general(ah, bh, NT, preferred_element_type=F32)
            + lax.dot_general(al, bh, NT, preferred_element_type=F32)
            + lax.dot_general(ah, bl, NT, preferred_element_type=F32))


def _wdot(a, w):
    if w.dtype == F32:
        return jnp.dot(a.astype(F32), w, precision=HI, preferred_element_type=F32)
    return jnp.dot(a.astype(BF16), w, preferred_element_type=F32)


def _rope128(x, a, bp, bm):
    return x * a + pltpu.roll(x, ROPE_HALF, 1) * bp + pltpu.roll(x, LANES - ROPE_HALF, 1) * bm


def _inproj_kernel(x_ref, g_ref, w_ref, sb_ref, wgk_ref, bgk_ref, tab_ref, *refs, feature_major, n_carried):
    (fq_ref, fkv_ref, nq_ref, nqr_ref, rows_ref, win_ref,
     gqk_ref, gv_ref, gog_ref, glog_ref, small_ref, *extra) = refs[n_carried:]
    h = _rms(x_ref[...], g_ref[...]).astype(w_ref.dtype)

    def put(ref, v):
        ref[...] = v.T if feature_major else v

    def mm(a, b):
        return _wdot(h, w_ref[:, a:b])

    fq_ref[...] = mm(C_FQ, C_FKV)
    put(fkv_ref, mm(C_FKV, C_NQ))
    tab = tab_ref[...]
    ab, pb, mb = tab[:, 0:128], tab[:, 128:256], tab[:, 256:384]
    af, pf, mf = tab[:, 384:512], tab[:, 512:640], tab[:, 640:768]
    nq = mm(C_NQ, C_NKV)
    nq_ref[...] = nq
    nqr_ref[:, 0:128] = _rope128(nq[:, 0:128], ab, pb, mb)
    nqr_ref[:, 128:256] = _rope128(nq[:, 128:256], ab, pb, mb)
    nkv = mm(C_NKV, C_GQK)
    put(rows_ref, jnp.concatenate([nkv[:, 0:128], _rope128(nkv[:, 128:256], af, pf, mf)], axis=1))
    put(win_ref, _rope128(nkv[:, 256:384], af, pf, mf))
    gqk_ref[...] = mm(C_GQK, C_GV)
    gv_ref[...] = mm(C_GV, C_GOG)
    gog_ref[...] = mm(C_GOG, C_SMALL)
    sm = mm(C_SMALL, C_END)
    glog_ref[...] = _log_sigmoid(_wdot(sm, wgk_ref[...]) + bgk_ref[...]) * (1.0 / GLA_TAU)
    smb = sm + sb_ref[...]
    lane = lax.broadcasted_iota(I32, smb.shape, 1)
    small = jnp.where(lane < SM_NG, _log_sigmoid(smb), _sigmoid(smb))
    small_ref[...] = small
    if feature_major:
        extra[0][...] = small.T[0:SUBLANES, :]


def _in_projection(x_all, row0, n_rows, tm, g, w_r, sb, wgk, bgk, tab, tab_period, batch=None, layer=None,
                   prev=None):
    assert n_rows % tm == 0 and row0 % tm == 0 and tab_period % tm == 0
    nt = n_rows // tm
    b0 = row0 // tm
    npd = tab_period // tm
    widths = (256, 512, 256, 256, 256, 128, 512, 512, 512, 256, 128)
    FKV, ROWS, WIN = 1, 4, 5
    full = lambda shape: pl.BlockSpec(shape, lambda i: (0, 0))
    row_spec = lambda w: pl.BlockSpec((tm, w), lambda i: (i, 0))
    carried = [] if prev is None else list(prev)
    if batch is not None:
        B, S, depth = batch
        per = S // tm
        assert n_rows == B * S and S % tm == 0
        t_spec = lambda w: pl.BlockSpec((None, w, tm), lambda i: (i // per, 0, i % per))
        l_spec = lambda w: pl.BlockSpec((None, None, w, tm), lambda i: (layer, i // per, 0, i % per))
        out_specs = [l_spec(w) if k in (FKV, ROWS) else t_spec(w) if k == WIN else row_spec(w)
                     for k, w in enumerate(widths)] + [t_spec(SUBLANES)]
        out_shape = [jax.ShapeDtypeStruct((depth, B, w, S) if k in (FKV, ROWS) else (B, w, S) if k == WIN
                                          else (n_rows, w), F32)
                     for k, w in enumerate(widths)] + [jax.ShapeDtypeStruct((B, SUBLANES, S), F32)]
    else:
        out_specs = [row_spec(w) for w in widths]
        out_shape = [jax.ShapeDtypeStruct((n_rows, w), F32) for w in widths]
    return pl.pallas_call(
        functools.partial(_inproj_kernel, feature_major=batch is not None, n_carried=len(carried)),
        grid=(nt,),
        in_specs=[
            pl.BlockSpec((tm, D_MODEL), lambda i: (b0 + i, 0)),
            full((1, D_MODEL)),
            full((D_MODEL, C_END)),
            full((1, LANES)),
            full((LANES, 256)),
            full((1, 256)),
            pl.BlockSpec((tm, 768), lambda i: (i % npd, 0)),
        ] + [pl.BlockSpec(memory_space=pl.ANY)] * len(carried),
        out_specs=out_specs,
        out_shape=out_shape,
        input_output_aliases={7: FKV, 8: ROWS} if carried else {},
        compiler_params=_params(("parallel",)),
        name="in_projection",
    )(x_all, g, w_r, sb, wgk, bgk, tab, *carried)


def _cumsum_kernel(sm_ref, cr_ref, carry):
    t = pl.program_id(1)
    ts = sm_ref.shape[1]

    @pl.when(t == 0)
    def _():
        carry[...] = jnp.zeros_like(carry)

    r = lax.broadcasted_iota(I32, (ts, ts), 0)
    c = lax.broadcasted_iota(I32, (ts, ts), 1)
    cs = _dot01_r(sm_ref[...], r <= c) + carry[...]
    carry[...] = cs[:, ts - 1:ts]
    cr_ref[...] = cs * LOG2E


def _fox_cumsum(small_t, ts):
    B, _, S = small_t.shape
    spec = pl.BlockSpec((None, SUBLANES, ts), lambda b, t: (b, 0, t))
    return pl.pallas_call(
        _cumsum_kernel,
        grid=(B, S // ts),
        in_specs=[spec],
        out_specs=spec,
        out_shape=jax.ShapeDtypeStruct((B, SUBLANES, S), F32),
        scratch_shapes=[pltpu.VMEM((SUBLANES, 1), F32)],
        compiler_params=_params(("parallel", "arbitrary")),
        name="fox_cumsum",
    )(small_t)


def _pair_mask(shape, h, axis=1):
    return (lax.broadcasted_iota(I32, shape, axis) // HEAD_DIM) == (h % 2)


def _fox_prompt_kernel(qi_ref, kj_ref, q_ref, kv_ref, cr_ref, o_ref, *scratch):
    i = qi_ref[pl.program_id(1)]
    j = kj_ref[pl.program_id(1)]
    tq = q_ref.shape[0]
    tk = kv_ref.shape[1]
    q_sc, m_sc, acc_sc = scratch[0:N_HEADS], scratch[N_HEADS:2 * N_HEADS], scratch[2 * N_HEADS:]

    @pl.when(j == 0)
    def _():
        for h in range(N_HEADS):
            m_sc[h][...] = jnp.full_like(m_sc[h], NEG)
            acc_sc[h][...] = jnp.zeros_like(acc_sc[h])
            slab = q_ref[:, (h // 2) * LANES:(h // 2 + 1) * LANES] * (SCALE * LOG2E)
            q_sc[h][...] = jnp.where(_pair_mask(slab.shape, h), slab, 0.0).astype(BF16)

    def tile(diagonal):
        k_slabs = [kv_ref[c0:c0 + LANES, :].astype(BF16) for c0 in (0, LANES)]
        if diagonal:
            mask = lax.broadcasted_iota(I32, (1, tk), 1) <= lax.broadcasted_iota(I32, (tq, 1), 0)
        for h in range(N_HEADS):
            v_slab = kv_ref[D_FOX + (h // 2) * LANES:D_FOX + (h // 2 + 1) * LANES, :]
            v_aug = jnp.where(_pair_mask(v_slab.shape, h, 0), v_slab, 1.0).astype(BF16)
            s = jnp.dot(q_sc[h][...], k_slabs[h // 2], preferred_element_type=F32) - cr_ref[h:h + 1, :]
            if diagonal:
                s = jnp.where(mask, s, NEG)
            m_old = m_sc[h][...]
            m_new = jnp.maximum(m_old, jnp.max(s, axis=-1, keepdims=True))
            p = jnp.exp2(s - m_new).astype(BF16)
            acc_sc[h][...] = (jnp.exp2(m_old - m_new) * acc_sc[h][...]
                              + lax.dot_general(p, v_aug, NT, preferred_element_type=F32))
            m_sc[h][...] = m_new

    @pl.when(j < i)
    def _():
        tile(False)

    @pl.when(j == i)
    def _():
        tile(True)
        for h in range(N_HEADS):
            a = acc_sc[h][...]
            lo = (h % 2) * HEAD_DIM
            den = a[:, HEAD_DIM - lo:HEAD_DIM - lo + 1]
            o_ref[:, h * HEAD_DIM:(h + 1) * HEAD_DIM] = a[:, lo:lo + HEAD_DIM] / den


def _fox_prompt(fq, fkv_all, l, cr, B, S, t):
    n = S // t
    pairs = [(i, j) for i in range(n) for j in range(i + 1)]
    qi = jnp.asarray([p[0] for p in pairs], I32)
    kj = jnp.asarray([p[1] for p in pairs], I32)
    return pl.pallas_call(
        _fox_prompt_kernel,
        grid_spec=pltpu.PrefetchScalarGridSpec(
            num_scalar_prefetch=2,
            grid=(B, len(pairs)),
            in_specs=[
                pl.BlockSpec((t, D_FOX), lambda b, s, qi, kj: (b * n + qi[s], 0)),
                pl.BlockSpec((None, None, 2 * D_FOX, t), lambda b, s, qi, kj: (l, b, 0, kj[s])),
                pl.BlockSpec((None, SUBLANES, t), lambda b, s, qi, kj: (b, 0, kj[s])),
            ],
            out_specs=pl.BlockSpec((t, D_FOX), lambda b, s, qi, kj: (b * n + qi[s], 0)),
            scratch_shapes=([pltpu.VMEM((t, LANES), BF16)] * N_HEADS + [pltpu.VMEM((t, 1), F32)] * N_HEADS
                            + [pltpu.VMEM((t, LANES), F32)] * N_HEADS),
        ),
        out_shape=jax.ShapeDtypeStruct((B * S, D_FOX), F32),
        compiler_params=_params(("parallel", "arbitrary")),
        name="fox_prompt",
    )(qi, kj, fq, fkv_all, cr)


def _compress_kernel(x_ref, pe_ref, w_ref, o_ref, *, exact):
    for s in range(2):
        x = x_ref[s] + pe_ref[s]
        w = w_ref[s]
        if exact:
            y = jnp.dot(x, w, precision=HI, preferred_element_type=F32)
        else:
            xh = x.astype(BF16)
            xl = (x - xh.astype(F32)).astype(BF16)
            wh = w.astype(BF16)
            wl = (w - wh.astype(F32)).astype(BF16)
            y = (jnp.dot(xh, wh, preferred_element_type=F32) + jnp.dot(xl, wh, preferred_element_type=F32)
                 + jnp.dot(xh, wl, preferred_element_type=F32))
        o_ref[:, s * HEAD_DIM:(s + 1) * HEAD_DIM] = y


def _compress(x3, pe, w, tr, exact, row0=0, n_rows=None):
    K = x3.shape[2]
    R = x3.shape[1] if n_rows is None else n_rows
    assert R % tr == 0 and row0 % tr == 0
    b0 = row0 // tr
    return pl.pallas_call(
        functools.partial(_compress_kernel, exact=exact),
        grid=(R // tr,),
        in_specs=[pl.BlockSpec((2, tr, K), lambda i: (0, b0 + i, 0)),
                  pl.BlockSpec((2, 1, K), lambda i: (0, 0, 0)),
                  pl.BlockSpec((2, K, HEAD_DIM), lambda i: (0, 0, 0))],
        out_specs=pl.BlockSpec((tr, LANES), lambda i: (i, 0)),
        out_shape=jax.ShapeDtypeStruct((R, LANES), F32),
        compiler_params=_params(("parallel",)),
        name="nsa_compress",
    )(x3, pe, w)


def _block_major(kv, n_blocks):
    return kv.reshape(n_blocks, CMP_LEN, 2, HEAD_DIM).transpose(2, 0, 1, 3).reshape(
        2, n_blocks, CMP_LEN * HEAD_DIM)


def _order_key(x):
    b = lax.bitcast_convert_type(x, I32)
    return jnp.where(b < 0, b ^ jnp.int32(0x7FFFFFFF), b)


def _nsa_prompt_kernel(nq_ref, nqr_ref, sm_ref, cmp_ref, rows_ref, win_ref, o_ref, qc_sc, qx_sc, *, tk):
    QB = nq_ref.shape[0]
    S = rows_ref.shape[1]
    nb = cmp_ref.shape[0]
    nsel = S // SEL_LEN
    i = pl.program_id(1)
    qs = i * QB
    qpos = qs + lax.broadcasted_iota(I32, (QB, 1), 0)

    HQ = N_HEADS * QB
    lo_half = lax.broadcasted_iota(I32, (QB, LANES), 1) < HEAD_DIM

    def stack_heads(ref, scale, dst):
        for h in range(N_HEADS):
            slab = ref[:, (h // 2) * LANES:(h // 2 + 1) * LANES] * scale
            if h % 2:
                slab = pltpu.roll(slab, HEAD_DIM, 1)
            dst[h * QB:(h + 1) * QB, :] = jnp.where(lo_half, slab, 0.0).astype(dst.dtype)

    cmp = cmp_ref[...]
    n_l = lax.broadcasted_iota(I32, (1, nb), 1)
    complete = ((n_l + 1) * CMP_LEN - 1) <= qpos
    stack_heads(nq_ref, SCALE, qc_sc)
    s = _dot_nt_hilo(qc_sc[...], cmp).reshape(N_HEADS, QB, nb)
    s = jnp.where(complete[None], s, NEG)
    e = jnp.exp(s - jnp.max(s, axis=-1, keepdims=True))
    p = e / jnp.sum(e, axis=-1, keepdims=True) * complete.astype(F32)[None]
    o_cmp = _bdot(p.reshape(HQ, nb), cmp)
    psum = jnp.sum(p, axis=0)

    pj = lax.broadcasted_iota(I32, (nsel, nb), 0)
    pn = lax.broadcasted_iota(I32, (nsel, nb), 1)
    imp_t = _dot01_nt(pn // (SEL_LEN // CMP_LEN) == pj, psum)
    jt = (qs + lax.broadcasted_iota(I32, (1, QB), 1)) // SEL_LEN
    jj = lax.broadcasted_iota(I32, (nsel, 1), 0)
    score = jnp.where(jj == jt, 2.0 * SEL_FORCE,
                      jnp.where((jj == 0) | (jj == jt - 1), SEL_FORCE,
                                jnp.where(jj <= jt, imp_t + 0.0, -1.0)))
    key = _order_key(score)
    key_m1 = key - 1
    ngrp = nsel // SUBLANES
    sub = lax.broadcasted_iota(I32, (SUBLANES, QB), 0)
    kg = [key[r * SUBLANES:(r + 1) * SUBLANES, :] for r in range(ngrp)]
    kg1 = [key_m1[r * SUBLANES:(r + 1) * SUBLANES, :] for r in range(ngrp)]
    cnt = [jnp.zeros((SUBLANES, QB), I32) for _ in range(ngrp)]
    for jp in range(nsel):
        g = jp // SUBLANES
        row = key[jp:jp + 1, :]
        mixed = jnp.where(sub > (jp % SUBLANES), kg1[g], kg[g])
        for r in range(ngrp):
            thr = kg[r] if r < g else (kg1[r] if r > g else mixed)
            cnt[r] = cnt[r] + (row > thr).astype(I32)
    sel_t = jnp.concatenate([(c < TOP_N).astype(F32) for c in cnt], axis=0)
    if nsel < QB:
        sel_t = jnp.concatenate([sel_t, jnp.zeros((QB - nsel, QB), F32)], axis=0)
    sel = sel_t.T.astype(BF16)

    stack_heads(nqr_ref, SCALE * LOG2E, qx_sc)
    qx = qx_sc[...]

    def attend(valid, slab_t, m_old, acc_old):
        n = slab_t.shape[1]
        s = jnp.dot(qx, slab_t.astype(BF16), preferred_element_type=F32)
        s = jnp.where(valid[None], s.reshape(N_HEADS, QB, n), NEG).reshape(HQ, n)
        m_new = jnp.maximum(m_old, jnp.max(s, axis=-1, keepdims=True))
        p = jnp.exp2(s - m_new).astype(BF16)
        ones_k = lax.broadcasted_iota(I32, slab_t.shape, 0) < HEAD_DIM
        v_aug = jnp.where(ones_k, 1.0, slab_t).astype(BF16)
        acc = jnp.exp2(m_old - m_new) * acc_old + lax.dot_general(p, v_aug, NT, preferred_element_type=F32)
        return m_new, acc

    jrow = lax.broadcasted_iota(I32, (QB, 1), 0)

    def sel_tile(k0, m_old, acc_old, diagonal):
        kpos = k0 + lax.broadcasted_iota(I32, (1, tk), 1)
        expand = (jrow == kpos // SEL_LEN).astype(BF16)
        valid = jnp.dot(sel, expand, preferred_element_type=F32) > 0.5
        if diagonal:
            valid = valid & (kpos <= qpos)
        return attend(valid, rows_ref[2 * HEAD_DIM:4 * HEAD_DIM, pl.ds(k0, tk)], m_old, acc_old)

    n_full = qs // tk
    init = (jnp.full((HQ, 1), NEG, F32), jnp.zeros((HQ, LANES), F32))
    m_s, acc_s = lax.fori_loop(
        0, n_full, lambda t, c: sel_tile(pl.multiple_of(t * tk, tk), c[0], c[1], False), init)
    _, acc_s = sel_tile(pl.multiple_of(n_full * tk, tk), m_s, acc_s, True)

    wlen = WINDOW + QB
    w0 = pl.multiple_of(jnp.maximum(qs - WINDOW, 0), QB)
    wpos = w0 + lax.broadcasted_iota(I32, (1, wlen), 1)
    d = qpos - wpos
    _, acc_w = attend((d >= 0) & (d < WINDOW), win_ref[:, pl.ds(w0, wlen)],
                      jnp.full((HQ, 1), NEG, F32), jnp.zeros((HQ, LANES), F32))

    sm = sm_ref[...]
    for h in range(N_HEADS):
        rs = slice(h * QB, (h + 1) * QB)
        o_sel = acc_s[rs] * (1.0 / acc_s[rs, 0:1])
        o_win = acc_w[rs] * (1.0 / acc_w[rs, 0:1])
        c = SM_NG + 3 * h
        mix = sm[:, c:c + 1] * o_cmp[rs] + sm[:, c + 1:c + 2] * o_sel + sm[:, c + 2:c + 3] * o_win
        if h % 2 == 0:
            mix = pltpu.roll(mix, HEAD_DIM, 1)
        lo = (h % 2) * HEAD_DIM
        o_ref[:, h * HEAD_DIM:(h + 1) * HEAD_DIM] = mix[:, lo:lo + HEAD_DIM]


def _nsa_prompt(nq, nqr, small, cmp, rows_all, l, win, B, S, tk):
    QB = 128
    nq_t = S // QB
    nb = S // CMP_LEN
    assert S % tk == 0 and S >= WINDOW + QB
    return pl.pallas_call(
        functools.partial(_nsa_prompt_kernel, tk=tk),
        grid=(B, nq_t),
        in_specs=[
            pl.BlockSpec((QB, D_NSA), lambda b, i: (b * nq_t + i, 0)),
            pl.BlockSpec((QB, D_NSA), lambda b, i: (b * nq_t + i, 0)),
            pl.BlockSpec((QB, LANES), lambda b, i: (b * nq_t + i, 0)),
            pl.BlockSpec((nb, LANES), lambda b, i: (b, 0)),
            pl.BlockSpec((None, None, 4 * HEAD_DIM, S), lambda b, i: (l, b, 0, 0)),
            pl.BlockSpec((None, 2 * HEAD_DIM, S), lambda b, i: (b, 0, 0)),
        ],
        out_specs=pl.BlockSpec((QB, D_NSA), lambda b, i: (b * nq_t + i, 0)),
        out_shape=jax.ShapeDtypeStruct((B * S, D_NSA), F32),
        scratch_shapes=[pltpu.VMEM((N_HEADS * QB, LANES), F32), pltpu.VMEM((N_HEADS * QB, LANES), BF16)],
        compiler_params=_params(("parallel", "parallel")),
        name="nsa_prompt",
    )(nq, nqr, small, cmp, rows_all, win)


def _gla_prompt_kernel(qk_ref, v_ref, g_ref, o_ref, st_ref, s_sc):
    t = pl.program_id(1)
    nt = pl.num_programs(1)
    tc = qk_ref.shape[0]
    C = GLA_CHUNK

    @pl.when(t == 0)
    def _():
        s_sc[...] = jnp.zeros_like(s_sc)

    r = lax.broadcasted_iota(I32, (tc, tc), 0)
    c = lax.broadcasted_iota(I32, (tc, tc), 1)
    same = (r // C) == (c // C)
    causal = same & (c <= r)
    g = g_ref[...]
    gcum = _dot01(causal, g)
    g_t = g.T
    gcum_t = _dot01_r(g_t, same & (r <= c))
    gtot_t = _dot01_r(g_t, same)
    q_e = (qk_ref[:, 0:D_FOX] * SCALE * jnp.exp(gcum)).astype(BF16)
    k_e = (qk_ref[:, D_FOX:2 * D_FOX] * jnp.exp(-gcum)).astype(BF16)
    kd_t = (qk_ref[:, D_FOX:2 * D_FOX].T * jnp.exp(gtot_t - gcum_t)).astype(BF16)
    decay_t = jnp.exp(gtot_t)
    for h in range(N_HEADS):
        hs = slice(h * HEAD_DIM, (h + 1) * HEAD_DIM)
        v = v_ref[:, h * GLA_DV:(h + 1) * GLA_DV].astype(BF16)
        a = jnp.where(causal, lax.dot_general(q_e[:, hs], k_e[:, hs], NT, preferred_element_type=F32), 0.0)
        o_intra = jnp.dot(a.astype(BF16), v, preferred_element_type=F32)
        state = s_sc[h]
        for ci in range(tc // C):
            rs = slice(ci * C, (ci + 1) * C)
            o_ref[rs, h * GLA_DV:(h + 1) * GLA_DV] = (
                o_intra[rs] + jnp.dot(q_e[rs, hs], state.astype(BF16), preferred_element_type=F32))
            state = (decay_t[hs, ci * C:ci * C + 1] * state
                     + jnp.dot(kd_t[hs, rs], v[rs], preferred_element_type=F32))
        s_sc[h] = state

    @pl.when(t == nt - 1)
    def _():
        st_ref[...] = s_sc[...]


def _gla_prompt(gqk, gv, glog, B, S, tc):
    nt = S // tc
    return pl.pallas_call(
        _gla_prompt_kernel,
        grid=(B, nt),
        in_specs=[pl.BlockSpec((tc, 2 * D_FOX), lambda b, t: (b * nt + t, 0)),
                  pl.BlockSpec((tc, D_GLA), lambda b, t: (b * nt + t, 0)),
                  pl.BlockSpec((tc, D_FOX), lambda b, t: (b * nt + t, 0))],
        out_specs=[pl.BlockSpec((tc, D_GLA), lambda b, t: (b * nt + t, 0)),
                   pl.BlockSpec((None, N_HEADS, HEAD_DIM, GLA_DV), lambda b, t: (b, 0, 0, 0))],
        out_shape=[jax.ShapeDtypeStruct((B * S, D_GLA), F32),
                   jax.ShapeDtypeStruct((B, N_HEADS, HEAD_DIM, GLA_DV), F32)],
        scratch_shapes=[pltpu.VMEM((N_HEADS, HEAD_DIM, GLA_DV), F32)],
        compiler_params=_params(("parallel", "arbitrary")),
        name="gla_prompt",
    )(gqk, gv, glog)


def _outproj_kernel(x_ref, of_ref, on_ref, og_ref, gog_ref, gn_ref, w_ref, o_ref):
    acc = _wdot(of_ref[...], w_ref[0:D_FOX, :])
    acc = acc + _wdot(on_ref[...], w_ref[D_FOX:D_FOX + D_NSA, :])
    for h in range(N_HEADS):
        hs = slice(h * GLA_DV, (h + 1) * GLA_DV)
        z = _rms(og_ref[:, hs], gn_ref[...]) * _silu(gog_ref[:, hs])
        w0 = D_FOX + D_NSA + h * GLA_DV
        acc = acc + _wdot(z, w_ref[w0:w0 + GLA_DV, :])
    o_ref[...] = x_ref[...] + acc


def _out_projection(x, o_fox, o_nsa, o_gla, gog, gn, w_out, tm):
    T = x.shape[0]
    assert T % tm == 0
    row = lambda w: pl.BlockSpec((tm, w), lambda i: (i, 0))
    return pl.pallas_call(
        _outproj_kernel,
        grid=(T // tm,),
        in_specs=[row(D_MODEL), row(D_FOX), row(D_NSA), row(D_GLA), row(D_GLA),
                  pl.BlockSpec((1, GLA_DV), lambda i: (0, 0)),
                  pl.BlockSpec((D_MODEL, D_MODEL), lambda i: (0, 0))],
        out_specs=row(D_MODEL),
        out_shape=jax.ShapeDtypeStruct((T, D_MODEL), F32),
        compiler_params=_params(("parallel",)),
        name="out_projection",
    )(x, o_fox, o_nsa, o_gla, gog, gn, w_out)


def _dense_ffn_kernel(x_ref, g_ref, wg_ref, wu_ref, wd_ref, o_ref, h_sc, acc_sc):
    f = pl.program_id(1)
    nf = pl.num_programs(1)

    @pl.when(f == 0)
    def _():
        h_sc[...] = _rms(x_ref[...], g_ref[...]).astype(h_sc.dtype)
        acc_sc[...] = jnp.zeros_like(acc_sc)

    h = h_sc[...]
    a = _wdot(h, wg_ref[...])
    u = _wdot(h, wu_ref[...])
    acc_sc[...] += _wdot(_silu(a) * u, wd_ref[...])

    @pl.when(f == nf - 1)
    def _():
        o_ref[...] = x_ref[...] + acc_sc[...]


def _dense_ffn(x, g, wg, wu, wd, tm, tf):
    T = x.shape[0]
    assert T % tm == 0 and D_FF % tf == 0
    return pl.pallas_call(
        _dense_ffn_kernel,
        grid=(T // tm, D_FF // tf),
        in_specs=[pl.BlockSpec((tm, D_MODEL), lambda i, f: (i, 0)),
                  pl.BlockSpec((1, D_MODEL), lambda i, f: (0, 0)),
                  pl.BlockSpec((D_MODEL, tf), lambda i, f: (0, f)),
                  pl.BlockSpec((D_MODEL, tf), lambda i, f: (0, f)),
                  pl.BlockSpec((tf, D_MODEL), lambda i, f: (f, 0))],
        out_specs=pl.BlockSpec((tm, D_MODEL), lambda i, f: (i, 0)),
        out_shape=jax.ShapeDtypeStruct((T, D_MODEL), F32),
        scratch_shapes=[pltpu.VMEM((tm, D_MODEL), wg.dtype), pltpu.VMEM((tm, D_MODEL), F32)],
        compiler_params=_params(("parallel", "arbitrary")),
        name="dense_ffn",
    )(x, g, wg, wu, wd)


def _top2_route(h, wr):
    logits = jnp.dot(h, wr, precision=HI, preferred_element_type=F32)
    lane = lax.broadcasted_iota(I32, logits.shape, 1)
    lg = jnp.where(lane < N_EXPERTS, logits, -jnp.inf)
    m1 = jnp.max(lg, axis=-1, keepdims=True)
    i1 = jnp.min(jnp.where(lg == m1, lane, LANES), axis=-1, keepdims=True)
    lg2 = jnp.where(lane == i1, -jnp.inf, lg)
    m2 = jnp.max(lg2, axis=-1, keepdims=True)
    i2 = jnp.min(jnp.where(lg2 == m2, lane, LANES), axis=-1, keepdims=True)
    e = jnp.exp(m2 - m1)
    den = 1.0 + e
    return jnp.where(lane == 0, i1.astype(F32),
                     jnp.where(lane == 1, i2.astype(F32),
                               jnp.where(lane == 2, 1.0 / den,
                                         jnp.where(lane == 3, e / den, 0.0))))


def _router_kernel(x_ref, g_ref, wr_ref, h_ref, r_ref):
    h = _rms(x_ref[...], g_ref[...])
    h_ref[...] = h
    r_ref[...] = _top2_route(h, wr_ref[...])


def _moe_small_kernel(x_ref, g_ref, wr_ref, fg_ref, wg_ref, wu_ref, wd_ref, o_ref, h_sc, acc_sc, r_sc, *,
                      final):
    e = pl.program_id(0)
    f = pl.program_id(1)

    @pl.when((e == 0) & (f == 0))
    def _():
        h = _rms(x_ref[...], g_ref[...])
        h_sc[...] = h.astype(BF16)
        r_sc[...] = _top2_route(h, wr_ref[...])
        acc_sc[...] = jnp.zeros_like(acc_sc)

    r = r_sc[...]
    ef = e.astype(F32)
    gate = jnp.where(r[:, 0:1] == ef, r[:, 2:3], 0.0) + jnp.where(r[:, 1:2] == ef, r[:, 3:4], 0.0)
    h = h_sc[...]
    a = jnp.dot(h, wg_ref[...].astype(BF16), preferred_element_type=F32)
    u = jnp.dot(h, wu_ref[...].astype(BF16), preferred_element_type=F32)
    y = _bdot(_silu(a) * u, wd_ref[...])
    acc_sc[...] += jnp.where(gate != 0.0, gate * y, 0.0)

    @pl.when((e == pl.num_programs(0) - 1) & (f == pl.num_programs(1) - 1))
    def _():
        out = x_ref[...] + acc_sc[...]
        o_ref[...] = _rms(out, fg_ref[...]) if final else out


def _moe_small(x, g, wr_pad, wg, wu, wd, tf, final_g):
    T = x.shape[0]
    fg = jnp.ones((1, D_MODEL), F32) if final_g is None else final_g
    const = lambda shape: pl.BlockSpec(shape, lambda e, f: (0, 0))
    return pl.pallas_call(
        functools.partial(_moe_small_kernel, final=final_g is not None),
        grid=(N_EXPERTS, D_FF // tf),
        in_specs=[const((T, D_MODEL)), const((1, D_MODEL)), const((D_MODEL, LANES)), const((1, D_MODEL)),
                  pl.BlockSpec((None, D_MODEL, tf), lambda e, f: (e, 0, f)),
                  pl.BlockSpec((None, D_MODEL, tf), lambda e, f: (e, 0, f)),
                  pl.BlockSpec((None, tf, D_MODEL), lambda e, f: (e, f, 0))],
        out_specs=const((T, D_MODEL)),
        out_shape=jax.ShapeDtypeStruct((T, D_MODEL), F32),
        scratch_shapes=[pltpu.VMEM((T, D_MODEL), BF16), pltpu.VMEM((T, D_MODEL), F32),
                        pltpu.VMEM((T, LANES), F32)],
        compiler_params=_params(("arbitrary", "arbitrary")),
        name="moe_small",
    )(x, g, wr_pad, fg, wg, wu, wd)


def _router(x, g, wr_pad, tm):
    T = x.shape[0]
    assert T % tm == 0
    return pl.pallas_call(
        _router_kernel,
        grid=(T // tm,),
        in_specs=[pl.BlockSpec((tm, D_MODEL), lambda i: (i, 0)),
                  pl.BlockSpec((1, D_MODEL), lambda i: (0, 0)),
                  pl.BlockSpec((D_MODEL, LANES), lambda i: (0, 0))],
        out_specs=[pl.BlockSpec((tm, D_MODEL), lambda i: (i, 0)),
                   pl.BlockSpec((tm, LANES), lambda i: (i, 0))],
        out_shape=[jax.ShapeDtypeStruct((T, D_MODEL), F32),
                   jax.ShapeDtypeStruct((T, LANES), F32)],
        compiler_params=_params(("parallel",)),
        name="moe_router",
    )(x, g, wr_pad)


GATHER_WINDOW = 32
SC_WORKERS = 32


def _row_gather(src, idx):
    n = idx.shape[0]
    step = GATHER_WINDOW * SC_WORKERS
    n_pad = -(-n // step) * step
    if n_pad != n:
        idx = jnp.concatenate([idx, jnp.zeros((n_pad - n,), idx.dtype)])
    width = src.shape[1]
    per_worker = n_pad // SC_WORKERS
    mesh = plsc.VectorSubcoreMesh(core_axis_name="core", subcore_axis_name="subcore")

    @functools.partial(pl.kernel, out_type=jax.ShapeDtypeStruct((n_pad, width), src.dtype), mesh=mesh,
                       scratch_types=[pltpu.VMEM((per_worker,), I32),
                                      pltpu.VMEM((GATHER_WINDOW, width), src.dtype)],
                       name="row_gather")
    def gather(src_hbm, idx_hbm, dst_hbm, idx_v, buf):
        worker = lax.axis_index("core") * (SC_WORKERS // 2) + lax.axis_index("subcore")
        base = worker * per_worker
        pltpu.sync_copy(idx_hbm.at[pl.ds(base, per_worker)], idx_v)

        @pl.loop(0, per_worker // GATHER_WINDOW)
        def _(j):
            pltpu.sync_copy(src_hbm.at[idx_v.at[pl.ds(j * GATHER_WINDOW, GATHER_WINDOW)]], buf)
            pltpu.sync_copy(buf, dst_hbm.at[pl.ds(base + j * GATHER_WINDOW, GATHER_WINDOW)])

    return gather(src, idx)


def _moe_ffn_kernel(be_ref, nu_ref, x_ref, wg_ref, wu_ref, wd_ref, o_ref, acc_sc, x_sc):
    b = pl.program_id(0)
    f = pl.program_id(1)
    nf = pl.num_programs(1)
    used = b < nu_ref[0]

    @pl.when(used)
    def _():
        @pl.when(f == 0)
        def _():
            acc_sc[...] = jnp.zeros_like(acc_sc)
            x_sc[...] = x_ref[...].astype(BF16)

        x = x_sc[...]
        a = jnp.dot(x, wg_ref[...].astype(BF16), preferred_element_type=F32)
        u = jnp.dot(x, wu_ref[...].astype(BF16), preferred_element_type=F32)
        acc_sc[...] += _bdot(_silu(a) * u, wd_ref[...])

        @pl.when(f == nf - 1)
        def _():
            o_ref[...] = acc_sc[...]

    @pl.when(jnp.logical_not(used) & (f == nf - 1))
    def _():
        o_ref[...] = jnp.zeros_like(o_ref)


def _moe_ffn(xb, block_e, n_used, wg, wu, wd, blk, tf):
    cap = xb.shape[0]
    nb = cap // blk
    nf = D_FF // tf

    def bsel(b, nu):
        return jnp.minimum(b, nu[0] - 1)

    def fsel(b, f, nu):
        return jnp.where(b < nu[0], f, nf - 1)

    return pl.pallas_call(
        _moe_ffn_kernel,
        grid_spec=pltpu.PrefetchScalarGridSpec(
            num_scalar_prefetch=2,
            grid=(nb, nf),
            in_specs=[
                pl.BlockSpec((blk, D_MODEL), lambda b, f, be, nu: (bsel(b, nu), 0)),
                pl.BlockSpec((None, D_MODEL, tf), lambda b, f, be, nu: (be[bsel(b, nu)], 0, fsel(b, f, nu))),
                pl.BlockSpec((None, D_MODEL, tf), lambda b, f, be, nu: (be[bsel(b, nu)], 0, fsel(b, f, nu))),
                pl.BlockSpec((None, tf, D_MODEL), lambda b, f, be, nu: (be[bsel(b, nu)], fsel(b, f, nu), 0)),
            ],
            out_specs=pl.BlockSpec((blk, D_MODEL), lambda b, f, be, nu: (b, 0)),
            scratch_shapes=[pltpu.VMEM((blk, D_MODEL), F32), pltpu.VMEM((blk, D_MODEL), BF16)],
        ),
        out_shape=jax.ShapeDtypeStruct((cap, D_MODEL), F32),
        compiler_params=_params(("arbitrary", "arbitrary")),
        name="moe_ffn",
    )(block_e, n_used, xb, wg, wu, wd)


def _moe_combine_kernel(x_ref, y1_ref, y2_ref, r_ref, g_ref, o_ref, *, final):
    r = r_ref[...]
    y = x_ref[...] + (r[:, 2:3] * y1_ref[...] + r[:, 3:4] * y2_ref[...])
    o_ref[...] = _rms(y, g_ref[...]) if final else y


def _moe_combine(x, y1, y2, route, tm, final_g):
    T = x.shape[0]
    g = jnp.ones((1, D_MODEL), F32) if final_g is None else final_g
    return pl.pallas_call(
        functools.partial(_moe_combine_kernel, final=final_g is not None),
        grid=(T // tm,),
        in_specs=[pl.BlockSpec((tm, D_MODEL), lambda i: (i, 0)),
                  pl.BlockSpec((tm, D_MODEL), lambda i: (i, 0)),
                  pl.BlockSpec((tm, D_MODEL), lambda i: (i, 0)),
                  pl.BlockSpec((tm, LANES), lambda i: (i, 0)),
                  pl.BlockSpec((1, D_MODEL), lambda i: (0, 0))],
        out_specs=pl.BlockSpec((tm, D_MODEL), lambda i: (i, 0)),
        out_shape=jax.ShapeDtypeStruct((T, D_MODEL), F32),
        compiler_params=_params(("parallel",)),
        name="moe_combine",
    )(x, y1, y2, route, g)


def _moe_plan(e_top, blk):
    T = e_top.shape[0]
    n = 2 * T
    flat_e = e_top.reshape(-1)
    onehot = (flat_e[:, None] == jnp.arange(N_EXPERTS, dtype=I32)[None, :]).astype(I32)
    csum = jnp.cumsum(onehot, axis=0)
    rank = jnp.sum((csum - onehot) * onehot, axis=1)
    counts = csum[-1]
    padded = (counts + blk - 1) // blk * blk
    ends = jnp.cumsum(padded)
    pstart = ends - padded
    dest = (pstart[flat_e] + rank).astype(I32)
    n_blocks = -(-n // blk) + N_EXPERTS
    cap = n_blocks * blk
    slot_tok = (jnp.arange(cap, dtype=I32) % T).at[dest].set(jnp.arange(n, dtype=I32) // 2)
    first = jnp.arange(n_blocks, dtype=I32) * blk
    block_e = jnp.minimum(jnp.sum((ends[None, :] <= first[:, None]).astype(I32), axis=1), N_EXPERTS - 1)
    n_used = (ends[-1] // blk).astype(I32).reshape(1)
    return dest, slot_tok, block_e, n_used


def _moe_dispatch(x, g, wr_pad, tm, blk):
    h, route = _router(x, g, wr_pad, tm)
    dest, slot_tok, block_e, n_used = _moe_plan(route[:, 0:2].astype(I32), blk)
    return dict(x=x, route=route, dest=dest, xb=_row_gather(h, slot_tok), block_e=block_e, n_used=n_used)


def _moe_experts(d, wg, wu, wd, blk, tf):
    yb = _moe_ffn(d['xb'], d['block_e'], d['n_used'], wg, wu, wd, blk, tf)
    d2 = d['dest'].reshape(-1, 2)
    return _row_gather(yb, d2[:, 0]), _row_gather(yb, d2[:, 1])


def _moe_merge(d, y1, y2, tm, final_g):
    return _moe_combine(d['x'], y1, y2, d['route'], tm, final_g)


def _norm_kernel(x_ref, g_ref, o_ref):
    o_ref[...] = _rms(x_ref[...], g_ref[...])


def _final_norm(x, g, tm):
    T = x.shape[0]
    return pl.pallas_call(
        _norm_kernel,
        grid=(T // tm,),
        in_specs=[pl.BlockSpec((tm, D_MODEL), lambda i: (i, 0)),
                  pl.BlockSpec((1, D_MODEL), lambda i: (0, 0))],
        out_specs=pl.BlockSpec((tm, D_MODEL), lambda i: (i, 0)),
        out_shape=jax.ShapeDtypeStruct((T, D_MODEL), F32),
        compiler_params=_params(("parallel",)),
        name="final_norm",
    )(x, g)


_IN_SPLITS = (D_FOX, D_FOX, D_FOX, N_HEADS, D_NSA, 6 * HEAD_DIM, 3 * N_HEADS,
              D_FOX, D_FOX, D_GLA, GLA_RANK, D_GLA)


def _reorder_w_in(w):
    offs = [0]
    for s in _IN_SPLITS:
        offs.append(offs[-1] + s)
    seg = lambda k: w[:, offs[k]:offs[k + 1]]
    fq, fk, fv, ff, nq, nkv, ng, gq, gk, gv, glr, gog = [seg(k) for k in range(12)]
    pad = jnp.zeros((w.shape[0], LANES - SM_GLR - GLA_RANK), w.dtype)
    return jnp.concatenate([fq, fk, fv, nq, nkv, gq, gk, gv, gog, ff, ng, glr, pad], axis=1)


def _rope_table(pos):
    inv = ROPE_THETA ** (-jnp.arange(ROPE_HALF, dtype=F32) / ROPE_HALF)
    ang = pos.astype(F32)[:, None] * inv[None, :]
    cos, sin = jnp.cos(ang), jnp.sin(ang)
    P = pos.shape[0]
    one = jnp.ones((P, HEAD_DIM - ROPE_DIM), F32)
    zero = jnp.zeros((P, HEAD_DIM - ROPE_DIM), F32)
    z8 = jnp.zeros((P, ROPE_HALF), F32)
    a64 = jnp.concatenate([cos, cos, one], axis=1)
    p64 = jnp.concatenate([z8, sin, zero], axis=1)
    m64 = jnp.concatenate([-sin, z8, zero], axis=1)
    i64 = jnp.ones((P, HEAD_DIM), F32)
    o64 = jnp.zeros((P, HEAD_DIM), F32)
    return jnp.concatenate([a64, a64, p64, p64, m64, m64, a64, i64, p64, o64, m64, o64], axis=1)


def _layer_mix_params(l, norm_mix_g, w_in, b_fox_f, w_cmp, pe_cmp, w_gla_gk, b_gla_gk, g_gla_norm, w_out):
    sb = jnp.zeros((1, LANES), F32).at[0, SM_FF:SM_FF + N_HEADS].set(b_fox_f[l])
    wgk = jnp.zeros((LANES, D_FOX), F32).at[SM_GLR:SM_GLR + GLA_RANK].set(w_gla_gk[l])
    w_r = _reorder_w_in(w_in[l])
    return dict(g=norm_mix_g[l].reshape(1, D_MODEL), w_r=w_r.astype(BF16), w_r32=w_r, sb=sb,
                wgk=wgk.astype(BF16), wgk32=wgk, w_out32=w_out[l],
                bgk=b_gla_gk[l].reshape(1, D_FOX), w_cmp=w_cmp[l],
                pe_cmp=pe_cmp[l].reshape(2, 1, CMP_LEN * HEAD_DIM),
                gn=g_gla_norm[l].reshape(1, GLA_DV), w_out=w_out[l].astype(BF16))


def _mix_prompt(x, B, S, l, depth, p, tab, tm, t_fox, tk_sel, tc_gla, carry):
    (fq, fkv_all, nq, nqr, rows_all, win_t, gqk, gv, gog, glog, small, small_t) = _in_projection(
        x, 0, B * S, tm, p['g'], p['w_r'], p['sb'], p['wgk'], p['bgk'], tab, S, batch=(B, S, depth), layer=l,
        prev=(carry['fkv'], carry['rows']) if carry else None)
    carry.update(fkv=fkv_all, rows=rows_all)
    cr = _fox_cumsum(small_t, min(S, Tiles.cumsum))
    o_fox = _fox_prompt(fq, fkv_all, l, cr, B, S, t_fox)
    n_blk = S // CMP_LEN
    blocks = rows_all[l, :, 0:2 * HEAD_DIM, :].reshape(B, 2, HEAD_DIM, n_blk, CMP_LEN).transpose(1, 0, 3, 4, 2)
    blocks = blocks.reshape(2, B * n_blk, CMP_LEN * HEAD_DIM)
    cmp = _compress(blocks, p['pe_cmp'], p['w_cmp'], min(Tiles.compress_rows, B * n_blk), True)
    o_nsa = _nsa_prompt(nq, nqr, small, cmp, rows_all, l, win_t, B, S, tk_sel)
    o_gla, g_state = _gla_prompt(gqk, gv, glog, B, S, tc_gla)
    x_new = _out_projection(x, o_fox, o_nsa, o_gla, gog, p['gn'], p['w_out'], tm)
    wp = min(WINDOW, S)
    return x_new, dict(small=small_t[:, 0:N_HEADS, :], win=win_t[:, :, S - wp:], g_state=g_state)


def _per_head_col(vals):
    r = lax.broadcasted_iota(I32, (SUBLANES, 1), 0)
    out = jnp.zeros((SUBLANES, 1), F32)
    for h, v in enumerate(vals):
        out = out + jnp.where(r == h, v, 0.0)
    return out


def _per_head_row(vals, width):
    grp = lax.broadcasted_iota(I32, (1, width), 1) // HEAD_DIM
    out = jnp.zeros((1, width), F32)
    for h, v in enumerate(vals):
        out = out + jnp.where(grp == h, v, 0.0)
    return out


def _head_lane_sums(row):
    grp = lax.broadcasted_iota(I32, row.shape, 1) // HEAD_DIM
    return [jnp.sum(jnp.where(grp == h, row, 0.0), axis=1, keepdims=True) for h in range(N_HEADS)]


def _cols_of(row):
    return jnp.concatenate([jnp.broadcast_to(row[:, j:j + LANES], (LANES, LANES)).T
                            for j in range(0, row.shape[1], LANES)], axis=0)


def _row_of(col):
    return jnp.concatenate([jnp.broadcast_to(col[j:j + LANES], (LANES, LANES)).T[0:1, :]
                            for j in range(0, col.shape[0], LANES)], axis=1)


def _sublane_group_sum(x):
    return jnp.sum(x.reshape(x.shape[0] // SUBLANES, SUBLANES, x.shape[1]), axis=0)


def _fold_matrix(n_pages):
    r = lax.broadcasted_iota(I32, (n_pages * SUBLANES, n_pages * N_HEADS * SUBLANES), 0)
    c = lax.broadcasted_iota(I32, (n_pages * SUBLANES, n_pages * N_HEADS * SUBLANES), 1)
    blk = c // SUBLANES
    return ((blk // N_HEADS == r // SUBLANES) & (blk % N_HEADS == r % SUBLANES)).astype(F32)


def _fox_decode_kernel(pt_ref, q_ref, kvn_ref, smn_ref, *refs, n_pages):
    del pt_ref
    kv_refs = refs[0:n_pages]
    lf_refs = refs[n_pages:2 * n_pages]
    o_ref, lf_sc, part_sc = refs[2 * n_pages:]
    R = n_pages * SUBLANES
    PG = kv_refs[0].shape[-1]
    row = pl.ds(pl.program_id(0) % SUBLANES, 1)
    q_row = q_ref[row, :]
    kvn = kvn_ref[row, :]
    smn = smn_ref[row, :]
    q_cols = _cols_of(q_row)

    lf_sc[...] = jnp.zeros_like(lf_sc)
    for p in range(n_pages):
        lf_sc[p * SUBLANES:p * SUBLANES + N_HEADS, :] = lf_refs[p][...]
    lft = lf_sc[...]
    k0 = lax.broadcasted_iota(I32, (PG, PG), 0)
    k1 = lax.broadcasted_iota(I32, (PG, PG), 1)
    within = _dot01_r(lft, k0 > k1)
    tot = jnp.broadcast_to(jnp.sum(lft, axis=1, keepdims=True), (R, PG))
    r0 = lax.broadcasted_iota(I32, (R, R), 0)
    r1 = lax.broadcasted_iota(I32, (R, R), 1)
    later = (r1 % SUBLANES == r0 % SUBLANES) & (r1 // SUBLANES > r0 // SUBLANES)
    cross = _dot01(later, tot)
    rr = lax.broadcasted_iota(I32, (R, 1), 0) % SUBLANES
    newcol = jnp.zeros((R, 1), F32)
    for h in range(N_HEADS):
        newcol = newcol + jnp.where(rr == h, smn[:, SM_FF + h:SM_FF + h + 1], 0.0)
    bias = (within + cross + newcol).reshape(n_pages, SUBLANES, PG)

    for p in range(n_pages):
        for h in range(N_HEADS):
            g = p * N_HEADS + h
            part_sc[g * SUBLANES:(g + 1) * SUBLANES, :] = _sublane_group_sum(
                kv_refs[p][0, h] * q_cols[h * HEAD_DIM:(h + 1) * HEAD_DIM])
    s = _dot01(_fold_matrix(n_pages), part_sc[...])
    s3 = s.reshape(n_pages, SUBLANES, PG) * SCALE + bias
    s_new = _per_head_col(_head_lane_sums(q_row * kvn[:, 0:D_FOX])) * SCALE
    m = jnp.max(jnp.max(s3, axis=2, keepdims=True), axis=0)
    m = jnp.maximum(m, s_new)
    p3 = jnp.exp(s3 - m[None])
    pn = jnp.exp(s_new - m)
    inv = 1.0 / (jnp.sum(jnp.sum(p3, axis=2, keepdims=True), axis=0) + pn)
    o_cols = []
    for h in range(N_HEADS):
        acc = jnp.zeros((HEAD_DIM, PG), F32)
        for p in range(n_pages):
            acc = acc + kv_refs[p][1, h] * p3[p, h:h + 1, :]
        o_cols.append(jnp.sum(acc, axis=1, keepdims=True) * inv[h:h + 1])
    w_new = _per_head_row([pn[h:h + 1] * inv[h:h + 1] for h in range(N_HEADS)], D_FOX)
    o_ref[row, :] = _row_of(jnp.concatenate(o_cols, axis=0)) + w_new * kvn[:, D_FOX:2 * D_FOX]


def _fox_decode(l, pt_flat, n_pages, fq, fkv, small, kv_cache_t, lft_cache):
    DB = fq.shape[0]
    PG = kv_cache_t.shape[-1]
    page = lambda p, nz: (lambda b, pt: (l, pt[b * n_pages + p]) + (0,) * nz)
    rows8 = lambda w: pl.BlockSpec((SUBLANES, w), lambda b, pt: (b // SUBLANES, 0))
    return pl.pallas_call(
        functools.partial(_fox_decode_kernel, n_pages=n_pages),
        grid_spec=pltpu.PrefetchScalarGridSpec(
            num_scalar_prefetch=1,
            grid=(DB,),
            in_specs=[rows8(D_FOX), rows8(2 * D_FOX), rows8(LANES)]
            + [pl.BlockSpec((None, None, 2, N_HEADS, HEAD_DIM, PG), page(p, 4)) for p in range(n_pages)]
            + [pl.BlockSpec((None, None, N_HEADS, PG), page(p, 2)) for p in range(n_pages)],
            out_specs=rows8(D_FOX),
            scratch_shapes=[pltpu.VMEM((n_pages * SUBLANES, PG), F32),
                            pltpu.VMEM((n_pages * N_HEADS * SUBLANES, PG), F32)],
        ),
        out_shape=jax.ShapeDtypeStruct((DB, D_FOX), F32),
        compiler_params=_params(("arbitrary",)),
        name="fox_decode",
    )(pt_flat, fq, fkv, small, *([kv_cache_t] * n_pages), *([lft_cache] * n_pages))


def _nsa_decode_kernel(pt_ref, q_ref, qr_ref, rown_ref, winn_ref, sm_ref, win_ref, *refs,
                       n_pages, past_len):
    del pt_ref
    pg_refs = refs[0:n_pages]
    cmp_refs = refs[n_pages:2 * n_pages]
    o_ref, nw_ref, cmp_sc, qc_sc, part_sc, sw_sc = refs[-6:]
    R = n_pages * SUBLANES
    PG = pg_refs[0].shape[-1]
    WB = win_ref.shape[-1]
    per_page = PG // CMP_LEN
    assert per_page <= SUBLANES and PG == 2 * SEL_LEN and R == LANES
    jt = past_len // SEL_LEN
    row = pl.ds(pl.program_id(0) % SUBLANES, 1)
    q_row = q_ref[row, :]
    qr_row = qr_ref[row, :]
    rown = rown_ref[row, :]
    winn = winn_ref[row, :]
    smn = sm_ref[row, :]
    qr_cols = _cols_of(qr_row)
    rep4 = lambda r64: jnp.concatenate([r64] * N_HEADS, axis=1)

    qc_sc[...] = jnp.zeros_like(qc_sc)
    for h in range(N_HEADS):
        qc_sc[h:h + 1, 0:HEAD_DIM] = q_row[:, h * HEAD_DIM:(h + 1) * HEAD_DIM]
    head_row = lax.broadcasted_iota(I32, (SUBLANES, 1), 0) < N_HEADS

    cmp_sc[...] = jnp.zeros_like(cmp_sc)
    for p in range(n_pages):
        cmp_sc[p * SUBLANES:p * SUBLANES + per_page, :] = cmp_refs[p][...]
    cmpa = cmp_sc[...]
    lane = lax.broadcasted_iota(I32, (1, R), 1)
    blk = per_page * (lane // SUBLANES) + lane % SUBLANES
    complete = (lane % SUBLANES < per_page) & ((blk + 1) * CMP_LEN - 1 <= past_len)
    s = _dot_nt_hilo(qc_sc[...], cmpa) * SCALE
    s = jnp.where(complete, s, NEG)
    e = jnp.exp(s - jnp.max(s, axis=-1, keepdims=True))
    pc = e / jnp.sum(e, axis=-1, keepdims=True) * complete.astype(F32)
    vcb_t = cmpa.T[HEAD_DIM:2 * HEAD_DIM, :]
    o_cmp = [jnp.sum(vcb_t * pc[h:h + 1, :], axis=1, keepdims=True) for h in range(N_HEADS)]

    imp_c = jnp.sum(jnp.where(head_row, pc, 0.0), axis=0, keepdims=True)
    imp_s = imp_c + pltpu.roll(imp_c, R - 1, 1)
    cand = (lane % SUBLANES == 0) | (lane % SUBLANES == 2)
    jsel = 2 * (lane // SUBLANES) + (lane % SUBLANES) // 2
    score = jnp.where(jsel == jt, 2.0 * SEL_FORCE,
                      jnp.where((jsel == 0) | (jsel == jt - 1), SEL_FORCE,
                                jnp.where(jsel <= jt, imp_s + 0.0, -1.0)))
    score_b = jnp.broadcast_to(score, (R, R))
    key_row = _order_key(score_b)
    key_col = _order_key(score_b.T)
    l0 = lax.broadcasted_iota(I32, (R, R), 0)
    l1 = lax.broadcasted_iota(I32, (R, R), 1)
    cand_col = (l0 % SUBLANES == 0) | (l0 % SUBLANES == 2)
    beats = cand_col & (key_col > jnp.where(l0 < l1, key_row - 1, key_row))
    cnt = jnp.sum(beats.astype(I32), axis=0, keepdims=True)
    sel_row = (cand & (cnt < TOP_N - 1)).astype(F32)
    sel_col = jnp.broadcast_to(sel_row, (R, R)).T
    half = ((l0 % SUBLANES == 0) & (l1 < SEL_LEN)) | ((l0 % SUBLANES == 2) & (l1 >= SEL_LEN))
    z = jnp.where(half, sel_col, 0.0)
    same_page = (l1 // SUBLANES == l0 // SUBLANES).astype(BF16)
    picked = jnp.dot(same_page, z.astype(BF16), preferred_element_type=F32)
    picked = picked.reshape(n_pages, SUBLANES, PG) > 0.5

    for p in range(n_pages):
        ks_t = pg_refs[p][2]
        for h in range(N_HEADS):
            g = p * N_HEADS + h
            part_sc[g * SUBLANES:(g + 1) * SUBLANES, :] = _sublane_group_sum(
                ks_t * qr_cols[h * HEAD_DIM:(h + 1) * HEAD_DIM])
    s = _dot01(_fold_matrix(n_pages), part_sc[...])
    s3 = jnp.where(picked, s.reshape(n_pages, SUBLANES, PG) * SCALE, NEG)
    s_new = _per_head_col(_head_lane_sums(qr_row * rep4(rown[:, 2 * HEAD_DIM:3 * HEAD_DIM]))) * SCALE
    m = jnp.maximum(jnp.max(jnp.max(s3, axis=2, keepdims=True), axis=0), s_new)
    p3 = jnp.exp(s3 - m[None])
    pn = jnp.exp(s_new - m)
    inv = 1.0 / (jnp.sum(jnp.sum(p3, axis=2, keepdims=True), axis=0) + pn)
    o_sel = []
    for h in range(N_HEADS):
        acc = jnp.zeros((HEAD_DIM, PG), F32)
        for p in range(n_pages):
            acc = acc + pg_refs[p][3] * p3[p, h:h + 1, :]
        o_sel.append(jnp.sum(acc, axis=1, keepdims=True) * inv[h:h + 1])

    kw_t = win_ref[0]
    vw_t = win_ref[1]
    wlane = lax.broadcasted_iota(I32, (1, WB), 1)
    wpos = past_len - WB + wlane
    wd = past_len - wpos
    wok = (wd >= 0) & (wd < WINDOW) & (wpos >= 0)
    sw_sc[...] = jnp.zeros_like(sw_sc)
    for h in range(N_HEADS):
        qh = qr_cols[h * HEAD_DIM:(h + 1) * HEAD_DIM]
        sw_sc[h:h + 1, :] = jnp.sum(kw_t * jnp.concatenate([qh] * (WB // LANES), axis=1), axis=0, keepdims=True)
    sw = jnp.where(wok, sw_sc[...] * SCALE, NEG)
    sw_new = _per_head_col(_head_lane_sums(qr_row * rep4(winn[:, 0:HEAD_DIM]))) * SCALE
    mw = jnp.maximum(jnp.max(sw, axis=-1, keepdims=True), sw_new)
    ew = jnp.exp(sw - mw)
    en = jnp.exp(sw_new - mw)
    invw = 1.0 / (jnp.sum(ew, axis=-1, keepdims=True) + en)

    gate = lambda h, c: smn[:, SM_NG + 3 * h + c:SM_NG + 3 * h + c + 1]
    o_cols = []
    for h in range(N_HEADS):
        o_win = jnp.sum(vw_t * ew[h:h + 1, :], axis=1, keepdims=True) * invw[h:h + 1]
        o_cols.append(gate(h, 0) * o_cmp[h] + gate(h, 1) * o_sel[h] + gate(h, 2) * o_win)
    w_sel = _per_head_row([gate(h, 1) * pn[h:h + 1] * inv[h:h + 1] for h in range(N_HEADS)], D_NSA)
    w_win = _per_head_row([gate(h, 2) * en[h:h + 1] * invw[h:h + 1] for h in range(N_HEADS)], D_NSA)
    o_ref[row, :] = (_row_of(jnp.concatenate(o_cols, axis=0))
                     + w_sel * rep4(rown[:, 3 * HEAD_DIM:4 * HEAD_DIM])
                     + w_win * rep4(winn[:, HEAD_DIM:2 * HEAD_DIM]))
    last = lax.broadcasted_iota(I32, (HEAD_DIM, WB), 1) == WB - 1
    winn_cols = _cols_of(winn)
    for s in range(2):
        new_col = winn_cols[s * HEAD_DIM:(s + 1) * HEAD_DIM, 0:1]
        nw_ref[s] = jnp.where(last, new_col, pltpu.roll(win_ref[s], WB - 1, 1))


def _nsa_decode(l, pt_flat, n_pages, past_len, nq, nqr, rows, win, small, nsa_cache_t, cmp_pool, win_state_t,
                prev):
    DB = nq.shape[0]
    PG = nsa_cache_t.shape[-1]
    WB = win_state_t.shape[-1]
    page = lambda p: (lambda b, pt: (l, pt[b * n_pages + p], 0, 0, 0))
    cpage = lambda p: (lambda b, pt: (pt[b * n_pages + p], 0, 0))
    carried = [] if prev is None else [prev]
    rows8 = lambda w: pl.BlockSpec((SUBLANES, w), lambda b, pt: (b // SUBLANES, 0))
    return pl.pallas_call(
        functools.partial(_nsa_decode_kernel, n_pages=n_pages, past_len=past_len),
        grid_spec=pltpu.PrefetchScalarGridSpec(
            num_scalar_prefetch=1,
            grid=(DB,),
            in_specs=[rows8(D_NSA), rows8(D_NSA), rows8(4 * HEAD_DIM), rows8(2 * HEAD_DIM), rows8(LANES),
                      pl.BlockSpec((None, None, 2, HEAD_DIM, WB), lambda b, pt: (l, b, 0, 0, 0))]
            + [pl.BlockSpec((None, None, 4, HEAD_DIM, PG), page(p)) for p in range(n_pages)]
            + [pl.BlockSpec((None, PG // CMP_LEN, LANES), cpage(p)) for p in range(n_pages)]
            + [pl.BlockSpec(memory_space=pl.ANY)] * len(carried),
            out_specs=[rows8(D_NSA),
                       pl.BlockSpec((None, None, 2, HEAD_DIM, WB), lambda b, pt: (l, b, 0, 0, 0))],
            scratch_shapes=[pltpu.VMEM((n_pages * SUBLANES, LANES), F32),
                            pltpu.VMEM((SUBLANES, LANES), F32),
                            pltpu.VMEM((n_pages * N_HEADS * SUBLANES, PG), F32),
                            pltpu.VMEM((SUBLANES, WB), F32)],
        ),
        out_shape=[jax.ShapeDtypeStruct((DB, D_NSA), F32),
                   jax.ShapeDtypeStruct(win_state_t.shape, F32)],
        input_output_aliases={7 + 2 * n_pages: 1} if carried else {},
        compiler_params=_params(("arbitrary",)),
        name="nsa_decode",
    )(pt_flat, nq, nqr, rows, win, small, win_state_t,
      *([nsa_cache_t] * n_pages), *([cmp_pool] * n_pages), *carried)


def _gla_decode_kernel(qk_ref, g_ref, v_ref, s_ref, *rest):
    o_ref, so_ref = rest[-2:]
    for j in range(SUBLANES):
        qk = qk_ref[j:j + 1, :]
        q_cols = _cols_of(qk[:, 0:D_FOX] * SCALE)
        k_cols = _cols_of(qk[:, D_FOX:2 * D_FOX])
        decay = jnp.exp(_cols_of(g_ref[j:j + 1, :]))
        for h in range(N_HEADS):
            hs = slice(h * HEAD_DIM, (h + 1) * HEAD_DIM)
            v_row = v_ref[j:j + 1, h * GLA_DV:(h + 1) * GLA_DV]
            s_new = decay[hs] * s_ref[j, h] + k_cols[hs] * v_row
            so_ref[j, h] = s_new
            o_ref[j:j + 1, h * GLA_DV:(h + 1) * GLA_DV] = jnp.sum(q_cols[hs] * s_new, axis=0, keepdims=True)


def _gla_decode(l, gqk, gv, glog, state, prev):
    depth, DB = state.shape[0:2]
    nb = SUBLANES
    rows = lambda w: pl.BlockSpec((nb, w), lambda i: (i, 0))
    sspec = pl.BlockSpec((None, nb, N_HEADS, HEAD_DIM, GLA_DV), lambda i: (l, i, 0, 0, 0))
    carried = [] if prev is None else [prev]
    return pl.pallas_call(
        _gla_decode_kernel,
        grid=(DB // nb,),
        in_specs=[rows(2 * D_FOX), rows(D_FOX), rows(D_GLA), sspec]
        + [pl.BlockSpec(memory_space=pl.ANY)] * len(carried),
        out_specs=[rows(D_GLA), sspec],
        out_shape=[jax.ShapeDtypeStruct((DB, D_GLA), F32), jax.ShapeDtypeStruct(state.shape, F32)],
        input_output_aliases={4: 1} if carried else {},
        compiler_params=_params(("parallel",)),
        name="gla_decode",
    )(gqk, glog, gv, state, *carried)


def _mix_sample(x, l, p, tab, pt_flat, n_pages, past_len, fox_kv_t, fox_lft_c, nsa_t, cmp_blocks,
                win_state_t, gla_state, carry):
    DB = x.shape[0]
    (fq, fkv, nq, nqr, rows, win, gqk, gv, gog, glog, small) = _in_projection(
        x, 0, DB, DB, p['g'], p['w_r32'], p['sb'], p['wgk32'], p['bgk'], tab, DB)
    o_fox = _fox_decode(l, pt_flat, n_pages, fq, fkv, small, fox_kv_t, fox_lft_c)
    n_pool, PG = nsa_t.shape[1], nsa_t.shape[-1]
    per_layer = n_pool * (PG // CMP_LEN)
    cmp_pool = _compress(cmp_blocks, p['pe_cmp'], p['w_cmp'], Tiles.compress_rows, False,
                         row0=l * per_layer, n_rows=per_layer)
    cmp_pool = cmp_pool.reshape(n_pool, PG // CMP_LEN, LANES)
    o_nsa, new_win = _nsa_decode(l, pt_flat, n_pages, past_len, nq, nqr, rows, win, small,
                                 nsa_t, cmp_pool, win_state_t, carry.get('win'))
    o_gla, g_state = _gla_decode(l, gqk, gv, glog, gla_state, carry.get('gla'))
    carry.update(win=new_win, gla=g_state)
    x_new = _out_projection(x, o_fox, o_nsa, o_gla, gog, p['gn'], p['w_out32'], DB)
    return x_new, dict(fkv=fkv, small=small, rows=rows)


def kernel(x_prompt, x_sample, cache_fox_kv, cache_fox_logf, cache_nsa_kv, state_nsa_win, state_gla,
           page_table, norm_mix_g, w_in, b_fox_f, w_cmp, pe_cmp, w_gla_gk, b_gla_gk, g_gla_norm, w_out,
           norm_ffn_g, dense_w_gate, dense_w_up, dense_w_down, moe_w_router, moe_w_gate, moe_w_up,
           moe_w_down, final_norm_g):
    B, S, _ = x_prompt.shape
    DB, TN, _ = x_sample.shape
    assert TN == 1
    depth, n_pool, PG = cache_fox_kv.shape[0:3]
    n_pages = page_table.shape[1]
    past_len = n_pages * PG
    WB = state_nsa_win.shape[2]
    xp = x_prompt.reshape(B * S, D_MODEL)
    xs = x_sample.reshape(DB, D_MODEL)
    tab_p = _rope_table(jnp.arange(S))
    tab_s = _rope_table(jnp.full((DB,), past_len, I32))
    pt_flat = page_table.reshape(-1).astype(I32)
    fox_kv_t = jnp.transpose(cache_fox_kv, (0, 1, 3, 4, 5, 2))
    fox_lft_c = jnp.swapaxes(cache_fox_logf, 2, 3)
    nsa_t = jnp.transpose(cache_nsa_kv, (0, 1, 3, 4, 2))
    win_state_t = jnp.transpose(state_nsa_win, (0, 1, 3, 4, 2))
    n_cmp = depth * n_pool * (PG // CMP_LEN)
    cmp_blocks = _block_major(cache_nsa_kv[:, :, :, 0:2, :].reshape(n_cmp * CMP_LEN, 2, HEAD_DIM), n_cmp)
    gfin = final_norm_g.reshape(1, D_MODEL)
    cp, cs = [], []
    carry_p, carry_s = {}, {}
    for l in range(depth):
        p = _layer_mix_params(l, norm_mix_g, w_in, b_fox_f, w_cmp, pe_cmp, w_gla_gk, b_gla_gk,
                              g_gla_norm, w_out)
        gf = norm_ffn_g[l].reshape(1, D_MODEL)
        i = l // 2
        moe = l % 2 == 1
        xp, c = _mix_prompt(xp, B, S, l, depth, p, tab_p, Tiles.prompt_rows, Tiles.fox, Tiles.nsa_keys,
                            Tiles.gla_rows, carry_p)
        cp.append(c)
        if moe:
            wr = jnp.zeros((D_MODEL, LANES), F32).at[:, 0:N_EXPERTS].set(moe_w_router[i])
            experts = (moe_w_gate[i], moe_w_up[i], moe_w_down[i])
            disp_p = _moe_dispatch(xp, gf, wr, Tiles.prompt_rows, Tiles.expert_block)
        xs, c = _mix_sample(xs, l, p, tab_s, pt_flat, n_pages, past_len, fox_kv_t, fox_lft_c, nsa_t,
                            cmp_blocks, win_state_t, state_gla, carry_s)
        cs.append(c)
        if moe:
            fin = gfin if l == depth - 1 else None
            y_p = _moe_experts(disp_p, *experts, Tiles.expert_block, Tiles.ffn_cols)
            xs = _moe_small(xs, gf, wr, *experts, Tiles.ffn_cols_sample, fin)
            xp = _moe_merge(disp_p, *y_p, Tiles.prompt_rows, fin)
        else:
            wg, wu, wd = (dense_w_gate[i].astype(BF16), dense_w_up[i].astype(BF16),
                          dense_w_down[i].astype(BF16))
            xp = _dense_ffn(xp, gf, wg, wu, wd, Tiles.dense_rows, Tiles.ffn_cols)
            xs = _dense_ffn(xs, gf, dense_w_gate[i], dense_w_up[i], dense_w_down[i], DB, Tiles.ffn_cols_sample)
    if depth % 2 == 1:
        xp = _final_norm(xp, gfin, Tiles.prompt_rows)
        xs = _final_norm(xs, gfin, DB)
    y_p = xp.reshape(B, S, D_MODEL)
    y_s = xs.reshape(DB, 1, D_MODEL)
    wp = min(WINDOW, S)
    st = lambda key, group: jnp.stack([c[key] for c in group])
    return (y_p, y_s,
            carry_p['fkv'].reshape(depth, B, 2, N_HEADS, HEAD_DIM, S).transpose(0, 1, 5, 2, 3, 4),
            st('small', cp).transpose(0, 1, 3, 2),
            carry_p['rows'].reshape(depth, B, 4, HEAD_DIM, S).transpose(0, 1, 4, 2, 3),
            st('win', cp).reshape(depth, B, 2, HEAD_DIM, wp).transpose(0, 1, 4, 2, 3),
            st('g_state', cp),
            st('fkv', cs).reshape(depth, DB, 1, 2, N_HEADS, HEAD_DIM),
            st('small', cs)[:, :, SM_FF:SM_FF + N_HEADS].reshape(depth, DB, 1, N_HEADS),
            st('rows', cs).reshape(depth, DB, 1, 4, HEAD_DIM),
            jnp.transpose(carry_s['win'], (0, 1, 4, 2, 3)),
            carry_s['gla'])
```

```python
import functools

import jax
import jax.numpy as jnp
from jax import lax
from jax.experimental import pallas as pl
from jax.experimental.pallas import tpu as pltpu
from jax.experimental.pallas import tpu_sc as plsc

F32 = jnp.float32
BF16 = jnp.bfloat16
I32 = jnp.int32
HI = lax.Precision.HIGHEST

D_MODEL = 1024
HEAD_DIM = 64
N_HEADS = 4
D_FOX = N_HEADS * HEAD_DIM
D_NSA = N_HEADS * HEAD_DIM
GLA_DV = 128
D_GLA = N_HEADS * GLA_DV
GLA_RANK = 16
GLA_TAU = 16.0
GLA_CHUNK = 64
CMP_LEN = 32
SEL_LEN = 64
TOP_N = 16
WINDOW = 512
ROPE_THETA = 500000.0
ROPE_DIM = HEAD_DIM // 4
ROPE_HALF = ROPE_DIM // 2
D_FF = 3584
N_EXPERTS = 8
EPS = 1e-6
SEL_FORCE = 1e9
NEG = -1e30
SCALE = HEAD_DIM ** -0.5
LOG2E = 1.4426950408889634

LANES = 128
SUBLANES = 8
VMEM_BYTES_V7X = 64 * 1024 * 1024
VMEM_LIMIT = VMEM_BYTES_V7X - 8 * 1024 * 1024


class Tiles:
    prompt_rows = 512
    fox = 512
    nsa_keys = 512
    gla_rows = 256
    cumsum = 512
    compress_rows = 256
    dense_rows = 1024
    ffn_cols = 512
    ffn_cols_sample = 896
    expert_block = 1024

C_FQ = 0
C_FKV = 256
C_NQ = 768
C_NKV = 1024
C_GQK = 1408
C_GV = 1920
C_GOG = 2432
C_SMALL = 2944
C_END = 3072
SM_FF = 0
SM_NG = 4
SM_GLR = 16

NT = (((1,), (1,)), ((), ()))


def _params(sem):
    return pltpu.CompilerParams(dimension_semantics=sem, vmem_limit_bytes=VMEM_LIMIT)


def _rms(x, g):
    ms = jnp.mean(x * x, axis=-1, keepdims=True)
    return x * lax.rsqrt(ms + EPS) * g


def _sigmoid(x):
    return 1.0 / (1.0 + jnp.exp(-x))


def _log_sigmoid(x):
    return -(jnp.maximum(-x, 0.0) + jnp.log1p(jnp.exp(-jnp.abs(x))))


def _silu(x):
    return x * _sigmoid(x)


def _bdot(a, b):
    return jnp.dot(a.astype(BF16), b.astype(BF16), preferred_element_type=F32)


def _split3(x):
    h = x.astype(BF16)
    r = x - h.astype(F32)
    m = r.astype(BF16)
    return h, m, (r - m.astype(F32)).astype(BF16)


def _dot01(m01, x):
    mb = m01.astype(BF16)
    h, m, l = _split3(x)
    return (jnp.dot(mb, h, preferred_element_type=F32) + jnp.dot(mb, m, preferred_element_type=F32)
            + jnp.dot(mb, l, preferred_element_type=F32))


def _dot01_r(x, m01):
    mb = m01.astype(BF16)
    h, m, l = _split3(x)
    return (jnp.dot(h, mb, preferred_element_type=F32) + jnp.dot(m, mb, preferred_element_type=F32)
            + jnp.dot(l, mb, preferred_element_type=F32))


def _dot01_nt(m01, x):
    mb = m01.astype(BF16)
    h, m, l = _split3(x)
    return (lax.dot_general(mb, h, NT, preferred_element_type=F32)
            + lax.dot_general(mb, m, NT, preferred_element_type=F32)
            + lax.dot_general(mb, l, NT, preferred_element_type=F32))


def _dot_nt_hilo(a, b):
    ah = a.astype(BF16)
    al = (a - ah.astype(F32)).astype(BF16)
    bh = b.astype(BF16)
    bl = (b - bh.astype(F32)).astype(BF16)
    return (lax.dot_general(ah, bh, NT, preferred_element_type=F32)
            + lax.dot_general(al, bh, NT, preferred_element_type=F32)
            + lax.dot_general(ah, bl, NT, preferred_element_type=F32))


def _wdot(a, w):
    if w.dtype == F32:
        return jnp.dot(a.astype(F32), w, precision=HI, preferred_element_type=F32)
    return jnp.dot(a.astype(BF16), w, preferred_element_type=F32)


def _rope128(x, a, bp, bm):
    return x * a + pltpu.roll(x, ROPE_HALF, 1) * bp + pltpu.roll(x, LANES - ROPE_HALF, 1) * bm


def _inproj_kernel(x_ref, g_ref, w_ref, sb_ref, wgk_ref, bgk_ref, tab_ref, *refs, feature_major, n_carried):
    (fq_ref, fkv_ref, nq_ref, nqr_ref, rows_ref, win_ref,
     gqk_ref, gv_ref, gog_ref, glog_ref, small_ref, *extra) = refs[n_carried:]
    h = _rms(x_ref[...], g_ref[...]).astype(w_ref.dtype)

    def put(ref, v):
        ref[...] = v.T if feature_major else v

    def mm(a, b):
        return _wdot(h, w_ref[:, a:b])

    fq_ref[...] = mm(C_FQ, C_FKV)
    put(fkv_ref, mm(C_FKV, C_NQ))
    tab = tab_ref[...]
    ab, pb, mb = tab[:, 0:128], tab[:, 128:256], tab[:, 256:384]
    af, pf, mf = tab[:, 384:512], tab[:, 512:640], tab[:, 640:768]
    nq = mm(C_NQ, C_NKV)
    nq_ref[...] = nq
    nqr_ref[:, 0:128] = _rope128(nq[:, 0:128], ab, pb, mb)
    nqr_ref[:, 128:256] = _rope128(nq[:, 128:256], ab, pb, mb)
    nkv = mm(C_NKV, C_GQK)
    put(rows_ref, jnp.concatenate([nkv[:, 0:128], _rope128(nkv[:, 128:256], af, pf, mf)], axis=1))
    put(win_ref, _rope128(nkv[:, 256:384], af, pf, mf))
    gqk_ref[...] = mm(C_GQK, C_GV)
    gv_ref[...] = mm(C_GV, C_GOG)
    gog_ref[...] = mm(C_GOG, C_SMALL)
    sm = mm(C_SMALL, C_END)
    glog_ref[...] = _log_sigmoid(_wdot(sm, wgk_ref[...]) + bgk_ref[...]) * (1.0 / GLA_TAU)
    smb = sm + sb_ref[...]
    lane = lax.broadcasted_iota(I32, smb.shape, 1)
    small = jnp.where(lane < SM_NG, _log_sigmoid(smb), _sigmoid(smb))
    small_ref[...] = small
    if feature_major:
        extra[0][...] = small.T[0:SUBLANES, :]


def _in_projection(x_all, row0, n_rows, tm, g, w_r, sb, wgk, bgk, tab, tab_period, batch=None, layer=None,
                   prev=None):
    assert n_rows % tm == 0 and row0 % tm == 0 and tab_period % tm == 0
    nt = n_rows // tm
    b0 = row0 // tm
    npd = tab_period // tm
    widths = (256, 512, 256, 256, 256, 128, 512, 512, 512, 256, 128)
    FKV, ROWS, WIN = 1, 4, 5
    full = lambda shape: pl.BlockSpec(shape, lambda i: (0, 0))
    row_spec = lambda w: pl.BlockSpec((tm, w), lambda i: (i, 0))
    carried = [] if prev is None else list(prev)
    if batch is not None:
        B, S, depth = batch
        per = S // tm
        assert n_rows == B * S and S % tm == 0
        t_spec = lambda w: pl.BlockSpec((None, w, tm), lambda i: (i // per, 0, i % per))
        l_spec = lambda w: pl.BlockSpec((None, None, w, tm), lambda i: (layer, i // per, 0, i % per))
        out_specs = [l_spec(w) if k in (FKV, ROWS) else t_spec(w) if k == WIN else row_spec(w)
                     for k, w in enumerate(widths)] + [t_spec(SUBLANES)]
        out_shape = [jax.ShapeDtypeStruct((depth, B, w, S) if k in (FKV, ROWS) else (B, w, S) if k == WIN
                                          else (n_rows, w), F32)
                     for k, w in enumerate(widths)] + [jax.ShapeDtypeStruct((B, SUBLANES, S), F32)]
    else:
        out_specs = [row_spec(w) for w in widths]
        out_shape = [jax.ShapeDtypeStruct((n_rows, w), F32) for w in widths]
    return pl.pallas_call(
        functools.partial(_inproj_kernel, feature_major=batch is not None, n_carried=len(carried)),
        grid=(nt,),
        in_specs=[
            pl.BlockSpec((tm, D_MODEL), lambda i: (b0 + i, 0)),
            full((1, D_MODEL)),
            full((D_MODEL, C_END)),
            full((1, LANES)),
            full((LANES, 256)),
            full((1, 256)),
            pl.BlockSpec((tm, 768), lambda i: (i % npd, 0)),
        ] + [pl.BlockSpec(memory_space=pl.ANY)] * len(carried),
        out_specs=out_specs,
        out_shape=out_shape,
        input_output_aliases={7: FKV, 8: ROWS} if carried else {},
        compiler_params=_params(("parallel",)),
        name="in_projection",
    )(x_all, g, w_r, sb, wgk, bgk, tab, *carried)


def _cumsum_kernel(sm_ref, cr_ref, carry):
    t = pl.program_id(1)
    ts = sm_ref.shape[1]

    @pl.when(t == 0)
    def _():
        carry[...] = jnp.zeros_like(carry)

    r = lax.broadcasted_iota(I32, (ts, ts), 0)
    c = lax.broadcasted_iota(I32, (ts, ts), 1)
    cs = _dot01_r(sm_ref[...], r <= c) + carry[...]
    carry[...] = cs[:, ts - 1:ts]
    cr_ref[...] = cs * LOG2E


def _fox_cumsum(small_t, ts):
    B, _, S = small_t.shape
    spec = pl.BlockSpec((None, SUBLANES, ts), lambda b, t: (b, 0, t))
    return pl.pallas_call(
        _cumsum_kernel,
        grid=(B, S // ts),
        in_specs=[spec],
        out_specs=spec,
        out_shape=jax.ShapeDtypeStruct((B, SUBLANES, S), F32),
        scratch_shapes=[pltpu.VMEM((SUBLANES, 1), F32)],
        compiler_params=_params(("parallel", "arbitrary")),
        name="fox_cumsum",
    )(small_t)


def _pair_mask(shape, h, axis=1):
    return (lax.broadcasted_iota(I32, shape, axis) // HEAD_DIM) == (h % 2)


def _fox_prompt_kernel(qi_ref, kj_ref, q_ref, kv_ref, cr_ref, o_ref, *scratch):
    i = qi_ref[pl.program_id(1)]
    j = kj_ref[pl.program_id(1)]
    tq = q_ref.shape[0]
    tk = kv_ref.shape[1]
    q_sc, m_sc, acc_sc = scratch[0:N_HEADS], scratch[N_HEADS:2 * N_HEADS], scratch[2 * N_HEADS:]

    @pl.when(j == 0)
    def _():
        for h in range(N_HEADS):
            m_sc[h][...] = jnp.full_like(m_sc[h], NEG)
            acc_sc[h][...] = jnp.zeros_like(acc_sc[h])
            slab = q_ref[:, (h // 2) * LANES:(h // 2 + 1) * LANES] * (SCALE * LOG2E)
            q_sc[h][...] = jnp.where(_pair_mask(slab.shape, h), slab, 0.0).astype(BF16)

    def tile(diagonal):
        k_slabs = [kv_ref[c0:c0 + LANES, :].astype(BF16) for c0 in (0, LANES)]
        if diagonal:
            mask = lax.broadcasted_iota(I32, (1, tk), 1) <= lax.broadcasted_iota(I32, (tq, 1), 0)
        for h in range(N_HEADS):
            v_slab = kv_ref[D_FOX + (h // 2) * LANES:D_FOX + (h // 2 + 1) * LANES, :]
            v_aug = jnp.where(_pair_mask(v_slab.shape, h, 0), v_slab, 1.0).astype(BF16)
            s = jnp.dot(q_sc[h][...], k_slabs[h // 2], preferred_element_type=F32) - cr_ref[h:h + 1, :]
            if diagonal:
                s = jnp.where(mask, s, NEG)
            m_old = m_sc[h][...]
            m_new = jnp.maximum(m_old, jnp.max(s, axis=-1, keepdims=True))
            p = jnp.exp2(s - m_new).astype(BF16)
            acc_sc[h][...] = (jnp.exp2(m_old - m_new) * acc_sc[h][...]
                              + lax.dot_general(p, v_aug, NT, preferred_element_type=F32))
            m_sc[h][...] = m_new

    @pl.when(j < i)
    def _():
        tile(False)

    @pl.when(j == i)
    def _():
        tile(True)
        for h in range(N_HEADS):
            a = acc_sc[h][...]
            lo = (h % 2) * HEAD_DIM
            den = a[:, HEAD_DIM - lo:HEAD_DIM - lo + 1]
            o_ref[:, h * HEAD_DIM:(h + 1) * HEAD_DIM] = a[:, lo:lo + HEAD_DIM] / den


def _fox_prompt(fq, fkv_all, l, cr, B, S, t):
    n = S // t
    pairs = [(i, j) for i in range(n) for j in range(i + 1)]
    qi = jnp.asarray([p[0] for p in pairs], I32)
    kj = jnp.asarray([p[1] for p in pairs], I32)
    return pl.pallas_call(
        _fox_prompt_kernel,
        grid_spec=pltpu.PrefetchScalarGridSpec(
            num_scalar_prefetch=2,
            grid=(B, len(pairs)),
            in_specs=[
                pl.BlockSpec((t, D_FOX), lambda b, s, qi, kj: (b * n + qi[s], 0)),
                pl.BlockSpec((None, None, 2 * D_FOX, t), lambda b, s, qi, kj: (l, b, 0, kj[s])),
                pl.BlockSpec((None, SUBLANES, t), lambda b, s, qi, kj: (b, 0, kj[s])),
            ],
            out_specs=pl.BlockSpec((t, D_FOX), lambda b, s, qi, kj: (b * n + qi[s], 0)),
            scratch_shapes=([pltpu.VMEM((t, LANES), BF16)] * N_HEADS + [pltpu.VMEM((t, 1), F32)] * N_HEADS
                            + [pltpu.VMEM((t, LANES), F32)] * N_HEADS),
        ),
        out_shape=jax.ShapeDtypeStruct((B * S, D_FOX), F32),
        compiler_params=_params(("parallel", "arbitrary")),
        name="fox_prompt",
    )(qi, kj, fq, fkv_all, cr)


def _compress_kernel(x_ref, pe_ref, w_ref, o_ref, *, exact):
    for s in range(2):
        x = x_ref[s] + pe_ref[s]
        w = w_ref[s]
        if exact:
            y = jnp.dot(x, w, precision=HI, preferred_element_type=F32)
        else:
            xh = x.astype(BF16)
            xl = (x - xh.astype(F32)).astype(BF16)
            wh = w.astype(BF16)
            wl = (w - wh.astype(F32)).astype(BF16)
            y = (jnp.dot(xh, wh, preferred_element_type=F32) + jnp.dot(xl, wh, preferred_element_type=F32)
                 + jnp.dot(xh, wl, preferred_element_type=F32))
        o_ref[:, s * HEAD_DIM:(s + 1) * HEAD_DIM] = y


def _compress(x3, pe, w, tr, exact, row0=0, n_rows=None):
    K = x3.shape[2]
    R = x3.shape[1] if n_rows is None else n_rows
    assert R % tr == 0 and row0 % tr == 0
    b0 = row0 // tr
    return pl.pallas_call(
        functools.partial(_compress_kernel, exact=exact),
        grid=(R // tr,),
        in_specs=[pl.BlockSpec((2, tr, K), lambda i: (0, b0 + i, 0)),
                  pl.BlockSpec((2, 1, K), lambda i: (0, 0, 0)),
                  pl.BlockSpec((2, K, HEAD_DIM), lambda i: (0, 0, 0))],
        out_specs=pl.BlockSpec((tr, LANES), lambda i: (i, 0)),
        out_shape=jax.ShapeDtypeStruct((R, LANES), F32),
        compiler_params=_params(("parallel",)),
        name="nsa_compress",
    )(x3, pe, w)


def _block_major(kv, n_blocks):
    return kv.reshape(n_blocks, CMP_LEN, 2, HEAD_DIM).transpose(2, 0, 1, 3).reshape(
        2, n_blocks, CMP_LEN * HEAD_DIM)


def _order_key(x):
    b = lax.bitcast_convert_type(x, I32)
    return jnp.where(b < 0, b ^ jnp.int32(0x7FFFFFFF), b)


def _nsa_prompt_kernel(nq_ref, nqr_ref, sm_ref, cmp_ref, rows_ref, win_ref, o_ref, qc_sc, qx_sc, *, tk):
    QB = nq_ref.shape[0]
    S = rows_ref.shape[1]
    nb = cmp_ref.shape[0]
    nsel = S // SEL_LEN
    i = pl.program_id(1)
    qs = i * QB
    qpos = qs + lax.broadcasted_iota(I32, (QB, 1), 0)

    HQ = N_HEADS * QB
    lo_half = lax.broadcasted_iota(I32, (QB, LANES), 1) < HEAD_DIM

    def stack_heads(ref, scale, dst):
        for h in range(N_HEADS):
            slab = ref[:, (h // 2) * LANES:(h // 2 + 1) * LANES] * scale
            if h % 2:
                slab = pltpu.roll(slab, HEAD_DIM, 1)
            dst[h * QB:(h + 1) * QB, :] = jnp.where(lo_half, slab, 0.0).astype(dst.dtype)

    cmp = cmp_ref[...]
    n_l = lax.broadcasted_iota(I32, (1, nb), 1)
    complete = ((n_l + 1) * CMP_LEN - 1) <= qpos
    stack_heads(nq_ref, SCALE, qc_sc)
    s = _dot_nt_hilo(qc_sc[...], cmp).reshape(N_HEADS, QB, nb)
    s = jnp.where(complete[None], s, NEG)
    e = jnp.exp(s - jnp.max(s, axis=-1, keepdims=True))
    p = e / jnp.sum(e, axis=-1, keepdims=True) * complete.astype(F32)[None]
    o_cmp = _bdot(p.reshape(HQ, nb), cmp)
    psum = jnp.sum(p, axis=0)

    pj = lax.broadcasted_iota(I32, (nsel, nb), 0)
    pn = lax.broadcasted_iota(I32, (nsel, nb), 1)
    imp_t = _dot01_nt(pn // (SEL_LEN // CMP_LEN) == pj, psum)
    jt = (qs + lax.broadcasted_iota(I32, (1, QB), 1)) // SEL_LEN
    jj = lax.broadcasted_iota(I32, (nsel, 1), 0)
    score = jnp.where(jj == jt, 2.0 * SEL_FORCE,
                      jnp.where((jj == 0) | (jj == jt - 1), SEL_FORCE,
                                jnp.where(jj <= jt, imp_t + 0.0, -1.0)))
    key = _order_key(score)
    key_m1 = key - 1
    ngrp = nsel // SUBLANES
    sub = lax.broadcasted_iota(I32, (SUBLANES, QB), 0)
    kg = [key[r * SUBLANES:(r + 1) * SUBLANES, :] for r in range(ngrp)]
    kg1 = [key_m1[r * SUBLANES:(r + 1) * SUBLANES, :] for r in range(ngrp)]
    cnt = [jnp.zeros((SUBLANES, QB), I32) for _ in range(ngrp)]
    for jp in range(nsel):
        g = jp // SUBLANES
        row = key[jp:jp + 1, :]
        mixed = jnp.where(sub > (jp % SUBLANES), kg1[g], kg[g])
        for r in range(ngrp):
            thr = kg[r] if r < g else (kg1[r] if r > g else mixed)
            cnt[r] = cnt[r] + (row > thr).astype(I32)
    sel_t = jnp.concatenate([(c < TOP_N).astype(F32) for c in cnt], axis=0)
    if nsel < QB:
        sel_t = jnp.concatenate([sel_t, jnp.zeros((QB - nsel, QB), F32)], axis=0)
    sel = sel_t.T.astype(BF16)

    stack_heads(nqr_ref, SCALE * LOG2E, qx_sc)
    qx = qx_sc[...]

    def attend(valid, slab_t, m_old, acc_old):
        n = slab_t.shape[1]
        s = jnp.dot(qx, slab_t.astype(BF16), preferred_element_type=F32)
        s = jnp.where(valid[None], s.reshape(N_HEADS, QB, n), NEG).reshape(HQ, n)
        m_new = jnp.maximum(m_old, jnp.max(s, axis=-1, keepdims=True))
        p = jnp.exp2(s - m_new).astype(BF16)
        ones_k = lax.broadcasted_iota(I32, slab_t.shape, 0) < HEAD_DIM
        v_aug = jnp.where(ones_k, 1.0, slab_t).astype(BF16)
        acc = jnp.exp2(m_old - m_new) * acc_old + lax.dot_general(p, v_aug, NT, preferred_element_type=F32)
        return m_new, acc

    jrow = lax.broadcasted_iota(I32, (QB, 1), 0)

    def sel_tile(k0, m_old, acc_old, diagonal):
        kpos = k0 + lax.broadcasted_iota(I32, (1, tk), 1)
        expand = (jrow == kpos // SEL_LEN).astype(BF16)
        valid = jnp.dot(sel, expand, preferred_element_type=F32) > 0.5
        if diagonal:
            valid = valid & (kpos <= qpos)
        return attend(valid, rows_ref[2 * HEAD_DIM:4 * HEAD_DIM, pl.ds(k0, tk)], m_old, acc_old)

    n_full = qs // tk
    init = (jnp.full((HQ, 1), NEG, F32), jnp.zeros((HQ, LANES), F32))
    m_s, acc_s = lax.fori_loop(
        0, n_full, lambda t, c: sel_tile(pl.multiple_of(t * tk, tk), c[0], c[1], False), init)
    _, acc_s = sel_tile(pl.multiple_of(n_full * tk, tk), m_s, acc_s, True)

    wlen = WINDOW + QB
    w0 = pl.multiple_of(jnp.maximum(qs - WINDOW, 0), QB)
    wpos = w0 + lax.broadcasted_iota(I32, (1, wlen), 1)
    d = qpos - wpos
    _, acc_w = attend((d >= 0) & (d < WINDOW), win_ref[:, pl.ds(w0, wlen)],
                      jnp.full((HQ, 1), NEG, F32), jnp.zeros((HQ, LANES), F32))

    sm = sm_ref[...]
    for h in range(N_HEADS):
        rs = slice(h * QB, (h + 1) * QB)
        o_sel = acc_s[rs] * (1.0 / acc_s[rs, 0:1])
        o_win = acc_w[rs] * (1.0 / acc_w[rs, 0:1])
        c = SM_NG + 3 * h
        mix = sm[:, c:c + 1] * o_cmp[rs] + sm[:, c + 1:c + 2] * o_sel + sm[:, c + 2:c + 3] * o_win
        if h % 2 == 0:
            mix = pltpu.roll(mix, HEAD_DIM, 1)
        lo = (h % 2) * HEAD_DIM
        o_ref[:, h * HEAD_DIM:(h + 1) * HEAD_DIM] = mix[:, lo:lo + HEAD_DIM]


def _nsa_prompt(nq, nqr, small, cmp, rows_all, l, win, B, S, tk):
    QB = 128
    nq_t = S // QB
    nb = S // CMP_LEN
    assert S % tk == 0 and S >= WINDOW + QB
    return pl.pallas_call(
        functools.partial(_nsa_prompt_kernel, tk=tk),
        grid=(B, nq_t),
        in_specs=[
            pl.BlockSpec((QB, D_NSA), lambda b, i: (b * nq_t + i, 0)),
            pl.BlockSpec((QB, D_NSA), lambda b, i: (b * nq_t + i, 0)),
            pl.BlockSpec((QB, LANES), lambda b, i: (b * nq_t + i, 0)),
            pl.BlockSpec((nb, LANES), lambda b, i: (b, 0)),
            pl.BlockSpec((None, None, 4 * HEAD_DIM, S), lambda b, i: (l, b, 0, 0)),
            pl.BlockSpec((None, 2 * HEAD_DIM, S), lambda b, i: (b, 0, 0)),
        ],
        out_specs=pl.BlockSpec((QB, D_NSA), lambda b, i: (b * nq_t + i, 0)),
        out_shape=jax.ShapeDtypeStruct((B * S, D_NSA), F32),
        scratch_shapes=[pltpu.VMEM((N_HEADS * QB, LANES), F32), pltpu.VMEM((N_HEADS * QB, LANES), BF16)],
        compiler_params=_params(("parallel", "parallel")),
        name="nsa_prompt",
    )(nq, nqr, small, cmp, rows_all, win)


def _gla_prompt_kernel(qk_ref, v_ref, g_ref, o_ref, st_ref, s_sc):
    t = pl.program_id(1)
    nt = pl.num_programs(1)
    tc = qk_ref.shape[0]
    C = GLA_CHUNK

    @pl.when(t == 0)
    def _():
        s_sc[...] = jnp.zeros_like(s_sc)

    r = lax.broadcasted_iota(I32, (tc, tc), 0)
    c = lax.broadcasted_iota(I32, (tc, tc), 1)
    same = (r // C) == (c // C)
    causal = same & (c <= r)
    g = g_ref[...]
    gcum = _dot01(causal, g)
    g_t = g.T
    gcum_t = _dot01_r(g_t, same & (r <= c))
    gtot_t = _dot01_r(g_t, same)
    q_e = (qk_ref[:, 0:D_FOX] * SCALE * jnp.exp(gcum)).astype(BF16)
    k_e = (qk_ref[:, D_FOX:2 * D_FOX] * jnp.exp(-gcum)).astype(BF16)
    kd_t = (qk_ref[:, D_FOX:2 * D_FOX].T * jnp.exp(gtot_t - gcum_t)).astype(BF16)
    decay_t = jnp.exp(gtot_t)
    for h in range(N_HEADS):
        hs = slice(h * HEAD_DIM, (h + 1) * HEAD_DIM)
        v = v_ref[:, h * GLA_DV:(h + 1) * GLA_DV].astype(BF16)
        a = jnp.where(causal, lax.dot_general(q_e[:, hs], k_e[:, hs], NT, preferred_element_type=F32), 0.0)
        o_intra = jnp.dot(a.astype(BF16), v, preferred_element_type=F32)
        state = s_sc[h]
        for ci in range(tc // C):
            rs = slice(ci * C, (ci + 1) * C)
            o_ref[rs, h * GLA_DV:(h + 1) * GLA_DV] = (
                o_intra[rs] + jnp.dot(q_e[rs, hs], state.astype(BF16), preferred_element_type=F32))
            state = (decay_t[hs, ci * C:ci * C + 1] * state
                     + jnp.dot(kd_t[hs, rs], v[rs], preferred_element_type=F32))
        s_sc[h] = state

    @pl.when(t == nt - 1)
    def _():
        st_ref[...] = s_sc[...]


def _gla_prompt(gqk, gv, glog, B, S, tc):
    nt = S // tc
    return pl.pallas_call(
        _gla_prompt_kernel,
        grid=(B, nt),
        in_specs=[pl.BlockSpec((tc, 2 * D_FOX), lambda b, t: (b * nt + t, 0)),
                  pl.BlockSpec((tc, D_GLA), lambda b, t: (b * nt + t, 0)),
                  pl.BlockSpec((tc, D_FOX), lambda b, t: (b * nt + t, 0))],
        out_specs=[pl.BlockSpec((tc, D_GLA), lambda b, t: (b * nt + t, 0)),
                   pl.BlockSpec((None, N_HEADS, HEAD_DIM, GLA_DV), lambda b, t: (b, 0, 0, 0))],
        out_shape=[jax.ShapeDtypeStruct((B * S, D_GLA), F32),
                   jax.ShapeDtypeStruct((B, N_HEADS, HEAD_DIM, GLA_DV), F32)],
        scratch_shapes=[pltpu.VMEM((N_HEADS, HEAD_DIM, GLA_DV), F32)],
        compiler_params=_params(("parallel", "arbitrary")),
        name="gla_prompt",
    )(gqk, gv, glog)


def _outproj_kernel(x_ref, of_ref, on_ref, og_ref, gog_ref, gn_ref, w_ref, o_ref):
    acc = _wdot(of_ref[...], w_ref[0:D_FOX, :])
    acc = acc + _wdot(on_ref[...], w_ref[D_FOX:D_FOX + D_NSA, :])
    for h in range(N_HEADS):
        hs = slice(h * GLA_DV, (h + 1) * GLA_DV)
        z = _rms(og_ref[:, hs], gn_ref[...]) * _silu(gog_ref[:, hs])
        w0 = D_FOX + D_NSA + h * GLA_DV
        acc = acc + _wdot(z, w_ref[w0:w0 + GLA_DV, :])
    o_ref[...] = x_ref[...] + acc


def _out_projection(x, o_fox, o_nsa, o_gla, gog, gn, w_out, tm):
    T = x.shape[0]
    assert T % tm == 0
    row = lambda w: pl.BlockSpec((tm, w), lambda i: (i, 0))
    return pl.pallas_call(
        _outproj_kernel,
        grid=(T // tm,),
        in_specs=[row(D_MODEL), row(D_FOX), row(D_NSA), row(D_GLA), row(D_GLA),
                  pl.BlockSpec((1, GLA_DV), lambda i: (0, 0)),
                  pl.BlockSpec((D_MODEL, D_MODEL), lambda i: (0, 0))],
        out_specs=row(D_MODEL),
        out_shape=jax.ShapeDtypeStruct((T, D_MODEL), F32),
        compiler_params=_params(("parallel",)),
        name="out_projection",
    )(x, o_fox, o_nsa, o_gla, gog, gn, w_out)


def _dense_ffn_kernel(x_ref, g_ref, wg_ref, wu_ref, wd_ref, o_ref, h_sc, acc_sc):
    f = pl.program_id(1)
    nf = pl.num_programs(1)

    @pl.when(f == 0)
    def _():
        h_sc[...] = _rms(x_ref[...], g_ref[...]).astype(h_sc.dtype)
        acc_sc[...] = jnp.zeros_like(acc_sc)

    h = h_sc[...]
    a = _wdot(h, wg_ref[...])
    u = _wdot(h, wu_ref[...])
    acc_sc[...] += _wdot(_silu(a) * u, wd_ref[...])

    @pl.when(f == nf - 1)
    def _():
        o_ref[...] = x_ref[...] + acc_sc[...]


def _dense_ffn(x, g, wg, wu, wd, tm, tf):
    T = x.shape[0]
    assert T % tm == 0 and D_FF % tf == 0
    return pl.pallas_call(
        _dense_ffn_kernel,
        grid=(T // tm, D_FF // tf),
        in_specs=[pl.BlockSpec((tm, D_MODEL), lambda i, f: (i, 0)),
                  pl.BlockSpec((1, D_MODEL), lambda i, f: (0, 0)),
                  pl.BlockSpec((D_MODEL, tf), lambda i, f: (0, f)),
                  pl.BlockSpec((D_MODEL, tf), lambda i, f: (0, f)),
                  pl.BlockSpec((tf, D_MODEL), lambda i, f: (f, 0))],
        out_specs=pl.BlockSpec((tm, D_MODEL), lambda i, f: (i, 0)),
        out_shape=jax.ShapeDtypeStruct((T, D_MODEL), F32),
        scratch_shapes=[pltpu.VMEM((tm, D_MODEL), wg.dtype), pltpu.VMEM((tm, D_MODEL), F32)],
        compiler_params=_params(("parallel", "arbitrary")),
        name="dense_ffn",
    )(x, g, wg, wu, wd)


def _top2_route(h, wr):
    logits = jnp.dot(h, wr, precision=HI, preferred_element_type=F32)
    lane = lax.broadcasted_iota(I32, logits.shape, 1)
    lg = jnp.where(lane < N_EXPERTS, logits, -jnp.inf)
    m1 = jnp.max(lg, axis=-1, keepdims=True)
    i1 = jnp.min(jnp.where(lg == m1, lane, LANES), axis=-1, keepdims=True)
    lg2 = jnp.where(lane == i1, -jnp.inf, lg)
    m2 = jnp.max(lg2, axis=-1, keepdims=True)
    i2 = jnp.min(jnp.where(lg2 == m2, lane, LANES), axis=-1, keepdims=True)
    e = jnp.exp(m2 - m1)
    den = 1.0 + e
    return jnp.where(lane == 0, i1.astype(F32),
                     jnp.where(lane == 1, i2.astype(F32),
                               jnp.where(lane == 2, 1.0 / den,
                                         jnp.where(lane == 3, e / den, 0.0))))


def _router_kernel(x_ref, g_ref, wr_ref, h_ref, r_ref):
    h = _rms(x_ref[...], g_ref[...])
    h_ref[...] = h
    r_ref[...] = _top2_route(h, wr_ref[...])


def _moe_small_kernel(x_ref, g_ref, wr_ref, fg_ref, wg_ref, wu_ref, wd_ref, o_ref, h_sc, acc_sc, r_sc, *,
                      final):
    e = pl.program_id(0)
    f = pl.program_id(1)

    @pl.when((e == 0) & (f == 0))
    def _():
        h = _rms(x_ref[...], g_ref[...])
        h_sc[...] = h.astype(BF16)
        r_sc[...] = _top2_route(h, wr_ref[...])
        acc_sc[...] = jnp.zeros_like(acc_sc)

    r = r_sc[...]
    ef = e.astype(F32)
    gate = jnp.where(r[:, 0:1] == ef, r[:, 2:3], 0.0) + jnp.where(r[:, 1:2] == ef, r[:, 3:4], 0.0)
    h = h_sc[...]
    a = jnp.dot(h, wg_ref[...].astype(BF16), preferred_element_type=F32)
    u = jnp.dot(h, wu_ref[...].astype(BF16), preferred_element_type=F32)
    y = _bdot(_silu(a) * u, wd_ref[...])
    acc_sc[...] += jnp.where(gate != 0.0, gate * y, 0.0)

    @pl.when((e == pl.num_programs(0) - 1) & (f == pl.num_programs(1) - 1))
    def _():
        out = x_ref[...] + acc_sc[...]
        o_ref[...] = _rms(out, fg_ref[...]) if final else out


def _moe_small(x, g, wr_pad, wg, wu, wd, tf, final_g):
    T = x.shape[0]
    fg = jnp.ones((1, D_MODEL), F32) if final_g is None else final_g
    const = lambda shape: pl.BlockSpec(shape, lambda e, f: (0, 0))
    return pl.pallas_call(
        functools.partial(_moe_small_kernel, final=final_g is not None),
        grid=(N_EXPERTS, D_FF // tf),
        in_specs=[const((T, D_MODEL)), const((1, D_MODEL)), const((D_MODEL, LANES)), const((1, D_MODEL)),
                  pl.BlockSpec((None, D_MODEL, tf), lambda e, f: (e, 0, f)),
                  pl.BlockSpec((None, D_MODEL, tf), lambda e, f: (e, 0, f)),
                  pl.BlockSpec((None, tf, D_MODEL), lambda e, f: (e, f, 0))],
        out_specs=const((T, D_MODEL)),
        out_shape=jax.ShapeDtypeStruct((T, D_MODEL), F32),
        scratch_shapes=[pltpu.VMEM((T, D_MODEL), BF16), pltpu.VMEM((T, D_MODEL), F32),
                        pltpu.VMEM((T, LANES), F32)],
        compiler_params=_params(("arbitrary", "arbitrary")),
        name="moe_small",
    )(x, g, wr_pad, fg, wg, wu, wd)


def _router(x, g, wr_pad, tm):
    T = x.shape[0]
    assert T % tm == 0
    return pl.pallas_call(
        _router_kernel,
        grid=(T // tm,),
        in_specs=[pl.BlockSpec((tm, D_MODEL), lambda i: (i, 0)),
                  pl.BlockSpec((1, D_MODEL), lambda i: (0, 0)),
                  pl.BlockSpec((D_MODEL, LANES), lambda i: (0, 0))],
        out_specs=[pl.BlockSpec((tm, D_MODEL), lambda i: (i, 0)),
                   pl.BlockSpec((tm, LANES), lambda i: (i, 0))],
        out_shape=[jax.ShapeDtypeStruct((T, D_MODEL), F32),
                   jax.ShapeDtypeStruct((T, LANES), F32)],
        compiler_params=_params(("parallel",)),
        name="moe_router",
    )(x, g, wr_pad)


GATHER_WINDOW = 32
SC_WORKERS = 32


def _row_gather(src, idx):
    n = idx.shape[0]
    step = GATHER_WINDOW * SC_WORKERS
    n_pad = -(-n // step) * step
    if n_pad != n:
        idx = jnp.concatenate([idx, jnp.zeros((n_pad - n,), idx.dtype)])
    width = src.shape[1]
    per_worker = n_pad // SC_WORKERS
    mesh = plsc.VectorSubcoreMesh(core_axis_name="core", subcore_axis_name="subcore")

    @functools.partial(pl.kernel, out_type=jax.ShapeDtypeStruct((n_pad, width), src.dtype), mesh=mesh,
                       scratch_types=[pltpu.VMEM((per_worker,), I32),
                                      pltpu.VMEM((GATHER_WINDOW, width), src.dtype)],
                       name="row_gather")
    def gather(src_hbm, idx_hbm, dst_hbm, idx_v, buf):
        worker = lax.axis_index("core") * (SC_WORKERS // 2) + lax.axis_index("subcore")
        base = worker * per_worker
        pltpu.sync_copy(idx_hbm.at[pl.ds(base, per_worker)], idx_v)

        @pl.loop(0, per_worker // GATHER_WINDOW)
        def _(j):
            pltpu.sync_copy(src_hbm.at[idx_v.at[pl.ds(j * GATHER_WINDOW, GATHER_WINDOW)]], buf)
            pltpu.sync_copy(buf, dst_hbm.at[pl.ds(base + j * GATHER_WINDOW, GATHER_WINDOW)])

    return gather(src, idx)


def _moe_ffn_kernel(be_ref, nu_ref, x_ref, wg_ref, wu_ref, wd_ref, o_ref, acc_sc, x_sc):
    b = pl.program_id(0)
    f = pl.program_id(1)
    nf = pl.num_programs(1)
    used = b < nu_ref[0]

    @pl.when(used)
    def _():
        @pl.when(f == 0)
        def _():
            acc_sc[...] = jnp.zeros_like(acc_sc)
            x_sc[...] = x_ref[...].astype(BF16)

        x = x_sc[...]
        a = jnp.dot(x, wg_ref[...].astype(BF16), preferred_element_type=F32)
        u = jnp.dot(x, wu_ref[...].astype(BF16), preferred_element_type=F32)
        acc_sc[...] += _bdot(_silu(a) * u, wd_ref[...])

        @pl.when(f == nf - 1)
        def _():
            o_ref[...] = acc_sc[...]

    @pl.when(jnp.logical_not(used) & (f == nf - 1))
    def _():
        o_ref[...] = jnp.zeros_like(o_ref)


def _moe_ffn(xb, block_e, n_used, wg, wu, wd, blk, tf):
    cap = xb.shape[0]
    nb = cap // blk
    nf = D_FF // tf

    def bsel(b, nu):
        return jnp.minimum(b, nu[0] - 1)

    def fsel(b, f, nu):
        return jnp.where(b < nu[0], f, nf - 1)

    return pl.pallas_call(
        _moe_ffn_kernel,
        grid_spec=pltpu.PrefetchScalarGridSpec(
            num_scalar_prefetch=2,
            grid=(nb, nf),
            in_specs=[
                pl.BlockSpec((blk, D_MODEL), lambda b, f, be, nu: (bsel(b, nu), 0)),
                pl.BlockSpec((None, D_MODEL, tf), lambda b, f, be, nu: (be[bsel(b, nu)], 0, fsel(b, f, nu))),
                pl.BlockSpec((None, D_MODEL, tf), lambda b, f, be, nu: (be[bsel(b, nu)], 0, fsel(b, f, nu))),
                pl.BlockSpec((None, tf, D_MODEL), lambda b, f, be, nu: (be[bsel(b, nu)], fsel(b, f, nu), 0)),
            ],
            out_specs=pl.BlockSpec((blk, D_MODEL), lambda b, f, be, nu: (b, 0)),
            scratch_shapes=[pltpu.VMEM((blk, D_MODEL), F32), pltpu.VMEM((blk, D_MODEL), BF16)],
        ),
        out_shape=jax.ShapeDtypeStruct((cap, D_MODEL), F32),
        compiler_params=_params(("arbitrary", "arbitrary")),
        name="moe_ffn",
    )(block_e, n_used, xb, wg, wu, wd)


def _moe_combine_kernel(x_ref, y1_ref, y2_ref, r_ref, g_ref, o_ref, *, final):
    r = r_ref[...]
    y = x_ref[...] + (r[:, 2:3] * y1_ref[...] + r[:, 3:4] * y2_ref[...])
    o_ref[...] = _rms(y, g_ref[...]) if final else y


def _moe_combine(x, y1, y2, route, tm, final_g):
    T = x.shape[0]
    g = jnp.ones((1, D_MODEL), F32) if final_g is None else final_g
    return pl.pallas_call(
        functools.partial(_moe_combine_kernel, final=final_g is not None),
        grid=(T // tm,),
        in_specs=[pl.BlockSpec((tm, D_MODEL), lambda i: (i, 0)),
                  pl.BlockSpec((tm, D_MODEL), lambda i: (i, 0)),
                  pl.BlockSpec((tm, D_MODEL), lambda i: (i, 0)),
                  pl.BlockSpec((tm, LANES), lambda i: (i, 0)),
                  pl.BlockSpec((1, D_MODEL), lambda i: (0, 0))],
        out_specs=pl.BlockSpec((tm, D_MODEL), lambda i: (i, 0)),
        out_shape=jax.ShapeDtypeStruct((T, D_MODEL), F32),
        compiler_params=_params(("parallel",)),
        name="moe_combine",
    )(x, y1, y2, route, g)


def _moe_plan(e_top, blk):
    T = e_top.shape[0]
    n = 2 * T
    flat_e = e_top.reshape(-1)
    onehot = (flat_e[:, None] == jnp.arange(N_EXPERTS, dtype=I32)[None, :]).astype(I32)
    csum = jnp.cumsum(onehot, axis=0)
    rank = jnp.sum((csum - onehot) * onehot, axis=1)
    counts = csum[-1]
    padded = (counts + blk - 1) // blk * blk
    ends = jnp.cumsum(padded)
    pstart = ends - padded
    dest = (pstart[flat_e] + rank).astype(I32)
    n_blocks = -(-n // blk) + N_EXPERTS
    cap = n_blocks * blk
    slot_tok = (jnp.arange(cap, dtype=I32) % T).at[dest].set(jnp.arange(n, dtype=I32) // 2)
    first = jnp.arange(n_blocks, dtype=I32) * blk
    block_e = jnp.minimum(jnp.sum((ends[None, :] <= first[:, None]).astype(I32), axis=1), N_EXPERTS - 1)
    n_used = (ends[-1] // blk).astype(I32).reshape(1)
    return dest, slot_tok, block_e, n_used


def _moe_dispatch(x, g, wr_pad, tm, blk):
    h, route = _router(x, g, wr_pad, tm)
    dest, slot_tok, block_e, n_used = _moe_plan(route[:, 0:2].astype(I32), blk)
    return dict(x=x, route=route, dest=dest, xb=_row_gather(h, slot_tok), block_e=block_e, n_used=n_used)


def _moe_experts(d, wg, wu, wd, blk, tf):
    yb = _moe_ffn(d['xb'], d['block_e'], d['n_used'], wg, wu, wd, blk, tf)
    d2 = d['dest'].reshape(-1, 2)
    return _row_gather(yb, d2[:, 0]), _row_gather(yb, d2[:, 1])


def _moe_merge(d, y1, y2, tm, final_g):
    return _moe_combine(d['x'], y1, y2, d['route'], tm, final_g)


def _norm_kernel(x_ref, g_ref, o_ref):
    o_ref[...] = _rms(x_ref[...], g_ref[...])


def _final_norm(x, g, tm):
    T = x.shape[0]
    return pl.pallas_call(
        _norm_kernel,
        grid=(T // tm,),
        in_specs=[pl.BlockSpec((tm, D_MODEL), lambda i: (i, 0)),
                  pl.BlockSpec((1, D_MODEL), lambda i: (0, 0))],
        out_specs=pl.BlockSpec((tm, D_MODEL), lambda i: (i, 0)),
        out_shape=jax.ShapeDtypeStruct((T, D_MODEL), F32),
        compiler_params=_params(("parallel",)),
        name="final_norm",
    )(x, g)


_IN_SPLITS = (D_FOX, D_FOX, D_FOX, N_HEADS, D_NSA, 6 * HEAD_DIM, 3 * N_HEADS,
              D_FOX, D_FOX, D_GLA, GLA_RANK, D_GLA)


def _reorder_w_in(w):
    offs = [0]
    for s in _IN_SPLITS:
        offs.append(offs[-1] + s)
    seg = lambda k: w[:, offs[k]:offs[k + 1]]
    fq, fk, fv, ff, nq, nkv, ng, gq, gk, gv, glr, gog = [seg(k) for k in range(12)]
    pad = jnp.zeros((w.shape[0], LANES - SM_GLR - GLA_RANK), w.dtype)
    return jnp.concatenate([fq, fk, fv, nq, nkv, gq, gk, gv, gog, ff, ng, glr, pad], axis=1)


def _rope_table(pos):
    inv = ROPE_THETA ** (-jnp.arange(ROPE_HALF, dtype=F32) / ROPE_HALF)
    ang = pos.astype(F32)[:, None] * inv[None, :]
    cos, sin = jnp.cos(ang), jnp.sin(ang)
    P = pos.shape[0]
    one = jnp.ones((P, HEAD_DIM - ROPE_DIM), F32)
    zero = jnp.zeros((P, HEAD_DIM - ROPE_DIM), F32)
    z8 = jnp.zeros((P, ROPE_HALF), F32)
    a64 = jnp.concatenate([cos, cos, one], axis=1)
    p64 = jnp.concatenate([z8, sin, zero], axis=1)
    m64 = jnp.concatenate([-sin, z8, zero], axis=1)
    i64 = jnp.ones((P, HEAD_DIM), F32)
    o64 = jnp.zeros((P, HEAD_DIM), F32)
    return jnp.concatenate([a64, a64, p64, p64, m64, m64, a64, i64, p64, o64, m64, o64], axis=1)


def _layer_mix_params(l, norm_mix_g, w_in, b_fox_f, w_cmp, pe_cmp, w_gla_gk, b_gla_gk, g_gla_norm, w_out):
    sb = jnp.zeros((1, LANES), F32).at[0, SM_FF:SM_FF + N_HEADS].set(b_fox_f[l])
    wgk = jnp.zeros((LANES, D_FOX), F32).at[SM_GLR:SM_GLR + GLA_RANK].set(w_gla_gk[l])
    w_r = _reorder_w_in(w_in[l])
    return dict(g=norm_mix_g[l].reshape(1, D_MODEL), w_r=w_r.astype(BF16), w_r32=w_r, sb=sb,
                wgk=wgk.astype(BF16), wgk32=wgk, w_out32=w_out[l],
                bgk=b_gla_gk[l].reshape(1, D_FOX), w_cmp=w_cmp[l],
                pe_cmp=pe_cmp[l].reshape(2, 1, CMP_LEN * HEAD_DIM),
                gn=g_gla_norm[l].reshape(1, GLA_DV), w_out=w_out[l].astype(BF16))


def _mix_prompt(x, B, S, l, depth, p, tab, tm, t_fox, tk_sel, tc_gla, carry):
    (fq, fkv_all, nq, nqr, rows_all, win_t, gqk, gv, gog, glog, small, small_t) = _in_projection(
        x, 0, B * S, tm, p['g'], p['w_r'], p['sb'], p['wgk'], p['bgk'], tab, S, batch=(B, S, depth), layer=l,
        prev=(carry['fkv'], carry['rows']) if carry else None)
    carry.update(fkv=fkv_all, rows=rows_all)
    cr = _fox_cumsum(small_t, min(S, Tiles.cumsum))
    o_fox = _fox_prompt(fq, fkv_all, l, cr, B, S, t_fox)
    n_blk = S // CMP_LEN
    blocks = rows_all[l, :, 0:2 * HEAD_DIM, :].reshape(B, 2, HEAD_DIM, n_blk, CMP_LEN).transpose(1, 0, 3, 4, 2)
    blocks = blocks.reshape(2, B * n_blk, CMP_LEN * HEAD_DIM)
    cmp = _compress(blocks, p['pe_cmp'], p['w_cmp'], min(Tiles.compress_rows, B * n_blk), True)
    o_nsa = _nsa_prompt(nq, nqr, small, cmp, rows_all, l, win_t, B, S, tk_sel)
    o_gla, g_state = _gla_prompt(gqk, gv, glog, B, S, tc_gla)
    x_new = _out_projection(x, o_fox, o_nsa, o_gla, gog, p['gn'], p['w_out'], tm)
    wp = min(WINDOW, S)
    return x_new, dict(small=small_t[:, 0:N_HEADS, :], win=win_t[:, :, S - wp:], g_state=g_state)


def _per_head_col(vals):
    r = lax.broadcasted_iota(I32, (SUBLANES, 1), 0)
    out = jnp.zeros((SUBLANES, 1), F32)
    for h, v in enumerate(vals):
        out = out + jnp.where(r == h, v, 0.0)
    return out


def _per_head_row(vals, width):
    grp = lax.broadcasted_iota(I32, (1, width), 1) // HEAD_DIM
    out = jnp.zeros((1, width), F32)
    for h, v in enumerate(vals):
        out = out + jnp.where(grp == h, v, 0.0)
    return out


def _head_lane_sums(row):
    grp = lax.broadcasted_iota(I32, row.shape, 1) // HEAD_DIM
    return [jnp.sum(jnp.where(grp == h, row, 0.0), axis=1, keepdims=True) for h in range(N_HEADS)]


def _cols_of(row):
    return jnp.concatenate([jnp.broadcast_to(row[:, j:j + LANES], (LANES, LANES)).T
                            for j in range(0, row.shape[1], LANES)], axis=0)


def _row_of(col):
    return jnp.concatenate([jnp.broadcast_to(col[j:j + LANES], (LANES, LANES)).T[0:1, :]
                            for j in range(0, col.shape[0], LANES)], axis=1)


def _sublane_group_sum(x):
    return jnp.sum(x.reshape(x.shape[0] // SUBLANES, SUBLANES, x.shape[1]), axis=0)


def _fold_matrix(n_pages):
    r = lax.broadcasted_iota(I32, (n_pages * SUBLANES, n_pages * N_HEADS * SUBLANES), 0)
    c = lax.broadcasted_iota(I32, (n_pages * SUBLANES, n_pages * N_HEADS * SUBLANES), 1)
    blk = c // SUBLANES
    return ((blk // N_HEADS == r // SUBLANES) & (blk % N_HEADS == r % SUBLANES)).astype(F32)


def _fox_decode_kernel(pt_ref, q_ref, kvn_ref, smn_ref, *refs, n_pages):
    del pt_ref
    kv_refs = refs[0:n_pages]
    lf_refs = refs[n_pages:2 * n_pages]
    o_ref, lf_sc, part_sc = refs[2 * n_pages:]
    R = n_pages * SUBLANES
    PG = kv_refs[0].shape[-1]
    row = pl.ds(pl.program_id(0) % SUBLANES, 1)
    q_row = q_ref[row, :]
    kvn = kvn_ref[row, :]
    smn = smn_ref[row, :]
    q_cols = _cols_of(q_row)

    lf_sc[...] = jnp.zeros_like(lf_sc)
    for p in range(n_pages):
        lf_sc[p * SUBLANES:p * SUBLANES + N_HEADS, :] = lf_refs[p][...]
    lft = lf_sc[...]
    k0 = lax.broadcasted_iota(I32, (PG, PG), 0)
    k1 = lax.broadcasted_iota(I32, (PG, PG), 1)
    within = _dot01_r(lft, k0 > k1)
    tot = jnp.broadcast_to(jnp.sum(lft, axis=1, keepdims=True), (R, PG))
    r0 = lax.broadcasted_iota(I32, (R, R), 0)
    r1 = lax.broadcasted_iota(I32, (R, R), 1)
    later = (r1 % SUBLANES == r0 % SUBLANES) & (r1 // SUBLANES > r0 // SUBLANES)
    cross = _dot01(later, tot)
    rr = lax.broadcasted_iota(I32, (R, 1), 0) % SUBLANES
    newcol = jnp.zeros((R, 1), F32)
    for h in range(N_HEADS):
        newcol = newcol + jnp.where(rr == h, smn[:, SM_FF + h:SM_FF + h + 1], 0.0)
    bias = (within + cross + newcol).reshape(n_pages, SUBLANES, PG)

    for p in range(n_pages):
        for h in range(N_HEADS):
            g = p * N_HEADS + h
            part_sc[g * SUBLANES:(g + 1) * SUBLANES, :] = _sublane_group_sum(
                kv_refs[p][0, h] * q_cols[h * HEAD_DIM:(h + 1) * HEAD_DIM])
    s = _dot01(_fold_matrix(n_pages), part_sc[...])
    s3 = s.reshape(n_pages, SUBLANES, PG) * SCALE + bias
    s_new = _per_head_col(_head_lane_sums(q_row * kvn[:, 0:D_FOX])) * SCALE
    m = jnp.max(jnp.max(s3, axis=2, keepdims=True), axis=0)
    m = jnp.maximum(m, s_new)
    p3 = jnp.exp(s3 - m[None])
    pn = jnp.exp(s_new - m)
    inv = 1.0 / (jnp.sum(jnp.sum(p3, axis=2, keepdims=True), axis=0) + pn)
    o_cols = []
    for h in range(N_HEADS):
        acc = jnp.zeros((HEAD_DIM, PG), F32)
        for p in range(n_pages):
            acc = acc + kv_refs[p][1, h] * p3[p, h:h + 1, :]
        o_cols.append(jnp.sum(acc, axis=1, keepdims=True) * inv[h:h + 1])
    w_new = _per_head_row([pn[h:h + 1] * inv[h:h + 1] for h in range(N_HEADS)], D_FOX)
    o_ref[row, :] = _row_of(jnp.concatenate(o_cols, axis=0)) + w_new * kvn[:, D_FOX:2 * D_FOX]


def _fox_decode(l, pt_flat, n_pages, fq, fkv, small, kv_cache_t, lft_cache):
    DB = fq.shape[0]
    PG = kv_cache_t.shape[-1]
    page = lambda p, nz: (lambda b, pt: (l, pt[b * n_pages + p]) + (0,) * nz)
    rows8 = lambda w: pl.BlockSpec((SUBLANES, w), lambda b, pt: (b // SUBLANES, 0))
    return pl.pallas_call(
        functools.partial(_fox_decode_kernel, n_pages=n_pages),
        grid_spec=pltpu.PrefetchScalarGridSpec(
            num_scalar_prefetch=1,
            grid=(DB,),
            in_specs=[rows8(D_FOX), rows8(2 * D_FOX), rows8(LANES)]
            + [pl.BlockSpec((None, None, 2, N_HEADS, HEAD_DIM, PG), page(p, 4)) for p in range(n_pages)]
            + [pl.BlockSpec((None, None, N_HEADS, PG), page(p, 2)) for p in range(n_pages)],
            out_specs=rows8(D_FOX),
            scratch_shapes=[pltpu.VMEM((n_pages * SUBLANES, PG), F32),
                            pltpu.VMEM((n_pages * N_HEADS * SUBLANES, PG), F32)],
        ),
        out_shape=jax.ShapeDtypeStruct((DB, D_FOX), F32),
        compiler_params=_params(("arbitrary",)),
        name="fox_decode",
    )(pt_flat, fq, fkv, small, *([kv_cache_t] * n_pages), *([lft_cache] * n_pages))


def _nsa_decode_kernel(pt_ref, q_ref, qr_ref, rown_ref, winn_ref, sm_ref, win_ref, *refs,
                       n_pages, past_len):
    del pt_ref
    pg_refs = refs[0:n_pages]
    cmp_refs = refs[n_pages:2 * n_pages]
    o_ref, nw_ref, cmp_sc, qc_sc, part_sc, sw_sc = refs[-6:]
    R = n_pages * SUBLANES
    PG = pg_refs[0].shape[-1]
    WB = win_ref.shape[-1]
    per_page = PG // CMP_LEN
    assert per_page <= SUBLANES and PG == 2 * SEL_LEN and R == LANES
    jt = past_len // SEL_LEN
    row = pl.ds(pl.program_id(0) % SUBLANES, 1)
    q_row = q_ref[row, :]
    qr_row = qr_ref[row, :]
    rown = rown_ref[row, :]
    winn = winn_ref[row, :]
    smn = sm_ref[row, :]
    qr_cols = _cols_of(qr_row)
    rep4 = lambda r64: jnp.concatenate([r64] * N_HEADS, axis=1)

    qc_sc[...] = jnp.zeros_like(qc_sc)
    for h in range(N_HEADS):
        qc_sc[h:h + 1, 0:HEAD_DIM] = q_row[:, h * HEAD_DIM:(h + 1) * HEAD_DIM]
    head_row = lax.broadcasted_iota(I32, (SUBLANES, 1), 0) < N_HEADS

    cmp_sc[...] = jnp.zeros_like(cmp_sc)
    for p in range(n_pages):
        cmp_sc[p * SUBLANES:p * SUBLANES + per_page, :] = cmp_refs[p][...]
    cmpa = cmp_sc[...]
    lane = lax.broadcasted_iota(I32, (1, R), 1)
    blk = per_page * (lane // SUBLANES) + lane % SUBLANES
    complete = (lane % SUBLANES < per_page) & ((blk + 1) * CMP_LEN - 1 <= past_len)
    s = _dot_nt_hilo(qc_sc[...], cmpa) * SCALE
    s = jnp.where(complete, s, NEG)
    e = jnp.exp(s - jnp.max(s, axis=-1, keepdims=True))
    pc = e / jnp.sum(e, axis=-1, keepdims=True) * complete.astype(F32)
    vcb_t = cmpa.T[HEAD_DIM:2 * HEAD_DIM, :]
    o_cmp = [jnp.sum(vcb_t * pc[h:h + 1, :], axis=1, keepdims=True) for h in range(N_HEADS)]

    imp_c = jnp.sum(jnp.where(head_row, pc, 0.0), axis=0, keepdims=True)
    imp_s = imp_c + pltpu.roll(imp_c, R - 1, 1)
    cand = (lane % SUBLANES == 0) | (lane % SUBLANES == 2)
    jsel = 2 * (lane // SUBLANES) + (lane % SUBLANES) // 2
    score = jnp.where(jsel == jt, 2.0 * SEL_FORCE,
                      jnp.where((jsel == 0) | (jsel == jt - 1), SEL_FORCE,
                                jnp.where(jsel <= jt, imp_s + 0.0, -1.0)))
    score_b = jnp.broadcast_to(score, (R, R))
    key_row = _order_key(score_b)
    key_col = _order_key(score_b.T)
    l0 = lax.broadcasted_iota(I32, (R, R), 0)
    l1 = lax.broadcasted_iota(I32, (R, R), 1)
    cand_col = (l0 % SUBLANES == 0) | (l0 % SUBLANES == 2)
    beats = cand_col & (key_col > jnp.where(l0 < l1, key_row - 1, key_row))
    cnt = jnp.sum(beats.astype(I32), axis=0, keepdims=True)
    sel_row = (cand & (cnt < TOP_N - 1)).astype(F32)
    sel_col = jnp.broadcast_to(sel_row, (R, R)).T
    half = ((l0 % SUBLANES == 0) & (l1 < SEL_LEN)) | ((l0 % SUBLANES == 2) & (l1 >= SEL_LEN))
    z = jnp.where(half, sel_col, 0.0)
    same_page = (l1 // SUBLANES == l0 // SUBLANES).astype(BF16)
    picked = jnp.dot(same_page, z.astype(BF16), preferred_element_type=F32)
    picked = picked.reshape(n_pages, SUBLANES, PG) > 0.5

    for p in range(n_pages):
        ks_t = pg_refs[p][2]
        for h in range(N_HEADS):
            g = p * N_HEADS + h
            part_sc[g * SUBLANES:(g + 1) * SUBLANES, :] = _sublane_group_sum(
                ks_t * qr_cols[h * HEAD_DIM:(h + 1) * HEAD_DIM])
    s = _dot01(_fold_matrix(n_pages), part_sc[...])
    s3 = jnp.where(picked, s.reshape(n_pages, SUBLANES, PG) * SCALE, NEG)
    s_new = _per_head_col(_head_lane_sums(qr_row * rep4(rown[:, 2 * HEAD_DIM:3 * HEAD_DIM]))) * SCALE
    m = jnp.maximum(jnp.max(jnp.max(s3, axis=2, keepdims=True), axis=0), s_new)
    p3 = jnp.exp(s3 - m[None])
    pn = jnp.exp(s_new - m)
    inv = 1.0 / (jnp.sum(jnp.sum(p3, axis=2, keepdims=True), axis=0) + pn)
    o_sel = []
    for h in range(N_HEADS):
        acc = jnp.zeros((HEAD_DIM, PG), F32)
        for p in range(n_pages):
            acc = acc + pg_refs[p][3] * p3[p, h:h + 1, :]
        o_sel.append(jnp.sum(acc, axis=1, keepdims=True) * inv[h:h + 1])

    kw_t = win_ref[0]
    vw_t = win_ref[1]
    wlane = lax.broadcasted_iota(I32, (1, WB), 1)
    wpos = past_len - WB + wlane
    wd = past_len - wpos
    wok = (wd >= 0) & (wd < WINDOW) & (wpos >= 0)
    sw_sc[...] = jnp.zeros_like(sw_sc)
    for h in range(N_HEADS):
        qh = qr_cols[h * HEAD_DIM:(h + 1) * HEAD_DIM]
        sw_sc[h:h + 1, :] = jnp.sum(kw_t * jnp.concatenate([qh] * (WB // LANES), axis=1), axis=0, keepdims=True)
    sw = jnp.where(wok, sw_sc[...] * SCALE, NEG)
    sw_new = _per_head_col(_head_lane_sums(qr_row * rep4(winn[:, 0:HEAD_DIM]))) * SCALE
    mw = jnp.maximum(jnp.max(sw, axis=-1, keepdims=True), sw_new)
    ew = jnp.exp(sw - mw)
    en = jnp.exp(sw_new - mw)
    invw = 1.0 / (jnp.sum(ew, axis=-1, keepdims=True) + en)

    gate = lambda h, c: smn[:, SM_NG + 3 * h + c:SM_NG + 3 * h + c + 1]
    o_cols = []
    for h in range(N_HEADS):
        o_win = jnp.sum(vw_t * ew[h:h + 1, :], axis=1, keepdims=True) * invw[h:h + 1]
        o_cols.append(gate(h, 0) * o_cmp[h] + gate(h, 1) * o_sel[h] + gate(h, 2) * o_win)
    w_sel = _per_head_row([gate(h, 1) * pn[h:h + 1] * inv[h:h + 1] for h in range(N_HEADS)], D_NSA)
    w_win = _per_head_row([gate(h, 2) * en[h:h + 1] * invw[h:h + 1] for h in range(N_HEADS)], D_NSA)
    o_ref[row, :] = (_row_of(jnp.concatenate(o_cols, axis=0))
                     + w_sel * rep4(rown[:, 3 * HEAD_DIM:4 * HEAD_DIM])
                     + w_win * rep4(winn[:, HEAD_DIM:2 * HEAD_DIM]))
    last = lax.broadcasted_iota(I32, (HEAD_DIM, WB), 1) == WB - 1
    winn_cols = _cols_of(winn)
    for s in range(2):
        new_col = winn_cols[s * HEAD_DIM:(s + 1) * HEAD_DIM, 0:1]
        nw_ref[s] = jnp.where(last, new_col, pltpu.roll(win_ref[s], WB - 1, 1))


def _nsa_decode(l, pt_flat, n_pages, past_len, nq, nqr, rows, win, small, nsa_cache_t, cmp_pool, win_state_t,
                prev):
    DB = nq.shape[0]
    PG = nsa_cache_t.shape[-1]
    WB = win_state_t.shape[-1]
    page = lambda p: (lambda b, pt: (l, pt[b * n_pages + p], 0, 0, 0))
    cpage = lambda p: (lambda b, pt: (pt[b * n_pages + p], 0, 0))
    carried = [] if prev is None else [prev]
    rows8 = lambda w: pl.BlockSpec((SUBLANES, w), lambda b, pt: (b // SUBLANES, 0))
    return pl.pallas_call(
        functools.partial(_nsa_decode_kernel, n_pages=n_pages, past_len=past_len),
        grid_spec=pltpu.PrefetchScalarGridSpec(
            num_scalar_prefetch=1,
            grid=(DB,),
            in_specs=[rows8(D_NSA), rows8(D_NSA), rows8(4 * HEAD_DIM), rows8(2 * HEAD_DIM), rows8(LANES),
                      pl.BlockSpec((None, None, 2, HEAD_DIM, WB), lambda b, pt: (l, b, 0, 0, 0))]
            + [pl.BlockSpec((None, None, 4, HEAD_DIM, PG), page(p)) for p in range(n_pages)]
            + [pl.BlockSpec((None, PG // CMP_LEN, LANES), cpage(p)) for p in range(n_pages)]
            + [pl.BlockSpec(memory_space=pl.ANY)] * len(carried),
            out_specs=[rows8(D_NSA),
                       pl.BlockSpec((None, None, 2, HEAD_DIM, WB), lambda b, pt: (l, b, 0, 0, 0))],
            scratch_shapes=[pltpu.VMEM((n_pages * SUBLANES, LANES), F32),
                            pltpu.VMEM((SUBLANES, LANES), F32),
                            pltpu.VMEM((n_pages * N_HEADS * SUBLANES, PG), F32),
                            pltpu.VMEM((SUBLANES, WB), F32)],
        ),
        out_shape=[jax.ShapeDtypeStruct((DB, D_NSA), F32),
                   jax.ShapeDtypeStruct(win_state_t.shape, F32)],
        input_output_aliases={7 + 2 * n_pages: 1} if carried else {},
        compiler_params=_params(("arbitrary",)),
        name="nsa_decode",
    )(pt_flat, nq, nqr, rows, win, small, win_state_t,
      *([nsa_cache_t] * n_pages), *([cmp_pool] * n_pages), *carried)


def _gla_decode_kernel(qk_ref, g_ref, v_ref, s_ref, *rest):
    o_ref, so_ref = rest[-2:]
    for j in range(SUBLANES):
        qk = qk_ref[j:j + 1, :]
        q_cols = _cols_of(qk[:, 0:D_FOX] * SCALE)
        k_cols = _cols_of(qk[:, D_FOX:2 * D_FOX])
        decay = jnp.exp(_cols_of(g_ref[j:j + 1, :]))
        for h in range(N_HEADS):
            hs = slice(h * HEAD_DIM, (h + 1) * HEAD_DIM)
            v_row = v_ref[j:j + 1, h * GLA_DV:(h + 1) * GLA_DV]
            s_new = decay[hs] * s_ref[j, h] + k_cols[hs] * v_row
            so_ref[j, h] = s_new
            o_ref[j:j + 1, h * GLA_DV:(h + 1) * GLA_DV] = jnp.sum(q_cols[hs] * s_new, axis=0, keepdims=True)


def _gla_decode(l, gqk, gv, glog, state, prev):
    depth, DB = state.shape[0:2]
    nb = SUBLANES
    rows = lambda w: pl.BlockSpec((nb, w), lambda i: (i, 0))
    sspec = pl.BlockSpec((None, nb, N_HEADS, HEAD_DIM, GLA_DV), lambda i: (l, i, 0, 0, 0))
    carried = [] if prev is None else [prev]
    return pl.pallas_call(
        _gla_decode_kernel,
        grid=(DB // nb,),
        in_specs=[rows(2 * D_FOX), rows(D_FOX), rows(D_GLA), sspec]
        + [pl.BlockSpec(memory_space=pl.ANY)] * len(carried),
        out_specs=[rows(D_GLA), sspec],
        out_shape=[jax.ShapeDtypeStruct((DB, D_GLA), F32), jax.ShapeDtypeStruct(state.shape, F32)],
        input_output_aliases={4: 1} if carried else {},
        compiler_params=_params(("parallel",)),
        name="gla_decode",
    )(gqk, glog, gv, state, *carried)


def _mix_sample(x, l, p, tab, pt_flat, n_pages, past_len, fox_kv_t, fox_lft_c, nsa_t, cmp_blocks,
                win_state_t, gla_state, carry, tie=None):
    DB = x.shape[0]
    (fq, fkv, nq, nqr, rows, win, gqk, gv, gog, glog, small) = _in_projection(
        x, 0, DB, DB, p['g'], p['w_r32'], p['sb'], p['wgk32'], p['bgk'], tab, DB)
    o_fox = _fox_decode(l, pt_flat, n_pages, fq, fkv, small, fox_kv_t, fox_lft_c)
    if tie is not None:
        o_fox, tie = lax.optimization_barrier((o_fox, tie))
    n_pool, PG = nsa_t.shape[1], nsa_t.shape[-1]
    per_layer = n_pool * (PG // CMP_LEN)
    cmp_pool = _compress(cmp_blocks, p['pe_cmp'], p['w_cmp'], Tiles.compress_rows, False,
                         row0=l * per_layer, n_rows=per_layer)
    cmp_pool = cmp_pool.reshape(n_pool, PG // CMP_LEN, LANES)
    o_nsa, new_win = _nsa_decode(l, pt_flat, n_pages, past_len, nq, nqr, rows, win, small,
                                 nsa_t, cmp_pool, win_state_t, carry.get('win'))
    o_gla, g_state = _gla_decode(l, gqk, gv, glog, gla_state, carry.get('gla'))
    carry.update(win=new_win, gla=g_state)
    x_new = _out_projection(x, o_fox, o_nsa, o_gla, gog, p['gn'], p['w_out32'], DB)
    return x_new, dict(fkv=fkv, small=small, rows=rows), tie


def kernel(x_prompt, x_sample, cache_fox_kv, cache_fox_logf, cache_nsa_kv, state_nsa_win, state_gla,
           page_table, norm_mix_g, w_in, b_fox_f, w_cmp, pe_cmp, w_gla_gk, b_gla_gk, g_gla_norm, w_out,
           norm_ffn_g, dense_w_gate, dense_w_up, dense_w_down, moe_w_router, moe_w_gate, moe_w_up,
           moe_w_down, final_norm_g):
    B, S, _ = x_prompt.shape
    DB, TN, _ = x_sample.shape
    assert TN == 1
    depth, n_pool, PG = cache_fox_kv.shape[0:3]
    n_pages = page_table.shape[1]
    past_len = n_pages * PG
    WB = state_nsa_win.shape[2]
    xp = x_prompt.reshape(B * S, D_MODEL)
    xs = x_sample.reshape(DB, D_MODEL)
    tab_p = _rope_table(jnp.arange(S))
    tab_s = _rope_table(jnp.full((DB,), past_len, I32))
    pt_flat = page_table.reshape(-1).astype(I32)
    fox_kv_t = jnp.transpose(cache_fox_kv, (0, 1, 3, 4, 5, 2))
    fox_lft_c = jnp.swapaxes(cache_fox_logf, 2, 3)
    nsa_t = jnp.transpose(cache_nsa_kv, (0, 1, 3, 4, 2))
    win_state_t = jnp.transpose(state_nsa_win, (0, 1, 3, 4, 2))
    n_cmp = depth * n_pool * (PG // CMP_LEN)
    cmp_blocks = _block_major(cache_nsa_kv[:, :, :, 0:2, :].reshape(n_cmp * CMP_LEN, 2, HEAD_DIM), n_cmp)
    gfin = final_norm_g.reshape(1, D_MODEL)
    cp, cs = [], []
    carry_p, carry_s = {}, {}
    for l in range(depth):
        p = _layer_mix_params(l, norm_mix_g, w_in, b_fox_f, w_cmp, pe_cmp, w_gla_gk, b_gla_gk,
                              g_gla_norm, w_out)
        gf = norm_ffn_g[l].reshape(1, D_MODEL)
        i = l // 2
        moe = l % 2 == 1
        xp, c = _mix_prompt(xp, B, S, l, depth, p, tab_p, Tiles.prompt_rows, Tiles.fox, Tiles.nsa_keys,
                            Tiles.gla_rows, carry_p)
        cp.append(c)
        if moe:
            wr = jnp.zeros((D_MODEL, LANES), F32).at[:, 0:N_EXPERTS].set(moe_w_router[i])
            experts = (moe_w_gate[i], moe_w_up[i], moe_w_down[i])
            disp_p = _moe_dispatch(xp, gf, wr, Tiles.prompt_rows, Tiles.expert_block)
        xs, c, tied = _mix_sample(xs, l, p, tab_s, pt_flat, n_pages, past_len, fox_kv_t, fox_lft_c, nsa_t,
                                  cmp_blocks, win_state_t, state_gla, carry_s, disp_p['xb'] if moe else None)
        cs.append(c)
        if moe:
            disp_p['xb'] = tied
        if moe:
            fin = gfin if l == depth - 1 else None
            y_p = _moe_experts(disp_p, *experts, Tiles.expert_block, Tiles.ffn_cols)
            xs = _moe_small(xs, gf, wr, *experts, Tiles.ffn_cols_sample, fin)
            xp = _moe_merge(disp_p, *y_p, Tiles.prompt_rows, fin)
        else:
            wg, wu, wd = (dense_w_gate[i].astype(BF16), dense_w_up[i].astype(BF16),
                          dense_w_down[i].astype(BF16))
            xp = _dense_ffn(xp, gf, wg, wu, wd, Tiles.dense_rows, Tiles.ffn_cols)
            xs = _dense_ffn(xs, gf, dense_w_gate[i], dense_w_up[i], dense_w_down[i], DB, Tiles.ffn_cols_sample)
    if depth % 2 == 1:
        xp = _final_norm(xp, gfin, Tiles.prompt_rows)
        xs = _final_norm(xs, gfin, DB)
    y_p = xp.reshape(B, S, D_MODEL)
    y_s = xs.reshape(DB, 1, D_MODEL)
    wp = min(WINDOW, S)
    st = lambda key, group: jnp.stack([c[key] for c in group])
    return (y_p, y_s,
            carry_p['fkv'].reshape(depth, B, 2, N_HEADS, HEAD_DIM, S).transpose(0, 1, 5, 2, 3, 4),
            st('small', cp).transpose(0, 1, 3, 2),
            carry_p['rows'].reshape(depth, B, 4, HEAD_DIM, S).transpose(0, 1, 4, 2, 3),
            st('win', cp).reshape(depth, B, 2, HEAD_DIM, wp).transpose(0, 1, 4, 2, 3),
            st('g_state', cp),
            st('fkv', cs).reshape(depth, DB, 1, 2, N_HEADS, HEAD_DIM),
            st('small', cs)[:, :, SM_FF:SM_FF + N_HEADS].reshape(depth, DB, 1, N_HEADS),
            st('rows', cs).reshape(depth, DB, 1, 4, HEAD_DIM),
            jnp.transpose(carry_s['win'], (0, 1, 4, 2, 3)),
            carry_s['gla'])
```

```python
import functools

import jax
import jax.numpy as jnp
from jax import lax
from jax.experimental import pallas as pl
from jax.experimental.pallas import tpu as pltpu
from jax.experimental.pallas import tpu_sc as plsc

F32 = jnp.float32
BF16 = jnp.bfloat16
I32 = jnp.int32
HI = lax.Precision.HIGHEST

D_MODEL = 1024
HEAD_DIM = 64
N_HEADS = 4
D_FOX = N_HEADS * HEAD_DIM
D_NSA = N_HEADS * HEAD_DIM
GLA_DV = 128
D_GLA = N_HEADS * GLA_DV
GLA_RANK = 16
GLA_TAU = 16.0
GLA_CHUNK = 64
CMP_LEN = 32
SEL_LEN = 64
TOP_N = 16
WINDOW = 512
ROPE_THETA = 500000.0
ROPE_DIM = HEAD_DIM // 4
ROPE_HALF = ROPE_DIM // 2
D_FF = 3584
N_EXPERTS = 8
EPS = 1e-6
SEL_FORCE = 1e9
NEG = -1e30
SCALE = HEAD_DIM ** -0.5
LOG2E = 1.4426950408889634

LANES = 128
SUBLANES = 8
VMEM_BYTES_V7X = 64 * 1024 * 1024
VMEM_LIMIT = VMEM_BYTES_V7X - 8 * 1024 * 1024


class Tiles:
    prompt_rows = 512
    fox = 512
    nsa_keys = 512
    gla_rows = 256
    cumsum = 512
    compress_rows = 256
    dense_rows = 1024
    ffn_cols = 512
    ffn_cols_sample = 896
    expert_block = 1024

C_FQ = 0
C_FKV = 256
C_NQ = 768
C_NKV = 1024
C_GQK = 1408
C_GV = 1920
C_GOG = 2432
C_SMALL = 2944
C_END = 3072
SM_FF = 0
SM_NG = 4
SM_GLR = 16

NT = (((1,), (1,)), ((), ()))


def _params(sem):
    return pltpu.CompilerParams(dimension_semantics=sem, vmem_limit_bytes=VMEM_LIMIT)


def _rms(x, g):
    ms = jnp.mean(x * x, axis=-1, keepdims=True)
    return x * lax.rsqrt(ms + EPS) * g


def _sigmoid(x):
    return 1.0 / (1.0 + jnp.exp(-x))


def _log_sigmoid(x):
    return -(jnp.maximum(-x, 0.0) + jnp.log1p(jnp.exp(-jnp.abs(x))))


def _silu(x):
    return x * _sigmoid(x)


def _bdot(a, b):
    return jnp.dot(a.astype(BF16), b.astype(BF16), preferred_element_type=F32)


def _split3(x):
    h = x.astype(BF16)
    r = x - h.astype(F32)
    m = r.astype(BF16)
    return h, m, (r - m.astype(F32)).astype(BF16)


def _dot01(m01, x):
    mb = m01.astype(BF16)
    h, m, l = _split3(x)
    return (jnp.dot(mb, h, preferred_element_type=F32) + jnp.dot(mb, m, preferred_element_type=F32)
            + jnp.dot(mb, l, preferred_element_type=F32))


def _dot01_r(x, m01):
    mb = m01.astype(BF16)
    h, m, l = _split3(x)
    return (jnp.dot(h, mb, preferred_element_type=F32) + jnp.dot(m, mb, preferred_element_type=F32)
            + jnp.dot(l, mb, preferred_element_type=F32))


def _dot01_nt(m01, x):
    mb = m01.astype(BF16)
    h, m, l = _split3(x)
    return (lax.dot_general(mb, h, NT, preferred_element_type=F32)
            + lax.dot_general(mb, m, NT, preferred_element_type=F32)
            + lax.dot_general(mb, l, NT, preferred_element_type=F32))


def _dot_nt_hilo(a, b):
    ah = a.astype(BF16)
    al = (a - ah.astype(F32)).astype(BF16)
    bh = b.astype(BF16)
    bl = (b - bh.astype(F32)).astype(BF16)
    return (lax.dot_general(ah, bh, NT, preferred_element_type=F32)
            + lax.dot_general(al, bh, NT, preferred_element_type=F32)
            + lax.dot_general(ah, bl, NT, preferred_element_type=F32))


def _wdot(a, w):
    if w.dtype == F32:
        return jnp.dot(a.astype(F32), w, precision=HI, preferred_element_type=F32)
    return jnp.dot(a.astype(BF16), w, preferred_element_type=F32)


def _rope128(x, a, bp, bm):
    return x * a + pltpu.roll(x, ROPE_HALF, 1) * bp + pltpu.roll(x, LANES - ROPE_HALF, 1) * bm


def _inproj_kernel(x_ref, g_ref, w_ref, sb_ref, wgk_ref, bgk_ref, tab_ref, *refs, feature_major, n_carried):
    (fq_ref, fkv_ref, nq_ref, nqr_ref, rows_ref, win_ref,
     gqk_ref, gv_ref, gog_ref, glog_ref, small_ref, *extra) = refs[n_carried:]
    h = _rms(x_ref[...], g_ref[...]).astype(w_ref.dtype)

    def put(ref, v):
        v = v.T if feature_major else v
        if len(ref.shape) == 3:
            for d in range(ref.shape[0]):
                ref[d] = v
        else:
            ref[...] = v

    def mm(a, b):
        return _wdot(h, w_ref[:, a:b])

    fq_ref[...] = mm(C_FQ, C_FKV)
    put(fkv_ref, mm(C_FKV, C_NQ))
    tab = tab_ref[...]
    ab, pb, mb = tab[:, 0:128], tab[:, 128:256], tab[:, 256:384]
    af, pf, mf = tab[:, 384:512], tab[:, 512:640], tab[:, 640:768]
    nq = mm(C_NQ, C_NKV)
    nq_ref[...] = nq
    nqr_ref[:, 0:128] = _rope128(nq[:, 0:128], ab, pb, mb)
    nqr_ref[:, 128:256] = _rope128(nq[:, 128:256], ab, pb, mb)
    nkv = mm(C_NKV, C_GQK)
    put(rows_ref, jnp.concatenate([nkv[:, 0:128], _rope128(nkv[:, 128:256], af, pf, mf)], axis=1))
    put(win_ref, _rope128(nkv[:, 256:384], af, pf, mf))
    gqk_ref[...] = mm(C_GQK, C_GV)
    gv_ref[...] = mm(C_GV, C_GOG)
    gog_ref[...] = mm(C_GOG, C_SMALL)
    sm = mm(C_SMALL, C_END)
    glog_ref[...] = _log_sigmoid(_wdot(sm, wgk_ref[...]) + bgk_ref[...]) * (1.0 / GLA_TAU)
    smb = sm + sb_ref[...]
    lane = lax.broadcasted_iota(I32, smb.shape, 1)
    small = jnp.where(lane < SM_NG, _log_sigmoid(smb), _sigmoid(smb))
    small_ref[...] = small
    if feature_major:
        extra[0][...] = small.T[0:SUBLANES, :]


def _in_projection(x_all, row0, n_rows, tm, g, w_r, sb, wgk, bgk, tab, tab_period, batch=None, layer=None,
                   prev=None):
    assert n_rows % tm == 0 and row0 % tm == 0 and tab_period % tm == 0
    nt = n_rows // tm
    b0 = row0 // tm
    npd = tab_period // tm
    widths = (256, 512, 256, 256, 256, 128, 512, 512, 512, 256, 128)
    FKV, ROWS, WIN = 1, 4, 5
    full = lambda shape: pl.BlockSpec(shape, lambda i: (0, 0))
    row_spec = lambda w: pl.BlockSpec((tm, w), lambda i: (i, 0))
    carried = [] if prev is None else list(prev)
    if batch is not None:
        B, S, depth = batch
        per = S // tm
        assert n_rows == B * S and S % tm == 0
        t_spec = lambda w: pl.BlockSpec((None, w, tm), lambda i: (i // per, 0, i % per))
        if carried:
            l_spec = lambda w: pl.BlockSpec((None, None, w, tm), lambda i: (layer, i // per, 0, i % per))
        else:
            l_spec = lambda w: pl.BlockSpec((depth, None, w, tm), lambda i: (0, i // per, 0, i % per))
        out_specs = [l_spec(w) if k in (FKV, ROWS) else t_spec(w) if k == WIN else row_spec(w)
                     for k, w in enumerate(widths)] + [t_spec(SUBLANES)]
        out_shape = [jax.ShapeDtypeStruct((depth, B, w, S) if k in (FKV, ROWS) else (B, w, S) if k == WIN
                                          else (n_rows, w), F32)
                     for k, w in enumerate(widths)] + [jax.ShapeDtypeStruct((B, SUBLANES, S), F32)]
    else:
        out_specs = [row_spec(w) for w in widths]
        out_shape = [jax.ShapeDtypeStruct((n_rows, w), F32) for w in widths]
    return pl.pallas_call(
        functools.partial(_inproj_kernel, feature_major=batch is not None, n_carried=len(carried)),
        grid=(nt,),
        in_specs=[
            pl.BlockSpec((tm, D_MODEL), lambda i: (b0 + i, 0)),
            full((1, D_MODEL)),
            full((D_MODEL, C_END)),
            full((1, LANES)),
            full((LANES, 256)),
            full((1, 256)),
            pl.BlockSpec((tm, 768), lambda i: (i % npd, 0)),
        ] + [pl.BlockSpec(memory_space=pl.ANY)] * len(carried),
        out_specs=out_specs,
        out_shape=out_shape,
        input_output_aliases={7: FKV, 8: ROWS} if carried else {},
        compiler_params=_params(("parallel",)),
        name="in_projection",
    )(x_all, g, w_r, sb, wgk, bgk, tab, *carried)


def _cumsum_kernel(sm_ref, cr_ref, carry):
    t = pl.program_id(1)
    ts = sm_ref.shape[1]

    @pl.when(t == 0)
    def _():
        carry[...] = jnp.zeros_like(carry)

    r = lax.broadcasted_iota(I32, (ts, ts), 0)
    c = lax.broadcasted_iota(I32, (ts, ts), 1)
    cs = _dot01_r(sm_ref[...], r <= c) + carry[...]
    carry[...] = cs[:, ts - 1:ts]
    cr_ref[...] = cs * LOG2E


def _fox_cumsum(small_t, ts):
    B, _, S = small_t.shape
    spec = pl.BlockSpec((None, SUBLANES, ts), lambda b, t: (b, 0, t))
    return pl.pallas_call(
        _cumsum_kernel,
        grid=(B, S // ts),
        in_specs=[spec],
        out_specs=spec,
        out_shape=jax.ShapeDtypeStruct((B, SUBLANES, S), F32),
        scratch_shapes=[pltpu.VMEM((SUBLANES, 1), F32)],
        compiler_params=_params(("parallel", "arbitrary")),
        name="fox_cumsum",
    )(small_t)


def _pair_mask(shape, h, axis=1):
    return (lax.broadcasted_iota(I32, shape, axis) // HEAD_DIM) == (h % 2)


def _fox_prompt_kernel(qi_ref, kj_ref, q_ref, kv_ref, cr_ref, o_ref, *scratch):
    i = qi_ref[pl.program_id(1)]
    j = kj_ref[pl.program_id(1)]
    tq = q_ref.shape[0]
    tk = kv_ref.shape[1]
    q_sc, m_sc, acc_sc = scratch[0:N_HEADS], scratch[N_HEADS:2 * N_HEADS], scratch[2 * N_HEADS:]

    @pl.when(j == 0)
    def _():
        for h in range(N_HEADS):
            m_sc[h][...] = jnp.full_like(m_sc[h], NEG)
            acc_sc[h][...] = jnp.zeros_like(acc_sc[h])
            slab = q_ref[:, (h // 2) * LANES:(h // 2 + 1) * LANES] * (SCALE * LOG2E)
            q_sc[h][...] = jnp.where(_pair_mask(slab.shape, h), slab, 0.0).astype(BF16)

    def tile(diagonal):
        k_slabs = [kv_ref[c0:c0 + LANES, :].astype(BF16) for c0 in (0, LANES)]
        if diagonal:
            mask = lax.broadcasted_iota(I32, (1, tk), 1) <= lax.broadcasted_iota(I32, (tq, 1), 0)
        for h in range(N_HEADS):
            v_slab = kv_ref[D_FOX + (h // 2) * LANES:D_FOX + (h // 2 + 1) * LANES, :]
            v_aug = jnp.where(_pair_mask(v_slab.shape, h, 0), v_slab, 1.0).astype(BF16)
            s = jnp.dot(q_sc[h][...], k_slabs[h // 2], preferred_element_type=F32) - cr_ref[h:h + 1, :]
            if diagonal:
                s = jnp.where(mask, s, NEG)
            m_old = m_sc[h][...]
            m_new = jnp.maximum(m_old, jnp.max(s, axis=-1, keepdims=True))
            p = jnp.exp2(s - m_new).astype(BF16)
            acc_sc[h][...] = (jnp.exp2(m_old - m_new) * acc_sc[h][...]
                              + lax.dot_general(p, v_aug, NT, preferred_element_type=F32))
            m_sc[h][...] = m_new

    @pl.when(j < i)
    def _():
        tile(False)

    @pl.when(j == i)
    def _():
        tile(True)
        for h in range(N_HEADS):
            a = acc_sc[h][...]
            lo = (h % 2) * HEAD_DIM
            den = a[:, HEAD_DIM - lo:HEAD_DIM - lo + 1]
            o_ref[:, h * HEAD_DIM:(h + 1) * HEAD_DIM] = a[:, lo:lo + HEAD_DIM] / den


def _fox_prompt(fq, fkv_all, l, cr, B, S, t):
    n = S // t
    pairs = [(i, j) for i in range(n) for j in range(i + 1)]
    qi = jnp.asarray([p[0] for p in pairs], I32)
    kj = jnp.asarray([p[1] for p in pairs], I32)
    return pl.pallas_call(
        _fox_prompt_kernel,
        grid_spec=pltpu.PrefetchScalarGridSpec(
            num_scalar_prefetch=2,
            grid=(B, len(pairs)),
            in_specs=[
                pl.BlockSpec((t, D_FOX), lambda b, s, qi, kj: (b * n + qi[s], 0)),
                pl.BlockSpec((None, None, 2 * D_FOX, t), lambda b, s, qi, kj: (l, b, 0, kj[s])),
                pl.BlockSpec((None, SUBLANES, t), lambda b, s, qi, kj: (b, 0, kj[s])),
            ],
            out_specs=pl.BlockSpec((t, D_FOX), lambda b, s, qi, kj: (b * n + qi[s], 0)),
            scratch_shapes=([pltpu.VMEM((t, LANES), BF16)] * N_HEADS + [pltpu.VMEM((t, 1), F32)] * N_HEADS
                            + [pltpu.VMEM((t, LANES), F32)] * N_HEADS),
        ),
        out_shape=jax.ShapeDtypeStruct((B * S, D_FOX), F32),
        compiler_params=_params(("parallel", "arbitrary")),
        name="fox_prompt",
    )(qi, kj, fq, fkv_all, cr)


def _compress_kernel(x_ref, pe_ref, w_ref, o_ref, *, exact):
    for s in range(2):
        x = x_ref[s] + pe_ref[s]
        w = w_ref[s]
        if exact:
            y = jnp.dot(x, w, precision=HI, preferred_element_type=F32)
        else:
            xh = x.astype(BF16)
            xl = (x - xh.astype(F32)).astype(BF16)
            wh = w.astype(BF16)
            wl = (w - wh.astype(F32)).astype(BF16)
            y = (jnp.dot(xh, wh, preferred_element_type=F32) + jnp.dot(xl, wh, preferred_element_type=F32)
                 + jnp.dot(xh, wl, preferred_element_type=F32))
        o_ref[:, s * HEAD_DIM:(s + 1) * HEAD_DIM] = y


def _compress(x3, pe, w, tr, exact, row0=0, n_rows=None):
    K = x3.shape[2]
    R = x3.shape[1] if n_rows is None else n_rows
    assert R % tr == 0 and row0 % tr == 0
    b0 = row0 // tr
    return pl.pallas_call(
        functools.partial(_compress_kernel, exact=exact),
        grid=(R // tr,),
        in_specs=[pl.BlockSpec((2, tr, K), lambda i: (0, b0 + i, 0)),
                  pl.BlockSpec((2, 1, K), lambda i: (0, 0, 0)),
                  pl.BlockSpec((2, K, HEAD_DIM), lambda i: (0, 0, 0))],
        out_specs=pl.BlockSpec((tr, LANES), lambda i: (i, 0)),
        out_shape=jax.ShapeDtypeStruct((R, LANES), F32),
        compiler_params=_params(("parallel",)),
        name="nsa_compress",
    )(x3, pe, w)


def _block_major(kv, n_blocks):
    return kv.reshape(n_blocks, CMP_LEN, 2, HEAD_DIM).transpose(2, 0, 1, 3).reshape(
        2, n_blocks, CMP_LEN * HEAD_DIM)


def _order_key(x):
    b = lax.bitcast_convert_type(x, I32)
    return jnp.where(b < 0, b ^ jnp.int32(0x7FFFFFFF), b)


def _nsa_prompt_kernel(nq_ref, nqr_ref, sm_ref, cmp_ref, rows_ref, win_ref, o_ref, qc_sc, qx_sc, *, tk):
    QB = nq_ref.shape[0]
    S = rows_ref.shape[1]
    nb = cmp_ref.shape[0]
    nsel = S // SEL_LEN
    i = pl.program_id(1)
    qs = i * QB
    qpos = qs + lax.broadcasted_iota(I32, (QB, 1), 0)

    HQ = N_HEADS * QB
    lo_half = lax.broadcasted_iota(I32, (QB, LANES), 1) < HEAD_DIM

    def stack_heads(ref, scale, dst):
        for h in range(N_HEADS):
            slab = ref[:, (h // 2) * LANES:(h // 2 + 1) * LANES] * scale
            if h % 2:
                slab = pltpu.roll(slab, HEAD_DIM, 1)
            dst[h * QB:(h + 1) * QB, :] = jnp.where(lo_half, slab, 0.0).astype(dst.dtype)

    cmp = cmp_ref[...]
    n_l = lax.broadcasted_iota(I32, (1, nb), 1)
    complete = ((n_l + 1) * CMP_LEN - 1) <= qpos
    stack_heads(nq_ref, SCALE, qc_sc)
    s = _dot_nt_hilo(qc_sc[...], cmp).reshape(N_HEADS, QB, nb)
    s = jnp.where(complete[None], s, NEG)
    e = jnp.exp(s - jnp.max(s, axis=-1, keepdims=True))
    p = e / jnp.sum(e, axis=-1, keepdims=True) * complete.astype(F32)[None]
    o_cmp = _bdot(p.reshape(HQ, nb), cmp)
    psum = jnp.sum(p, axis=0)

    pj = lax.broadcasted_iota(I32, (nsel, nb), 0)
    pn = lax.broadcasted_iota(I32, (nsel, nb), 1)
    imp_t = _dot01_nt(pn // (SEL_LEN // CMP_LEN) == pj, psum)
    jt = (qs + lax.broadcasted_iota(I32, (1, QB), 1)) // SEL_LEN
    jj = lax.broadcasted_iota(I32, (nsel, 1), 0)
    score = jnp.where(jj == jt, 2.0 * SEL_FORCE,
                      jnp.where((jj == 0) | (jj == jt - 1), SEL_FORCE,
                                jnp.where(jj <= jt, imp_t + 0.0, -1.0)))
    key = _order_key(score)
    key_m1 = key - 1
    ngrp = nsel // SUBLANES
    sub = lax.broadcasted_iota(I32, (SUBLANES, QB), 0)
    kg = [key[r * SUBLANES:(r + 1) * SUBLANES, :] for r in range(ngrp)]
    kg1 = [key_m1[r * SUBLANES:(r + 1) * SUBLANES, :] for r in range(ngrp)]
    cnt = [jnp.zeros((SUBLANES, QB), I32) for _ in range(ngrp)]
    for jp in range(nsel):
        g = jp // SUBLANES
        row = key[jp:jp + 1, :]
        mixed = jnp.where(sub > (jp % SUBLANES), kg1[g], kg[g])
        for r in range(ngrp):
            thr = kg[r] if r < g else (kg1[r] if r > g else mixed)
            cnt[r] = cnt[r] + (row > thr).astype(I32)
    sel_t = jnp.concatenate([(c < TOP_N).astype(F32) for c in cnt], axis=0)
    if nsel < QB:
        sel_t = jnp.concatenate([sel_t, jnp.zeros((QB - nsel, QB), F32)], axis=0)
    sel = sel_t.T.astype(BF16)

    stack_heads(nqr_ref, SCALE * LOG2E, qx_sc)
    qx = qx_sc[...]

    def attend(valid, slab_t, m_old, acc_old):
        n = slab_t.shape[1]
        s = jnp.dot(qx, slab_t.astype(BF16), preferred_element_type=F32)
        s = jnp.where(valid[None], s.reshape(N_HEADS, QB, n), NEG).reshape(HQ, n)
        m_new = jnp.maximum(m_old, jnp.max(s, axis=-1, keepdims=True))
        p = jnp.exp2(s - m_new).astype(BF16)
        ones_k = lax.broadcasted_iota(I32, slab_t.shape, 0) < HEAD_DIM
        v_aug = jnp.where(ones_k, 1.0, slab_t).astype(BF16)
        acc = jnp.exp2(m_old - m_new) * acc_old + lax.dot_general(p, v_aug, NT, preferred_element_type=F32)
        return m_new, acc

    jrow = lax.broadcasted_iota(I32, (QB, 1), 0)

    def sel_tile(k0, m_old, acc_old, diagonal):
        kpos = k0 + lax.broadcasted_iota(I32, (1, tk), 1)
        expand = (jrow == kpos // SEL_LEN).astype(BF16)
        valid = jnp.dot(sel, expand, preferred_element_type=F32) > 0.5
        if diagonal:
            valid = valid & (kpos <= qpos)
        return attend(valid, rows_ref[2 * HEAD_DIM:4 * HEAD_DIM, pl.ds(k0, tk)], m_old, acc_old)

    n_full = qs // tk
    init = (jnp.full((HQ, 1), NEG, F32), jnp.zeros((HQ, LANES), F32))
    m_s, acc_s = lax.fori_loop(
        0, n_full, lambda t, c: sel_tile(pl.multiple_of(t * tk, tk), c[0], c[1], False), init)
    _, acc_s = sel_tile(pl.multiple_of(n_full * tk, tk), m_s, acc_s, True)

    wlen = WINDOW + QB
    w0 = pl.multiple_of(jnp.maximum(qs - WINDOW, 0), QB)
    wpos = w0 + lax.broadcasted_iota(I32, (1, wlen), 1)
    d = qpos - wpos
    _, acc_w = attend((d >= 0) & (d < WINDOW), win_ref[:, pl.ds(w0, wlen)],
                      jnp.full((HQ, 1), NEG, F32), jnp.zeros((HQ, LANES), F32))

    sm = sm_ref[...]
    for h in range(N_HEADS):
        rs = slice(h * QB, (h + 1) * QB)
        o_sel = acc_s[rs] * (1.0 / acc_s[rs, 0:1])
        o_win = acc_w[rs] * (1.0 / acc_w[rs, 0:1])
        c = SM_NG + 3 * h
        mix = sm[:, c:c + 1] * o_cmp[rs] + sm[:, c + 1:c + 2] * o_sel + sm[:, c + 2:c + 3] * o_win
        if h % 2 == 0:
            mix = pltpu.roll(mix, HEAD_DIM, 1)
        lo = (h % 2) * HEAD_DIM
        o_ref[:, h * HEAD_DIM:(h + 1) * HEAD_DIM] = mix[:, lo:lo + HEAD_DIM]


def _nsa_prompt(nq, nqr, small, cmp, rows_all, l, win, B, S, tk):
    QB = 128
    nq_t = S // QB
    nb = S // CMP_LEN
    assert S % tk == 0 and S >= WINDOW + QB
    return pl.pallas_call(
        functools.partial(_nsa_prompt_kernel, tk=tk),
        grid=(B, nq_t),
        in_specs=[
            pl.BlockSpec((QB, D_NSA), lambda b, i: (b * nq_t + i, 0)),
            pl.BlockSpec((QB, D_NSA), lambda b, i: (b * nq_t + i, 0)),
            pl.BlockSpec((QB, LANES), lambda b, i: (b * nq_t + i, 0)),
            pl.BlockSpec((nb, LANES), lambda b, i: (b, 0)),
            pl.BlockSpec((None, None, 4 * HEAD_DIM, S), lambda b, i: (l, b, 0, 0)),
            pl.BlockSpec((None, 2 * HEAD_DIM, S), lambda b, i: (b, 0, 0)),
        ],
        out_specs=pl.BlockSpec((QB, D_NSA), lambda b, i: (b * nq_t + i, 0)),
        out_shape=jax.ShapeDtypeStruct((B * S, D_NSA), F32),
        scratch_shapes=[pltpu.VMEM((N_HEADS * QB, LANES), F32), pltpu.VMEM((N_HEADS * QB, LANES), BF16)],
        compiler_params=_params(("parallel", "parallel")),
        name="nsa_prompt",
    )(nq, nqr, small, cmp, rows_all, win)


def _gla_prompt_kernel(qk_ref, v_ref, g_ref, o_ref, st_ref, s_sc):
    t = pl.program_id(1)
    nt = pl.num_programs(1)
    tc = qk_ref.shape[0]
    C = GLA_CHUNK

    @pl.when(t == 0)
    def _():
        s_sc[...] = jnp.zeros_like(s_sc)

    r = lax.broadcasted_iota(I32, (tc, tc), 0)
    c = lax.broadcasted_iota(I32, (tc, tc), 1)
    same = (r // C) == (c // C)
    causal = same & (c <= r)
    g = g_ref[...]
    gcum = _dot01(causal, g)
    g_t = g.T
    gcum_t = _dot01_r(g_t, same & (r <= c))
    gtot_t = _dot01_r(g_t, same)
    q_e = (qk_ref[:, 0:D_FOX] * SCALE * jnp.exp(gcum)).astype(BF16)
    k_e = (qk_ref[:, D_FOX:2 * D_FOX] * jnp.exp(-gcum)).astype(BF16)
    kd_t = (qk_ref[:, D_FOX:2 * D_FOX].T * jnp.exp(gtot_t - gcum_t)).astype(BF16)
    decay_t = jnp.exp(gtot_t)
    for h in range(N_HEADS):
        hs = slice(h * HEAD_DIM, (h + 1) * HEAD_DIM)
        v = v_ref[:, h * GLA_DV:(h + 1) * GLA_DV].astype(BF16)
        a = jnp.where(causal, lax.dot_general(q_e[:, hs], k_e[:, hs], NT, preferred_element_type=F32), 0.0)
        o_intra = jnp.dot(a.astype(BF16), v, preferred_element_type=F32)
        state = s_sc[h]
        for ci in range(tc // C):
            rs = slice(ci * C, (ci + 1) * C)
            o_ref[rs, h * GLA_DV:(h + 1) * GLA_DV] = (
                o_intra[rs] + jnp.dot(q_e[rs, hs], state.astype(BF16), preferred_element_type=F32))
            state = (decay_t[hs, ci * C:ci * C + 1] * state
                     + jnp.dot(kd_t[hs, rs], v[rs], preferred_element_type=F32))
        s_sc[h] = state

    @pl.when(t == nt - 1)
    def _():
        st_ref[...] = s_sc[...]


def _gla_prompt(gqk, gv, glog, B, S, tc):
    nt = S // tc
    return pl.pallas_call(
        _gla_prompt_kernel,
        grid=(B, nt),
        in_specs=[pl.BlockSpec((tc, 2 * D_FOX), lambda b, t: (b * nt + t, 0)),
                  pl.BlockSpec((tc, D_GLA), lambda b, t: (b * nt + t, 0)),
                  pl.BlockSpec((tc, D_FOX), lambda b, t: (b * nt + t, 0))],
        out_specs=[pl.BlockSpec((tc, D_GLA), lambda b, t: (b * nt + t, 0)),
                   pl.BlockSpec((None, N_HEADS, HEAD_DIM, GLA_DV), lambda b, t: (b, 0, 0, 0))],
        out_shape=[jax.ShapeDtypeStruct((B * S, D_GLA), F32),
                   jax.ShapeDtypeStruct((B, N_HEADS, HEAD_DIM, GLA_DV), F32)],
        scratch_shapes=[pltpu.VMEM((N_HEADS, HEAD_DIM, GLA_DV), F32)],
        compiler_params=_params(("parallel", "arbitrary")),
        name="gla_prompt",
    )(gqk, gv, glog)


def _outproj_kernel(x_ref, of_ref, on_ref, og_ref, gog_ref, gn_ref, w_ref, o_ref):
    acc = _wdot(of_ref[...], w_ref[0:D_FOX, :])
    acc = acc + _wdot(on_ref[...], w_ref[D_FOX:D_FOX + D_NSA, :])
    for h in range(N_HEADS):
        hs = slice(h * GLA_DV, (h + 1) * GLA_DV)
        z = _rms(og_ref[:, hs], gn_ref[...]) * _silu(gog_ref[:, hs])
        w0 = D_FOX + D_NSA + h * GLA_DV
        acc = acc + _wdot(z, w_ref[w0:w0 + GLA_DV, :])
    o_ref[...] = x_ref[...] + acc


def _out_projection(x, o_fox, o_nsa, o_gla, gog, gn, w_out, tm):
    T = x.shape[0]
    assert T % tm == 0
    row = lambda w: pl.BlockSpec((tm, w), lambda i: (i, 0))
    return pl.pallas_call(
        _outproj_kernel,
        grid=(T // tm,),
        in_specs=[row(D_MODEL), row(D_FOX), row(D_NSA), row(D_GLA), row(D_GLA),
                  pl.BlockSpec((1, GLA_DV), lambda i: (0, 0)),
                  pl.BlockSpec((D_MODEL, D_MODEL), lambda i: (0, 0))],
        out_specs=row(D_MODEL),
        out_shape=jax.ShapeDtypeStruct((T, D_MODEL), F32),
        compiler_params=_params(("parallel",)),
        name="out_projection",
    )(x, o_fox, o_nsa, o_gla, gog, gn, w_out)


def _dense_ffn_kernel(x_ref, g_ref, wg_ref, wu_ref, wd_ref, o_ref, h_sc, acc_sc):
    f = pl.program_id(1)
    nf = pl.num_programs(1)

    @pl.when(f == 0)
    def _():
        h_sc[...] = _rms(x_ref[...], g_ref[...]).astype(h_sc.dtype)
        acc_sc[...] = jnp.zeros_like(acc_sc)

    h = h_sc[...]
    a = _wdot(h, wg_ref[...])
    u = _wdot(h, wu_ref[...])
    acc_sc[...] += _wdot(_silu(a) * u, wd_ref[...])

    @pl.when(f == nf - 1)
    def _():
        o_ref[...] = x_ref[...] + acc_sc[...]


def _dense_ffn(x, g, wg, wu, wd, tm, tf):
    T = x.shape[0]
    assert T % tm == 0 and D_FF % tf == 0
    return pl.pallas_call(
        _dense_ffn_kernel,
        grid=(T // tm, D_FF // tf),
        in_specs=[pl.BlockSpec((tm, D_MODEL), lambda i, f: (i, 0)),
                  pl.BlockSpec((1, D_MODEL), lambda i, f: (0, 0)),
                  pl.BlockSpec((D_MODEL, tf), lambda i, f: (0, f)),
                  pl.BlockSpec((D_MODEL, tf), lambda i, f: (0, f)),
                  pl.BlockSpec((tf, D_MODEL), lambda i, f: (f, 0))],
        out_specs=pl.BlockSpec((tm, D_MODEL), lambda i, f: (i, 0)),
        out_shape=jax.ShapeDtypeStruct((T, D_MODEL), F32),
        scratch_shapes=[pltpu.VMEM((tm, D_MODEL), wg.dtype), pltpu.VMEM((tm, D_MODEL), F32)],
        compiler_params=_params(("parallel", "arbitrary")),
        name="dense_ffn",
    )(x, g, wg, wu, wd)


def _top2_route(h, wr):
    logits = jnp.dot(h, wr, precision=HI, preferred_element_type=F32)
    lane = lax.broadcasted_iota(I32, logits.shape, 1)
    lg = jnp.where(lane < N_EXPERTS, logits, -jnp.inf)
    m1 = jnp.max(lg, axis=-1, keepdims=True)
    i1 = jnp.min(jnp.where(lg == m1, lane, LANES), axis=-1, keepdims=True)
    lg2 = jnp.where(lane == i1, -jnp.inf, lg)
    m2 = jnp.max(lg2, axis=-1, keepdims=True)
    i2 = jnp.min(jnp.where(lg2 == m2, lane, LANES), axis=-1, keepdims=True)
    e = jnp.exp(m2 - m1)
    den = 1.0 + e
    return jnp.where(lane == 0, i1.astype(F32),
                     jnp.where(lane == 1, i2.astype(F32),
                               jnp.where(lane == 2, 1.0 / den,
                                         jnp.where(lane == 3, e / den, 0.0))))


def _router_kernel(x_ref, g_ref, wr_ref, h_ref, r_ref):
    h = _rms(x_ref[...], g_ref[...])
    h_ref[...] = h
    r_ref[...] = _top2_route(h, wr_ref[...])


def _moe_small_kernel(x_ref, g_ref, wr_ref, fg_ref, wg_ref, wu_ref, wd_ref, o_ref, h_sc, acc_sc, r_sc, *,
                      final):
    e = pl.program_id(0)
    f = pl.program_id(1)

    @pl.when((e == 0) & (f == 0))
    def _():
        h = _rms(x_ref[...], g_ref[...])
        h_sc[...] = h.astype(BF16)
        r_sc[...] = _top2_route(h, wr_ref[...])
        acc_sc[...] = jnp.zeros_like(acc_sc)

    r = r_sc[...]
    ef = e.astype(F32)
    gate = jnp.where(r[:, 0:1] == ef, r[:, 2:3], 0.0) + jnp.where(r[:, 1:2] == ef, r[:, 3:4], 0.0)
    h = h_sc[...]
    a = jnp.dot(h, wg_ref[...].astype(BF16), preferred_element_type=F32)
    u = jnp.dot(h, wu_ref[...].astype(BF16), preferred_element_type=F32)
    y = _bdot(_silu(a) * u, wd_ref[...])
    acc_sc[...] += jnp.where(gate != 0.0, gate * y, 0.0)

    @pl.when((e == pl.num_programs(0) - 1) & (f == pl.num_programs(1) - 1))
    def _():
        out = x_ref[...] + acc_sc[...]
        o_ref[...] = _rms(out, fg_ref[...]) if final else out


def _moe_small(x, g, wr_pad, wg, wu, wd, tf, final_g):
    T = x.shape[0]
    fg = jnp.ones((1, D_MODEL), F32) if final_g is None else final_g
    const = lambda shape: pl.BlockSpec(shape, lambda e, f: (0, 0))
    return pl.pallas_call(
        functools.partial(_moe_small_kernel, final=final_g is not None),
        grid=(N_EXPERTS, D_FF // tf),
        in_specs=[const((T, D_MODEL)), const((1, D_MODEL)), const((D_MODEL, LANES)), const((1, D_MODEL)),
                  pl.BlockSpec((None, D_MODEL, tf), lambda e, f: (e, 0, f)),
                  pl.BlockSpec((None, D_MODEL, tf), lambda e, f: (e, 0, f)),
                  pl.BlockSpec((None, tf, D_MODEL), lambda e, f: (e, f, 0))],
        out_specs=const((T, D_MODEL)),
        out_shape=jax.ShapeDtypeStruct((T, D_MODEL), F32),
        scratch_shapes=[pltpu.VMEM((T, D_MODEL), BF16), pltpu.VMEM((T, D_MODEL), F32),
                        pltpu.VMEM((T, LANES), F32)],
        compiler_params=_params(("arbitrary", "arbitrary")),
        name="moe_small",
    )(x, g, wr_pad, fg, wg, wu, wd)


def _router(x, g, wr_pad, tm):
    T = x.shape[0]
    assert T % tm == 0
    return pl.pallas_call(
        _router_kernel,
        grid=(T // tm,),
        in_specs=[pl.BlockSpec((tm, D_MODEL), lambda i: (i, 0)),
                  pl.BlockSpec((1, D_MODEL), lambda i: (0, 0)),
                  pl.BlockSpec((D_MODEL, LANES), lambda i: (0, 0))],
        out_specs=[pl.BlockSpec((tm, D_MODEL), lambda i: (i, 0)),
                   pl.BlockSpec((tm, LANES), lambda i: (i, 0))],
        out_shape=[jax.ShapeDtypeStruct((T, D_MODEL), F32),
                   jax.ShapeDtypeStruct((T, LANES), F32)],
        compiler_params=_params(("parallel",)),
        name="moe_router",
    )(x, g, wr_pad)


GATHER_WINDOW = 32
SC_WORKERS = 32


def _row_gather(src, idx):
    n = idx.shape[0]
    step = GATHER_WINDOW * SC_WORKERS
    n_pad = -(-n // step) * step
    if n_pad != n:
        idx = jnp.concatenate([idx, jnp.zeros((n_pad - n,), idx.dtype)])
    width = src.shape[1]
    per_worker = n_pad // SC_WORKERS
    mesh = plsc.VectorSubcoreMesh(core_axis_name="core", subcore_axis_name="subcore")

    @functools.partial(pl.kernel, out_type=jax.ShapeDtypeStruct((n_pad, width), src.dtype), mesh=mesh,
                       scratch_types=[pltpu.VMEM((per_worker,), I32),
                                      pltpu.VMEM((GATHER_WINDOW, width), src.dtype)],
                       name="row_gather")
    def gather(src_hbm, idx_hbm, dst_hbm, idx_v, buf):
        worker = lax.axis_index("core") * (SC_WORKERS // 2) + lax.axis_index("subcore")
        base = worker * per_worker
        pltpu.sync_copy(idx_hbm.at[pl.ds(base, per_worker)], idx_v)

        @pl.loop(0, per_worker // GATHER_WINDOW)
        def _(j):
            pltpu.sync_copy(src_hbm.at[idx_v.at[pl.ds(j * GATHER_WINDOW, GATHER_WINDOW)]], buf)
            pltpu.sync_copy(buf, dst_hbm.at[pl.ds(base + j * GATHER_WINDOW, GATHER_WINDOW)])

    return gather(src, idx)


def _moe_ffn_kernel(be_ref, nu_ref, x_ref, wg_ref, wu_ref, wd_ref, o_ref, acc_sc, x_sc):
    b = pl.program_id(0)
    f = pl.program_id(1)
    nf = pl.num_programs(1)
    used = b < nu_ref[0]

    @pl.when(used)
    def _():
        @pl.when(f == 0)
        def _():
            acc_sc[...] = jnp.zeros_like(acc_sc)
            x_sc[...] = x_ref[...].astype(BF16)

        x = x_sc[...]
        a = jnp.dot(x, wg_ref[...].astype(BF16), preferred_element_type=F32)
        u = jnp.dot(x, wu_ref[...].astype(BF16), preferred_element_type=F32)
        acc_sc[...] += _bdot(_silu(a) * u, wd_ref[...])

        @pl.when(f == nf - 1)
        def _():
            o_ref[...] = acc_sc[...]

    @pl.when(jnp.logical_not(used) & (f == nf - 1))
    def _():
        o_ref[...] = jnp.zeros_like(o_ref)


def _moe_ffn(xb, block_e, n_used, wg, wu, wd, blk, tf):
    cap = xb.shape[0]
    nb = cap // blk
    nf = D_FF // tf

    def bsel(b, nu):
        return jnp.minimum(b, nu[0] - 1)

    def fsel(b, f, nu):
        return jnp.where(b < nu[0], f, nf - 1)

    return pl.pallas_call(
        _moe_ffn_kernel,
        grid_spec=pltpu.PrefetchScalarGridSpec(
            num_scalar_prefetch=2,
            grid=(nb, nf),
            in_specs=[
                pl.BlockSpec((blk, D_MODEL), lambda b, f, be, nu: (bsel(b, nu), 0)),
                pl.BlockSpec((None, D_MODEL, tf), lambda b, f, be, nu: (be[bsel(b, nu)], 0, fsel(b, f, nu))),
                pl.BlockSpec((None, D_MODEL, tf), lambda b, f, be, nu: (be[bsel(b, nu)], 0, fsel(b, f, nu))),
                pl.BlockSpec((None, tf, D_MODEL), lambda b, f, be, nu: (be[bsel(b, nu)], fsel(b, f, nu), 0)),
            ],
            out_specs=pl.BlockSpec((blk, D_MODEL), lambda b, f, be, nu: (b, 0)),
            scratch_shapes=[pltpu.VMEM((blk, D_MODEL), F32), pltpu.VMEM((blk, D_MODEL), BF16)],
        ),
        out_shape=jax.ShapeDtypeStruct((cap, D_MODEL), F32),
        compiler_params=_params(("arbitrary", "arbitrary")),
        name="moe_ffn",
    )(block_e, n_used, xb, wg, wu, wd)


def _moe_combine_kernel(x_ref, y1_ref, y2_ref, r_ref, g_ref, o_ref, *, final):
    r = r_ref[...]
    y = x_ref[...] + (r[:, 2:3] * y1_ref[...] + r[:, 3:4] * y2_ref[...])
    o_ref[...] = _rms(y, g_ref[...]) if final else y


def _moe_combine(x, y1, y2, route, tm, final_g):
    T = x.shape[0]
    g = jnp.ones((1, D_MODEL), F32) if final_g is None else final_g
    return pl.pallas_call(
        functools.partial(_moe_combine_kernel, final=final_g is not None),
        grid=(T // tm,),
        in_specs=[pl.BlockSpec((tm, D_MODEL), lambda i: (i, 0)),
                  pl.BlockSpec((tm, D_MODEL), lambda i: (i, 0)),
                  pl.BlockSpec((tm, D_MODEL), lambda i: (i, 0)),
                  pl.BlockSpec((tm, LANES), lambda i: (i, 0)),
                  pl.BlockSpec((1, D_MODEL), lambda i: (0, 0))],
        out_specs=pl.BlockSpec((tm, D_MODEL), lambda i: (i, 0)),
        out_shape=jax.ShapeDtypeStruct((T, D_MODEL), F32),
        compiler_params=_params(("parallel",)),
        name="moe_combine",
    )(x, y1, y2, route, g)


def _moe_plan(e_top, blk):
    T = e_top.shape[0]
    n = 2 * T
    flat_e = e_top.reshape(-1)
    onehot = (flat_e[:, None] == jnp.arange(N_EXPERTS, dtype=I32)[None, :]).astype(I32)
    csum = jnp.cumsum(onehot, axis=0)
    rank = jnp.sum((csum - onehot) * onehot, axis=1)
    counts = csum[-1]
    padded = (counts + blk - 1) // blk * blk
    ends = jnp.cumsum(padded)
    pstart = ends - padded
    dest = (pstart[flat_e] + rank).astype(I32)
    n_blocks = -(-n // blk) + N_EXPERTS
    cap = n_blocks * blk
    slot_tok = (jnp.arange(cap, dtype=I32) % T).at[dest].set(jnp.arange(n, dtype=I32) // 2)
    first = jnp.arange(n_blocks, dtype=I32) * blk
    block_e = jnp.minimum(jnp.sum((ends[None, :] <= first[:, None]).astype(I32), axis=1), N_EXPERTS - 1)
    n_used = (ends[-1] // blk).astype(I32).reshape(1)
    return dest, slot_tok, block_e, n_used


def _moe_dispatch(x, g, wr_pad, tm, blk):
    h, route = _router(x, g, wr_pad, tm)
    dest, slot_tok, block_e, n_used = _moe_plan(route[:, 0:2].astype(I32), blk)
    return dict(x=x, route=route, dest=dest, xb=_row_gather(h, slot_tok), block_e=block_e, n_used=n_used)


def _moe_experts(d, wg, wu, wd, blk, tf):
    yb = _moe_ffn(d['xb'], d['block_e'], d['n_used'], wg, wu, wd, blk, tf)
    d2 = d['dest'].reshape(-1, 2)
    return _row_gather(yb, d2[:, 0]), _row_gather(yb, d2[:, 1])


def _moe_merge(d, y1, y2, tm, final_g):
    return _moe_combine(d['x'], y1, y2, d['route'], tm, final_g)


def _norm_kernel(x_ref, g_ref, o_ref):
    o_ref[...] = _rms(x_ref[...], g_ref[...])


def _final_norm(x, g, tm):
    T = x.shape[0]
    return pl.pallas_call(
        _norm_kernel,
        grid=(T // tm,),
        in_specs=[pl.BlockSpec((tm, D_MODEL), lambda i: (i, 0)),
                  pl.BlockSpec((1, D_MODEL), lambda i: (0, 0))],
        out_specs=pl.BlockSpec((tm, D_MODEL), lambda i: (i, 0)),
        out_shape=jax.ShapeDtypeStruct((T, D_MODEL), F32),
        compiler_params=_params(("parallel",)),
        name="final_norm",
    )(x, g)


_IN_SPLITS = (D_FOX, D_FOX, D_FOX, N_HEADS, D_NSA, 6 * HEAD_DIM, 3 * N_HEADS,
              D_FOX, D_FOX, D_GLA, GLA_RANK, D_GLA)


def _reorder_w_in(w):
    offs = [0]
    for s in _IN_SPLITS:
        offs.append(offs[-1] + s)
    seg = lambda k: w[:, offs[k]:offs[k + 1]]
    fq, fk, fv, ff, nq, nkv, ng, gq, gk, gv, glr, gog = [seg(k) for k in range(12)]
    pad = jnp.zeros((w.shape[0], LANES - SM_GLR - GLA_RANK), w.dtype)
    return jnp.concatenate([fq, fk, fv, nq, nkv, gq, gk, gv, gog, ff, ng, glr, pad], axis=1)


def _rope_table(pos):
    inv = ROPE_THETA ** (-jnp.arange(ROPE_HALF, dtype=F32) / ROPE_HALF)
    ang = pos.astype(F32)[:, None] * inv[None, :]
    cos, sin = jnp.cos(ang), jnp.sin(ang)
    P = pos.shape[0]
    one = jnp.ones((P, HEAD_DIM - ROPE_DIM), F32)
    zero = jnp.zeros((P, HEAD_DIM - ROPE_DIM), F32)
    z8 = jnp.zeros((P, ROPE_HALF), F32)
    a64 = jnp.concatenate([cos, cos, one], axis=1)
    p64 = jnp.concatenate([z8, sin, zero], axis=1)
    m64 = jnp.concatenate([-sin, z8, zero], axis=1)
    i64 = jnp.ones((P, HEAD_DIM), F32)
    o64 = jnp.zeros((P, HEAD_DIM), F32)
    return jnp.concatenate([a64, a64, p64, p64, m64, m64, a64, i64, p64, o64, m64, o64], axis=1)


def _layer_mix_params(l, norm_mix_g, w_in, b_fox_f, w_cmp, pe_cmp, w_gla_gk, b_gla_gk, g_gla_norm, w_out):
    sb = jnp.zeros((1, LANES), F32).at[0, SM_FF:SM_FF + N_HEADS].set(b_fox_f[l])
    wgk = jnp.zeros((LANES, D_FOX), F32).at[SM_GLR:SM_GLR + GLA_RANK].set(w_gla_gk[l])
    w_r = _reorder_w_in(w_in[l])
    return dict(g=norm_mix_g[l].reshape(1, D_MODEL), w_r=w_r.astype(BF16), w_r32=w_r, sb=sb,
                wgk=wgk.astype(BF16), wgk32=wgk, w_out32=w_out[l],
                bgk=b_gla_gk[l].reshape(1, D_FOX), w_cmp=w_cmp[l],
                pe_cmp=pe_cmp[l].reshape(2, 1, CMP_LEN * HEAD_DIM),
                gn=g_gla_norm[l].reshape(1, GLA_DV), w_out=w_out[l].astype(BF16))


def _mix_prompt(x, B, S, l, depth, p, tab, tm, t_fox, tk_sel, tc_gla, carry):
    (fq, fkv_all, nq, nqr, rows_all, win_t, gqk, gv, gog, glog, small, small_t) = _in_projection(
        x, 0, B * S, tm, p['g'], p['w_r'], p['sb'], p['wgk'], p['bgk'], tab, S, batch=(B, S, depth), layer=l,
        prev=(carry['fkv'], carry['rows']) if carry else None)
    carry.update(fkv=fkv_all, rows=rows_all)
    cr = _fox_cumsum(small_t, min(S, Tiles.cumsum))
    o_fox = _fox_prompt(fq, fkv_all, l, cr, B, S, t_fox)
    n_blk = S // CMP_LEN
    blocks = rows_all[l, :, 0:2 * HEAD_DIM, :].reshape(B, 2, HEAD_DIM, n_blk, CMP_LEN).transpose(1, 0, 3, 4, 2)
    blocks = blocks.reshape(2, B * n_blk, CMP_LEN * HEAD_DIM)
    cmp = _compress(blocks, p['pe_cmp'], p['w_cmp'], min(Tiles.compress_rows, B * n_blk), True)
    o_nsa = _nsa_prompt(nq, nqr, small, cmp, rows_all, l, win_t, B, S, tk_sel)
    o_gla, g_state = _gla_prompt(gqk, gv, glog, B, S, tc_gla)
    x_new = _out_projection(x, o_fox, o_nsa, o_gla, gog, p['gn'], p['w_out'], tm)
    wp = min(WINDOW, S)
    return x_new, dict(small=small_t[:, 0:N_HEADS, :], win=win_t[:, :, S - wp:], g_state=g_state)


def _per_head_col(vals):
    r = lax.broadcasted_iota(I32, (SUBLANES, 1), 0)
    out = jnp.zeros((SUBLANES, 1), F32)
    for h, v in enumerate(vals):
        out = out + jnp.where(r == h, v, 0.0)
    return out


def _per_head_row(vals, width):
    grp = lax.broadcasted_iota(I32, (1, width), 1) // HEAD_DIM
    out = jnp.zeros((1, width), F32)
    for h, v in enumerate(vals):
        out = out + jnp.where(grp == h, v, 0.0)
    return out


def _head_lane_sums(row):
    grp = lax.broadcasted_iota(I32, row.shape, 1) // HEAD_DIM
    return [jnp.sum(jnp.where(grp == h, row, 0.0), axis=1, keepdims=True) for h in range(N_HEADS)]


def _cols_of(row):
    return jnp.concatenate([jnp.broadcast_to(row[:, j:j + LANES], (LANES, LANES)).T
                            for j in range(0, row.shape[1], LANES)], axis=0)


def _row_of(col):
    return jnp.concatenate([jnp.broadcast_to(col[j:j + LANES], (LANES, LANES)).T[0:1, :]
                            for j in range(0, col.shape[0], LANES)], axis=1)


def _sublane_group_sum(x):
    return jnp.sum(x.reshape(x.shape[0] // SUBLANES, SUBLANES, x.shape[1]), axis=0)


def _fold_matrix(n_pages):
    r = lax.broadcasted_iota(I32, (n_pages * SUBLANES, n_pages * N_HEADS * SUBLANES), 0)
    c = lax.broadcasted_iota(I32, (n_pages * SUBLANES, n_pages * N_HEADS * SUBLANES), 1)
    blk = c // SUBLANES
    return ((blk // N_HEADS == r // SUBLANES) & (blk % N_HEADS == r % SUBLANES)).astype(F32)


def _fox_decode_kernel(pt_ref, q_ref, kvn_ref, smn_ref, *refs, n_pages):
    del pt_ref
    kv_refs = refs[0:n_pages]
    lf_refs = refs[n_pages:2 * n_pages]
    o_ref, lf_sc, part_sc = refs[2 * n_pages:]
    R = n_pages * SUBLANES
    PG = kv_refs[0].shape[-1]
    row = pl.ds(pl.program_id(0) % SUBLANES, 1)
    q_row = q_ref[row, :]
    kvn = kvn_ref[row, :]
    smn = smn_ref[row, :]
    q_cols = _cols_of(q_row)

    lf_sc[...] = jnp.zeros_like(lf_sc)
    for p in range(n_pages):
        lf_sc[p * SUBLANES:p * SUBLANES + N_HEADS, :] = lf_refs[p][...]
    lft = lf_sc[...]
    k0 = lax.broadcasted_iota(I32, (PG, PG), 0)
    k1 = lax.broadcasted_iota(I32, (PG, PG), 1)
    within = _dot01_r(lft, k0 > k1)
    tot = jnp.broadcast_to(jnp.sum(lft, axis=1, keepdims=True), (R, PG))
    r0 = lax.broadcasted_iota(I32, (R, R), 0)
    r1 = lax.broadcasted_iota(I32, (R, R), 1)
    later = (r1 % SUBLANES == r0 % SUBLANES) & (r1 // SUBLANES > r0 // SUBLANES)
    cross = _dot01(later, tot)
    rr = lax.broadcasted_iota(I32, (R, 1), 0) % SUBLANES
    newcol = jnp.zeros((R, 1), F32)
    for h in range(N_HEADS):
        newcol = newcol + jnp.where(rr == h, smn[:, SM_FF + h:SM_FF + h + 1], 0.0)
    bias = (within + cross + newcol).reshape(n_pages, SUBLANES, PG)

    for p in range(n_pages):
        for h in range(N_HEADS):
            g = p * N_HEADS + h
            part_sc[g * SUBLANES:(g + 1) * SUBLANES, :] = _sublane_group_sum(
                kv_refs[p][0, h] * q_cols[h * HEAD_DIM:(h + 1) * HEAD_DIM])
    s = _dot01(_fold_matrix(n_pages), part_sc[...])
    s3 = s.reshape(n_pages, SUBLANES, PG) * SCALE + bias
    s_new = _per_head_col(_head_lane_sums(q_row * kvn[:, 0:D_FOX])) * SCALE
    m = jnp.max(jnp.max(s3, axis=2, keepdims=True), axis=0)
    m = jnp.maximum(m, s_new)
    p3 = jnp.exp(s3 - m[None])
    pn = jnp.exp(s_new - m)
    inv = 1.0 / (jnp.sum(jnp.sum(p3, axis=2, keepdims=True), axis=0) + pn)
    o_cols = []
    for h in range(N_HEADS):
        acc = jnp.zeros((HEAD_DIM, PG), F32)
        for p in range(n_pages):
            acc = acc + kv_refs[p][1, h] * p3[p, h:h + 1, :]
        o_cols.append(jnp.sum(acc, axis=1, keepdims=True) * inv[h:h + 1])
    w_new = _per_head_row([pn[h:h + 1] * inv[h:h + 1] for h in range(N_HEADS)], D_FOX)
    o_ref[row, :] = _row_of(jnp.concatenate(o_cols, axis=0)) + w_new * kvn[:, D_FOX:2 * D_FOX]


def _fox_decode(l, pt_flat, n_pages, fq, fkv, small, kv_cache_t, lft_cache):
    DB = fq.shape[0]
    PG = kv_cache_t.shape[-1]
    page = lambda p, nz: (lambda b, pt: (l, pt[b * n_pages + p]) + (0,) * nz)
    rows8 = lambda w: pl.BlockSpec((SUBLANES, w), lambda b, pt: (b // SUBLANES, 0))
    return pl.pallas_call(
        functools.partial(_fox_decode_kernel, n_pages=n_pages),
        grid_spec=pltpu.PrefetchScalarGridSpec(
            num_scalar_prefetch=1,
            grid=(DB,),
            in_specs=[rows8(D_FOX), rows8(2 * D_FOX), rows8(LANES)]
            + [pl.BlockSpec((None, None, 2, N_HEADS, HEAD_DIM, PG), page(p, 4)) for p in range(n_pages)]
            + [pl.BlockSpec((None, None, N_HEADS, PG), page(p, 2)) for p in range(n_pages)],
            out_specs=rows8(D_FOX),
            scratch_shapes=[pltpu.VMEM((n_pages * SUBLANES, PG), F32),
                            pltpu.VMEM((n_pages * N_HEADS * SUBLANES, PG), F32)],
        ),
        out_shape=jax.ShapeDtypeStruct((DB, D_FOX), F32),
        compiler_params=_params(("arbitrary",)),
        name="fox_decode",
    )(pt_flat, fq, fkv, small, *([kv_cache_t] * n_pages), *([lft_cache] * n_pages))


def _nsa_decode_kernel(pt_ref, q_ref, qr_ref, rown_ref, winn_ref, sm_ref, win_ref, *refs,
                       n_pages, past_len):
    del pt_ref
    pg_refs = refs[0:n_pages]
    cmp_refs = refs[n_pages:2 * n_pages]
    o_ref, nw_ref, cmp_sc, qc_sc, part_sc, sw_sc = refs[-6:]
    R = n_pages * SUBLANES
    PG = pg_refs[0].shape[-1]
    WB = win_ref.shape[-1]
    per_page = PG // CMP_LEN
    assert per_page <= SUBLANES and PG == 2 * SEL_LEN and R == LANES
    jt = past_len // SEL_LEN
    row = pl.ds(pl.program_id(0) % SUBLANES, 1)
    q_row = q_ref[row, :]
    qr_row = qr_ref[row, :]
    rown = rown_ref[row, :]
    winn = winn_ref[row, :]
    smn = sm_ref[row, :]
    qr_cols = _cols_of(qr_row)
    rep4 = lambda r64: jnp.concatenate([r64] * N_HEADS, axis=1)

    qc_sc[...] = jnp.zeros_like(qc_sc)
    for h in range(N_HEADS):
        qc_sc[h:h + 1, 0:HEAD_DIM] = q_row[:, h * HEAD_DIM:(h + 1) * HEAD_DIM]
    head_row = lax.broadcasted_iota(I32, (SUBLANES, 1), 0) < N_HEADS

    cmp_sc[...] = jnp.zeros_like(cmp_sc)
    for p in range(n_pages):
        cmp_sc[p * SUBLANES:p * SUBLANES + per_page, :] = cmp_refs[p][...]
    cmpa = cmp_sc[...]
    lane = lax.broadcasted_iota(I32, (1, R), 1)
    blk = per_page * (lane // SUBLANES) + lane % SUBLANES
    complete = (lane % SUBLANES < per_page) & ((blk + 1) * CMP_LEN - 1 <= past_len)
    s = _dot_nt_hilo(qc_sc[...], cmpa) * SCALE
    s = jnp.where(complete, s, NEG)
    e = jnp.exp(s - jnp.max(s, axis=-1, keepdims=True))
    pc = e / jnp.sum(e, axis=-1, keepdims=True) * complete.astype(F32)
    vcb_t = cmpa.T[HEAD_DIM:2 * HEAD_DIM, :]
    o_cmp = [jnp.sum(vcb_t * pc[h:h + 1, :], axis=1, keepdims=True) for h in range(N_HEADS)]

    imp_c = jnp.sum(jnp.where(head_row, pc, 0.0), axis=0, keepdims=True)
    imp_s = imp_c + pltpu.roll(imp_c, R - 1, 1)
    cand = (lane % SUBLANES == 0) | (lane % SUBLANES == 2)
    jsel = 2 * (lane // SUBLANES) + (lane % SUBLANES) // 2
    score = jnp.where(jsel == jt, 2.0 * SEL_FORCE,
                      jnp.where((jsel == 0) | (jsel == jt - 1), SEL_FORCE,
                                jnp.where(jsel <= jt, imp_s + 0.0, -1.0)))
    score_b = jnp.broadcast_to(score, (R, R))
    key_row = _order_key(score_b)
    key_col = _order_key(score_b.T)
    l0 = lax.broadcasted_iota(I32, (R, R), 0)
    l1 = lax.broadcasted_iota(I32, (R, R), 1)
    cand_col = (l0 % SUBLANES == 0) | (l0 % SUBLANES == 2)
    beats = cand_col & (key_col > jnp.where(l0 < l1, key_row - 1, key_row))
    cnt = jnp.sum(beats.astype(I32), axis=0, keepdims=True)
    sel_row = (cand & (cnt < TOP_N - 1)).astype(F32)
    sel_col = jnp.broadcast_to(sel_row, (R, R)).T
    half = ((l0 % SUBLANES == 0) & (l1 < SEL_LEN)) | ((l0 % SUBLANES == 2) & (l1 >= SEL_LEN))
    z = jnp.where(half, sel_col, 0.0)
    same_page = (l1 // SUBLANES == l0 // SUBLANES).astype(BF16)
    picked = jnp.dot(same_page, z.astype(BF16), preferred_element_type=F32)
    picked = picked.reshape(n_pages, SUBLANES, PG) > 0.5

    for p in range(n_pages):
        ks_t = pg_refs[p][2]
        for h in range(N_HEADS):
            g = p * N_HEADS + h
            part_sc[g * SUBLANES:(g + 1) * SUBLANES, :] = _sublane_group_sum(
                ks_t * qr_cols[h * HEAD_DIM:(h + 1) * HEAD_DIM])
    s = _dot01(_fold_matrix(n_pages), part_sc[...])
    s3 = jnp.where(picked, s.reshape(n_pages, SUBLANES, PG) * SCALE, NEG)
    s_new = _per_head_col(_head_lane_sums(qr_row * rep4(rown[:, 2 * HEAD_DIM:3 * HEAD_DIM]))) * SCALE
    m = jnp.maximum(jnp.max(jnp.max(s3, axis=2, keepdims=True), axis=0), s_new)
    p3 = jnp.exp(s3 - m[None])
    pn = jnp.exp(s_new - m)
    inv = 1.0 / (jnp.sum(jnp.sum(p3, axis=2, keepdims=True), axis=0) + pn)
    o_sel = []
    for h in range(N_HEADS):
        acc = jnp.zeros((HEAD_DIM, PG), F32)
        for p in range(n_pages):
            acc = acc + pg_refs[p][3] * p3[p, h:h + 1, :]
        o_sel.append(jnp.sum(acc, axis=1, keepdims=True) * inv[h:h + 1])

    kw_t = win_ref[0]
    vw_t = win_ref[1]
    wlane = lax.broadcasted_iota(I32, (1, WB), 1)
    wpos = past_len - WB + wlane
    wd = past_len - wpos
    wok = (wd >= 0) & (wd < WINDOW) & (wpos >= 0)
    sw_sc[...] = jnp.zeros_like(sw_sc)
    for h in range(N_HEADS):
        qh = qr_cols[h * HEAD_DIM:(h + 1) * HEAD_DIM]
        sw_sc[h:h + 1, :] = jnp.sum(kw_t * jnp.concatenate([qh] * (WB // LANES), axis=1), axis=0, keepdims=True)
    sw = jnp.where(wok, sw_sc[...] * SCALE, NEG)
    sw_new = _per_head_col(_head_lane_sums(qr_row * rep4(winn[:, 0:HEAD_DIM]))) * SCALE
    mw = jnp.maximum(jnp.max(sw, axis=-1, keepdims=True), sw_new)
    ew = jnp.exp(sw - mw)
    en = jnp.exp(sw_new - mw)
    invw = 1.0 / (jnp.sum(ew, axis=-1, keepdims=True) + en)

    gate = lambda h, c: smn[:, SM_NG + 3 * h + c:SM_NG + 3 * h + c + 1]
    o_cols = []
    for h in range(N_HEADS):
        o_win = jnp.sum(vw_t * ew[h:h + 1, :], axis=1, keepdims=True) * invw[h:h + 1]
        o_cols.append(gate(h, 0) * o_cmp[h] + gate(h, 1) * o_sel[h] + gate(h, 2) * o_win)
    w_sel = _per_head_row([gate(h, 1) * pn[h:h + 1] * inv[h:h + 1] for h in range(N_HEADS)], D_NSA)
    w_win = _per_head_row([gate(h, 2) * en[h:h + 1] * invw[h:h + 1] for h in range(N_HEADS)], D_NSA)
    o_ref[row, :] = (_row_of(jnp.concatenate(o_cols, axis=0))
                     + w_sel * rep4(rown[:, 3 * HEAD_DIM:4 * HEAD_DIM])
                     + w_win * rep4(winn[:, HEAD_DIM:2 * HEAD_DIM]))
    last = lax.broadcasted_iota(I32, (HEAD_DIM, WB), 1) == WB - 1
    winn_cols = _cols_of(winn)
    for s in range(2):
        new_col = winn_cols[s * HEAD_DIM:(s + 1) * HEAD_DIM, 0:1]
        shifted = jnp.where(last, new_col, pltpu.roll(win_ref[s], WB - 1, 1))
        if len(nw_ref.shape) == 4:
            for d in range(nw_ref.shape[0]):
                nw_ref[d, s] = shifted
        else:
            nw_ref[s] = shifted


def _nsa_decode(l, pt_flat, n_pages, past_len, nq, nqr, rows, win, small, nsa_cache_t, cmp_pool, win_state_t,
                prev):
    DB = nq.shape[0]
    PG = nsa_cache_t.shape[-1]
    WB = win_state_t.shape[-1]
    page = lambda p: (lambda b, pt: (l, pt[b * n_pages + p], 0, 0, 0))
    cpage = lambda p: (lambda b, pt: (pt[b * n_pages + p], 0, 0))
    carried = [] if prev is None else [prev]
    rows8 = lambda w: pl.BlockSpec((SUBLANES, w), lambda b, pt: (b // SUBLANES, 0))
    return pl.pallas_call(
        functools.partial(_nsa_decode_kernel, n_pages=n_pages, past_len=past_len),
        grid_spec=pltpu.PrefetchScalarGridSpec(
            num_scalar_prefetch=1,
            grid=(DB,),
            in_specs=[rows8(D_NSA), rows8(D_NSA), rows8(4 * HEAD_DIM), rows8(2 * HEAD_DIM), rows8(LANES),
                      pl.BlockSpec((None, None, 2, HEAD_DIM, WB), lambda b, pt: (l, b, 0, 0, 0))]
            + [pl.BlockSpec((None, None, 4, HEAD_DIM, PG), page(p)) for p in range(n_pages)]
            + [pl.BlockSpec((None, PG // CMP_LEN, LANES), cpage(p)) for p in range(n_pages)]
            + [pl.BlockSpec(memory_space=pl.ANY)] * len(carried),
            out_specs=[rows8(D_NSA),
                       pl.BlockSpec((None, None, 2, HEAD_DIM, WB), lambda b, pt: (l, b, 0, 0, 0)) if carried else
                       pl.BlockSpec((win_state_t.shape[0], None, 2, HEAD_DIM, WB), lambda b, pt: (0, b, 0, 0, 0))],
            scratch_shapes=[pltpu.VMEM((n_pages * SUBLANES, LANES), F32),
                            pltpu.VMEM((SUBLANES, LANES), F32),
                            pltpu.VMEM((n_pages * N_HEADS * SUBLANES, PG), F32),
                            pltpu.VMEM((SUBLANES, WB), F32)],
        ),
        out_shape=[jax.ShapeDtypeStruct((DB, D_NSA), F32),
                   jax.ShapeDtypeStruct(win_state_t.shape, F32)],
        input_output_aliases={7 + 2 * n_pages: 1} if carried else {},
        compiler_params=_params(("arbitrary",)),
        name="nsa_decode",
    )(pt_flat, nq, nqr, rows, win, small, win_state_t,
      *([nsa_cache_t] * n_pages), *([cmp_pool] * n_pages), *carried)


def _gla_decode_kernel(qk_ref, g_ref, v_ref, s_ref, *rest):
    o_ref, so_ref = rest[-2:]
    for j in range(SUBLANES):
        qk = qk_ref[j:j + 1, :]
        q_cols = _cols_of(qk[:, 0:D_FOX] * SCALE)
        k_cols = _cols_of(qk[:, D_FOX:2 * D_FOX])
        decay = jnp.exp(_cols_of(g_ref[j:j + 1, :]))
        for h in range(N_HEADS):
            hs = slice(h * HEAD_DIM, (h + 1) * HEAD_DIM)
            v_row = v_ref[j:j + 1, h * GLA_DV:(h + 1) * GLA_DV]
            s_new = decay[hs] * s_ref[j, h] + k_cols[hs] * v_row
            if len(so_ref.shape) == 5:
                for d in range(so_ref.shape[0]):
                    so_ref[d, j, h] = s_new
            else:
                so_ref[j, h] = s_new
            o_ref[j:j + 1, h * GLA_DV:(h + 1) * GLA_DV] = jnp.sum(q_cols[hs] * s_new, axis=0, keepdims=True)


def _gla_decode(l, gqk, gv, glog, state, prev):
    depth, DB = state.shape[0:2]
    nb = SUBLANES
    rows = lambda w: pl.BlockSpec((nb, w), lambda i: (i, 0))
    sspec = pl.BlockSpec((None, nb, N_HEADS, HEAD_DIM, GLA_DV), lambda i: (l, i, 0, 0, 0))
    carried = [] if prev is None else [prev]
    return pl.pallas_call(
        _gla_decode_kernel,
        grid=(DB // nb,),
        in_specs=[rows(2 * D_FOX), rows(D_FOX), rows(D_GLA), sspec]
        + [pl.BlockSpec(memory_space=pl.ANY)] * len(carried),
        out_specs=[rows(D_GLA), sspec if carried else
                   pl.BlockSpec((depth, nb, N_HEADS, HEAD_DIM, GLA_DV), lambda i: (0, i, 0, 0, 0))],
        out_shape=[jax.ShapeDtypeStruct((DB, D_GLA), F32), jax.ShapeDtypeStruct(state.shape, F32)],
        input_output_aliases={4: 1} if carried else {},
        compiler_params=_params(("parallel",)),
        name="gla_decode",
    )(gqk, glog, gv, state, *carried)


def _mix_sample(x, l, p, tab, pt_flat, n_pages, past_len, fox_kv_t, fox_lft_c, nsa_t, cmp_blocks,
                win_state_t, gla_state, carry):
    DB = x.shape[0]
    (fq, fkv, nq, nqr, rows, win, gqk, gv, gog, glog, small) = _in_projection(
        x, 0, DB, DB, p['g'], p['w_r32'], p['sb'], p['wgk32'], p['bgk'], tab, DB)
    o_fox = _fox_decode(l, pt_flat, n_pages, fq, fkv, small, fox_kv_t, fox_lft_c)
    n_pool, PG = nsa_t.shape[1], nsa_t.shape[-1]
    per_layer = n_pool * (PG // CMP_LEN)
    cmp_pool = _compress(cmp_blocks, p['pe_cmp'], p['w_cmp'], Tiles.compress_rows, False,
                         row0=l * per_layer, n_rows=per_layer)
    cmp_pool = cmp_pool.reshape(n_pool, PG // CMP_LEN, LANES)
    o_nsa, new_win = _nsa_decode(l, pt_flat, n_pages, past_len, nq, nqr, rows, win, small,
                                 nsa_t, cmp_pool, win_state_t, carry.get('win'))
    o_gla, g_state = _gla_decode(l, gqk, gv, glog, gla_state, carry.get('gla'))
    carry.update(win=new_win, gla=g_state)
    x_new = _out_projection(x, o_fox, o_nsa, o_gla, gog, p['gn'], p['w_out32'], DB)
    return x_new, dict(fkv=fkv, small=small, rows=rows)


def kernel(x_prompt, x_sample, cache_fox_kv, cache_fox_logf, cache_nsa_kv, state_nsa_win, state_gla,
           page_table, norm_mix_g, w_in, b_fox_f, w_cmp, pe_cmp, w_gla_gk, b_gla_gk, g_gla_norm, w_out,
           norm_ffn_g, dense_w_gate, dense_w_up, dense_w_down, moe_w_router, moe_w_gate, moe_w_up,
           moe_w_down, final_norm_g):
    B, S, _ = x_prompt.shape
    DB, TN, _ = x_sample.shape
    assert TN == 1
    depth, n_pool, PG = cache_fox_kv.shape[0:3]
    n_pages = page_table.shape[1]
    past_len = n_pages * PG
    WB = state_nsa_win.shape[2]
    xp = x_prompt.reshape(B * S, D_MODEL)
    xs = x_sample.reshape(DB, D_MODEL)
    tab_p = _rope_table(jnp.arange(S))
    tab_s = _rope_table(jnp.full((DB,), past_len, I32))
    pt_flat = page_table.reshape(-1).astype(I32)
    fox_kv_t = jnp.transpose(cache_fox_kv, (0, 1, 3, 4, 5, 2))
    fox_lft_c = jnp.swapaxes(cache_fox_logf, 2, 3)
    nsa_t = jnp.transpose(cache_nsa_kv, (0, 1, 3, 4, 2))
    win_state_t = jnp.transpose(state_nsa_win, (0, 1, 3, 4, 2))
    n_cmp = depth * n_pool * (PG // CMP_LEN)
    cmp_blocks = _block_major(cache_nsa_kv[:, :, :, 0:2, :].reshape(n_cmp * CMP_LEN, 2, HEAD_DIM), n_cmp)
    gfin = final_norm_g.reshape(1, D_MODEL)
    cp, cs = [], []
    carry_p, carry_s = {}, {}
    for l in range(depth):
        p = _layer_mix_params(l, norm_mix_g, w_in, b_fox_f, w_cmp, pe_cmp, w_gla_gk, b_gla_gk,
                              g_gla_norm, w_out)
        gf = norm_ffn_g[l].reshape(1, D_MODEL)
        i = l // 2
        moe = l % 2 == 1
        xp, c = _mix_prompt(xp, B, S, l, depth, p, tab_p, Tiles.prompt_rows, Tiles.fox, Tiles.nsa_keys,
                            Tiles.gla_rows, carry_p)
        cp.append(c)
        if moe:
            wr = jnp.zeros((D_MODEL, LANES), F32).at[:, 0:N_EXPERTS].set(moe_w_router[i])
            experts = (moe_w_gate[i], moe_w_up[i], moe_w_down[i])
            disp_p = _moe_dispatch(xp, gf, wr, Tiles.prompt_rows, Tiles.expert_block)
        xs, c = _mix_sample(xs, l, p, tab_s, pt_flat, n_pages, past_len, fox_kv_t, fox_lft_c, nsa_t,
                            cmp_blocks, win_state_t, state_gla, carry_s)
        cs.append(c)
        if moe:
            fin = gfin if l == depth - 1 else None
            y_p = _moe_experts(disp_p, *experts, Tiles.expert_block, Tiles.ffn_cols)
            xs = _moe_small(xs, gf, wr, *experts, Tiles.ffn_cols_sample, fin)
            xp = _moe_merge(disp_p, *y_p, Tiles.prompt_rows, fin)
        else:
            wg, wu, wd = (dense_w_gate[i].astype(BF16), dense_w_up[i].astype(BF16),
                          dense_w_down[i].astype(BF16))
            xp = _dense_ffn(xp, gf, wg, wu, wd, Tiles.dense_rows, Tiles.ffn_cols)
            xs = _dense_ffn(xs, gf, dense_w_gate[i], dense_w_up[i], dense_w_down[i], DB, Tiles.ffn_cols_sample)
    if depth % 2 == 1:
        xp = _final_norm(xp, gfin, Tiles.prompt_rows)
        xs = _final_norm(xs, gfin, DB)
    y_p = xp.reshape(B, S, D_MODEL)
    y_s = xs.reshape(DB, 1, D_MODEL)
    wp = min(WINDOW, S)
    st = lambda key, group: jnp.stack([c[key] for c in group])
    return (y_p, y_s,
            carry_p['fkv'].reshape(depth, B, 2, N_HEADS, HEAD_DIM, S).transpose(0, 1, 5, 2, 3, 4),
            st('small', cp).transpose(0, 1, 3, 2),
            carry_p['rows'].reshape(depth, B, 4, HEAD_DIM, S).transpose(0, 1, 4, 2, 3),
            st('win', cp).reshape(depth, B, 2, HEAD_DIM, wp).transpose(0, 1, 4, 2, 3),
            st('g_state', cp),
            st('fkv', cs).reshape(depth, DB, 1, 2, N_HEADS, HEAD_DIM),
            st('small', cs)[:, :, SM_FF:SM_FF + N_HEADS].reshape(depth, DB, 1, N_HEADS),
            st('rows', cs).reshape(depth, DB, 1, 4, HEAD_DIM),
            jnp.transpose(carry_s['win'], (0, 1, 4, 2, 3)),
            carry_s['gla'])
```
